```python
import math
import jax
import jax.numpy as jnp
from jax import lax
import numpy as np

D_MODEL = 2048
BATCH = 4
SEQ = 2048
DEPTH = 4
DEC_BATCH = 32
DEC_SEQ = 4
PAST_LEN = 16384
PAGE_SIZE = 128

N_MIXERS = 4
LPM = DEPTH // N_MIXERS
PLE_DIM = 256
ALPHA = (2 * DEPTH) ** 0.25
BETA = (8 * DEPTH) ** -0.25
LN_EPS = 1e-5
N_BUCKETS = 32
REL_MAX_DIST = 2048
N_HEADS = 32
HEAD_DIM = 64
ATT_WIDTH = N_HEADS * HEAD_DIM
WINDOW = 128
KV_A = 4
A_KV = KV_A * HEAD_DIM
A_IN = 2 * ATT_WIDTH + 2 * A_KV
KV_D = 8
D_KV = KV_D * HEAD_DIM
IDX_HEADS = 16
IDX_DIM = 128
TOPK_MAX = 256
Q_BLOCK = 128
D_IN = 2 * ATT_WIDTH + 2 * D_KV + IDX_HEADS * IDX_DIM + IDX_DIM + IDX_HEADS
S5_WIDTH = D_MODEL
S5_GROUP = 16
S5_GROUPS = S5_WIDTH // S5_GROUP
S5_STATE = 64
GDN_QK_HEADS = 16
GDN_V_HEADS = 32
GDN_DK = 128
GDN_DV = 128
GDN_CONV = 4
GDN_CHUNK = 64
GDN_QK_WIDTH = GDN_QK_HEADS * GDN_DK
GDN_V_WIDTH = GDN_V_HEADS * GDN_DV
GDN_CONV_CH = 2 * GDN_QK_WIDTH + GDN_V_WIDTH
GDN_IN = GDN_CONV_CH + GDN_V_WIDTH + 2 * GDN_V_HEADS

kernel_name = 'hybrid_swa_s5_gdn_dsa_step'

F32 = jnp.float32


def layer_norm(x, g, b):
    xf = x.astype(F32)
    mu = jnp.mean(xf, axis=-1, keepdims=True)
    var = jnp.mean(jnp.square(xf - mu), axis=-1, keepdims=True)
    return ((xf - mu) * lax.rsqrt(var + LN_EPS) * g.astype(F32) + b.astype(F32)).astype(x.dtype)


def post_norm_ple(x, h, g, b, p, w_gate, w_ple):
    x = layer_norm(ALPHA * x + h, g, b)
    return x + jax.nn.sigmoid(x @ w_gate) * (p @ w_ple)


def rel_bucket(dist):
    n = jnp.maximum(dist, 0)
    exact = N_BUCKETS // 2
    logb = exact + (jnp.log(jnp.maximum(n, exact).astype(F32) / exact)
                    / math.log(REL_MAX_DIST / exact) * (N_BUCKETS - exact)).astype(jnp.int32)
    return jnp.where(n < exact, n, jnp.minimum(logb, N_BUCKETS - 1))


def head_bias(rel_bias, dist, n_kv, g):
    b = jnp.moveaxis(rel_bias[rel_bucket(dist)].astype(F32), -1, -3)
    return b.reshape(b.shape[:-3] + (n_kv, g) + b.shape[-2:])


def masked_softmax(logits, mask, sink=None):
    logits = jnp.where(mask, logits, -jnp.inf)
    m = jnp.max(logits, axis=-1, keepdims=True)
    if sink is not None:
        m = jnp.maximum(m, sink)
    e = jnp.exp(logits - m)
    den = jnp.sum(e, axis=-1, keepdims=True)
    if sink is not None:
        den = den + jnp.exp(sink - m)
    return e / den


def take_rows(rows, idx):
    return jax.vmap(lambda r, i: r[i])(rows, idx)


def window_attend(q, k, v, qpos, kpos, sinks, rel_bias):
    n_kv, g = q.shape[-3], q.shape[-2]
    dist = qpos[..., :, None] - kpos[..., None, :]
    mask = (dist >= 0) & (dist < WINDOW) & (kpos[..., None, :] >= 0)
    logits = jnp.einsum('...qhgd,...khd->...hgqk', q, k).astype(F32) * (HEAD_DIM ** -0.5)
    logits = logits + head_bias(rel_bias, dist, n_kv, g)
    sink = sinks.astype(F32).reshape(n_kv, g, 1, 1)
    p = masked_softmax(logits, mask[..., None, None, :, :], sink)
    return jnp.einsum('...hgqk,...khd->...qhgd', p.astype(v.dtype), v)


def swa_mixer(x, kv_cache, start, w_in, sinks, w_out, rel_bias):
    Bn, L, _ = x.shape
    G = N_HEADS // KV_A
    q, k, v, z = jnp.split(x @ w_in, [ATT_WIDTH, ATT_WIDTH + A_KV, ATT_WIDTH + 2 * A_KV], axis=-1)
    q = q.reshape(Bn, L, KV_A, G, HEAD_DIM)
    k = k.reshape(Bn, L, KV_A, HEAD_DIM)
    v = v.reshape(Bn, L, KV_A, HEAD_DIM)
    if kv_cache is None:
        nb = L // WINDOW
        qb = q.reshape(Bn, nb, WINDOW, KV_A, G, HEAD_DIM)
        kb = k.reshape(Bn, nb, WINDOW, KV_A, HEAD_DIM)
        vb = v.reshape(Bn, nb, WINDOW, KV_A, HEAD_DIM)
        prev = lambda t: jnp.concatenate([jnp.zeros_like(t[:, :1]), t[:, :-1]], axis=1)
        kk = jnp.concatenate([prev(kb), kb], axis=2)
        vv = jnp.concatenate([prev(vb), vb], axis=2)
        qpos = jnp.arange(L).reshape(nb, WINDOW)
        kpos = jnp.concatenate([qpos - WINDOW, qpos], axis=1)
        o = window_attend(qb, kk, vv, qpos, kpos, sinks, rel_bias)
        new_kv = jnp.stack([k[:, L - WINDOW:], v[:, L - WINDOW:]], axis=2)
    else:
        kk = jnp.concatenate([kv_cache[:, :, 0].astype(k.dtype), k], axis=1)
        vv = jnp.concatenate([kv_cache[:, :, 1].astype(v.dtype), v], axis=1)
        qpos = start + jnp.arange(L)
        kpos = start - WINDOW + jnp.arange(WINDOW + L)
        o = window_attend(q, kk, vv, qpos, kpos, sinks, rel_bias)
        new_kv = jnp.stack([kk[:, -WINDOW:], vv[:, -WINDOW:]], axis=2)
    o = o.reshape(Bn, L, ATT_WIDTH)
    return (o * jax.nn.silu(z)) @ w_out, new_kv


def _linear_combine(l, r):
    return (l[0] * r[0], r[0] * l[1] + r[1])


def s5_mixer(x, h0, w_in, a_re, a_im, b_re, b_im, c_re, c_im, d_skip, log_dt, w_glu, w_out):
    Bn, L, _ = x.shape
    u, z = jnp.split(x @ w_in, 2, axis=-1)
    uf = u.astype(F32).reshape(Bn, L, S5_GROUPS, S5_GROUP)
    a = lax.complex(a_re.astype(F32), a_im.astype(F32))
    dt = jnp.exp(log_dt.astype(F32))[:, None]
    a_bar = jnp.exp(a * dt)
    b_bar = ((a_bar - 1.0) / a)[..., None] * lax.complex(b_re.astype(F32), b_im.astype(F32))
    c = lax.complex(c_re.astype(F32), c_im.astype(F32))
    bu = jnp.einsum('gpc,blgc->blgp', b_bar, uf.astype(jnp.complex64))
    if h0 is not None:
        h0c = lax.complex(h0[..., 0].astype(F32), h0[..., 1].astype(F32))
        bu = bu.at[:, 0].add(a_bar * h0c)
    a_seq = jnp.broadcast_to(a_bar, bu.shape)
    _, h = lax.associative_scan(_linear_combine, (a_seq, bu), axis=1)
    y = jnp.einsum('gcp,blgp->blgc', c, h).real + d_skip.astype(F32).reshape(S5_GROUPS, S5_GROUP) * uf
    y = jax.nn.gelu(y.reshape(Bn, L, S5_WIDTH))
    y = y * jax.nn.sigmoid(y @ w_glu.astype(F32))
    out = (y.astype(x.dtype) * jax.nn.silu(z)) @ w_out
    h_last = h[:, -1]
    return out, jnp.stack([h_last.real, h_last.imag], axis=-1)


def l2_normalize(t, eps=1e-6):
    tf = t.astype(F32)
    return tf * lax.rsqrt(jnp.sum(tf * tf, axis=-1, keepdims=True) + eps)


def chunk_gated_delta(q, k, v, g, beta, S0):
    Bn, L, H, dk = k.shape
    dv = v.shape[-1]
    C = min(GDN_CHUNK, L)
    n = -(-L // C)
    pad = n * C - L

    def chunks(t):
        t = jnp.pad(t, [(0, 0), (0, pad)] + [(0, 0)] * (t.ndim - 2))
        t = t.reshape((Bn, n, C) + t.shape[2:])
        return jnp.moveaxis(t, 3, 2)

    qc, kc, vc, gc, bc = [chunks(t) for t in (q, k, v, g, beta)]
    gam = jnp.cumsum(gc, axis=-1)
    pos = jnp.arange(C)
    causal = pos[:, None] >= pos[None, :]
    strict = pos[:, None] > pos[None, :]
    decay = jnp.exp(jnp.where(causal, gam[..., :, None] - gam[..., None, :], -jnp.inf))
    kk = jnp.einsum('bnhid,bnhjd->bnhij', kc, kc)
    tri = jnp.eye(C, dtype=F32) + jnp.where(strict, bc[..., :, None] * kk * decay, 0.0)
    rhs = jnp.concatenate([bc[..., None] * vc, (bc * jnp.exp(gam))[..., None] * kc], axis=-1)
    sol = lax.linalg.triangular_solve(tri, rhs, left_side=True, lower=True, unit_diagonal=True)
    u, w = sol[..., :dv], sol[..., dv:]
    qk = jnp.einsum('bnhid,bnhjd->bnhij', qc, kc) * decay
    q_dec = qc * jnp.exp(gam)[..., None]
    k_dec = kc * jnp.exp(gam[..., -1:] - gam)[..., None]
    g_tot = jnp.exp(gam[..., -1])

    def step(S, xs):
        u_c, w_c, qk_c, qd_c, kd_c, gt_c = xs
        v_new = u_c - jnp.einsum('bhcd,bhde->bhce', w_c, S)
        o = jnp.einsum('bhcd,bhde->bhce', qd_c, S) + jnp.einsum('bhij,bhje->bhie', qk_c, v_new)
        S = S * gt_c[..., None, None] + jnp.einsum('bhcd,bhce->bhde', kd_c, v_new)
        return S, o

    xs = tuple(jnp.moveaxis(t, 1, 0) for t in (u, w, qk, q_dec, k_dec, g_tot))
    S, o = lax.scan(step, S0, xs)
    o = o.transpose(1, 0, 3, 2, 4).reshape(Bn, n * C, H, dv)[:, :L]
    return o, S


def gdn_mixer(x, S0, conv_buf, w_in, conv_w, a_log, dt_bias, norm_w, w_out):
    Bn, L, _ = x.shape
    qkv, z, a, b = jnp.split(x @ w_in, [GDN_CONV_CH, GDN_CONV_CH + GDN_V_WIDTH,
                                        GDN_CONV_CH + GDN_V_WIDTH + GDN_V_HEADS], axis=-1)
    if conv_buf is None:
        conv_buf = jnp.zeros((Bn, GDN_CONV - 1, GDN_CONV_CH), qkv.dtype)
    xx = jnp.concatenate([conv_buf.astype(qkv.dtype), qkv], axis=1)
    conv = jax.nn.silu(sum(xx[:, j:j + L] * conv_w[j] for j in range(GDN_CONV)))
    new_buf = xx[:, L:]
    q, k, v = jnp.split(conv, [GDN_QK_WIDTH, 2 * GDN_QK_WIDTH], axis=-1)
    rep = GDN_V_HEADS // GDN_QK_HEADS
    q = jnp.repeat(l2_normalize(q.reshape(Bn, L, GDN_QK_HEADS, GDN_DK)), rep, axis=2) * (GDN_DK ** -0.5)
    k = jnp.repeat(l2_normalize(k.reshape(Bn, L, GDN_QK_HEADS, GDN_DK)), rep, axis=2)
    v = v.reshape(Bn, L, GDN_V_HEADS, GDN_DV).astype(F32)
    beta = jax.nn.sigmoid(b.astype(F32))
    g = -jnp.exp(a_log.astype(F32)) * jax.nn.softplus(a.astype(F32) + dt_bias.astype(F32))
    if S0 is None:
        S0 = jnp.zeros((Bn, GDN_V_HEADS, GDN_DK, GDN_DV), F32)
    o, S = chunk_gated_delta(q, k, v, g, beta, S0.astype(F32))
    of = o * lax.rsqrt(jnp.mean(o * o, axis=-1, keepdims=True) + 1e-6) * norm_w.astype(F32)
    of = of * jax.nn.silu(z.astype(F32).reshape(Bn, L, GDN_V_HEADS, GDN_DV))
    return of.reshape(Bn, L, GDN_V_WIDTH).astype(x.dtype) @ w_out, S, new_buf


def dsa_project(x, w_in):
    Bn, L, _ = x.shape
    c0 = ATT_WIDTH
    c1 = c0 + D_KV
    c2 = c1 + D_KV
    c3 = c2 + ATT_WIDTH
    c4 = c3 + IDX_HEADS * IDX_DIM
    c5 = c4 + IDX_DIM
    q, k, v, z, qi, ki, wi = jnp.split(x @ w_in, [c0, c1, c2, c3, c4, c5], axis=-1)
    q = q.reshape(Bn, L, KV_D, N_HEADS // KV_D, HEAD_DIM)
    kv = jnp.stack([k.reshape(Bn, L, KV_D, HEAD_DIM), v.reshape(Bn, L, KV_D, HEAD_DIM)], axis=2)
    qi = qi.reshape(Bn, L, IDX_HEADS, IDX_DIM)
    return q, kv, z, qi, ki, wi


def index_scores(qi, ki, wi):
    s = jnp.einsum('bthd,bsd->bths', qi, ki).astype(F32) * (IDX_DIM ** -0.5)
    return jnp.einsum('bths,bth->bts', jax.nn.relu(s), wi.astype(F32) * (IDX_HEADS ** -0.5))


def gathered_attend(q, kvs, qpos, kpos, valid, rel_bias):
    n_kv, g = q.shape[-3], q.shape[-2]
    ks, vs = kvs[..., 0, :, :], kvs[..., 1, :, :]
    logits = jnp.einsum('bthgd,btkhd->bhgtk', q, ks).astype(F32) * (HEAD_DIM ** -0.5)
    dist = qpos[None, :, None] - kpos
    logits = logits + head_bias(rel_bias, dist, n_kv, g)
    p = masked_softmax(logits, (valid & (dist >= 0))[:, None, None])
    return jnp.einsum('bhgtk,btkhd->bthgd', p.astype(vs.dtype), vs)


def dsa_prompt(x, w_in, w_out, rel_bias):
    Bn, L, _ = x.shape
    q, kv, z, qi, ki, wi = dsa_project(x, w_in)
    topk = min(TOPK_MAX, L // 4)
    key_pos = jnp.arange(L)

    def block(t0):
        qb = lax.dynamic_slice_in_dim(q, t0, Q_BLOCK, axis=1)
        qib = lax.dynamic_slice_in_dim(qi, t0, Q_BLOCK, axis=1)
        wib = lax.dynamic_slice_in_dim(wi, t0, Q_BLOCK, axis=1)
        qpos = t0 + jnp.arange(Q_BLOCK)
        sc = index_scores(qib, ki, wib)
        sc = jnp.where(key_pos[None, None, :] <= qpos[None, :, None], sc, -jnp.inf)
        vals, idx = lax.top_k(sc, topk)
        return gathered_attend(qb, take_rows(kv, idx), qpos, idx, vals > -jnp.inf, rel_bias)

    o = lax.map(block, jnp.arange(0, L, Q_BLOCK))
    o = jnp.moveaxis(o, 0, 1).reshape(Bn, L, ATT_WIDTH)
    return (o * jax.nn.silu(z)) @ w_out, kv, ki


def dsa_sample(x, kv_pool, kidx_pool, layer, page_table, w_in, w_out, rel_bias):
    Bd, L, _ = x.shape
    n_pages = page_table.shape[1]
    past = n_pages * PAGE_SIZE
    q, kv, z, qi, ki, wi = dsa_project(x, w_in)
    ki_past = kidx_pool[layer, page_table].reshape(Bd, past, IDX_DIM).astype(ki.dtype)
    ki_all = jnp.concatenate([ki_past, ki], axis=1)
    total = past + L
    topk = min(TOPK_MAX, total // 4)
    qpos = past + jnp.arange(L)
    sc = index_scores(qi, ki_all, wi)
    sc = jnp.where(jnp.arange(total)[None, None, :] <= qpos[None, :, None], sc, -jnp.inf)
    vals, idx = lax.top_k(sc, topk)
    pidx = jnp.minimum(idx, past - 1)
    phys = jnp.take_along_axis(page_table, (pidx // PAGE_SIZE).reshape(Bd, -1), axis=1).reshape(idx.shape)
    kv_past = kv_pool[layer, phys, pidx % PAGE_SIZE].astype(kv.dtype)
    kv_new = take_rows(kv, jnp.clip(idx - past, 0, L - 1))
    kv_sel = jnp.where((idx >= past)[..., None, None, None], kv_new, kv_past)
    o = gathered_attend(q, kv_sel, qpos, idx, vals > -jnp.inf, rel_bias).reshape(Bd, L, ATT_WIDTH)
    return (o * jax.nn.silu(z)) @ w_out, kv, ki


def setup_inputs(seed: int = 0) -> dict:
    key = jax.random.key(seed)
    ks = iter(jax.random.split(key, 64))
    nrm = lambda shape, scale=1.0: jax.random.normal(next(ks), shape, F32) * scale
    n_pages = PAST_LEN // PAGE_SIZE
    n_pool = (5 * DEC_BATCH * n_pages) // 4
    page_table = jax.random.permutation(next(ks), n_pool)[:DEC_BATCH * n_pages].reshape(DEC_BATCH, n_pages).astype(jnp.int32)
    s5_log_dt = jax.random.uniform(next(ks), (LPM, S5_GROUPS), F32, math.log(1e-3), math.log(1e-1))
    gdn_a_log = jnp.log(jax.random.uniform(next(ks), (LPM, GDN_V_HEADS), F32, 1.0, 16.0))
    dt = jnp.exp(jax.random.uniform(next(ks), (LPM, GDN_V_HEADS), F32, math.log(1e-3), math.log(1e-1)))
    gdn_dt_bias = dt + jnp.log(-jnp.expm1(-dt))
    a_im = jnp.pi * jnp.arange(S5_STATE, dtype=F32)[None, None, :] + nrm((LPM, S5_GROUPS, S5_STATE), 0.01)
    return {
        'x_prompt': nrm((BATCH, SEQ, D_MODEL)),
        'x_sample': nrm((DEC_BATCH, DEC_SEQ, D_MODEL)),
        'cache_a_kv': nrm((LPM, DEC_BATCH, WINDOW, 2, KV_A, HEAD_DIM)),
        'state_s5': nrm((LPM, DEC_BATCH, S5_GROUPS, S5_STATE, 2), 0.5),
        'state_gdn': nrm((LPM, DEC_BATCH, GDN_V_HEADS, GDN_DK, GDN_DV), 0.05),
        'state_gdn_conv': nrm((LPM, DEC_BATCH, GDN_CONV - 1, GDN_CONV_CH)),
        'cache_d_kv': nrm((LPM, n_pool, PAGE_SIZE, 2, KV_D, HEAD_DIM)),
        'cache_d_kidx': nrm((LPM, n_pool, PAGE_SIZE, IDX_DIM)),
        'page_table': page_table,
        'p_prompt': nrm((DEPTH, BATCH, SEQ, PLE_DIM)),
        'p_sample': nrm((DEPTH, DEC_BATCH, DEC_SEQ, PLE_DIM)),
        'rel_bias': nrm((N_BUCKETS, N_HEADS), 0.2),
        'ln_g': 1.0 + nrm((DEPTH, D_MODEL), 0.02),
        'ln_b': nrm((DEPTH, D_MODEL), 0.02),
        'ple_gate_w': nrm((DEPTH, D_MODEL, D_MODEL), D_MODEL ** -0.5),
        'ple_w': nrm((DEPTH, PLE_DIM, D_MODEL), 0.5 * PLE_DIM ** -0.5),
        'a_w_in': nrm((LPM, D_MODEL, A_IN), D_MODEL ** -0.5),
        'a_sinks': nrm((LPM, N_HEADS), 0.5),
        'a_w_out': nrm((LPM, ATT_WIDTH, D_MODEL), BETA * ATT_WIDTH ** -0.5),
        's5_w_in': nrm((LPM, D_MODEL, 2 * S5_WIDTH), D_MODEL ** -0.5),
        's5_a_re': -0.5 + nrm((LPM, S5_GROUPS, S5_STATE), 0.01),
        's5_a_im': a_im,
        's5_b_re': nrm((LPM, S5_GROUPS, S5_STATE, S5_GROUP), (2 * S5_GROUP) ** -0.5),
        's5_b_im': nrm((LPM, S5_GROUPS, S5_STATE, S5_GROUP), (2 * S5_GROUP) ** -0.5),
        's5_c_re': nrm((LPM, S5_GROUPS, S5_GROUP, S5_STATE), (2 * S5_STATE) ** -0.5),
        's5_c_im': nrm((LPM, S5_GROUPS, S5_GROUP, S5_STATE), (2 * S5_STATE) ** -0.5),
        's5_d': nrm((LPM, S5_WIDTH)),
        's5_log_dt': s5_log_dt,
        's5_w_glu': nrm((LPM, S5_WIDTH, S5_WIDTH), S5_WIDTH ** -0.5),
        's5_w_out': nrm((LPM, S5_WIDTH, D_MODEL), BETA * S5_WIDTH ** -0.5),
        'gdn_w_in': nrm((LPM, D_MODEL, GDN_IN), D_MODEL ** -0.5),
        'gdn_conv_w': nrm((LPM, GDN_CONV, GDN_CONV_CH), GDN_CONV ** -0.5),
        'gdn_a_log': gdn_a_log,
        'gdn_dt_bias': gdn_dt_bias,
        'gdn_norm_w': 1.0 + nrm((LPM, GDN_DV), 0.02),
        'gdn_w_out': nrm((LPM, GDN_V_WIDTH, D_MODEL), BETA * GDN_V_WIDTH ** -0.5),
        'dsa_w_in': nrm((LPM, D_MODEL, D_IN), D_MODEL ** -0.5),
        'dsa_w_out': nrm((LPM, ATT_WIDTH, D_MODEL), BETA * ATT_WIDTH ** -0.5),
    }


def reference(x_prompt, x_sample, cache_a_kv, state_s5, state_gdn, state_gdn_conv, cache_d_kv, cache_d_kidx,
              page_table, p_prompt, p_sample, rel_bias, ln_g, ln_b, ple_gate_w, ple_w,
              a_w_in, a_sinks, a_w_out,
              s5_w_in, s5_a_re, s5_a_im, s5_b_re, s5_b_im, s5_c_re, s5_c_im, s5_d, s5_log_dt, s5_w_glu, s5_w_out,
              gdn_w_in, gdn_conv_w, gdn_a_log, gdn_dt_bias, gdn_norm_w, gdn_w_out,
              dsa_w_in, dsa_w_out):
    past_len = page_table.shape[1] * PAGE_SIZE
    xp, xs = x_prompt, x_sample
    a_p, a_s, s5_p, s5_s, gd_p, gd_s, gc_p, gc_s, dkv_p, dkv_s, dki_p, dki_s = [[] for _ in range(12)]
    for i in range(DEPTH):
        kind, r = i % N_MIXERS, i // N_MIXERS
        if kind == 0:
            hp, st_p = swa_mixer(xp, None, 0, a_w_in[r], a_sinks[r], a_w_out[r], rel_bias)
            hs, st_s = swa_mixer(xs, cache_a_kv[r], past_len, a_w_in[r], a_sinks[r], a_w_out[r], rel_bias)
            a_p.append(st_p)
            a_s.append(st_s)
        elif kind == 1:
            s5w = (s5_w_in[r], s5_a_re[r], s5_a_im[r], s5_b_re[r], s5_b_im[r], s5_c_re[r], s5_c_im[r],
                   s5_d[r], s5_log_dt[r], s5_w_glu[r], s5_w_out[r])
            hp, st_p = s5_mixer(xp, None, *s5w)
            hs, st_s = s5_mixer(xs, state_s5[r], *s5w)
            s5_p.append(st_p)
            s5_s.append(st_s)
        elif kind == 2:
            gw = (gdn_w_in[r], gdn_conv_w[r], gdn_a_log[r], gdn_dt_bias[r], gdn_norm_w[r], gdn_w_out[r])
            hp, sp, cp = gdn_mixer(xp, None, None, *gw)
            hs, ss, cs = gdn_mixer(xs, state_gdn[r], state_gdn_conv[r], *gw)
            gd_p.append(sp)
            gd_s.append(ss)
            gc_p.append(cp)
            gc_s.append(cs)
        else:
            hp, kvp, kip = dsa_prompt(xp, dsa_w_in[r], dsa_w_out[r], rel_bias)
            hs, kvs, kis = dsa_sample(xs, cache_d_kv, cache_d_kidx, r, page_table, dsa_w_in[r], dsa_w_out[r], rel_bias)
            dkv_p.append(kvp)
            dkv_s.append(kvs)
            dki_p.append(kip)
            dki_s.append(kis)
        xp = post_norm_ple(xp, hp, ln_g[i], ln_b[i], p_prompt[i], ple_gate_w[i], ple_w[i])
        xs = post_norm_ple(xs, hs, ln_g[i], ln_b[i], p_sample[i], ple_gate_w[i], ple_w[i])
    return (xp, xs, jnp.stack(a_p), jnp.stack(a_s), jnp.stack(s5_p), jnp.stack(s5_s),
            jnp.stack(gd_p), jnp.stack(gd_s), jnp.stack(gc_p), jnp.stack(gc_s),
            jnp.stack(dkv_p), jnp.stack(dkv_s), jnp.stack(dki_p), jnp.stack(dki_s))
```

```python
import functools
import math

import jax
import jax.numpy as jnp
from jax import lax
from jax.experimental import pallas as pl
from jax.experimental.pallas import tpu as pltpu

D_MODEL = 2048
BATCH = 4
SEQ = 2048
DEPTH = 4
DEC_BATCH = 32
DEC_SEQ = 4
PAGE_SIZE = 128
N_MIXERS = 4
PLE_DIM = 256
ALPHA = (2 * DEPTH) ** 0.25
LN_EPS = 1e-5
N_BUCKETS = 32
REL_MAX_DIST = 2048
N_HEADS = 32
HEAD_DIM = 64
ATT_WIDTH = N_HEADS * HEAD_DIM
WINDOW = 128
KV_A = 4
A_KV = KV_A * HEAD_DIM
KV_D = 8
D_KV = KV_D * HEAD_DIM
IDX_HEADS = 16
IDX_DIM = 128
TOPK_MAX = 256
Q_BLOCK = 128
S5_WIDTH = D_MODEL
S5_GROUP = 16
S5_GROUPS = S5_WIDTH // S5_GROUP
S5_STATE = 64
GDN_QK_HEADS = 16
GDN_V_HEADS = 32
GDN_DK = 128
GDN_DV = 128
GDN_CONV = 4
GDN_CHUNK = 64
GDN_QK_WIDTH = GDN_QK_HEADS * GDN_DK
GDN_V_WIDTH = GDN_V_HEADS * GDN_DV
GDN_CONV_CH = 2 * GDN_QK_WIDTH + GDN_V_WIDTH

F32 = jnp.float32
BF16 = jnp.bfloat16

N_PROMPT_TOK = BATCH * SEQ
N_SAMPLE_TOK = DEC_BATCH * DEC_SEQ
N_TOK = N_PROMPT_TOK + N_SAMPLE_TOK

V7X_VMEM_BYTES = 64 * 1024 * 1024
VMEM_LIMIT = 48 * 1024 * 1024
LANE = 128


def _mm_kernel(x_ref, w_ref, o_ref):
    o_ref[...] = jnp.dot(x_ref[...], w_ref[...], preferred_element_type=F32).astype(o_ref.dtype)


def _pick_tile(n, prefs):
    for t in prefs:
        if n % t == 0:
            return t
    raise ValueError(f"no tile for {n}")


def matmul(x, w, out_dtype=F32):
    m, k = x.shape
    n = w.shape[1]
    tm = _pick_tile(m, (640, 512, 320, 256, 128, 64, 32, 16, 8))
    tn = _pick_tile(n, (512, 384, 256, 128))
    return pl.pallas_call(
        _mm_kernel,
        grid=(m // tm, n // tn),
        in_specs=[pl.BlockSpec((tm, k), lambda i, j: (i, 0)),
                  pl.BlockSpec((k, tn), lambda i, j: (0, j))],
        out_specs=pl.BlockSpec((tm, tn), lambda i, j: (i, j)),
        out_shape=jax.ShapeDtypeStruct((m, n), out_dtype),
        compiler_params=pltpu.CompilerParams(
            dimension_semantics=("parallel", "parallel"), vmem_limit_bytes=VMEM_LIMIT),
        name="proj_matmul",
    )(x, w)


POST_TM = 320
POST_TN = 512


def _post_kernel(x_ref, h_ref, p_ref, g_ref, b_ref, wg_ref, wp_ref, o_ref, obf_ref, y_sc, ybf_sc):
    j = pl.program_id(1)

    @pl.when(j == 0)
    def _():
        t = ALPHA * x_ref[...] + h_ref[...]
        mu = jnp.mean(t, axis=-1, keepdims=True)
        d = t - mu
        var = jnp.mean(d * d, axis=-1, keepdims=True)
        y = d * lax.rsqrt(var + LN_EPS) * g_ref[...] + b_ref[...]
        ybf_sc[...] = y.astype(BF16)
        for jj in range(D_MODEL // POST_TN):
            y_sc[jj] = y[:, jj * POST_TN:(jj + 1) * POST_TN]

    gate = jnp.dot(ybf_sc[...], wg_ref[...], preferred_element_type=F32)
    ple = jnp.dot(p_ref[...], wp_ref[...], preferred_element_type=F32)
    o = y_sc[j] + (1.0 / (1.0 + jnp.exp(-gate))) * ple
    o_ref[...] = o
    obf_ref[...] = o.astype(BF16)


def post_norm_ple(x, h, p_bf, g, b, wg_bf, wp_bf):
    m = x.shape[0]
    tm, tn = POST_TM, POST_TN
    return pl.pallas_call(
        _post_kernel,
        grid=(m // tm, D_MODEL // tn),
        in_specs=[pl.BlockSpec((tm, D_MODEL), lambda i, j: (i, 0)),
                  pl.BlockSpec((tm, D_MODEL), lambda i, j: (i, 0)),
                  pl.BlockSpec((tm, PLE_DIM), lambda i, j: (i, 0)),
                  pl.BlockSpec((1, D_MODEL), lambda i, j: (0, 0)),
                  pl.BlockSpec((1, D_MODEL), lambda i, j: (0, 0)),
                  pl.BlockSpec((D_MODEL, tn), lambda i, j: (0, j)),
                  pl.BlockSpec((PLE_DIM, tn), lambda i, j: (0, j))],
        out_specs=[pl.BlockSpec((tm, tn), lambda i, j: (i, j)),
                   pl.BlockSpec((tm, tn), lambda i, j: (i, j))],
        out_shape=[jax.ShapeDtypeStruct((m, D_MODEL), F32),
                   jax.ShapeDtypeStruct((m, D_MODEL), BF16)],
        scratch_shapes=[pltpu.VMEM((D_MODEL // tn, tm, tn), F32),
                        pltpu.VMEM((tm, D_MODEL), BF16)],
        compiler_params=pltpu.CompilerParams(
            dimension_semantics=("parallel", "arbitrary"), vmem_limit_bytes=VMEM_LIMIT),
        name="post_norm_ple",
    )(x, h, p_bf, g.reshape(1, D_MODEL), b.reshape(1, D_MODEL), wg_bf, wp_bf)


def rel_bucket(dist):
    n = jnp.maximum(dist, 0)
    exact = N_BUCKETS // 2
    logb = exact + (jnp.log(jnp.maximum(n, exact).astype(F32) / exact)
                    / math.log(REL_MAX_DIST / exact) * (N_BUCKETS - exact)).astype(jnp.int32)
    return jnp.where(n < exact, n, jnp.minimum(logb, N_BUCKETS - 1))


def head_bias(rel_bias, dist, n_kv, g):
    b = jnp.moveaxis(rel_bias[rel_bucket(dist)].astype(F32), -1, -3)
    return b.reshape(b.shape[:-3] + (n_kv, g) + b.shape[-2:])


def masked_softmax(logits, mask, sink=None):
    logits = jnp.where(mask, logits, -jnp.inf)
    m = jnp.max(logits, axis=-1, keepdims=True)
    if sink is not None:
        m = jnp.maximum(m, sink)
    e = jnp.exp(logits - m)
    den = jnp.sum(e, axis=-1, keepdims=True)
    if sink is not None:
        den = den + jnp.exp(sink - m)
    return e / den


def take_rows(rows, idx):
    return jax.vmap(lambda r, i: r[i])(rows, idx)


def split_tokens(t):
    c = t.shape[-1]
    return (t[:N_PROMPT_TOK].reshape(BATCH, SEQ, c), t[N_PROMPT_TOK:].reshape(DEC_BATCH, DEC_SEQ, c))


def join_tokens(tp, ts):
    c = tp.shape[-1]
    return jnp.concatenate([tp.reshape(N_PROMPT_TOK, c), ts.reshape(N_SAMPLE_TOK, c)], axis=0)


def window_attend(q, k, v, qpos, kpos, sinks, rel_bias):
    n_kv, g = q.shape[-3], q.shape[-2]
    dist = qpos[..., :, None] - kpos[..., None, :]
    mask = (dist >= 0) & (dist < WINDOW) & (kpos[..., None, :] >= 0)
    logits = jnp.einsum('...qhgd,...khd->...hgqk', q, k).astype(F32) * (HEAD_DIM ** -0.5)
    logits = logits + head_bias(rel_bias, dist, n_kv, g)
    sink = sinks.astype(F32).reshape(n_kv, g, 1, 1)
    p = masked_softmax(logits, mask[..., None, None, :, :], sink)
    return jnp.einsum('...hgqk,...khd->...qhgd', p.astype(v.dtype), v)


def swa_core(proj, kv_cache, start, sinks, rel_bias):
    Bn, L, _ = proj.shape
    G = N_HEADS // KV_A
    q, k, v, z = jnp.split(proj, [ATT_WIDTH, ATT_WIDTH + A_KV, ATT_WIDTH + 2 * A_KV], axis=-1)
    q = q.reshape(Bn, L, KV_A, G, HEAD_DIM)
    k = k.reshape(Bn, L, KV_A, HEAD_DIM)
    v = v.reshape(Bn, L, KV_A, HEAD_DIM)
    if kv_cache is None:
        nb = L // WINDOW
        qb = q.reshape(Bn, nb, WINDOW, KV_A, G, HEAD_DIM)
        kb = k.reshape(Bn, nb, WINDOW, KV_A, HEAD_DIM)
        vb = v.reshape(Bn, nb, WINDOW, KV_A, HEAD_DIM)
        prev = lambda t: jnp.concatenate([jnp.zeros_like(t[:, :1]), t[:, :-1]], axis=1)
        kk = jnp.concatenate([prev(kb), kb], axis=2)
        vv = jnp.concatenate([prev(vb), vb], axis=2)
        qpos = jnp.arange(L).reshape(nb, WINDOW)
        kpos = jnp.concatenate([qpos - WINDOW, qpos], axis=1)
        o = window_attend(qb, kk, vv, qpos, kpos, sinks, rel_bias)
        new_kv = jnp.stack([k[:, L - WINDOW:], v[:, L - WINDOW:]], axis=2)
    else:
        kk = jnp.concatenate([kv_cache[:, :, 0], k], axis=1)
        vv = jnp.concatenate([kv_cache[:, :, 1], v], axis=1)
        qpos = start + jnp.arange(L)
        kpos = start - WINDOW + jnp.arange(WINDOW + L)
        o = window_attend(q, kk, vv, qpos, kpos, sinks, rel_bias)
        new_kv = jnp.stack([kk[:, -WINDOW:], vv[:, -WINDOW:]], axis=2)
    o = o.reshape(Bn, L, ATT_WIDTH)
    return o * jax.nn.silu(z), new_kv


def _linear_combine(l, r):
    return (l[0] * r[0], r[0] * l[1] + r[1])


def s5_core(proj, h0, a_re, a_im, b_re, b_im, c_re, c_im, d_skip, log_dt, w_glu_bf):
    Bn, L, _ = proj.shape
    u, z = jnp.split(proj, 2, axis=-1)
    uf = u.reshape(Bn, L, S5_GROUPS, S5_GROUP)
    a = lax.complex(a_re, a_im)
    dt = jnp.exp(log_dt)[:, None]
    a_bar = jnp.exp(a * dt)
    b_bar = ((a_bar - 1.0) / a)[..., None] * lax.complex(b_re, b_im)
    c = lax.complex(c_re, c_im)
    bu = jnp.einsum('gpc,blgc->blgp', b_bar, uf.astype(jnp.complex64))
    if h0 is not None:
        h0c = lax.complex(h0[..., 0], h0[..., 1])
        bu = bu.at[:, 0].add(a_bar * h0c)
    a_seq = jnp.broadcast_to(a_bar, bu.shape)
    _, h = lax.associative_scan(_linear_combine, (a_seq, bu), axis=1)
    y = jnp.einsum('gcp,blgp->blgc', c, h).real + d_skip.reshape(S5_GROUPS, S5_GROUP) * uf
    y = jax.nn.gelu(y.reshape(Bn, L, S5_WIDTH))
    glu = matmul(y.reshape(Bn * L, S5_WIDTH).astype(BF16), w_glu_bf).reshape(Bn, L, S5_WIDTH)
    y = y * jax.nn.sigmoid(glu)
    h_last = h[:, -1]
    return y * jax.nn.silu(z), jnp.stack([h_last.real, h_last.imag], axis=-1)


def l2_normalize(t, eps=1e-6):
    return t * lax.rsqrt(jnp.sum(t * t, axis=-1, keepdims=True) + eps)


def chunk_gated_delta(q, k, v, g, beta, S0):
    Bn, L, H, dk = k.shape
    dv = v.shape[-1]
    C = min(GDN_CHUNK, L)
    n = -(-L // C)
    pad = n * C - L

    def chunks(t):
        t = jnp.pad(t, [(0, 0), (0, pad)] + [(0, 0)] * (t.ndim - 2))
        t = t.reshape((Bn, n, C) + t.shape[2:])
        return jnp.moveaxis(t, 3, 2)

    qc, kc, vc, gc, bc = [chunks(t) for t in (q, k, v, g, beta)]
    gam = jnp.cumsum(gc, axis=-1)
    pos = jnp.arange(C)
    causal = pos[:, None] >= pos[None, :]
    strict = pos[:, None] > pos[None, :]
    decay = jnp.exp(jnp.where(causal, gam[..., :, None] - gam[..., None, :], -jnp.inf))
    kk = jnp.einsum('bnhid,bnhjd->bnhij', kc, kc)
    tri = jnp.eye(C, dtype=F32) + jnp.where(strict, bc[..., :, None] * kk * decay, 0.0)
    rhs = jnp.concatenate([bc[..., None] * vc, (bc * jnp.exp(gam))[..., None] * kc], axis=-1)
    sol = lax.linalg.triangular_solve(tri, rhs, left_side=True, lower=True, unit_diagonal=True)
    u, w = sol[..., :dv], sol[..., dv:]
    qk = jnp.einsum('bnhid,bnhjd->bnhij', qc, kc) * decay
    q_dec = qc * jnp.exp(gam)[..., None]
    k_dec = kc * jnp.exp(gam[..., -1:] - gam)[..., None]
    g_tot = jnp.exp(gam[..., -1])

    def step(S, xs):
        u_c, w_c, qk_c, qd_c, kd_c, gt_c = xs
        v_new = u_c - jnp.einsum('bhcd,bhde->bhce', w_c, S)
        o = jnp.einsum('bhcd,bhde->bhce', qd_c, S) + jnp.einsum('bhij,bhje->bhie', qk_c, v_new)
        S = S * gt_c[..., None, None] + jnp.einsum('bhcd,bhce->bhde', kd_c, v_new)
        return S, o

    xs = tuple(jnp.moveaxis(t, 1, 0) for t in (u, w, qk, q_dec, k_dec, g_tot))
    S, o = lax.scan(step, S0, xs)
    o = o.transpose(1, 0, 3, 2, 4).reshape(Bn, n * C, H, dv)[:, :L]
    return o, S


def gdn_core(qkv, z, ab, S0, conv_buf, conv_w, a_log, dt_bias, norm_w):
    Bn, L, _ = qkv.shape
    a, b = ab[..., :GDN_V_HEADS], ab[..., GDN_V_HEADS:2 * GDN_V_HEADS]
    if conv_buf is None:
        conv_buf = jnp.zeros((Bn, GDN_CONV - 1, GDN_CONV_CH), qkv.dtype)
    xx = jnp.concatenate([conv_buf, qkv], axis=1)
    conv = jax.nn.silu(sum(xx[:, j:j + L] * conv_w[j] for j in range(GDN_CONV)))
    new_buf = xx[:, L:]
    q, k, v = jnp.split(conv, [GDN_QK_WIDTH, 2 * GDN_QK_WIDTH], axis=-1)
    rep = GDN_V_HEADS // GDN_QK_HEADS
    q = jnp.repeat(l2_normalize(q.reshape(Bn, L, GDN_QK_HEADS, GDN_DK)), rep, axis=2) * (GDN_DK ** -0.5)
    k = jnp.repeat(l2_normalize(k.reshape(Bn, L, GDN_QK_HEADS, GDN_DK)), rep, axis=2)
    v = v.reshape(Bn, L, GDN_V_HEADS, GDN_DV)
    beta = jax.nn.sigmoid(b)
    g = -jnp.exp(a_log) * jax.nn.softplus(a + dt_bias)
    if S0 is None:
        S0 = jnp.zeros((Bn, GDN_V_HEADS, GDN_DK, GDN_DV), F32)
    o, S = chunk_gated_delta(q, k, v, g, beta, S0)
    of = o * lax.rsqrt(jnp.mean(o * o, axis=-1, keepdims=True) + 1e-6) * norm_w
    of = of * jax.nn.silu(z.reshape(Bn, L, GDN_V_HEADS, GDN_DV))
    return of.reshape(Bn, L, GDN_V_WIDTH), S, new_buf


def dsa_split(qkvz, qi, kiw):
    Bn, L, _ = qkvz.shape
    q = qkvz[..., :ATT_WIDTH].reshape(Bn, L, KV_D, N_HEADS // KV_D, HEAD_DIM)
    k = qkvz[..., ATT_WIDTH:ATT_WIDTH + D_KV].reshape(Bn, L, KV_D, HEAD_DIM)
    v = qkvz[..., ATT_WIDTH + D_KV:ATT_WIDTH + 2 * D_KV].reshape(Bn, L, KV_D, HEAD_DIM)
    z = qkvz[..., ATT_WIDTH + 2 * D_KV:]
    kv = jnp.stack([k, v], axis=2)
    qi = qi.reshape(Bn, L, IDX_HEADS, IDX_DIM)
    ki = kiw[..., :IDX_DIM]
    wi = kiw[..., IDX_DIM:IDX_DIM + IDX_HEADS]
    return q, kv, z, qi, ki, wi


def index_scores(qi, ki, wi):
    s = jnp.einsum('bthd,bsd->bths', qi, ki).astype(F32) * (IDX_DIM ** -0.5)
    return jnp.einsum('bths,bth->bts', jax.nn.relu(s), wi * (IDX_HEADS ** -0.5))


def gathered_attend(q, kvs, qpos, kpos, valid, rel_bias):
    n_kv, g = q.shape[-3], q.shape[-2]
    ks, vs = kvs[..., 0, :, :], kvs[..., 1, :, :]
    logits = jnp.einsum('bthgd,btkhd->bhgtk', q, ks).astype(F32) * (HEAD_DIM ** -0.5)
    dist = qpos[None, :, None] - kpos
    logits = logits + head_bias(rel_bias, dist, n_kv, g)
    p = masked_softmax(logits, (valid & (dist >= 0))[:, None, None])
    return jnp.einsum('bhgtk,btkhd->bthgd', p, vs)


def dsa_prompt_core(qkvz, qi, kiw, rel_bias):
    Bn, L, _ = qkvz.shape
    q, kv, z, qi, ki, wi = dsa_split(qkvz, qi, kiw)
    topk = min(TOPK_MAX, L // 4)
    key_pos = jnp.arange(L)

    def block(t0):
        qb = lax.dynamic_slice_in_dim(q, t0, Q_BLOCK, axis=1)
        qib = lax.dynamic_slice_in_dim(qi, t0, Q_BLOCK, axis=1)
        wib = lax.dynamic_slice_in_dim(wi, t0, Q_BLOCK, axis=1)
        qpos = t0 + jnp.arange(Q_BLOCK)
        sc = index_scores(qib, ki, wib)
        sc = jnp.where(key_pos[None, None, :] <= qpos[None, :, None], sc, -jnp.inf)
        vals, idx = lax.top_k(sc, topk)
        return gathered_attend(qb, take_rows(kv, idx), qpos, idx, vals > -jnp.inf, rel_bias)

    o = lax.map(block, jnp.arange(0, L, Q_BLOCK))
    o = jnp.moveaxis(o, 0, 1).reshape(Bn, L, ATT_WIDTH)
    return o * jax.nn.silu(z), kv, ki


def dsa_sample_core(qkvz, qi, kiw, kv_pool, kidx_pool, layer, page_table, rel_bias):
    Bd, L, _ = qkvz.shape
    n_pages = page_table.shape[1]
    past = n_pages * PAGE_SIZE
    q, kv, z, qi, ki, wi = dsa_split(qkvz, qi, kiw)
    ki_past = kidx_pool[layer, page_table].reshape(Bd, past, IDX_DIM)
    ki_all = jnp.concatenate([ki_past, ki], axis=1)
    total = past + L
    topk = min(TOPK_MAX, total // 4)
    qpos = past + jnp.arange(L)
    sc = index_scores(qi, ki_all, wi)
    sc = jnp.where(jnp.arange(total)[None, None, :] <= qpos[None, :, None], sc, -jnp.inf)
    vals, idx = lax.top_k(sc, topk)
    pidx = jnp.minimum(idx, past - 1)
    phys = jnp.take_along_axis(page_table, (pidx // PAGE_SIZE).reshape(Bd, -1), axis=1).reshape(idx.shape)
    kv_past = kv_pool[layer, phys, pidx % PAGE_SIZE]
    kv_new = take_rows(kv, jnp.clip(idx - past, 0, L - 1))
    kv_sel = jnp.where((idx >= past)[..., None, None, None], kv_new, kv_past)
    o = gathered_attend(q, kv_sel, qpos, idx, vals > -jnp.inf, rel_bias).reshape(Bd, L, ATT_WIDTH)
    return o * jax.nn.silu(z), kv, ki


def _pad_cols(w, n):
    return jnp.pad(w, ((0, 0), (0, n - w.shape[1])))


def kernel(x_prompt, x_sample, cache_a_kv, state_s5, state_gdn, state_gdn_conv, cache_d_kv, cache_d_kidx,
           page_table, p_prompt, p_sample, rel_bias, ln_g, ln_b, ple_gate_w, ple_w,
           a_w_in, a_sinks, a_w_out,
           s5_w_in, s5_a_re, s5_a_im, s5_b_re, s5_b_im, s5_c_re, s5_c_im, s5_d, s5_log_dt, s5_w_glu, s5_w_out,
           gdn_w_in, gdn_conv_w, gdn_a_log, gdn_dt_bias, gdn_norm_w, gdn_w_out,
           dsa_w_in, dsa_w_out):
    past_len = page_table.shape[1] * PAGE_SIZE
    x = join_tokens(x_prompt, x_sample)
    x_bf = x.astype(BF16)
    outs = {}

    def finish_layer(i, x, gated, w_out):
        h = matmul(gated.astype(BF16), w_out.astype(BF16))
        p_bf = join_tokens(p_prompt[i], p_sample[i]).astype(BF16)
        return post_norm_ple(x, h, p_bf, ln_g[i], ln_b[i], ple_gate_w[i].astype(BF16), ple_w[i].astype(BF16))

    proj = matmul(x_bf, a_w_in[0].astype(BF16))
    pp, ps = split_tokens(proj)
    gp, outs['a_p'] = swa_core(pp, None, 0, a_sinks[0], rel_bias)
    gs, outs['a_s'] = swa_core(ps, cache_a_kv[0], past_len, a_sinks[0], rel_bias)
    x, x_bf = finish_layer(0, x, join_tokens(gp, gs), a_w_out[0])

    proj = matmul(x_bf, s5_w_in[0].astype(BF16))
    pp, ps = split_tokens(proj)
    s5w = (s5_a_re[0], s5_a_im[0], s5_b_re[0], s5_b_im[0], s5_c_re[0], s5_c_im[0], s5_d[0], s5_log_dt[0],
           s5_w_glu[0].astype(BF16))
    gp, outs['s5_p'] = s5_core(pp, None, *s5w)
    gs, outs['s5_s'] = s5_core(ps, state_s5[0], *s5w)
    x, x_bf = finish_layer(1, x, join_tokens(gp, gs), s5_w_out[0])

    w_in = gdn_w_in[0]
    qkv = matmul(x_bf, w_in[:, :GDN_CONV_CH].astype(BF16))
    z = matmul(x_bf, w_in[:, GDN_CONV_CH:GDN_CONV_CH + GDN_V_WIDTH].astype(BF16))
    ab = matmul(x_bf, _pad_cols(w_in[:, GDN_CONV_CH + GDN_V_WIDTH:], LANE).astype(BF16))
    gw = (gdn_conv_w[0], gdn_a_log[0], gdn_dt_bias[0], gdn_norm_w[0])
    qkv_p, qkv_s = split_tokens(qkv)
    z_p, z_s = split_tokens(z)
    ab_p, ab_s = split_tokens(ab)
    gp, outs['gd_p'], outs['gc_p'] = gdn_core(qkv_p, z_p, ab_p, None, None, *gw)
    gs, outs['gd_s'], outs['gc_s'] = gdn_core(qkv_s, z_s, ab_s, state_gdn[0], state_gdn_conv[0], *gw)
    x, x_bf = finish_layer(2, x, join_tokens(gp, gs), gdn_w_out[0])

    w_in = dsa_w_in[0]
    c_z = 2 * ATT_WIDTH + 2 * D_KV
    c_qi = c_z + IDX_HEADS * IDX_DIM
    qkvz = matmul(x_bf, w_in[:, :c_z].astype(BF16))
    qi = matmul(x_bf, w_in[:, c_z:c_qi].astype(BF16))
    kiw = matmul(x_bf, _pad_cols(w_in[:, c_qi:], 2 * LANE).astype(BF16))
    qkvz_p, qkvz_s = split_tokens(qkvz)
    qi_p, qi_s = split_tokens(qi)
    kiw_p, kiw_s = split_tokens(kiw)
    gp, outs['dkv_p'], outs['dki_p'] = dsa_prompt_core(qkvz_p, qi_p, kiw_p, rel_bias)
    gs, outs['dkv_s'], outs['dki_s'] = dsa_sample_core(qkvz_s, qi_s, kiw_s, cache_d_kv, cache_d_kidx, 0,
                                                       page_table, rel_bias)
    x, x_bf = finish_layer(3, x, join_tokens(gp, gs), dsa_w_out[0])

    yp, ys = split_tokens(x)
    st = lambda name: outs[name][None]
    return (yp, ys, st('a_p'), st('a_s'), st('s5_p'), st('s5_s'), st('gd_p'), st('gd_s'),
            st('gc_p'), st('gc_s'), st('dkv_p'), st('dkv_s'), st('dki_p'), st('dki_s'))
```

```python
import functools
import math

import jax
import jax.numpy as jnp
from jax import lax
from jax.experimental import pallas as pl
from jax.experimental.pallas import tpu as pltpu

D_MODEL = 2048
BATCH = 4
SEQ = 2048
DEPTH = 4
DEC_BATCH = 32
DEC_SEQ = 4
PAGE_SIZE = 128
N_MIXERS = 4
PLE_DIM = 256
ALPHA = (2 * DEPTH) ** 0.25
LN_EPS = 1e-5
N_BUCKETS = 32
REL_MAX_DIST = 2048
N_HEADS = 32
HEAD_DIM = 64
ATT_WIDTH = N_HEADS * HEAD_DIM
WINDOW = 128
KV_A = 4
A_KV = KV_A * HEAD_DIM
KV_D = 8
D_KV = KV_D * HEAD_DIM
IDX_HEADS = 16
IDX_DIM = 128
TOPK_MAX = 256
Q_BLOCK = 128
S5_WIDTH = D_MODEL
S5_GROUP = 16
S5_GROUPS = S5_WIDTH // S5_GROUP
S5_STATE = 64
GDN_QK_HEADS = 16
GDN_V_HEADS = 32
GDN_DK = 128
GDN_DV = 128
GDN_CONV = 4
GDN_CHUNK = 64
GDN_QK_WIDTH = GDN_QK_HEADS * GDN_DK
GDN_V_WIDTH = GDN_V_HEADS * GDN_DV
GDN_CONV_CH = 2 * GDN_QK_WIDTH + GDN_V_WIDTH

F32 = jnp.float32
BF16 = jnp.bfloat16

N_PROMPT_TOK = BATCH * SEQ
N_SAMPLE_TOK = DEC_BATCH * DEC_SEQ
N_TOK = N_PROMPT_TOK + N_SAMPLE_TOK

V7X_VMEM_BYTES = 64 * 1024 * 1024
VMEM_LIMIT = 48 * 1024 * 1024
LANE = 128


def _mm_kernel(x_ref, w_ref, o_ref):
    o_ref[...] = jnp.dot(x_ref[...], w_ref[...], preferred_element_type=F32).astype(o_ref.dtype)


def _pick_tile(n, prefs):
    for t in prefs:
        if n % t == 0:
            return t
    raise ValueError(f"no tile for {n}")


def matmul(x, w, out_dtype=F32):
    m, k = x.shape
    n = w.shape[1]
    tm = _pick_tile(m, (640, 512, 320, 256, 128, 64, 32, 16, 8))
    tn = _pick_tile(n, (512, 384, 256, 128))
    return pl.pallas_call(
        _mm_kernel,
        grid=(m // tm, n // tn),
        in_specs=[pl.BlockSpec((tm, k), lambda i, j: (i, 0)),
                  pl.BlockSpec((k, tn), lambda i, j: (0, j))],
        out_specs=pl.BlockSpec((tm, tn), lambda i, j: (i, j)),
        out_shape=jax.ShapeDtypeStruct((m, n), out_dtype),
        compiler_params=pltpu.CompilerParams(
            dimension_semantics=("parallel", "parallel"), vmem_limit_bytes=VMEM_LIMIT),
        name="proj_matmul",
    )(x, w)


def _mm_ta_kernel(xt_ref, w_ref, o_ref):
    o_ref[...] = lax.dot_general(xt_ref[...], w_ref[...], (((0,), (0,)), ((), ())),
                                 preferred_element_type=F32).astype(o_ref.dtype)


def matmul_ta(xt, w, out_dtype=F32):
    k, m = xt.shape
    n = w.shape[1]
    tm = _pick_tile(m, (512, 256, 128))
    tn = _pick_tile(n, (512, 384, 256, 128))
    return pl.pallas_call(
        _mm_ta_kernel,
        grid=(m // tm, n // tn),
        in_specs=[pl.BlockSpec((k, tm), lambda i, j: (0, i)),
                  pl.BlockSpec((k, tn), lambda i, j: (0, j))],
        out_specs=pl.BlockSpec((tm, tn), lambda i, j: (i, j)),
        out_shape=jax.ShapeDtypeStruct((m, n), out_dtype),
        compiler_params=pltpu.CompilerParams(
            dimension_semantics=("parallel", "parallel"), vmem_limit_bytes=VMEM_LIMIT),
        name="proj_matmul_ta",
    )(xt, w)


POST_TM = 320
POST_TN = 512


def _post_kernel(x_ref, h_ref, p_ref, g_ref, b_ref, wg_ref, wp_ref, o_ref, obf_ref, y_sc, ybf_sc):
    j = pl.program_id(1)

    @pl.when(j == 0)
    def _():
        t = ALPHA * x_ref[...] + h_ref[...]
        mu = jnp.mean(t, axis=-1, keepdims=True)
        d = t - mu
        var = jnp.mean(d * d, axis=-1, keepdims=True)
        y = d * lax.rsqrt(var + LN_EPS) * g_ref[...] + b_ref[...]
        ybf_sc[...] = y.astype(BF16)
        for jj in range(D_MODEL // POST_TN):
            y_sc[jj] = y[:, jj * POST_TN:(jj + 1) * POST_TN]

    gate = jnp.dot(ybf_sc[...], wg_ref[...], preferred_element_type=F32)
    ple = jnp.dot(p_ref[...], wp_ref[...], preferred_element_type=F32)
    o = y_sc[j] + (1.0 / (1.0 + jnp.exp(-gate))) * ple
    o_ref[...] = o
    obf_ref[...] = o.astype(BF16)


def post_norm_ple(x, h, p_bf, g, b, wg_bf, wp_bf):
    m = x.shape[0]
    tm, tn = POST_TM, POST_TN
    return pl.pallas_call(
        _post_kernel,
        grid=(m // tm, D_MODEL // tn),
        in_specs=[pl.BlockSpec((tm, D_MODEL), lambda i, j: (i, 0)),
                  pl.BlockSpec((tm, D_MODEL), lambda i, j: (i, 0)),
                  pl.BlockSpec((tm, PLE_DIM), lambda i, j: (i, 0)),
                  pl.BlockSpec((1, D_MODEL), lambda i, j: (0, 0)),
                  pl.BlockSpec((1, D_MODEL), lambda i, j: (0, 0)),
                  pl.BlockSpec((D_MODEL, tn), lambda i, j: (0, j)),
                  pl.BlockSpec((PLE_DIM, tn), lambda i, j: (0, j))],
        out_specs=[pl.BlockSpec((tm, tn), lambda i, j: (i, j)),
                   pl.BlockSpec((tm, tn), lambda i, j: (i, j))],
        out_shape=[jax.ShapeDtypeStruct((m, D_MODEL), F32),
                   jax.ShapeDtypeStruct((m, D_MODEL), BF16)],
        scratch_shapes=[pltpu.VMEM((D_MODEL // tn, tm, tn), F32),
                        pltpu.VMEM((tm, D_MODEL), BF16)],
        compiler_params=pltpu.CompilerParams(
            dimension_semantics=("parallel", "arbitrary"), vmem_limit_bytes=VMEM_LIMIT),
        name="post_norm_ple",
    )(x, h, p_bf, g.reshape(1, D_MODEL), b.reshape(1, D_MODEL), wg_bf, wp_bf)


def rel_bucket(dist):
    n = jnp.maximum(dist, 0)
    exact = N_BUCKETS // 2
    logb = exact + (jnp.log(jnp.maximum(n, exact).astype(F32) / exact)
                    / math.log(REL_MAX_DIST / exact) * (N_BUCKETS - exact)).astype(jnp.int32)
    return jnp.where(n < exact, n, jnp.minimum(logb, N_BUCKETS - 1))


def head_bias(rel_bias, dist, n_kv, g):
    b = jnp.moveaxis(rel_bias[rel_bucket(dist)].astype(F32), -1, -3)
    return b.reshape(b.shape[:-3] + (n_kv, g) + b.shape[-2:])


def masked_softmax(logits, mask, sink=None):
    logits = jnp.where(mask, logits, -jnp.inf)
    m = jnp.max(logits, axis=-1, keepdims=True)
    if sink is not None:
        m = jnp.maximum(m, sink)
    e = jnp.exp(logits - m)
    den = jnp.sum(e, axis=-1, keepdims=True)
    if sink is not None:
        den = den + jnp.exp(sink - m)
    return e / den


def take_rows(rows, idx):
    return jax.vmap(lambda r, i: r[i])(rows, idx)


def split_tokens(t):
    c = t.shape[-1]
    return (t[:N_PROMPT_TOK].reshape(BATCH, SEQ, c), t[N_PROMPT_TOK:].reshape(DEC_BATCH, DEC_SEQ, c))


def join_tokens(tp, ts):
    c = tp.shape[-1]
    return jnp.concatenate([tp.reshape(N_PROMPT_TOK, c), ts.reshape(N_SAMPLE_TOK, c)], axis=0)


def window_attend(q, k, v, qpos, kpos, sinks, rel_bias):
    n_kv, g = q.shape[-3], q.shape[-2]
    dist = qpos[..., :, None] - kpos[..., None, :]
    mask = (dist >= 0) & (dist < WINDOW) & (kpos[..., None, :] >= 0)
    logits = jnp.einsum('...qhgd,...khd->...hgqk', q, k).astype(F32) * (HEAD_DIM ** -0.5)
    logits = logits + head_bias(rel_bias, dist, n_kv, g)
    sink = sinks.astype(F32).reshape(n_kv, g, 1, 1)
    p = masked_softmax(logits, mask[..., None, None, :, :], sink)
    return jnp.einsum('...hgqk,...khd->...qhgd', p.astype(v.dtype), v)


def swa_core(proj, kv_cache, start, sinks, rel_bias):
    Bn, L, _ = proj.shape
    G = N_HEADS // KV_A
    q, k, v, z = jnp.split(proj, [ATT_WIDTH, ATT_WIDTH + A_KV, ATT_WIDTH + 2 * A_KV], axis=-1)
    q = q.reshape(Bn, L, KV_A, G, HEAD_DIM)
    k = k.reshape(Bn, L, KV_A, HEAD_DIM)
    v = v.reshape(Bn, L, KV_A, HEAD_DIM)
    if kv_cache is None:
        nb = L // WINDOW
        qb = q.reshape(Bn, nb, WINDOW, KV_A, G, HEAD_DIM)
        kb = k.reshape(Bn, nb, WINDOW, KV_A, HEAD_DIM)
        vb = v.reshape(Bn, nb, WINDOW, KV_A, HEAD_DIM)
        prev = lambda t: jnp.concatenate([jnp.zeros_like(t[:, :1]), t[:, :-1]], axis=1)
        kk = jnp.concatenate([prev(kb), kb], axis=2)
        vv = jnp.concatenate([prev(vb), vb], axis=2)
        qpos = jnp.arange(L).reshape(nb, WINDOW)
        kpos = jnp.concatenate([qpos - WINDOW, qpos], axis=1)
        o = window_attend(qb, kk, vv, qpos, kpos, sinks, rel_bias)
        new_kv = jnp.stack([k[:, L - WINDOW:], v[:, L - WINDOW:]], axis=2)
    else:
        kk = jnp.concatenate([kv_cache[:, :, 0], k], axis=1)
        vv = jnp.concatenate([kv_cache[:, :, 1], v], axis=1)
        qpos = start + jnp.arange(L)
        kpos = start - WINDOW + jnp.arange(WINDOW + L)
        o = window_attend(q, kk, vv, qpos, kpos, sinks, rel_bias)
        new_kv = jnp.stack([kk[:, -WINDOW:], vv[:, -WINDOW:]], axis=2)
    o = o.reshape(Bn, L, ATT_WIDTH)
    return o * jax.nn.silu(z), new_kv


def _linear_combine(l, r):
    return (l[0] * r[0], r[0] * l[1] + r[1])


def s5_core(proj, h0, a_re, a_im, b_re, b_im, c_re, c_im, d_skip, log_dt, w_glu_bf):
    Bn, L, _ = proj.shape
    u, z = jnp.split(proj, 2, axis=-1)
    uf = u.reshape(Bn, L, S5_GROUPS, S5_GROUP)
    a = lax.complex(a_re, a_im)
    dt = jnp.exp(log_dt)[:, None]
    a_bar = jnp.exp(a * dt)
    b_bar = ((a_bar - 1.0) / a)[..., None] * lax.complex(b_re, b_im)
    c = lax.complex(c_re, c_im)
    bu = jnp.einsum('gpc,blgc->blgp', b_bar, uf.astype(jnp.complex64))
    if h0 is not None:
        h0c = lax.complex(h0[..., 0], h0[..., 1])
        bu = bu.at[:, 0].add(a_bar * h0c)
    a_seq = jnp.broadcast_to(a_bar, bu.shape)
    _, h = lax.associative_scan(_linear_combine, (a_seq, bu), axis=1)
    y = jnp.einsum('gcp,blgp->blgc', c, h).real + d_skip.reshape(S5_GROUPS, S5_GROUP) * uf
    y = jax.nn.gelu(y.reshape(Bn, L, S5_WIDTH))
    glu = matmul(y.reshape(Bn * L, S5_WIDTH).astype(BF16), w_glu_bf).reshape(Bn, L, S5_WIDTH)
    y = y * jax.nn.sigmoid(glu)
    h_last = h[:, -1]
    return y * jax.nn.silu(z), jnp.stack([h_last.real, h_last.imag], axis=-1)


def l2_normalize(t, eps=1e-6):
    return t * lax.rsqrt(jnp.sum(t * t, axis=-1, keepdims=True) + eps)


def chunk_gated_delta(q, k, v, g, beta, S0):
    Bn, L, H, dk = k.shape
    dv = v.shape[-1]
    C = min(GDN_CHUNK, L)
    n = -(-L // C)
    pad = n * C - L

    def chunks(t):
        t = jnp.pad(t, [(0, 0), (0, pad)] + [(0, 0)] * (t.ndim - 2))
        t = t.reshape((Bn, n, C) + t.shape[2:])
        return jnp.moveaxis(t, 3, 2)

    qc, kc, vc, gc, bc = [chunks(t) for t in (q, k, v, g, beta)]
    gam = jnp.cumsum(gc, axis=-1)
    pos = jnp.arange(C)
    causal = pos[:, None] >= pos[None, :]
    strict = pos[:, None] > pos[None, :]
    decay = jnp.exp(jnp.where(causal, gam[..., :, None] - gam[..., None, :], -jnp.inf))
    kk = jnp.einsum('bnhid,bnhjd->bnhij', kc, kc)
    tri = jnp.eye(C, dtype=F32) + jnp.where(strict, bc[..., :, None] * kk * decay, 0.0)
    rhs = jnp.concatenate([bc[..., None] * vc, (bc * jnp.exp(gam))[..., None] * kc], axis=-1)
    sol = lax.linalg.triangular_solve(tri, rhs, left_side=True, lower=True, unit_diagonal=True)
    u, w = sol[..., :dv], sol[..., dv:]
    qk = jnp.einsum('bnhid,bnhjd->bnhij', qc, kc) * decay
    q_dec = qc * jnp.exp(gam)[..., None]
    k_dec = kc * jnp.exp(gam[..., -1:] - gam)[..., None]
    g_tot = jnp.exp(gam[..., -1])

    def step(S, xs):
        u_c, w_c, qk_c, qd_c, kd_c, gt_c = xs
        v_new = u_c - jnp.einsum('bhcd,bhde->bhce', w_c, S)
        o = jnp.einsum('bhcd,bhde->bhce', qd_c, S) + jnp.einsum('bhij,bhje->bhie', qk_c, v_new)
        S = S * gt_c[..., None, None] + jnp.einsum('bhcd,bhce->bhde', kd_c, v_new)
        return S, o

    xs = tuple(jnp.moveaxis(t, 1, 0) for t in (u, w, qk, q_dec, k_dec, g_tot))
    S, o = lax.scan(step, S0, xs)
    o = o.transpose(1, 0, 3, 2, 4).reshape(Bn, n * C, H, dv)[:, :L]
    return o, S


def gdn_core(qkv, z, ab, S0, conv_buf, conv_w, a_log, dt_bias, norm_w):
    Bn, L, _ = qkv.shape
    a, b = ab[..., :GDN_V_HEADS], ab[..., GDN_V_HEADS:2 * GDN_V_HEADS]
    if conv_buf is None:
        conv_buf = jnp.zeros((Bn, GDN_CONV - 1, GDN_CONV_CH), qkv.dtype)
    xx = jnp.concatenate([conv_buf, qkv], axis=1)
    conv = jax.nn.silu(sum(xx[:, j:j + L] * conv_w[j] for j in range(GDN_CONV)))
    new_buf = xx[:, L:]
    q, k, v = jnp.split(conv, [GDN_QK_WIDTH, 2 * GDN_QK_WIDTH], axis=-1)
    rep = GDN_V_HEADS // GDN_QK_HEADS
    q = jnp.repeat(l2_normalize(q.reshape(Bn, L, GDN_QK_HEADS, GDN_DK)), rep, axis=2) * (GDN_DK ** -0.5)
    k = jnp.repeat(l2_normalize(k.reshape(Bn, L, GDN_QK_HEADS, GDN_DK)), rep, axis=2)
    v = v.reshape(Bn, L, GDN_V_HEADS, GDN_DV)
    beta = jax.nn.sigmoid(b)
    g = -jnp.exp(a_log) * jax.nn.softplus(a + dt_bias)
    if S0 is None:
        S0 = jnp.zeros((Bn, GDN_V_HEADS, GDN_DK, GDN_DV), F32)
    o, S = chunk_gated_delta(q, k, v, g, beta, S0)
    of = o * lax.rsqrt(jnp.mean(o * o, axis=-1, keepdims=True) + 1e-6) * norm_w
    of = of * jax.nn.silu(z.reshape(Bn, L, GDN_V_HEADS, GDN_DV))
    return of.reshape(Bn, L, GDN_V_WIDTH), S, new_buf


def index_scores(qi, ki, wi):
    s = jnp.einsum('bthd,bsd->bths', qi, ki).astype(F32) * (IDX_DIM ** -0.5)
    return jnp.einsum('bths,bth->bts', jax.nn.relu(s), wi * (IDX_HEADS ** -0.5))


def gathered_attend(q, kvs, qpos, kpos, valid, rel_bias):
    n_kv, g = q.shape[-3], q.shape[-2]
    ks, vs = kvs[..., 0, :, :], kvs[..., 1, :, :]
    logits = jnp.einsum('bthgd,btkhd->bhgtk', q, ks).astype(F32) * (HEAD_DIM ** -0.5)
    dist = qpos[None, :, None] - kpos
    logits = logits + head_bias(rel_bias, dist, n_kv, g)
    p = masked_softmax(logits, (valid & (dist >= 0))[:, None, None])
    return jnp.einsum('bhgtk,btkhd->bthgd', p, vs)


def dsa_sample_core(q, kv, z, qi, ki, wi, kv_pool, kidx_pool, layer, page_table, rel_bias):
    Bd, L, _ = q.shape
    n_pages = page_table.shape[1]
    past = n_pages * PAGE_SIZE
    q = q.reshape(Bd, L, KV_D, N_HEADS // KV_D, HEAD_DIM)
    kv = kv.reshape(Bd, L, 2, KV_D, HEAD_DIM)
    qi = qi.reshape(Bd, L, IDX_HEADS, IDX_DIM)
    ki_past = kidx_pool[layer, page_table].reshape(Bd, past, IDX_DIM)
    ki_all = jnp.concatenate([ki_past, ki], axis=1)
    total = past + L
    topk = min(TOPK_MAX, total // 4)
    qpos = past + jnp.arange(L)
    sc = index_scores(qi, ki_all, wi)
    sc = jnp.where(jnp.arange(total)[None, None, :] <= qpos[None, :, None], sc, -jnp.inf)
    vals, idx = lax.top_k(sc, topk)
    pidx = jnp.minimum(idx, past - 1)
    phys = jnp.take_along_axis(page_table, (pidx // PAGE_SIZE).reshape(Bd, -1), axis=1).reshape(idx.shape)
    kv_past = kv_pool[layer, phys, pidx % PAGE_SIZE]
    kv_new = take_rows(kv, jnp.clip(idx - past, 0, L - 1))
    kv_sel = jnp.where((idx >= past)[..., None, None, None], kv_new, kv_past)
    o = gathered_attend(q, kv_sel, qpos, idx, vals > -jnp.inf, rel_bias).reshape(Bd, L, ATT_WIDTH)
    return o * jax.nn.silu(z)


DSA_KC = 256
INT_MIN = -2 ** 31
NEG_BIG = -1e30
G_D = N_HEADS // KV_D
BIAS_WIN = DSA_KC + Q_BLOCK


def _sortable_key(s):
    b = pltpu.bitcast(s, jnp.int32)
    return jnp.where(b < 0, b ^ jnp.int32(0x7FFFFFFF), b)


def _dsa_prompt_kernel(qT_ref, qiT_ref, wiT_ref, zT_ref, ki_ref, k_ref, vT_ref, win_ref, o_ref,
                       key_sc, mask_sc, *, topk, idx_bits, cdt):
    qb = pl.program_id(1)
    t0 = qb * Q_BLOCK
    nch = (qb + 2) // 2
    t_idx = t0 + lax.broadcasted_iota(jnp.int32, (1, Q_BLOCK), 1)
    row_iota = lax.broadcasted_iota(jnp.int32, (DSA_KC, Q_BLOCK), 0)

    def score_chunk(c, carry):
        kic = ki_ref[0, c].astype(cdt)
        acc = jnp.zeros((DSA_KC, Q_BLOCK), F32)
        for hp in range(IDX_HEADS // 2):
            rhs = jnp.concatenate([qiT_ref[(2 * hp) * IDX_DIM:(2 * hp + 1) * IDX_DIM, :],
                                   qiT_ref[(2 * hp + 1) * IDX_DIM:(2 * hp + 2) * IDX_DIM, :]], axis=1)
            s = jnp.dot(kic, rhs, preferred_element_type=F32) * (IDX_DIM ** -0.5)
            s = jnp.maximum(s, 0.0)
            w0 = wiT_ref[2 * hp:2 * hp + 1, :] * (IDX_HEADS ** -0.5)
            w1 = wiT_ref[2 * hp + 1:2 * hp + 2, :] * (IDX_HEADS ** -0.5)
            acc = acc + s[:, :Q_BLOCK] * w0 + s[:, Q_BLOCK:] * w1
        s_idx = c * DSA_KC + row_iota
        key_sc[c] = jnp.where(s_idx <= t_idx, _sortable_key(acc), INT_MIN)
        return carry

    lax.fori_loop(0, nch, score_chunk, 0)

    def count(pred):
        def body(c, acc):
            hit = pred(key_sc[c], c * DSA_KC + row_iota)
            return acc + hit.reshape(DSA_KC // 8, 8, Q_BLOCK).sum(axis=0)
        acc = lax.fori_loop(0, nch, body, jnp.zeros((8, Q_BLOCK), jnp.int32))
        return jnp.sum(acc, axis=0, keepdims=True)

    c_nonneg = count(lambda k, s: jnp.where(k >= 0, 1, 0))
    thr = jnp.where(c_nonneg >= topk, 0, INT_MIN).astype(jnp.int32)

    def thr_bit(i, thr):
        cand = thr + jnp.left_shift(jnp.int32(1), 30 - i)
        return jnp.where(count(lambda k, s: jnp.where(k >= cand, 1, 0)) >= topk, cand, thr)

    thr = lax.fori_loop(0, 31, thr_bit, thr)
    need = topk - count(lambda k, s: jnp.where(k > thr, 1, 0))

    def lim_bit(i, lim):
        cand = lim + jnp.left_shift(jnp.int32(1), idx_bits - 1 - i)
        c = count(lambda k, s: jnp.where(k == thr, jnp.where(s < cand, 1, 0), 0))
        return jnp.where(c <= need, cand, lim)

    lim = lax.fori_loop(0, idx_bits, lim_bit, jnp.zeros((1, Q_BLOCK), jnp.int32))

    def mask_chunk(c, carry):
        k = key_sc[c]
        s_idx = c * DSA_KC + row_iota
        tie = jnp.where(k == thr, jnp.where(s_idx < lim, 0.0, NEG_BIG), NEG_BIG)
        m = jnp.where(k > thr, 0.0, tie)
        mask_sc[c] = jnp.where(k == INT_MIN, NEG_BIG, m)
        return carry

    lax.fori_loop(0, nch, mask_chunk, 0)

    for j in range(KV_D):
        jp = j // 2
        qj = jnp.concatenate([qT_ref[(G_D * j + g) * HEAD_DIM:(G_D * j + g + 1) * HEAD_DIM, :]
                              for g in range(G_D)], axis=1)
        qj = (qj.astype(F32) * (HEAD_DIM ** -0.5)).astype(cdt)
        zpad = jnp.zeros_like(qj)
        rhs = jnp.concatenate([qj, zpad] if j % 2 == 0 else [zpad, qj], axis=0)

        def chunk_body(c, carry, j=j, jp=jp, rhs=rhs):
            m, l, acc = carry
            kc = k_ref[0, c, :, jp * 2 * HEAD_DIM:(jp + 1) * 2 * HEAD_DIM].astype(cdt)
            s = jnp.dot(kc, rhs, preferred_element_type=F32)
            wt = win_ref[qb - 2 * c]
            madd = mask_sc[c]
            parts = []
            for g in range(G_D):
                h = G_D * j + g
                r = jnp.broadcast_to(wt[h:h + 1, :], (DSA_KC, BIAS_WIN))
                b = pltpu.roll(r, 0, 1, stride=1, stride_axis=0)[:, DSA_KC:]
                parts.append(s[:, g * Q_BLOCK:(g + 1) * Q_BLOCK] + (b + madd))
            s = jnp.concatenate(parts, axis=1)
            m_new = jnp.maximum(m, jnp.max(s, axis=0, keepdims=True))
            alpha = jnp.exp(m - m_new)
            p = jnp.exp(s - m_new)
            l = l * alpha + jnp.sum(p, axis=0, keepdims=True)
            vt = vT_ref[0, c, j * HEAD_DIM:(j + 1) * HEAD_DIM, :]
            acc = acc * alpha + jnp.dot(vt, p.astype(cdt), preferred_element_type=F32)
            return m_new, l, acc

        init = (jnp.full((1, G_D * Q_BLOCK), NEG_BIG, F32), jnp.zeros((1, G_D * Q_BLOCK), F32),
                jnp.zeros((HEAD_DIM, G_D * Q_BLOCK), F32))
        m, l, acc = lax.fori_loop(0, nch, chunk_body, init)
        o = acc * (1.0 / l)
        for g in range(G_D):
            r0 = (G_D * j + g) * HEAD_DIM
            z = zT_ref[r0:r0 + HEAD_DIM, :]
            gate = z * (1.0 / (1.0 + jnp.exp(-z)))
            o_ref[r0:r0 + HEAD_DIM, :] = (o[:, g * Q_BLOCK:(g + 1) * Q_BLOCK] * gate).astype(o_ref.dtype)


def dsa_bias_windows(rel_bias, seq_len):
    o = jnp.arange(seq_len // Q_BLOCK)[:, None]
    m = jnp.arange(BIAS_WIN)[None, :]
    d = jnp.maximum(o * Q_BLOCK + m - DSA_KC, 0)
    return jnp.moveaxis(rel_bias[rel_bucket(d)].astype(F32), -1, 1)


def dsa_prompt_attend(qqiT, wiT, zT, ki4, k4, v4T, win, *, n_batch, seq_len, cdt=BF16):
    nqb = seq_len // Q_BLOCK
    nc = seq_len // DSA_KC
    topk = min(TOPK_MAX, seq_len // 4)
    idx_bits = int(math.log2(seq_len)) + 1
    tok = lambda b, q: (0, b * nqb + q)
    per_batch = lambda b, q: (b, 0, 0, 0)
    return pl.pallas_call(
        functools.partial(_dsa_prompt_kernel, topk=topk, idx_bits=idx_bits, cdt=cdt),
        grid=(n_batch, nqb),
        in_specs=[pl.BlockSpec((ATT_WIDTH, Q_BLOCK), tok),
                  pl.BlockSpec((IDX_HEADS * IDX_DIM, Q_BLOCK), lambda b, q: (1, b * nqb + q)),
                  pl.BlockSpec((IDX_HEADS, Q_BLOCK), tok),
                  pl.BlockSpec((ATT_WIDTH, Q_BLOCK), tok),
                  pl.BlockSpec((1, nc, DSA_KC, IDX_DIM), per_batch),
                  pl.BlockSpec((1, nc, DSA_KC, D_KV), per_batch),
                  pl.BlockSpec((1, nc, D_KV, DSA_KC), per_batch),
                  pl.BlockSpec((nqb, N_HEADS, BIAS_WIN), lambda b, q: (0, 0, 0))],
        out_specs=pl.BlockSpec((ATT_WIDTH, Q_BLOCK), tok),
        out_shape=jax.ShapeDtypeStruct((ATT_WIDTH, n_batch * seq_len), BF16),
        scratch_shapes=[pltpu.VMEM((nc, DSA_KC, Q_BLOCK), jnp.int32),
                        pltpu.VMEM((nc, DSA_KC, Q_BLOCK), F32)],
        compiler_params=pltpu.CompilerParams(
            dimension_semantics=("parallel", "arbitrary"), vmem_limit_bytes=VMEM_LIMIT),
        name="dsa_prompt_attend",
    )(qqiT, qqiT, wiT, zT, ki4, k4, v4T, win)


def _pad_cols(w, n):
    return jnp.pad(w, ((0, 0), (0, n - w.shape[1])))


def kernel(x_prompt, x_sample, cache_a_kv, state_s5, state_gdn, state_gdn_conv, cache_d_kv, cache_d_kidx,
           page_table, p_prompt, p_sample, rel_bias, ln_g, ln_b, ple_gate_w, ple_w,
           a_w_in, a_sinks, a_w_out,
           s5_w_in, s5_a_re, s5_a_im, s5_b_re, s5_b_im, s5_c_re, s5_c_im, s5_d, s5_log_dt, s5_w_glu, s5_w_out,
           gdn_w_in, gdn_conv_w, gdn_a_log, gdn_dt_bias, gdn_norm_w, gdn_w_out,
           dsa_w_in, dsa_w_out):
    past_len = page_table.shape[1] * PAGE_SIZE
    x = join_tokens(x_prompt, x_sample)
    x_bf = x.astype(BF16)
    outs = {}

    def post(i, x, h):
        p_bf = join_tokens(p_prompt[i], p_sample[i]).astype(BF16)
        return post_norm_ple(x, h, p_bf, ln_g[i], ln_b[i], ple_gate_w[i].astype(BF16), ple_w[i].astype(BF16))

    def finish_layer(i, x, gated, w_out):
        return post(i, x, matmul(gated.astype(BF16), w_out.astype(BF16)))

    proj = matmul(x_bf, a_w_in[0].astype(BF16))
    pp, ps = split_tokens(proj)
    gp, outs['a_p'] = swa_core(pp, None, 0, a_sinks[0], rel_bias)
    gs, outs['a_s'] = swa_core(ps, cache_a_kv[0], past_len, a_sinks[0], rel_bias)
    x, x_bf = finish_layer(0, x, join_tokens(gp, gs), a_w_out[0])

    proj = matmul(x_bf, s5_w_in[0].astype(BF16))
    pp, ps = split_tokens(proj)
    s5w = (s5_a_re[0], s5_a_im[0], s5_b_re[0], s5_b_im[0], s5_c_re[0], s5_c_im[0], s5_d[0], s5_log_dt[0],
           s5_w_glu[0].astype(BF16))
    gp, outs['s5_p'] = s5_core(pp, None, *s5w)
    gs, outs['s5_s'] = s5_core(ps, state_s5[0], *s5w)
    x, x_bf = finish_layer(1, x, join_tokens(gp, gs), s5_w_out[0])

    w_in = gdn_w_in[0]
    qkv = matmul(x_bf, w_in[:, :GDN_CONV_CH].astype(BF16))
    z = matmul(x_bf, w_in[:, GDN_CONV_CH:GDN_CONV_CH + GDN_V_WIDTH].astype(BF16))
    ab = matmul(x_bf, _pad_cols(w_in[:, GDN_CONV_CH + GDN_V_WIDTH:], LANE).astype(BF16))
    gw = (gdn_conv_w[0], gdn_a_log[0], gdn_dt_bias[0], gdn_norm_w[0])
    qkv_p, qkv_s = split_tokens(qkv)
    z_p, z_s = split_tokens(z)
    ab_p, ab_s = split_tokens(ab)
    gp, outs['gd_p'], outs['gc_p'] = gdn_core(qkv_p, z_p, ab_p, None, None, *gw)
    gs, outs['gd_s'], outs['gc_s'] = gdn_core(qkv_s, z_s, ab_s, state_gdn[0], state_gdn_conv[0], *gw)
    x, x_bf = finish_layer(2, x, join_tokens(gp, gs), gdn_w_out[0])

    w_in = dsa_w_in[0]
    c_z = 2 * ATT_WIDTH + 2 * D_KV
    c_qi = c_z + IDX_HEADS * IDX_DIM
    c_kv = ATT_WIDTH + 2 * D_KV
    w_q, w_kv, w_z, w_qi = w_in[:, :ATT_WIDTH], w_in[:, ATT_WIDTH:c_kv], w_in[:, c_kv:c_z], w_in[:, c_z:c_qi]
    w_out_bf = dsa_w_out[0].astype(BF16)
    kv_nat = matmul(x_bf, w_kv.astype(BF16))
    kiw = matmul(x_bf, _pad_cols(w_in[:, c_qi:], 2 * LANE).astype(BF16))
    xT_bf = x_bf[:N_PROMPT_TOK].T
    qqiT = matmul(jnp.concatenate([w_q, w_qi], axis=1).T.astype(BF16), xT_bf, out_dtype=BF16)
    zT = matmul(w_z.T.astype(BF16), xT_bf)
    wiT = matmul(w_in[:, c_qi + IDX_DIM:].T.astype(BF16), xT_bf)
    nc = SEQ // DSA_KC
    kv_p = kv_nat[:N_PROMPT_TOK]
    v4T = jnp.swapaxes(kv_p[:, D_KV:].astype(BF16).reshape(BATCH, nc, DSA_KC, D_KV), 2, 3)
    gT = dsa_prompt_attend(qqiT, wiT, zT, kiw[:N_PROMPT_TOK].reshape(BATCH, nc, DSA_KC, 2 * LANE),
                           kv_p.reshape(BATCH, nc, DSA_KC, 2 * D_KV), v4T, dsa_bias_windows(rel_bias, SEQ),
                           n_batch=BATCH, seq_len=SEQ)
    h_p = matmul_ta(gT, w_out_bf)
    x_s = x_bf[N_PROMPT_TOK:]
    qzqi_s = matmul(x_s, jnp.concatenate([w_q, w_z, w_qi], axis=1).astype(BF16))
    s3 = lambda t: t.reshape(DEC_BATCH, DEC_SEQ, t.shape[-1])
    kiw_s = kiw[N_PROMPT_TOK:]
    gs = dsa_sample_core(s3(qzqi_s[:, :ATT_WIDTH]), s3(kv_nat[N_PROMPT_TOK:]), s3(qzqi_s[:, ATT_WIDTH:2 * ATT_WIDTH]),
                         s3(qzqi_s[:, 2 * ATT_WIDTH:]), s3(kiw_s[:, :IDX_DIM]),
                         s3(kiw_s[:, IDX_DIM:IDX_DIM + IDX_HEADS]), cache_d_kv, cache_d_kidx, 0, page_table, rel_bias)
    h_s = matmul(gs.reshape(N_SAMPLE_TOK, ATT_WIDTH).astype(BF16), w_out_bf)
    outs['dkv_p'] = kv_p.reshape(BATCH, SEQ, 2, KV_D, HEAD_DIM)
    outs['dkv_s'] = kv_nat[N_PROMPT_TOK:].reshape(DEC_BATCH, DEC_SEQ, 2, KV_D, HEAD_DIM)
    outs['dki_p'] = kiw[:N_PROMPT_TOK, :IDX_DIM].reshape(BATCH, SEQ, IDX_DIM)
    outs['dki_s'] = kiw_s[:, :IDX_DIM].reshape(DEC_BATCH, DEC_SEQ, IDX_DIM)
    x, x_bf = post(3, x, jnp.concatenate([h_p, h_s], axis=0))

    yp, ys = split_tokens(x)
    st = lambda name: outs[name][None]
    return (yp, ys, st('a_p'), st('a_s'), st('s5_p'), st('s5_s'), st('gd_p'), st('gd_s'),
            st('gc_p'), st('gc_s'), st('dkv_p'), st('dkv_s'), st('dki_p'), st('dki_s'))
```

```python
import functools
import math

import jax
import jax.numpy as jnp
from jax import lax
from jax.experimental import pallas as pl
from jax.experimental.pallas import tpu as pltpu

D_MODEL = 2048
BATCH = 4
SEQ = 2048
DEPTH = 4
DEC_BATCH = 32
DEC_SEQ = 4
PAGE_SIZE = 128
N_MIXERS = 4
PLE_DIM = 256
ALPHA = (2 * DEPTH) ** 0.25
LN_EPS = 1e-5
N_BUCKETS = 32
REL_MAX_DIST = 2048
N_HEADS = 32
HEAD_DIM = 64
ATT_WIDTH = N_HEADS * HEAD_DIM
WINDOW = 128
KV_A = 4
A_KV = KV_A * HEAD_DIM
KV_D = 8
D_KV = KV_D * HEAD_DIM
IDX_HEADS = 16
IDX_DIM = 128
TOPK_MAX = 256
Q_BLOCK = 128
S5_WIDTH = D_MODEL
S5_GROUP = 16
S5_GROUPS = S5_WIDTH // S5_GROUP
S5_STATE = 64
GDN_QK_HEADS = 16
GDN_V_HEADS = 32
GDN_DK = 128
GDN_DV = 128
GDN_CONV = 4
GDN_CHUNK = 64
GDN_QK_WIDTH = GDN_QK_HEADS * GDN_DK
GDN_V_WIDTH = GDN_V_HEADS * GDN_DV
GDN_CONV_CH = 2 * GDN_QK_WIDTH + GDN_V_WIDTH

F32 = jnp.float32
BF16 = jnp.bfloat16

N_PROMPT_TOK = BATCH * SEQ
N_SAMPLE_TOK = DEC_BATCH * DEC_SEQ
N_TOK = N_PROMPT_TOK + N_SAMPLE_TOK

V7X_VMEM_BYTES = 64 * 1024 * 1024
VMEM_LIMIT = 48 * 1024 * 1024
LANE = 128


def _mm_kernel(x_ref, w_ref, o_ref):
    o_ref[...] = jnp.dot(x_ref[...], w_ref[...], preferred_element_type=F32).astype(o_ref.dtype)


def _pick_tile(n, prefs):
    for t in prefs:
        if n % t == 0:
            return t
    raise ValueError(f"no tile for {n}")


def matmul(x, w, out_dtype=F32):
    m, k = x.shape
    n = w.shape[1]
    tm = _pick_tile(m, (640, 512, 320, 256, 128, 64, 32, 16, 8))
    tn = _pick_tile(n, (512, 384, 256, 128))
    return pl.pallas_call(
        _mm_kernel,
        grid=(m // tm, n // tn),
        in_specs=[pl.BlockSpec((tm, k), lambda i, j: (i, 0)),
                  pl.BlockSpec((k, tn), lambda i, j: (0, j))],
        out_specs=pl.BlockSpec((tm, tn), lambda i, j: (i, j)),
        out_shape=jax.ShapeDtypeStruct((m, n), out_dtype),
        compiler_params=pltpu.CompilerParams(
            dimension_semantics=("parallel", "parallel"), vmem_limit_bytes=VMEM_LIMIT),
        name="proj_matmul",
    )(x, w)


def _mm_ta_kernel(xt_ref, w_ref, o_ref):
    o_ref[...] = lax.dot_general(xt_ref[...], w_ref[...], (((0,), (0,)), ((), ())),
                                 preferred_element_type=F32).astype(o_ref.dtype)


def matmul_ta(xt, w, out_dtype=F32):
    k, m = xt.shape
    n = w.shape[1]
    tm = _pick_tile(m, (512, 256, 128))
    tn = _pick_tile(n, (512, 384, 256, 128))
    return pl.pallas_call(
        _mm_ta_kernel,
        grid=(m // tm, n // tn),
        in_specs=[pl.BlockSpec((k, tm), lambda i, j: (0, i)),
                  pl.BlockSpec((k, tn), lambda i, j: (0, j))],
        out_specs=pl.BlockSpec((tm, tn), lambda i, j: (i, j)),
        out_shape=jax.ShapeDtypeStruct((m, n), out_dtype),
        compiler_params=pltpu.CompilerParams(
            dimension_semantics=("parallel", "parallel"), vmem_limit_bytes=VMEM_LIMIT),
        name="proj_matmul_ta",
    )(xt, w)


POST_TM = 320
POST_TN = 512


def _post_kernel(x_ref, h_ref, p_ref, g_ref, b_ref, wg_ref, wp_ref, o_ref, obf_ref, y_sc, ybf_sc):
    j = pl.program_id(1)

    @pl.when(j == 0)
    def _():
        t = ALPHA * x_ref[...] + h_ref[...]
        mu = jnp.mean(t, axis=-1, keepdims=True)
        d = t - mu
        var = jnp.mean(d * d, axis=-1, keepdims=True)
        y = d * lax.rsqrt(var + LN_EPS) * g_ref[...] + b_ref[...]
        ybf_sc[...] = y.astype(BF16)
        for jj in range(D_MODEL // POST_TN):
            y_sc[jj] = y[:, jj * POST_TN:(jj + 1) * POST_TN]

    gate = jnp.dot(ybf_sc[...], wg_ref[...], preferred_element_type=F32)
    ple = jnp.dot(p_ref[...], wp_ref[...], preferred_element_type=F32)
    o = y_sc[j] + (1.0 / (1.0 + jnp.exp(-gate))) * ple
    o_ref[...] = o
    obf_ref[...] = o.astype(BF16)


def post_norm_ple(x, h, p_bf, g, b, wg_bf, wp_bf):
    m = x.shape[0]
    tm, tn = POST_TM, POST_TN
    return pl.pallas_call(
        _post_kernel,
        grid=(m // tm, D_MODEL // tn),
        in_specs=[pl.BlockSpec((tm, D_MODEL), lambda i, j: (i, 0)),
                  pl.BlockSpec((tm, D_MODEL), lambda i, j: (i, 0)),
                  pl.BlockSpec((tm, PLE_DIM), lambda i, j: (i, 0)),
                  pl.BlockSpec((1, D_MODEL), lambda i, j: (0, 0)),
                  pl.BlockSpec((1, D_MODEL), lambda i, j: (0, 0)),
                  pl.BlockSpec((D_MODEL, tn), lambda i, j: (0, j)),
                  pl.BlockSpec((PLE_DIM, tn), lambda i, j: (0, j))],
        out_specs=[pl.BlockSpec((tm, tn), lambda i, j: (i, j)),
                   pl.BlockSpec((tm, tn), lambda i, j: (i, j))],
        out_shape=[jax.ShapeDtypeStruct((m, D_MODEL), F32),
                   jax.ShapeDtypeStruct((m, D_MODEL), BF16)],
        scratch_shapes=[pltpu.VMEM((D_MODEL // tn, tm, tn), F32),
                        pltpu.VMEM((tm, D_MODEL), BF16)],
        compiler_params=pltpu.CompilerParams(
            dimension_semantics=("parallel", "arbitrary"), vmem_limit_bytes=VMEM_LIMIT),
        name="post_norm_ple",
    )(x, h, p_bf, g.reshape(1, D_MODEL), b.reshape(1, D_MODEL), wg_bf, wp_bf)


def rel_bucket(dist):
    n = jnp.maximum(dist, 0)
    exact = N_BUCKETS // 2
    logb = exact + (jnp.log(jnp.maximum(n, exact).astype(F32) / exact)
                    / math.log(REL_MAX_DIST / exact) * (N_BUCKETS - exact)).astype(jnp.int32)
    return jnp.where(n < exact, n, jnp.minimum(logb, N_BUCKETS - 1))


def head_bias(rel_bias, dist, n_kv, g):
    b = jnp.moveaxis(rel_bias[rel_bucket(dist)].astype(F32), -1, -3)
    return b.reshape(b.shape[:-3] + (n_kv, g) + b.shape[-2:])


def masked_softmax(logits, mask, sink=None):
    logits = jnp.where(mask, logits, -jnp.inf)
    m = jnp.max(logits, axis=-1, keepdims=True)
    if sink is not None:
        m = jnp.maximum(m, sink)
    e = jnp.exp(logits - m)
    den = jnp.sum(e, axis=-1, keepdims=True)
    if sink is not None:
        den = den + jnp.exp(sink - m)
    return e / den


def take_rows(rows, idx):
    return jax.vmap(lambda r, i: r[i])(rows, idx)


def split_tokens(t):
    c = t.shape[-1]
    return (t[:N_PROMPT_TOK].reshape(BATCH, SEQ, c), t[N_PROMPT_TOK:].reshape(DEC_BATCH, DEC_SEQ, c))


def join_tokens(tp, ts):
    c = tp.shape[-1]
    return jnp.concatenate([tp.reshape(N_PROMPT_TOK, c), ts.reshape(N_SAMPLE_TOK, c)], axis=0)


def window_attend(q, k, v, qpos, kpos, sinks, rel_bias):
    n_kv, g = q.shape[-3], q.shape[-2]
    dist = qpos[..., :, None] - kpos[..., None, :]
    mask = (dist >= 0) & (dist < WINDOW) & (kpos[..., None, :] >= 0)
    logits = jnp.einsum('...qhgd,...khd->...hgqk', q, k).astype(F32) * (HEAD_DIM ** -0.5)
    logits = logits + head_bias(rel_bias, dist, n_kv, g)
    sink = sinks.astype(F32).reshape(n_kv, g, 1, 1)
    p = masked_softmax(logits, mask[..., None, None, :, :], sink)
    return jnp.einsum('...hgqk,...khd->...qhgd', p.astype(v.dtype), v)


def swa_core(proj, kv_cache, start, sinks, rel_bias):
    Bn, L, _ = proj.shape
    G = N_HEADS // KV_A
    q, k, v, z = jnp.split(proj, [ATT_WIDTH, ATT_WIDTH + A_KV, ATT_WIDTH + 2 * A_KV], axis=-1)
    q = q.reshape(Bn, L, KV_A, G, HEAD_DIM)
    k = k.reshape(Bn, L, KV_A, HEAD_DIM)
    v = v.reshape(Bn, L, KV_A, HEAD_DIM)
    if kv_cache is None:
        nb = L // WINDOW
        qb = q.reshape(Bn, nb, WINDOW, KV_A, G, HEAD_DIM)
        kb = k.reshape(Bn, nb, WINDOW, KV_A, HEAD_DIM)
        vb = v.reshape(Bn, nb, WINDOW, KV_A, HEAD_DIM)
        prev = lambda t: jnp.concatenate([jnp.zeros_like(t[:, :1]), t[:, :-1]], axis=1)
        kk = jnp.concatenate([prev(kb), kb], axis=2)
        vv = jnp.concatenate([prev(vb), vb], axis=2)
        qpos = jnp.arange(L).reshape(nb, WINDOW)
        kpos = jnp.concatenate([qpos - WINDOW, qpos], axis=1)
        o = window_attend(qb, kk, vv, qpos, kpos, sinks, rel_bias)
        new_kv = jnp.stack([k[:, L - WINDOW:], v[:, L - WINDOW:]], axis=2)
    else:
        kk = jnp.concatenate([kv_cache[:, :, 0], k], axis=1)
        vv = jnp.concatenate([kv_cache[:, :, 1], v], axis=1)
        qpos = start + jnp.arange(L)
        kpos = start - WINDOW + jnp.arange(WINDOW + L)
        o = window_attend(q, kk, vv, qpos, kpos, sinks, rel_bias)
        new_kv = jnp.stack([kk[:, -WINDOW:], vv[:, -WINDOW:]], axis=2)
    o = o.reshape(Bn, L, ATT_WIDTH)
    return o * jax.nn.silu(z), new_kv


def _linear_combine(l, r):
    return (l[0] * r[0], r[0] * l[1] + r[1])


def s5_core(proj, h0, a_re, a_im, b_re, b_im, c_re, c_im, d_skip, log_dt, w_glu_bf):
    Bn, L, _ = proj.shape
    u, z = jnp.split(proj, 2, axis=-1)
    uf = u.reshape(Bn, L, S5_GROUPS, S5_GROUP)
    a = lax.complex(a_re, a_im)
    dt = jnp.exp(log_dt)[:, None]
    a_bar = jnp.exp(a * dt)
    b_bar = ((a_bar - 1.0) / a)[..., None] * lax.complex(b_re, b_im)
    c = lax.complex(c_re, c_im)
    bu = jnp.einsum('gpc,blgc->blgp', b_bar, uf.astype(jnp.complex64))
    if h0 is not None:
        h0c = lax.complex(h0[..., 0], h0[..., 1])
        bu = bu.at[:, 0].add(a_bar * h0c)
    a_seq = jnp.broadcast_to(a_bar, bu.shape)
    _, h = lax.associative_scan(_linear_combine, (a_seq, bu), axis=1)
    y = jnp.einsum('gcp,blgp->blgc', c, h).real + d_skip.reshape(S5_GROUPS, S5_GROUP) * uf
    y = jax.nn.gelu(y.reshape(Bn, L, S5_WIDTH))
    glu = matmul(y.reshape(Bn * L, S5_WIDTH).astype(BF16), w_glu_bf).reshape(Bn, L, S5_WIDTH)
    y = y * jax.nn.sigmoid(glu)
    h_last = h[:, -1]
    return y * jax.nn.silu(z), jnp.stack([h_last.real, h_last.imag], axis=-1)


S5_SLAB_G = 8
S5_SLAB_CH = S5_SLAB_G * S5_GROUP
S5_SLAB_ST = S5_SLAB_G * S5_STATE
S5_N_SLABS = S5_GROUPS // S5_SLAB_G
S5_CHAINS = 8
S5_HALF_CH = S5_CHAINS * S5_SLAB_CH
S5_T = 256
S5_LT = 2 * S5_SLAB_ST // LANE


def _gelu_tanh(x):
    return 0.5 * x * (1.0 + jnp.tanh(math.sqrt(2.0 / math.pi) * (x + 0.044715 * (x * x * x))))


def s5_tables(a_re, a_im, b_re, b_im, c_re, c_im, log_dt):
    a = lax.complex(a_re, a_im)
    dt = jnp.exp(log_dt)[:, None]
    a_bar = jnp.exp(a * dt)
    b_bar = ((a_bar - 1.0) / a)[..., None] * lax.complex(b_re, b_im)
    eye = jnp.eye(S5_SLAB_G, dtype=F32)

    def b_blk(t):
        t = t.reshape(S5_N_SLABS, S5_SLAB_G, S5_STATE, S5_GROUP)
        return jnp.einsum('ij,sipc->sicjp', eye, t).reshape(S5_N_SLABS, S5_SLAB_CH, S5_SLAB_ST)

    def c_blk(t):
        t = t.reshape(S5_N_SLABS, S5_SLAB_G, S5_GROUP, S5_STATE)
        return jnp.einsum('ij,sicp->sjpic', eye, t).reshape(S5_N_SLABS, S5_SLAB_ST, S5_SLAB_CH)

    bcat = jnp.concatenate([b_blk(b_bar.real), b_blk(b_bar.imag)], axis=2)
    ccat = jnp.concatenate([c_blk(c_re), -c_blk(c_im)], axis=1)
    a_cat = jnp.concatenate([a_bar.real.reshape(S5_N_SLABS, S5_SLAB_ST),
                             a_bar.imag.reshape(S5_N_SLABS, S5_SLAB_ST)], axis=1)
    return a_cat, bcat, ccat


def _s5_prompt_kernel(u_ref, bcat_ref, ccat_ref, a_ref, d_ref, y_ref, hout_ref, sc, h_sc, *, cdt):
    tc = pl.program_id(2)
    n_lt_half = S5_LT // 2

    @pl.when(tc == 0)
    def _():
        h_sc[...] = jnp.zeros_like(h_sc)

    for j in range(S5_CHAINS):
        uj = u_ref[:, j * S5_SLAB_CH:(j + 1) * S5_SLAB_CH].astype(cdt)
        bu = jnp.dot(uj, bcat_ref[0, j], preferred_element_type=F32)
        for lt in range(S5_LT):
            sc[lt, pl.ds(j, S5_T, stride=S5_CHAINS), :] = bu[:, lt * LANE:(lt + 1) * LANE]

    a_re = [a_ref[0, :, lt * LANE:(lt + 1) * LANE] for lt in range(n_lt_half)]
    a_im = [a_ref[0, :, (n_lt_half + lt) * LANE:(n_lt_half + lt + 1) * LANE] for lt in range(n_lt_half)]

    def step(t, h):
        r0 = pl.multiple_of(t * S5_CHAINS, S5_CHAINS)
        new = list(h)
        for lt in range(n_lt_half):
            hr, hi = h[lt], h[n_lt_half + lt]
            nr = a_re[lt] * hr - a_im[lt] * hi + sc[lt, pl.ds(r0, S5_CHAINS), :]
            ni = a_re[lt] * hi + a_im[lt] * hr + sc[n_lt_half + lt, pl.ds(r0, S5_CHAINS), :]
            sc[lt, pl.ds(r0, S5_CHAINS), :] = nr
            sc[n_lt_half + lt, pl.ds(r0, S5_CHAINS), :] = ni
            new[lt], new[n_lt_half + lt] = nr, ni
        return tuple(new)

    h = lax.fori_loop(0, S5_T, step, tuple(h_sc[lt] for lt in range(S5_LT)), unroll=8)
    for lt in range(S5_LT):
        h_sc[lt] = h[lt]
        hout_ref[0, 0, :, lt * LANE:(lt + 1) * LANE] = h[lt]

    for j in range(S5_CHAINS):
        hcat = jnp.concatenate([sc[lt, pl.ds(j, S5_T, stride=S5_CHAINS), :] for lt in range(S5_LT)], axis=1)
        cols = slice(j * S5_SLAB_CH, (j + 1) * S5_SLAB_CH)
        y = jnp.dot(hcat.astype(cdt), ccat_ref[0, j], preferred_element_type=F32) + d_ref[0, :, cols] * u_ref[:, cols]
        y_ref[:, cols] = _gelu_tanh(y)


def s5_prompt(proj, a_cat, bcat, ccat, d_skip, *, n_batch, seq_len, n_rows_out, cdt=BF16):
    n_t = seq_len // S5_T
    n_half = S5_WIDTH // S5_HALF_CH
    half = lambda t: t.reshape((n_half, S5_CHAINS) + t.shape[1:])
    return pl.pallas_call(
        functools.partial(_s5_prompt_kernel, cdt=cdt),
        grid=(n_batch, n_half, n_t),
        in_specs=[pl.BlockSpec((S5_T, S5_HALF_CH), lambda b, hf, t: (b * n_t + t, hf)),
                  pl.BlockSpec((1, S5_CHAINS, S5_SLAB_CH, 2 * S5_SLAB_ST), lambda b, hf, t: (hf, 0, 0, 0)),
                  pl.BlockSpec((1, S5_CHAINS, 2 * S5_SLAB_ST, S5_SLAB_CH), lambda b, hf, t: (hf, 0, 0, 0)),
                  pl.BlockSpec((1, S5_CHAINS, 2 * S5_SLAB_ST), lambda b, hf, t: (hf, 0, 0)),
                  pl.BlockSpec((1, 1, S5_HALF_CH), lambda b, hf, t: (hf, 0, 0))],
        out_specs=[pl.BlockSpec((S5_T, S5_HALF_CH), lambda b, hf, t: (b * n_t + t, hf)),
                   pl.BlockSpec((1, 1, S5_CHAINS, 2 * S5_SLAB_ST), lambda b, hf, t: (b, hf, 0, 0))],
        out_shape=[jax.ShapeDtypeStruct((n_rows_out, S5_WIDTH), F32),
                   jax.ShapeDtypeStruct((n_batch, n_half, S5_CHAINS, 2 * S5_SLAB_ST), F32)],
        scratch_shapes=[pltpu.VMEM((S5_LT, S5_T * S5_CHAINS, LANE), F32),
                        pltpu.VMEM((S5_LT, S5_CHAINS, LANE), F32)],
        compiler_params=pltpu.CompilerParams(
            dimension_semantics=("parallel", "parallel", "arbitrary"), vmem_limit_bytes=VMEM_LIMIT),
        name="s5_prompt",
    )(proj, half(bcat.astype(cdt)), half(ccat.astype(cdt)), half(a_cat), d_skip.reshape(n_half, 1, S5_HALF_CH))


def _s5_sample_kernel(u_ref, bcat_ref, ccat_ref, a_ref, d_ref, h0_ref, yin_ref, y_ref, hout_ref, sc, *,
                      n_b, n_t, cdt):
    del yin_ref
    u = u_ref[...]
    bu = jnp.dot(u.astype(cdt), bcat_ref[0], preferred_element_type=F32)
    a_re = a_ref[0, :, :S5_SLAB_ST]
    a_im = a_ref[0, :, S5_SLAB_ST:]
    for bg in range(n_b // 8):
        hr = h0_ref[0, bg * 8:(bg + 1) * 8, :S5_SLAB_ST]
        hi = h0_ref[0, bg * 8:(bg + 1) * 8, S5_SLAB_ST:]
        for t in range(n_t):
            r = t * n_b + bg * 8
            hr, hi = (a_re * hr - a_im * hi + bu[r:r + 8, :S5_SLAB_ST],
                      a_re * hi + a_im * hr + bu[r:r + 8, S5_SLAB_ST:])
            sc[r:r + 8, :S5_SLAB_ST] = hr
            sc[r:r + 8, S5_SLAB_ST:] = hi
        hout_ref[0, bg * 8:(bg + 1) * 8, :S5_SLAB_ST] = hr
        hout_ref[0, bg * 8:(bg + 1) * 8, S5_SLAB_ST:] = hi
    y = jnp.dot(sc[...].astype(cdt), ccat_ref[0], preferred_element_type=F32) + d_ref[0] * u
    y_ref[...] = _gelu_tanh(y)


def s5_sample(u_tb, a_cat, bcat, ccat, d_skip, h0_cat, y_all, *, n_b, n_t, row_block, cdt=BF16):
    rows = n_t * n_b
    return pl.pallas_call(
        functools.partial(_s5_sample_kernel, n_b=n_b, n_t=n_t, cdt=cdt),
        grid=(S5_N_SLABS,),
        in_specs=[pl.BlockSpec((rows, S5_SLAB_CH), lambda s: (0, s)),
                  pl.BlockSpec((1, S5_SLAB_CH, 2 * S5_SLAB_ST), lambda s: (s, 0, 0)),
                  pl.BlockSpec((1, 2 * S5_SLAB_ST, S5_SLAB_CH), lambda s: (s, 0, 0)),
                  pl.BlockSpec((1, 1, 2 * S5_SLAB_ST), lambda s: (s, 0, 0)),
                  pl.BlockSpec((1, 1, S5_SLAB_CH), lambda s: (s, 0, 0)),
                  pl.BlockSpec((1, n_b, 2 * S5_SLAB_ST), lambda s: (s, 0, 0)),
                  pl.BlockSpec(memory_space=pl.ANY)],
        out_specs=[pl.BlockSpec((rows, S5_SLAB_CH), lambda s: (row_block, s)),
                   pl.BlockSpec((1, n_b, 2 * S5_SLAB_ST), lambda s: (s, 0, 0))],
        out_shape=[jax.ShapeDtypeStruct(y_all.shape, F32),
                   jax.ShapeDtypeStruct((S5_N_SLABS, n_b, 2 * S5_SLAB_ST), F32)],
        scratch_shapes=[pltpu.VMEM((rows, 2 * S5_SLAB_ST), F32)],
        input_output_aliases={6: 0},
        compiler_params=pltpu.CompilerParams(
            dimension_semantics=("arbitrary",), vmem_limit_bytes=VMEM_LIMIT),
        name="s5_sample",
    )(u_tb, bcat.astype(cdt), ccat.astype(cdt), a_cat.reshape(S5_N_SLABS, 1, 2 * S5_SLAB_ST),
      d_skip.reshape(S5_N_SLABS, 1, S5_SLAB_CH), h0_cat, y_all)


GLU_TM = 320
GLU_TN = 512


def _glu_kernel(yfull_ref, w_ref, ycol_ref, z_ref, o_ref, ybf_sc):
    @pl.when(pl.program_id(1) == 0)
    def _():
        ybf_sc[...] = yfull_ref[...].astype(ybf_sc.dtype)

    glu = jnp.dot(ybf_sc[...], w_ref[...], preferred_element_type=F32)
    z = z_ref[...]
    y = ycol_ref[...]
    o_ref[...] = (y * (1.0 / (1.0 + jnp.exp(-glu))) * (z * (1.0 / (1.0 + jnp.exp(-z))))).astype(o_ref.dtype)


def s5_glu_gate(y, w_glu, proj, cdt=BF16):
    m = y.shape[0]
    tm, tn = _pick_tile(m, (GLU_TM, 256, 128, 64, 32, 16)), GLU_TN
    z_off = S5_WIDTH // tn
    return pl.pallas_call(
        _glu_kernel,
        grid=(m // tm, S5_WIDTH // tn),
        in_specs=[pl.BlockSpec((tm, S5_WIDTH), lambda i, j: (i, 0)),
                  pl.BlockSpec((S5_WIDTH, tn), lambda i, j: (0, j)),
                  pl.BlockSpec((tm, tn), lambda i, j: (i, j)),
                  pl.BlockSpec((tm, tn), lambda i, j: (i, z_off + j))],
        out_specs=pl.BlockSpec((tm, tn), lambda i, j: (i, j)),
        out_shape=jax.ShapeDtypeStruct((m, S5_WIDTH), BF16),
        scratch_shapes=[pltpu.VMEM((tm, S5_WIDTH), cdt)],
        compiler_params=pltpu.CompilerParams(
            dimension_semantics=("parallel", "arbitrary"), vmem_limit_bytes=VMEM_LIMIT),
        name="s5_glu_gate",
    )(y, w_glu.astype(cdt), y, proj)


def s5_layer(proj, state_in, tables, d_skip, w_glu, *, n_batch, seq_len, n_dec, dec_len, cdt=BF16):
    a_cat, bcat, ccat = tables
    n_p = n_batch * seq_len
    n_s = n_dec * dec_len
    y_all, h_p = s5_prompt(proj, a_cat, bcat, ccat, d_skip, n_batch=n_batch, seq_len=seq_len,
                           n_rows_out=n_p + n_s, cdt=cdt)
    u_tb = jnp.swapaxes(proj[n_p:, :S5_WIDTH].reshape(n_dec, dec_len, S5_WIDTH), 0, 1).reshape(n_s, S5_WIDTH)
    h0 = state_in.reshape(n_dec, S5_N_SLABS, S5_SLAB_ST, 2)
    h0_cat = jnp.concatenate([jnp.swapaxes(h0[..., 0], 0, 1), jnp.swapaxes(h0[..., 1], 0, 1)], axis=-1)
    y_all, h_s = s5_sample(u_tb, a_cat, bcat, ccat, d_skip, h0_cat, y_all, n_b=n_dec, n_t=dec_len,
                           row_block=n_p // n_s, cdt=cdt)
    y_s = jnp.swapaxes(y_all[n_p:].reshape(dec_len, n_dec, S5_WIDTH), 0, 1).reshape(n_s, S5_WIDTH)
    y_all = lax.dynamic_update_slice(y_all, y_s, (n_p, 0))
    gated = s5_glu_gate(y_all, w_glu, proj, cdt=cdt)
    hp = h_p.reshape(n_batch, S5_N_SLABS, 2, S5_SLAB_ST)
    st_p = jnp.stack([hp[:, :, 0], hp[:, :, 1]], axis=-1).reshape(n_batch, S5_GROUPS, S5_STATE, 2)
    hs = jnp.swapaxes(h_s, 0, 1).reshape(n_dec, S5_N_SLABS, 2, S5_SLAB_ST)
    st_s = jnp.stack([hs[:, :, 0], hs[:, :, 1]], axis=-1).reshape(n_dec, S5_GROUPS, S5_STATE, 2)
    return gated, st_p, st_s


def l2_normalize(t, eps=1e-6):
    return t * lax.rsqrt(jnp.sum(t * t, axis=-1, keepdims=True) + eps)


def chunk_gated_delta(q, k, v, g, beta, S0):
    Bn, L, H, dk = k.shape
    dv = v.shape[-1]
    C = min(GDN_CHUNK, L)
    n = -(-L // C)
    pad = n * C - L

    def chunks(t):
        t = jnp.pad(t, [(0, 0), (0, pad)] + [(0, 0)] * (t.ndim - 2))
        t = t.reshape((Bn, n, C) + t.shape[2:])
        return jnp.moveaxis(t, 3, 2)

    qc, kc, vc, gc, bc = [chunks(t) for t in (q, k, v, g, beta)]
    gam = jnp.cumsum(gc, axis=-1)
    pos = jnp.arange(C)
    causal = pos[:, None] >= pos[None, :]
    strict = pos[:, None] > pos[None, :]
    decay = jnp.exp(jnp.where(causal, gam[..., :, None] - gam[..., None, :], -jnp.inf))
    kk = jnp.einsum('bnhid,bnhjd->bnhij', kc, kc)
    tri = jnp.eye(C, dtype=F32) + jnp.where(strict, bc[..., :, None] * kk * decay, 0.0)
    rhs = jnp.concatenate([bc[..., None] * vc, (bc * jnp.exp(gam))[..., None] * kc], axis=-1)
    sol = lax.linalg.triangular_solve(tri, rhs, left_side=True, lower=True, unit_diagonal=True)
    u, w = sol[..., :dv], sol[..., dv:]
    qk = jnp.einsum('bnhid,bnhjd->bnhij', qc, kc) * decay
    q_dec = qc * jnp.exp(gam)[..., None]
    k_dec = kc * jnp.exp(gam[..., -1:] - gam)[..., None]
    g_tot = jnp.exp(gam[..., -1])

    def step(S, xs):
        u_c, w_c, qk_c, qd_c, kd_c, gt_c = xs
        v_new = u_c - jnp.einsum('bhcd,bhde->bhce', w_c, S)
        o = jnp.einsum('bhcd,bhde->bhce', qd_c, S) + jnp.einsum('bhij,bhje->bhie', qk_c, v_new)
        S = S * gt_c[..., None, None] + jnp.einsum('bhcd,bhce->bhde', kd_c, v_new)
        return S, o

    xs = tuple(jnp.moveaxis(t, 1, 0) for t in (u, w, qk, q_dec, k_dec, g_tot))
    S, o = lax.scan(step, S0, xs)
    o = o.transpose(1, 0, 3, 2, 4).reshape(Bn, n * C, H, dv)[:, :L]
    return o, S


def gdn_core(qkv, z, ab, S0, conv_buf, conv_w, a_log, dt_bias, norm_w):
    Bn, L, _ = qkv.shape
    a, b = ab[..., :GDN_V_HEADS], ab[..., GDN_V_HEADS:2 * GDN_V_HEADS]
    if conv_buf is None:
        conv_buf = jnp.zeros((Bn, GDN_CONV - 1, GDN_CONV_CH), qkv.dtype)
    xx = jnp.concatenate([conv_buf, qkv], axis=1)
    conv = jax.nn.silu(sum(xx[:, j:j + L] * conv_w[j] for j in range(GDN_CONV)))
    new_buf = xx[:, L:]
    q, k, v = jnp.split(conv, [GDN_QK_WIDTH, 2 * GDN_QK_WIDTH], axis=-1)
    rep = GDN_V_HEADS // GDN_QK_HEADS
    q = jnp.repeat(l2_normalize(q.reshape(Bn, L, GDN_QK_HEADS, GDN_DK)), rep, axis=2) * (GDN_DK ** -0.5)
    k = jnp.repeat(l2_normalize(k.reshape(Bn, L, GDN_QK_HEADS, GDN_DK)), rep, axis=2)
    v = v.reshape(Bn, L, GDN_V_HEADS, GDN_DV)
    beta = jax.nn.sigmoid(b)
    g = -jnp.exp(a_log) * jax.nn.softplus(a + dt_bias)
    if S0 is None:
        S0 = jnp.zeros((Bn, GDN_V_HEADS, GDN_DK, GDN_DV), F32)
    o, S = chunk_gated_delta(q, k, v, g, beta, S0)
    of = o * lax.rsqrt(jnp.mean(o * o, axis=-1, keepdims=True) + 1e-6) * norm_w
    of = of * jax.nn.silu(z.reshape(Bn, L, GDN_V_HEADS, GDN_DV))
    return of.reshape(Bn, L, GDN_V_WIDTH), S, new_buf


GDN_CONV_TT = 256
GDN_CONV_CW = 1024
GDN_HIST = 8
GDN_HB = 4
GDN_TT = 256
GDN_SAMPLE_ROWS = 8


def _gdn_conv_kernel(x_ref, hist_ref, w_ref, o_ref, *, rows, n_t, zero_first):
    i, j = pl.program_id(0), pl.program_id(1)
    hist = hist_ref[...]
    if zero_first:
        hist = jnp.where(i % n_t == 0, 0.0, hist)
    ext = jnp.concatenate([hist, x_ref[...]], axis=0)
    acc = ext[GDN_HIST:GDN_HIST + rows] * w_ref[GDN_CONV - 1:GDN_CONV, :]
    for s in range(1, GDN_CONV):
        acc = acc + ext[GDN_HIST - s:GDN_HIST - s + rows] * w_ref[GDN_CONV - 1 - s:GDN_CONV - s, :]
    conv = acc * (1.0 / (1.0 + jnp.exp(-acc)))
    n_qk_blocks = 2 * GDN_QK_WIDTH // GDN_CONV_CW

    @pl.when(j >= n_qk_blocks)
    def _():
        o_ref[...] = conv

    @pl.when(j < n_qk_blocks)
    def _():
        scale = jnp.where(j < GDN_QK_WIDTH // GDN_CONV_CW, GDN_DK ** -0.5, 1.0)
        for h in range(GDN_CONV_CW // GDN_DK):
            t = conv[:, h * GDN_DK:(h + 1) * GDN_DK]
            n = t * lax.rsqrt(jnp.sum(t * t, axis=-1, keepdims=True) + 1e-6)
            o_ref[:, h * GDN_DK:(h + 1) * GDN_DK] = n * scale


def gdn_conv(x, hist_src, conv_w, *, rows, n_blocks, n_t, data_map, hist_map, zero_first):
    n_out = n_blocks * rows
    return pl.pallas_call(
        functools.partial(_gdn_conv_kernel, rows=rows, n_t=n_t, zero_first=zero_first),
        grid=(n_blocks, GDN_CONV_CH // GDN_CONV_CW),
        in_specs=[pl.BlockSpec((rows, GDN_CONV_CW), lambda i, j: (data_map(i), j)),
                  pl.BlockSpec((GDN_HIST, GDN_CONV_CW), lambda i, j: (hist_map(i), j)),
                  pl.BlockSpec((GDN_CONV, GDN_CONV_CW), lambda i, j: (0, j))],
        out_specs=pl.BlockSpec((rows, GDN_CONV_CW), lambda i, j: (i, j)),
        out_shape=jax.ShapeDtypeStruct((n_out, GDN_CONV_CH), F32),
        compiler_params=pltpu.CompilerParams(
            dimension_semantics=("parallel", "parallel"), vmem_limit_bytes=VMEM_LIMIT),
        name="gdn_conv",
    )(x, hist_src, conv_w)


def _gdn_chunk_kernel(q_ref, k_ref, v_ref, z_ref, ab_ref, alog_ref, dtb_ref, nw_ref, s0_ref, o_ref, sout_ref,
                      s_sc, *, chunk, n_inner, n_tt, valid_len):
    C = chunk
    hb, tt = pl.program_id(1), pl.program_id(2)

    @pl.when(tt == 0)
    def _():
        s_sc[...] = s0_ref[0]

    rowi = lax.broadcasted_iota(jnp.int32, (C, C), 0)
    coli = lax.broadcasted_iota(jnp.int32, (C, C), 1)
    causal = rowi >= coli
    strict = rowi > coli
    ltri = jnp.where(causal, 1.0, 0.0)
    utri = jnp.where(rowi <= coli, 1.0, 0.0)
    eye = jnp.where(rowi == coli, 1.0, 0.0)
    hi = lax.Precision.HIGHEST
    shift = (LANE - hb * GDN_HB) % LANE
    alog = pltpu.roll(jnp.broadcast_to(alog_ref[...], (8, LANE)), shift, 1)[0:1]
    dtb = pltpu.roll(jnp.broadcast_to(dtb_ref[...], (8, LANE)), shift, 1)[0:1]
    nw = nw_ref[...]
    tok_valid = lax.broadcasted_iota(jnp.int32, (C, LANE), 0) < valid_len

    def chunk_body(c, carry):
        r0 = pl.multiple_of(c * C, C)
        ab = pltpu.roll(ab_ref[pl.ds(r0, C), :], shift, 1)
        xa = ab + dtb
        softplus = jnp.maximum(xa, 0.0) + jnp.log1p(jnp.exp(-jnp.abs(xa)))
        g_all = jnp.where(tok_valid, -jnp.exp(alog) * softplus, 0.0)
        beta_all = jnp.where(tok_valid, 1.0 / (1.0 + jnp.exp(-ab)), 0.0)
        gam_all = jnp.dot(ltri, g_all, preferred_element_type=F32, precision=hi)
        gamT_all = lax.dot_general(g_all, utri, (((0,), (0,)), ((), ())),
                                   preferred_element_type=F32, precision=hi)
        for i in range(GDN_HB):
            qk_cols = slice((i // 2) * GDN_DK, (i // 2 + 1) * GDN_DK)
            v_cols = slice(i * GDN_DV, (i + 1) * GDN_DV)
            qh = q_ref[pl.ds(r0, C), qk_cols]
            kh = k_ref[pl.ds(r0, C), qk_cols]
            vh = v_ref[pl.ds(r0, C), v_cols]
            gam_c = jnp.broadcast_to(gam_all[:, i:i + 1], (C, LANE))
            gam_r = jnp.broadcast_to(gamT_all[i:i + 1, :], (C, C))
            beta_c = jnp.broadcast_to(beta_all[:, 32 + i:33 + i], (C, LANE))
            gam_last = jnp.broadcast_to(gam_all[C - 1:C, i:i + 1], (1, LANE))
            decay = jnp.where(causal, jnp.exp(jnp.where(causal, gam_c[:, :C] - gam_r, 0.0)), 0.0)
            kb = kh.astype(BF16)
            kk = lax.dot_general(kb, kb, (((1,), (1,)), ((), ())), preferred_element_type=F32)
            neg_a = jnp.where(strict, -(beta_c[:, :C] * kk * decay), 0.0)
            p_inv = eye + neg_a
            m_pow = neg_a
            for _ in range(int(math.log2(C)) - 1):
                m_pow = jnp.dot(m_pow, m_pow, preferred_element_type=F32, precision=hi)
                p_inv = p_inv + jnp.dot(p_inv, m_pow, preferred_element_type=F32, precision=hi)
            eg = jnp.exp(gam_c)
            rhs = jnp.concatenate([beta_c * vh, (beta_c * eg) * kh], axis=1)
            sol = jnp.dot(p_inv, rhs, preferred_element_type=F32, precision=hi)
            u, w = sol[:, :GDN_DV], sol[:, GDN_DV:]
            qk = lax.dot_general(qh.astype(BF16), kb, (((1,), (1,)), ((), ())), preferred_element_type=F32) * decay
            s_old = s_sc[i]
            s_bf = s_old.astype(BF16)
            v_new = u - jnp.dot(w.astype(BF16), s_bf, preferred_element_type=F32)
            vn_bf = v_new.astype(BF16)
            o = (jnp.dot((qh * eg).astype(BF16), s_bf, preferred_element_type=F32)
                 + jnp.dot(qk.astype(BF16), vn_bf, preferred_element_type=F32))
            k_dec = kh * jnp.exp(gam_last - gam_c)
            s_sc[i] = s_old * jnp.exp(gam_last) + lax.dot_general(
                k_dec.astype(BF16), vn_bf, (((0,), (0,)), ((), ())), preferred_element_type=F32)
            rms = lax.rsqrt(jnp.mean(o * o, axis=-1, keepdims=True) + 1e-6)
            zz = z_ref[pl.ds(r0, C), v_cols]
            o_ref[pl.ds(r0, C), v_cols] = (o * rms * nw * (zz * (1.0 / (1.0 + jnp.exp(-zz))))).astype(o_ref.dtype)
        return carry

    lax.fori_loop(0, n_inner, chunk_body, 0)

    @pl.when(tt == n_tt - 1)
    def _():
        sout_ref[0] = s_sc[...]


def gdn_chunk(conv, z, ab, a_log, dt_bias, norm_w, s0, *, n_seq, rows_per_seq, rows_per_step, chunk, valid_len,
              z_col_off, out_dtype):
    n_tt = rows_per_seq // rows_per_step
    n_inner = rows_per_step // chunk
    n_hb = GDN_V_HEADS // GDN_HB
    qw, vw = GDN_HB // 2 * GDN_DK, GDN_HB * GDN_DV
    k_off, v_off, z_off = GDN_QK_WIDTH // qw, 2 * GDN_QK_WIDTH // vw, z_col_off // vw
    row = lambda b, hb, t: b * n_tt + t
    pad_row = lambda p: jnp.pad(p.astype(F32), (0, LANE - p.shape[0])).reshape(1, LANE)
    return pl.pallas_call(
        functools.partial(_gdn_chunk_kernel, chunk=chunk, n_inner=n_inner, n_tt=n_tt, valid_len=valid_len),
        grid=(n_seq, n_hb, n_tt),
        in_specs=[pl.BlockSpec((rows_per_step, qw), lambda b, hb, t: (row(b, hb, t), hb)),
                  pl.BlockSpec((rows_per_step, qw), lambda b, hb, t: (row(b, hb, t), k_off + hb)),
                  pl.BlockSpec((rows_per_step, vw), lambda b, hb, t: (row(b, hb, t), v_off + hb)),
                  pl.BlockSpec((rows_per_step, vw), lambda b, hb, t: (row(b, hb, t), z_off + hb)),
                  pl.BlockSpec((rows_per_step, LANE), lambda b, hb, t: (row(b, hb, t), 0)),
                  pl.BlockSpec((1, LANE), lambda b, hb, t: (0, 0)),
                  pl.BlockSpec((1, LANE), lambda b, hb, t: (0, 0)),
                  pl.BlockSpec((1, GDN_DV), lambda b, hb, t: (0, 0)),
                  pl.BlockSpec((1, GDN_HB, GDN_DK, GDN_DV), lambda b, hb, t: (b, hb, 0, 0))],
        out_specs=[pl.BlockSpec((rows_per_step, vw), lambda b, hb, t: (row(b, hb, t), hb)),
                   pl.BlockSpec((1, GDN_HB, GDN_DK, GDN_DV), lambda b, hb, t: (b, hb, 0, 0))],
        out_shape=[jax.ShapeDtypeStruct((n_seq * rows_per_seq, GDN_V_WIDTH), out_dtype),
                   jax.ShapeDtypeStruct((n_seq, GDN_V_HEADS, GDN_DK, GDN_DV), F32)],
        scratch_shapes=[pltpu.VMEM((GDN_HB, GDN_DK, GDN_DV), F32)],
        compiler_params=pltpu.CompilerParams(
            dimension_semantics=("parallel", "parallel", "arbitrary"), vmem_limit_bytes=VMEM_LIMIT),
        name="gdn_chunk",
    )(conv, conv, conv, z, ab, pad_row(a_log), pad_row(dt_bias), norm_w.astype(F32).reshape(1, GDN_DV), s0)


def gdn_layer(qkvz, ab, state_in, conv_in, conv_w, a_log, dt_bias, norm_w, *, n_batch, seq_len, n_dec, dec_len):
    n_p = n_batch * seq_len
    n_tp = seq_len // GDN_CONV_TT
    hist_per_block = GDN_CONV_TT // GDN_HIST
    conv_p = gdn_conv(qkvz, qkvz, conv_w, rows=GDN_CONV_TT, n_blocks=n_batch * n_tp, n_t=n_tp,
                      data_map=lambda i: i, hist_map=lambda i: jnp.maximum(i * hist_per_block - 1, 0),
                      zero_first=True)
    zeros_s = jnp.zeros((n_batch, GDN_V_HEADS, GDN_DK, GDN_DV), F32)
    gated_p, st_p = gdn_chunk(conv_p, qkvz, ab, a_log, dt_bias, norm_w, zeros_s, n_seq=n_batch,
                              rows_per_seq=seq_len, rows_per_step=GDN_TT, chunk=GDN_CHUNK, valid_len=GDN_CHUNK,
                              z_col_off=GDN_CONV_CH, out_dtype=BF16)
    buf_p = qkvz[:n_p, :GDN_CONV_CH].reshape(n_batch, seq_len, GDN_CONV_CH)[:, seq_len - (GDN_CONV - 1):]
    R = GDN_SAMPLE_ROWS
    x_s = qkvz[n_p:].reshape(n_dec, dec_len, -1)
    pad_t = lambda t, front: jnp.pad(t, ((0, 0), (front, R - front - t.shape[1]), (0, 0)))
    ext = jnp.concatenate([pad_t(conv_in, R - (GDN_CONV - 1)), pad_t(x_s[..., :GDN_CONV_CH], 0)], axis=1)
    ext = ext.reshape(n_dec * 2 * R, GDN_CONV_CH)
    conv_s = gdn_conv(ext, ext, conv_w, rows=R, n_blocks=n_dec, n_t=1,
                      data_map=lambda i: 2 * i + 1, hist_map=lambda i: 2 * i, zero_first=False)
    z_s = pad_t(x_s[..., GDN_CONV_CH:], 0).reshape(n_dec * R, GDN_V_WIDTH)
    ab_s = pad_t(ab[n_p:].reshape(n_dec, dec_len, LANE), 0).reshape(n_dec * R, LANE)
    gated_s, st_s = gdn_chunk(conv_s, z_s, ab_s, a_log, dt_bias, norm_w, state_in, n_seq=n_dec, rows_per_seq=R,
                              rows_per_step=R, chunk=R, valid_len=dec_len, z_col_off=0, out_dtype=F32)
    gated_s = gated_s.reshape(n_dec, R, GDN_V_WIDTH)[:, :dec_len].reshape(n_dec * dec_len, GDN_V_WIDTH)
    buf_s = jnp.concatenate([conv_in, x_s[..., :GDN_CONV_CH]], axis=1)[:, dec_len:]
    return gated_p, gated_s, st_p, st_s, buf_p, buf_s


def index_scores(qi, ki, wi):
    s = jnp.einsum('bthd,bsd->bths', qi, ki).astype(F32) * (IDX_DIM ** -0.5)
    return jnp.einsum('bths,bth->bts', jax.nn.relu(s), wi * (IDX_HEADS ** -0.5))


def gathered_attend(q, kvs, qpos, kpos, valid, rel_bias):
    n_kv, g = q.shape[-3], q.shape[-2]
    ks, vs = kvs[..., 0, :, :], kvs[..., 1, :, :]
    logits = jnp.einsum('bthgd,btkhd->bhgtk', q, ks).astype(F32) * (HEAD_DIM ** -0.5)
    dist = qpos[None, :, None] - kpos
    logits = logits + head_bias(rel_bias, dist, n_kv, g)
    p = masked_softmax(logits, (valid & (dist >= 0))[:, None, None])
    return jnp.einsum('bhgtk,btkhd->bthgd', p, vs)


def dsa_sample_core(q, kv, z, qi, ki, wi, kv_pool, kidx_pool, layer, page_table, rel_bias):
    Bd, L, _ = q.shape
    n_pages = page_table.shape[1]
    past = n_pages * PAGE_SIZE
    q = q.reshape(Bd, L, KV_D, N_HEADS // KV_D, HEAD_DIM)
    kv = kv.reshape(Bd, L, 2, KV_D, HEAD_DIM)
    qi = qi.reshape(Bd, L, IDX_HEADS, IDX_DIM)
    ki_past = kidx_pool[layer, page_table].reshape(Bd, past, IDX_DIM)
    ki_all = jnp.concatenate([ki_past, ki], axis=1)
    total = past + L
    topk = min(TOPK_MAX, total // 4)
    qpos = past + jnp.arange(L)
    sc = index_scores(qi, ki_all, wi)
    sc = jnp.where(jnp.arange(total)[None, None, :] <= qpos[None, :, None], sc, -jnp.inf)
    vals, idx = lax.top_k(sc, topk)
    pidx = jnp.minimum(idx, past - 1)
    phys = jnp.take_along_axis(page_table, (pidx // PAGE_SIZE).reshape(Bd, -1), axis=1).reshape(idx.shape)
    kv_past = kv_pool[layer, phys, pidx % PAGE_SIZE]
    kv_new = take_rows(kv, jnp.clip(idx - past, 0, L - 1))
    kv_sel = jnp.where((idx >= past)[..., None, None, None], kv_new, kv_past)
    o = gathered_attend(q, kv_sel, qpos, idx, vals > -jnp.inf, rel_bias).reshape(Bd, L, ATT_WIDTH)
    return o * jax.nn.silu(z)


DSA_KC = 256
INT_MIN = -2 ** 31
NEG_BIG = -1e30
G_D = N_HEADS // KV_D
BIAS_WIN = DSA_KC + Q_BLOCK


def _sortable_key(s):
    b = pltpu.bitcast(s, jnp.int32)
    return jnp.where(b < 0, b ^ jnp.int32(0x7FFFFFFF), b)


def _dsa_prompt_kernel(qT_ref, qiT_ref, wiT_ref, zT_ref, ki_ref, k_ref, vT_ref, win_ref, o_ref,
                       key_sc, mask_sc, *, topk, idx_bits, cdt):
    qb = pl.program_id(1)
    t0 = qb * Q_BLOCK
    nch = (qb + 2) // 2
    t_idx = t0 + lax.broadcasted_iota(jnp.int32, (1, Q_BLOCK), 1)
    row_iota = lax.broadcasted_iota(jnp.int32, (DSA_KC, Q_BLOCK), 0)

    def score_chunk(c, carry):
        kic = ki_ref[0, c].astype(cdt)
        acc = jnp.zeros((DSA_KC, Q_BLOCK), F32)
        for hp in range(IDX_HEADS // 2):
            rhs = jnp.concatenate([qiT_ref[(2 * hp) * IDX_DIM:(2 * hp + 1) * IDX_DIM, :],
                                   qiT_ref[(2 * hp + 1) * IDX_DIM:(2 * hp + 2) * IDX_DIM, :]], axis=1)
            s = jnp.dot(kic, rhs, preferred_element_type=F32) * (IDX_DIM ** -0.5)
            s = jnp.maximum(s, 0.0)
            w0 = wiT_ref[2 * hp:2 * hp + 1, :] * (IDX_HEADS ** -0.5)
            w1 = wiT_ref[2 * hp + 1:2 * hp + 2, :] * (IDX_HEADS ** -0.5)
            acc = acc + s[:, :Q_BLOCK] * w0 + s[:, Q_BLOCK:] * w1
        s_idx = c * DSA_KC + row_iota
        key_sc[c] = jnp.where(s_idx <= t_idx, _sortable_key(acc), INT_MIN)
        return carry

    lax.fori_loop(0, nch, score_chunk, 0)

    def count(pred):
        def body(c, acc):
            hit = pred(key_sc[c], c * DSA_KC + row_iota)
            return acc + hit.reshape(DSA_KC // 8, 8, Q_BLOCK).sum(axis=0)
        acc = lax.fori_loop(0, nch, body, jnp.zeros((8, Q_BLOCK), jnp.int32))
        return jnp.sum(acc, axis=0, keepdims=True)

    c_nonneg = count(lambda k, s: jnp.where(k >= 0, 1, 0))
    thr = jnp.where(c_nonneg >= topk, 0, INT_MIN).astype(jnp.int32)

    def thr_bit(i, thr):
        cand = thr + jnp.left_shift(jnp.int32(1), 30 - i)
        return jnp.where(count(lambda k, s: jnp.where(k >= cand, 1, 0)) >= topk, cand, thr)

    thr = lax.fori_loop(0, 31, thr_bit, thr)
    need = topk - count(lambda k, s: jnp.where(k > thr, 1, 0))

    def lim_bit(i, lim):
        cand = lim + jnp.left_shift(jnp.int32(1), idx_bits - 1 - i)
        c = count(lambda k, s: jnp.where(k == thr, jnp.where(s < cand, 1, 0), 0))
        return jnp.where(c <= need, cand, lim)

    lim = lax.fori_loop(0, idx_bits, lim_bit, jnp.zeros((1, Q_BLOCK), jnp.int32))

    def mask_chunk(c, carry):
        k = key_sc[c]
        s_idx = c * DSA_KC + row_iota
        tie = jnp.where(k == thr, jnp.where(s_idx < lim, 0.0, NEG_BIG), NEG_BIG)
        m = jnp.where(k > thr, 0.0, tie)
        mask_sc[c] = jnp.where(k == INT_MIN, NEG_BIG, m)
        return carry

    lax.fori_loop(0, nch, mask_chunk, 0)

    for j in range(KV_D):
        jp = j // 2
        qj = jnp.concatenate([qT_ref[(G_D * j + g) * HEAD_DIM:(G_D * j + g + 1) * HEAD_DIM, :]
                              for g in range(G_D)], axis=1)
        qj = (qj.astype(F32) * (HEAD_DIM ** -0.5)).astype(cdt)
        zpad = jnp.zeros_like(qj)
        rhs = jnp.concatenate([qj, zpad] if j % 2 == 0 else [zpad, qj], axis=0)

        def chunk_body(c, carry, j=j, jp=jp, rhs=rhs):
            m, l, acc = carry
            kc = k_ref[0, c, :, jp * 2 * HEAD_DIM:(jp + 1) * 2 * HEAD_DIM].astype(cdt)
            s = jnp.dot(kc, rhs, preferred_element_type=F32)
            wt = win_ref[qb - 2 * c]
            madd = mask_sc[c]
            parts = []
            for g in range(G_D):
                h = G_D * j + g
                r = jnp.broadcast_to(wt[h:h + 1, :], (DSA_KC, BIAS_WIN))
                b = pltpu.roll(r, 0, 1, stride=1, stride_axis=0)[:, DSA_KC:]
                parts.append(s[:, g * Q_BLOCK:(g + 1) * Q_BLOCK] + (b + madd))
            s = jnp.concatenate(parts, axis=1)
            m_new = jnp.maximum(m, jnp.max(s, axis=0, keepdims=True))
            alpha = jnp.exp(m - m_new)
            p = jnp.exp(s - m_new)
            l = l * alpha + jnp.sum(p, axis=0, keepdims=True)
            vt = vT_ref[0, c, j * HEAD_DIM:(j + 1) * HEAD_DIM, :]
            acc = acc * alpha + jnp.dot(vt, p.astype(cdt), preferred_element_type=F32)
            return m_new, l, acc

        init = (jnp.full((1, G_D * Q_BLOCK), NEG_BIG, F32), jnp.zeros((1, G_D * Q_BLOCK), F32),
                jnp.zeros((HEAD_DIM, G_D * Q_BLOCK), F32))
        m, l, acc = lax.fori_loop(0, nch, chunk_body, init)
        o = acc * (1.0 / l)
        for g in range(G_D):
            r0 = (G_D * j + g) * HEAD_DIM
            z = zT_ref[r0:r0 + HEAD_DIM, :]
            gate = z * (1.0 / (1.0 + jnp.exp(-z)))
            o_ref[r0:r0 + HEAD_DIM, :] = (o[:, g * Q_BLOCK:(g + 1) * Q_BLOCK] * gate).astype(o_ref.dtype)


def dsa_bias_windows(rel_bias, seq_len):
    o = jnp.arange(seq_len // Q_BLOCK)[:, None]
    m = jnp.arange(BIAS_WIN)[None, :]
    d = jnp.maximum(o * Q_BLOCK + m - DSA_KC, 0)
    return jnp.moveaxis(rel_bias[rel_bucket(d)].astype(F32), -1, 1)


def dsa_prompt_attend(qqiT, wiT, zT, ki4, k4, v4T, win, *, n_batch, seq_len, cdt=BF16):
    nqb = seq_len // Q_BLOCK
    nc = seq_len // DSA_KC
    topk = min(TOPK_MAX, seq_len // 4)
    idx_bits = int(math.log2(seq_len)) + 1
    tok = lambda b, q: (0, b * nqb + q)
    per_batch = lambda b, q: (b, 0, 0, 0)
    return pl.pallas_call(
        functools.partial(_dsa_prompt_kernel, topk=topk, idx_bits=idx_bits, cdt=cdt),
        grid=(n_batch, nqb),
        in_specs=[pl.BlockSpec((ATT_WIDTH, Q_BLOCK), tok),
                  pl.BlockSpec((IDX_HEADS * IDX_DIM, Q_BLOCK), lambda b, q: (1, b * nqb + q)),
                  pl.BlockSpec((IDX_HEADS, Q_BLOCK), tok),
                  pl.BlockSpec((ATT_WIDTH, Q_BLOCK), tok),
                  pl.BlockSpec((1, nc, DSA_KC, IDX_DIM), per_batch),
                  pl.BlockSpec((1, nc, DSA_KC, D_KV), per_batch),
                  pl.BlockSpec((1, nc, D_KV, DSA_KC), per_batch),
                  pl.BlockSpec((nqb, N_HEADS, BIAS_WIN), lambda b, q: (0, 0, 0))],
        out_specs=pl.BlockSpec((ATT_WIDTH, Q_BLOCK), tok),
        out_shape=jax.ShapeDtypeStruct((ATT_WIDTH, n_batch * seq_len), BF16),
        scratch_shapes=[pltpu.VMEM((nc, DSA_KC, Q_BLOCK), jnp.int32),
                        pltpu.VMEM((nc, DSA_KC, Q_BLOCK), F32)],
        compiler_params=pltpu.CompilerParams(
            dimension_semantics=("parallel", "arbitrary"), vmem_limit_bytes=VMEM_LIMIT),
        name="dsa_prompt_attend",
    )(qqiT, qqiT, wiT, zT, ki4, k4, v4T, win)


def _pad_cols(w, n):
    return jnp.pad(w, ((0, 0), (0, n - w.shape[1])))


def kernel(x_prompt, x_sample, cache_a_kv, state_s5, state_gdn, state_gdn_conv, cache_d_kv, cache_d_kidx,
           page_table, p_prompt, p_sample, rel_bias, ln_g, ln_b, ple_gate_w, ple_w,
           a_w_in, a_sinks, a_w_out,
           s5_w_in, s5_a_re, s5_a_im, s5_b_re, s5_b_im, s5_c_re, s5_c_im, s5_d, s5_log_dt, s5_w_glu, s5_w_out,
           gdn_w_in, gdn_conv_w, gdn_a_log, gdn_dt_bias, gdn_norm_w, gdn_w_out,
           dsa_w_in, dsa_w_out):
    past_len = page_table.shape[1] * PAGE_SIZE
    x = join_tokens(x_prompt, x_sample)
    x_bf = x.astype(BF16)
    outs = {}

    def post(i, x, h):
        p_bf = join_tokens(p_prompt[i], p_sample[i]).astype(BF16)
        return post_norm_ple(x, h, p_bf, ln_g[i], ln_b[i], ple_gate_w[i].astype(BF16), ple_w[i].astype(BF16))

    def finish_layer(i, x, gated, w_out):
        return post(i, x, matmul(gated.astype(BF16), w_out.astype(BF16)))

    proj = matmul(x_bf, a_w_in[0].astype(BF16))
    pp, ps = split_tokens(proj)
    gp, outs['a_p'] = swa_core(pp, None, 0, a_sinks[0], rel_bias)
    gs, outs['a_s'] = swa_core(ps, cache_a_kv[0], past_len, a_sinks[0], rel_bias)
    x, x_bf = finish_layer(0, x, join_tokens(gp, gs), a_w_out[0])

    proj = matmul(x_bf, s5_w_in[0].astype(BF16))
    tables = s5_tables(s5_a_re[0], s5_a_im[0], s5_b_re[0], s5_b_im[0], s5_c_re[0], s5_c_im[0], s5_log_dt[0])
    gated, outs['s5_p'], outs['s5_s'] = s5_layer(proj, state_s5[0], tables, s5_d[0], s5_w_glu[0],
                                                 n_batch=BATCH, seq_len=SEQ, n_dec=DEC_BATCH, dec_len=DEC_SEQ)
    x, x_bf = post(1, x, matmul(gated, s5_w_out[0].astype(BF16)))

    w_in = gdn_w_in[0]
    c_gz = GDN_CONV_CH + GDN_V_WIDTH
    qkvz = matmul(x_bf, w_in[:, :c_gz].astype(BF16))
    ab = matmul(x_bf, _pad_cols(w_in[:, c_gz:], LANE).astype(BF16))
    gp, gs, outs['gd_p'], outs['gd_s'], outs['gc_p'], outs['gc_s'] = gdn_layer(
        qkvz, ab, state_gdn[0], state_gdn_conv[0], gdn_conv_w[0], gdn_a_log[0], gdn_dt_bias[0], gdn_norm_w[0],
        n_batch=BATCH, seq_len=SEQ, n_dec=DEC_BATCH, dec_len=DEC_SEQ)
    w_out_bf = gdn_w_out[0].astype(BF16)
    x, x_bf = post(2, x, jnp.concatenate([matmul(gp, w_out_bf), matmul(gs.astype(BF16), w_out_bf)], axis=0))

    w_in = dsa_w_in[0]
    c_z = 2 * ATT_WIDTH + 2 * D_KV
    c_qi = c_z + IDX_HEADS * IDX_DIM
    c_kv = ATT_WIDTH + 2 * D_KV
    w_q, w_kv, w_z, w_qi = w_in[:, :ATT_WIDTH], w_in[:, ATT_WIDTH:c_kv], w_in[:, c_kv:c_z], w_in[:, c_z:c_qi]
    w_out_bf = dsa_w_out[0].astype(BF16)
    kv_nat = matmul(x_bf, w_kv.astype(BF16))
    kiw = matmul(x_bf, _pad_cols(w_in[:, c_qi:], 2 * LANE).astype(BF16))
    xT_bf = x_bf[:N_PROMPT_TOK].T
    qqiT = matmul(jnp.concatenate([w_q, w_qi], axis=1).T.astype(BF16), xT_bf, out_dtype=BF16)
    zT = matmul(w_z.T.astype(BF16), xT_bf)
    wiT = matmul(w_in[:, c_qi + IDX_DIM:].T.astype(BF16), xT_bf)
    nc = SEQ // DSA_KC
    kv_p = kv_nat[:N_PROMPT_TOK]
    v4T = jnp.swapaxes(kv_p[:, D_KV:].astype(BF16).reshape(BATCH, nc, DSA_KC, D_KV), 2, 3)
    gT = dsa_prompt_attend(qqiT, wiT, zT, kiw[:N_PROMPT_TOK].reshape(BATCH, nc, DSA_KC, 2 * LANE),
                           kv_p.reshape(BATCH, nc, DSA_KC, 2 * D_KV), v4T, dsa_bias_windows(rel_bias, SEQ),
                           n_batch=BATCH, seq_len=SEQ)
    h_p = matmul_ta(gT, w_out_bf)
    x_s = x_bf[N_PROMPT_TOK:]
    qzqi_s = matmul(x_s, jnp.concatenate([w_q, w_z, w_qi], axis=1).astype(BF16))
    s3 = lambda t: t.reshape(DEC_BATCH, DEC_SEQ, t.shape[-1])
    kiw_s = kiw[N_PROMPT_TOK:]
    gs = dsa_sample_core(s3(qzqi_s[:, :ATT_WIDTH]), s3(kv_nat[N_PROMPT_TOK:]), s3(qzqi_s[:, ATT_WIDTH:2 * ATT_WIDTH]),
                         s3(qzqi_s[:, 2 * ATT_WIDTH:]), s3(kiw_s[:, :IDX_DIM]),
                         s3(kiw_s[:, IDX_DIM:IDX_DIM + IDX_HEADS]), cache_d_kv, cache_d_kidx, 0, page_table, rel_bias)
    h_s = matmul(gs.reshape(N_SAMPLE_TOK, ATT_WIDTH).astype(BF16), w_out_bf)
    outs['dkv_p'] = kv_p.reshape(BATCH, SEQ, 2, KV_D, HEAD_DIM)
    outs['dkv_s'] = kv_nat[N_PROMPT_TOK:].reshape(DEC_BATCH, DEC_SEQ, 2, KV_D, HEAD_DIM)
    outs['dki_p'] = kiw[:N_PROMPT_TOK, :IDX_DIM].reshape(BATCH, SEQ, IDX_DIM)
    outs['dki_s'] = kiw_s[:, :IDX_DIM].reshape(DEC_BATCH, DEC_SEQ, IDX_DIM)
    x, x_bf = post(3, x, jnp.concatenate([h_p, h_s], axis=0))

    yp, ys = split_tokens(x)
    st = lambda name: outs[name][None]
    return (yp, ys, st('a_p'), st('a_s'), st('s5_p'), st('s5_s'), st('gd_p'), st('gd_s'),
            st('gc_p'), st('gc_s'), st('dkv_p'), st('dkv_s'), st('dki_p'), st('dki_s'))
```

```python
import functools
import math

import jax
import jax.numpy as jnp
from jax import lax
from jax.experimental import pallas as pl
from jax.experimental.pallas import tpu as pltpu

D_MODEL = 2048
BATCH = 4
SEQ = 2048
DEPTH = 4
DEC_BATCH = 32
DEC_SEQ = 4
PAGE_SIZE = 128
N_MIXERS = 4
PLE_DIM = 256
ALPHA = (2 * DEPTH) ** 0.25
LN_EPS = 1e-5
N_BUCKETS = 32
REL_MAX_DIST = 2048
N_HEADS = 32
HEAD_DIM = 64
ATT_WIDTH = N_HEADS * HEAD_DIM
WINDOW = 128
KV_A = 4
A_KV = KV_A * HEAD_DIM
KV_D = 8
D_KV = KV_D * HEAD_DIM
IDX_HEADS = 16
IDX_DIM = 128
TOPK_MAX = 256
Q_BLOCK = 128
S5_WIDTH = D_MODEL
S5_GROUP = 16
S5_GROUPS = S5_WIDTH // S5_GROUP
S5_STATE = 64
GDN_QK_HEADS = 16
GDN_V_HEADS = 32
GDN_DK = 128
GDN_DV = 128
GDN_CONV = 4
GDN_CHUNK = 64
GDN_QK_WIDTH = GDN_QK_HEADS * GDN_DK
GDN_V_WIDTH = GDN_V_HEADS * GDN_DV
GDN_CONV_CH = 2 * GDN_QK_WIDTH + GDN_V_WIDTH

F32 = jnp.float32
BF16 = jnp.bfloat16

N_PROMPT_TOK = BATCH * SEQ
N_SAMPLE_TOK = DEC_BATCH * DEC_SEQ
N_TOK = N_PROMPT_TOK + N_SAMPLE_TOK

V7X_VMEM_BYTES = 64 * 1024 * 1024
VMEM_LIMIT = 48 * 1024 * 1024
LANE = 128


def _mm_kernel(x_ref, w_ref, o_ref):
    o_ref[...] = jnp.dot(x_ref[...], w_ref[...], preferred_element_type=F32).astype(o_ref.dtype)


def _pick_tile(n, prefs):
    for t in prefs:
        if n % t == 0:
            return t
    raise ValueError(f"no tile for {n}")


def matmul(x, w, out_dtype=F32):
    m, k = x.shape
    n = w.shape[1]
    tm = _pick_tile(m, (640, 512, 320, 256, 128, 64, 32, 16, 8))
    tn = _pick_tile(n, (512, 384, 256, 128))
    return pl.pallas_call(
        _mm_kernel,
        grid=(m // tm, n // tn),
        in_specs=[pl.BlockSpec((tm, k), lambda i, j: (i, 0)),
                  pl.BlockSpec((k, tn), lambda i, j: (0, j))],
        out_specs=pl.BlockSpec((tm, tn), lambda i, j: (i, j)),
        out_shape=jax.ShapeDtypeStruct((m, n), out_dtype),
        compiler_params=pltpu.CompilerParams(
            dimension_semantics=("parallel", "parallel"), vmem_limit_bytes=VMEM_LIMIT),
        name="proj_matmul",
    )(x, w)


def _mm_ta_kernel(xt_ref, w_ref, o_ref):
    o_ref[...] = lax.dot_general(xt_ref[...], w_ref[...], (((0,), (0,)), ((), ())),
                                 preferred_element_type=F32).astype(o_ref.dtype)


def matmul_ta(xt, w, out_dtype=F32):
    k, m = xt.shape
    n = w.shape[1]
    tm = _pick_tile(m, (512, 256, 128))
    tn = _pick_tile(n, (512, 384, 256, 128))
    return pl.pallas_call(
        _mm_ta_kernel,
        grid=(m // tm, n // tn),
        in_specs=[pl.BlockSpec((k, tm), lambda i, j: (0, i)),
                  pl.BlockSpec((k, tn), lambda i, j: (0, j))],
        out_specs=pl.BlockSpec((tm, tn), lambda i, j: (i, j)),
        out_shape=jax.ShapeDtypeStruct((m, n), out_dtype),
        compiler_params=pltpu.CompilerParams(
            dimension_semantics=("parallel", "parallel"), vmem_limit_bytes=VMEM_LIMIT),
        name="proj_matmul_ta",
    )(xt, w)


POST_TM = 320
POST_TN = 512


def _post_kernel(x_ref, h_ref, p_ref, g_ref, b_ref, wg_ref, wp_ref, o_ref, obf_ref, y_sc, ybf_sc):
    j = pl.program_id(1)

    @pl.when(j == 0)
    def _():
        t = ALPHA * x_ref[...] + h_ref[...]
        mu = jnp.mean(t, axis=-1, keepdims=True)
        d = t - mu
        var = jnp.mean(d * d, axis=-1, keepdims=True)
        y = d * lax.rsqrt(var + LN_EPS) * g_ref[...] + b_ref[...]
        ybf_sc[...] = y.astype(BF16)
        for jj in range(D_MODEL // POST_TN):
            y_sc[jj] = y[:, jj * POST_TN:(jj + 1) * POST_TN]

    gate = jnp.dot(ybf_sc[...], wg_ref[...], preferred_element_type=F32)
    ple = jnp.dot(p_ref[...], wp_ref[...], preferred_element_type=F32)
    o = y_sc[j] + (1.0 / (1.0 + jnp.exp(-gate))) * ple
    o_ref[...] = o
    obf_ref[...] = o.astype(BF16)


def post_norm_ple(x, h, p_bf, g, b, wg_bf, wp_bf):
    m = x.shape[0]
    tm, tn = POST_TM, POST_TN
    return pl.pallas_call(
        _post_kernel,
        grid=(m // tm, D_MODEL // tn),
        in_specs=[pl.BlockSpec((tm, D_MODEL), lambda i, j: (i, 0)),
                  pl.BlockSpec((tm, D_MODEL), lambda i, j: (i, 0)),
                  pl.BlockSpec((tm, PLE_DIM), lambda i, j: (i, 0)),
                  pl.BlockSpec((1, D_MODEL), lambda i, j: (0, 0)),
                  pl.BlockSpec((1, D_MODEL), lambda i, j: (0, 0)),
                  pl.BlockSpec((D_MODEL, tn), lambda i, j: (0, j)),
                  pl.BlockSpec((PLE_DIM, tn), lambda i, j: (0, j))],
        out_specs=[pl.BlockSpec((tm, tn), lambda i, j: (i, j)),
                   pl.BlockSpec((tm, tn), lambda i, j: (i, j))],
        out_shape=[jax.ShapeDtypeStruct((m, D_MODEL), F32),
                   jax.ShapeDtypeStruct((m, D_MODEL), BF16)],
        scratch_shapes=[pltpu.VMEM((D_MODEL // tn, tm, tn), F32),
                        pltpu.VMEM((tm, D_MODEL), BF16)],
        compiler_params=pltpu.CompilerParams(
            dimension_semantics=("parallel", "arbitrary"), vmem_limit_bytes=VMEM_LIMIT),
        name="post_norm_ple",
    )(x, h, p_bf, g.reshape(1, D_MODEL), b.reshape(1, D_MODEL), wg_bf, wp_bf)


def rel_bucket(dist):
    n = jnp.maximum(dist, 0)
    exact = N_BUCKETS // 2
    logb = exact + (jnp.log(jnp.maximum(n, exact).astype(F32) / exact)
                    / math.log(REL_MAX_DIST / exact) * (N_BUCKETS - exact)).astype(jnp.int32)
    return jnp.where(n < exact, n, jnp.minimum(logb, N_BUCKETS - 1))


def head_bias(rel_bias, dist, n_kv, g):
    b = jnp.moveaxis(rel_bias[rel_bucket(dist)].astype(F32), -1, -3)
    return b.reshape(b.shape[:-3] + (n_kv, g) + b.shape[-2:])


def masked_softmax(logits, mask, sink=None):
    logits = jnp.where(mask, logits, -jnp.inf)
    m = jnp.max(logits, axis=-1, keepdims=True)
    if sink is not None:
        m = jnp.maximum(m, sink)
    e = jnp.exp(logits - m)
    den = jnp.sum(e, axis=-1, keepdims=True)
    if sink is not None:
        den = den + jnp.exp(sink - m)
    return e / den


def take_rows(rows, idx):
    return jax.vmap(lambda r, i: r[i])(rows, idx)


def split_tokens(t):
    c = t.shape[-1]
    return (t[:N_PROMPT_TOK].reshape(BATCH, SEQ, c), t[N_PROMPT_TOK:].reshape(DEC_BATCH, DEC_SEQ, c))


def join_tokens(tp, ts):
    c = tp.shape[-1]
    return jnp.concatenate([tp.reshape(N_PROMPT_TOK, c), ts.reshape(N_SAMPLE_TOK, c)], axis=0)


def window_attend(q, k, v, qpos, kpos, sinks, rel_bias):
    n_kv, g = q.shape[-3], q.shape[-2]
    dist = qpos[..., :, None] - kpos[..., None, :]
    mask = (dist >= 0) & (dist < WINDOW) & (kpos[..., None, :] >= 0)
    logits = jnp.einsum('...qhgd,...khd->...hgqk', q, k).astype(F32) * (HEAD_DIM ** -0.5)
    logits = logits + head_bias(rel_bias, dist, n_kv, g)
    sink = sinks.astype(F32).reshape(n_kv, g, 1, 1)
    p = masked_softmax(logits, mask[..., None, None, :, :], sink)
    return jnp.einsum('...hgqk,...khd->...qhgd', p.astype(v.dtype), v)


def swa_core(proj, kv_cache, start, sinks, rel_bias):
    Bn, L, _ = proj.shape
    G = N_HEADS // KV_A
    q, k, v, z = jnp.split(proj, [ATT_WIDTH, ATT_WIDTH + A_KV, ATT_WIDTH + 2 * A_KV], axis=-1)
    q = q.reshape(Bn, L, KV_A, G, HEAD_DIM)
    k = k.reshape(Bn, L, KV_A, HEAD_DIM)
    v = v.reshape(Bn, L, KV_A, HEAD_DIM)
    if kv_cache is None:
        nb = L // WINDOW
        qb = q.reshape(Bn, nb, WINDOW, KV_A, G, HEAD_DIM)
        kb = k.reshape(Bn, nb, WINDOW, KV_A, HEAD_DIM)
        vb = v.reshape(Bn, nb, WINDOW, KV_A, HEAD_DIM)
        prev = lambda t: jnp.concatenate([jnp.zeros_like(t[:, :1]), t[:, :-1]], axis=1)
        kk = jnp.concatenate([prev(kb), kb], axis=2)
        vv = jnp.concatenate([prev(vb), vb], axis=2)
        qpos = jnp.arange(L).reshape(nb, WINDOW)
        kpos = jnp.concatenate([qpos - WINDOW, qpos], axis=1)
        o = window_attend(qb, kk, vv, qpos, kpos, sinks, rel_bias)
        new_kv = jnp.stack([k[:, L - WINDOW:], v[:, L - WINDOW:]], axis=2)
    else:
        kk = jnp.concatenate([kv_cache[:, :, 0], k], axis=1)
        vv = jnp.concatenate([kv_cache[:, :, 1], v], axis=1)
        qpos = start + jnp.arange(L)
        kpos = start - WINDOW + jnp.arange(WINDOW + L)
        o = window_attend(q, kk, vv, qpos, kpos, sinks, rel_bias)
        new_kv = jnp.stack([kk[:, -WINDOW:], vv[:, -WINDOW:]], axis=2)
    o = o.reshape(Bn, L, ATT_WIDTH)
    return o * jax.nn.silu(z), new_kv


def _linear_combine(l, r):
    return (l[0] * r[0], r[0] * l[1] + r[1])


def s5_core(proj, h0, a_re, a_im, b_re, b_im, c_re, c_im, d_skip, log_dt, w_glu_bf):
    Bn, L, _ = proj.shape
    u, z = jnp.split(proj, 2, axis=-1)
    uf = u.reshape(Bn, L, S5_GROUPS, S5_GROUP)
    a = lax.complex(a_re, a_im)
    dt = jnp.exp(log_dt)[:, None]
    a_bar = jnp.exp(a * dt)
    b_bar = ((a_bar - 1.0) / a)[..., None] * lax.complex(b_re, b_im)
    c = lax.complex(c_re, c_im)
    bu = jnp.einsum('gpc,blgc->blgp', b_bar, uf.astype(jnp.complex64))
    if h0 is not None:
        h0c = lax.complex(h0[..., 0], h0[..., 1])
        bu = bu.at[:, 0].add(a_bar * h0c)
    a_seq = jnp.broadcast_to(a_bar, bu.shape)
    _, h = lax.associative_scan(_linear_combine, (a_seq, bu), axis=1)
    y = jnp.einsum('gcp,blgp->blgc', c, h).real + d_skip.reshape(S5_GROUPS, S5_GROUP) * uf
    y = jax.nn.gelu(y.reshape(Bn, L, S5_WIDTH))
    glu = matmul(y.reshape(Bn * L, S5_WIDTH).astype(BF16), w_glu_bf).reshape(Bn, L, S5_WIDTH)
    y = y * jax.nn.sigmoid(glu)
    h_last = h[:, -1]
    return y * jax.nn.silu(z), jnp.stack([h_last.real, h_last.imag], axis=-1)


S5_SLAB_G = 8
S5_SLAB_CH = S5_SLAB_G * S5_GROUP
S5_SLAB_ST = S5_SLAB_G * S5_STATE
S5_N_SLABS = S5_GROUPS // S5_SLAB_G
S5_CHAINS = 8
S5_HALF_CH = S5_CHAINS * S5_SLAB_CH
S5_T = 256
S5_LT = 2 * S5_SLAB_ST // LANE


def _gelu_tanh(x):
    return 0.5 * x * (1.0 + jnp.tanh(math.sqrt(2.0 / math.pi) * (x + 0.044715 * (x * x * x))))


def s5_tables(a_re, a_im, b_re, b_im, c_re, c_im, log_dt):
    a = lax.complex(a_re, a_im)
    dt = jnp.exp(log_dt)[:, None]
    a_bar = jnp.exp(a * dt)
    b_bar = ((a_bar - 1.0) / a)[..., None] * lax.complex(b_re, b_im)
    eye = jnp.eye(S5_SLAB_G, dtype=F32)

    def b_blk(t):
        t = t.reshape(S5_N_SLABS, S5_SLAB_G, S5_STATE, S5_GROUP)
        return jnp.einsum('ij,sipc->sicjp', eye, t).reshape(S5_N_SLABS, S5_SLAB_CH, S5_SLAB_ST)

    def c_blk(t):
        t = t.reshape(S5_N_SLABS, S5_SLAB_G, S5_GROUP, S5_STATE)
        return jnp.einsum('ij,sicp->sjpic', eye, t).reshape(S5_N_SLABS, S5_SLAB_ST, S5_SLAB_CH)

    bcat = jnp.concatenate([b_blk(b_bar.real), b_blk(b_bar.imag)], axis=2)
    ccat = jnp.concatenate([c_blk(c_re), -c_blk(c_im)], axis=1)
    a_cat = jnp.concatenate([a_bar.real.reshape(S5_N_SLABS, S5_SLAB_ST),
                             a_bar.imag.reshape(S5_N_SLABS, S5_SLAB_ST)], axis=1)
    return a_cat, bcat, ccat


def _s5_prompt_kernel(u_ref, bcat_ref, ccat_ref, a_ref, d_ref, y_ref, hout_ref, sc, h_sc, *, cdt):
    tc = pl.program_id(2)
    n_lt_half = S5_LT // 2

    @pl.when(tc == 0)
    def _():
        h_sc[...] = jnp.zeros_like(h_sc)

    for j in range(S5_CHAINS):
        uj = u_ref[:, j * S5_SLAB_CH:(j + 1) * S5_SLAB_CH].astype(cdt)
        bu = jnp.dot(uj, bcat_ref[0, j], preferred_element_type=F32)
        for lt in range(S5_LT):
            sc[lt, pl.ds(j, S5_T, stride=S5_CHAINS), :] = bu[:, lt * LANE:(lt + 1) * LANE]

    a_re = [a_ref[0, :, lt * LANE:(lt + 1) * LANE] for lt in range(n_lt_half)]
    a_im = [a_ref[0, :, (n_lt_half + lt) * LANE:(n_lt_half + lt + 1) * LANE] for lt in range(n_lt_half)]

    def step(t, h):
        r0 = pl.multiple_of(t * S5_CHAINS, S5_CHAINS)
        new = list(h)
        for lt in range(n_lt_half):
            hr, hi = h[lt], h[n_lt_half + lt]
            nr = a_re[lt] * hr - a_im[lt] * hi + sc[lt, pl.ds(r0, S5_CHAINS), :]
            ni = a_re[lt] * hi + a_im[lt] * hr + sc[n_lt_half + lt, pl.ds(r0, S5_CHAINS), :]
            sc[lt, pl.ds(r0, S5_CHAINS), :] = nr
            sc[n_lt_half + lt, pl.ds(r0, S5_CHAINS), :] = ni
            new[lt], new[n_lt_half + lt] = nr, ni
        return tuple(new)

    h = lax.fori_loop(0, S5_T, step, tuple(h_sc[lt] for lt in range(S5_LT)), unroll=8)
    for lt in range(S5_LT):
        h_sc[lt] = h[lt]
        hout_ref[0, 0, :, lt * LANE:(lt + 1) * LANE] = h[lt]

    for j in range(S5_CHAINS):
        hcat = jnp.concatenate([sc[lt, pl.ds(j, S5_T, stride=S5_CHAINS), :] for lt in range(S5_LT)], axis=1)
        cols = slice(j * S5_SLAB_CH, (j + 1) * S5_SLAB_CH)
        y = jnp.dot(hcat.astype(cdt), ccat_ref[0, j], preferred_element_type=F32) + d_ref[0, :, cols] * u_ref[:, cols]
        y_ref[:, cols] = _gelu_tanh(y)


def s5_prompt(proj, a_cat, bcat, ccat, d_skip, *, n_batch, seq_len, n_rows_out, cdt=BF16):
    n_t = seq_len // S5_T
    n_half = S5_WIDTH // S5_HALF_CH
    half = lambda t: t.reshape((n_half, S5_CHAINS) + t.shape[1:])
    return pl.pallas_call(
        functools.partial(_s5_prompt_kernel, cdt=cdt),
        grid=(n_batch, n_half, n_t),
        in_specs=[pl.BlockSpec((S5_T, S5_HALF_CH), lambda b, hf, t: (b * n_t + t, hf)),
                  pl.BlockSpec((1, S5_CHAINS, S5_SLAB_CH, 2 * S5_SLAB_ST), lambda b, hf, t: (hf, 0, 0, 0)),
                  pl.BlockSpec((1, S5_CHAINS, 2 * S5_SLAB_ST, S5_SLAB_CH), lambda b, hf, t: (hf, 0, 0, 0)),
                  pl.BlockSpec((1, S5_CHAINS, 2 * S5_SLAB_ST), lambda b, hf, t: (hf, 0, 0)),
                  pl.BlockSpec((1, 1, S5_HALF_CH), lambda b, hf, t: (hf, 0, 0))],
        out_specs=[pl.BlockSpec((S5_T, S5_HALF_CH), lambda b, hf, t: (b * n_t + t, hf)),
                   pl.BlockSpec((1, 1, S5_CHAINS, 2 * S5_SLAB_ST), lambda b, hf, t: (b, hf, 0, 0))],
        out_shape=[jax.ShapeDtypeStruct((n_rows_out, S5_WIDTH), F32),
                   jax.ShapeDtypeStruct((n_batch, n_half, S5_CHAINS, 2 * S5_SLAB_ST), F32)],
        scratch_shapes=[pltpu.VMEM((S5_LT, S5_T * S5_CHAINS, LANE), F32),
                        pltpu.VMEM((S5_LT, S5_CHAINS, LANE), F32)],
        compiler_params=pltpu.CompilerParams(
            dimension_semantics=("parallel", "parallel", "arbitrary"), vmem_limit_bytes=VMEM_LIMIT),
        name="s5_prompt",
    )(proj, half(bcat.astype(cdt)), half(ccat.astype(cdt)), half(a_cat), d_skip.reshape(n_half, 1, S5_HALF_CH))


def _s5_sample_kernel(u_ref, bcat_ref, ccat_ref, a_ref, d_ref, h0_ref, yin_ref, y_ref, hout_ref, sc, *,
                      n_b, n_t, cdt):
    del yin_ref
    u = u_ref[...]
    bu = jnp.dot(u.astype(cdt), bcat_ref[0], preferred_element_type=F32)
    a_re = a_ref[0, :, :S5_SLAB_ST]
    a_im = a_ref[0, :, S5_SLAB_ST:]
    for bg in range(n_b // 8):
        hr = h0_ref[0, bg * 8:(bg + 1) * 8, :S5_SLAB_ST]
        hi = h0_ref[0, bg * 8:(bg + 1) * 8, S5_SLAB_ST:]
        for t in range(n_t):
            r = t * n_b + bg * 8
            hr, hi = (a_re * hr - a_im * hi + bu[r:r + 8, :S5_SLAB_ST],
                      a_re * hi + a_im * hr + bu[r:r + 8, S5_SLAB_ST:])
            sc[r:r + 8, :S5_SLAB_ST] = hr
            sc[r:r + 8, S5_SLAB_ST:] = hi
        hout_ref[0, bg * 8:(bg + 1) * 8, :S5_SLAB_ST] = hr
        hout_ref[0, bg * 8:(bg + 1) * 8, S5_SLAB_ST:] = hi
    y = jnp.dot(sc[...].astype(cdt), ccat_ref[0], preferred_element_type=F32) + d_ref[0] * u
    y_ref[...] = _gelu_tanh(y)


def s5_sample(u_tb, a_cat, bcat, ccat, d_skip, h0_cat, y_all, *, n_b, n_t, row_block, cdt=BF16):
    rows = n_t * n_b
    return pl.pallas_call(
        functools.partial(_s5_sample_kernel, n_b=n_b, n_t=n_t, cdt=cdt),
        grid=(S5_N_SLABS,),
        in_specs=[pl.BlockSpec((rows, S5_SLAB_CH), lambda s: (0, s)),
                  pl.BlockSpec((1, S5_SLAB_CH, 2 * S5_SLAB_ST), lambda s: (s, 0, 0)),
                  pl.BlockSpec((1, 2 * S5_SLAB_ST, S5_SLAB_CH), lambda s: (s, 0, 0)),
                  pl.BlockSpec((1, 1, 2 * S5_SLAB_ST), lambda s: (s, 0, 0)),
                  pl.BlockSpec((1, 1, S5_SLAB_CH), lambda s: (s, 0, 0)),
                  pl.BlockSpec((1, n_b, 2 * S5_SLAB_ST), lambda s: (s, 0, 0)),
                  pl.BlockSpec(memory_space=pl.ANY)],
        out_specs=[pl.BlockSpec((rows, S5_SLAB_CH), lambda s: (row_block, s)),
                   pl.BlockSpec((1, n_b, 2 * S5_SLAB_ST), lambda s: (s, 0, 0))],
        out_shape=[jax.ShapeDtypeStruct(y_all.shape, F32),
                   jax.ShapeDtypeStruct((S5_N_SLABS, n_b, 2 * S5_SLAB_ST), F32)],
        scratch_shapes=[pltpu.VMEM((rows, 2 * S5_SLAB_ST), F32)],
        input_output_aliases={6: 0},
        compiler_params=pltpu.CompilerParams(
            dimension_semantics=("arbitrary",), vmem_limit_bytes=VMEM_LIMIT),
        name="s5_sample",
    )(u_tb, bcat.astype(cdt), ccat.astype(cdt), a_cat.reshape(S5_N_SLABS, 1, 2 * S5_SLAB_ST),
      d_skip.reshape(S5_N_SLABS, 1, S5_SLAB_CH), h0_cat, y_all)


GLU_TM = 320
GLU_TN = 512


def _glu_kernel(yfull_ref, w_ref, ycol_ref, z_ref, o_ref, ybf_sc):
    @pl.when(pl.program_id(1) == 0)
    def _():
        ybf_sc[...] = yfull_ref[...].astype(ybf_sc.dtype)

    glu = jnp.dot(ybf_sc[...], w_ref[...], preferred_element_type=F32)
    z = z_ref[...]
    y = ycol_ref[...]
    o_ref[...] = (y * (1.0 / (1.0 + jnp.exp(-glu))) * (z * (1.0 / (1.0 + jnp.exp(-z))))).astype(o_ref.dtype)


def s5_glu_gate(y, w_glu, proj, cdt=BF16):
    m = y.shape[0]
    tm, tn = _pick_tile(m, (GLU_TM, 256, 128, 64, 32, 16)), GLU_TN
    z_off = S5_WIDTH // tn
    return pl.pallas_call(
        _glu_kernel,
        grid=(m // tm, S5_WIDTH // tn),
        in_specs=[pl.BlockSpec((tm, S5_WIDTH), lambda i, j: (i, 0)),
                  pl.BlockSpec((S5_WIDTH, tn), lambda i, j: (0, j)),
                  pl.BlockSpec((tm, tn), lambda i, j: (i, j)),
                  pl.BlockSpec((tm, tn), lambda i, j: (i, z_off + j))],
        out_specs=pl.BlockSpec((tm, tn), lambda i, j: (i, j)),
        out_shape=jax.ShapeDtypeStruct((m, S5_WIDTH), BF16),
        scratch_shapes=[pltpu.VMEM((tm, S5_WIDTH), cdt)],
        compiler_params=pltpu.CompilerParams(
            dimension_semantics=("parallel", "arbitrary"), vmem_limit_bytes=VMEM_LIMIT),
        name="s5_glu_gate",
    )(y, w_glu.astype(cdt), y, proj)


def s5_layer(proj, state_in, tables, d_skip, w_glu, *, n_batch, seq_len, n_dec, dec_len, cdt=BF16):
    a_cat, bcat, ccat = tables
    n_p = n_batch * seq_len
    n_s = n_dec * dec_len
    y_all, h_p = s5_prompt(proj, a_cat, bcat, ccat, d_skip, n_batch=n_batch, seq_len=seq_len,
                           n_rows_out=n_p + n_s, cdt=cdt)
    u_tb = jnp.swapaxes(proj[n_p:, :S5_WIDTH].reshape(n_dec, dec_len, S5_WIDTH), 0, 1).reshape(n_s, S5_WIDTH)
    h0 = state_in.reshape(n_dec, S5_N_SLABS, S5_SLAB_ST, 2)
    h0_cat = jnp.concatenate([jnp.swapaxes(h0[..., 0], 0, 1), jnp.swapaxes(h0[..., 1], 0, 1)], axis=-1)
    y_all, h_s = s5_sample(u_tb, a_cat, bcat, ccat, d_skip, h0_cat, y_all, n_b=n_dec, n_t=dec_len,
                           row_block=n_p // n_s, cdt=cdt)
    y_s = jnp.swapaxes(y_all[n_p:].reshape(dec_len, n_dec, S5_WIDTH), 0, 1).reshape(n_s, S5_WIDTH)
    y_all = lax.dynamic_update_slice(y_all, y_s, (n_p, 0))
    gated = s5_glu_gate(y_all, w_glu, proj, cdt=cdt)
    hp = h_p.reshape(n_batch, S5_N_SLABS, 2, S5_SLAB_ST)
    st_p = jnp.stack([hp[:, :, 0], hp[:, :, 1]], axis=-1).reshape(n_batch, S5_GROUPS, S5_STATE, 2)
    hs = jnp.swapaxes(h_s, 0, 1).reshape(n_dec, S5_N_SLABS, 2, S5_SLAB_ST)
    st_s = jnp.stack([hs[:, :, 0], hs[:, :, 1]], axis=-1).reshape(n_dec, S5_GROUPS, S5_STATE, 2)
    return gated, st_p, st_s


def l2_normalize(t, eps=1e-6):
    return t * lax.rsqrt(jnp.sum(t * t, axis=-1, keepdims=True) + eps)


def chunk_gated_delta(q, k, v, g, beta, S0):
    Bn, L, H, dk = k.shape
    dv = v.shape[-1]
    C = min(GDN_CHUNK, L)
    n = -(-L // C)
    pad = n * C - L

    def chunks(t):
        t = jnp.pad(t, [(0, 0), (0, pad)] + [(0, 0)] * (t.ndim - 2))
        t = t.reshape((Bn, n, C) + t.shape[2:])
        return jnp.moveaxis(t, 3, 2)

    qc, kc, vc, gc, bc = [chunks(t) for t in (q, k, v, g, beta)]
    gam = jnp.cumsum(gc, axis=-1)
    pos = jnp.arange(C)
    causal = pos[:, None] >= pos[None, :]
    strict = pos[:, None] > pos[None, :]
    decay = jnp.exp(jnp.where(causal, gam[..., :, None] - gam[..., None, :], -jnp.inf))
    kk = jnp.einsum('bnhid,bnhjd->bnhij', kc, kc)
    tri = jnp.eye(C, dtype=F32) + jnp.where(strict, bc[..., :, None] * kk * decay, 0.0)
    rhs = jnp.concatenate([bc[..., None] * vc, (bc * jnp.exp(gam))[..., None] * kc], axis=-1)
    sol = lax.linalg.triangular_solve(tri, rhs, left_side=True, lower=True, unit_diagonal=True)
    u, w = sol[..., :dv], sol[..., dv:]
    qk = jnp.einsum('bnhid,bnhjd->bnhij', qc, kc) * decay
    q_dec = qc * jnp.exp(gam)[..., None]
    k_dec = kc * jnp.exp(gam[..., -1:] - gam)[..., None]
    g_tot = jnp.exp(gam[..., -1])

    def step(S, xs):
        u_c, w_c, qk_c, qd_c, kd_c, gt_c = xs
        v_new = u_c - jnp.einsum('bhcd,bhde->bhce', w_c, S)
        o = jnp.einsum('bhcd,bhde->bhce', qd_c, S) + jnp.einsum('bhij,bhje->bhie', qk_c, v_new)
        S = S * gt_c[..., None, None] + jnp.einsum('bhcd,bhce->bhde', kd_c, v_new)
        return S, o

    xs = tuple(jnp.moveaxis(t, 1, 0) for t in (u, w, qk, q_dec, k_dec, g_tot))
    S, o = lax.scan(step, S0, xs)
    o = o.transpose(1, 0, 3, 2, 4).reshape(Bn, n * C, H, dv)[:, :L]
    return o, S


def gdn_core(qkv, z, ab, S0, conv_buf, conv_w, a_log, dt_bias, norm_w):
    Bn, L, _ = qkv.shape
    a, b = ab[..., :GDN_V_HEADS], ab[..., GDN_V_HEADS:2 * GDN_V_HEADS]
    if conv_buf is None:
        conv_buf = jnp.zeros((Bn, GDN_CONV - 1, GDN_CONV_CH), qkv.dtype)
    xx = jnp.concatenate([conv_buf, qkv], axis=1)
    conv = jax.nn.silu(sum(xx[:, j:j + L] * conv_w[j] for j in range(GDN_CONV)))
    new_buf = xx[:, L:]
    q, k, v = jnp.split(conv, [GDN_QK_WIDTH, 2 * GDN_QK_WIDTH], axis=-1)
    rep = GDN_V_HEADS // GDN_QK_HEADS
    q = jnp.repeat(l2_normalize(q.reshape(Bn, L, GDN_QK_HEADS, GDN_DK)), rep, axis=2) * (GDN_DK ** -0.5)
    k = jnp.repeat(l2_normalize(k.reshape(Bn, L, GDN_QK_HEADS, GDN_DK)), rep, axis=2)
    v = v.reshape(Bn, L, GDN_V_HEADS, GDN_DV)
    beta = jax.nn.sigmoid(b)
    g = -jnp.exp(a_log) * jax.nn.softplus(a + dt_bias)
    if S0 is None:
        S0 = jnp.zeros((Bn, GDN_V_HEADS, GDN_DK, GDN_DV), F32)
    o, S = chunk_gated_delta(q, k, v, g, beta, S0)
    of = o * lax.rsqrt(jnp.mean(o * o, axis=-1, keepdims=True) + 1e-6) * norm_w
    of = of * jax.nn.silu(z.reshape(Bn, L, GDN_V_HEADS, GDN_DV))
    return of.reshape(Bn, L, GDN_V_WIDTH), S, new_buf


GDN_CONV_TT = 256
GDN_CONV_CW = 1024
GDN_HIST = 8
GDN_HB = 4
GDN_TT = 256
GDN_SAMPLE_ROWS = 8


def _gdn_conv_kernel(x_ref, hist_ref, w_ref, o_ref, *, rows, n_t, zero_first):
    i, j = pl.program_id(0), pl.program_id(1)
    hist = hist_ref[...]
    if zero_first:
        hist = jnp.where(i % n_t == 0, 0.0, hist)
    ext = jnp.concatenate([hist, x_ref[...]], axis=0)
    acc = ext[GDN_HIST:GDN_HIST + rows] * w_ref[GDN_CONV - 1:GDN_CONV, :]
    for s in range(1, GDN_CONV):
        acc = acc + ext[GDN_HIST - s:GDN_HIST - s + rows] * w_ref[GDN_CONV - 1 - s:GDN_CONV - s, :]
    conv = acc * (1.0 / (1.0 + jnp.exp(-acc)))
    n_qk_blocks = 2 * GDN_QK_WIDTH // GDN_CONV_CW

    @pl.when(j >= n_qk_blocks)
    def _():
        o_ref[...] = conv

    @pl.when(j < n_qk_blocks)
    def _():
        scale = jnp.where(j < GDN_QK_WIDTH // GDN_CONV_CW, GDN_DK ** -0.5, 1.0)
        for h in range(GDN_CONV_CW // GDN_DK):
            t = conv[:, h * GDN_DK:(h + 1) * GDN_DK]
            n = t * lax.rsqrt(jnp.sum(t * t, axis=-1, keepdims=True) + 1e-6)
            o_ref[:, h * GDN_DK:(h + 1) * GDN_DK] = n * scale


def gdn_conv(x, hist_src, conv_w, *, rows, n_blocks, n_t, data_map, hist_map, zero_first):
    n_out = n_blocks * rows
    return pl.pallas_call(
        functools.partial(_gdn_conv_kernel, rows=rows, n_t=n_t, zero_first=zero_first),
        grid=(n_blocks, GDN_CONV_CH // GDN_CONV_CW),
        in_specs=[pl.BlockSpec((rows, GDN_CONV_CW), lambda i, j: (data_map(i), j)),
                  pl.BlockSpec((GDN_HIST, GDN_CONV_CW), lambda i, j: (hist_map(i), j)),
                  pl.BlockSpec((GDN_CONV, GDN_CONV_CW), lambda i, j: (0, j))],
        out_specs=pl.BlockSpec((rows, GDN_CONV_CW), lambda i, j: (i, j)),
        out_shape=jax.ShapeDtypeStruct((n_out, GDN_CONV_CH), F32),
        compiler_params=pltpu.CompilerParams(
            dimension_semantics=("parallel", "parallel"), vmem_limit_bytes=VMEM_LIMIT),
        name="gdn_conv",
    )(x, hist_src, conv_w)


def _gdn_chunk_kernel(q_ref, k_ref, v_ref, z_ref, ab_ref, alog_ref, dtb_ref, nw_ref, s0_ref, o_ref, sout_ref,
                      s_sc, *, chunk, n_inner, n_tt, valid_len):
    C = chunk
    hb, tt = pl.program_id(1), pl.program_id(2)

    @pl.when(tt == 0)
    def _():
        s_sc[...] = s0_ref[0]

    rowi = lax.broadcasted_iota(jnp.int32, (C, C), 0)
    coli = lax.broadcasted_iota(jnp.int32, (C, C), 1)
    causal = rowi >= coli
    strict = rowi > coli
    ltri = jnp.where(causal, 1.0, 0.0)
    utri = jnp.where(rowi <= coli, 1.0, 0.0)
    eye = jnp.where(rowi == coli, 1.0, 0.0)
    hi = lax.Precision.HIGHEST
    shift = (LANE - hb * GDN_HB) % LANE
    alog = pltpu.roll(jnp.broadcast_to(alog_ref[...], (8, LANE)), shift, 1)[0:1]
    dtb = pltpu.roll(jnp.broadcast_to(dtb_ref[...], (8, LANE)), shift, 1)[0:1]
    nw = nw_ref[...]
    tok_valid = lax.broadcasted_iota(jnp.int32, (C, LANE), 0) < valid_len

    def chunk_body(c, carry):
        r0 = pl.multiple_of(c * C, C)
        ab = pltpu.roll(ab_ref[pl.ds(r0, C), :], shift, 1)
        xa = ab + dtb
        softplus = jnp.maximum(xa, 0.0) + jnp.log1p(jnp.exp(-jnp.abs(xa)))
        g_all = jnp.where(tok_valid, -jnp.exp(alog) * softplus, 0.0)
        beta_all = jnp.where(tok_valid, 1.0 / (1.0 + jnp.exp(-ab)), 0.0)
        gam_all = jnp.dot(ltri, g_all, preferred_element_type=F32, precision=hi)
        gamT_all = lax.dot_general(g_all, utri, (((0,), (0,)), ((), ())),
                                   preferred_element_type=F32, precision=hi)
        for i in range(GDN_HB):
            qk_cols = slice((i // 2) * GDN_DK, (i // 2 + 1) * GDN_DK)
            v_cols = slice(i * GDN_DV, (i + 1) * GDN_DV)
            qh = q_ref[pl.ds(r0, C), qk_cols]
            kh = k_ref[pl.ds(r0, C), qk_cols]
            vh = v_ref[pl.ds(r0, C), v_cols]
            gam_c = jnp.broadcast_to(gam_all[:, i:i + 1], (C, LANE))
            gam_r = jnp.broadcast_to(gamT_all[i:i + 1, :], (C, C))
            beta_c = jnp.broadcast_to(beta_all[:, 32 + i:33 + i], (C, LANE))
            gam_last = jnp.broadcast_to(gam_all[C - 1:C, i:i + 1], (1, LANE))
            decay = jnp.where(causal, jnp.exp(jnp.where(causal, gam_c[:, :C] - gam_r, 0.0)), 0.0)
            kb = kh.astype(BF16)
            qkk = lax.dot_general(jnp.concatenate([qh, kh], axis=0).astype(BF16), kb, (((1,), (1,)), ((), ())),
                                  preferred_element_type=F32)
            qk = qkk[:C] * decay
            neg_a = jnp.where(strict, -(beta_c[:, :C] * qkk[C:] * decay), 0.0)
            p_inv = eye + neg_a
            m_pow = neg_a
            for _ in range(int(math.log2(C)) - 1):
                m_bf = m_pow.astype(BF16)
                m_pow = jnp.dot(m_bf, m_bf, preferred_element_type=F32)
                p_inv = p_inv + jnp.dot(p_inv.astype(BF16), m_pow.astype(BF16), preferred_element_type=F32)
            eg = jnp.exp(gam_c)
            rhs = jnp.concatenate([beta_c * vh, (beta_c * eg) * kh], axis=1)
            sol = jnp.dot(p_inv.astype(BF16), rhs.astype(BF16), preferred_element_type=F32)
            u, w = sol[:, :GDN_DV], sol[:, GDN_DV:]
            s_old = s_sc[i]
            s_bf = s_old.astype(BF16)
            ws = jnp.dot(jnp.concatenate([w, qh * eg], axis=0).astype(BF16), s_bf, preferred_element_type=F32)
            v_new = u - ws[:C]
            vn_bf = v_new.astype(BF16)
            o = ws[C:] + jnp.dot(qk.astype(BF16), vn_bf, preferred_element_type=F32)
            k_dec = kh * jnp.exp(gam_last - gam_c)
            s_sc[i] = s_old * jnp.exp(gam_last) + lax.dot_general(
                k_dec.astype(BF16), vn_bf, (((0,), (0,)), ((), ())), preferred_element_type=F32)
            rms = lax.rsqrt(jnp.mean(o * o, axis=-1, keepdims=True) + 1e-6)
            zz = z_ref[pl.ds(r0, C), v_cols]
            o_ref[pl.ds(r0, C), v_cols] = (o * rms * nw * (zz * (1.0 / (1.0 + jnp.exp(-zz))))).astype(o_ref.dtype)
        return carry

    lax.fori_loop(0, n_inner, chunk_body, 0)

    @pl.when(tt == n_tt - 1)
    def _():
        sout_ref[0] = s_sc[...]


def gdn_chunk(conv, z, ab, a_log, dt_bias, norm_w, s0, *, n_seq, rows_per_seq, rows_per_step, chunk, valid_len,
              z_col_off, out_dtype):
    n_tt = rows_per_seq // rows_per_step
    n_inner = rows_per_step // chunk
    n_hb = GDN_V_HEADS // GDN_HB
    qw, vw = GDN_HB // 2 * GDN_DK, GDN_HB * GDN_DV
    k_off, v_off, z_off = GDN_QK_WIDTH // qw, 2 * GDN_QK_WIDTH // vw, z_col_off // vw
    row = lambda b, hb, t: b * n_tt + t
    pad_row = lambda p: jnp.pad(p.astype(F32), (0, LANE - p.shape[0])).reshape(1, LANE)
    return pl.pallas_call(
        functools.partial(_gdn_chunk_kernel, chunk=chunk, n_inner=n_inner, n_tt=n_tt, valid_len=valid_len),
        grid=(n_seq, n_hb, n_tt),
        in_specs=[pl.BlockSpec((rows_per_step, qw), lambda b, hb, t: (row(b, hb, t), hb)),
                  pl.BlockSpec((rows_per_step, qw), lambda b, hb, t: (row(b, hb, t), k_off + hb)),
                  pl.BlockSpec((rows_per_step, vw), lambda b, hb, t: (row(b, hb, t), v_off + hb)),
                  pl.BlockSpec((rows_per_step, vw), lambda b, hb, t: (row(b, hb, t), z_off + hb)),
                  pl.BlockSpec((rows_per_step, LANE), lambda b, hb, t: (row(b, hb, t), 0)),
                  pl.BlockSpec((1, LANE), lambda b, hb, t: (0, 0)),
                  pl.BlockSpec((1, LANE), lambda b, hb, t: (0, 0)),
                  pl.BlockSpec((1, GDN_DV), lambda b, hb, t: (0, 0)),
                  pl.BlockSpec((1, GDN_HB, GDN_DK, GDN_DV), lambda b, hb, t: (b, hb, 0, 0))],
        out_specs=[pl.BlockSpec((rows_per_step, vw), lambda b, hb, t: (row(b, hb, t), hb)),
                   pl.BlockSpec((1, GDN_HB, GDN_DK, GDN_DV), lambda b, hb, t: (b, hb, 0, 0))],
        out_shape=[jax.ShapeDtypeStruct((n_seq * rows_per_seq, GDN_V_WIDTH), out_dtype),
                   jax.ShapeDtypeStruct((n_seq, GDN_V_HEADS, GDN_DK, GDN_DV), F32)],
        scratch_shapes=[pltpu.VMEM((GDN_HB, GDN_DK, GDN_DV), F32)],
        compiler_params=pltpu.CompilerParams(
            dimension_semantics=("parallel", "parallel", "arbitrary"), vmem_limit_bytes=VMEM_LIMIT),
        name="gdn_chunk",
    )(conv, conv, conv, z, ab, pad_row(a_log), pad_row(dt_bias), norm_w.astype(F32).reshape(1, GDN_DV), s0)


def gdn_layer(qkvz, ab, state_in, conv_in, conv_w, a_log, dt_bias, norm_w, *, n_batch, seq_len, n_dec, dec_len):
    n_p = n_batch * seq_len
    n_tp = seq_len // GDN_CONV_TT
    hist_per_block = GDN_CONV_TT // GDN_HIST
    conv_p = gdn_conv(qkvz, qkvz, conv_w, rows=GDN_CONV_TT, n_blocks=n_batch * n_tp, n_t=n_tp,
                      data_map=lambda i: i, hist_map=lambda i: jnp.maximum(i * hist_per_block - 1, 0),
                      zero_first=True)
    zeros_s = jnp.zeros((n_batch, GDN_V_HEADS, GDN_DK, GDN_DV), F32)
    gated_p, st_p = gdn_chunk(conv_p, qkvz, ab, a_log, dt_bias, norm_w, zeros_s, n_seq=n_batch,
                              rows_per_seq=seq_len, rows_per_step=GDN_TT, chunk=GDN_CHUNK, valid_len=GDN_CHUNK,
                              z_col_off=GDN_CONV_CH, out_dtype=BF16)
    buf_p = qkvz[:n_p, :GDN_CONV_CH].reshape(n_batch, seq_len, GDN_CONV_CH)[:, seq_len - (GDN_CONV - 1):]
    R = GDN_SAMPLE_ROWS
    x_s = qkvz[n_p:].reshape(n_dec, dec_len, -1)
    pad_t = lambda t, front: jnp.pad(t, ((0, 0), (front, R - front - t.shape[1]), (0, 0)))
    ext = jnp.concatenate([pad_t(conv_in, R - (GDN_CONV - 1)), pad_t(x_s[..., :GDN_CONV_CH], 0)], axis=1)
    ext = ext.reshape(n_dec * 2 * R, GDN_CONV_CH)
    conv_s = gdn_conv(ext, ext, conv_w, rows=R, n_blocks=n_dec, n_t=1,
                      data_map=lambda i: 2 * i + 1, hist_map=lambda i: 2 * i, zero_first=False)
    z_s = pad_t(x_s[..., GDN_CONV_CH:], 0).reshape(n_dec * R, GDN_V_WIDTH)
    ab_s = pad_t(ab[n_p:].reshape(n_dec, dec_len, LANE), 0).reshape(n_dec * R, LANE)
    gated_s, st_s = gdn_chunk(conv_s, z_s, ab_s, a_log, dt_bias, norm_w, state_in, n_seq=n_dec, rows_per_seq=R,
                              rows_per_step=R, chunk=R, valid_len=dec_len, z_col_off=0, out_dtype=F32)
    gated_s = gated_s.reshape(n_dec, R, GDN_V_WIDTH)[:, :dec_len].reshape(n_dec * dec_len, GDN_V_WIDTH)
    buf_s = jnp.concatenate([conv_in, x_s[..., :GDN_CONV_CH]], axis=1)[:, dec_len:]
    return gated_p, gated_s, st_p, st_s, buf_p, buf_s


def index_scores(qi, ki, wi):
    s = jnp.einsum('bthd,bsd->bths', qi, ki).astype(F32) * (IDX_DIM ** -0.5)
    return jnp.einsum('bths,bth->bts', jax.nn.relu(s), wi * (IDX_HEADS ** -0.5))


def gathered_attend(q, kvs, qpos, kpos, valid, rel_bias):
    n_kv, g = q.shape[-3], q.shape[-2]
    ks, vs = kvs[..., 0, :, :], kvs[..., 1, :, :]
    logits = jnp.einsum('bthgd,btkhd->bhgtk', q, ks).astype(F32) * (HEAD_DIM ** -0.5)
    dist = qpos[None, :, None] - kpos
    logits = logits + head_bias(rel_bias, dist, n_kv, g)
    p = masked_softmax(logits, (valid & (dist >= 0))[:, None, None])
    return jnp.einsum('bhgtk,btkhd->bthgd', p, vs)


def dsa_sample_core(q, kv, z, qi, ki, wi, kv_pool, kidx_pool, layer, page_table, rel_bias):
    Bd, L, _ = q.shape
    n_pages = page_table.shape[1]
    past = n_pages * PAGE_SIZE
    q = q.reshape(Bd, L, KV_D, N_HEADS // KV_D, HEAD_DIM)
    kv = kv.reshape(Bd, L, 2, KV_D, HEAD_DIM)
    qi = qi.reshape(Bd, L, IDX_HEADS, IDX_DIM)
    ki_past = kidx_pool[layer, page_table].reshape(Bd, past, IDX_DIM)
    ki_all = jnp.concatenate([ki_past, ki], axis=1)
    total = past + L
    topk = min(TOPK_MAX, total // 4)
    qpos = past + jnp.arange(L)
    sc = index_scores(qi, ki_all, wi)
    sc = jnp.where(jnp.arange(total)[None, None, :] <= qpos[None, :, None], sc, -jnp.inf)
    vals, idx = lax.top_k(sc, topk)
    pidx = jnp.minimum(idx, past - 1)
    phys = jnp.take_along_axis(page_table, (pidx // PAGE_SIZE).reshape(Bd, -1), axis=1).reshape(idx.shape)
    kv_past = kv_pool[layer, phys, pidx % PAGE_SIZE]
    kv_new = take_rows(kv, jnp.clip(idx - past, 0, L - 1))
    kv_sel = jnp.where((idx >= past)[..., None, None, None], kv_new, kv_past)
    o = gathered_attend(q, kv_sel, qpos, idx, vals > -jnp.inf, rel_bias).reshape(Bd, L, ATT_WIDTH)
    return o * jax.nn.silu(z)


DSA_KC = 256
INT_MIN = -2 ** 31
NEG_BIG = -1e30
G_D = N_HEADS // KV_D
BIAS_WIN = DSA_KC + Q_BLOCK


def _sortable_key(s):
    b = pltpu.bitcast(s, jnp.int32)
    return jnp.where(b < 0, b ^ jnp.int32(0x7FFFFFFF), b)


def _dsa_prompt_kernel(qT_ref, qiT_ref, wiT_ref, zT_ref, ki_ref, k_ref, vT_ref, win_ref, o_ref,
                       key_sc, mask_sc, *, topk, idx_bits, cdt):
    qb = pl.program_id(1)
    t0 = qb * Q_BLOCK
    nch = (qb + 2) // 2
    t_idx = t0 + lax.broadcasted_iota(jnp.int32, (1, Q_BLOCK), 1)
    row_iota = lax.broadcasted_iota(jnp.int32, (DSA_KC, Q_BLOCK), 0)

    def score_chunk(c, carry):
        kic = ki_ref[0, c].astype(cdt)
        acc = jnp.zeros((DSA_KC, Q_BLOCK), F32)
        for hp in range(IDX_HEADS // 2):
            rhs = jnp.concatenate([qiT_ref[(2 * hp) * IDX_DIM:(2 * hp + 1) * IDX_DIM, :],
                                   qiT_ref[(2 * hp + 1) * IDX_DIM:(2 * hp + 2) * IDX_DIM, :]], axis=1)
            s = jnp.dot(kic, rhs, preferred_element_type=F32) * (IDX_DIM ** -0.5)
            s = jnp.maximum(s, 0.0)
            w0 = wiT_ref[2 * hp:2 * hp + 1, :] * (IDX_HEADS ** -0.5)
            w1 = wiT_ref[2 * hp + 1:2 * hp + 2, :] * (IDX_HEADS ** -0.5)
            acc = acc + s[:, :Q_BLOCK] * w0 + s[:, Q_BLOCK:] * w1
        s_idx = c * DSA_KC + row_iota
        key_sc[c] = jnp.where(s_idx <= t_idx, _sortable_key(acc), INT_MIN)
        return carry

    lax.fori_loop(0, nch, score_chunk, 0)

    def count(pred):
        def body(c, acc):
            hit = pred(key_sc[c], c * DSA_KC + row_iota)
            return acc + hit.reshape(DSA_KC // 8, 8, Q_BLOCK).sum(axis=0)
        acc = lax.fori_loop(0, nch, body, jnp.zeros((8, Q_BLOCK), jnp.int32))
        return jnp.sum(acc, axis=0, keepdims=True)

    c_nonneg = count(lambda k, s: jnp.where(k >= 0, 1, 0))
    thr = jnp.where(c_nonneg >= topk, 0, INT_MIN).astype(jnp.int32)

    def thr_bit(i, thr):
        cand = thr + jnp.left_shift(jnp.int32(1), 30 - i)
        return jnp.where(count(lambda k, s: jnp.where(k >= cand, 1, 0)) >= topk, cand, thr)

    thr = lax.fori_loop(0, 31, thr_bit, thr)
    need = topk - count(lambda k, s: jnp.where(k > thr, 1, 0))

    def lim_bit(i, lim):
        cand = lim + jnp.left_shift(jnp.int32(1), idx_bits - 1 - i)
        c = count(lambda k, s: jnp.where(k == thr, jnp.where(s < cand, 1, 0), 0))
        return jnp.where(c <= need, cand, lim)

    lim = lax.fori_loop(0, idx_bits, lim_bit, jnp.zeros((1, Q_BLOCK), jnp.int32))

    def mask_chunk(c, carry):
        k = key_sc[c]
        s_idx = c * DSA_KC + row_iota
        tie = jnp.where(k == thr, jnp.where(s_idx < lim, 0.0, NEG_BIG), NEG_BIG)
        m = jnp.where(k > thr, 0.0, tie)
        mask_sc[c] = jnp.where(k == INT_MIN, NEG_BIG, m)
        return carry

    lax.fori_loop(0, nch, mask_chunk, 0)

    for j in range(KV_D):
        jp = j // 2
        qj = jnp.concatenate([qT_ref[(G_D * j + g) * HEAD_DIM:(G_D * j + g + 1) * HEAD_DIM, :]
                              for g in range(G_D)], axis=1)
        qj = (qj.astype(F32) * (HEAD_DIM ** -0.5)).astype(cdt)
        zpad = jnp.zeros_like(qj)
        rhs = jnp.concatenate([qj, zpad] if j % 2 == 0 else [zpad, qj], axis=0)

        def chunk_body(c, carry, j=j, jp=jp, rhs=rhs):
            m, l, acc = carry
            kc = k_ref[0, c, :, jp * 2 * HEAD_DIM:(jp + 1) * 2 * HEAD_DIM].astype(cdt)
            s = jnp.dot(kc, rhs, preferred_element_type=F32)
            wt = win_ref[qb - 2 * c]
            madd = mask_sc[c]
            parts = []
            for g in range(G_D):
                h = G_D * j + g
                r = jnp.broadcast_to(wt[h:h + 1, :], (DSA_KC, BIAS_WIN))
                b = pltpu.roll(r, 0, 1, stride=1, stride_axis=0)[:, DSA_KC:]
                parts.append(s[:, g * Q_BLOCK:(g + 1) * Q_BLOCK] + (b + madd))
            s = jnp.concatenate(parts, axis=1)
            m_new = jnp.maximum(m, jnp.max(s, axis=0, keepdims=True))
            alpha = jnp.exp(m - m_new)
            p = jnp.exp(s - m_new)
            l = l * alpha + jnp.sum(p, axis=0, keepdims=True)
            vt = vT_ref[0, c, j * HEAD_DIM:(j + 1) * HEAD_DIM, :]
            acc = acc * alpha + jnp.dot(vt, p.astype(cdt), preferred_element_type=F32)
            return m_new, l, acc

        init = (jnp.full((1, G_D * Q_BLOCK), NEG_BIG, F32), jnp.zeros((1, G_D * Q_BLOCK), F32),
                jnp.zeros((HEAD_DIM, G_D * Q_BLOCK), F32))
        m, l, acc = lax.fori_loop(0, nch, chunk_body, init)
        o = acc * (1.0 / l)
        for g in range(G_D):
            r0 = (G_D * j + g) * HEAD_DIM
            z = zT_ref[r0:r0 + HEAD_DIM, :]
            gate = z * (1.0 / (1.0 + jnp.exp(-z)))
            o_ref[r0:r0 + HEAD_DIM, :] = (o[:, g * Q_BLOCK:(g + 1) * Q_BLOCK] * gate).astype(o_ref.dtype)


def dsa_bias_windows(rel_bias, seq_len):
    o = jnp.arange(seq_len // Q_BLOCK)[:, None]
    m = jnp.arange(BIAS_WIN)[None, :]
    d = jnp.maximum(o * Q_BLOCK + m - DSA_KC, 0)
    return jnp.moveaxis(rel_bias[rel_bucket(d)].astype(F32), -1, 1)


def dsa_prompt_attend(qqiT, wiT, zT, ki4, k4, v4T, win, *, n_batch, seq_len, cdt=BF16):
    nqb = seq_len // Q_BLOCK
    nc = seq_len // DSA_KC
    topk = min(TOPK_MAX, seq_len // 4)
    idx_bits = int(math.log2(seq_len)) + 1
    tok = lambda b, q: (0, b * nqb + q)
    per_batch = lambda b, q: (b, 0, 0, 0)
    return pl.pallas_call(
        functools.partial(_dsa_prompt_kernel, topk=topk, idx_bits=idx_bits, cdt=cdt),
        grid=(n_batch, nqb),
        in_specs=[pl.BlockSpec((ATT_WIDTH, Q_BLOCK), tok),
                  pl.BlockSpec((IDX_HEADS * IDX_DIM, Q_BLOCK), lambda b, q: (1, b * nqb + q)),
                  pl.BlockSpec((IDX_HEADS, Q_BLOCK), tok),
                  pl.BlockSpec((ATT_WIDTH, Q_BLOCK), tok),
                  pl.BlockSpec((1, nc, DSA_KC, IDX_DIM), per_batch),
                  pl.BlockSpec((1, nc, DSA_KC, D_KV), per_batch),
                  pl.BlockSpec((1, nc, D_KV, DSA_KC), per_batch),
                  pl.BlockSpec((nqb, N_HEADS, BIAS_WIN), lambda b, q: (0, 0, 0))],
        out_specs=pl.BlockSpec((ATT_WIDTH, Q_BLOCK), tok),
        out_shape=jax.ShapeDtypeStruct((ATT_WIDTH, n_batch * seq_len), BF16),
        scratch_shapes=[pltpu.VMEM((nc, DSA_KC, Q_BLOCK), jnp.int32),
                        pltpu.VMEM((nc, DSA_KC, Q_BLOCK), F32)],
        compiler_params=pltpu.CompilerParams(
            dimension_semantics=("parallel", "arbitrary"), vmem_limit_bytes=VMEM_LIMIT),
        name="dsa_prompt_attend",
    )(qqiT, qqiT, wiT, zT, ki4, k4, v4T, win)


DSS_NP1 = 16
DSS_NP2 = 8
DSS_TP = 8
DSS_ROWS = 2 * G_D * DSS_TP
T5_LAST_BUCKET_DIST = 1600


def _dsa_sample_select_kernel(pt_ref, qi_ref, wb_ref, kinew_ref, *rest, n_pages, n_new, topk, idx_bits, cdt):
    del pt_ref
    page_refs, mask_ref, key_sc = rest[:DSS_NP1], rest[DSS_NP1], rest[DSS_NP1 + 1]
    s = pl.program_id(1)
    lane = lax.broadcasted_iota(jnp.int32, (DSS_TP, PAGE_SIZE), 1)
    trow = lax.broadcasted_iota(jnp.int32, (DSS_TP, PAGE_SIZE), 0)
    qi = qi_ref[0]
    wb = wb_ref[0]

    def page_keys(kp):
        sc = lax.dot_general(qi, kp.astype(cdt), (((1,), (1,)), ((), ())),
                             preferred_element_type=F32) * (IDX_DIM ** -0.5)
        sc = jnp.maximum(sc, 0.0) * wb
        return _sortable_key(sc.reshape(IDX_HEADS, DSS_TP, PAGE_SIZE).sum(axis=0))

    for i in range(DSS_NP1):
        key_sc[s * DSS_NP1 + i] = page_keys(page_refs[i][0, 0])

    @pl.when(s == 0)
    def _():
        kn = page_keys(kinew_ref[0])
        key_sc[n_pages] = jnp.where(lane < n_new, jnp.where(lane <= trow, kn, INT_MIN), INT_MIN)

    @pl.when(s == n_pages // DSS_NP1 - 1)
    def _():
        def count(pred):
            def body(p, acc):
                return acc + pred(key_sc[p], p * PAGE_SIZE + lane)
            acc = lax.fori_loop(0, n_pages + 1, body, jnp.zeros((DSS_TP, PAGE_SIZE), jnp.int32))
            return jnp.broadcast_to(jnp.sum(acc, axis=1, keepdims=True), (DSS_TP, PAGE_SIZE))

        c_nonneg = count(lambda k, i: jnp.where(k >= 0, 1, 0))
        thr = jnp.where(c_nonneg >= topk, 0, INT_MIN).astype(jnp.int32)

        def thr_bit(b, thr):
            cand = thr + jnp.left_shift(jnp.int32(1), 30 - b)
            return jnp.where(count(lambda k, i: jnp.where(k >= cand, 1, 0)) >= topk, cand, thr)

        thr = lax.fori_loop(0, 31, thr_bit, thr)
        need = topk - count(lambda k, i: jnp.where(k > thr, 1, 0))

        def lim_bit(b, lim):
            cand = lim + jnp.left_shift(jnp.int32(1), idx_bits - 1 - b)
            c = count(lambda k, i: jnp.where(k == thr, jnp.where(i < cand, 1, 0), 0))
            return jnp.where(c <= need, cand, lim)

        lim = lax.fori_loop(0, idx_bits, lim_bit, jnp.zeros((DSS_TP, PAGE_SIZE), jnp.int32))

        def mask_page(p, carry):
            k = key_sc[p]
            tie = jnp.where(k == thr, jnp.where(p * PAGE_SIZE + lane < lim, 0.0, NEG_BIG), NEG_BIG)
            m = jnp.where(k > thr, 0.0, tie)
            mask_ref[0, p] = jnp.where(k == INT_MIN, NEG_BIG, m)
            return carry

        lax.fori_loop(0, n_pages + 1, mask_page, 0)


def _dsa_sample_attend_kernel(pt_ref, q_ref, z_ref, mask_ref, masknew_ref, bnear_ref, bfar_ref, kvnew_ref, *rest,
                              n_pages, n_far, cdt):
    del pt_ref
    page_refs, o_ref = rest[:DSS_NP2], rest[DSS_NP2]
    m_sc, l_sc, acc_sc = rest[DSS_NP2 + 1:]
    s = pl.program_id(1)
    n_pairs = KV_D // 2
    pw = 2 * HEAD_DIM

    def attend(pages, masks, page_ids):
        for jp in range(n_pairs):
            kcat = jnp.concatenate([pg[:, jp * pw:(jp + 1) * pw] for pg in pages], axis=0).astype(cdt)
            vcat = jnp.concatenate([pg[:, D_KV + jp * pw:D_KV + (jp + 1) * pw] for pg in pages], axis=0).astype(cdt)
            logits = lax.dot_general(q_ref[0, jp], kcat, (((1,), (1,)), ((), ())), preferred_element_type=F32)
            far = bfar_ref[jp]
            bias = [jnp.where(pid >= n_far, bnear_ref[jnp.maximum(pid - n_far, 0), jp], far) for pid in page_ids]
            madd = [jnp.concatenate([mk] * (DSS_ROWS // DSS_TP), axis=0) for mk in masks]
            logits = logits + jnp.concatenate([b + m for b, m in zip(bias, madd)], axis=1)
            m_old = m_sc[jp]
            m_new = jnp.maximum(m_old, jnp.broadcast_to(jnp.max(logits, axis=1, keepdims=True), m_old.shape))
            alpha = jnp.exp(m_old - m_new)
            p = jnp.exp(logits - jnp.concatenate([m_new] * len(pages), axis=1))
            l_sc[jp] = l_sc[jp] * alpha + jnp.broadcast_to(jnp.sum(p, axis=1, keepdims=True), m_old.shape)
            acc_sc[jp] = acc_sc[jp] * alpha + jnp.dot(p.astype(cdt), vcat, preferred_element_type=F32)
            m_sc[jp] = m_new

    @pl.when(s == 0)
    def _():
        m_sc[...] = jnp.full_like(m_sc, NEG_BIG)
        l_sc[...] = jnp.zeros_like(l_sc)
        acc_sc[...] = jnp.zeros_like(acc_sc)
        attend([kvnew_ref[0]], [masknew_ref[0, 0]], [n_pages])

    attend([r[0, 0] for r in page_refs], [mask_ref[0, i] for i in range(DSS_NP2)],
           [s * DSS_NP2 + i for i in range(DSS_NP2)])

    @pl.when(s == n_pages // DSS_NP2 - 1)
    def _():
        for jp in range(n_pairs):
            z = z_ref[0, jp]
            o_ref[0, jp] = acc_sc[jp] * (1.0 / l_sc[jp]) * (z * (1.0 / (1.0 + jnp.exp(-z))))


def dsa_sample(q_s, z_s, qi_s, wi_s, ki_s, kv_s, kv_pool, kidx_pool, layer, page_table, rel_bias, cdt=BF16):
    n_dec, n_pages = page_table.shape
    dec_len = q_s.shape[0] // n_dec
    past = n_pages * PAGE_SIZE
    total = past + dec_len
    topk = min(TOPK_MAX, total // 4)
    idx_bits = int(math.log2(total)) + 1
    pad_t = DSS_TP - dec_len
    n_pairs = KV_D // 2
    eye2 = jnp.eye(2, dtype=F32)

    qi = jnp.pad(jnp.swapaxes(qi_s.reshape(n_dec, dec_len, IDX_HEADS, IDX_DIM), 1, 2), ((0, 0), (0, 0), (0, pad_t), (0, 0)))
    qi = qi.reshape(n_dec, IDX_HEADS * DSS_TP, IDX_DIM).astype(cdt)
    wb = jnp.pad(jnp.swapaxes(wi_s.reshape(n_dec, dec_len, IDX_HEADS), 1, 2) * (IDX_HEADS ** -0.5), ((0, 0), (0, 0), (0, pad_t)))
    wb = jnp.broadcast_to(wb.reshape(n_dec, IDX_HEADS * DSS_TP, 1), (n_dec, IDX_HEADS * DSS_TP, PAGE_SIZE))
    ki_new = jnp.pad(ki_s.reshape(n_dec, dec_len, IDX_DIM), ((0, 0), (0, PAGE_SIZE - dec_len), (0, 0)))
    kidx4 = kidx_pool.reshape(kidx_pool.shape[0], kidx_pool.shape[1], PAGE_SIZE, IDX_DIM)
    page_spec = lambda np_, i, width: pl.BlockSpec(
        (1, 1, PAGE_SIZE, width), lambda b, s, pt: (layer, pt[b, s * np_ + i], 0, 0))
    per_b3 = lambda b, s, pt: (b, 0, 0)
    mask = pl.pallas_call(
        functools.partial(_dsa_sample_select_kernel, n_pages=n_pages, n_new=dec_len, topk=topk, idx_bits=idx_bits,
                          cdt=cdt),
        grid_spec=pltpu.PrefetchScalarGridSpec(
            num_scalar_prefetch=1, grid=(n_dec, n_pages // DSS_NP1),
            in_specs=[pl.BlockSpec((1, IDX_HEADS * DSS_TP, IDX_DIM), per_b3),
                      pl.BlockSpec((1, IDX_HEADS * DSS_TP, PAGE_SIZE), per_b3),
                      pl.BlockSpec((1, PAGE_SIZE, IDX_DIM), per_b3)]
                     + [page_spec(DSS_NP1, i, IDX_DIM) for i in range(DSS_NP1)],
            out_specs=pl.BlockSpec((1, n_pages + 1, DSS_TP, PAGE_SIZE), lambda b, s, pt: (b, 0, 0, 0)),
            scratch_shapes=[pltpu.VMEM((n_pages + 1, DSS_TP, PAGE_SIZE), jnp.int32)]),
        out_shape=jax.ShapeDtypeStruct((n_dec, n_pages + 1, DSS_TP, PAGE_SIZE), F32),
        compiler_params=pltpu.CompilerParams(
            dimension_semantics=("parallel", "arbitrary"), vmem_limit_bytes=VMEM_LIMIT),
        name="dsa_sample_select",
    )(page_table, qi, wb, ki_new, *([kidx4] * DSS_NP1))

    def pair_rows(t, scale):
        t = t.reshape(n_dec, dec_len, n_pairs, 2, G_D, HEAD_DIM).transpose(0, 2, 3, 4, 1, 5) * scale
        t = jnp.pad(t, ((0, 0),) * 4 + ((0, pad_t), (0, 0)))
        return jnp.einsum('bpjgtd,jk->bpjgtkd', t, eye2).reshape(n_dec, n_pairs, DSS_ROWS, 2 * HEAD_DIM)

    q_pr = pair_rows(q_s, HEAD_DIM ** -0.5).astype(cdt)
    z_pr = pair_rows(z_s, 1.0)
    n_far = max(0, min(n_pages, (past - (PAGE_SIZE - 1) - T5_LAST_BUCKET_DIST) // PAGE_SIZE + 1))
    near_pages = jnp.arange(n_far, n_pages + 1)
    t8 = jnp.arange(DSS_TP)
    dist = past + t8[None, :, None] - (near_pages[:, None, None] * PAGE_SIZE + jnp.arange(PAGE_SIZE)[None, None, :])
    b_near = rel_bias[rel_bucket(dist)].astype(F32)
    b_near = b_near.transpose(0, 3, 1, 2).reshape(n_pages + 1 - n_far, n_pairs, DSS_ROWS, PAGE_SIZE)
    b_far = jnp.broadcast_to(rel_bias[N_BUCKETS - 1].astype(F32)[:, None, None], (N_HEADS, DSS_TP, PAGE_SIZE))
    b_far = b_far.reshape(n_pairs, DSS_ROWS, PAGE_SIZE)
    kv_new = jnp.pad(kv_s.reshape(n_dec, dec_len, 2 * D_KV), ((0, 0), (0, PAGE_SIZE - dec_len), (0, 0)))
    kv4 = kv_pool.reshape(kv_pool.shape[0], kv_pool.shape[1], PAGE_SIZE, 2 * D_KV)
    per_b4 = lambda b, s, pt: (b, 0, 0, 0)
    o = pl.pallas_call(
        functools.partial(_dsa_sample_attend_kernel, n_pages=n_pages, n_far=n_far, cdt=cdt),
        grid_spec=pltpu.PrefetchScalarGridSpec(
            num_scalar_prefetch=1, grid=(n_dec, n_pages // DSS_NP2),
            in_specs=[pl.BlockSpec((1, n_pairs, DSS_ROWS, 2 * HEAD_DIM), per_b4),
                      pl.BlockSpec((1, n_pairs, DSS_ROWS, 2 * HEAD_DIM), per_b4),
                      pl.BlockSpec((1, DSS_NP2, DSS_TP, PAGE_SIZE), lambda b, s, pt: (b, s, 0, 0)),
                      pl.BlockSpec((1, 1, DSS_TP, PAGE_SIZE), lambda b, s, pt: (b, n_pages, 0, 0)),
                      pl.BlockSpec(b_near.shape, lambda b, s, pt: (0, 0, 0, 0)),
                      pl.BlockSpec(b_far.shape, lambda b, s, pt: (0, 0, 0)),
                      pl.BlockSpec((1, PAGE_SIZE, 2 * D_KV), per_b3)]
                     + [page_spec(DSS_NP2, i, 2 * D_KV) for i in range(DSS_NP2)],
            out_specs=pl.BlockSpec((1, n_pairs, DSS_ROWS, 2 * HEAD_DIM), per_b4),
            scratch_shapes=[pltpu.VMEM((n_pairs, DSS_ROWS, 2 * HEAD_DIM), F32)] * 3),
        out_shape=jax.ShapeDtypeStruct((n_dec, n_pairs, DSS_ROWS, 2 * HEAD_DIM), F32),
        compiler_params=pltpu.CompilerParams(
            dimension_semantics=("parallel", "arbitrary"), vmem_limit_bytes=VMEM_LIMIT),
        name="dsa_sample_attend",
    )(page_table, q_pr, z_pr, mask, mask, b_near, b_far, kv_new, *([kv4] * DSS_NP2))
    o = o.reshape(n_dec, n_pairs, 2, G_D, DSS_TP, 2, HEAD_DIM)
    o = jnp.einsum('bpjgtkd,jk->bpjgtd', o, eye2)[:, :, :, :, :dec_len]
    return o.transpose(0, 4, 1, 2, 3, 5).reshape(n_dec * dec_len, ATT_WIDTH)


def _pad_cols(w, n):
    return jnp.pad(w, ((0, 0), (0, n - w.shape[1])))


def kernel(x_prompt, x_sample, cache_a_kv, state_s5, state_gdn, state_gdn_conv, cache_d_kv, cache_d_kidx,
           page_table, p_prompt, p_sample, rel_bias, ln_g, ln_b, ple_gate_w, ple_w,
           a_w_in, a_sinks, a_w_out,
           s5_w_in, s5_a_re, s5_a_im, s5_b_re, s5_b_im, s5_c_re, s5_c_im, s5_d, s5_log_dt, s5_w_glu, s5_w_out,
           gdn_w_in, gdn_conv_w, gdn_a_log, gdn_dt_bias, gdn_norm_w, gdn_w_out,
           dsa_w_in, dsa_w_out):
    past_len = page_table.shape[1] * PAGE_SIZE
    x = join_tokens(x_prompt, x_sample)
    x_bf = x.astype(BF16)
    outs = {}

    def post(i, x, h):
        p_bf = join_tokens(p_prompt[i], p_sample[i]).astype(BF16)
        return post_norm_ple(x, h, p_bf, ln_g[i], ln_b[i], ple_gate_w[i].astype(BF16), ple_w[i].astype(BF16))

    def finish_layer(i, x, gated, w_out):
        return post(i, x, matmul(gated.astype(BF16), w_out.astype(BF16)))

    proj = matmul(x_bf, a_w_in[0].astype(BF16))
    pp, ps = split_tokens(proj)
    gp, outs['a_p'] = swa_core(pp, None, 0, a_sinks[0], rel_bias)
    gs, outs['a_s'] = swa_core(ps, cache_a_kv[0], past_len, a_sinks[0], rel_bias)
    x, x_bf = finish_layer(0, x, join_tokens(gp, gs), a_w_out[0])

    proj = matmul(x_bf, s5_w_in[0].astype(BF16))
    tables = s5_tables(s5_a_re[0], s5_a_im[0], s5_b_re[0], s5_b_im[0], s5_c_re[0], s5_c_im[0], s5_log_dt[0])
    gated, outs['s5_p'], outs['s5_s'] = s5_layer(proj, state_s5[0], tables, s5_d[0], s5_w_glu[0],
                                                 n_batch=BATCH, seq_len=SEQ, n_dec=DEC_BATCH, dec_len=DEC_SEQ)
    x, x_bf = post(1, x, matmul(gated, s5_w_out[0].astype(BF16)))

    w_in = gdn_w_in[0]
    c_gz = GDN_CONV_CH + GDN_V_WIDTH
    qkvz = matmul(x_bf, w_in[:, :c_gz].astype(BF16))
    ab = matmul(x_bf, _pad_cols(w_in[:, c_gz:], LANE).astype(BF16))
    gp, gs, outs['gd_p'], outs['gd_s'], outs['gc_p'], outs['gc_s'] = gdn_layer(
        qkvz, ab, state_gdn[0], state_gdn_conv[0], gdn_conv_w[0], gdn_a_log[0], gdn_dt_bias[0], gdn_norm_w[0],
        n_batch=BATCH, seq_len=SEQ, n_dec=DEC_BATCH, dec_len=DEC_SEQ)
    w_out_bf = gdn_w_out[0].astype(BF16)
    x, x_bf = post(2, x, jnp.concatenate([matmul(gp, w_out_bf), matmul(gs.astype(BF16), w_out_bf)], axis=0))

    w_in = dsa_w_in[0]
    c_z = 2 * ATT_WIDTH + 2 * D_KV
    c_qi = c_z + IDX_HEADS * IDX_DIM
    c_kv = ATT_WIDTH + 2 * D_KV
    w_q, w_kv, w_z, w_qi = w_in[:, :ATT_WIDTH], w_in[:, ATT_WIDTH:c_kv], w_in[:, c_kv:c_z], w_in[:, c_z:c_qi]
    w_out_bf = dsa_w_out[0].astype(BF16)
    kv_nat = matmul(x_bf, w_kv.astype(BF16))
    kiw = matmul(x_bf, _pad_cols(w_in[:, c_qi:], 2 * LANE).astype(BF16))
    xT_bf = x_bf[:N_PROMPT_TOK].T
    qqiT = matmul(jnp.concatenate([w_q, w_qi], axis=1).T.astype(BF16), xT_bf, out_dtype=BF16)
    zT = matmul(w_z.T.astype(BF16), xT_bf)
    wiT = matmul(w_in[:, c_qi + IDX_DIM:].T.astype(BF16), xT_bf)
    nc = SEQ // DSA_KC
    kv_p = kv_nat[:N_PROMPT_TOK]
    v4T = jnp.swapaxes(kv_p[:, D_KV:].astype(BF16).reshape(BATCH, nc, DSA_KC, D_KV), 2, 3)
    gT = dsa_prompt_attend(qqiT, wiT, zT, kiw[:N_PROMPT_TOK].reshape(BATCH, nc, DSA_KC, 2 * LANE),
                           kv_p.reshape(BATCH, nc, DSA_KC, 2 * D_KV), v4T, dsa_bias_windows(rel_bias, SEQ),
                           n_batch=BATCH, seq_len=SEQ)
    h_p = matmul_ta(gT, w_out_bf)
    x_s = x_bf[N_PROMPT_TOK:]
    qzqi_s = matmul(x_s, jnp.concatenate([w_q, w_z, w_qi], axis=1).astype(BF16))
    kiw_s = kiw[N_PROMPT_TOK:]
    gs = dsa_sample(qzqi_s[:, :ATT_WIDTH], qzqi_s[:, ATT_WIDTH:2 * ATT_WIDTH], qzqi_s[:, 2 * ATT_WIDTH:],
                    kiw_s[:, IDX_DIM:IDX_DIM + IDX_HEADS], kiw_s[:, :IDX_DIM], kv_nat[N_PROMPT_TOK:],
                    cache_d_kv, cache_d_kidx, 0, page_table, rel_bias)
    h_s = matmul(gs.astype(BF16), w_out_bf)
    outs['dkv_p'] = kv_p.reshape(BATCH, SEQ, 2, KV_D, HEAD_DIM)
    outs['dkv_s'] = kv_nat[N_PROMPT_TOK:].reshape(DEC_BATCH, DEC_SEQ, 2, KV_D, HEAD_DIM)
    outs['dki_p'] = kiw[:N_PROMPT_TOK, :IDX_DIM].reshape(BATCH, SEQ, IDX_DIM)
    outs['dki_s'] = kiw_s[:, :IDX_DIM].reshape(DEC_BATCH, DEC_SEQ, IDX_DIM)
    x, x_bf = post(3, x, jnp.concatenate([h_p, h_s], axis=0))

    yp, ys = split_tokens(x)
    st = lambda name: outs[name][None]
    return (yp, ys, st('a_p'), st('a_s'), st('s5_p'), st('s5_s'), st('gd_p'), st('gd_s'),
            st('gc_p'), st('gc_s'), st('dkv_p'), st('dkv_s'), st('dki_p'), st('dki_s'))
```

```python
import functools
import math

import jax
import jax.numpy as jnp
from jax import lax
from jax.experimental import pallas as pl
from jax.experimental.pallas import tpu as pltpu

D_MODEL = 2048
BATCH = 4
SEQ = 2048
DEPTH = 4
DEC_BATCH = 32
DEC_SEQ = 4
PAGE_SIZE = 128
N_MIXERS = 4
PLE_DIM = 256
ALPHA = (2 * DEPTH) ** 0.25
LN_EPS = 1e-5
N_BUCKETS = 32
REL_MAX_DIST = 2048
N_HEADS = 32
HEAD_DIM = 64
ATT_WIDTH = N_HEADS * HEAD_DIM
WINDOW = 128
KV_A = 4
A_KV = KV_A * HEAD_DIM
KV_D = 8
D_KV = KV_D * HEAD_DIM
IDX_HEADS = 16
IDX_DIM = 128
TOPK_MAX = 256
Q_BLOCK = 128
S5_WIDTH = D_MODEL
S5_GROUP = 16
S5_GROUPS = S5_WIDTH // S5_GROUP
S5_STATE = 64
GDN_QK_HEADS = 16
GDN_V_HEADS = 32
GDN_DK = 128
GDN_DV = 128
GDN_CONV = 4
GDN_CHUNK = 64
GDN_QK_WIDTH = GDN_QK_HEADS * GDN_DK
GDN_V_WIDTH = GDN_V_HEADS * GDN_DV
GDN_CONV_CH = 2 * GDN_QK_WIDTH + GDN_V_WIDTH

F32 = jnp.float32
BF16 = jnp.bfloat16

N_PROMPT_TOK = BATCH * SEQ
N_SAMPLE_TOK = DEC_BATCH * DEC_SEQ
N_TOK = N_PROMPT_TOK + N_SAMPLE_TOK

V7X_VMEM_BYTES = 64 * 1024 * 1024
VMEM_LIMIT = 48 * 1024 * 1024
LANE = 128


def _mm_kernel(x_ref, w_ref, o_ref):
    o_ref[...] = jnp.dot(x_ref[...], w_ref[...], preferred_element_type=F32).astype(o_ref.dtype)


def _pick_tile(n, prefs):
    for t in prefs:
        if n % t == 0:
            return t
    raise ValueError(f"no tile for {n}")


def matmul(x, w, out_dtype=F32):
    m, k = x.shape
    n = w.shape[1]
    tm = _pick_tile(m, (640, 512, 320, 256, 128, 64, 32, 16, 8))
    tn = _pick_tile(n, (512, 384, 256, 128))
    return pl.pallas_call(
        _mm_kernel,
        grid=(m // tm, n // tn),
        in_specs=[pl.BlockSpec((tm, k), lambda i, j: (i, 0)),
                  pl.BlockSpec((k, tn), lambda i, j: (0, j))],
        out_specs=pl.BlockSpec((tm, tn), lambda i, j: (i, j)),
        out_shape=jax.ShapeDtypeStruct((m, n), out_dtype),
        compiler_params=pltpu.CompilerParams(
            dimension_semantics=("parallel", "parallel"), vmem_limit_bytes=VMEM_LIMIT),
        name="proj_matmul",
    )(x, w)


def _mm_ta_kernel(xt_ref, w_ref, o_ref):
    o_ref[...] = lax.dot_general(xt_ref[...], w_ref[...], (((0,), (0,)), ((), ())),
                                 preferred_element_type=F32).astype(o_ref.dtype)


def matmul_ta(xt, w, out_dtype=F32):
    k, m = xt.shape
    n = w.shape[1]
    tm = _pick_tile(m, (512, 256, 128))
    tn = _pick_tile(n, (512, 384, 256, 128))
    return pl.pallas_call(
        _mm_ta_kernel,
        grid=(m // tm, n // tn),
        in_specs=[pl.BlockSpec((k, tm), lambda i, j: (0, i)),
                  pl.BlockSpec((k, tn), lambda i, j: (0, j))],
        out_specs=pl.BlockSpec((tm, tn), lambda i, j: (i, j)),
        out_shape=jax.ShapeDtypeStruct((m, n), out_dtype),
        compiler_params=pltpu.CompilerParams(
            dimension_semantics=("parallel", "parallel"), vmem_limit_bytes=VMEM_LIMIT),
        name="proj_matmul_ta",
    )(xt, w)


POST_TM = 320
POST_TN = 512


def _post_kernel(x_ref, h_ref, p_ref, g_ref, b_ref, wg_ref, wp_ref, o_ref, obf_ref, y_sc, ybf_sc):
    j = pl.program_id(1)

    @pl.when(j == 0)
    def _():
        t = ALPHA * x_ref[...] + h_ref[...]
        mu = jnp.mean(t, axis=-1, keepdims=True)
        d = t - mu
        var = jnp.mean(d * d, axis=-1, keepdims=True)
        y = d * lax.rsqrt(var + LN_EPS) * g_ref[...] + b_ref[...]
        ybf_sc[...] = y.astype(BF16)
        for jj in range(D_MODEL // POST_TN):
            y_sc[jj] = y[:, jj * POST_TN:(jj + 1) * POST_TN]

    gate = jnp.dot(ybf_sc[...], wg_ref[...], preferred_element_type=F32)
    ple = jnp.dot(p_ref[...], wp_ref[...], preferred_element_type=F32)
    o = y_sc[j] + (1.0 / (1.0 + jnp.exp(-gate))) * ple
    o_ref[...] = o
    obf_ref[...] = o.astype(BF16)


def post_norm_ple(x, h, p_bf, g, b, wg_bf, wp_bf):
    m = x.shape[0]
    tm, tn = POST_TM, POST_TN
    return pl.pallas_call(
        _post_kernel,
        grid=(m // tm, D_MODEL // tn),
        in_specs=[pl.BlockSpec((tm, D_MODEL), lambda i, j: (i, 0)),
                  pl.BlockSpec((tm, D_MODEL), lambda i, j: (i, 0)),
                  pl.BlockSpec((tm, PLE_DIM), lambda i, j: (i, 0)),
                  pl.BlockSpec((1, D_MODEL), lambda i, j: (0, 0)),
                  pl.BlockSpec((1, D_MODEL), lambda i, j: (0, 0)),
                  pl.BlockSpec((D_MODEL, tn), lambda i, j: (0, j)),
                  pl.BlockSpec((PLE_DIM, tn), lambda i, j: (0, j))],
        out_specs=[pl.BlockSpec((tm, tn), lambda i, j: (i, j)),
                   pl.BlockSpec((tm, tn), lambda i, j: (i, j))],
        out_shape=[jax.ShapeDtypeStruct((m, D_MODEL), F32),
                   jax.ShapeDtypeStruct((m, D_MODEL), BF16)],
        scratch_shapes=[pltpu.VMEM((D_MODEL // tn, tm, tn), F32),
                        pltpu.VMEM((tm, D_MODEL), BF16)],
        compiler_params=pltpu.CompilerParams(
            dimension_semantics=("parallel", "arbitrary"), vmem_limit_bytes=VMEM_LIMIT),
        name="post_norm_ple",
    )(x, h, p_bf, g.reshape(1, D_MODEL), b.reshape(1, D_MODEL), wg_bf, wp_bf)


def rel_bucket(dist):
    n = jnp.maximum(dist, 0)
    exact = N_BUCKETS // 2
    logb = exact + (jnp.log(jnp.maximum(n, exact).astype(F32) / exact)
                    / math.log(REL_MAX_DIST / exact) * (N_BUCKETS - exact)).astype(jnp.int32)
    return jnp.where(n < exact, n, jnp.minimum(logb, N_BUCKETS - 1))


def head_bias(rel_bias, dist, n_kv, g):
    b = jnp.moveaxis(rel_bias[rel_bucket(dist)].astype(F32), -1, -3)
    return b.reshape(b.shape[:-3] + (n_kv, g) + b.shape[-2:])


def masked_softmax(logits, mask, sink=None):
    logits = jnp.where(mask, logits, -jnp.inf)
    m = jnp.max(logits, axis=-1, keepdims=True)
    if sink is not None:
        m = jnp.maximum(m, sink)
    e = jnp.exp(logits - m)
    den = jnp.sum(e, axis=-1, keepdims=True)
    if sink is not None:
        den = den + jnp.exp(sink - m)
    return e / den


def take_rows(rows, idx):
    return jax.vmap(lambda r, i: r[i])(rows, idx)


def split_tokens(t):
    c = t.shape[-1]
    return (t[:N_PROMPT_TOK].reshape(BATCH, SEQ, c), t[N_PROMPT_TOK:].reshape(DEC_BATCH, DEC_SEQ, c))


def join_tokens(tp, ts):
    c = tp.shape[-1]
    return jnp.concatenate([tp.reshape(N_PROMPT_TOK, c), ts.reshape(N_SAMPLE_TOK, c)], axis=0)


def window_attend(q, k, v, qpos, kpos, sinks, rel_bias):
    n_kv, g = q.shape[-3], q.shape[-2]
    dist = qpos[..., :, None] - kpos[..., None, :]
    mask = (dist >= 0) & (dist < WINDOW) & (kpos[..., None, :] >= 0)
    logits = jnp.einsum('...qhgd,...khd->...hgqk', q, k).astype(F32) * (HEAD_DIM ** -0.5)
    logits = logits + head_bias(rel_bias, dist, n_kv, g)
    sink = sinks.astype(F32).reshape(n_kv, g, 1, 1)
    p = masked_softmax(logits, mask[..., None, None, :, :], sink)
    return jnp.einsum('...hgqk,...khd->...qhgd', p.astype(v.dtype), v)


def swa_core(proj, kv_cache, start, sinks, rel_bias):
    Bn, L, _ = proj.shape
    G = N_HEADS // KV_A
    q, k, v, z = jnp.split(proj, [ATT_WIDTH, ATT_WIDTH + A_KV, ATT_WIDTH + 2 * A_KV], axis=-1)
    q = q.reshape(Bn, L, KV_A, G, HEAD_DIM)
    k = k.reshape(Bn, L, KV_A, HEAD_DIM)
    v = v.reshape(Bn, L, KV_A, HEAD_DIM)
    if kv_cache is None:
        nb = L // WINDOW
        qb = q.reshape(Bn, nb, WINDOW, KV_A, G, HEAD_DIM)
        kb = k.reshape(Bn, nb, WINDOW, KV_A, HEAD_DIM)
        vb = v.reshape(Bn, nb, WINDOW, KV_A, HEAD_DIM)
        prev = lambda t: jnp.concatenate([jnp.zeros_like(t[:, :1]), t[:, :-1]], axis=1)
        kk = jnp.concatenate([prev(kb), kb], axis=2)
        vv = jnp.concatenate([prev(vb), vb], axis=2)
        qpos = jnp.arange(L).reshape(nb, WINDOW)
        kpos = jnp.concatenate([qpos - WINDOW, qpos], axis=1)
        o = window_attend(qb, kk, vv, qpos, kpos, sinks, rel_bias)
        new_kv = jnp.stack([k[:, L - WINDOW:], v[:, L - WINDOW:]], axis=2)
    else:
        kk = jnp.concatenate([kv_cache[:, :, 0], k], axis=1)
        vv = jnp.concatenate([kv_cache[:, :, 1], v], axis=1)
        qpos = start + jnp.arange(L)
        kpos = start - WINDOW + jnp.arange(WINDOW + L)
        o = window_attend(q, kk, vv, qpos, kpos, sinks, rel_bias)
        new_kv = jnp.stack([kk[:, -WINDOW:], vv[:, -WINDOW:]], axis=2)
    o = o.reshape(Bn, L, ATT_WIDTH)
    return o * jax.nn.silu(z), new_kv


G_A = N_HEADS // KV_A
SWA_KEYS = 2 * WINDOW


def swa_tables(rel_bias, sinks, dec_len):
    def heads_to(b, lead):
        return jnp.moveaxis(b, -1, 0).reshape((KV_A, G_A) + lead)

    dist = jnp.arange(WINDOW)[None, :] - (jnp.arange(SWA_KEYS)[:, None] - WINDOW)
    ok = (dist >= 0) & (dist < WINDOW)
    b = jnp.where(ok[..., None], rel_bias[rel_bucket(dist)].astype(F32), NEG_BIG)
    bias_p = heads_to(b, (SWA_KEYS, WINDOW)).transpose(0, 2, 1, 3).reshape(KV_A, SWA_KEYS, G_A * WINDOW)
    sink_p = jnp.broadcast_to(sinks.astype(F32).reshape(KV_A, 1, G_A, 1), (KV_A, 1, G_A, WINDOW))
    sink_p = sink_p.reshape(KV_A, 1, G_A * WINDOW)
    key_i = jnp.arange(SWA_KEYS)[None, :]
    dist = jnp.arange(dec_len)[:, None] + WINDOW - key_i
    ok = (dist >= 0) & (dist < WINDOW) & (key_i < WINDOW + dec_len)
    b = jnp.where(ok[..., None], rel_bias[rel_bucket(dist)].astype(F32), NEG_BIG)
    bias_s = heads_to(b, (dec_len, SWA_KEYS)).reshape(KV_A, G_A * dec_len, SWA_KEYS)
    sink_s = jnp.broadcast_to(sinks.astype(F32).reshape(KV_A, G_A, 1, 1), (KV_A, G_A, dec_len, LANE))
    sink_s = sink_s.reshape(KV_A, G_A * dec_len, LANE)
    return bias_p, sink_p, bias_s, sink_s


def _swa_prompt_kernel(qT_ref, zT_ref, vTp_ref, vTc_ref, kp_ref, kc_ref, bias_ref, sink_ref, o_ref, *, cdt):
    first = pl.program_id(1) == 0
    kk = jnp.concatenate([kp_ref[...], kc_ref[...]], axis=0)
    vT = jnp.concatenate([vTp_ref[...], vTc_ref[...]], axis=1)
    prev_key = lax.broadcasted_iota(jnp.int32, (SWA_KEYS, G_A * WINDOW), 0) < WINDOW
    pw = 2 * HEAD_DIM
    for j in range(KV_A):
        kpair = kk[:, (j // 2) * pw:(j // 2 + 1) * pw].astype(cdt)
        qj = jnp.concatenate([qT_ref[(G_A * j + g) * HEAD_DIM:(G_A * j + g + 1) * HEAD_DIM, :]
                              for g in range(G_A)], axis=1)
        qj = (qj.astype(F32) * (HEAD_DIM ** -0.5)).astype(cdt)
        zpad = jnp.zeros_like(qj)
        rhs = jnp.concatenate([qj, zpad] if j % 2 == 0 else [zpad, qj], axis=0)
        s = jnp.dot(kpair, rhs, preferred_element_type=F32) + bias_ref[j]
        s = jnp.where(prev_key, jnp.where(first, NEG_BIG, s), s)
        sink = sink_ref[j]
        m = jnp.maximum(jnp.max(s, axis=0, keepdims=True), sink)
        e = jnp.exp(s - m)
        den = jnp.sum(e, axis=0, keepdims=True) + jnp.exp(sink - m)
        p = (e * (1.0 / den)).astype(cdt)
        acc = jnp.dot(vT[j * HEAD_DIM:(j + 1) * HEAD_DIM, :].astype(cdt), p, preferred_element_type=F32)
        for g in range(G_A):
            r0 = (G_A * j + g) * HEAD_DIM
            z = zT_ref[r0:r0 + HEAD_DIM, :]
            o_ref[r0:r0 + HEAD_DIM, :] = (acc[:, g * WINDOW:(g + 1) * WINDOW]
                                          * (z * (1.0 / (1.0 + jnp.exp(-z))))).astype(o_ref.dtype)


def swa_prompt(qvT, zT, kv_nat, bias_p, sink_p, *, n_batch, seq_len, cdt=BF16):
    nb = seq_len // WINDOW
    cur = lambda b, i: b * nb + i
    prev = lambda b, i: b * nb + jnp.maximum(i - 1, 0)
    v_row_blk = ATT_WIDTH // A_KV
    return pl.pallas_call(
        functools.partial(_swa_prompt_kernel, cdt=cdt),
        grid=(n_batch, nb),
        in_specs=[pl.BlockSpec((ATT_WIDTH, WINDOW), lambda b, i: (0, cur(b, i))),
                  pl.BlockSpec((ATT_WIDTH, WINDOW), lambda b, i: (0, cur(b, i))),
                  pl.BlockSpec((A_KV, WINDOW), lambda b, i: (v_row_blk, prev(b, i))),
                  pl.BlockSpec((A_KV, WINDOW), lambda b, i: (v_row_blk, cur(b, i))),
                  pl.BlockSpec((WINDOW, A_KV), lambda b, i: (prev(b, i), 0)),
                  pl.BlockSpec((WINDOW, A_KV), lambda b, i: (cur(b, i), 0)),
                  pl.BlockSpec(bias_p.shape, lambda b, i: (0, 0, 0)),
                  pl.BlockSpec(sink_p.shape, lambda b, i: (0, 0, 0))],
        out_specs=pl.BlockSpec((ATT_WIDTH, WINDOW), lambda b, i: (0, cur(b, i))),
        out_shape=jax.ShapeDtypeStruct((ATT_WIDTH, n_batch * seq_len), BF16),
        compiler_params=pltpu.CompilerParams(
            dimension_semantics=("parallel", "parallel"), vmem_limit_bytes=VMEM_LIMIT),
        name="swa_prompt",
    )(qvT, zT, qvT, qvT, kv_nat, kv_nat, bias_p, sink_p)


def _swa_sample_kernel(q_ref, z_ref, k_ref, v_ref, bias_ref, sink_ref, o_ref, *, cdt):
    for j in range(KV_A):
        s = lax.dot_general(q_ref[0, j], k_ref[0, j].astype(cdt), (((1,), (1,)), ((), ())),
                            preferred_element_type=F32) + bias_ref[j]
        sink = sink_ref[j][:, 0:1]
        m = jnp.maximum(jnp.max(s, axis=1, keepdims=True), sink)
        e = jnp.exp(s - m)
        den = jnp.sum(e, axis=1, keepdims=True) + jnp.exp(sink - m)
        p = (e * (1.0 / den)).astype(cdt)
        z = z_ref[0, j]
        o_ref[0, j] = jnp.dot(p, v_ref[0, j].astype(cdt), preferred_element_type=F32) * (z * (1.0 / (1.0 + jnp.exp(-z))))


def swa_sample(q_s, z_s, kv_s, kv_cache, bias_s, sink_s, cdt=BF16):
    n_dec = kv_cache.shape[0]
    dec_len = q_s.shape[0] // n_dec
    rows = G_A * dec_len

    def head_rows(t, scale):
        t = t.reshape(n_dec, dec_len, KV_A, G_A, HEAD_DIM).transpose(0, 2, 3, 1, 4) * scale
        return jnp.pad(t.reshape(n_dec, KV_A, rows, HEAD_DIM), ((0, 0), (0, 0), (0, 0), (0, HEAD_DIM)))

    new = kv_s.reshape(n_dec, dec_len, 2, KV_A, HEAD_DIM)
    cat = jnp.concatenate([kv_cache, new], axis=1)
    keys = jnp.pad(cat.transpose(2, 0, 3, 1, 4),
                   ((0, 0), (0, 0), (0, 0), (0, SWA_KEYS - WINDOW - dec_len), (0, HEAD_DIM)))
    blk = lambda r: pl.BlockSpec((1, KV_A, r, 2 * HEAD_DIM), lambda b: (b, 0, 0, 0))
    o = pl.pallas_call(
        functools.partial(_swa_sample_kernel, cdt=cdt),
        grid=(n_dec,),
        in_specs=[blk(rows), blk(rows), blk(SWA_KEYS), blk(SWA_KEYS),
                  pl.BlockSpec(bias_s.shape, lambda b: (0, 0, 0)),
                  pl.BlockSpec(sink_s.shape, lambda b: (0, 0, 0))],
        out_specs=blk(rows),
        out_shape=jax.ShapeDtypeStruct((n_dec, KV_A, rows, 2 * HEAD_DIM), F32),
        compiler_params=pltpu.CompilerParams(dimension_semantics=("parallel",), vmem_limit_bytes=VMEM_LIMIT),
        name="swa_sample",
    )(head_rows(q_s, HEAD_DIM ** -0.5).astype(cdt), head_rows(z_s, 1.0), keys[0], keys[1], bias_s, sink_s)
    o = o[..., :HEAD_DIM].reshape(n_dec, KV_A, G_A, dec_len, HEAD_DIM).transpose(0, 3, 1, 2, 4)
    return o.reshape(n_dec * dec_len, ATT_WIDTH), cat[:, dec_len:]


def _linear_combine(l, r):
    return (l[0] * r[0], r[0] * l[1] + r[1])


def s5_core(proj, h0, a_re, a_im, b_re, b_im, c_re, c_im, d_skip, log_dt, w_glu_bf):
    Bn, L, _ = proj.shape
    u, z = jnp.split(proj, 2, axis=-1)
    uf = u.reshape(Bn, L, S5_GROUPS, S5_GROUP)
    a = lax.complex(a_re, a_im)
    dt = jnp.exp(log_dt)[:, None]
    a_bar = jnp.exp(a * dt)
    b_bar = ((a_bar - 1.0) / a)[..., None] * lax.complex(b_re, b_im)
    c = lax.complex(c_re, c_im)
    bu = jnp.einsum('gpc,blgc->blgp', b_bar, uf.astype(jnp.complex64))
    if h0 is not None:
        h0c = lax.complex(h0[..., 0], h0[..., 1])
        bu = bu.at[:, 0].add(a_bar * h0c)
    a_seq = jnp.broadcast_to(a_bar, bu.shape)
    _, h = lax.associative_scan(_linear_combine, (a_seq, bu), axis=1)
    y = jnp.einsum('gcp,blgp->blgc', c, h).real + d_skip.reshape(S5_GROUPS, S5_GROUP) * uf
    y = jax.nn.gelu(y.reshape(Bn, L, S5_WIDTH))
    glu = matmul(y.reshape(Bn * L, S5_WIDTH).astype(BF16), w_glu_bf).reshape(Bn, L, S5_WIDTH)
    y = y * jax.nn.sigmoid(glu)
    h_last = h[:, -1]
    return y * jax.nn.silu(z), jnp.stack([h_last.real, h_last.imag], axis=-1)


S5_SLAB_G = 8
S5_SLAB_CH = S5_SLAB_G * S5_GROUP
S5_SLAB_ST = S5_SLAB_G * S5_STATE
S5_N_SLABS = S5_GROUPS // S5_SLAB_G
S5_CHAINS = 8
S5_HALF_CH = S5_CHAINS * S5_SLAB_CH
S5_T = 256
S5_LT = 2 * S5_SLAB_ST // LANE


def _gelu_tanh(x):
    return 0.5 * x * (1.0 + jnp.tanh(math.sqrt(2.0 / math.pi) * (x + 0.044715 * (x * x * x))))


def s5_tables(a_re, a_im, b_re, b_im, c_re, c_im, log_dt):
    a = lax.complex(a_re, a_im)
    dt = jnp.exp(log_dt)[:, None]
    a_bar = jnp.exp(a * dt)
    b_bar = ((a_bar - 1.0) / a)[..., None] * lax.complex(b_re, b_im)
    eye = jnp.eye(S5_SLAB_G, dtype=F32)

    def b_blk(t):
        t = t.reshape(S5_N_SLABS, S5_SLAB_G, S5_STATE, S5_GROUP)
        return jnp.einsum('ij,sipc->sicjp', eye, t).reshape(S5_N_SLABS, S5_SLAB_CH, S5_SLAB_ST)

    def c_blk(t):
        t = t.reshape(S5_N_SLABS, S5_SLAB_G, S5_GROUP, S5_STATE)
        return jnp.einsum('ij,sicp->sjpic', eye, t).reshape(S5_N_SLABS, S5_SLAB_ST, S5_SLAB_CH)

    bcat = jnp.concatenate([b_blk(b_bar.real), b_blk(b_bar.imag)], axis=2)
    ccat = jnp.concatenate([c_blk(c_re), -c_blk(c_im)], axis=1)
    a_cat = jnp.concatenate([a_bar.real.reshape(S5_N_SLABS, S5_SLAB_ST),
                             a_bar.imag.reshape(S5_N_SLABS, S5_SLAB_ST)], axis=1)
    return a_cat, bcat, ccat


def _s5_prompt_kernel(u_ref, bcat_ref, ccat_ref, a_ref, d_ref, y_ref, hout_ref, sc, h_sc, *, cdt):
    tc = pl.program_id(2)
    n_lt_half = S5_LT // 2

    @pl.when(tc == 0)
    def _():
        h_sc[...] = jnp.zeros_like(h_sc)

    for j in range(S5_CHAINS):
        uj = u_ref[:, j * S5_SLAB_CH:(j + 1) * S5_SLAB_CH].astype(cdt)
        bu = jnp.dot(uj, bcat_ref[0, j], preferred_element_type=F32)
        for lt in range(S5_LT):
            sc[lt, pl.ds(j, S5_T, stride=S5_CHAINS), :] = bu[:, lt * LANE:(lt + 1) * LANE]

    a_re = [a_ref[0, :, lt * LANE:(lt + 1) * LANE] for lt in range(n_lt_half)]
    a_im = [a_ref[0, :, (n_lt_half + lt) * LANE:(n_lt_half + lt + 1) * LANE] for lt in range(n_lt_half)]

    def step(t, h):
        r0 = pl.multiple_of(t * S5_CHAINS, S5_CHAINS)
        new = list(h)
        for lt in range(n_lt_half):
            hr, hi = h[lt], h[n_lt_half + lt]
            nr = a_re[lt] * hr - a_im[lt] * hi + sc[lt, pl.ds(r0, S5_CHAINS), :]
            ni = a_re[lt] * hi + a_im[lt] * hr + sc[n_lt_half + lt, pl.ds(r0, S5_CHAINS), :]
            sc[lt, pl.ds(r0, S5_CHAINS), :] = nr
            sc[n_lt_half + lt, pl.ds(r0, S5_CHAINS), :] = ni
            new[lt], new[n_lt_half + lt] = nr, ni
        return tuple(new)

    h = lax.fori_loop(0, S5_T, step, tuple(h_sc[lt] for lt in range(S5_LT)), unroll=8)
    for lt in range(S5_LT):
        h_sc[lt] = h[lt]
        hout_ref[0, 0, :, lt * LANE:(lt + 1) * LANE] = h[lt]

    for j in range(S5_CHAINS):
        hcat = jnp.concatenate([sc[lt, pl.ds(j, S5_T, stride=S5_CHAINS), :] for lt in range(S5_LT)], axis=1)
        cols = slice(j * S5_SLAB_CH, (j + 1) * S5_SLAB_CH)
        y = jnp.dot(hcat.astype(cdt), ccat_ref[0, j], preferred_element_type=F32) + d_ref[0, :, cols] * u_ref[:, cols]
        y_ref[:, cols] = _gelu_tanh(y)


def s5_prompt(proj, a_cat, bcat, ccat, d_skip, *, n_batch, seq_len, n_rows_out, cdt=BF16):
    n_t = seq_len // S5_T
    n_half = S5_WIDTH // S5_HALF_CH
    half = lambda t: t.reshape((n_half, S5_CHAINS) + t.shape[1:])
    return pl.pallas_call(
        functools.partial(_s5_prompt_kernel, cdt=cdt),
        grid=(n_batch, n_half, n_t),
        in_specs=[pl.BlockSpec((S5_T, S5_HALF_CH), lambda b, hf, t: (b * n_t + t, hf)),
                  pl.BlockSpec((1, S5_CHAINS, S5_SLAB_CH, 2 * S5_SLAB_ST), lambda b, hf, t: (hf, 0, 0, 0)),
                  pl.BlockSpec((1, S5_CHAINS, 2 * S5_SLAB_ST, S5_SLAB_CH), lambda b, hf, t: (hf, 0, 0, 0)),
                  pl.BlockSpec((1, S5_CHAINS, 2 * S5_SLAB_ST), lambda b, hf, t: (hf, 0, 0)),
                  pl.BlockSpec((1, 1, S5_HALF_CH), lambda b, hf, t: (hf, 0, 0))],
        out_specs=[pl.BlockSpec((S5_T, S5_HALF_CH), lambda b, hf, t: (b * n_t + t, hf)),
                   pl.BlockSpec((1, 1, S5_CHAINS, 2 * S5_SLAB_ST), lambda b, hf, t: (b, hf, 0, 0))],
        out_shape=[jax.ShapeDtypeStruct((n_rows_out, S5_WIDTH), F32),
                   jax.ShapeDtypeStruct((n_batch, n_half, S5_CHAINS, 2 * S5_SLAB_ST), F32)],
        scratch_shapes=[pltpu.VMEM((S5_LT, S5_T * S5_CHAINS, LANE), F32),
                        pltpu.VMEM((S5_LT, S5_CHAINS, LANE), F32)],
        compiler_params=pltpu.CompilerParams(
            dimension_semantics=("parallel", "parallel", "arbitrary"), vmem_limit_bytes=VMEM_LIMIT),
        name="s5_prompt",
    )(proj, half(bcat.astype(cdt)), half(ccat.astype(cdt)), half(a_cat), d_skip.reshape(n_half, 1, S5_HALF_CH))


def _s5_sample_kernel(u_ref, bcat_ref, ccat_ref, a_ref, d_ref, h0_ref, y_ref, hout_ref, sc, *, n_b, n_t, cdt):
    u = u_ref[...]
    bu = jnp.dot(u.astype(cdt), bcat_ref[0], preferred_element_type=F32)
    a_re = a_ref[0, :, :S5_SLAB_ST]
    a_im = a_ref[0, :, S5_SLAB_ST:]
    for bg in range(n_b // 8):
        hr = h0_ref[0, bg * 8:(bg + 1) * 8, :S5_SLAB_ST]
        hi = h0_ref[0, bg * 8:(bg + 1) * 8, S5_SLAB_ST:]
        for t in range(n_t):
            r = t * n_b + bg * 8
            hr, hi = (a_re * hr - a_im * hi + bu[r:r + 8, :S5_SLAB_ST],
                      a_re * hi + a_im * hr + bu[r:r + 8, S5_SLAB_ST:])
            sc[r:r + 8, :S5_SLAB_ST] = hr
            sc[r:r + 8, S5_SLAB_ST:] = hi
        hout_ref[0, bg * 8:(bg + 1) * 8, :S5_SLAB_ST] = hr
        hout_ref[0, bg * 8:(bg + 1) * 8, S5_SLAB_ST:] = hi
    y = jnp.dot(sc[...].astype(cdt), ccat_ref[0], preferred_element_type=F32) + d_ref[0] * u
    y_ref[...] = _gelu_tanh(y)


def s5_sample(u_tb, a_cat, bcat, ccat, d_skip, h0_cat, *, n_b, n_t, cdt=BF16):
    rows = n_t * n_b
    return pl.pallas_call(
        functools.partial(_s5_sample_kernel, n_b=n_b, n_t=n_t, cdt=cdt),
        grid=(S5_N_SLABS,),
        in_specs=[pl.BlockSpec((rows, S5_SLAB_CH), lambda s: (0, s)),
                  pl.BlockSpec((1, S5_SLAB_CH, 2 * S5_SLAB_ST), lambda s: (s, 0, 0)),
                  pl.BlockSpec((1, 2 * S5_SLAB_ST, S5_SLAB_CH), lambda s: (s, 0, 0)),
                  pl.BlockSpec((1, 1, 2 * S5_SLAB_ST), lambda s: (s, 0, 0)),
                  pl.BlockSpec((1, 1, S5_SLAB_CH), lambda s: (s, 0, 0)),
                  pl.BlockSpec((1, n_b, 2 * S5_SLAB_ST), lambda s: (s, 0, 0))],
        out_specs=[pl.BlockSpec((rows, S5_SLAB_CH), lambda s: (0, s)),
                   pl.BlockSpec((1, n_b, 2 * S5_SLAB_ST), lambda s: (s, 0, 0))],
        out_shape=[jax.ShapeDtypeStruct((rows, S5_WIDTH), F32),
                   jax.ShapeDtypeStruct((S5_N_SLABS, n_b, 2 * S5_SLAB_ST), F32)],
        scratch_shapes=[pltpu.VMEM((rows, 2 * S5_SLAB_ST), F32)],
        compiler_params=pltpu.CompilerParams(
            dimension_semantics=("arbitrary",), vmem_limit_bytes=VMEM_LIMIT),
        name="s5_sample",
    )(u_tb, bcat.astype(cdt), ccat.astype(cdt), a_cat.reshape(S5_N_SLABS, 1, 2 * S5_SLAB_ST),
      d_skip.reshape(S5_N_SLABS, 1, S5_SLAB_CH), h0_cat)


GLU_TM = 320
GLU_TN = 512


def _glu_kernel(yfull_ref, w_ref, ycol_ref, z_ref, o_ref, ybf_sc):
    @pl.when(pl.program_id(1) == 0)
    def _():
        ybf_sc[...] = yfull_ref[...].astype(ybf_sc.dtype)

    glu = jnp.dot(ybf_sc[...], w_ref[...], preferred_element_type=F32)
    z = z_ref[...]
    y = ycol_ref[...]
    o_ref[...] = (y * (1.0 / (1.0 + jnp.exp(-glu))) * (z * (1.0 / (1.0 + jnp.exp(-z))))).astype(o_ref.dtype)


def s5_glu_gate(y, w_glu, proj, row_off, cdt=BF16):
    m = y.shape[0]
    tm, tn = _pick_tile(m, (256, 128, 64, 32, 16)), GLU_TN
    assert row_off % tm == 0
    z_off, r_off = S5_WIDTH // tn, row_off // tm
    return pl.pallas_call(
        _glu_kernel,
        grid=(m // tm, S5_WIDTH // tn),
        in_specs=[pl.BlockSpec((tm, S5_WIDTH), lambda i, j: (i, 0)),
                  pl.BlockSpec((S5_WIDTH, tn), lambda i, j: (0, j)),
                  pl.BlockSpec((tm, tn), lambda i, j: (i, j)),
                  pl.BlockSpec((tm, tn), lambda i, j: (r_off + i, z_off + j))],
        out_specs=pl.BlockSpec((tm, tn), lambda i, j: (i, j)),
        out_shape=jax.ShapeDtypeStruct((m, S5_WIDTH), BF16),
        scratch_shapes=[pltpu.VMEM((tm, S5_WIDTH), cdt)],
        compiler_params=pltpu.CompilerParams(
            dimension_semantics=("parallel", "arbitrary"), vmem_limit_bytes=VMEM_LIMIT),
        name="s5_glu_gate",
    )(y, w_glu.astype(cdt), y, proj)


def s5_layer(proj, state_in, tables, d_skip, w_glu, *, n_batch, seq_len, n_dec, dec_len, cdt=BF16):
    a_cat, bcat, ccat = tables
    n_p = n_batch * seq_len
    n_s = n_dec * dec_len
    y_p, h_p = s5_prompt(proj, a_cat, bcat, ccat, d_skip, n_batch=n_batch, seq_len=seq_len, n_rows_out=n_p, cdt=cdt)
    u_tb = jnp.swapaxes(proj[n_p:, :S5_WIDTH].reshape(n_dec, dec_len, S5_WIDTH), 0, 1).reshape(n_s, S5_WIDTH)
    h0 = state_in.reshape(n_dec, S5_N_SLABS, S5_SLAB_ST, 2)
    h0_cat = jnp.concatenate([jnp.swapaxes(h0[..., 0], 0, 1), jnp.swapaxes(h0[..., 1], 0, 1)], axis=-1)
    y_tb, h_s = s5_sample(u_tb, a_cat, bcat, ccat, d_skip, h0_cat, n_b=n_dec, n_t=dec_len, cdt=cdt)
    y_s = jnp.swapaxes(y_tb.reshape(dec_len, n_dec, S5_WIDTH), 0, 1).reshape(n_s, S5_WIDTH)
    gated_p = s5_glu_gate(y_p, w_glu, proj, 0, cdt=cdt)
    gated_s = s5_glu_gate(y_s, w_glu, proj, n_p, cdt=cdt)
    hp = h_p.reshape(n_batch, S5_N_SLABS, 2, S5_SLAB_ST)
    st_p = jnp.stack([hp[:, :, 0], hp[:, :, 1]], axis=-1).reshape(n_batch, S5_GROUPS, S5_STATE, 2)
    hs = jnp.swapaxes(h_s, 0, 1).reshape(n_dec, S5_N_SLABS, 2, S5_SLAB_ST)
    st_s = jnp.stack([hs[:, :, 0], hs[:, :, 1]], axis=-1).reshape(n_dec, S5_GROUPS, S5_STATE, 2)
    return gated_p, gated_s, st_p, st_s


def l2_normalize(t, eps=1e-6):
    return t * lax.rsqrt(jnp.sum(t * t, axis=-1, keepdims=True) + eps)


def chunk_gated_delta(q, k, v, g, beta, S0):
    Bn, L, H, dk = k.shape
    dv = v.shape[-1]
    C = min(GDN_CHUNK, L)
    n = -(-L // C)
    pad = n * C - L

    def chunks(t):
        t = jnp.pad(t, [(0, 0), (0, pad)] + [(0, 0)] * (t.ndim - 2))
        t = t.reshape((Bn, n, C) + t.shape[2:])
        return jnp.moveaxis(t, 3, 2)

    qc, kc, vc, gc, bc = [chunks(t) for t in (q, k, v, g, beta)]
    gam = jnp.cumsum(gc, axis=-1)
    pos = jnp.arange(C)
    causal = pos[:, None] >= pos[None, :]
    strict = pos[:, None] > pos[None, :]
    decay = jnp.exp(jnp.where(causal, gam[..., :, None] - gam[..., None, :], -jnp.inf))
    kk = jnp.einsum('bnhid,bnhjd->bnhij', kc, kc)
    tri = jnp.eye(C, dtype=F32) + jnp.where(strict, bc[..., :, None] * kk * decay, 0.0)
    rhs = jnp.concatenate([bc[..., None] * vc, (bc * jnp.exp(gam))[..., None] * kc], axis=-1)
    sol = lax.linalg.triangular_solve(tri, rhs, left_side=True, lower=True, unit_diagonal=True)
    u, w = sol[..., :dv], sol[..., dv:]
    qk = jnp.einsum('bnhid,bnhjd->bnhij', qc, kc) * decay
    q_dec = qc * jnp.exp(gam)[..., None]
    k_dec = kc * jnp.exp(gam[..., -1:] - gam)[..., None]
    g_tot = jnp.exp(gam[..., -1])

    def step(S, xs):
        u_c, w_c, qk_c, qd_c, kd_c, gt_c = xs
        v_new = u_c - jnp.einsum('bhcd,bhde->bhce', w_c, S)
        o = jnp.einsum('bhcd,bhde->bhce', qd_c, S) + jnp.einsum('bhij,bhje->bhie', qk_c, v_new)
        S = S * gt_c[..., None, None] + jnp.einsum('bhcd,bhce->bhde', kd_c, v_new)
        return S, o

    xs = tuple(jnp.moveaxis(t, 1, 0) for t in (u, w, qk, q_dec, k_dec, g_tot))
    S, o = lax.scan(step, S0, xs)
    o = o.transpose(1, 0, 3, 2, 4).reshape(Bn, n * C, H, dv)[:, :L]
    return o, S


def gdn_core(qkv, z, ab, S0, conv_buf, conv_w, a_log, dt_bias, norm_w):
    Bn, L, _ = qkv.shape
    a, b = ab[..., :GDN_V_HEADS], ab[..., GDN_V_HEADS:2 * GDN_V_HEADS]
    if conv_buf is None:
        conv_buf = jnp.zeros((Bn, GDN_CONV - 1, GDN_CONV_CH), qkv.dtype)
    xx = jnp.concatenate([conv_buf, qkv], axis=1)
    conv = jax.nn.silu(sum(xx[:, j:j + L] * conv_w[j] for j in range(GDN_CONV)))
    new_buf = xx[:, L:]
    q, k, v = jnp.split(conv, [GDN_QK_WIDTH, 2 * GDN_QK_WIDTH], axis=-1)
    rep = GDN_V_HEADS // GDN_QK_HEADS
    q = jnp.repeat(l2_normalize(q.reshape(Bn, L, GDN_QK_HEADS, GDN_DK)), rep, axis=2) * (GDN_DK ** -0.5)
    k = jnp.repeat(l2_normalize(k.reshape(Bn, L, GDN_QK_HEADS, GDN_DK)), rep, axis=2)
    v = v.reshape(Bn, L, GDN_V_HEADS, GDN_DV)
    beta = jax.nn.sigmoid(b)
    g = -jnp.exp(a_log) * jax.nn.softplus(a + dt_bias)
    if S0 is None:
        S0 = jnp.zeros((Bn, GDN_V_HEADS, GDN_DK, GDN_DV), F32)
    o, S = chunk_gated_delta(q, k, v, g, beta, S0)
    of = o * lax.rsqrt(jnp.mean(o * o, axis=-1, keepdims=True) + 1e-6) * norm_w
    of = of * jax.nn.silu(z.reshape(Bn, L, GDN_V_HEADS, GDN_DV))
    return of.reshape(Bn, L, GDN_V_WIDTH), S, new_buf


GDN_CONV_TT = 256
GDN_CONV_CW = 1024
GDN_HIST = 8
GDN_HB = 4
GDN_TT = 256
GDN_SAMPLE_ROWS = 8


def _gdn_conv_kernel(x_ref, hist_ref, w_ref, o_ref, *, rows, n_t, zero_first):
    i, j = pl.program_id(0), pl.program_id(1)
    hist = hist_ref[...]
    if zero_first:
        hist = jnp.where(i % n_t == 0, 0.0, hist)
    ext = jnp.concatenate([hist, x_ref[...]], axis=0)
    acc = ext[GDN_HIST:GDN_HIST + rows] * w_ref[GDN_CONV - 1:GDN_CONV, :]
    for s in range(1, GDN_CONV):
        acc = acc + ext[GDN_HIST - s:GDN_HIST - s + rows] * w_ref[GDN_CONV - 1 - s:GDN_CONV - s, :]
    conv = acc * (1.0 / (1.0 + jnp.exp(-acc)))
    n_qk_blocks = 2 * GDN_QK_WIDTH // GDN_CONV_CW

    @pl.when(j >= n_qk_blocks)
    def _():
        o_ref[...] = conv

    @pl.when(j < n_qk_blocks)
    def _():
        scale = jnp.where(j < GDN_QK_WIDTH // GDN_CONV_CW, GDN_DK ** -0.5, 1.0)
        for h in range(GDN_CONV_CW // GDN_DK):
            t = conv[:, h * GDN_DK:(h + 1) * GDN_DK]
            n = t * lax.rsqrt(jnp.sum(t * t, axis=-1, keepdims=True) + 1e-6)
            o_ref[:, h * GDN_DK:(h + 1) * GDN_DK] = n * scale


def gdn_conv(x, hist_src, conv_w, *, rows, n_blocks, n_t, data_map, hist_map, zero_first):
    n_out = n_blocks * rows
    return pl.pallas_call(
        functools.partial(_gdn_conv_kernel, rows=rows, n_t=n_t, zero_first=zero_first),
        grid=(n_blocks, GDN_CONV_CH // GDN_CONV_CW),
        in_specs=[pl.BlockSpec((rows, GDN_CONV_CW), lambda i, j: (data_map(i), j)),
                  pl.BlockSpec((GDN_HIST, GDN_CONV_CW), lambda i, j: (hist_map(i), j)),
                  pl.BlockSpec((GDN_CONV, GDN_CONV_CW), lambda i, j: (0, j))],
        out_specs=pl.BlockSpec((rows, GDN_CONV_CW), lambda i, j: (i, j)),
        out_shape=jax.ShapeDtypeStruct((n_out, GDN_CONV_CH), F32),
        compiler_params=pltpu.CompilerParams(
            dimension_semantics=("parallel", "parallel"), vmem_limit_bytes=VMEM_LIMIT),
        name="gdn_conv",
    )(x, hist_src, conv_w)


def _gdn_chunk_kernel(q_ref, k_ref, v_ref, z_ref, ab_ref, alog_ref, dtb_ref, nw_ref, s0_ref, o_ref, sout_ref,
                      s_sc, *, chunk, n_inner, n_tt, valid_len):
    C = chunk
    hb, tt = pl.program_id(1), pl.program_id(2)

    @pl.when(tt == 0)
    def _():
        s_sc[...] = s0_ref[0]

    rowi = lax.broadcasted_iota(jnp.int32, (C, C), 0)
    coli = lax.broadcasted_iota(jnp.int32, (C, C), 1)
    causal = rowi >= coli
    strict = rowi > coli
    ltri = jnp.where(causal, 1.0, 0.0)
    utri = jnp.where(rowi <= coli, 1.0, 0.0)
    eye = jnp.where(rowi == coli, 1.0, 0.0)
    hi = lax.Precision.HIGHEST
    shift = (LANE - hb * GDN_HB) % LANE
    alog = pltpu.roll(jnp.broadcast_to(alog_ref[...], (8, LANE)), shift, 1)[0:1]
    dtb = pltpu.roll(jnp.broadcast_to(dtb_ref[...], (8, LANE)), shift, 1)[0:1]
    nw = nw_ref[...]
    tok_valid = lax.broadcasted_iota(jnp.int32, (C, LANE), 0) < valid_len

    def chunk_body(c, carry):
        r0 = pl.multiple_of(c * C, C)
        ab = pltpu.roll(ab_ref[pl.ds(r0, C), :], shift, 1)
        xa = ab + dtb
        softplus = jnp.maximum(xa, 0.0) + jnp.log1p(jnp.exp(-jnp.abs(xa)))
        g_all = jnp.where(tok_valid, -jnp.exp(alog) * softplus, 0.0)
        beta_all = jnp.where(tok_valid, 1.0 / (1.0 + jnp.exp(-ab)), 0.0)
        gam_all = jnp.dot(ltri, g_all, preferred_element_type=F32, precision=hi)
        gamT_all = lax.dot_general(g_all, utri, (((0,), (0,)), ((), ())),
                                   preferred_element_type=F32, precision=hi)
        for i in range(GDN_HB):
            qk_cols = slice((i // 2) * GDN_DK, (i // 2 + 1) * GDN_DK)
            v_cols = slice(i * GDN_DV, (i + 1) * GDN_DV)
            qh = q_ref[pl.ds(r0, C), qk_cols]
            kh = k_ref[pl.ds(r0, C), qk_cols]
            vh = v_ref[pl.ds(r0, C), v_cols]
            gam_c = jnp.broadcast_to(gam_all[:, i:i + 1], (C, LANE))
            gam_r = jnp.broadcast_to(gamT_all[i:i + 1, :], (C, C))
            beta_c = jnp.broadcast_to(beta_all[:, 32 + i:33 + i], (C, LANE))
            gam_last = jnp.broadcast_to(gam_all[C - 1:C, i:i + 1], (1, LANE))
            decay = jnp.where(causal, jnp.exp(jnp.where(causal, gam_c[:, :C] - gam_r, 0.0)), 0.0)
            kb = kh.astype(BF16)
            qkk = lax.dot_general(jnp.concatenate([qh, kh], axis=0).astype(BF16), kb, (((1,), (1,)), ((), ())),
                                  preferred_element_type=F32)
            qk = qkk[:C] * decay
            neg_a = jnp.where(strict, -(beta_c[:, :C] * qkk[C:] * decay), 0.0)
            p_inv = eye + neg_a
            m_pow = neg_a
            for _ in range(int(math.log2(C)) - 1):
                m_bf = m_pow.astype(BF16)
                m_pow = jnp.dot(m_bf, m_bf, preferred_element_type=F32)
                p_inv = p_inv + jnp.dot(p_inv.astype(BF16), m_pow.astype(BF16), preferred_element_type=F32)
            eg = jnp.exp(gam_c)
            rhs = jnp.concatenate([beta_c * vh, (beta_c * eg) * kh], axis=1)
            sol = jnp.dot(p_inv.astype(BF16), rhs.astype(BF16), preferred_element_type=F32)
            u, w = sol[:, :GDN_DV], sol[:, GDN_DV:]
            s_old = s_sc[i]
            s_bf = s_old.astype(BF16)
            ws = jnp.dot(jnp.concatenate([w, qh * eg], axis=0).astype(BF16), s_bf, preferred_element_type=F32)
            v_new = u - ws[:C]
            vn_bf = v_new.astype(BF16)
            o = ws[C:] + jnp.dot(qk.astype(BF16), vn_bf, preferred_element_type=F32)
            k_dec = kh * jnp.exp(gam_last - gam_c)
            s_sc[i] = s_old * jnp.exp(gam_last) + lax.dot_general(
                k_dec.astype(BF16), vn_bf, (((0,), (0,)), ((), ())), preferred_element_type=F32)
            rms = lax.rsqrt(jnp.mean(o * o, axis=-1, keepdims=True) + 1e-6)
            zz = z_ref[pl.ds(r0, C), v_cols]
            o_ref[pl.ds(r0, C), v_cols] = (o * rms * nw * (zz * (1.0 / (1.0 + jnp.exp(-zz))))).astype(o_ref.dtype)
        return carry

    lax.fori_loop(0, n_inner, chunk_body, 0)

    @pl.when(tt == n_tt - 1)
    def _():
        sout_ref[0] = s_sc[...]


def _gdn_chunk_lockstep_kernel(q_ref, k_ref, v_ref, z_ref, ab_ref, alog_ref, dtb_ref, nw_ref, s0_ref, o_ref,
                               sout_ref, s_sc, *, chunk, n_inner, n_tt, valid_len):
    C = chunk
    hb, tt = pl.program_id(1), pl.program_id(2)

    @pl.when(tt == 0)
    def _():
        s_sc[...] = s0_ref[0]

    rowi = lax.broadcasted_iota(jnp.int32, (C, C), 0)
    coli = lax.broadcasted_iota(jnp.int32, (C, C), 1)
    causal = rowi >= coli
    strict = rowi > coli
    ltri = jnp.where(causal, 1.0, 0.0)
    utri = jnp.where(rowi <= coli, 1.0, 0.0)
    eye = jnp.where(rowi == coli, 1.0, 0.0)
    hi = lax.Precision.HIGHEST
    shift = (LANE - hb * GDN_HB) % LANE
    alog = pltpu.roll(jnp.broadcast_to(alog_ref[...], (8, LANE)), shift, 1)[0:1]
    dtb = pltpu.roll(jnp.broadcast_to(dtb_ref[...], (8, LANE)), shift, 1)[0:1]
    nw = nw_ref[...]
    tok_valid = lax.broadcasted_iota(jnp.int32, (C, LANE), 0) < valid_len
    dot = functools.partial(jnp.dot, preferred_element_type=F32)
    dot_nt = lambda a, b: lax.dot_general(a, b, (((1,), (1,)), ((), ())), preferred_element_type=F32)
    dot_tn = lambda a, b: lax.dot_general(a, b, (((0,), (0,)), ((), ())), preferred_element_type=F32)
    units = [(c, i) for c in range(n_inner) for i in range(GDN_HB)]
    rows = lambda c: slice(c * C, (c + 1) * C)
    qk_cols = lambda i: slice((i // 2) * GDN_DK, (i // 2 + 1) * GDN_DK)
    v_cols = lambda i: slice(i * GDN_DV, (i + 1) * GDN_DV)

    g_all, beta_all = [], []
    for c in range(n_inner):
        ab = pltpu.roll(ab_ref[rows(c), :], shift, 1)
        xa = ab + dtb
        softplus = jnp.maximum(xa, 0.0) + jnp.log1p(jnp.exp(-jnp.abs(xa)))
        g_all.append(jnp.where(tok_valid, -jnp.exp(alog) * softplus, 0.0))
        beta_all.append(jnp.where(tok_valid, 1.0 / (1.0 + jnp.exp(-ab)), 0.0))
    gam_all = [jnp.dot(ltri, g, preferred_element_type=F32, precision=hi) for g in g_all]
    gamT_all = [lax.dot_general(g, utri, (((0,), (0,)), ((), ())), preferred_element_type=F32, precision=hi)
                for g in g_all]

    qkk = [dot_nt(jnp.concatenate([q_ref[rows(c), qk_cols(i)], k_ref[rows(c), qk_cols(i)]], axis=0).astype(BF16),
                  k_ref[rows(c), qk_cols(i)].astype(BF16)) for c, i in units]
    gam_c = [jnp.broadcast_to(gam_all[c][:, i:i + 1], (C, LANE)) for c, i in units]
    beta_c = [jnp.broadcast_to(beta_all[c][:, 32 + i:33 + i], (C, LANE)) for c, i in units]
    gam_last = [jnp.broadcast_to(gam_all[c][C - 1:C, i:i + 1], (1, LANE)) for c, i in units]
    decay = [jnp.where(causal, jnp.exp(jnp.where(causal, gc[:, :C] - jnp.broadcast_to(gamT_all[c][i:i + 1, :], (C, C)),
                                                 0.0)), 0.0) for gc, (c, i) in zip(gam_c, units)]
    qk = [(x[:C] * d).astype(BF16) for x, d in zip(qkk, decay)]
    neg_a = [jnp.where(strict, -(b[:, :C] * x[C:] * d), 0.0) for b, x, d in zip(beta_c, qkk, decay)]
    p_inv = [eye + n for n in neg_a]
    m_pow = neg_a
    for _ in range(int(math.log2(C)) - 1):
        m_pow = [dot(m.astype(BF16), m.astype(BF16)) for m in m_pow]
        p_inv = [p + dot(p.astype(BF16), m.astype(BF16)) for p, m in zip(p_inv, m_pow)]
    eg = [jnp.exp(gc) for gc in gam_c]
    sol = [dot(p.astype(BF16), jnp.concatenate([b * v_ref[rows(c), v_cols(i)], (b * e) * k_ref[rows(c), qk_cols(i)]],
                                               axis=1).astype(BF16))
           for p, b, e, (c, i) in zip(p_inv, beta_c, eg, units)]
    wq = [jnp.concatenate([s[:, GDN_DV:], q_ref[rows(c), qk_cols(i)] * e], axis=0).astype(BF16)
          for s, e, (c, i) in zip(sol, eg, units)]
    k_dec = [(k_ref[rows(c), qk_cols(i)] * jnp.exp(gl - gc)).astype(BF16)
             for gl, gc, (c, i) in zip(gam_last, gam_c, units)]

    state = [s_sc[i] for i in range(GDN_HB)]
    for c in range(n_inner):
        base = c * GDN_HB
        ws = [dot(wq[base + i], state[i].astype(BF16)) for i in range(GDN_HB)]
        v_new = [(sol[base + i][:, :GDN_DV] - ws[i][:C]).astype(BF16) for i in range(GDN_HB)]
        o = [ws[i][C:] + dot(qk[base + i], v_new[i]) for i in range(GDN_HB)]
        state = [state[i] * jnp.exp(gam_last[base + i]) + dot_tn(k_dec[base + i], v_new[i]) for i in range(GDN_HB)]
        for i in range(GDN_HB):
            rms = lax.rsqrt(jnp.mean(o[i] * o[i], axis=-1, keepdims=True) + 1e-6)
            zz = z_ref[rows(c), v_cols(i)]
            o_ref[rows(c), v_cols(i)] = (o[i] * rms * nw * (zz * (1.0 / (1.0 + jnp.exp(-zz))))).astype(o_ref.dtype)
    for i in range(GDN_HB):
        s_sc[i] = state[i]

    @pl.when(tt == n_tt - 1)
    def _():
        sout_ref[0] = s_sc[...]


def gdn_chunk(conv, z, ab, a_log, dt_bias, norm_w, s0, *, n_seq, rows_per_seq, rows_per_step, chunk, valid_len,
              z_col_off, out_dtype):
    n_tt = rows_per_seq // rows_per_step
    n_inner = rows_per_step // chunk
    n_hb = GDN_V_HEADS // GDN_HB
    qw, vw = GDN_HB // 2 * GDN_DK, GDN_HB * GDN_DV
    k_off, v_off, z_off = GDN_QK_WIDTH // qw, 2 * GDN_QK_WIDTH // vw, z_col_off // vw
    row = lambda b, hb, t: b * n_tt + t
    pad_row = lambda p: jnp.pad(p.astype(F32), (0, LANE - p.shape[0])).reshape(1, LANE)
    return pl.pallas_call(
        functools.partial(_gdn_chunk_lockstep_kernel, chunk=chunk, n_inner=n_inner, n_tt=n_tt, valid_len=valid_len),
        grid=(n_seq, n_hb, n_tt),
        in_specs=[pl.BlockSpec((rows_per_step, qw), lambda b, hb, t: (row(b, hb, t), hb)),
                  pl.BlockSpec((rows_per_step, qw), lambda b, hb, t: (row(b, hb, t), k_off + hb)),
                  pl.BlockSpec((rows_per_step, vw), lambda b, hb, t: (row(b, hb, t), v_off + hb)),
                  pl.BlockSpec((rows_per_step, vw), lambda b, hb, t: (row(b, hb, t), z_off + hb)),
                  pl.BlockSpec((rows_per_step, LANE), lambda b, hb, t: (row(b, hb, t), 0)),
                  pl.BlockSpec((1, LANE), lambda b, hb, t: (0, 0)),
                  pl.BlockSpec((1, LANE), lambda b, hb, t: (0, 0)),
                  pl.BlockSpec((1, GDN_DV), lambda b, hb, t: (0, 0)),
                  pl.BlockSpec((1, GDN_HB, GDN_DK, GDN_DV), lambda b, hb, t: (b, hb, 0, 0))],
        out_specs=[pl.BlockSpec((rows_per_step, vw), lambda b, hb, t: (row(b, hb, t), hb)),
                   pl.BlockSpec((1, GDN_HB, GDN_DK, GDN_DV), lambda b, hb, t: (b, hb, 0, 0))],
        out_shape=[jax.ShapeDtypeStruct((n_seq * rows_per_seq, GDN_V_WIDTH), out_dtype),
                   jax.ShapeDtypeStruct((n_seq, GDN_V_HEADS, GDN_DK, GDN_DV), F32)],
        scratch_shapes=[pltpu.VMEM((GDN_HB, GDN_DK, GDN_DV), F32)],
        compiler_params=pltpu.CompilerParams(
            dimension_semantics=("parallel", "parallel", "arbitrary"), vmem_limit_bytes=VMEM_LIMIT),
        name="gdn_chunk",
    )(conv, conv, conv, z, ab, pad_row(a_log), pad_row(dt_bias), norm_w.astype(F32).reshape(1, GDN_DV), s0)


def gdn_layer(qkvz, ab, state_in, conv_in, conv_w, a_log, dt_bias, norm_w, *, n_batch, seq_len, n_dec, dec_len):
    n_p = n_batch * seq_len
    n_tp = seq_len // GDN_CONV_TT
    hist_per_block = GDN_CONV_TT // GDN_HIST
    conv_p = gdn_conv(qkvz, qkvz, conv_w, rows=GDN_CONV_TT, n_blocks=n_batch * n_tp, n_t=n_tp,
                      data_map=lambda i: i, hist_map=lambda i: jnp.maximum(i * hist_per_block - 1, 0),
                      zero_first=True)
    zeros_s = jnp.zeros((n_batch, GDN_V_HEADS, GDN_DK, GDN_DV), F32)
    gated_p, st_p = gdn_chunk(conv_p, qkvz, ab, a_log, dt_bias, norm_w, zeros_s, n_seq=n_batch,
                              rows_per_seq=seq_len, rows_per_step=GDN_TT, chunk=GDN_CHUNK, valid_len=GDN_CHUNK,
                              z_col_off=GDN_CONV_CH, out_dtype=BF16)
    buf_p = qkvz[:n_p, :GDN_CONV_CH].reshape(n_batch, seq_len, GDN_CONV_CH)[:, seq_len - (GDN_CONV - 1):]
    R = GDN_SAMPLE_ROWS
    x_s = qkvz[n_p:].reshape(n_dec, dec_len, -1)
    pad_t = lambda t, front: jnp.pad(t, ((0, 0), (front, R - front - t.shape[1]), (0, 0)))
    ext = jnp.concatenate([pad_t(conv_in, R - (GDN_CONV - 1)), pad_t(x_s[..., :GDN_CONV_CH], 0)], axis=1)
    ext = ext.reshape(n_dec * 2 * R, GDN_CONV_CH)
    conv_s = gdn_conv(ext, ext, conv_w, rows=R, n_blocks=n_dec, n_t=1,
                      data_map=lambda i: 2 * i + 1, hist_map=lambda i: 2 * i, zero_first=False)
    z_s = pad_t(x_s[..., GDN_CONV_CH:], 0).reshape(n_dec * R, GDN_V_WIDTH)
    ab_s = pad_t(ab[n_p:].reshape(n_dec, dec_len, LANE), 0).reshape(n_dec * R, LANE)
    gated_s, st_s = gdn_chunk(conv_s, z_s, ab_s, a_log, dt_bias, norm_w, state_in, n_seq=n_dec, rows_per_seq=R,
                              rows_per_step=R, chunk=R, valid_len=dec_len, z_col_off=0, out_dtype=F32)
    gated_s = gated_s.reshape(n_dec, R, GDN_V_WIDTH)[:, :dec_len].reshape(n_dec * dec_len, GDN_V_WIDTH)
    buf_s = jnp.concatenate([conv_in, x_s[..., :GDN_CONV_CH]], axis=1)[:, dec_len:]
    return gated_p, gated_s, st_p, st_s, buf_p, buf_s


def index_scores(qi, ki, wi):
    s = jnp.einsum('bthd,bsd->bths', qi, ki).astype(F32) * (IDX_DIM ** -0.5)
    return jnp.einsum('bths,bth->bts', jax.nn.relu(s), wi * (IDX_HEADS ** -0.5))


def gathered_attend(q, kvs, qpos, kpos, valid, rel_bias):
    n_kv, g = q.shape[-3], q.shape[-2]
    ks, vs = kvs[..., 0, :, :], kvs[..., 1, :, :]
    logits = jnp.einsum('bthgd,btkhd->bhgtk', q, ks).astype(F32) * (HEAD_DIM ** -0.5)
    dist = qpos[None, :, None] - kpos
    logits = logits + head_bias(rel_bias, dist, n_kv, g)
    p = masked_softmax(logits, (valid & (dist >= 0))[:, None, None])
    return jnp.einsum('bhgtk,btkhd->bthgd', p, vs)


def dsa_sample_core(q, kv, z, qi, ki, wi, kv_pool, kidx_pool, layer, page_table, rel_bias):
    Bd, L, _ = q.shape
    n_pages = page_table.shape[1]
    past = n_pages * PAGE_SIZE
    q = q.reshape(Bd, L, KV_D, N_HEADS // KV_D, HEAD_DIM)
    kv = kv.reshape(Bd, L, 2, KV_D, HEAD_DIM)
    qi = qi.reshape(Bd, L, IDX_HEADS, IDX_DIM)
    ki_past = kidx_pool[layer, page_table].reshape(Bd, past, IDX_DIM)
    ki_all = jnp.concatenate([ki_past, ki], axis=1)
    total = past + L
    topk = min(TOPK_MAX, total // 4)
    qpos = past + jnp.arange(L)
    sc = index_scores(qi, ki_all, wi)
    sc = jnp.where(jnp.arange(total)[None, None, :] <= qpos[None, :, None], sc, -jnp.inf)
    vals, idx = lax.top_k(sc, topk)
    pidx = jnp.minimum(idx, past - 1)
    phys = jnp.take_along_axis(page_table, (pidx // PAGE_SIZE).reshape(Bd, -1), axis=1).reshape(idx.shape)
    kv_past = kv_pool[layer, phys, pidx % PAGE_SIZE]
    kv_new = take_rows(kv, jnp.clip(idx - past, 0, L - 1))
    kv_sel = jnp.where((idx >= past)[..., None, None, None], kv_new, kv_past)
    o = gathered_attend(q, kv_sel, qpos, idx, vals > -jnp.inf, rel_bias).reshape(Bd, L, ATT_WIDTH)
    return o * jax.nn.silu(z)


DSA_KC = 256
INT_MIN = -2 ** 31
NEG_BIG = -1e30
G_D = N_HEADS // KV_D
BIAS_WIN = DSA_KC + Q_BLOCK


def _sortable_key(s):
    b = pltpu.bitcast(s, jnp.int32)
    return jnp.where(b < 0, b ^ jnp.int32(0x7FFFFFFF), b)


def _dsa_prompt_kernel(qT_ref, qiT_ref, wiT_ref, zT_ref, ki_ref, k_ref, vT_ref, win_ref, o_ref,
                       key_sc, mask_sc, *, topk, idx_bits, cdt):
    qb = pl.program_id(1)
    t0 = qb * Q_BLOCK
    nch = (qb + 2) // 2
    t_idx = t0 + lax.broadcasted_iota(jnp.int32, (1, Q_BLOCK), 1)
    row_iota = lax.broadcasted_iota(jnp.int32, (DSA_KC, Q_BLOCK), 0)

    def score_chunk(c, carry):
        kic = ki_ref[0, c].astype(cdt)
        acc = jnp.zeros((DSA_KC, Q_BLOCK), F32)
        for hp in range(IDX_HEADS // 2):
            rhs = jnp.concatenate([qiT_ref[(2 * hp) * IDX_DIM:(2 * hp + 1) * IDX_DIM, :],
                                   qiT_ref[(2 * hp + 1) * IDX_DIM:(2 * hp + 2) * IDX_DIM, :]], axis=1)
            s = jnp.dot(kic, rhs, preferred_element_type=F32) * (IDX_DIM ** -0.5)
            s = jnp.maximum(s, 0.0)
            w0 = wiT_ref[2 * hp:2 * hp + 1, :] * (IDX_HEADS ** -0.5)
            w1 = wiT_ref[2 * hp + 1:2 * hp + 2, :] * (IDX_HEADS ** -0.5)
            acc = acc + s[:, :Q_BLOCK] * w0 + s[:, Q_BLOCK:] * w1
        s_idx = c * DSA_KC + row_iota
        key_sc[c] = jnp.where(s_idx <= t_idx, _sortable_key(acc), INT_MIN)
        return carry

    lax.fori_loop(0, nch, score_chunk, 0)

    def count(pred):
        def body(c, acc):
            hit = pred(key_sc[c], c * DSA_KC + row_iota)
            return acc + hit.reshape(DSA_KC // 8, 8, Q_BLOCK).sum(axis=0)
        acc = lax.fori_loop(0, nch, body, jnp.zeros((8, Q_BLOCK), jnp.int32))
        return jnp.sum(acc, axis=0, keepdims=True)

    c_nonneg = count(lambda k, s: jnp.where(k >= 0, 1, 0))
    thr = jnp.where(c_nonneg >= topk, 0, INT_MIN).astype(jnp.int32)

    def thr_bit(i, thr):
        cand = thr + jnp.left_shift(jnp.int32(1), 30 - i)
        return jnp.where(count(lambda k, s: jnp.where(k >= cand, 1, 0)) >= topk, cand, thr)

    thr = lax.fori_loop(0, 31, thr_bit, thr)
    need = topk - count(lambda k, s: jnp.where(k > thr, 1, 0))

    def lim_bit(i, lim):
        cand = lim + jnp.left_shift(jnp.int32(1), idx_bits - 1 - i)
        c = count(lambda k, s: jnp.where(k == thr, jnp.where(s < cand, 1, 0), 0))
        return jnp.where(c <= need, cand, lim)

    lim = lax.fori_loop(0, idx_bits, lim_bit, jnp.zeros((1, Q_BLOCK), jnp.int32))

    def mask_chunk(c, carry):
        k = key_sc[c]
        s_idx = c * DSA_KC + row_iota
        tie = jnp.where(k == thr, jnp.where(s_idx < lim, 0.0, NEG_BIG), NEG_BIG)
        m = jnp.where(k > thr, 0.0, tie)
        mask_sc[c] = jnp.where(k == INT_MIN, NEG_BIG, m)
        return carry

    lax.fori_loop(0, nch, mask_chunk, 0)

    for j in range(KV_D):
        jp = j // 2
        qj = jnp.concatenate([qT_ref[(G_D * j + g) * HEAD_DIM:(G_D * j + g + 1) * HEAD_DIM, :]
                              for g in range(G_D)], axis=1)
        qj = (qj.astype(F32) * (HEAD_DIM ** -0.5)).astype(cdt)
        zpad = jnp.zeros_like(qj)
        rhs = jnp.concatenate([qj, zpad] if j % 2 == 0 else [zpad, qj], axis=0)

        def chunk_body(c, carry, j=j, jp=jp, rhs=rhs):
            m, l, acc = carry
            kc = k_ref[0, c, :, jp * 2 * HEAD_DIM:(jp + 1) * 2 * HEAD_DIM].astype(cdt)
            s = jnp.dot(kc, rhs, preferred_element_type=F32)
            wt = win_ref[qb - 2 * c]
            madd = mask_sc[c]
            parts = []
            for g in range(G_D):
                h = G_D * j + g
                r = jnp.broadcast_to(wt[h:h + 1, :], (DSA_KC, BIAS_WIN))
                b = pltpu.roll(r, 0, 1, stride=1, stride_axis=0)[:, DSA_KC:]
                parts.append(s[:, g * Q_BLOCK:(g + 1) * Q_BLOCK] + (b + madd))
            s = jnp.concatenate(parts, axis=1)
            m_new = jnp.maximum(m, jnp.max(s, axis=0, keepdims=True))
            alpha = jnp.exp(m - m_new)
            p = jnp.exp(s - m_new)
            l = l * alpha + jnp.sum(p, axis=0, keepdims=True)
            vt = vT_ref[0, c, j * HEAD_DIM:(j + 1) * HEAD_DIM, :]
            acc = acc * alpha + jnp.dot(vt, p.astype(cdt), preferred_element_type=F32)
            return m_new, l, acc

        init = (jnp.full((1, G_D * Q_BLOCK), NEG_BIG, F32), jnp.zeros((1, G_D * Q_BLOCK), F32),
                jnp.zeros((HEAD_DIM, G_D * Q_BLOCK), F32))
        m, l, acc = lax.fori_loop(0, nch, chunk_body, init)
        o = acc * (1.0 / l)
        for g in range(G_D):
            r0 = (G_D * j + g) * HEAD_DIM
            z = zT_ref[r0:r0 + HEAD_DIM, :]
            gate = z * (1.0 / (1.0 + jnp.exp(-z)))
            o_ref[r0:r0 + HEAD_DIM, :] = (o[:, g * Q_BLOCK:(g + 1) * Q_BLOCK] * gate).astype(o_ref.dtype)


def dsa_bias_windows(rel_bias, seq_len):
    o = jnp.arange(seq_len // Q_BLOCK)[:, None]
    m = jnp.arange(BIAS_WIN)[None, :]
    d = jnp.maximum(o * Q_BLOCK + m - DSA_KC, 0)
    return jnp.moveaxis(rel_bias[rel_bucket(d)].astype(F32), -1, 1)


def dsa_prompt_attend(qqiT, wiT, zT, ki4, k4, v4T, win, *, n_batch, seq_len, cdt=BF16):
    nqb = seq_len // Q_BLOCK
    nc = seq_len // DSA_KC
    topk = min(TOPK_MAX, seq_len // 4)
    idx_bits = int(math.log2(seq_len)) + 1
    tok = lambda b, q: (0, b * nqb + q)
    per_batch = lambda b, q: (b, 0, 0, 0)
    return pl.pallas_call(
        functools.partial(_dsa_prompt_kernel, topk=topk, idx_bits=idx_bits, cdt=cdt),
        grid=(n_batch, nqb),
        in_specs=[pl.BlockSpec((ATT_WIDTH, Q_BLOCK), tok),
                  pl.BlockSpec((IDX_HEADS * IDX_DIM, Q_BLOCK), lambda b, q: (1, b * nqb + q)),
                  pl.BlockSpec((IDX_HEADS, Q_BLOCK), tok),
                  pl.BlockSpec((ATT_WIDTH, Q_BLOCK), tok),
                  pl.BlockSpec((1, nc, DSA_KC, IDX_DIM), per_batch),
                  pl.BlockSpec((1, nc, DSA_KC, D_KV), per_batch),
                  pl.BlockSpec((1, nc, D_KV, DSA_KC), per_batch),
                  pl.BlockSpec((nqb, N_HEADS, BIAS_WIN), lambda b, q: (0, 0, 0))],
        out_specs=pl.BlockSpec((ATT_WIDTH, Q_BLOCK), tok),
        out_shape=jax.ShapeDtypeStruct((ATT_WIDTH, n_batch * seq_len), BF16),
        scratch_shapes=[pltpu.VMEM((nc, DSA_KC, Q_BLOCK), jnp.int32),
                        pltpu.VMEM((nc, DSA_KC, Q_BLOCK), F32)],
        compiler_params=pltpu.CompilerParams(
            dimension_semantics=("parallel", "arbitrary"), vmem_limit_bytes=VMEM_LIMIT),
        name="dsa_prompt_attend",
    )(qqiT, qqiT, wiT, zT, ki4, k4, v4T, win)


DSS_NP1 = 16
DSS_NP2 = 8
DSS_TP = 8
DSS_ROWS = 2 * G_D * DSS_TP
T5_LAST_BUCKET_DIST = 1600


def _dsa_sample_select_kernel(pt_ref, qi_ref, wb_ref, kinew_ref, *rest, n_pages, n_new, topk, idx_bits, cdt):
    del pt_ref
    page_refs, mask_ref, key_sc = rest[:DSS_NP1], rest[DSS_NP1], rest[DSS_NP1 + 1]
    s = pl.program_id(1)
    lane = lax.broadcasted_iota(jnp.int32, (DSS_TP, PAGE_SIZE), 1)
    trow = lax.broadcasted_iota(jnp.int32, (DSS_TP, PAGE_SIZE), 0)
    qi = qi_ref[0]
    wb = wb_ref[0]

    def page_keys(kp):
        sc = lax.dot_general(qi, kp.astype(cdt), (((1,), (1,)), ((), ())),
                             preferred_element_type=F32) * (IDX_DIM ** -0.5)
        sc = jnp.maximum(sc, 0.0) * wb
        return _sortable_key(sc.reshape(IDX_HEADS, DSS_TP, PAGE_SIZE).sum(axis=0))

    for i in range(DSS_NP1):
        key_sc[s * DSS_NP1 + i] = page_keys(page_refs[i][0, 0])

    @pl.when(s == 0)
    def _():
        kn = page_keys(kinew_ref[0])
        key_sc[n_pages] = jnp.where(lane < n_new, jnp.where(lane <= trow, kn, INT_MIN), INT_MIN)

    @pl.when(s == n_pages // DSS_NP1 - 1)
    def _():
        all_shape = (n_pages + 1, DSS_TP, PAGE_SIZE)

        def count(pred):
            key_idx = (lax.broadcasted_iota(jnp.int32, all_shape, 0) * PAGE_SIZE
                       + lax.broadcasted_iota(jnp.int32, all_shape, 2))
            acc = pred(key_sc[...], key_idx).sum(axis=0)
            return jnp.broadcast_to(jnp.sum(acc, axis=1, keepdims=True), (DSS_TP, PAGE_SIZE))

        c_nonneg = count(lambda k, i: jnp.where(k >= 0, 1, 0))
        thr = jnp.where(c_nonneg >= topk, 0, INT_MIN).astype(jnp.int32)

        def thr_bit(b, thr):
            cand = thr + jnp.left_shift(jnp.int32(1), 30 - b)
            return jnp.where(count(lambda k, i: jnp.where(k >= cand, 1, 0)) >= topk, cand, thr)

        thr = lax.fori_loop(0, 31, thr_bit, thr)
        need = topk - count(lambda k, i: jnp.where(k > thr, 1, 0))

        def lim_bit(b, lim):
            cand = lim + jnp.left_shift(jnp.int32(1), idx_bits - 1 - b)
            c = count(lambda k, i: jnp.where(k == thr, jnp.where(i < cand, 1, 0), 0))
            return jnp.where(c <= need, cand, lim)

        lim = lax.fori_loop(0, idx_bits, lim_bit, jnp.zeros((DSS_TP, PAGE_SIZE), jnp.int32))

        k = key_sc[...]
        key_idx = (lax.broadcasted_iota(jnp.int32, all_shape, 0) * PAGE_SIZE
                   + lax.broadcasted_iota(jnp.int32, all_shape, 2))
        tie = jnp.where(k == thr, jnp.where(key_idx < lim, 0.0, NEG_BIG), NEG_BIG)
        mask_ref[0] = jnp.where(k == INT_MIN, NEG_BIG, jnp.where(k > thr, 0.0, tie))


def _dsa_sample_attend_kernel(pt_ref, q_ref, z_ref, mask_ref, masknew_ref, bnear_ref, bfar_ref, kvnew_ref, *rest,
                              n_pages, n_far, cdt):
    del pt_ref
    page_refs, o_ref = rest[:DSS_NP2], rest[DSS_NP2]
    m_sc, l_sc, acc_sc = rest[DSS_NP2 + 1:]
    s = pl.program_id(1)
    n_pairs = KV_D // 2
    pw = 2 * HEAD_DIM

    def attend(pages, masks, page_ids):
        for jp in range(n_pairs):
            kcat = jnp.concatenate([pg[:, jp * pw:(jp + 1) * pw] for pg in pages], axis=0).astype(cdt)
            vcat = jnp.concatenate([pg[:, D_KV + jp * pw:D_KV + (jp + 1) * pw] for pg in pages], axis=0).astype(cdt)
            logits = lax.dot_general(q_ref[0, jp], kcat, (((1,), (1,)), ((), ())), preferred_element_type=F32)
            far = bfar_ref[jp]
            bias = [jnp.where(pid >= n_far, bnear_ref[jnp.maximum(pid - n_far, 0), jp], far) for pid in page_ids]
            madd = [jnp.concatenate([mk] * (DSS_ROWS // DSS_TP), axis=0) for mk in masks]
            logits = logits + jnp.concatenate([b + m for b, m in zip(bias, madd)], axis=1)
            m_old = m_sc[jp]
            m_new = jnp.maximum(m_old, jnp.broadcast_to(jnp.max(logits, axis=1, keepdims=True), m_old.shape))
            alpha = jnp.exp(m_old - m_new)
            p = jnp.exp(logits - jnp.concatenate([m_new] * len(pages), axis=1))
            l_sc[jp] = l_sc[jp] * alpha + jnp.broadcast_to(jnp.sum(p, axis=1, keepdims=True), m_old.shape)
            acc_sc[jp] = acc_sc[jp] * alpha + jnp.dot(p.astype(cdt), vcat, preferred_element_type=F32)
            m_sc[jp] = m_new

    @pl.when(s == 0)
    def _():
        m_sc[...] = jnp.full_like(m_sc, NEG_BIG)
        l_sc[...] = jnp.zeros_like(l_sc)
        acc_sc[...] = jnp.zeros_like(acc_sc)
        attend([kvnew_ref[0]], [masknew_ref[0, 0]], [n_pages])

    attend([r[0, 0] for r in page_refs], [mask_ref[0, i] for i in range(DSS_NP2)],
           [s * DSS_NP2 + i for i in range(DSS_NP2)])

    @pl.when(s == n_pages // DSS_NP2 - 1)
    def _():
        for jp in range(n_pairs):
            z = z_ref[0, jp]
            o_ref[0, jp] = acc_sc[jp] * (1.0 / l_sc[jp]) * (z * (1.0 / (1.0 + jnp.exp(-z))))


def dsa_sample(q_s, z_s, qi_s, wi_s, ki_s, kv_s, kv_pool, kidx_pool, layer, page_table, rel_bias, cdt=BF16):
    n_dec, n_pages = page_table.shape
    dec_len = q_s.shape[0] // n_dec
    past = n_pages * PAGE_SIZE
    total = past + dec_len
    topk = min(TOPK_MAX, total // 4)
    idx_bits = int(math.log2(total)) + 1
    pad_t = DSS_TP - dec_len
    n_pairs = KV_D // 2
    eye2 = jnp.eye(2, dtype=F32)

    qi = jnp.pad(jnp.swapaxes(qi_s.reshape(n_dec, dec_len, IDX_HEADS, IDX_DIM), 1, 2), ((0, 0), (0, 0), (0, pad_t), (0, 0)))
    qi = qi.reshape(n_dec, IDX_HEADS * DSS_TP, IDX_DIM).astype(cdt)
    wb = jnp.pad(jnp.swapaxes(wi_s.reshape(n_dec, dec_len, IDX_HEADS), 1, 2) * (IDX_HEADS ** -0.5), ((0, 0), (0, 0), (0, pad_t)))
    wb = jnp.broadcast_to(wb.reshape(n_dec, IDX_HEADS * DSS_TP, 1), (n_dec, IDX_HEADS * DSS_TP, PAGE_SIZE))
    ki_new = jnp.pad(ki_s.reshape(n_dec, dec_len, IDX_DIM), ((0, 0), (0, PAGE_SIZE - dec_len), (0, 0)))
    kidx4 = kidx_pool.reshape(kidx_pool.shape[0], kidx_pool.shape[1], PAGE_SIZE, IDX_DIM)
    page_spec = lambda np_, i, width: pl.BlockSpec(
        (1, 1, PAGE_SIZE, width), lambda b, s, pt: (layer, pt[b, s * np_ + i], 0, 0))
    per_b3 = lambda b, s, pt: (b, 0, 0)
    mask = pl.pallas_call(
        functools.partial(_dsa_sample_select_kernel, n_pages=n_pages, n_new=dec_len, topk=topk, idx_bits=idx_bits,
                          cdt=cdt),
        grid_spec=pltpu.PrefetchScalarGridSpec(
            num_scalar_prefetch=1, grid=(n_dec, n_pages // DSS_NP1),
            in_specs=[pl.BlockSpec((1, IDX_HEADS * DSS_TP, IDX_DIM), per_b3),
                      pl.BlockSpec((1, IDX_HEADS * DSS_TP, PAGE_SIZE), per_b3),
                      pl.BlockSpec((1, PAGE_SIZE, IDX_DIM), per_b3)]
                     + [page_spec(DSS_NP1, i, IDX_DIM) for i in range(DSS_NP1)],
            out_specs=pl.BlockSpec((1, n_pages + 1, DSS_TP, PAGE_SIZE), lambda b, s, pt: (b, 0, 0, 0)),
            scratch_shapes=[pltpu.VMEM((n_pages + 1, DSS_TP, PAGE_SIZE), jnp.int32)]),
        out_shape=jax.ShapeDtypeStruct((n_dec, n_pages + 1, DSS_TP, PAGE_SIZE), F32),
        compiler_params=pltpu.CompilerParams(
            dimension_semantics=("parallel", "arbitrary"), vmem_limit_bytes=VMEM_LIMIT),
        name="dsa_sample_select",
    )(page_table, qi, wb, ki_new, *([kidx4] * DSS_NP1))

    def pair_rows(t, scale):
        t = t.reshape(n_dec, dec_len, n_pairs, 2, G_D, HEAD_DIM).transpose(0, 2, 3, 4, 1, 5) * scale
        t = jnp.pad(t, ((0, 0),) * 4 + ((0, pad_t), (0, 0)))
        return jnp.einsum('bpjgtd,jk->bpjgtkd', t, eye2).reshape(n_dec, n_pairs, DSS_ROWS, 2 * HEAD_DIM)

    q_pr = pair_rows(q_s, HEAD_DIM ** -0.5).astype(cdt)
    z_pr = pair_rows(z_s, 1.0)
    n_far = max(0, min(n_pages, (past - (PAGE_SIZE - 1) - T5_LAST_BUCKET_DIST) // PAGE_SIZE + 1))
    near_pages = jnp.arange(n_far, n_pages + 1)
    t8 = jnp.arange(DSS_TP)
    dist = past + t8[None, :, None] - (near_pages[:, None, None] * PAGE_SIZE + jnp.arange(PAGE_SIZE)[None, None, :])
    b_near = rel_bias[rel_bucket(dist)].astype(F32)
    b_near = b_near.transpose(0, 3, 1, 2).reshape(n_pages + 1 - n_far, n_pairs, DSS_ROWS, PAGE_SIZE)
    b_far = jnp.broadcast_to(rel_bias[N_BUCKETS - 1].astype(F32)[:, None, None], (N_HEADS, DSS_TP, PAGE_SIZE))
    b_far = b_far.reshape(n_pairs, DSS_ROWS, PAGE_SIZE)
    kv_new = jnp.pad(kv_s.reshape(n_dec, dec_len, 2 * D_KV), ((0, 0), (0, PAGE_SIZE - dec_len), (0, 0)))
    kv4 = kv_pool.reshape(kv_pool.shape[0], kv_pool.shape[1], PAGE_SIZE, 2 * D_KV)
    per_b4 = lambda b, s, pt: (b, 0, 0, 0)
    o = pl.pallas_call(
        functools.partial(_dsa_sample_attend_kernel, n_pages=n_pages, n_far=n_far, cdt=cdt),
        grid_spec=pltpu.PrefetchScalarGridSpec(
            num_scalar_prefetch=1, grid=(n_dec, n_pages // DSS_NP2),
            in_specs=[pl.BlockSpec((1, n_pairs, DSS_ROWS, 2 * HEAD_DIM), per_b4),
                      pl.BlockSpec((1, n_pairs, DSS_ROWS, 2 * HEAD_DIM), per_b4),
                      pl.BlockSpec((1, DSS_NP2, DSS_TP, PAGE_SIZE), lambda b, s, pt: (b, s, 0, 0)),
                      pl.BlockSpec((1, 1, DSS_TP, PAGE_SIZE), lambda b, s, pt: (b, n_pages, 0, 0)),
                      pl.BlockSpec(b_near.shape, lambda b, s, pt: (0, 0, 0, 0)),
                      pl.BlockSpec(b_far.shape, lambda b, s, pt: (0, 0, 0)),
                      pl.BlockSpec((1, PAGE_SIZE, 2 * D_KV), per_b3)]
                     + [page_spec(DSS_NP2, i, 2 * D_KV) for i in range(DSS_NP2)],
            out_specs=pl.BlockSpec((1, n_pairs, DSS_ROWS, 2 * HEAD_DIM), per_b4),
            scratch_shapes=[pltpu.VMEM((n_pairs, DSS_ROWS, 2 * HEAD_DIM), F32)] * 3),
        out_shape=jax.ShapeDtypeStruct((n_dec, n_pairs, DSS_ROWS, 2 * HEAD_DIM), F32),
        compiler_params=pltpu.CompilerParams(
            dimension_semantics=("parallel", "arbitrary"), vmem_limit_bytes=VMEM_LIMIT),
        name="dsa_sample_attend",
    )(page_table, q_pr, z_pr, mask, mask, b_near, b_far, kv_new, *([kv4] * DSS_NP2))
    o = o.reshape(n_dec, n_pairs, 2, G_D, DSS_TP, 2, HEAD_DIM)
    o = jnp.einsum('bpjgtkd,jk->bpjgtd', o, eye2)[:, :, :, :, :dec_len]
    return o.transpose(0, 4, 1, 2, 3, 5).reshape(n_dec * dec_len, ATT_WIDTH)


def _pad_cols(w, n):
    return jnp.pad(w, ((0, 0), (0, n - w.shape[1])))


def kernel(x_prompt, x_sample, cache_a_kv, state_s5, state_gdn, state_gdn_conv, cache_d_kv, cache_d_kidx,
           page_table, p_prompt, p_sample, rel_bias, ln_g, ln_b, ple_gate_w, ple_w,
           a_w_in, a_sinks, a_w_out,
           s5_w_in, s5_a_re, s5_a_im, s5_b_re, s5_b_im, s5_c_re, s5_c_im, s5_d, s5_log_dt, s5_w_glu, s5_w_out,
           gdn_w_in, gdn_conv_w, gdn_a_log, gdn_dt_bias, gdn_norm_w, gdn_w_out,
           dsa_w_in, dsa_w_out):
    past_len = page_table.shape[1] * PAGE_SIZE
    x = join_tokens(x_prompt, x_sample)
    x_bf = x.astype(BF16)
    outs = {}

    def post(i, x, h):
        p_bf = join_tokens(p_prompt[i], p_sample[i]).astype(BF16)
        return post_norm_ple(x, h, p_bf, ln_g[i], ln_b[i], ple_gate_w[i].astype(BF16), ple_w[i].astype(BF16))

    def finish_layer(i, x, gated, w_out):
        return post(i, x, matmul(gated.astype(BF16), w_out.astype(BF16)))

    w_in = a_w_in[0]
    c_k, c_v, c_z = ATT_WIDTH, ATT_WIDTH + A_KV, ATT_WIDTH + 2 * A_KV
    w_q, w_z = w_in[:, :c_k], w_in[:, c_z:]
    w_out_bf = a_w_out[0].astype(BF16)
    kv_nat = matmul(x_bf, w_in[:, c_k:c_z].astype(BF16))
    xT_bf = x_bf[:N_PROMPT_TOK].T
    qvT = matmul(jnp.concatenate([w_q, w_in[:, c_v:c_z]], axis=1).T.astype(BF16), xT_bf, out_dtype=BF16)
    zT = matmul(w_z.T.astype(BF16), xT_bf)
    bias_p, sink_p, bias_s, sink_s = swa_tables(rel_bias, a_sinks[0], DEC_SEQ)
    h_p = matmul_ta(swa_prompt(qvT, zT, kv_nat, bias_p, sink_p, n_batch=BATCH, seq_len=SEQ), w_out_bf)
    qz_s = matmul(x_bf[N_PROMPT_TOK:], jnp.concatenate([w_q, w_z], axis=1).astype(BF16))
    gs, outs['a_s'] = swa_sample(qz_s[:, :ATT_WIDTH], qz_s[:, ATT_WIDTH:], kv_nat[N_PROMPT_TOK:], cache_a_kv[0],
                                 bias_s, sink_s)
    outs['a_p'] = kv_nat[:N_PROMPT_TOK].reshape(BATCH, SEQ, 2, KV_A, HEAD_DIM)[:, SEQ - WINDOW:]
    x, x_bf = post(0, x, jnp.concatenate([h_p, matmul(gs.astype(BF16), w_out_bf)], axis=0))

    proj = matmul(x_bf, s5_w_in[0].astype(BF16))
    tables = s5_tables(s5_a_re[0], s5_a_im[0], s5_b_re[0], s5_b_im[0], s5_c_re[0], s5_c_im[0], s5_log_dt[0])
    gp, gs, outs['s5_p'], outs['s5_s'] = s5_layer(proj, state_s5[0], tables, s5_d[0], s5_w_glu[0],
                                                  n_batch=BATCH, seq_len=SEQ, n_dec=DEC_BATCH, dec_len=DEC_SEQ)
    w_out_bf = s5_w_out[0].astype(BF16)
    x, x_bf = post(1, x, jnp.concatenate([matmul(gp, w_out_bf), matmul(gs, w_out_bf)], axis=0))

    w_in = gdn_w_in[0]
    c_gz = GDN_CONV_CH + GDN_V_WIDTH
    qkvz = matmul(x_bf, w_in[:, :c_gz].astype(BF16))
    ab = matmul(x_bf, _pad_cols(w_in[:, c_gz:], LANE).astype(BF16))
    gp, gs, outs['gd_p'], outs['gd_s'], outs['gc_p'], outs['gc_s'] = gdn_layer(
        qkvz, ab, state_gdn[0], state_gdn_conv[0], gdn_conv_w[0], gdn_a_log[0], gdn_dt_bias[0], gdn_norm_w[0],
        n_batch=BATCH, seq_len=SEQ, n_dec=DEC_BATCH, dec_len=DEC_SEQ)
    w_out_bf = gdn_w_out[0].astype(BF16)
    x, x_bf = post(2, x, jnp.concatenate([matmul(gp, w_out_bf), matmul(gs.astype(BF16), w_out_bf)], axis=0))

    w_in = dsa_w_in[0]
    c_z = 2 * ATT_WIDTH + 2 * D_KV
    c_qi = c_z + IDX_HEADS * IDX_DIM
    c_kv = ATT_WIDTH + 2 * D_KV
    w_q, w_kv, w_z, w_qi = w_in[:, :ATT_WIDTH], w_in[:, ATT_WIDTH:c_kv], w_in[:, c_kv:c_z], w_in[:, c_z:c_qi]
    w_out_bf = dsa_w_out[0].astype(BF16)
    kv_nat = matmul(x_bf, w_kv.astype(BF16))
    kiw = matmul(x_bf, _pad_cols(w_in[:, c_qi:], 2 * LANE).astype(BF16))
    xT_bf = x_bf[:N_PROMPT_TOK].T
    qqiT = matmul(jnp.concatenate([w_q, w_qi], axis=1).T.astype(BF16), xT_bf, out_dtype=BF16)
    zT = matmul(w_z.T.astype(BF16), xT_bf)
    wiT = matmul(w_in[:, c_qi + IDX_DIM:].T.astype(BF16), xT_bf)
    nc = SEQ // DSA_KC
    kv_p = kv_nat[:N_PROMPT_TOK]
    v4T = jnp.swapaxes(kv_p[:, D_KV:].astype(BF16).reshape(BATCH, nc, DSA_KC, D_KV), 2, 3)
    gT = dsa_prompt_attend(qqiT, wiT, zT, kiw[:N_PROMPT_TOK].reshape(BATCH, nc, DSA_KC, 2 * LANE),
                           kv_p.reshape(BATCH, nc, DSA_KC, 2 * D_KV), v4T, dsa_bias_windows(rel_bias, SEQ),
                           n_batch=BATCH, seq_len=SEQ)
    h_p = matmul_ta(gT, w_out_bf)
    x_s = x_bf[N_PROMPT_TOK:]
    qzqi_s = matmul(x_s, jnp.concatenate([w_q, w_z, w_qi], axis=1).astype(BF16))
    kiw_s = kiw[N_PROMPT_TOK:]
    gs = dsa_sample(qzqi_s[:, :ATT_WIDTH], qzqi_s[:, ATT_WIDTH:2 * ATT_WIDTH], qzqi_s[:, 2 * ATT_WIDTH:],
                    kiw_s[:, IDX_DIM:IDX_DIM + IDX_HEADS], kiw_s[:, :IDX_DIM], kv_nat[N_PROMPT_TOK:],
                    cache_d_kv, cache_d_kidx, 0, page_table, rel_bias)
    h_s = matmul(gs.astype(BF16), w_out_bf)
    outs['dkv_p'] = kv_p.reshape(BATCH, SEQ, 2, KV_D, HEAD_DIM)
    outs['dkv_s'] = kv_nat[N_PROMPT_TOK:].reshape(DEC_BATCH, DEC_SEQ, 2, KV_D, HEAD_DIM)
    outs['dki_p'] = kiw[:N_PROMPT_TOK, :IDX_DIM].reshape(BATCH, SEQ, IDX_DIM)
    outs['dki_s'] = kiw_s[:, :IDX_DIM].reshape(DEC_BATCH, DEC_SEQ, IDX_DIM)
    x, x_bf = post(3, x, jnp.concatenate([h_p, h_s], axis=0))

    yp, ys = split_tokens(x)
    st = lambda name: outs[name][None]
    return (yp, ys, st('a_p'), st('a_s'), st('s5_p'), st('s5_s'), st('gd_p'), st('gd_s'),
            st('gc_p'), st('gc_s'), st('dkv_p'), st('dkv_s'), st('dki_p'), st('dki_s'))
```

```python
import functools
import math

import jax
import jax.numpy as jnp
from jax import lax
from jax.experimental import pallas as pl
from jax.experimental.pallas import tpu as pltpu

D_MODEL = 2048
BATCH = 4
SEQ = 2048
DEPTH = 4
DEC_BATCH = 32
DEC_SEQ = 4
PAGE_SIZE = 128
N_MIXERS = 4
PLE_DIM = 256
ALPHA = (2 * DEPTH) ** 0.25
LN_EPS = 1e-5
N_BUCKETS = 32
REL_MAX_DIST = 2048
N_HEADS = 32
HEAD_DIM = 64
ATT_WIDTH = N_HEADS * HEAD_DIM
WINDOW = 128
KV_A = 4
A_KV = KV_A * HEAD_DIM
KV_D = 8
D_KV = KV_D * HEAD_DIM
IDX_HEADS = 16
IDX_DIM = 128
TOPK_MAX = 256
Q_BLOCK = 128
S5_WIDTH = D_MODEL
S5_GROUP = 16
S5_GROUPS = S5_WIDTH // S5_GROUP
S5_STATE = 64
GDN_QK_HEADS = 16
GDN_V_HEADS = 32
GDN_DK = 128
GDN_DV = 128
GDN_CONV = 4
GDN_CHUNK = 64
GDN_QK_WIDTH = GDN_QK_HEADS * GDN_DK
GDN_V_WIDTH = GDN_V_HEADS * GDN_DV
GDN_CONV_CH = 2 * GDN_QK_WIDTH + GDN_V_WIDTH

F32 = jnp.float32
BF16 = jnp.bfloat16

N_PROMPT_TOK = BATCH * SEQ
N_SAMPLE_TOK = DEC_BATCH * DEC_SEQ
N_TOK = N_PROMPT_TOK + N_SAMPLE_TOK

V7X_VMEM_BYTES = 64 * 1024 * 1024
VMEM_LIMIT = 48 * 1024 * 1024
LANE = 128


def _mm_kernel(x_ref, w_ref, o_ref):
    o_ref[...] = jnp.dot(x_ref[...], w_ref[...], preferred_element_type=F32).astype(o_ref.dtype)


def _pick_tile(n, prefs):
    for t in prefs:
        if n % t == 0:
            return t
    raise ValueError(f"no tile for {n}")


def matmul(x, w, out_dtype=F32):
    m, k = x.shape
    n = w.shape[1]
    tm = _pick_tile(m, (640, 512, 320, 256, 128, 64, 32, 16, 8))
    tn = _pick_tile(n, (512, 384, 256, 128))
    return pl.pallas_call(
        _mm_kernel,
        grid=(m // tm, n // tn),
        in_specs=[pl.BlockSpec((tm, k), lambda i, j: (i, 0)),
                  pl.BlockSpec((k, tn), lambda i, j: (0, j))],
        out_specs=pl.BlockSpec((tm, tn), lambda i, j: (i, j)),
        out_shape=jax.ShapeDtypeStruct((m, n), out_dtype),
        compiler_params=pltpu.CompilerParams(
            dimension_semantics=("parallel", "parallel"), vmem_limit_bytes=VMEM_LIMIT),
        name="proj_matmul",
    )(x, w)


def _mm_ta_kernel(xt_ref, w_ref, o_ref):
    o_ref[...] = lax.dot_general(xt_ref[...], w_ref[...], (((0,), (0,)), ((), ())),
                                 preferred_element_type=F32).astype(o_ref.dtype)


def matmul_ta(xt, w, out_dtype=F32):
    k, m = xt.shape
    n = w.shape[1]
    tm = _pick_tile(m, (512, 256, 128))
    tn = _pick_tile(n, (512, 384, 256, 128))
    return pl.pallas_call(
        _mm_ta_kernel,
        grid=(m // tm, n // tn),
        in_specs=[pl.BlockSpec((k, tm), lambda i, j: (0, i)),
                  pl.BlockSpec((k, tn), lambda i, j: (0, j))],
        out_specs=pl.BlockSpec((tm, tn), lambda i, j: (i, j)),
        out_shape=jax.ShapeDtypeStruct((m, n), out_dtype),
        compiler_params=pltpu.CompilerParams(
            dimension_semantics=("parallel", "parallel"), vmem_limit_bytes=VMEM_LIMIT),
        name="proj_matmul_ta",
    )(xt, w)


POST_TM = 320
POST_TN = 512


def _post_kernel(x_ref, h_ref, p_ref, g_ref, b_ref, wg_ref, wp_ref, o_ref, obf_ref, y_sc, ybf_sc):
    j = pl.program_id(1)

    @pl.when(j == 0)
    def _():
        t = ALPHA * x_ref[...] + h_ref[...]
        mu = jnp.mean(t, axis=-1, keepdims=True)
        d = t - mu
        var = jnp.mean(d * d, axis=-1, keepdims=True)
        y = d * lax.rsqrt(var + LN_EPS) * g_ref[...] + b_ref[...]
        ybf_sc[...] = y.astype(BF16)
        for jj in range(D_MODEL // POST_TN):
            y_sc[jj] = y[:, jj * POST_TN:(jj + 1) * POST_TN]

    gate = jnp.dot(ybf_sc[...], wg_ref[...], preferred_element_type=F32)
    ple = jnp.dot(p_ref[...], wp_ref[...], preferred_element_type=F32)
    o = y_sc[j] + (1.0 / (1.0 + jnp.exp(-gate))) * ple
    o_ref[...] = o
    obf_ref[...] = o.astype(BF16)


def post_norm_ple(x, h, p_bf, g, b, wg_bf, wp_bf):
    m = x.shape[0]
    tm, tn = POST_TM, POST_TN
    return pl.pallas_call(
        _post_kernel,
        grid=(m // tm, D_MODEL // tn),
        in_specs=[pl.BlockSpec((tm, D_MODEL), lambda i, j: (i, 0)),
                  pl.BlockSpec((tm, D_MODEL), lambda i, j: (i, 0)),
                  pl.BlockSpec((tm, PLE_DIM), lambda i, j: (i, 0)),
                  pl.BlockSpec((1, D_MODEL), lambda i, j: (0, 0)),
                  pl.BlockSpec((1, D_MODEL), lambda i, j: (0, 0)),
                  pl.BlockSpec((D_MODEL, tn), lambda i, j: (0, j)),
                  pl.BlockSpec((PLE_DIM, tn), lambda i, j: (0, j))],
        out_specs=[pl.BlockSpec((tm, tn), lambda i, j: (i, j)),
                   pl.BlockSpec((tm, tn), lambda i, j: (i, j))],
        out_shape=[jax.ShapeDtypeStruct((m, D_MODEL), F32),
                   jax.ShapeDtypeStruct((m, D_MODEL), BF16)],
        scratch_shapes=[pltpu.VMEM((D_MODEL // tn, tm, tn), F32),
                        pltpu.VMEM((tm, D_MODEL), BF16)],
        compiler_params=pltpu.CompilerParams(
            dimension_semantics=("parallel", "arbitrary"), vmem_limit_bytes=VMEM_LIMIT),
        name="post_norm_ple",
    )(x, h, p_bf, g.reshape(1, D_MODEL), b.reshape(1, D_MODEL), wg_bf, wp_bf)


def rel_bucket(dist):
    n = jnp.maximum(dist, 0)
    exact = N_BUCKETS // 2
    logb = exact + (jnp.log(jnp.maximum(n, exact).astype(F32) / exact)
                    / math.log(REL_MAX_DIST / exact) * (N_BUCKETS - exact)).astype(jnp.int32)
    return jnp.where(n < exact, n, jnp.minimum(logb, N_BUCKETS - 1))


def split_tokens(t):
    c = t.shape[-1]
    return (t[:N_PROMPT_TOK].reshape(BATCH, SEQ, c), t[N_PROMPT_TOK:].reshape(DEC_BATCH, DEC_SEQ, c))


def join_tokens(tp, ts):
    c = tp.shape[-1]
    return jnp.concatenate([tp.reshape(N_PROMPT_TOK, c), ts.reshape(N_SAMPLE_TOK, c)], axis=0)


G_A = N_HEADS // KV_A
SWA_KEYS = 2 * WINDOW


def swa_tables(rel_bias, sinks, dec_len):
    def heads_to(b, lead):
        return jnp.moveaxis(b, -1, 0).reshape((KV_A, G_A) + lead)

    dist = jnp.arange(WINDOW)[None, :] - (jnp.arange(SWA_KEYS)[:, None] - WINDOW)
    ok = (dist >= 0) & (dist < WINDOW)
    b = jnp.where(ok[..., None], rel_bias[rel_bucket(dist)].astype(F32), NEG_BIG)
    bias_p = heads_to(b, (SWA_KEYS, WINDOW)).transpose(0, 2, 1, 3).reshape(KV_A, SWA_KEYS, G_A * WINDOW)
    sink_p = jnp.broadcast_to(sinks.astype(F32).reshape(KV_A, 1, G_A, 1), (KV_A, 1, G_A, WINDOW))
    sink_p = sink_p.reshape(KV_A, 1, G_A * WINDOW)
    key_i = jnp.arange(SWA_KEYS)[None, :]
    dist = jnp.arange(dec_len)[:, None] + WINDOW - key_i
    ok = (dist >= 0) & (dist < WINDOW) & (key_i < WINDOW + dec_len)
    b = jnp.where(ok[..., None], rel_bias[rel_bucket(dist)].astype(F32), NEG_BIG)
    bias_s = heads_to(b, (dec_len, SWA_KEYS)).reshape(KV_A, G_A * dec_len, SWA_KEYS)
    sink_s = jnp.broadcast_to(sinks.astype(F32).reshape(KV_A, G_A, 1, 1), (KV_A, G_A, dec_len, LANE))
    sink_s = sink_s.reshape(KV_A, G_A * dec_len, LANE)
    return bias_p, sink_p, bias_s, sink_s


def _swa_prompt_kernel(qT_ref, zT_ref, vTp_ref, vTc_ref, kp_ref, kc_ref, bias_ref, sink_ref, o_ref, *, cdt):
    first = pl.program_id(1) == 0
    kk = jnp.concatenate([kp_ref[...], kc_ref[...]], axis=0)
    vT = jnp.concatenate([vTp_ref[...], vTc_ref[...]], axis=1)
    prev_key = lax.broadcasted_iota(jnp.int32, (SWA_KEYS, G_A * WINDOW), 0) < WINDOW
    pw = 2 * HEAD_DIM
    for j in range(KV_A):
        kpair = kk[:, (j // 2) * pw:(j // 2 + 1) * pw].astype(cdt)
        qj = jnp.concatenate([qT_ref[(G_A * j + g) * HEAD_DIM:(G_A * j + g + 1) * HEAD_DIM, :]
                              for g in range(G_A)], axis=1)
        qj = (qj.astype(F32) * (HEAD_DIM ** -0.5)).astype(cdt)
        zpad = jnp.zeros_like(qj)
        rhs = jnp.concatenate([qj, zpad] if j % 2 == 0 else [zpad, qj], axis=0)
        s = jnp.dot(kpair, rhs, preferred_element_type=F32) + bias_ref[j]
        s = jnp.where(prev_key, jnp.where(first, NEG_BIG, s), s)
        sink = sink_ref[j]
        m = jnp.maximum(jnp.max(s, axis=0, keepdims=True), sink)
        e = jnp.exp(s - m)
        den = jnp.sum(e, axis=0, keepdims=True) + jnp.exp(sink - m)
        p = (e * (1.0 / den)).astype(cdt)
        acc = jnp.dot(vT[j * HEAD_DIM:(j + 1) * HEAD_DIM, :].astype(cdt), p, preferred_element_type=F32)
        for g in range(G_A):
            r0 = (G_A * j + g) * HEAD_DIM
            z = zT_ref[r0:r0 + HEAD_DIM, :]
            o_ref[r0:r0 + HEAD_DIM, :] = (acc[:, g * WINDOW:(g + 1) * WINDOW]
                                          * (z * (1.0 / (1.0 + jnp.exp(-z))))).astype(o_ref.dtype)


def swa_prompt(qvT, zT, kv_nat, bias_p, sink_p, *, n_batch, seq_len, cdt=BF16):
    nb = seq_len // WINDOW
    cur = lambda b, i: b * nb + i
    prev = lambda b, i: b * nb + jnp.maximum(i - 1, 0)
    v_row_blk = ATT_WIDTH // A_KV
    return pl.pallas_call(
        functools.partial(_swa_prompt_kernel, cdt=cdt),
        grid=(n_batch, nb),
        in_specs=[pl.BlockSpec((ATT_WIDTH, WINDOW), lambda b, i: (0, cur(b, i))),
                  pl.BlockSpec((ATT_WIDTH, WINDOW), lambda b, i: (0, cur(b, i))),
                  pl.BlockSpec((A_KV, WINDOW), lambda b, i: (v_row_blk, prev(b, i))),
                  pl.BlockSpec((A_KV, WINDOW), lambda b, i: (v_row_blk, cur(b, i))),
                  pl.BlockSpec((WINDOW, A_KV), lambda b, i: (prev(b, i), 0)),
                  pl.BlockSpec((WINDOW, A_KV), lambda b, i: (cur(b, i), 0)),
                  pl.BlockSpec(bias_p.shape, lambda b, i: (0, 0, 0)),
                  pl.BlockSpec(sink_p.shape, lambda b, i: (0, 0, 0))],
        out_specs=pl.BlockSpec((ATT_WIDTH, WINDOW), lambda b, i: (0, cur(b, i))),
        out_shape=jax.ShapeDtypeStruct((ATT_WIDTH, n_batch * seq_len), BF16),
        compiler_params=pltpu.CompilerParams(
            dimension_semantics=("parallel", "parallel"), vmem_limit_bytes=VMEM_LIMIT),
        name="swa_prompt",
    )(qvT, zT, qvT, qvT, kv_nat, kv_nat, bias_p, sink_p)


def _swa_sample_kernel(q_ref, z_ref, k_ref, v_ref, bias_ref, sink_ref, o_ref, *, cdt):
    for j in range(KV_A):
        s = lax.dot_general(q_ref[0, j], k_ref[0, j].astype(cdt), (((1,), (1,)), ((), ())),
                            preferred_element_type=F32) + bias_ref[j]
        sink = sink_ref[j][:, 0:1]
        m = jnp.maximum(jnp.max(s, axis=1, keepdims=True), sink)
        e = jnp.exp(s - m)
        den = jnp.sum(e, axis=1, keepdims=True) + jnp.exp(sink - m)
        p = (e * (1.0 / den)).astype(cdt)
        z = z_ref[0, j]
        o_ref[0, j] = jnp.dot(p, v_ref[0, j].astype(cdt), preferred_element_type=F32) * (z * (1.0 / (1.0 + jnp.exp(-z))))


def swa_sample(q_s, z_s, kv_s, kv_cache, bias_s, sink_s, cdt=BF16):
    n_dec = kv_cache.shape[0]
    dec_len = q_s.shape[0] // n_dec
    rows = G_A * dec_len

    def head_rows(t, scale):
        t = t.reshape(n_dec, dec_len, KV_A, G_A, HEAD_DIM).transpose(0, 2, 3, 1, 4) * scale
        return jnp.pad(t.reshape(n_dec, KV_A, rows, HEAD_DIM), ((0, 0), (0, 0), (0, 0), (0, HEAD_DIM)))

    new = kv_s.reshape(n_dec, dec_len, 2, KV_A, HEAD_DIM)
    cat = jnp.concatenate([kv_cache, new], axis=1)
    keys = jnp.pad(cat.transpose(2, 0, 3, 1, 4),
                   ((0, 0), (0, 0), (0, 0), (0, SWA_KEYS - WINDOW - dec_len), (0, HEAD_DIM)))
    blk = lambda r: pl.BlockSpec((1, KV_A, r, 2 * HEAD_DIM), lambda b: (b, 0, 0, 0))
    o = pl.pallas_call(
        functools.partial(_swa_sample_kernel, cdt=cdt),
        grid=(n_dec,),
        in_specs=[blk(rows), blk(rows), blk(SWA_KEYS), blk(SWA_KEYS),
                  pl.BlockSpec(bias_s.shape, lambda b: (0, 0, 0)),
                  pl.BlockSpec(sink_s.shape, lambda b: (0, 0, 0))],
        out_specs=blk(rows),
        out_shape=jax.ShapeDtypeStruct((n_dec, KV_A, rows, 2 * HEAD_DIM), F32),
        compiler_params=pltpu.CompilerParams(dimension_semantics=("parallel",), vmem_limit_bytes=VMEM_LIMIT),
        name="swa_sample",
    )(head_rows(q_s, HEAD_DIM ** -0.5).astype(cdt), head_rows(z_s, 1.0), keys[0], keys[1], bias_s, sink_s)
    o = o[..., :HEAD_DIM].reshape(n_dec, KV_A, G_A, dec_len, HEAD_DIM).transpose(0, 3, 1, 2, 4)
    return o.reshape(n_dec * dec_len, ATT_WIDTH), cat[:, dec_len:]


S5_SLAB_G = 8
S5_SLAB_CH = S5_SLAB_G * S5_GROUP
S5_SLAB_ST = S5_SLAB_G * S5_STATE
S5_N_SLABS = S5_GROUPS // S5_SLAB_G
S5_CHAINS = 8
S5_HALF_CH = S5_CHAINS * S5_SLAB_CH
S5_T = 256
S5_LT = 2 * S5_SLAB_ST // LANE


def _gelu_tanh(x):
    return 0.5 * x * (1.0 + jnp.tanh(math.sqrt(2.0 / math.pi) * (x + 0.044715 * (x * x * x))))


def s5_tables(a_re, a_im, b_re, b_im, c_re, c_im, log_dt):
    a = lax.complex(a_re, a_im)
    dt = jnp.exp(log_dt)[:, None]
    a_bar = jnp.exp(a * dt)
    b_bar = ((a_bar - 1.0) / a)[..., None] * lax.complex(b_re, b_im)
    eye = jnp.eye(S5_SLAB_G, dtype=F32)

    def b_blk(t):
        t = t.reshape(S5_N_SLABS, S5_SLAB_G, S5_STATE, S5_GROUP)
        return jnp.einsum('ij,sipc->sicjp', eye, t).reshape(S5_N_SLABS, S5_SLAB_CH, S5_SLAB_ST)

    def c_blk(t):
        t = t.reshape(S5_N_SLABS, S5_SLAB_G, S5_GROUP, S5_STATE)
        return jnp.einsum('ij,sicp->sjpic', eye, t).reshape(S5_N_SLABS, S5_SLAB_ST, S5_SLAB_CH)

    bcat = jnp.concatenate([b_blk(b_bar.real), b_blk(b_bar.imag)], axis=2)
    ccat = jnp.concatenate([c_blk(c_re), -c_blk(c_im)], axis=1)
    a_cat = jnp.concatenate([a_bar.real.reshape(S5_N_SLABS, S5_SLAB_ST),
                             a_bar.imag.reshape(S5_N_SLABS, S5_SLAB_ST)], axis=1)
    return a_cat, bcat, ccat


def _s5_prompt_kernel(u_ref, bcat_ref, ccat_ref, a_ref, d_ref, y_ref, hout_ref, sc, h_sc, *, cdt):
    tc = pl.program_id(2)
    n_lt_half = S5_LT // 2

    @pl.when(tc == 0)
    def _():
        h_sc[...] = jnp.zeros_like(h_sc)

    for j in range(S5_CHAINS):
        uj = u_ref[:, j * S5_SLAB_CH:(j + 1) * S5_SLAB_CH].astype(cdt)
        bu = jnp.dot(uj, bcat_ref[0, j], preferred_element_type=F32)
        for lt in range(S5_LT):
            sc[lt, pl.ds(j, S5_T, stride=S5_CHAINS), :] = bu[:, lt * LANE:(lt + 1) * LANE]

    a_re = [a_ref[0, :, lt * LANE:(lt + 1) * LANE] for lt in range(n_lt_half)]
    a_im = [a_ref[0, :, (n_lt_half + lt) * LANE:(n_lt_half + lt + 1) * LANE] for lt in range(n_lt_half)]

    def step(t, h):
        r0 = pl.multiple_of(t * S5_CHAINS, S5_CHAINS)
        new = list(h)
        for lt in range(n_lt_half):
            hr, hi = h[lt], h[n_lt_half + lt]
            nr = a_re[lt] * hr - a_im[lt] * hi + sc[lt, pl.ds(r0, S5_CHAINS), :]
            ni = a_re[lt] * hi + a_im[lt] * hr + sc[n_lt_half + lt, pl.ds(r0, S5_CHAINS), :]
            sc[lt, pl.ds(r0, S5_CHAINS), :] = nr
            sc[n_lt_half + lt, pl.ds(r0, S5_CHAINS), :] = ni
            new[lt], new[n_lt_half + lt] = nr, ni
        return tuple(new)

    h = lax.fori_loop(0, S5_T, step, tuple(h_sc[lt] for lt in range(S5_LT)), unroll=8)
    for lt in range(S5_LT):
        h_sc[lt] = h[lt]
        hout_ref[0, 0, :, lt * LANE:(lt + 1) * LANE] = h[lt]

    for j in range(S5_CHAINS):
        hcat = jnp.concatenate([sc[lt, pl.ds(j, S5_T, stride=S5_CHAINS), :] for lt in range(S5_LT)], axis=1)
        cols = slice(j * S5_SLAB_CH, (j + 1) * S5_SLAB_CH)
        y = jnp.dot(hcat.astype(cdt), ccat_ref[0, j], preferred_element_type=F32) + d_ref[0, :, cols] * u_ref[:, cols]
        y_ref[:, cols] = _gelu_tanh(y)


def s5_prompt(proj, a_cat, bcat, ccat, d_skip, *, n_batch, seq_len, n_rows_out, cdt=BF16):
    n_t = seq_len // S5_T
    n_half = S5_WIDTH // S5_HALF_CH
    half = lambda t: t.reshape((n_half, S5_CHAINS) + t.shape[1:])
    return pl.pallas_call(
        functools.partial(_s5_prompt_kernel, cdt=cdt),
        grid=(n_batch, n_half, n_t),
        in_specs=[pl.BlockSpec((S5_T, S5_HALF_CH), lambda b, hf, t: (b * n_t + t, hf)),
                  pl.BlockSpec((1, S5_CHAINS, S5_SLAB_CH, 2 * S5_SLAB_ST), lambda b, hf, t: (hf, 0, 0, 0)),
                  pl.BlockSpec((1, S5_CHAINS, 2 * S5_SLAB_ST, S5_SLAB_CH), lambda b, hf, t: (hf, 0, 0, 0)),
                  pl.BlockSpec((1, S5_CHAINS, 2 * S5_SLAB_ST), lambda b, hf, t: (hf, 0, 0)),
                  pl.BlockSpec((1, 1, S5_HALF_CH), lambda b, hf, t: (hf, 0, 0))],
        out_specs=[pl.BlockSpec((S5_T, S5_HALF_CH), lambda b, hf, t: (b * n_t + t, hf)),
                   pl.BlockSpec((1, 1, S5_CHAINS, 2 * S5_SLAB_ST), lambda b, hf, t: (b, hf, 0, 0))],
        out_shape=[jax.ShapeDtypeStruct((n_rows_out, S5_WIDTH), F32),
                   jax.ShapeDtypeStruct((n_batch, n_half, S5_CHAINS, 2 * S5_SLAB_ST), F32)],
        scratch_shapes=[pltpu.VMEM((S5_LT, S5_T * S5_CHAINS, LANE), F32),
                        pltpu.VMEM((S5_LT, S5_CHAINS, LANE), F32)],
        compiler_params=pltpu.CompilerParams(
            dimension_semantics=("parallel", "parallel", "arbitrary"), vmem_limit_bytes=VMEM_LIMIT),
        name="s5_prompt",
    )(proj, half(bcat.astype(cdt)), half(ccat.astype(cdt)), half(a_cat), d_skip.reshape(n_half, 1, S5_HALF_CH))


def _s5_sample_kernel(u_ref, bcat_ref, ccat_ref, a_ref, d_ref, h0_ref, y_ref, hout_ref, sc, *, n_b, n_t, cdt):
    u = u_ref[...]
    bu = jnp.dot(u.astype(cdt), bcat_ref[0], preferred_element_type=F32)
    a_re = a_ref[0, :, :S5_SLAB_ST]
    a_im = a_ref[0, :, S5_SLAB_ST:]
    for bg in range(n_b // 8):
        hr = h0_ref[0, bg * 8:(bg + 1) * 8, :S5_SLAB_ST]
        hi = h0_ref[0, bg * 8:(bg + 1) * 8, S5_SLAB_ST:]
        for t in range(n_t):
            r = t * n_b + bg * 8
            hr, hi = (a_re * hr - a_im * hi + bu[r:r + 8, :S5_SLAB_ST],
                      a_re * hi + a_im * hr + bu[r:r + 8, S5_SLAB_ST:])
            sc[r:r + 8, :S5_SLAB_ST] = hr
            sc[r:r + 8, S5_SLAB_ST:] = hi
        hout_ref[0, bg * 8:(bg + 1) * 8, :S5_SLAB_ST] = hr
        hout_ref[0, bg * 8:(bg + 1) * 8, S5_SLAB_ST:] = hi
    y = jnp.dot(sc[...].astype(cdt), ccat_ref[0], preferred_element_type=F32) + d_ref[0] * u
    y_ref[...] = _gelu_tanh(y)


def s5_sample(u_tb, a_cat, bcat, ccat, d_skip, h0_cat, *, n_b, n_t, cdt=BF16):
    rows = n_t * n_b
    return pl.pallas_call(
        functools.partial(_s5_sample_kernel, n_b=n_b, n_t=n_t, cdt=cdt),
        grid=(S5_N_SLABS,),
        in_specs=[pl.BlockSpec((rows, S5_SLAB_CH), lambda s: (0, s)),
                  pl.BlockSpec((1, S5_SLAB_CH, 2 * S5_SLAB_ST), lambda s: (s, 0, 0)),
                  pl.BlockSpec((1, 2 * S5_SLAB_ST, S5_SLAB_CH), lambda s: (s, 0, 0)),
                  pl.BlockSpec((1, 1, 2 * S5_SLAB_ST), lambda s: (s, 0, 0)),
                  pl.BlockSpec((1, 1, S5_SLAB_CH), lambda s: (s, 0, 0)),
                  pl.BlockSpec((1, n_b, 2 * S5_SLAB_ST), lambda s: (s, 0, 0))],
        out_specs=[pl.BlockSpec((rows, S5_SLAB_CH), lambda s: (0, s)),
                   pl.BlockSpec((1, n_b, 2 * S5_SLAB_ST), lambda s: (s, 0, 0))],
        out_shape=[jax.ShapeDtypeStruct((rows, S5_WIDTH), F32),
                   jax.ShapeDtypeStruct((S5_N_SLABS, n_b, 2 * S5_SLAB_ST), F32)],
        scratch_shapes=[pltpu.VMEM((rows, 2 * S5_SLAB_ST), F32)],
        compiler_params=pltpu.CompilerParams(
            dimension_semantics=("arbitrary",), vmem_limit_bytes=VMEM_LIMIT),
        name="s5_sample",
    )(u_tb, bcat.astype(cdt), ccat.astype(cdt), a_cat.reshape(S5_N_SLABS, 1, 2 * S5_SLAB_ST),
      d_skip.reshape(S5_N_SLABS, 1, S5_SLAB_CH), h0_cat)


GLU_TM = 320
GLU_TN = 512


def _glu_kernel(yfull_ref, w_ref, ycol_ref, z_ref, o_ref, ybf_sc):
    @pl.when(pl.program_id(1) == 0)
    def _():
        ybf_sc[...] = yfull_ref[...].astype(ybf_sc.dtype)

    glu = jnp.dot(ybf_sc[...], w_ref[...], preferred_element_type=F32)
    z = z_ref[...]
    y = ycol_ref[...]
    o_ref[...] = (y * (1.0 / (1.0 + jnp.exp(-glu))) * (z * (1.0 / (1.0 + jnp.exp(-z))))).astype(o_ref.dtype)


def s5_glu_gate(y, w_glu, proj, row_off, cdt=BF16):
    m = y.shape[0]
    tm, tn = _pick_tile(m, (256, 128, 64, 32, 16)), GLU_TN
    assert row_off % tm == 0
    z_off, r_off = S5_WIDTH // tn, row_off // tm
    return pl.pallas_call(
        _glu_kernel,
        grid=(m // tm, S5_WIDTH // tn),
        in_specs=[pl.BlockSpec((tm, S5_WIDTH), lambda i, j: (i, 0)),
                  pl.BlockSpec((S5_WIDTH, tn), lambda i, j: (0, j)),
                  pl.BlockSpec((tm, tn), lambda i, j: (i, j)),
                  pl.BlockSpec((tm, tn), lambda i, j: (r_off + i, z_off + j))],
        out_specs=pl.BlockSpec((tm, tn), lambda i, j: (i, j)),
        out_shape=jax.ShapeDtypeStruct((m, S5_WIDTH), BF16),
        scratch_shapes=[pltpu.VMEM((tm, S5_WIDTH), cdt)],
        compiler_params=pltpu.CompilerParams(
            dimension_semantics=("parallel", "arbitrary"), vmem_limit_bytes=VMEM_LIMIT),
        name="s5_glu_gate",
    )(y, w_glu.astype(cdt), y, proj)


def s5_layer(proj, state_in, tables, d_skip, w_glu, *, n_batch, seq_len, n_dec, dec_len, cdt=BF16):
    a_cat, bcat, ccat = tables
    n_p = n_batch * seq_len
    n_s = n_dec * dec_len
    y_p, h_p = s5_prompt(proj, a_cat, bcat, ccat, d_skip, n_batch=n_batch, seq_len=seq_len, n_rows_out=n_p, cdt=cdt)
    u_tb = jnp.swapaxes(proj[n_p:, :S5_WIDTH].reshape(n_dec, dec_len, S5_WIDTH), 0, 1).reshape(n_s, S5_WIDTH)
    h0 = state_in.reshape(n_dec, S5_N_SLABS, S5_SLAB_ST, 2)
    h0_cat = jnp.concatenate([jnp.swapaxes(h0[..., 0], 0, 1), jnp.swapaxes(h0[..., 1], 0, 1)], axis=-1)
    y_tb, h_s = s5_sample(u_tb, a_cat, bcat, ccat, d_skip, h0_cat, n_b=n_dec, n_t=dec_len, cdt=cdt)
    y_s = jnp.swapaxes(y_tb.reshape(dec_len, n_dec, S5_WIDTH), 0, 1).reshape(n_s, S5_WIDTH)
    gated_p = s5_glu_gate(y_p, w_glu, proj, 0, cdt=cdt)
    gated_s = s5_glu_gate(y_s, w_glu, proj, n_p, cdt=cdt)
    hp = h_p.reshape(n_batch, S5_N_SLABS, 2, S5_SLAB_ST)
    st_p = jnp.stack([hp[:, :, 0], hp[:, :, 1]], axis=-1).reshape(n_batch, S5_GROUPS, S5_STATE, 2)
    hs = jnp.swapaxes(h_s, 0, 1).reshape(n_dec, S5_N_SLABS, 2, S5_SLAB_ST)
    st_s = jnp.stack([hs[:, :, 0], hs[:, :, 1]], axis=-1).reshape(n_dec, S5_GROUPS, S5_STATE, 2)
    return gated_p, gated_s, st_p, st_s


GDN_CONV_TT = 256
GDN_CONV_CW = 1024
GDN_HIST = 8
GDN_HB = 4
GDN_TT = 256
GDN_SAMPLE_ROWS = 8


def _gdn_conv_kernel(x_ref, hist_ref, w_ref, o_ref, *, rows, n_t, zero_first):
    i, j = pl.program_id(0), pl.program_id(1)
    hist = hist_ref[...]
    if zero_first:
        hist = jnp.where(i % n_t == 0, 0.0, hist)
    ext = jnp.concatenate([hist, x_ref[...]], axis=0)
    acc = ext[GDN_HIST:GDN_HIST + rows] * w_ref[GDN_CONV - 1:GDN_CONV, :]
    for s in range(1, GDN_CONV):
        acc = acc + ext[GDN_HIST - s:GDN_HIST - s + rows] * w_ref[GDN_CONV - 1 - s:GDN_CONV - s, :]
    conv = acc * (1.0 / (1.0 + jnp.exp(-acc)))
    n_qk_blocks = 2 * GDN_QK_WIDTH // GDN_CONV_CW

    @pl.when(j >= n_qk_blocks)
    def _():
        o_ref[...] = conv

    @pl.when(j < n_qk_blocks)
    def _():
        scale = jnp.where(j < GDN_QK_WIDTH // GDN_CONV_CW, GDN_DK ** -0.5, 1.0)
        for h in range(GDN_CONV_CW // GDN_DK):
            t = conv[:, h * GDN_DK:(h + 1) * GDN_DK]
            n = t * lax.rsqrt(jnp.sum(t * t, axis=-1, keepdims=True) + 1e-6)
            o_ref[:, h * GDN_DK:(h + 1) * GDN_DK] = n * scale


def gdn_conv(x, hist_src, conv_w, *, rows, n_blocks, n_t, data_map, hist_map, zero_first):
    n_out = n_blocks * rows
    return pl.pallas_call(
        functools.partial(_gdn_conv_kernel, rows=rows, n_t=n_t, zero_first=zero_first),
        grid=(n_blocks, GDN_CONV_CH // GDN_CONV_CW),
        in_specs=[pl.BlockSpec((rows, GDN_CONV_CW), lambda i, j: (data_map(i), j)),
                  pl.BlockSpec((GDN_HIST, GDN_CONV_CW), lambda i, j: (hist_map(i), j)),
                  pl.BlockSpec((GDN_CONV, GDN_CONV_CW), lambda i, j: (0, j))],
        out_specs=pl.BlockSpec((rows, GDN_CONV_CW), lambda i, j: (i, j)),
        out_shape=jax.ShapeDtypeStruct((n_out, GDN_CONV_CH), F32),
        compiler_params=pltpu.CompilerParams(
            dimension_semantics=("parallel", "parallel"), vmem_limit_bytes=VMEM_LIMIT),
        name="gdn_conv",
    )(x, hist_src, conv_w)


def _gdn_chunk_lockstep_kernel(q_ref, k_ref, v_ref, z_ref, ab_ref, alog_ref, dtb_ref, nw_ref, s0_ref, o_ref,
                               sout_ref, s_sc, *, chunk, n_inner, n_tt, valid_len):
    C = chunk
    hb, tt = pl.program_id(1), pl.program_id(2)

    @pl.when(tt == 0)
    def _():
        s_sc[...] = s0_ref[0]

    rowi = lax.broadcasted_iota(jnp.int32, (C, C), 0)
    coli = lax.broadcasted_iota(jnp.int32, (C, C), 1)
    causal = rowi >= coli
    strict = rowi > coli
    ltri = jnp.where(causal, 1.0, 0.0)
    utri = jnp.where(rowi <= coli, 1.0, 0.0)
    eye = jnp.where(rowi == coli, 1.0, 0.0)
    hi = lax.Precision.HIGHEST
    shift = (LANE - hb * GDN_HB) % LANE
    alog = pltpu.roll(jnp.broadcast_to(alog_ref[...], (8, LANE)), shift, 1)[0:1]
    dtb = pltpu.roll(jnp.broadcast_to(dtb_ref[...], (8, LANE)), shift, 1)[0:1]
    nw = nw_ref[...]
    tok_valid = lax.broadcasted_iota(jnp.int32, (C, LANE), 0) < valid_len
    dot = functools.partial(jnp.dot, preferred_element_type=F32)
    dot_nt = lambda a, b: lax.dot_general(a, b, (((1,), (1,)), ((), ())), preferred_element_type=F32)
    dot_tn = lambda a, b: lax.dot_general(a, b, (((0,), (0,)), ((), ())), preferred_element_type=F32)
    units = [(c, i) for c in range(n_inner) for i in range(GDN_HB)]
    rows = lambda c: slice(c * C, (c + 1) * C)
    qk_cols = lambda i: slice((i // 2) * GDN_DK, (i // 2 + 1) * GDN_DK)
    v_cols = lambda i: slice(i * GDN_DV, (i + 1) * GDN_DV)

    g_all, beta_all = [], []
    for c in range(n_inner):
        ab = pltpu.roll(ab_ref[rows(c), :], shift, 1)
        xa = ab + dtb
        softplus = jnp.maximum(xa, 0.0) + jnp.log1p(jnp.exp(-jnp.abs(xa)))
        g_all.append(jnp.where(tok_valid, -jnp.exp(alog) * softplus, 0.0))
        beta_all.append(jnp.where(tok_valid, 1.0 / (1.0 + jnp.exp(-ab)), 0.0))
    gam_all = [jnp.dot(ltri, g, preferred_element_type=F32, precision=hi) for g in g_all]
    gamT_all = [lax.dot_general(g, utri, (((0,), (0,)), ((), ())), preferred_element_type=F32, precision=hi)
                for g in g_all]

    qkk = [dot_nt(jnp.concatenate([q_ref[rows(c), qk_cols(i)], k_ref[rows(c), qk_cols(i)]], axis=0).astype(BF16),
                  k_ref[rows(c), qk_cols(i)].astype(BF16)) for c, i in units]
    gam_c = [jnp.broadcast_to(gam_all[c][:, i:i + 1], (C, LANE)) for c, i in units]
    beta_c = [jnp.broadcast_to(beta_all[c][:, 32 + i:33 + i], (C, LANE)) for c, i in units]
    gam_last = [jnp.broadcast_to(gam_all[c][C - 1:C, i:i + 1], (1, LANE)) for c, i in units]
    decay = [jnp.where(causal, jnp.exp(jnp.where(causal, gc[:, :C] - jnp.broadcast_to(gamT_all[c][i:i + 1, :], (C, C)),
                                                 0.0)), 0.0) for gc, (c, i) in zip(gam_c, units)]
    qk = [(x[:C] * d).astype(BF16) for x, d in zip(qkk, decay)]
    neg_a = [jnp.where(strict, -(b[:, :C] * x[C:] * d), 0.0) for b, x, d in zip(beta_c, qkk, decay)]
    p_inv = [eye + n for n in neg_a]
    m_pow = neg_a
    for _ in range(int(math.log2(C)) - 1):
        m_pow = [dot(m.astype(BF16), m.astype(BF16)) for m in m_pow]
        p_inv = [p + dot(p.astype(BF16), m.astype(BF16)) for p, m in zip(p_inv, m_pow)]
    eg = [jnp.exp(gc) for gc in gam_c]
    sol = [dot(p.astype(BF16), jnp.concatenate([b * v_ref[rows(c), v_cols(i)], (b * e) * k_ref[rows(c), qk_cols(i)]],
                                               axis=1).astype(BF16))
           for p, b, e, (c, i) in zip(p_inv, beta_c, eg, units)]
    wq = [jnp.concatenate([s[:, GDN_DV:], q_ref[rows(c), qk_cols(i)] * e], axis=0).astype(BF16)
          for s, e, (c, i) in zip(sol, eg, units)]
    k_dec = [(k_ref[rows(c), qk_cols(i)] * jnp.exp(gl - gc)).astype(BF16)
             for gl, gc, (c, i) in zip(gam_last, gam_c, units)]

    state = [s_sc[i] for i in range(GDN_HB)]
    for c in range(n_inner):
        base = c * GDN_HB
        ws = [dot(wq[base + i], state[i].astype(BF16)) for i in range(GDN_HB)]
        v_new = [(sol[base + i][:, :GDN_DV] - ws[i][:C]).astype(BF16) for i in range(GDN_HB)]
        o = [ws[i][C:] + dot(qk[base + i], v_new[i]) for i in range(GDN_HB)]
        state = [state[i] * jnp.exp(gam_last[base + i]) + dot_tn(k_dec[base + i], v_new[i]) for i in range(GDN_HB)]
        for i in range(GDN_HB):
            rms = lax.rsqrt(jnp.mean(o[i] * o[i], axis=-1, keepdims=True) + 1e-6)
            zz = z_ref[rows(c), v_cols(i)]
            o_ref[rows(c), v_cols(i)] = (o[i] * rms * nw * (zz * (1.0 / (1.0 + jnp.exp(-zz))))).astype(o_ref.dtype)
    for i in range(GDN_HB):
        s_sc[i] = state[i]

    @pl.when(tt == n_tt - 1)
    def _():
        sout_ref[0] = s_sc[...]


def gdn_chunk(conv, z, ab, a_log, dt_bias, norm_w, s0, *, n_seq, rows_per_seq, rows_per_step, chunk, valid_len,
              z_col_off, out_dtype):
    n_tt = rows_per_seq // rows_per_step
    n_inner = rows_per_step // chunk
    n_hb = GDN_V_HEADS // GDN_HB
    qw, vw = GDN_HB // 2 * GDN_DK, GDN_HB * GDN_DV
    k_off, v_off, z_off = GDN_QK_WIDTH // qw, 2 * GDN_QK_WIDTH // vw, z_col_off // vw
    row = lambda b, hb, t: b * n_tt + t
    pad_row = lambda p: jnp.pad(p.astype(F32), (0, LANE - p.shape[0])).reshape(1, LANE)
    return pl.pallas_call(
        functools.partial(_gdn_chunk_lockstep_kernel, chunk=chunk, n_inner=n_inner, n_tt=n_tt, valid_len=valid_len),
        grid=(n_seq, n_hb, n_tt),
        in_specs=[pl.BlockSpec((rows_per_step, qw), lambda b, hb, t: (row(b, hb, t), hb)),
                  pl.BlockSpec((rows_per_step, qw), lambda b, hb, t: (row(b, hb, t), k_off + hb)),
                  pl.BlockSpec((rows_per_step, vw), lambda b, hb, t: (row(b, hb, t), v_off + hb)),
                  pl.BlockSpec((rows_per_step, vw), lambda b, hb, t: (row(b, hb, t), z_off + hb)),
                  pl.BlockSpec((rows_per_step, LANE), lambda b, hb, t: (row(b, hb, t), 0)),
                  pl.BlockSpec((1, LANE), lambda b, hb, t: (0, 0)),
                  pl.BlockSpec((1, LANE), lambda b, hb, t: (0, 0)),
                  pl.BlockSpec((1, GDN_DV), lambda b, hb, t: (0, 0)),
                  pl.BlockSpec((1, GDN_HB, GDN_DK, GDN_DV), lambda b, hb, t: (b, hb, 0, 0))],
        out_specs=[pl.BlockSpec((rows_per_step, vw), lambda b, hb, t: (row(b, hb, t), hb)),
                   pl.BlockSpec((1, GDN_HB, GDN_DK, GDN_DV), lambda b, hb, t: (b, hb, 0, 0))],
        out_shape=[jax.ShapeDtypeStruct((n_seq * rows_per_seq, GDN_V_WIDTH), out_dtype),
                   jax.ShapeDtypeStruct((n_seq, GDN_V_HEADS, GDN_DK, GDN_DV), F32)],
        scratch_shapes=[pltpu.VMEM((GDN_HB, GDN_DK, GDN_DV), F32)],
        compiler_params=pltpu.CompilerParams(
            dimension_semantics=("parallel", "parallel", "arbitrary"), vmem_limit_bytes=VMEM_LIMIT),
        name="gdn_chunk",
    )(conv, conv, conv, z, ab, pad_row(a_log), pad_row(dt_bias), norm_w.astype(F32).reshape(1, GDN_DV), s0)


def gdn_layer(qkvz, ab, state_in, conv_in, conv_w, a_log, dt_bias, norm_w, *, n_batch, seq_len, n_dec, dec_len):
    n_p = n_batch * seq_len
    n_tp = seq_len // GDN_CONV_TT
    hist_per_block = GDN_CONV_TT // GDN_HIST
    conv_p = gdn_conv(qkvz, qkvz, conv_w, rows=GDN_CONV_TT, n_blocks=n_batch * n_tp, n_t=n_tp,
                      data_map=lambda i: i, hist_map=lambda i: jnp.maximum(i * hist_per_block - 1, 0),
                      zero_first=True)
    zeros_s = jnp.zeros((n_batch, GDN_V_HEADS, GDN_DK, GDN_DV), F32)
    gated_p, st_p = gdn_chunk(conv_p, qkvz, ab, a_log, dt_bias, norm_w, zeros_s, n_seq=n_batch,
                              rows_per_seq=seq_len, rows_per_step=GDN_TT, chunk=GDN_CHUNK, valid_len=GDN_CHUNK,
                              z_col_off=GDN_CONV_CH, out_dtype=BF16)
    buf_p = qkvz[:n_p, :GDN_CONV_CH].reshape(n_batch, seq_len, GDN_CONV_CH)[:, seq_len - (GDN_CONV - 1):]
    R = GDN_SAMPLE_ROWS
    x_s = qkvz[n_p:].reshape(n_dec, dec_len, -1)
    pad_t = lambda t, front: jnp.pad(t, ((0, 0), (front, R - front - t.shape[1]), (0, 0)))
    ext = jnp.concatenate([pad_t(conv_in, R - (GDN_CONV - 1)), pad_t(x_s[..., :GDN_CONV_CH], 0)], axis=1)
    ext = ext.reshape(n_dec * 2 * R, GDN_CONV_CH)
    conv_s = gdn_conv(ext, ext, conv_w, rows=R, n_blocks=n_dec, n_t=1,
                      data_map=lambda i: 2 * i + 1, hist_map=lambda i: 2 * i, zero_first=False)
    z_s = pad_t(x_s[..., GDN_CONV_CH:], 0).reshape(n_dec * R, GDN_V_WIDTH)
    ab_s = pad_t(ab[n_p:].reshape(n_dec, dec_len, LANE), 0).reshape(n_dec * R, LANE)
    gated_s, st_s = gdn_chunk(conv_s, z_s, ab_s, a_log, dt_bias, norm_w, state_in, n_seq=n_dec, rows_per_seq=R,
                              rows_per_step=R, chunk=R, valid_len=dec_len, z_col_off=0, out_dtype=F32)
    gated_s = gated_s.reshape(n_dec, R, GDN_V_WIDTH)[:, :dec_len].reshape(n_dec * dec_len, GDN_V_WIDTH)
    buf_s = jnp.concatenate([conv_in, x_s[..., :GDN_CONV_CH]], axis=1)[:, dec_len:]
    return gated_p, gated_s, st_p, st_s, buf_p, buf_s


DSA_KC = 256
INT_MIN = -2 ** 31
NEG_BIG = -1e30
G_D = N_HEADS // KV_D
BIAS_WIN = DSA_KC + Q_BLOCK


def _sortable_key(s):
    b = pltpu.bitcast(s, jnp.int32)
    return jnp.where(b < 0, b ^ jnp.int32(0x7FFFFFFF), b)


def _dsa_prompt_kernel(qT_ref, qiT_ref, wiT_ref, zT_ref, ki_ref, k_ref, vT_ref, win_ref, o_ref,
                       key_sc, mask_sc, *, topk, idx_bits, cdt):
    qb = pl.program_id(1)
    t0 = qb * Q_BLOCK
    nch = (qb + 2) // 2
    t_idx = t0 + lax.broadcasted_iota(jnp.int32, (1, Q_BLOCK), 1)
    row_iota = lax.broadcasted_iota(jnp.int32, (DSA_KC, Q_BLOCK), 0)

    def score_chunk(c, carry):
        kic = ki_ref[0, c].astype(cdt)
        acc = jnp.zeros((DSA_KC, Q_BLOCK), F32)
        for hp in range(IDX_HEADS // 2):
            rhs = jnp.concatenate([qiT_ref[(2 * hp) * IDX_DIM:(2 * hp + 1) * IDX_DIM, :],
                                   qiT_ref[(2 * hp + 1) * IDX_DIM:(2 * hp + 2) * IDX_DIM, :]], axis=1)
            s = jnp.dot(kic, rhs, preferred_element_type=F32) * (IDX_DIM ** -0.5)
            s = jnp.maximum(s, 0.0)
            w0 = wiT_ref[2 * hp:2 * hp + 1, :] * (IDX_HEADS ** -0.5)
            w1 = wiT_ref[2 * hp + 1:2 * hp + 2, :] * (IDX_HEADS ** -0.5)
            acc = acc + s[:, :Q_BLOCK] * w0 + s[:, Q_BLOCK:] * w1
        s_idx = c * DSA_KC + row_iota
        key_sc[c] = jnp.where(s_idx <= t_idx, _sortable_key(acc), INT_MIN)
        return carry

    lax.fori_loop(0, nch, score_chunk, 0)

    def count(pred):
        def body(c, acc):
            hit = pred(key_sc[c], c * DSA_KC + row_iota)
            return acc + hit.reshape(DSA_KC // 8, 8, Q_BLOCK).sum(axis=0)
        acc = lax.fori_loop(0, nch, body, jnp.zeros((8, Q_BLOCK), jnp.int32))
        return jnp.sum(acc, axis=0, keepdims=True)

    c_nonneg = count(lambda k, s: jnp.where(k >= 0, 1, 0))
    thr = jnp.where(c_nonneg >= topk, 0, INT_MIN).astype(jnp.int32)

    def thr_bit(i, thr):
        cand = thr + jnp.left_shift(jnp.int32(1), 30 - i)
        return jnp.where(count(lambda k, s: jnp.where(k >= cand, 1, 0)) >= topk, cand, thr)

    thr = lax.fori_loop(0, 31, thr_bit, thr)
    need = topk - count(lambda k, s: jnp.where(k > thr, 1, 0))

    def lim_bit(i, lim):
        cand = lim + jnp.left_shift(jnp.int32(1), idx_bits - 1 - i)
        c = count(lambda k, s: jnp.where(k == thr, jnp.where(s < cand, 1, 0), 0))
        return jnp.where(c <= need, cand, lim)

    lim = lax.fori_loop(0, idx_bits, lim_bit, jnp.zeros((1, Q_BLOCK), jnp.int32))

    def mask_chunk(c, carry):
        k = key_sc[c]
        s_idx = c * DSA_KC + row_iota
        tie = jnp.where(k == thr, jnp.where(s_idx < lim, 0.0, NEG_BIG), NEG_BIG)
        m = jnp.where(k > thr, 0.0, tie)
        mask_sc[c] = jnp.where(k == INT_MIN, NEG_BIG, m)
        return carry

    lax.fori_loop(0, nch, mask_chunk, 0)

    n_cols = G_D * Q_BLOCK
    half = DSA_KC // 2

    def head_q(j):
        qj = jnp.concatenate([qT_ref[(G_D * j + g) * HEAD_DIM:(G_D * j + g + 1) * HEAD_DIM, :]
                              for g in range(G_D)], axis=1)
        return (qj.astype(F32) * (HEAD_DIM ** -0.5)).astype(cdt)

    for jp in range(KV_D // 2):
        q0, q1 = head_q(2 * jp), head_q(2 * jp + 1)
        zq = jnp.zeros_like(q0)
        rhs = jnp.concatenate([jnp.concatenate([q0, zq], axis=0), jnp.concatenate([zq, q1], axis=0)], axis=1)

        def chunk_body(c, carry, jp=jp, rhs=rhs):
            m, l, acc0, acc1 = carry
            kc = k_ref[0, c, :, jp * 2 * HEAD_DIM:(jp + 1) * 2 * HEAD_DIM].astype(cdt)
            s = jnp.dot(kc, rhs, preferred_element_type=F32)
            wt = win_ref[qb - 2 * c]
            madd = mask_sc[c]
            parts = []
            for hh in range(2 * G_D):
                h = 2 * G_D * jp + hh
                tiles = []
                for u in range(2):
                    lo = (1 - u) * half
                    r = jnp.broadcast_to(wt[h:h + 1, lo:lo + 2 * half], (half, 2 * half))
                    tiles.append(pltpu.roll(r, 0, 1, stride=1, stride_axis=0)[:, half:])
                parts.append(s[:, hh * Q_BLOCK:(hh + 1) * Q_BLOCK] + (jnp.concatenate(tiles, axis=0) + madd))
            s = jnp.concatenate(parts, axis=1)
            m_new = jnp.maximum(m, jnp.max(s, axis=0, keepdims=True))
            alpha = jnp.exp(m - m_new)
            p = jnp.exp(s - m_new)
            l = l * alpha + jnp.sum(p, axis=0, keepdims=True)
            p = p.astype(cdt)
            v0 = vT_ref[0, c, (2 * jp) * HEAD_DIM:(2 * jp + 1) * HEAD_DIM, :]
            v1 = vT_ref[0, c, (2 * jp + 1) * HEAD_DIM:(2 * jp + 2) * HEAD_DIM, :]
            acc0 = acc0 * alpha[:, :n_cols] + jnp.dot(v0, p[:, :n_cols], preferred_element_type=F32)
            acc1 = acc1 * alpha[:, n_cols:] + jnp.dot(v1, p[:, n_cols:], preferred_element_type=F32)
            return m_new, l, acc0, acc1

        init = (jnp.full((1, 2 * n_cols), NEG_BIG, F32), jnp.zeros((1, 2 * n_cols), F32),
                jnp.zeros((HEAD_DIM, n_cols), F32), jnp.zeros((HEAD_DIM, n_cols), F32))
        m, l, acc0, acc1 = lax.fori_loop(0, nch, chunk_body, init)
        inv = 1.0 / l
        for jj, acc in enumerate((acc0, acc1)):
            o = acc * inv[:, jj * n_cols:(jj + 1) * n_cols]
            for g in range(G_D):
                r0 = (G_D * (2 * jp + jj) + g) * HEAD_DIM
                z = zT_ref[r0:r0 + HEAD_DIM, :]
                gate = z * (1.0 / (1.0 + jnp.exp(-z)))
                o_ref[r0:r0 + HEAD_DIM, :] = (o[:, g * Q_BLOCK:(g + 1) * Q_BLOCK] * gate).astype(o_ref.dtype)


def dsa_bias_windows(rel_bias, seq_len):
    o = jnp.arange(seq_len // Q_BLOCK)[:, None]
    m = jnp.arange(BIAS_WIN)[None, :]
    d = jnp.maximum(o * Q_BLOCK + m - DSA_KC, 0)
    return jnp.moveaxis(rel_bias[rel_bucket(d)].astype(F32), -1, 1)


def dsa_prompt_attend(qqiT, wiT, zT, ki4, k4, v4T, win, *, n_batch, seq_len, cdt=BF16):
    nqb = seq_len // Q_BLOCK
    nc = seq_len // DSA_KC
    topk = min(TOPK_MAX, seq_len // 4)
    idx_bits = int(math.log2(seq_len)) + 1
    tok = lambda b, q: (0, b * nqb + q)
    per_batch = lambda b, q: (b, 0, 0, 0)
    return pl.pallas_call(
        functools.partial(_dsa_prompt_kernel, topk=topk, idx_bits=idx_bits, cdt=cdt),
        grid=(n_batch, nqb),
        in_specs=[pl.BlockSpec((ATT_WIDTH, Q_BLOCK), tok),
                  pl.BlockSpec((IDX_HEADS * IDX_DIM, Q_BLOCK), lambda b, q: (1, b * nqb + q)),
                  pl.BlockSpec((IDX_HEADS, Q_BLOCK), tok),
                  pl.BlockSpec((ATT_WIDTH, Q_BLOCK), tok),
                  pl.BlockSpec((1, nc, DSA_KC, IDX_DIM), per_batch),
                  pl.BlockSpec((1, nc, DSA_KC, D_KV), per_batch),
                  pl.BlockSpec((1, nc, D_KV, DSA_KC), per_batch),
                  pl.BlockSpec((nqb, N_HEADS, BIAS_WIN), lambda b, q: (0, 0, 0))],
        out_specs=pl.BlockSpec((ATT_WIDTH, Q_BLOCK), tok),
        out_shape=jax.ShapeDtypeStruct((ATT_WIDTH, n_batch * seq_len), BF16),
        scratch_shapes=[pltpu.VMEM((nc, DSA_KC, Q_BLOCK), jnp.int32),
                        pltpu.VMEM((nc, DSA_KC, Q_BLOCK), F32)],
        compiler_params=pltpu.CompilerParams(
            dimension_semantics=("parallel", "arbitrary"), vmem_limit_bytes=VMEM_LIMIT),
        name="dsa_prompt_attend",
    )(qqiT, qqiT, wiT, zT, ki4, k4, v4T, win)


DSS_NP1 = 16
DSS_NP2 = 8
DSS_TP = 8
DSS_ROWS = 2 * G_D * DSS_TP
T5_LAST_BUCKET_DIST = 1600


def _dsa_sample_select_kernel(pt_ref, qi_ref, wb_ref, kinew_ref, *rest, n_pages, n_new, topk, idx_bits, cdt):
    del pt_ref
    page_refs, mask_ref, key_sc = rest[:DSS_NP1], rest[DSS_NP1], rest[DSS_NP1 + 1]
    s = pl.program_id(1)
    lane = lax.broadcasted_iota(jnp.int32, (DSS_TP, PAGE_SIZE), 1)
    trow = lax.broadcasted_iota(jnp.int32, (DSS_TP, PAGE_SIZE), 0)
    qi = qi_ref[0]
    wb = wb_ref[0]

    def page_keys(kp):
        sc = lax.dot_general(qi, kp.astype(cdt), (((1,), (1,)), ((), ())),
                             preferred_element_type=F32) * (IDX_DIM ** -0.5)
        sc = jnp.maximum(sc, 0.0) * wb
        return _sortable_key(sc.reshape(IDX_HEADS, DSS_TP, PAGE_SIZE).sum(axis=0))

    for i in range(DSS_NP1):
        key_sc[s * DSS_NP1 + i] = page_keys(page_refs[i][0, 0])

    @pl.when(s == 0)
    def _():
        kn = page_keys(kinew_ref[0])
        key_sc[n_pages] = jnp.where(lane < n_new, jnp.where(lane <= trow, kn, INT_MIN), INT_MIN)

    @pl.when(s == n_pages // DSS_NP1 - 1)
    def _():
        all_shape = (n_pages + 1, DSS_TP, PAGE_SIZE)

        def count(pred):
            key_idx = (lax.broadcasted_iota(jnp.int32, all_shape, 0) * PAGE_SIZE
                       + lax.broadcasted_iota(jnp.int32, all_shape, 2))
            acc = pred(key_sc[...], key_idx).sum(axis=0)
            return jnp.broadcast_to(jnp.sum(acc, axis=1, keepdims=True), (DSS_TP, PAGE_SIZE))

        c_nonneg = count(lambda k, i: jnp.where(k >= 0, 1, 0))
        thr = jnp.where(c_nonneg >= topk, 0, INT_MIN).astype(jnp.int32)

        def thr_bit(b, thr):
            cand = thr + jnp.left_shift(jnp.int32(1), 30 - b)
            return jnp.where(count(lambda k, i: jnp.where(k >= cand, 1, 0)) >= topk, cand, thr)

        thr = lax.fori_loop(0, 31, thr_bit, thr)
        need = topk - count(lambda k, i: jnp.where(k > thr, 1, 0))

        def lim_bit(b, lim):
            cand = lim + jnp.left_shift(jnp.int32(1), idx_bits - 1 - b)
            c = count(lambda k, i: jnp.where(k == thr, jnp.where(i < cand, 1, 0), 0))
            return jnp.where(c <= need, cand, lim)

        lim = lax.fori_loop(0, idx_bits, lim_bit, jnp.zeros((DSS_TP, PAGE_SIZE), jnp.int32))

        k = key_sc[...]
        key_idx = (lax.broadcasted_iota(jnp.int32, all_shape, 0) * PAGE_SIZE
                   + lax.broadcasted_iota(jnp.int32, all_shape, 2))
        tie = jnp.where(k == thr, jnp.where(key_idx < lim, 0.0, NEG_BIG), NEG_BIG)
        mask_ref[0] = jnp.where(k == INT_MIN, NEG_BIG, jnp.where(k > thr, 0.0, tie))


def _dsa_sample_attend_kernel(pt_ref, qT_ref, zT_ref, mask_ref, masknew_ref, bnear_ref, bfar_ref, hmask_ref,
                              expand_ref, kvnew_ref, *rest, n_pages, n_far, cdt):
    del pt_ref
    page_refs, o_ref = rest[:DSS_NP2], rest[DSS_NP2]
    m_sc, l_sc, acc_sc = rest[DSS_NP2 + 1:]
    s = pl.program_id(1)
    n_rows = PAGE_SIZE * KV_D
    qT = qT_ref[0]
    hmask = hmask_ref[...]
    expand = expand_ref[...]

    def rows_of(t2d):
        return jnp.broadcast_to(t2d[:, None, :], (PAGE_SIZE, KV_D, LANE)).reshape(n_rows, LANE)

    def attend(pg, mask_tile, pid):
        k2 = pg[:, 0].reshape(n_rows, HEAD_DIM).astype(cdt)
        v2 = pg[:, 1].reshape(n_rows, HEAD_DIM).astype(cdt)
        sel = jnp.where(mask_tile == 0.0, 1.0, 0.0)
        sel_cols = lax.dot_general(sel, expand, (((0,), (0,)), ((), ())), preferred_element_type=F32)
        bias = jnp.where(pid >= n_far, bnear_ref[jnp.maximum(pid - n_far, 0)], bfar_ref[...])
        add2d = bias + jnp.where(sel_cols > 0.5, 0.0, NEG_BIG)
        add = (jnp.broadcast_to(add2d[:, None, :], (PAGE_SIZE, KV_D, LANE)) + hmask[None]).reshape(n_rows, LANE)
        logits = jnp.dot(k2, qT, preferred_element_type=F32) + add
        m_old = m_sc[...]
        m_new = jnp.maximum(m_old, jnp.max(logits, axis=0, keepdims=True))
        alpha = jnp.exp(m_old - m_new)
        p = jnp.exp(logits - m_new)
        l_sc[...] = l_sc[...] * alpha + jnp.sum(p, axis=0, keepdims=True)
        acc_sc[...] = acc_sc[...] * alpha + lax.dot_general(v2, p.astype(cdt), (((0,), (0,)), ((), ())),
                                                            preferred_element_type=F32)
        m_sc[...] = m_new

    @pl.when(s == 0)
    def _():
        m_sc[...] = jnp.full_like(m_sc, NEG_BIG)
        l_sc[...] = jnp.zeros_like(l_sc)
        acc_sc[...] = jnp.zeros_like(acc_sc)
        attend(kvnew_ref[0], masknew_ref[0, 0], n_pages)

    for i in range(DSS_NP2):
        attend(page_refs[i][0, 0], mask_ref[0, i], s * DSS_NP2 + i)

    @pl.when(s == n_pages // DSS_NP2 - 1)
    def _():
        z = zT_ref[0]
        o_ref[0] = acc_sc[...] * (1.0 / l_sc[...]) * (z * (1.0 / (1.0 + jnp.exp(-z))))


def dsa_sample(q_s, z_s, qi_s, wi_s, ki_s, kv_s, kv_pool, kidx_pool, layer, page_table, rel_bias, cdt=BF16):
    n_dec, n_pages = page_table.shape
    dec_len = q_s.shape[0] // n_dec
    past = n_pages * PAGE_SIZE
    total = past + dec_len
    topk = min(TOPK_MAX, total // 4)
    idx_bits = int(math.log2(total)) + 1
    pad_t = DSS_TP - dec_len
    n_pairs = KV_D // 2
    eye2 = jnp.eye(2, dtype=F32)

    qi = jnp.pad(jnp.swapaxes(qi_s.reshape(n_dec, dec_len, IDX_HEADS, IDX_DIM), 1, 2), ((0, 0), (0, 0), (0, pad_t), (0, 0)))
    qi = qi.reshape(n_dec, IDX_HEADS * DSS_TP, IDX_DIM).astype(cdt)
    wb = jnp.pad(jnp.swapaxes(wi_s.reshape(n_dec, dec_len, IDX_HEADS), 1, 2) * (IDX_HEADS ** -0.5), ((0, 0), (0, 0), (0, pad_t)))
    wb = jnp.broadcast_to(wb.reshape(n_dec, IDX_HEADS * DSS_TP, 1), (n_dec, IDX_HEADS * DSS_TP, PAGE_SIZE))
    ki_new = jnp.pad(ki_s.reshape(n_dec, dec_len, IDX_DIM), ((0, 0), (0, PAGE_SIZE - dec_len), (0, 0)))
    kidx4 = kidx_pool.reshape(kidx_pool.shape[0], kidx_pool.shape[1], PAGE_SIZE, IDX_DIM)
    page_spec = lambda np_, i, width: pl.BlockSpec(
        (1, 1, PAGE_SIZE, width), lambda b, s, pt: (layer, pt[b, s * np_ + i], 0, 0))
    per_b3 = lambda b, s, pt: (b, 0, 0)
    mask = pl.pallas_call(
        functools.partial(_dsa_sample_select_kernel, n_pages=n_pages, n_new=dec_len, topk=topk, idx_bits=idx_bits,
                          cdt=cdt),
        grid_spec=pltpu.PrefetchScalarGridSpec(
            num_scalar_prefetch=1, grid=(n_dec, n_pages // DSS_NP1),
            in_specs=[pl.BlockSpec((1, IDX_HEADS * DSS_TP, IDX_DIM), per_b3),
                      pl.BlockSpec((1, IDX_HEADS * DSS_TP, PAGE_SIZE), per_b3),
                      pl.BlockSpec((1, PAGE_SIZE, IDX_DIM), per_b3)]
                     + [page_spec(DSS_NP1, i, IDX_DIM) for i in range(DSS_NP1)],
            out_specs=pl.BlockSpec((1, n_pages + 1, DSS_TP, PAGE_SIZE), lambda b, s, pt: (b, 0, 0, 0)),
            scratch_shapes=[pltpu.VMEM((n_pages + 1, DSS_TP, PAGE_SIZE), jnp.int32)]),
        out_shape=jax.ShapeDtypeStruct((n_dec, n_pages + 1, DSS_TP, PAGE_SIZE), F32),
        compiler_params=pltpu.CompilerParams(
            dimension_semantics=("parallel", "arbitrary"), vmem_limit_bytes=VMEM_LIMIT),
        name="dsa_sample_select",
    )(page_table, qi, wb, ki_new, *([kidx4] * DSS_NP1))

    n_cols = N_HEADS * dec_len
    assert n_cols == LANE, "query columns (heads x decode steps) must fill one lane tile"
    cols_T = lambda t: t.reshape(n_dec, dec_len, N_HEADS, HEAD_DIM).transpose(0, 3, 2, 1).reshape(n_dec, HEAD_DIM, n_cols)
    qT = (cols_T(q_s) * (HEAD_DIM ** -0.5)).astype(cdt)
    zT = cols_T(z_s)
    n_far = max(0, min(n_pages, (past - (PAGE_SIZE - 1) - T5_LAST_BUCKET_DIST) // PAGE_SIZE + 1))
    near_pages = jnp.arange(n_far, n_pages + 1)
    dist = (past + jnp.arange(dec_len)[None, None, :]
            - (near_pages[:, None, None] * PAGE_SIZE + jnp.arange(PAGE_SIZE)[None, :, None]))
    b_near = rel_bias[rel_bucket(dist)].astype(F32)
    b_near = b_near.transpose(0, 1, 3, 2).reshape(n_pages + 1 - n_far, PAGE_SIZE, n_cols)
    b_far = jnp.broadcast_to(rel_bias[N_BUCKETS - 1].astype(F32)[:, None], (N_HEADS, dec_len)).reshape(1, n_cols)
    col = jnp.arange(n_cols)[None, :]
    hmask = jnp.where(col // (G_D * dec_len) == jnp.arange(KV_D)[:, None], 0.0, NEG_BIG).astype(F32)
    expand = jnp.where(col % dec_len == jnp.arange(DSS_TP)[:, None], 1.0, 0.0).astype(F32)
    kv_new = jnp.pad(kv_s.reshape(n_dec, dec_len, 2, KV_D, HEAD_DIM),
                     ((0, 0), (0, PAGE_SIZE - dec_len), (0, 0), (0, 0), (0, 0)))
    kv6 = kv_pool.reshape(kv_pool.shape[0], kv_pool.shape[1], PAGE_SIZE, 2, KV_D, HEAD_DIM)
    kv_page = lambda i: pl.BlockSpec((1, 1, PAGE_SIZE, 2, KV_D, HEAD_DIM),
                                     lambda b, s, pt: (layer, pt[b, s * DSS_NP2 + i], 0, 0, 0, 0))
    full2 = lambda a: pl.BlockSpec(a.shape, lambda b, s, pt: (0, 0))
    o = pl.pallas_call(
        functools.partial(_dsa_sample_attend_kernel, n_pages=n_pages, n_far=n_far, cdt=cdt),
        grid_spec=pltpu.PrefetchScalarGridSpec(
            num_scalar_prefetch=1, grid=(n_dec, n_pages // DSS_NP2),
            in_specs=[pl.BlockSpec((1, HEAD_DIM, n_cols), per_b3),
                      pl.BlockSpec((1, HEAD_DIM, n_cols), per_b3),
                      pl.BlockSpec((1, DSS_NP2, DSS_TP, PAGE_SIZE), lambda b, s, pt: (b, s, 0, 0)),
                      pl.BlockSpec((1, 1, DSS_TP, PAGE_SIZE), lambda b, s, pt: (b, n_pages, 0, 0)),
                      pl.BlockSpec(b_near.shape, lambda b, s, pt: (0, 0, 0)),
                      full2(b_far), full2(hmask), full2(expand),
                      pl.BlockSpec((1, PAGE_SIZE, 2, KV_D, HEAD_DIM), lambda b, s, pt: (b, 0, 0, 0, 0))]
                     + [kv_page(i) for i in range(DSS_NP2)],
            out_specs=pl.BlockSpec((1, HEAD_DIM, n_cols), per_b3),
            scratch_shapes=[pltpu.VMEM((1, n_cols), F32), pltpu.VMEM((1, n_cols), F32),
                            pltpu.VMEM((HEAD_DIM, n_cols), F32)]),
        out_shape=jax.ShapeDtypeStruct((n_dec, HEAD_DIM, n_cols), F32),
        compiler_params=pltpu.CompilerParams(
            dimension_semantics=("parallel", "arbitrary"), vmem_limit_bytes=VMEM_LIMIT),
        name="dsa_sample_attend",
    )(page_table, qT, zT, mask, mask, b_near, b_far, hmask, expand, kv_new, *([kv6] * DSS_NP2))
    return o.reshape(n_dec, HEAD_DIM, N_HEADS, dec_len).transpose(0, 3, 2, 1).reshape(n_dec * dec_len, ATT_WIDTH)


def _pad_cols(w, n):
    return jnp.pad(w, ((0, 0), (0, n - w.shape[1])))


def kernel(x_prompt, x_sample, cache_a_kv, state_s5, state_gdn, state_gdn_conv, cache_d_kv, cache_d_kidx,
           page_table, p_prompt, p_sample, rel_bias, ln_g, ln_b, ple_gate_w, ple_w,
           a_w_in, a_sinks, a_w_out,
           s5_w_in, s5_a_re, s5_a_im, s5_b_re, s5_b_im, s5_c_re, s5_c_im, s5_d, s5_log_dt, s5_w_glu, s5_w_out,
           gdn_w_in, gdn_conv_w, gdn_a_log, gdn_dt_bias, gdn_norm_w, gdn_w_out,
           dsa_w_in, dsa_w_out):
    x = join_tokens(x_prompt, x_sample)
    x_bf = x.astype(BF16)
    outs = {}

    def post(i, x, h):
        p_bf = join_tokens(p_prompt[i], p_sample[i]).astype(BF16)
        return post_norm_ple(x, h, p_bf, ln_g[i], ln_b[i], ple_gate_w[i].astype(BF16), ple_w[i].astype(BF16))

    w_in = a_w_in[0]
    c_k, c_v, c_z = ATT_WIDTH, ATT_WIDTH + A_KV, ATT_WIDTH + 2 * A_KV
    w_q, w_z = w_in[:, :c_k], w_in[:, c_z:]
    w_out_bf = a_w_out[0].astype(BF16)
    kv_nat = matmul(x_bf, w_in[:, c_k:c_z].astype(BF16))
    xT_bf = x_bf[:N_PROMPT_TOK].T
    qvT = matmul(jnp.concatenate([w_q, w_in[:, c_v:c_z]], axis=1).T.astype(BF16), xT_bf, out_dtype=BF16)
    zT = matmul(w_z.T.astype(BF16), xT_bf)
    bias_p, sink_p, bias_s, sink_s = swa_tables(rel_bias, a_sinks[0], DEC_SEQ)
    h_p = matmul_ta(swa_prompt(qvT, zT, kv_nat, bias_p, sink_p, n_batch=BATCH, seq_len=SEQ), w_out_bf)
    qz_s = matmul(x_bf[N_PROMPT_TOK:], jnp.concatenate([w_q, w_z], axis=1).astype(BF16))
    gs, outs['a_s'] = swa_sample(qz_s[:, :ATT_WIDTH], qz_s[:, ATT_WIDTH:], kv_nat[N_PROMPT_TOK:], cache_a_kv[0],
                                 bias_s, sink_s)
    outs['a_p'] = kv_nat[:N_PROMPT_TOK].reshape(BATCH, SEQ, 2, KV_A, HEAD_DIM)[:, SEQ - WINDOW:]
    x, x_bf = post(0, x, jnp.concatenate([h_p, matmul(gs.astype(BF16), w_out_bf)], axis=0))

    proj = matmul(x_bf, s5_w_in[0].astype(BF16))
    tables = s5_tables(s5_a_re[0], s5_a_im[0], s5_b_re[0], s5_b_im[0], s5_c_re[0], s5_c_im[0], s5_log_dt[0])
    gp, gs, outs['s5_p'], outs['s5_s'] = s5_layer(proj, state_s5[0], tables, s5_d[0], s5_w_glu[0],
                                                  n_batch=BATCH, seq_len=SEQ, n_dec=DEC_BATCH, dec_len=DEC_SEQ)
    w_out_bf = s5_w_out[0].astype(BF16)
    x, x_bf = post(1, x, jnp.concatenate([matmul(gp, w_out_bf), matmul(gs, w_out_bf)], axis=0))

    w_in = gdn_w_in[0]
    c_gz = GDN_CONV_CH + GDN_V_WIDTH
    qkvz = matmul(x_bf, w_in[:, :c_gz].astype(BF16))
    ab = matmul(x_bf, _pad_cols(w_in[:, c_gz:], LANE).astype(BF16))
    gp, gs, outs['gd_p'], outs['gd_s'], outs['gc_p'], outs['gc_s'] = gdn_layer(
        qkvz, ab, state_gdn[0], state_gdn_conv[0], gdn_conv_w[0], gdn_a_log[0], gdn_dt_bias[0], gdn_norm_w[0],
        n_batch=BATCH, seq_len=SEQ, n_dec=DEC_BATCH, dec_len=DEC_SEQ)
    w_out_bf = gdn_w_out[0].astype(BF16)
    x, x_bf = post(2, x, jnp.concatenate([matmul(gp, w_out_bf), matmul(gs.astype(BF16), w_out_bf)], axis=0))

    w_in = dsa_w_in[0]
    c_z = 2 * ATT_WIDTH + 2 * D_KV
    c_qi = c_z + IDX_HEADS * IDX_DIM
    c_kv = ATT_WIDTH + 2 * D_KV
    w_q, w_kv, w_z, w_qi = w_in[:, :ATT_WIDTH], w_in[:, ATT_WIDTH:c_kv], w_in[:, c_kv:c_z], w_in[:, c_z:c_qi]
    w_out_bf = dsa_w_out[0].astype(BF16)
    kv_nat = matmul(x_bf, w_kv.astype(BF16))
    kiw = matmul(x_bf, _pad_cols(w_in[:, c_qi:], 2 * LANE).astype(BF16))
    xT_bf = x_bf[:N_PROMPT_TOK].T
    qqiT = matmul(jnp.concatenate([w_q, w_qi], axis=1).T.astype(BF16), xT_bf, out_dtype=BF16)
    zT = matmul(w_z.T.astype(BF16), xT_bf)
    wiT = matmul(w_in[:, c_qi + IDX_DIM:].T.astype(BF16), xT_bf)
    nc = SEQ // DSA_KC
    kv_p = kv_nat[:N_PROMPT_TOK]
    v4T = jnp.swapaxes(kv_p[:, D_KV:].astype(BF16).reshape(BATCH, nc, DSA_KC, D_KV), 2, 3)
    gT = dsa_prompt_attend(qqiT, wiT, zT, kiw[:N_PROMPT_TOK].reshape(BATCH, nc, DSA_KC, 2 * LANE),
                           kv_p.reshape(BATCH, nc, DSA_KC, 2 * D_KV), v4T, dsa_bias_windows(rel_bias, SEQ),
                           n_batch=BATCH, seq_len=SEQ)
    h_p = matmul_ta(gT, w_out_bf)
    x_s = x_bf[N_PROMPT_TOK:]
    qzqi_s = matmul(x_s, jnp.concatenate([w_q, w_z, w_qi], axis=1).astype(BF16))
    kiw_s = kiw[N_PROMPT_TOK:]
    gs = dsa_sample(qzqi_s[:, :ATT_WIDTH], qzqi_s[:, ATT_WIDTH:2 * ATT_WIDTH], qzqi_s[:, 2 * ATT_WIDTH:],
                    kiw_s[:, IDX_DIM:IDX_DIM + IDX_HEADS], kiw_s[:, :IDX_DIM], kv_nat[N_PROMPT_TOK:],
                    cache_d_kv, cache_d_kidx, 0, page_table, rel_bias)
    h_s = matmul(gs.astype(BF16), w_out_bf)
    outs['dkv_p'] = kv_p.reshape(BATCH, SEQ, 2, KV_D, HEAD_DIM)
    outs['dkv_s'] = kv_nat[N_PROMPT_TOK:].reshape(DEC_BATCH, DEC_SEQ, 2, KV_D, HEAD_DIM)
    outs['dki_p'] = kiw[:N_PROMPT_TOK, :IDX_DIM].reshape(BATCH, SEQ, IDX_DIM)
    outs['dki_s'] = kiw_s[:, :IDX_DIM].reshape(DEC_BATCH, DEC_SEQ, IDX_DIM)
    x, x_bf = post(3, x, jnp.concatenate([h_p, h_s], axis=0))

    yp, ys = split_tokens(x)
    st = lambda name: outs[name][None]
    return (yp, ys, st('a_p'), st('a_s'), st('s5_p'), st('s5_s'), st('gd_p'), st('gd_s'),
            st('gc_p'), st('gc_s'), st('dkv_p'), st('dkv_s'), st('dki_p'), st('dki_s'))
```

```python
import functools
import math

import jax
import jax.numpy as jnp
from jax import lax
from jax.experimental import pallas as pl
from jax.experimental.pallas import tpu as pltpu

D_MODEL = 2048
BATCH = 4
SEQ = 2048
DEPTH = 4
DEC_BATCH = 32
DEC_SEQ = 4
PAGE_SIZE = 128
N_MIXERS = 4
PLE_DIM = 256
ALPHA = (2 * DEPTH) ** 0.25
LN_EPS = 1e-5
N_BUCKETS = 32
REL_MAX_DIST = 2048
N_HEADS = 32
HEAD_DIM = 64
ATT_WIDTH = N_HEADS * HEAD_DIM
WINDOW = 128
KV_A = 4
A_KV = KV_A * HEAD_DIM
KV_D = 8
D_KV = KV_D * HEAD_DIM
IDX_HEADS = 16
IDX_DIM = 128
TOPK_MAX = 256
Q_BLOCK = 128
S5_WIDTH = D_MODEL
S5_GROUP = 16
S5_GROUPS = S5_WIDTH // S5_GROUP
S5_STATE = 64
GDN_QK_HEADS = 16
GDN_V_HEADS = 32
GDN_DK = 128
GDN_DV = 128
GDN_CONV = 4
GDN_CHUNK = 64
GDN_QK_WIDTH = GDN_QK_HEADS * GDN_DK
GDN_V_WIDTH = GDN_V_HEADS * GDN_DV
GDN_CONV_CH = 2 * GDN_QK_WIDTH + GDN_V_WIDTH

F32 = jnp.float32
BF16 = jnp.bfloat16

N_PROMPT_TOK = BATCH * SEQ
N_SAMPLE_TOK = DEC_BATCH * DEC_SEQ
N_TOK = N_PROMPT_TOK + N_SAMPLE_TOK

V7X_VMEM_BYTES = 64 * 1024 * 1024
VMEM_LIMIT = 48 * 1024 * 1024
LANE = 128


def _mm_kernel(x_ref, w_ref, o_ref):
    o_ref[...] = jnp.dot(x_ref[...], w_ref[...], preferred_element_type=F32).astype(o_ref.dtype)


def _pick_tile(n, prefs):
    for t in prefs:
        if n % t == 0:
            return t
    raise ValueError(f"no tile for {n}")


def matmul(x, w, out_dtype=F32):
    m, k = x.shape
    n = w.shape[1]
    tm = _pick_tile(m, (640, 512, 320, 256, 128, 64, 32, 16, 8))
    tn = _pick_tile(n, (512, 384, 256, 128))
    return pl.pallas_call(
        _mm_kernel,
        grid=(m // tm, n // tn),
        in_specs=[pl.BlockSpec((tm, k), lambda i, j: (i, 0)),
                  pl.BlockSpec((k, tn), lambda i, j: (0, j))],
        out_specs=pl.BlockSpec((tm, tn), lambda i, j: (i, j)),
        out_shape=jax.ShapeDtypeStruct((m, n), out_dtype),
        compiler_params=pltpu.CompilerParams(
            dimension_semantics=("parallel", "parallel"), vmem_limit_bytes=VMEM_LIMIT),
        name="proj_matmul",
    )(x, w)


def _mm_ta_kernel(xt_ref, w_ref, o_ref):
    o_ref[...] = lax.dot_general(xt_ref[...], w_ref[...], (((0,), (0,)), ((), ())),
                                 preferred_element_type=F32).astype(o_ref.dtype)


def matmul_ta(xt, w, out_dtype=F32):
    k, m = xt.shape
    n = w.shape[1]
    tm = _pick_tile(m, (512, 256, 128))
    tn = _pick_tile(n, (512, 384, 256, 128))
    return pl.pallas_call(
        _mm_ta_kernel,
        grid=(m // tm, n // tn),
        in_specs=[pl.BlockSpec((k, tm), lambda i, j: (0, i)),
                  pl.BlockSpec((k, tn), lambda i, j: (0, j))],
        out_specs=pl.BlockSpec((tm, tn), lambda i, j: (i, j)),
        out_shape=jax.ShapeDtypeStruct((m, n), out_dtype),
        compiler_params=pltpu.CompilerParams(
            dimension_semantics=("parallel", "parallel"), vmem_limit_bytes=VMEM_LIMIT),
        name="proj_matmul_ta",
    )(xt, w)


POST_TM = 320
POST_TN = 512


def _post_kernel(x_ref, h_ref, p_ref, g_ref, b_ref, wg_ref, wp_ref, o_ref, obf_ref, y_sc, ybf_sc):
    j = pl.program_id(1)

    @pl.when(j == 0)
    def _():
        t = ALPHA * x_ref[...] + h_ref[...]
        mu = jnp.mean(t, axis=-1, keepdims=True)
        d = t - mu
        var = jnp.mean(d * d, axis=-1, keepdims=True)
        y = d * lax.rsqrt(var + LN_EPS) * g_ref[...] + b_ref[...]
        ybf_sc[...] = y.astype(BF16)
        for jj in range(D_MODEL // POST_TN):
            y_sc[jj] = y[:, jj * POST_TN:(jj + 1) * POST_TN]

    gate = jnp.dot(ybf_sc[...], wg_ref[...], preferred_element_type=F32)
    ple = jnp.dot(p_ref[...], wp_ref[...], preferred_element_type=F32)
    o = y_sc[j] + (1.0 / (1.0 + jnp.exp(-gate))) * ple
    o_ref[...] = o
    obf_ref[...] = o.astype(BF16)


def post_norm_ple(x, h, p_bf, g, b, wg_bf, wp_bf):
    m = x.shape[0]
    tm, tn = POST_TM, POST_TN
    return pl.pallas_call(
        _post_kernel,
        grid=(m // tm, D_MODEL // tn),
        in_specs=[pl.BlockSpec((tm, D_MODEL), lambda i, j: (i, 0)),
                  pl.BlockSpec((tm, D_MODEL), lambda i, j: (i, 0)),
                  pl.BlockSpec((tm, PLE_DIM), lambda i, j: (i, 0)),
                  pl.BlockSpec((1, D_MODEL), lambda i, j: (0, 0)),
                  pl.BlockSpec((1, D_MODEL), lambda i, j: (0, 0)),
                  pl.BlockSpec((D_MODEL, tn), lambda i, j: (0, j)),
                  pl.BlockSpec((PLE_DIM, tn), lambda i, j: (0, j))],
        out_specs=[pl.BlockSpec((tm, tn), lambda i, j: (i, j)),
                   pl.BlockSpec((tm, tn), lambda i, j: (i, j))],
        out_shape=[jax.ShapeDtypeStruct((m, D_MODEL), F32),
                   jax.ShapeDtypeStruct((m, D_MODEL), BF16)],
        scratch_shapes=[pltpu.VMEM((D_MODEL // tn, tm, tn), F32),
                        pltpu.VMEM((tm, D_MODEL), BF16)],
        compiler_params=pltpu.CompilerParams(
            dimension_semantics=("parallel", "arbitrary"), vmem_limit_bytes=VMEM_LIMIT),
        name="post_norm_ple",
    )(x, h, p_bf, g.reshape(1, D_MODEL), b.reshape(1, D_MODEL), wg_bf, wp_bf)


def rel_bucket(dist):
    n = jnp.maximum(dist, 0)
    exact = N_BUCKETS // 2
    logb = exact + (jnp.log(jnp.maximum(n, exact).astype(F32) / exact)
                    / math.log(REL_MAX_DIST / exact) * (N_BUCKETS - exact)).astype(jnp.int32)
    return jnp.where(n < exact, n, jnp.minimum(logb, N_BUCKETS - 1))


def split_tokens(t):
    c = t.shape[-1]
    return (t[:N_PROMPT_TOK].reshape(BATCH, SEQ, c), t[N_PROMPT_TOK:].reshape(DEC_BATCH, DEC_SEQ, c))


def join_tokens(tp, ts):
    c = tp.shape[-1]
    return jnp.concatenate([tp.reshape(N_PROMPT_TOK, c), ts.reshape(N_SAMPLE_TOK, c)], axis=0)


G_A = N_HEADS // KV_A
SWA_KEYS = 2 * WINDOW


def swa_tables(rel_bias, sinks, dec_len):
    def heads_to(b, lead):
        return jnp.moveaxis(b, -1, 0).reshape((KV_A, G_A) + lead)

    dist = jnp.arange(WINDOW)[None, :] - (jnp.arange(SWA_KEYS)[:, None] - WINDOW)
    ok = (dist >= 0) & (dist < WINDOW)
    b = jnp.where(ok[..., None], rel_bias[rel_bucket(dist)].astype(F32), NEG_BIG)
    bias_p = heads_to(b, (SWA_KEYS, WINDOW)).transpose(0, 2, 1, 3).reshape(KV_A, SWA_KEYS, G_A * WINDOW)
    sink_p = jnp.broadcast_to(sinks.astype(F32).reshape(KV_A, 1, G_A, 1), (KV_A, 1, G_A, WINDOW))
    sink_p = sink_p.reshape(KV_A, 1, G_A * WINDOW)
    key_i = jnp.arange(SWA_KEYS)[None, :]
    dist = jnp.arange(dec_len)[:, None] + WINDOW - key_i
    ok = (dist >= 0) & (dist < WINDOW) & (key_i < WINDOW + dec_len)
    b = jnp.where(ok[..., None], rel_bias[rel_bucket(dist)].astype(F32), NEG_BIG)
    bias_s = heads_to(b, (dec_len, SWA_KEYS)).reshape(KV_A, G_A * dec_len, SWA_KEYS)
    sink_s = jnp.broadcast_to(sinks.astype(F32).reshape(KV_A, G_A, 1, 1), (KV_A, G_A, dec_len, LANE))
    sink_s = sink_s.reshape(KV_A, G_A * dec_len, LANE)
    return bias_p, sink_p, bias_s, sink_s


def _swa_prompt_kernel(qT_ref, zT_ref, vTp_ref, vTc_ref, kp_ref, kc_ref, bias_ref, sink_ref, o_ref, *, cdt):
    first = pl.program_id(1) == 0
    kk = jnp.concatenate([kp_ref[...], kc_ref[...]], axis=0)
    vT = jnp.concatenate([vTp_ref[...], vTc_ref[...]], axis=1)
    prev_key = lax.broadcasted_iota(jnp.int32, (SWA_KEYS, G_A * WINDOW), 0) < WINDOW
    pw = 2 * HEAD_DIM
    for j in range(KV_A):
        kpair = kk[:, (j // 2) * pw:(j // 2 + 1) * pw].astype(cdt)
        qj = jnp.concatenate([qT_ref[(G_A * j + g) * HEAD_DIM:(G_A * j + g + 1) * HEAD_DIM, :]
                              for g in range(G_A)], axis=1)
        qj = (qj.astype(F32) * (HEAD_DIM ** -0.5)).astype(cdt)
        zpad = jnp.zeros_like(qj)
        rhs = jnp.concatenate([qj, zpad] if j % 2 == 0 else [zpad, qj], axis=0)
        s = jnp.dot(kpair, rhs, preferred_element_type=F32) + bias_ref[j]
        s = jnp.where(prev_key, jnp.where(first, NEG_BIG, s), s)
        sink = sink_ref[j]
        m = jnp.maximum(jnp.max(s, axis=0, keepdims=True), sink)
        e = jnp.exp(s - m)
        den = jnp.sum(e, axis=0, keepdims=True) + jnp.exp(sink - m)
        p = (e * (1.0 / den)).astype(cdt)
        acc = jnp.dot(vT[j * HEAD_DIM:(j + 1) * HEAD_DIM, :].astype(cdt), p, preferred_element_type=F32)
        for g in range(G_A):
            r0 = (G_A * j + g) * HEAD_DIM
            z = zT_ref[r0:r0 + HEAD_DIM, :]
            o_ref[r0:r0 + HEAD_DIM, :] = (acc[:, g * WINDOW:(g + 1) * WINDOW]
                                          * (z * (1.0 / (1.0 + jnp.exp(-z))))).astype(o_ref.dtype)


def swa_prompt(qvT, zT, kv_nat, bias_p, sink_p, *, n_batch, seq_len, cdt=BF16):
    nb = seq_len // WINDOW
    cur = lambda b, i: b * nb + i
    prev = lambda b, i: b * nb + jnp.maximum(i - 1, 0)
    v_row_blk = ATT_WIDTH // A_KV
    return pl.pallas_call(
        functools.partial(_swa_prompt_kernel, cdt=cdt),
        grid=(n_batch, nb),
        in_specs=[pl.BlockSpec((ATT_WIDTH, WINDOW), lambda b, i: (0, cur(b, i))),
                  pl.BlockSpec((ATT_WIDTH, WINDOW), lambda b, i: (0, cur(b, i))),
                  pl.BlockSpec((A_KV, WINDOW), lambda b, i: (v_row_blk, prev(b, i))),
                  pl.BlockSpec((A_KV, WINDOW), lambda b, i: (v_row_blk, cur(b, i))),
                  pl.BlockSpec((WINDOW, A_KV), lambda b, i: (prev(b, i), 0)),
                  pl.BlockSpec((WINDOW, A_KV), lambda b, i: (cur(b, i), 0)),
                  pl.BlockSpec(bias_p.shape, lambda b, i: (0, 0, 0)),
                  pl.BlockSpec(sink_p.shape, lambda b, i: (0, 0, 0))],
        out_specs=pl.BlockSpec((ATT_WIDTH, WINDOW), lambda b, i: (0, cur(b, i))),
        out_shape=jax.ShapeDtypeStruct((ATT_WIDTH, n_batch * seq_len), BF16),
        compiler_params=pltpu.CompilerParams(
            dimension_semantics=("parallel", "parallel"), vmem_limit_bytes=VMEM_LIMIT),
        name="swa_prompt",
    )(qvT, zT, qvT, qvT, kv_nat, kv_nat, bias_p, sink_p)


def _swa_sample_kernel(q_ref, z_ref, k_ref, v_ref, bias_ref, sink_ref, o_ref, *, cdt):
    for j in range(KV_A):
        s = lax.dot_general(q_ref[0, j], k_ref[0, j].astype(cdt), (((1,), (1,)), ((), ())),
                            preferred_element_type=F32) + bias_ref[j]
        sink = sink_ref[j][:, 0:1]
        m = jnp.maximum(jnp.max(s, axis=1, keepdims=True), sink)
        e = jnp.exp(s - m)
        den = jnp.sum(e, axis=1, keepdims=True) + jnp.exp(sink - m)
        p = (e * (1.0 / den)).astype(cdt)
        z = z_ref[0, j]
        o_ref[0, j] = jnp.dot(p, v_ref[0, j].astype(cdt), preferred_element_type=F32) * (z * (1.0 / (1.0 + jnp.exp(-z))))


def swa_sample(q_s, z_s, kv_s, kv_cache, bias_s, sink_s, cdt=BF16):
    n_dec = kv_cache.shape[0]
    dec_len = q_s.shape[0] // n_dec
    rows = G_A * dec_len

    def head_rows(t, scale):
        t = t.reshape(n_dec, dec_len, KV_A, G_A, HEAD_DIM).transpose(0, 2, 3, 1, 4) * scale
        return jnp.pad(t.reshape(n_dec, KV_A, rows, HEAD_DIM), ((0, 0), (0, 0), (0, 0), (0, HEAD_DIM)))

    new = kv_s.reshape(n_dec, dec_len, 2, KV_A, HEAD_DIM)
    cat = jnp.concatenate([kv_cache, new], axis=1)
    keys = jnp.pad(cat.transpose(2, 0, 3, 1, 4),
                   ((0, 0), (0, 0), (0, 0), (0, SWA_KEYS - WINDOW - dec_len), (0, HEAD_DIM)))
    blk = lambda r: pl.BlockSpec((1, KV_A, r, 2 * HEAD_DIM), lambda b: (b, 0, 0, 0))
    o = pl.pallas_call(
        functools.partial(_swa_sample_kernel, cdt=cdt),
        grid=(n_dec,),
        in_specs=[blk(rows), blk(rows), blk(SWA_KEYS), blk(SWA_KEYS),
                  pl.BlockSpec(bias_s.shape, lambda b: (0, 0, 0)),
                  pl.BlockSpec(sink_s.shape, lambda b: (0, 0, 0))],
        out_specs=blk(rows),
        out_shape=jax.ShapeDtypeStruct((n_dec, KV_A, rows, 2 * HEAD_DIM), F32),
        compiler_params=pltpu.CompilerParams(dimension_semantics=("parallel",), vmem_limit_bytes=VMEM_LIMIT),
        name="swa_sample",
    )(head_rows(q_s, HEAD_DIM ** -0.5).astype(cdt), head_rows(z_s, 1.0), keys[0], keys[1], bias_s, sink_s)
    o = o[..., :HEAD_DIM].reshape(n_dec, KV_A, G_A, dec_len, HEAD_DIM).transpose(0, 3, 1, 2, 4)
    return o.reshape(n_dec * dec_len, ATT_WIDTH), cat[:, dec_len:]


S5_SLAB_G = 8
S5_SLAB_CH = S5_SLAB_G * S5_GROUP
S5_SLAB_ST = S5_SLAB_G * S5_STATE
S5_N_SLABS = S5_GROUPS // S5_SLAB_G
S5_CHAINS = 8
S5_HALF_CH = S5_CHAINS * S5_SLAB_CH
S5_T = 256
S5_LT = 2 * S5_SLAB_ST // LANE


def _gelu_tanh(x):
    return 0.5 * x * (1.0 + jnp.tanh(math.sqrt(2.0 / math.pi) * (x + 0.044715 * (x * x * x))))


def s5_tables(a_re, a_im, b_re, b_im, c_re, c_im, log_dt):
    a = lax.complex(a_re, a_im)
    dt = jnp.exp(log_dt)[:, None]
    a_bar = jnp.exp(a * dt)
    b_bar = ((a_bar - 1.0) / a)[..., None] * lax.complex(b_re, b_im)
    eye = jnp.eye(S5_SLAB_G, dtype=F32)

    def b_blk(t):
        t = t.reshape(S5_N_SLABS, S5_SLAB_G, S5_STATE, S5_GROUP)
        return jnp.einsum('ij,sipc->sicjp', eye, t).reshape(S5_N_SLABS, S5_SLAB_CH, S5_SLAB_ST)

    def c_blk(t):
        t = t.reshape(S5_N_SLABS, S5_SLAB_G, S5_GROUP, S5_STATE)
        return jnp.einsum('ij,sicp->sjpic', eye, t).reshape(S5_N_SLABS, S5_SLAB_ST, S5_SLAB_CH)

    bcat = jnp.concatenate([b_blk(b_bar.real), b_blk(b_bar.imag)], axis=2)
    ccat = jnp.concatenate([c_blk(c_re), -c_blk(c_im)], axis=1)
    a_cat = jnp.concatenate([a_bar.real.reshape(S5_N_SLABS, S5_SLAB_ST),
                             a_bar.imag.reshape(S5_N_SLABS, S5_SLAB_ST)], axis=1)
    return a_cat, bcat, ccat


def _s5_prompt_kernel(u_ref, bcat_ref, ccat_ref, a_ref, d_ref, y_ref, hout_ref, sc, h_sc, *, cdt):
    tc = pl.program_id(2)
    n_lt_half = S5_LT // 2

    @pl.when(tc == 0)
    def _():
        h_sc[...] = jnp.zeros_like(h_sc)

    for j in range(S5_CHAINS):
        uj = u_ref[:, j * S5_SLAB_CH:(j + 1) * S5_SLAB_CH].astype(cdt)
        bu = jnp.dot(uj, bcat_ref[0, j], preferred_element_type=F32)
        for lt in range(S5_LT):
            sc[lt, pl.ds(j, S5_T, stride=S5_CHAINS), :] = bu[:, lt * LANE:(lt + 1) * LANE]

    a_re = [a_ref[0, :, lt * LANE:(lt + 1) * LANE] for lt in range(n_lt_half)]
    a_im = [a_ref[0, :, (n_lt_half + lt) * LANE:(n_lt_half + lt + 1) * LANE] for lt in range(n_lt_half)]

    def step(t, h):
        r0 = pl.multiple_of(t * S5_CHAINS, S5_CHAINS)
        new = list(h)
        for lt in range(n_lt_half):
            hr, hi = h[lt], h[n_lt_half + lt]
            nr = a_re[lt] * hr - a_im[lt] * hi + sc[lt, pl.ds(r0, S5_CHAINS), :]
            ni = a_re[lt] * hi + a_im[lt] * hr + sc[n_lt_half + lt, pl.ds(r0, S5_CHAINS), :]
            sc[lt, pl.ds(r0, S5_CHAINS), :] = nr
            sc[n_lt_half + lt, pl.ds(r0, S5_CHAINS), :] = ni
            new[lt], new[n_lt_half + lt] = nr, ni
        return tuple(new)

    h = lax.fori_loop(0, S5_T, step, tuple(h_sc[lt] for lt in range(S5_LT)), unroll=8)
    for lt in range(S5_LT):
        h_sc[lt] = h[lt]
        hout_ref[0, 0, :, lt * LANE:(lt + 1) * LANE] = h[lt]

    for j in range(S5_CHAINS):
        hcat = jnp.concatenate([sc[lt, pl.ds(j, S5_T, stride=S5_CHAINS), :] for lt in range(S5_LT)], axis=1)
        cols = slice(j * S5_SLAB_CH, (j + 1) * S5_SLAB_CH)
        y = jnp.dot(hcat.astype(cdt), ccat_ref[0, j], preferred_element_type=F32) + d_ref[0, :, cols] * u_ref[:, cols]
        y_ref[:, cols] = _gelu_tanh(y)


def s5_prompt(proj, a_cat, bcat, ccat, d_skip, *, n_batch, seq_len, n_rows_out, cdt=BF16):
    n_t = seq_len // S5_T
    n_half = S5_WIDTH // S5_HALF_CH
    half = lambda t: t.reshape((n_half, S5_CHAINS) + t.shape[1:])
    return pl.pallas_call(
        functools.partial(_s5_prompt_kernel, cdt=cdt),
        grid=(n_batch, n_half, n_t),
        in_specs=[pl.BlockSpec((S5_T, S5_HALF_CH), lambda b, hf, t: (b * n_t + t, hf)),
                  pl.BlockSpec((1, S5_CHAINS, S5_SLAB_CH, 2 * S5_SLAB_ST), lambda b, hf, t: (hf, 0, 0, 0)),
                  pl.BlockSpec((1, S5_CHAINS, 2 * S5_SLAB_ST, S5_SLAB_CH), lambda b, hf, t: (hf, 0, 0, 0)),
                  pl.BlockSpec((1, S5_CHAINS, 2 * S5_SLAB_ST), lambda b, hf, t: (hf, 0, 0)),
                  pl.BlockSpec((1, 1, S5_HALF_CH), lambda b, hf, t: (hf, 0, 0))],
        out_specs=[pl.BlockSpec((S5_T, S5_HALF_CH), lambda b, hf, t: (b * n_t + t, hf)),
                   pl.BlockSpec((1, 1, S5_CHAINS, 2 * S5_SLAB_ST), lambda b, hf, t: (b, hf, 0, 0))],
        out_shape=[jax.ShapeDtypeStruct((n_rows_out, S5_WIDTH), F32),
                   jax.ShapeDtypeStruct((n_batch, n_half, S5_CHAINS, 2 * S5_SLAB_ST), F32)],
        scratch_shapes=[pltpu.VMEM((S5_LT, S5_T * S5_CHAINS, LANE), F32),
                        pltpu.VMEM((S5_LT, S5_CHAINS, LANE), F32)],
        compiler_params=pltpu.CompilerParams(
            dimension_semantics=("parallel", "parallel", "arbitrary"), vmem_limit_bytes=VMEM_LIMIT),
        name="s5_prompt",
    )(proj, half(bcat.astype(cdt)), half(ccat.astype(cdt)), half(a_cat), d_skip.reshape(n_half, 1, S5_HALF_CH))


def _s5_sample_kernel(u_ref, bcat_ref, ccat_ref, a_ref, d_ref, h0_ref, y_ref, hout_ref, sc, *, n_b, n_t, cdt):
    u = u_ref[...]
    bu = jnp.dot(u.astype(cdt), bcat_ref[0], preferred_element_type=F32)
    a_re = a_ref[0, :, :S5_SLAB_ST]
    a_im = a_ref[0, :, S5_SLAB_ST:]
    for bg in range(n_b // 8):
        hr = h0_ref[0, bg * 8:(bg + 1) * 8, :S5_SLAB_ST]
        hi = h0_ref[0, bg * 8:(bg + 1) * 8, S5_SLAB_ST:]
        for t in range(n_t):
            r = t * n_b + bg * 8
            hr, hi = (a_re * hr - a_im * hi + bu[r:r + 8, :S5_SLAB_ST],
                      a_re * hi + a_im * hr + bu[r:r + 8, S5_SLAB_ST:])
            sc[r:r + 8, :S5_SLAB_ST] = hr
            sc[r:r + 8, S5_SLAB_ST:] = hi
        hout_ref[0, bg * 8:(bg + 1) * 8, :S5_SLAB_ST] = hr
        hout_ref[0, bg * 8:(bg + 1) * 8, S5_SLAB_ST:] = hi
    y = jnp.dot(sc[...].astype(cdt), ccat_ref[0], preferred_element_type=F32) + d_ref[0] * u
    y_ref[...] = _gelu_tanh(y)


def s5_sample(u_tb, a_cat, bcat, ccat, d_skip, h0_cat, *, n_b, n_t, cdt=BF16):
    rows = n_t * n_b
    return pl.pallas_call(
        functools.partial(_s5_sample_kernel, n_b=n_b, n_t=n_t, cdt=cdt),
        grid=(S5_N_SLABS,),
        in_specs=[pl.BlockSpec((rows, S5_SLAB_CH), lambda s: (0, s)),
                  pl.BlockSpec((1, S5_SLAB_CH, 2 * S5_SLAB_ST), lambda s: (s, 0, 0)),
                  pl.BlockSpec((1, 2 * S5_SLAB_ST, S5_SLAB_CH), lambda s: (s, 0, 0)),
                  pl.BlockSpec((1, 1, 2 * S5_SLAB_ST), lambda s: (s, 0, 0)),
                  pl.BlockSpec((1, 1, S5_SLAB_CH), lambda s: (s, 0, 0)),
                  pl.BlockSpec((1, n_b, 2 * S5_SLAB_ST), lambda s: (s, 0, 0))],
        out_specs=[pl.BlockSpec((rows, S5_SLAB_CH), lambda s: (0, s)),
                   pl.BlockSpec((1, n_b, 2 * S5_SLAB_ST), lambda s: (s, 0, 0))],
        out_shape=[jax.ShapeDtypeStruct((rows, S5_WIDTH), F32),
                   jax.ShapeDtypeStruct((S5_N_SLABS, n_b, 2 * S5_SLAB_ST), F32)],
        scratch_shapes=[pltpu.VMEM((rows, 2 * S5_SLAB_ST), F32)],
        compiler_params=pltpu.CompilerParams(
            dimension_semantics=("arbitrary",), vmem_limit_bytes=VMEM_LIMIT),
        name="s5_sample",
    )(u_tb, bcat.astype(cdt), ccat.astype(cdt), a_cat.reshape(S5_N_SLABS, 1, 2 * S5_SLAB_ST),
      d_skip.reshape(S5_N_SLABS, 1, S5_SLAB_CH), h0_cat)


GLU_TM = 320
GLU_TN = 512


def _glu_kernel(yfull_ref, w_ref, ycol_ref, z_ref, o_ref, ybf_sc):
    @pl.when(pl.program_id(1) == 0)
    def _():
        ybf_sc[...] = yfull_ref[...].astype(ybf_sc.dtype)

    glu = jnp.dot(ybf_sc[...], w_ref[...], preferred_element_type=F32)
    z = z_ref[...]
    y = ycol_ref[...]
    o_ref[...] = (y * (1.0 / (1.0 + jnp.exp(-glu))) * (z * (1.0 / (1.0 + jnp.exp(-z))))).astype(o_ref.dtype)


def s5_glu_gate(y, w_glu, proj, row_off, cdt=BF16):
    m = y.shape[0]
    tm, tn = _pick_tile(m, (256, 128, 64, 32, 16)), GLU_TN
    assert row_off % tm == 0
    z_off, r_off = S5_WIDTH // tn, row_off // tm
    return pl.pallas_call(
        _glu_kernel,
        grid=(m // tm, S5_WIDTH // tn),
        in_specs=[pl.BlockSpec((tm, S5_WIDTH), lambda i, j: (i, 0)),
                  pl.BlockSpec((S5_WIDTH, tn), lambda i, j: (0, j)),
                  pl.BlockSpec((tm, tn), lambda i, j: (i, j)),
                  pl.BlockSpec((tm, tn), lambda i, j: (r_off + i, z_off + j))],
        out_specs=pl.BlockSpec((tm, tn), lambda i, j: (i, j)),
        out_shape=jax.ShapeDtypeStruct((m, S5_WIDTH), BF16),
        scratch_shapes=[pltpu.VMEM((tm, S5_WIDTH), cdt)],
        compiler_params=pltpu.CompilerParams(
            dimension_semantics=("parallel", "arbitrary"), vmem_limit_bytes=VMEM_LIMIT),
        name="s5_glu_gate",
    )(y, w_glu.astype(cdt), y, proj)


def s5_layer(proj, state_in, tables, d_skip, w_glu, *, n_batch, seq_len, n_dec, dec_len, cdt=BF16):
    a_cat, bcat, ccat = tables
    n_p = n_batch * seq_len
    n_s = n_dec * dec_len
    y_p, h_p = s5_prompt(proj, a_cat, bcat, ccat, d_skip, n_batch=n_batch, seq_len=seq_len, n_rows_out=n_p, cdt=cdt)
    u_tb = jnp.swapaxes(proj[n_p:, :S5_WIDTH].reshape(n_dec, dec_len, S5_WIDTH), 0, 1).reshape(n_s, S5_WIDTH)
    h0 = state_in.reshape(n_dec, S5_N_SLABS, S5_SLAB_ST, 2)
    h0_cat = jnp.concatenate([jnp.swapaxes(h0[..., 0], 0, 1), jnp.swapaxes(h0[..., 1], 0, 1)], axis=-1)
    y_tb, h_s = s5_sample(u_tb, a_cat, bcat, ccat, d_skip, h0_cat, n_b=n_dec, n_t=dec_len, cdt=cdt)
    y_s = jnp.swapaxes(y_tb.reshape(dec_len, n_dec, S5_WIDTH), 0, 1).reshape(n_s, S5_WIDTH)
    gated_p = s5_glu_gate(y_p, w_glu, proj, 0, cdt=cdt)
    gated_s = s5_glu_gate(y_s, w_glu, proj, n_p, cdt=cdt)
    hp = h_p.reshape(n_batch, S5_N_SLABS, 2, S5_SLAB_ST)
    st_p = jnp.stack([hp[:, :, 0], hp[:, :, 1]], axis=-1).reshape(n_batch, S5_GROUPS, S5_STATE, 2)
    hs = jnp.swapaxes(h_s, 0, 1).reshape(n_dec, S5_N_SLABS, 2, S5_SLAB_ST)
    st_s = jnp.stack([hs[:, :, 0], hs[:, :, 1]], axis=-1).reshape(n_dec, S5_GROUPS, S5_STATE, 2)
    return gated_p, gated_s, st_p, st_s


GDN_CONV_TT = 256
GDN_CONV_CW = 1024
GDN_HIST = 8
GDN_HB = 4
GDN_TT = 256
GDN_SAMPLE_ROWS = 8


def _gdn_conv_kernel(x_ref, hist_ref, w_ref, o_ref, *, rows, n_t, zero_first):
    i, j = pl.program_id(0), pl.program_id(1)
    hist = hist_ref[...]
    if zero_first:
        hist = jnp.where(i % n_t == 0, 0.0, hist)
    ext = jnp.concatenate([hist, x_ref[...]], axis=0)
    acc = ext[GDN_HIST:GDN_HIST + rows] * w_ref[GDN_CONV - 1:GDN_CONV, :]
    for s in range(1, GDN_CONV):
        acc = acc + ext[GDN_HIST - s:GDN_HIST - s + rows] * w_ref[GDN_CONV - 1 - s:GDN_CONV - s, :]
    conv = acc * (1.0 / (1.0 + jnp.exp(-acc)))
    n_qk_blocks = 2 * GDN_QK_WIDTH // GDN_CONV_CW

    @pl.when(j >= n_qk_blocks)
    def _():
        o_ref[...] = conv

    @pl.when(j < n_qk_blocks)
    def _():
        scale = jnp.where(j < GDN_QK_WIDTH // GDN_CONV_CW, GDN_DK ** -0.5, 1.0)
        for h in range(GDN_CONV_CW // GDN_DK):
            t = conv[:, h * GDN_DK:(h + 1) * GDN_DK]
            n = t * lax.rsqrt(jnp.sum(t * t, axis=-1, keepdims=True) + 1e-6)
            o_ref[:, h * GDN_DK:(h + 1) * GDN_DK] = n * scale


def gdn_conv(x, hist_src, conv_w, *, rows, n_blocks, n_t, data_map, hist_map, zero_first):
    n_out = n_blocks * rows
    return pl.pallas_call(
        functools.partial(_gdn_conv_kernel, rows=rows, n_t=n_t, zero_first=zero_first),
        grid=(n_blocks, GDN_CONV_CH // GDN_CONV_CW),
        in_specs=[pl.BlockSpec((rows, GDN_CONV_CW), lambda i, j: (data_map(i), j)),
                  pl.BlockSpec((GDN_HIST, GDN_CONV_CW), lambda i, j: (hist_map(i), j)),
                  pl.BlockSpec((GDN_CONV, GDN_CONV_CW), lambda i, j: (0, j))],
        out_specs=pl.BlockSpec((rows, GDN_CONV_CW), lambda i, j: (i, j)),
        out_shape=jax.ShapeDtypeStruct((n_out, GDN_CONV_CH), F32),
        compiler_params=pltpu.CompilerParams(
            dimension_semantics=("parallel", "parallel"), vmem_limit_bytes=VMEM_LIMIT),
        name="gdn_conv",
    )(x, hist_src, conv_w)


def _gdn_chunk_lockstep_kernel(q_ref, k_ref, v_ref, z_ref, ab_ref, alog_ref, dtb_ref, nw_ref, s0_ref, o_ref,
                               sout_ref, s_sc, *, chunk, n_inner, n_tt, valid_len):
    C = chunk
    hb, tt = pl.program_id(1), pl.program_id(2)

    @pl.when(tt == 0)
    def _():
        s_sc[...] = s0_ref[0]

    rowi = lax.broadcasted_iota(jnp.int32, (C, C), 0)
    coli = lax.broadcasted_iota(jnp.int32, (C, C), 1)
    causal = rowi >= coli
    strict = rowi > coli
    ltri = jnp.where(causal, 1.0, 0.0)
    utri = jnp.where(rowi <= coli, 1.0, 0.0)
    eye = jnp.where(rowi == coli, 1.0, 0.0)
    hi = lax.Precision.HIGHEST
    shift = (LANE - hb * GDN_HB) % LANE
    alog = pltpu.roll(jnp.broadcast_to(alog_ref[...], (8, LANE)), shift, 1)[0:1]
    dtb = pltpu.roll(jnp.broadcast_to(dtb_ref[...], (8, LANE)), shift, 1)[0:1]
    nw = nw_ref[...]
    tok_valid = lax.broadcasted_iota(jnp.int32, (C, LANE), 0) < valid_len
    dot = functools.partial(jnp.dot, preferred_element_type=F32)
    dot_nt = lambda a, b: lax.dot_general(a, b, (((1,), (1,)), ((), ())), preferred_element_type=F32)
    dot_tn = lambda a, b: lax.dot_general(a, b, (((0,), (0,)), ((), ())), preferred_element_type=F32)
    units = [(c, i) for c in range(n_inner) for i in range(GDN_HB)]
    rows = lambda c: slice(c * C, (c + 1) * C)
    qk_cols = lambda i: slice((i // 2) * GDN_DK, (i // 2 + 1) * GDN_DK)
    v_cols = lambda i: slice(i * GDN_DV, (i + 1) * GDN_DV)

    g_all, beta_all = [], []
    for c in range(n_inner):
        ab = pltpu.roll(ab_ref[rows(c), :], shift, 1)
        xa = ab + dtb
        softplus = jnp.maximum(xa, 0.0) + jnp.log1p(jnp.exp(-jnp.abs(xa)))
        g_all.append(jnp.where(tok_valid, -jnp.exp(alog) * softplus, 0.0))
        beta_all.append(jnp.where(tok_valid, 1.0 / (1.0 + jnp.exp(-ab)), 0.0))
    gam_all = [jnp.dot(ltri, g, preferred_element_type=F32, precision=hi) for g in g_all]
    gamT_all = [lax.dot_general(g, utri, (((0,), (0,)), ((), ())), preferred_element_type=F32, precision=hi)
                for g in g_all]

    qkk = [dot_nt(jnp.concatenate([q_ref[rows(c), qk_cols(i)], k_ref[rows(c), qk_cols(i)]], axis=0).astype(BF16),
                  k_ref[rows(c), qk_cols(i)].astype(BF16)) for c, i in units]
    gam_c = [jnp.broadcast_to(gam_all[c][:, i:i + 1], (C, LANE)) for c, i in units]
    beta_c = [jnp.broadcast_to(beta_all[c][:, 32 + i:33 + i], (C, LANE)) for c, i in units]
    gam_last = [jnp.broadcast_to(gam_all[c][C - 1:C, i:i + 1], (1, LANE)) for c, i in units]
    decay = [jnp.where(causal, jnp.exp(jnp.where(causal, gc[:, :C] - jnp.broadcast_to(gamT_all[c][i:i + 1, :], (C, C)),
                                                 0.0)), 0.0) for gc, (c, i) in zip(gam_c, units)]
    qk = [(x[:C] * d).astype(BF16) for x, d in zip(qkk, decay)]
    neg_a = [jnp.where(strict, -(b[:, :C] * x[C:] * d), 0.0) for b, x, d in zip(beta_c, qkk, decay)]
    p_inv = [eye + n for n in neg_a]
    m_pow = neg_a
    for _ in range(int(math.log2(C)) - 1):
        m_pow = [dot(m.astype(BF16), m.astype(BF16)) for m in m_pow]
        p_inv = [p + dot(p.astype(BF16), m.astype(BF16)) for p, m in zip(p_inv, m_pow)]
    eg = [jnp.exp(gc) for gc in gam_c]
    sol = [dot(p.astype(BF16), jnp.concatenate([b * v_ref[rows(c), v_cols(i)], (b * e) * k_ref[rows(c), qk_cols(i)]],
                                               axis=1).astype(BF16))
           for p, b, e, (c, i) in zip(p_inv, beta_c, eg, units)]
    wq = [jnp.concatenate([s[:, GDN_DV:], q_ref[rows(c), qk_cols(i)] * e], axis=0).astype(BF16)
          for s, e, (c, i) in zip(sol, eg, units)]
    k_dec = [(k_ref[rows(c), qk_cols(i)] * jnp.exp(gl - gc)).astype(BF16)
             for gl, gc, (c, i) in zip(gam_last, gam_c, units)]

    state = [s_sc[i] for i in range(GDN_HB)]
    for c in range(n_inner):
        base = c * GDN_HB
        ws = [dot(wq[base + i], state[i].astype(BF16)) for i in range(GDN_HB)]
        v_new = [(sol[base + i][:, :GDN_DV] - ws[i][:C]).astype(BF16) for i in range(GDN_HB)]
        o = [ws[i][C:] + dot(qk[base + i], v_new[i]) for i in range(GDN_HB)]
        state = [state[i] * jnp.exp(gam_last[base + i]) + dot_tn(k_dec[base + i], v_new[i]) for i in range(GDN_HB)]
        for i in range(GDN_HB):
            rms = lax.rsqrt(jnp.mean(o[i] * o[i], axis=-1, keepdims=True) + 1e-6)
            zz = z_ref[rows(c), v_cols(i)]
            o_ref[rows(c), v_cols(i)] = (o[i] * rms * nw * (zz * (1.0 / (1.0 + jnp.exp(-zz))))).astype(o_ref.dtype)
    for i in range(GDN_HB):
        s_sc[i] = state[i]

    @pl.when(tt == n_tt - 1)
    def _():
        sout_ref[0] = s_sc[...]


def gdn_chunk(conv, z, ab, a_log, dt_bias, norm_w, s0, *, n_seq, rows_per_seq, rows_per_step, chunk, valid_len,
              z_col_off, out_dtype):
    n_tt = rows_per_seq // rows_per_step
    n_inner = rows_per_step // chunk
    n_hb = GDN_V_HEADS // GDN_HB
    qw, vw = GDN_HB // 2 * GDN_DK, GDN_HB * GDN_DV
    k_off, v_off, z_off = GDN_QK_WIDTH // qw, 2 * GDN_QK_WIDTH // vw, z_col_off // vw
    row = lambda b, hb, t: b * n_tt + t
    pad_row = lambda p: jnp.pad(p.astype(F32), (0, LANE - p.shape[0])).reshape(1, LANE)
    return pl.pallas_call(
        functools.partial(_gdn_chunk_lockstep_kernel, chunk=chunk, n_inner=n_inner, n_tt=n_tt, valid_len=valid_len),
        grid=(n_seq, n_hb, n_tt),
        in_specs=[pl.BlockSpec((rows_per_step, qw), lambda b, hb, t: (row(b, hb, t), hb)),
                  pl.BlockSpec((rows_per_step, qw), lambda b, hb, t: (row(b, hb, t), k_off + hb)),
                  pl.BlockSpec((rows_per_step, vw), lambda b, hb, t: (row(b, hb, t), v_off + hb)),
                  pl.BlockSpec((rows_per_step, vw), lambda b, hb, t: (row(b, hb, t), z_off + hb)),
                  pl.BlockSpec((rows_per_step, LANE), lambda b, hb, t: (row(b, hb, t), 0)),
                  pl.BlockSpec((1, LANE), lambda b, hb, t: (0, 0)),
                  pl.BlockSpec((1, LANE), lambda b, hb, t: (0, 0)),
                  pl.BlockSpec((1, GDN_DV), lambda b, hb, t: (0, 0)),
                  pl.BlockSpec((1, GDN_HB, GDN_DK, GDN_DV), lambda b, hb, t: (b, hb, 0, 0))],
        out_specs=[pl.BlockSpec((rows_per_step, vw), lambda b, hb, t: (row(b, hb, t), hb)),
                   pl.BlockSpec((1, GDN_HB, GDN_DK, GDN_DV), lambda b, hb, t: (b, hb, 0, 0))],
        out_shape=[jax.ShapeDtypeStruct((n_seq * rows_per_seq, GDN_V_WIDTH), out_dtype),
                   jax.ShapeDtypeStruct((n_seq, GDN_V_HEADS, GDN_DK, GDN_DV), F32)],
        scratch_shapes=[pltpu.VMEM((GDN_HB, GDN_DK, GDN_DV), F32)],
        compiler_params=pltpu.CompilerParams(
            dimension_semantics=("parallel", "parallel", "arbitrary"), vmem_limit_bytes=VMEM_LIMIT),
        name="gdn_chunk",
    )(conv, conv, conv, z, ab, pad_row(a_log), pad_row(dt_bias), norm_w.astype(F32).reshape(1, GDN_DV), s0)


def gdn_layer(qkvz, ab, state_in, conv_in, conv_w, a_log, dt_bias, norm_w, *, n_batch, seq_len, n_dec, dec_len):
    n_p = n_batch * seq_len
    n_tp = seq_len // GDN_CONV_TT
    hist_per_block = GDN_CONV_TT // GDN_HIST
    conv_p = gdn_conv(qkvz, qkvz, conv_w, rows=GDN_CONV_TT, n_blocks=n_batch * n_tp, n_t=n_tp,
                      data_map=lambda i: i, hist_map=lambda i: jnp.maximum(i * hist_per_block - 1, 0),
                      zero_first=True)
    zeros_s = jnp.zeros((n_batch, GDN_V_HEADS, GDN_DK, GDN_DV), F32)
    gated_p, st_p = gdn_chunk(conv_p, qkvz, ab, a_log, dt_bias, norm_w, zeros_s, n_seq=n_batch,
                              rows_per_seq=seq_len, rows_per_step=GDN_TT, chunk=GDN_CHUNK, valid_len=GDN_CHUNK,
                              z_col_off=GDN_CONV_CH, out_dtype=BF16)
    buf_p = qkvz[:n_p, :GDN_CONV_CH].reshape(n_batch, seq_len, GDN_CONV_CH)[:, seq_len - (GDN_CONV - 1):]
    R = GDN_SAMPLE_ROWS
    x_s = qkvz[n_p:].reshape(n_dec, dec_len, -1)
    pad_t = lambda t, front: jnp.pad(t, ((0, 0), (front, R - front - t.shape[1]), (0, 0)))
    ext = jnp.concatenate([pad_t(conv_in, R - (GDN_CONV - 1)), pad_t(x_s[..., :GDN_CONV_CH], 0)], axis=1)
    ext = ext.reshape(n_dec * 2 * R, GDN_CONV_CH)
    conv_s = gdn_conv(ext, ext, conv_w, rows=R, n_blocks=n_dec, n_t=1,
                      data_map=lambda i: 2 * i + 1, hist_map=lambda i: 2 * i, zero_first=False)
    z_s = pad_t(x_s[..., GDN_CONV_CH:], 0).reshape(n_dec * R, GDN_V_WIDTH)
    ab_s = pad_t(ab[n_p:].reshape(n_dec, dec_len, LANE), 0).reshape(n_dec * R, LANE)
    gated_s, st_s = gdn_chunk(conv_s, z_s, ab_s, a_log, dt_bias, norm_w, state_in, n_seq=n_dec, rows_per_seq=R,
                              rows_per_step=R, chunk=R, valid_len=dec_len, z_col_off=0, out_dtype=F32)
    gated_s = gated_s.reshape(n_dec, R, GDN_V_WIDTH)[:, :dec_len].reshape(n_dec * dec_len, GDN_V_WIDTH)
    buf_s = jnp.concatenate([conv_in, x_s[..., :GDN_CONV_CH]], axis=1)[:, dec_len:]
    return gated_p, gated_s, st_p, st_s, buf_p, buf_s


DSA_KC = 256
INT_MIN = -2 ** 31
NEG_BIG = -1e30
G_D = N_HEADS // KV_D
BIAS_WIN = DSA_KC + Q_BLOCK


def _sortable_key(s):
    b = pltpu.bitcast(s, jnp.int32)
    return jnp.where(b < 0, b ^ jnp.int32(0x7FFFFFFF), b)


def _dsa_prompt_kernel(qT_ref, qiT_ref, wiT_ref, zT_ref, ki_ref, k_ref, vT_ref, win_ref, o_ref,
                       key_sc, mask_sc, *, topk, idx_bits, cdt):
    qb = pl.program_id(1)
    t0 = qb * Q_BLOCK
    nch = (qb + 2) // 2
    t_idx = t0 + lax.broadcasted_iota(jnp.int32, (1, Q_BLOCK), 1)
    row_iota = lax.broadcasted_iota(jnp.int32, (DSA_KC, Q_BLOCK), 0)

    def score_chunk(c, carry):
        kic = ki_ref[0, c].astype(cdt)
        acc = jnp.zeros((DSA_KC, Q_BLOCK), F32)
        for hp in range(IDX_HEADS // 2):
            rhs = jnp.concatenate([qiT_ref[(2 * hp) * IDX_DIM:(2 * hp + 1) * IDX_DIM, :],
                                   qiT_ref[(2 * hp + 1) * IDX_DIM:(2 * hp + 2) * IDX_DIM, :]], axis=1)
            s = jnp.dot(kic, rhs, preferred_element_type=F32) * (IDX_DIM ** -0.5)
            s = jnp.maximum(s, 0.0)
            w0 = wiT_ref[2 * hp:2 * hp + 1, :] * (IDX_HEADS ** -0.5)
            w1 = wiT_ref[2 * hp + 1:2 * hp + 2, :] * (IDX_HEADS ** -0.5)
            acc = acc + s[:, :Q_BLOCK] * w0 + s[:, Q_BLOCK:] * w1
        s_idx = c * DSA_KC + row_iota
        key_sc[c] = jnp.where(s_idx <= t_idx, _sortable_key(acc), INT_MIN)
        return carry

    lax.fori_loop(0, nch, score_chunk, 0)

    def count(pred):
        def body(c, acc):
            hit = pred(key_sc[c], c * DSA_KC + row_iota)
            return acc + hit.reshape(DSA_KC // 8, 8, Q_BLOCK).sum(axis=0)
        acc = lax.fori_loop(0, nch, body, jnp.zeros((8, Q_BLOCK), jnp.int32))
        return jnp.sum(acc, axis=0, keepdims=True)

    c_nonneg = count(lambda k, s: jnp.where(k >= 0, 1, 0))
    thr = jnp.where(c_nonneg >= topk, 0, INT_MIN).astype(jnp.int32)

    def thr_bit(i, thr):
        cand = thr + jnp.left_shift(jnp.int32(1), 30 - i)
        return jnp.where(count(lambda k, s: jnp.where(k >= cand, 1, 0)) >= topk, cand, thr)

    thr = lax.fori_loop(0, 31, thr_bit, thr)
    need = topk - count(lambda k, s: jnp.where(k > thr, 1, 0))

    def lim_bit(i, lim):
        cand = lim + jnp.left_shift(jnp.int32(1), idx_bits - 1 - i)
        c = count(lambda k, s: jnp.where(k == thr, jnp.where(s < cand, 1, 0), 0))
        return jnp.where(c <= need, cand, lim)

    lim = lax.fori_loop(0, idx_bits, lim_bit, jnp.zeros((1, Q_BLOCK), jnp.int32))

    def mask_chunk(c, carry):
        k = key_sc[c]
        s_idx = c * DSA_KC + row_iota
        tie = jnp.where(k == thr, jnp.where(s_idx < lim, 0.0, NEG_BIG), NEG_BIG)
        m = jnp.where(k > thr, 0.0, tie)
        mask_sc[c] = jnp.where(k == INT_MIN, NEG_BIG, m)
        return carry

    lax.fori_loop(0, nch, mask_chunk, 0)

    n_cols = G_D * Q_BLOCK
    half = DSA_KC // 2

    def head_q(j):
        qj = jnp.concatenate([qT_ref[(G_D * j + g) * HEAD_DIM:(G_D * j + g + 1) * HEAD_DIM, :]
                              for g in range(G_D)], axis=1)
        return (qj.astype(F32) * (HEAD_DIM ** -0.5)).astype(cdt)

    for jp in range(KV_D // 2):
        q0, q1 = head_q(2 * jp), head_q(2 * jp + 1)
        zq = jnp.zeros_like(q0)
        rhs = jnp.concatenate([jnp.concatenate([q0, zq], axis=0), jnp.concatenate([zq, q1], axis=0)], axis=1)

        def chunk_body(c, carry, jp=jp, rhs=rhs):
            m, l, acc0, acc1 = carry
            kc = k_ref[0, c, :, jp * 2 * HEAD_DIM:(jp + 1) * 2 * HEAD_DIM].astype(cdt)
            s = jnp.dot(kc, rhs, preferred_element_type=F32)
            wt = win_ref[qb - 2 * c]
            madd = mask_sc[c]
            parts = []
            for hh in range(2 * G_D):
                h = 2 * G_D * jp + hh
                tiles = []
                for u in range(2):
                    lo = (1 - u) * half
                    r = jnp.broadcast_to(wt[h:h + 1, lo:lo + 2 * half], (half, 2 * half))
                    tiles.append(pltpu.roll(r, 0, 1, stride=1, stride_axis=0)[:, half:])
                parts.append(s[:, hh * Q_BLOCK:(hh + 1) * Q_BLOCK] + (jnp.concatenate(tiles, axis=0) + madd))
            s = jnp.concatenate(parts, axis=1)
            m_new = jnp.maximum(m, jnp.max(s, axis=0, keepdims=True))
            alpha = jnp.exp(m - m_new)
            p = jnp.exp(s - m_new)
            l = l * alpha + jnp.sum(p, axis=0, keepdims=True)
            p = p.astype(cdt)
            v0 = vT_ref[0, c, (2 * jp) * HEAD_DIM:(2 * jp + 1) * HEAD_DIM, :]
            v1 = vT_ref[0, c, (2 * jp + 1) * HEAD_DIM:(2 * jp + 2) * HEAD_DIM, :]
            acc0 = acc0 * alpha[:, :n_cols] + jnp.dot(v0, p[:, :n_cols], preferred_element_type=F32)
            acc1 = acc1 * alpha[:, n_cols:] + jnp.dot(v1, p[:, n_cols:], preferred_element_type=F32)
            return m_new, l, acc0, acc1

        init = (jnp.full((1, 2 * n_cols), NEG_BIG, F32), jnp.zeros((1, 2 * n_cols), F32),
                jnp.zeros((HEAD_DIM, n_cols), F32), jnp.zeros((HEAD_DIM, n_cols), F32))
        m, l, acc0, acc1 = lax.fori_loop(0, nch, chunk_body, init)
        inv = 1.0 / l
        for jj, acc in enumerate((acc0, acc1)):
            o = acc * inv[:, jj * n_cols:(jj + 1) * n_cols]
            for g in range(G_D):
                r0 = (G_D * (2 * jp + jj) + g) * HEAD_DIM
                z = zT_ref[r0:r0 + HEAD_DIM, :]
                gate = z * (1.0 / (1.0 + jnp.exp(-z)))
                o_ref[r0:r0 + HEAD_DIM, :] = (o[:, g * Q_BLOCK:(g + 1) * Q_BLOCK] * gate).astype(o_ref.dtype)


def dsa_bias_windows(rel_bias, seq_len):
    o = jnp.arange(seq_len // Q_BLOCK)[:, None]
    m = jnp.arange(BIAS_WIN)[None, :]
    d = jnp.maximum(o * Q_BLOCK + m - DSA_KC, 0)
    return jnp.moveaxis(rel_bias[rel_bucket(d)].astype(F32), -1, 1)


def dsa_prompt_attend(qqiT, wiT, zT, ki4, k4, v4T, win, *, n_batch, seq_len, cdt=BF16):
    nqb = seq_len // Q_BLOCK
    nc = seq_len // DSA_KC
    topk = min(TOPK_MAX, seq_len // 4)
    idx_bits = int(math.log2(seq_len)) + 1
    tok = lambda b, q: (0, b * nqb + q)
    per_batch = lambda b, q: (b, 0, 0, 0)
    return pl.pallas_call(
        functools.partial(_dsa_prompt_kernel, topk=topk, idx_bits=idx_bits, cdt=cdt),
        grid=(n_batch, nqb),
        in_specs=[pl.BlockSpec((ATT_WIDTH, Q_BLOCK), tok),
                  pl.BlockSpec((IDX_HEADS * IDX_DIM, Q_BLOCK), lambda b, q: (1, b * nqb + q)),
                  pl.BlockSpec((IDX_HEADS, Q_BLOCK), tok),
                  pl.BlockSpec((ATT_WIDTH, Q_BLOCK), tok),
                  pl.BlockSpec((1, nc, DSA_KC, IDX_DIM), per_batch),
                  pl.BlockSpec((1, nc, DSA_KC, D_KV), per_batch),
                  pl.BlockSpec((1, nc, D_KV, DSA_KC), per_batch),
                  pl.BlockSpec((nqb, N_HEADS, BIAS_WIN), lambda b, q: (0, 0, 0))],
        out_specs=pl.BlockSpec((ATT_WIDTH, Q_BLOCK), tok),
        out_shape=jax.ShapeDtypeStruct((ATT_WIDTH, n_batch * seq_len), BF16),
        scratch_shapes=[pltpu.VMEM((nc, DSA_KC, Q_BLOCK), jnp.int32),
                        pltpu.VMEM((nc, DSA_KC, Q_BLOCK), F32)],
        compiler_params=pltpu.CompilerParams(
            dimension_semantics=("parallel", "arbitrary"), vmem_limit_bytes=VMEM_LIMIT),
        name="dsa_prompt_attend",
    )(qqiT, qqiT, wiT, zT, ki4, k4, v4T, win)


DSS_NP1 = 16
DSS_NP2 = 8
DSS_TP = 8
DSS_GROWS = G_D * DSS_TP
T5_LAST_BUCKET_DIST = 1600


def _dsa_sample_select_kernel(pt_ref, qi_ref, wb_ref, kinew_ref, *rest, n_pages, n_new, topk, idx_bits, cdt):
    del pt_ref
    page_refs, mask_ref, key_sc = rest[:DSS_NP1], rest[DSS_NP1], rest[DSS_NP1 + 1]
    s = pl.program_id(1)
    lane = lax.broadcasted_iota(jnp.int32, (DSS_TP, PAGE_SIZE), 1)
    trow = lax.broadcasted_iota(jnp.int32, (DSS_TP, PAGE_SIZE), 0)
    qi = qi_ref[0]
    wb = wb_ref[0]

    def page_keys(kp):
        sc = lax.dot_general(qi, kp.astype(cdt), (((1,), (1,)), ((), ())),
                             preferred_element_type=F32) * (IDX_DIM ** -0.5)
        sc = jnp.maximum(sc, 0.0) * wb
        return _sortable_key(sc.reshape(IDX_HEADS, DSS_TP, PAGE_SIZE).sum(axis=0))

    for i in range(DSS_NP1):
        key_sc[s * DSS_NP1 + i] = page_keys(page_refs[i][0, 0])

    @pl.when(s == 0)
    def _():
        kn = page_keys(kinew_ref[0])
        key_sc[n_pages] = jnp.where(lane < n_new, jnp.where(lane <= trow, kn, INT_MIN), INT_MIN)

    @pl.when(s == n_pages // DSS_NP1 - 1)
    def _():
        all_shape = (n_pages + 1, DSS_TP, PAGE_SIZE)

        def count(pred):
            key_idx = (lax.broadcasted_iota(jnp.int32, all_shape, 0) * PAGE_SIZE
                       + lax.broadcasted_iota(jnp.int32, all_shape, 2))
            acc = pred(key_sc[...], key_idx).sum(axis=0)
            return jnp.broadcast_to(jnp.sum(acc, axis=1, keepdims=True), (DSS_TP, PAGE_SIZE))

        c_nonneg = count(lambda k, i: jnp.where(k >= 0, 1, 0))
        thr = jnp.where(c_nonneg >= topk, 0, INT_MIN).astype(jnp.int32)

        def thr_bit(b, thr):
            cand = thr + jnp.left_shift(jnp.int32(1), 30 - b)
            return jnp.where(count(lambda k, i: jnp.where(k >= cand, 1, 0)) >= topk, cand, thr)

        thr = lax.fori_loop(0, 31, thr_bit, thr)
        need = topk - count(lambda k, i: jnp.where(k > thr, 1, 0))

        def lim_bit(b, lim):
            cand = lim + jnp.left_shift(jnp.int32(1), idx_bits - 1 - b)
            c = count(lambda k, i: jnp.where(k == thr, jnp.where(i < cand, 1, 0), 0))
            return jnp.where(c <= need, cand, lim)

        lim = lax.fori_loop(0, idx_bits, lim_bit, jnp.zeros((DSS_TP, PAGE_SIZE), jnp.int32))

        k = key_sc[...]
        key_idx = (lax.broadcasted_iota(jnp.int32, all_shape, 0) * PAGE_SIZE
                   + lax.broadcasted_iota(jnp.int32, all_shape, 2))
        tie = jnp.where(k == thr, jnp.where(key_idx < lim, 0.0, NEG_BIG), NEG_BIG)
        mask_ref[0] = jnp.where(k == INT_MIN, NEG_BIG, jnp.where(k > thr, 0.0, tie))


def _dsa_sample_attend_kernel(pt_ref, q_ref, z_ref, mask_ref, masknew_ref, bnear_ref, bfar_ref, kvnew_ref, *rest,
                              n_pages, n_far, cdt):
    del pt_ref
    page_refs, o_ref = rest[:DSS_NP2], rest[DSS_NP2]
    m_sc, l_sc, acc_sc = rest[DSS_NP2 + 1:]
    s = pl.program_id(1)
    heads = range(KV_D)
    rep = DSS_GROWS // DSS_TP

    def attend(pages, masks, page_ids):
        n = len(pages)
        madd = jnp.concatenate([jnp.concatenate([mk] * rep, axis=0) for mk in masks], axis=1)
        kT = [jnp.concatenate([pg(0, h) for pg in pages], axis=1).astype(cdt) for h in heads]
        vT = [jnp.concatenate([pg(1, h) for pg in pages], axis=1).astype(cdt) for h in heads]
        logits = [jnp.dot(q_ref[0, h], kT[h], preferred_element_type=F32) for h in heads]
        bias = [jnp.concatenate([jnp.where(pid >= n_far, bnear_ref[jnp.maximum(pid - n_far, 0), h], bfar_ref[h])
                                 for pid in page_ids], axis=1) for h in heads]
        logits = [lg + (b + madd) for lg, b in zip(logits, bias)]
        m_old = [m_sc[h] for h in heads]
        m_new = [jnp.maximum(mo, jnp.broadcast_to(jnp.max(lg, axis=1, keepdims=True), mo.shape))
                 for mo, lg in zip(m_old, logits)]
        alpha = [jnp.exp(mo - mn) for mo, mn in zip(m_old, m_new)]
        p = [jnp.exp(lg - jnp.concatenate([mn] * n, axis=1)) for lg, mn in zip(logits, m_new)]
        pv = [lax.dot_general(ph.astype(cdt), vT[h], (((1,), (1,)), ((), ())), preferred_element_type=F32)
              for h, ph in zip(heads, p)]
        for h in heads:
            l_sc[h] = l_sc[h] * alpha[h] + jnp.broadcast_to(jnp.sum(p[h], axis=1, keepdims=True), alpha[h].shape)
            acc_sc[h] = acc_sc[h] * alpha[h][:, :HEAD_DIM] + pv[h]
            m_sc[h] = m_new[h]

    @pl.when(s == 0)
    def _():
        m_sc[...] = jnp.full_like(m_sc, NEG_BIG)
        l_sc[...] = jnp.zeros_like(l_sc)
        acc_sc[...] = jnp.zeros_like(acc_sc)
        attend([lambda c, h: kvnew_ref[0, c, h]], [masknew_ref[0, 0]], [n_pages])

    attend([(lambda c, h, r=r: r[0, 0, c, h]) for r in page_refs], [mask_ref[0, i] for i in range(DSS_NP2)],
           [s * DSS_NP2 + i for i in range(DSS_NP2)])

    @pl.when(s == n_pages // DSS_NP2 - 1)
    def _():
        for h in heads:
            z = z_ref[0, h]
            o_ref[0, h] = acc_sc[h] * (1.0 / l_sc[h][:, :HEAD_DIM]) * (z * (1.0 / (1.0 + jnp.exp(-z))))


def dsa_sample(q_s, z_s, qi_s, wi_s, ki_s, kv_s, kv_pool, kidx_pool, layer, page_table, rel_bias, cdt=BF16):
    n_dec, n_pages = page_table.shape
    dec_len = q_s.shape[0] // n_dec
    past = n_pages * PAGE_SIZE
    total = past + dec_len
    topk = min(TOPK_MAX, total // 4)
    idx_bits = int(math.log2(total)) + 1
    pad_t = DSS_TP - dec_len
    n_pairs = KV_D // 2
    eye2 = jnp.eye(2, dtype=F32)

    qi = jnp.pad(jnp.swapaxes(qi_s.reshape(n_dec, dec_len, IDX_HEADS, IDX_DIM), 1, 2), ((0, 0), (0, 0), (0, pad_t), (0, 0)))
    qi = qi.reshape(n_dec, IDX_HEADS * DSS_TP, IDX_DIM).astype(cdt)
    wb = jnp.pad(jnp.swapaxes(wi_s.reshape(n_dec, dec_len, IDX_HEADS), 1, 2) * (IDX_HEADS ** -0.5), ((0, 0), (0, 0), (0, pad_t)))
    wb = jnp.broadcast_to(wb.reshape(n_dec, IDX_HEADS * DSS_TP, 1), (n_dec, IDX_HEADS * DSS_TP, PAGE_SIZE))
    ki_new = jnp.pad(ki_s.reshape(n_dec, dec_len, IDX_DIM), ((0, 0), (0, PAGE_SIZE - dec_len), (0, 0)))
    kidx4 = kidx_pool.reshape(kidx_pool.shape[0], kidx_pool.shape[1], PAGE_SIZE, IDX_DIM)
    page_spec = lambda np_, i, width: pl.BlockSpec(
        (1, 1, PAGE_SIZE, width), lambda b, s, pt: (layer, pt[b, s * np_ + i], 0, 0))
    per_b3 = lambda b, s, pt: (b, 0, 0)
    mask = pl.pallas_call(
        functools.partial(_dsa_sample_select_kernel, n_pages=n_pages, n_new=dec_len, topk=topk, idx_bits=idx_bits,
                          cdt=cdt),
        grid_spec=pltpu.PrefetchScalarGridSpec(
            num_scalar_prefetch=1, grid=(n_dec, n_pages // DSS_NP1),
            in_specs=[pl.BlockSpec((1, IDX_HEADS * DSS_TP, IDX_DIM), per_b3),
                      pl.BlockSpec((1, IDX_HEADS * DSS_TP, PAGE_SIZE), per_b3),
                      pl.BlockSpec((1, PAGE_SIZE, IDX_DIM), per_b3)]
                     + [page_spec(DSS_NP1, i, IDX_DIM) for i in range(DSS_NP1)],
            out_specs=pl.BlockSpec((1, n_pages + 1, DSS_TP, PAGE_SIZE), lambda b, s, pt: (b, 0, 0, 0)),
            scratch_shapes=[pltpu.VMEM((n_pages + 1, DSS_TP, PAGE_SIZE), jnp.int32)]),
        out_shape=jax.ShapeDtypeStruct((n_dec, n_pages + 1, DSS_TP, PAGE_SIZE), F32),
        compiler_params=pltpu.CompilerParams(
            dimension_semantics=("parallel", "arbitrary"), vmem_limit_bytes=VMEM_LIMIT),
        name="dsa_sample_select",
    )(page_table, qi, wb, ki_new, *([kidx4] * DSS_NP1))

    def head_rows(t, scale):
        t = t.reshape(n_dec, dec_len, KV_D, G_D, HEAD_DIM).transpose(0, 2, 3, 1, 4) * scale
        return jnp.pad(t, ((0, 0),) * 3 + ((0, pad_t), (0, 0))).reshape(n_dec, KV_D, DSS_GROWS, HEAD_DIM)

    q_hr = head_rows(q_s, HEAD_DIM ** -0.5).astype(cdt)
    z_hr = head_rows(z_s, 1.0)
    n_far = max(0, min(n_pages, (past - (PAGE_SIZE - 1) - T5_LAST_BUCKET_DIST) // PAGE_SIZE + 1))
    near_pages = jnp.arange(n_far, n_pages + 1)
    dist = (past + jnp.arange(DSS_TP)[None, :, None]
            - (near_pages[:, None, None] * PAGE_SIZE + jnp.arange(PAGE_SIZE)[None, None, :]))
    b_near = rel_bias[rel_bucket(dist)].astype(F32)
    b_near = b_near.transpose(0, 3, 1, 2).reshape(n_pages + 1 - n_far, KV_D, DSS_GROWS, PAGE_SIZE)
    b_far = jnp.broadcast_to(rel_bias[N_BUCKETS - 1].astype(F32)[:, None, None], (N_HEADS, DSS_TP, PAGE_SIZE))
    b_far = b_far.reshape(KV_D, DSS_GROWS, PAGE_SIZE)
    kv_new = jnp.pad(kv_s.reshape(n_dec, dec_len, 2, KV_D, HEAD_DIM).transpose(0, 2, 3, 4, 1),
                     ((0, 0),) * 4 + ((0, PAGE_SIZE - dec_len),))
    kv_t = kv_pool.reshape(kv_pool.shape[0], kv_pool.shape[1], PAGE_SIZE, 2, KV_D, HEAD_DIM).transpose(0, 1, 3, 4, 5, 2)
    kv_page = lambda i: pl.BlockSpec((1, 1, 2, KV_D, HEAD_DIM, PAGE_SIZE),
                                     lambda b, s, pt: (layer, pt[b, s * DSS_NP2 + i], 0, 0, 0, 0))
    per_b4 = lambda b, s, pt: (b, 0, 0, 0)
    o = pl.pallas_call(
        functools.partial(_dsa_sample_attend_kernel, n_pages=n_pages, n_far=n_far, cdt=cdt),
        grid_spec=pltpu.PrefetchScalarGridSpec(
            num_scalar_prefetch=1, grid=(n_dec, n_pages // DSS_NP2),
            in_specs=[pl.BlockSpec((1, KV_D, DSS_GROWS, HEAD_DIM), per_b4),
                      pl.BlockSpec((1, KV_D, DSS_GROWS, HEAD_DIM), per_b4),
                      pl.BlockSpec((1, DSS_NP2, DSS_TP, PAGE_SIZE), lambda b, s, pt: (b, s, 0, 0)),
                      pl.BlockSpec((1, 1, DSS_TP, PAGE_SIZE), lambda b, s, pt: (b, n_pages, 0, 0)),
                      pl.BlockSpec(b_near.shape, lambda b, s, pt: (0, 0, 0, 0)),
                      pl.BlockSpec(b_far.shape, lambda b, s, pt: (0, 0, 0)),
                      pl.BlockSpec((1, 2, KV_D, HEAD_DIM, PAGE_SIZE), lambda b, s, pt: (b, 0, 0, 0, 0))]
                     + [kv_page(i) for i in range(DSS_NP2)],
            out_specs=pl.BlockSpec((1, KV_D, DSS_GROWS, HEAD_DIM), per_b4),
            scratch_shapes=[pltpu.VMEM((KV_D, DSS_GROWS, LANE), F32), pltpu.VMEM((KV_D, DSS_GROWS, LANE), F32),
                            pltpu.VMEM((KV_D, DSS_GROWS, HEAD_DIM), F32)]),
        out_shape=jax.ShapeDtypeStruct((n_dec, KV_D, DSS_GROWS, HEAD_DIM), F32),
        compiler_params=pltpu.CompilerParams(
            dimension_semantics=("parallel", "arbitrary"), vmem_limit_bytes=VMEM_LIMIT),
        name="dsa_sample_attend",
    )(page_table, q_hr, z_hr, mask, mask, b_near, b_far, kv_new, *([kv_t] * DSS_NP2))
    o = o.reshape(n_dec, KV_D, G_D, DSS_TP, HEAD_DIM)[:, :, :, :dec_len]
    return o.transpose(0, 3, 1, 2, 4).reshape(n_dec * dec_len, ATT_WIDTH)


def _pad_cols(w, n):
    return jnp.pad(w, ((0, 0), (0, n - w.shape[1])))


def kernel(x_prompt, x_sample, cache_a_kv, state_s5, state_gdn, state_gdn_conv, cache_d_kv, cache_d_kidx,
           page_table, p_prompt, p_sample, rel_bias, ln_g, ln_b, ple_gate_w, ple_w,
           a_w_in, a_sinks, a_w_out,
           s5_w_in, s5_a_re, s5_a_im, s5_b_re, s5_b_im, s5_c_re, s5_c_im, s5_d, s5_log_dt, s5_w_glu, s5_w_out,
           gdn_w_in, gdn_conv_w, gdn_a_log, gdn_dt_bias, gdn_norm_w, gdn_w_out,
           dsa_w_in, dsa_w_out):
    x = join_tokens(x_prompt, x_sample)
    x_bf = x.astype(BF16)
    outs = {}

    def post(i, x, h):
        p_bf = join_tokens(p_prompt[i], p_sample[i]).astype(BF16)
        return post_norm_ple(x, h, p_bf, ln_g[i], ln_b[i], ple_gate_w[i].astype(BF16), ple_w[i].astype(BF16))

    w_in = a_w_in[0]
    c_k, c_v, c_z = ATT_WIDTH, ATT_WIDTH + A_KV, ATT_WIDTH + 2 * A_KV
    w_q, w_z = w_in[:, :c_k], w_in[:, c_z:]
    w_out_bf = a_w_out[0].astype(BF16)
    kv_nat = matmul(x_bf, w_in[:, c_k:c_z].astype(BF16))
    xT_bf = x_bf[:N_PROMPT_TOK].T
    qvT = matmul(jnp.concatenate([w_q, w_in[:, c_v:c_z]], axis=1).T.astype(BF16), xT_bf, out_dtype=BF16)
    zT = matmul(w_z.T.astype(BF16), xT_bf)
    bias_p, sink_p, bias_s, sink_s = swa_tables(rel_bias, a_sinks[0], DEC_SEQ)
    h_p = matmul_ta(swa_prompt(qvT, zT, kv_nat, bias_p, sink_p, n_batch=BATCH, seq_len=SEQ), w_out_bf)
    qz_s = matmul(x_bf[N_PROMPT_TOK:], jnp.concatenate([w_q, w_z], axis=1).astype(BF16))
    gs, outs['a_s'] = swa_sample(qz_s[:, :ATT_WIDTH], qz_s[:, ATT_WIDTH:], kv_nat[N_PROMPT_TOK:], cache_a_kv[0],
                                 bias_s, sink_s)
    outs['a_p'] = kv_nat[:N_PROMPT_TOK].reshape(BATCH, SEQ, 2, KV_A, HEAD_DIM)[:, SEQ - WINDOW:]
    x, x_bf = post(0, x, jnp.concatenate([h_p, matmul(gs.astype(BF16), w_out_bf)], axis=0))

    proj = matmul(x_bf, s5_w_in[0].astype(BF16))
    tables = s5_tables(s5_a_re[0], s5_a_im[0], s5_b_re[0], s5_b_im[0], s5_c_re[0], s5_c_im[0], s5_log_dt[0])
    gp, gs, outs['s5_p'], outs['s5_s'] = s5_layer(proj, state_s5[0], tables, s5_d[0], s5_w_glu[0],
                                                  n_batch=BATCH, seq_len=SEQ, n_dec=DEC_BATCH, dec_len=DEC_SEQ)
    w_out_bf = s5_w_out[0].astype(BF16)
    x, x_bf = post(1, x, jnp.concatenate([matmul(gp, w_out_bf), matmul(gs, w_out_bf)], axis=0))

    w_in = gdn_w_in[0]
    c_gz = GDN_CONV_CH + GDN_V_WIDTH
    qkvz = matmul(x_bf, w_in[:, :c_gz].astype(BF16))
    ab = matmul(x_bf, _pad_cols(w_in[:, c_gz:], LANE).astype(BF16))
    gp, gs, outs['gd_p'], outs['gd_s'], outs['gc_p'], outs['gc_s'] = gdn_layer(
        qkvz, ab, state_gdn[0], state_gdn_conv[0], gdn_conv_w[0], gdn_a_log[0], gdn_dt_bias[0], gdn_norm_w[0],
        n_batch=BATCH, seq_len=SEQ, n_dec=DEC_BATCH, dec_len=DEC_SEQ)
    w_out_bf = gdn_w_out[0].astype(BF16)
    x, x_bf = post(2, x, jnp.concatenate([matmul(gp, w_out_bf), matmul(gs.astype(BF16), w_out_bf)], axis=0))

    w_in = dsa_w_in[0]
    c_z = 2 * ATT_WIDTH + 2 * D_KV
    c_qi = c_z + IDX_HEADS * IDX_DIM
    c_kv = ATT_WIDTH + 2 * D_KV
    w_q, w_kv, w_z, w_qi = w_in[:, :ATT_WIDTH], w_in[:, ATT_WIDTH:c_kv], w_in[:, c_kv:c_z], w_in[:, c_z:c_qi]
    w_out_bf = dsa_w_out[0].astype(BF16)
    kv_nat = matmul(x_bf, w_kv.astype(BF16))
    kiw = matmul(x_bf, _pad_cols(w_in[:, c_qi:], 2 * LANE).astype(BF16))
    xT_bf = x_bf[:N_PROMPT_TOK].T
    qqiT = matmul(jnp.concatenate([w_q, w_qi], axis=1).T.astype(BF16), xT_bf, out_dtype=BF16)
    zT = matmul(w_z.T.astype(BF16), xT_bf)
    wiT = matmul(w_in[:, c_qi + IDX_DIM:].T.astype(BF16), xT_bf)
    nc = SEQ // DSA_KC
    kv_p = kv_nat[:N_PROMPT_TOK]
    v4T = jnp.swapaxes(kv_p[:, D_KV:].astype(BF16).reshape(BATCH, nc, DSA_KC, D_KV), 2, 3)
    gT = dsa_prompt_attend(qqiT, wiT, zT, kiw[:N_PROMPT_TOK].reshape(BATCH, nc, DSA_KC, 2 * LANE),
                           kv_p.reshape(BATCH, nc, DSA_KC, 2 * D_KV), v4T, dsa_bias_windows(rel_bias, SEQ),
                           n_batch=BATCH, seq_len=SEQ)
    h_p = matmul_ta(gT, w_out_bf)
    x_s = x_bf[N_PROMPT_TOK:]
    qzqi_s = matmul(x_s, jnp.concatenate([w_q, w_z, w_qi], axis=1).astype(BF16))
    kiw_s = kiw[N_PROMPT_TOK:]
    gs = dsa_sample(qzqi_s[:, :ATT_WIDTH], qzqi_s[:, ATT_WIDTH:2 * ATT_WIDTH], qzqi_s[:, 2 * ATT_WIDTH:],
                    kiw_s[:, IDX_DIM:IDX_DIM + IDX_HEADS], kiw_s[:, :IDX_DIM], kv_nat[N_PROMPT_TOK:],
                    cache_d_kv, cache_d_kidx, 0, page_table, rel_bias)
    h_s = matmul(gs.astype(BF16), w_out_bf)
    outs['dkv_p'] = kv_p.reshape(BATCH, SEQ, 2, KV_D, HEAD_DIM)
    outs['dkv_s'] = kv_nat[N_PROMPT_TOK:].reshape(DEC_BATCH, DEC_SEQ, 2, KV_D, HEAD_DIM)
    outs['dki_p'] = kiw[:N_PROMPT_TOK, :IDX_DIM].reshape(BATCH, SEQ, IDX_DIM)
    outs['dki_s'] = kiw_s[:, :IDX_DIM].reshape(DEC_BATCH, DEC_SEQ, IDX_DIM)
    x, x_bf = post(3, x, jnp.concatenate([h_p, h_s], axis=0))

    yp, ys = split_tokens(x)
    st = lambda name: outs[name][None]
    return (yp, ys, st('a_p'), st('a_s'), st('s5_p'), st('s5_s'), st('gd_p'), st('gd_s'),
            st('gc_p'), st('gc_s'), st('dkv_p'), st('dkv_s'), st('dki_p'), st('dki_s'))
```

```python
import functools
import math

import jax
import jax.numpy as jnp
from jax import lax
from jax.experimental import pallas as pl
from jax.experimental.pallas import tpu as pltpu

D_MODEL = 2048
BATCH = 4
SEQ = 2048
DEPTH = 4
DEC_BATCH = 32
DEC_SEQ = 4
PAGE_SIZE = 128
N_MIXERS = 4
PLE_DIM = 256
ALPHA = (2 * DEPTH) ** 0.25
LN_EPS = 1e-5
N_BUCKETS = 32
REL_MAX_DIST = 2048
N_HEADS = 32
HEAD_DIM = 64
ATT_WIDTH = N_HEADS * HEAD_DIM
WINDOW = 128
KV_A = 4
A_KV = KV_A * HEAD_DIM
KV_D = 8
D_KV = KV_D * HEAD_DIM
IDX_HEADS = 16
IDX_DIM = 128
TOPK_MAX = 256
Q_BLOCK = 128
S5_WIDTH = D_MODEL
S5_GROUP = 16
S5_GROUPS = S5_WIDTH // S5_GROUP
S5_STATE = 64
GDN_QK_HEADS = 16
GDN_V_HEADS = 32
GDN_DK = 128
GDN_DV = 128
GDN_CONV = 4
GDN_CHUNK = 64
GDN_QK_WIDTH = GDN_QK_HEADS * GDN_DK
GDN_V_WIDTH = GDN_V_HEADS * GDN_DV
GDN_CONV_CH = 2 * GDN_QK_WIDTH + GDN_V_WIDTH

F32 = jnp.float32
BF16 = jnp.bfloat16

N_PROMPT_TOK = BATCH * SEQ
N_SAMPLE_TOK = DEC_BATCH * DEC_SEQ
N_TOK = N_PROMPT_TOK + N_SAMPLE_TOK

V7X_VMEM_BYTES = 64 * 1024 * 1024
VMEM_LIMIT = 48 * 1024 * 1024
LANE = 128


def _mm_kernel(x_ref, w_ref, o_ref):
    o_ref[...] = jnp.dot(x_ref[...], w_ref[...], preferred_element_type=F32).astype(o_ref.dtype)


def _pick_tile(n, prefs):
    for t in prefs:
        if n % t == 0:
            return t
    raise ValueError(f"no tile for {n}")


def matmul(x, w, out_dtype=F32):
    m, k = x.shape
    n = w.shape[1]
    tm = _pick_tile(m, (640, 512, 320, 256, 128, 64, 32, 16, 8))
    tn = _pick_tile(n, (512, 384, 256, 128))
    return pl.pallas_call(
        _mm_kernel,
        grid=(m // tm, n // tn),
        in_specs=[pl.BlockSpec((tm, k), lambda i, j: (i, 0)),
                  pl.BlockSpec((k, tn), lambda i, j: (0, j))],
        out_specs=pl.BlockSpec((tm, tn), lambda i, j: (i, j)),
        out_shape=jax.ShapeDtypeStruct((m, n), out_dtype),
        compiler_params=pltpu.CompilerParams(
            dimension_semantics=("parallel", "parallel"), vmem_limit_bytes=VMEM_LIMIT),
        name="proj_matmul",
    )(x, w)


def _mm_ta_kernel(xt_ref, w_ref, o_ref):
    o_ref[...] = lax.dot_general(xt_ref[...], w_ref[...], (((0,), (0,)), ((), ())),
                                 preferred_element_type=F32).astype(o_ref.dtype)


def matmul_ta(xt, w, out_dtype=F32):
    k, m = xt.shape
    n = w.shape[1]
    tm = _pick_tile(m, (512, 256, 128))
    tn = _pick_tile(n, (512, 384, 256, 128))
    return pl.pallas_call(
        _mm_ta_kernel,
        grid=(m // tm, n // tn),
        in_specs=[pl.BlockSpec((k, tm), lambda i, j: (0, i)),
                  pl.BlockSpec((k, tn), lambda i, j: (0, j))],
        out_specs=pl.BlockSpec((tm, tn), lambda i, j: (i, j)),
        out_shape=jax.ShapeDtypeStruct((m, n), out_dtype),
        compiler_params=pltpu.CompilerParams(
            dimension_semantics=("parallel", "parallel"), vmem_limit_bytes=VMEM_LIMIT),
        name="proj_matmul_ta",
    )(xt, w)


POST_TM = 320
POST_TN = 512


def _post_kernel(x_ref, h_ref, p_ref, g_ref, b_ref, wg_ref, wp_ref, o_ref, obf_ref, y_sc, ybf_sc):
    j = pl.program_id(1)

    @pl.when(j == 0)
    def _():
        t = ALPHA * x_ref[...] + h_ref[...]
        mu = jnp.mean(t, axis=-1, keepdims=True)
        d = t - mu
        var = jnp.mean(d * d, axis=-1, keepdims=True)
        y = d * lax.rsqrt(var + LN_EPS) * g_ref[...] + b_ref[...]
        ybf_sc[...] = y.astype(BF16)
        for jj in range(D_MODEL // POST_TN):
            y_sc[jj] = y[:, jj * POST_TN:(jj + 1) * POST_TN]

    gate = jnp.dot(ybf_sc[...], wg_ref[...], preferred_element_type=F32)
    ple = jnp.dot(p_ref[...], wp_ref[...], preferred_element_type=F32)
    o = y_sc[j] + (1.0 / (1.0 + jnp.exp(-gate))) * ple
    o_ref[...] = o
    obf_ref[...] = o.astype(BF16)


def post_norm_ple(x, h, p_bf, g, b, wg_bf, wp_bf):
    m = x.shape[0]
    tm, tn = POST_TM, POST_TN
    return pl.pallas_call(
        _post_kernel,
        grid=(m // tm, D_MODEL // tn),
        in_specs=[pl.BlockSpec((tm, D_MODEL), lambda i, j: (i, 0)),
                  pl.BlockSpec((tm, D_MODEL), lambda i, j: (i, 0)),
                  pl.BlockSpec((tm, PLE_DIM), lambda i, j: (i, 0)),
                  pl.BlockSpec((1, D_MODEL), lambda i, j: (0, 0)),
                  pl.BlockSpec((1, D_MODEL), lambda i, j: (0, 0)),
                  pl.BlockSpec((D_MODEL, tn), lambda i, j: (0, j)),
                  pl.BlockSpec((PLE_DIM, tn), lambda i, j: (0, j))],
        out_specs=[pl.BlockSpec((tm, tn), lambda i, j: (i, j)),
                   pl.BlockSpec((tm, tn), lambda i, j: (i, j))],
        out_shape=[jax.ShapeDtypeStruct((m, D_MODEL), F32),
                   jax.ShapeDtypeStruct((m, D_MODEL), BF16)],
        scratch_shapes=[pltpu.VMEM((D_MODEL // tn, tm, tn), F32),
                        pltpu.VMEM((tm, D_MODEL), BF16)],
        compiler_params=pltpu.CompilerParams(
            dimension_semantics=("parallel", "arbitrary"), vmem_limit_bytes=VMEM_LIMIT),
        name="post_norm_ple",
    )(x, h, p_bf, g.reshape(1, D_MODEL), b.reshape(1, D_MODEL), wg_bf, wp_bf)


def rel_bucket(dist):
    n = jnp.maximum(dist, 0)
    exact = N_BUCKETS // 2
    logb = exact + (jnp.log(jnp.maximum(n, exact).astype(F32) / exact)
                    / math.log(REL_MAX_DIST / exact) * (N_BUCKETS - exact)).astype(jnp.int32)
    return jnp.where(n < exact, n, jnp.minimum(logb, N_BUCKETS - 1))


def split_tokens(t):
    c = t.shape[-1]
    return (t[:N_PROMPT_TOK].reshape(BATCH, SEQ, c), t[N_PROMPT_TOK:].reshape(DEC_BATCH, DEC_SEQ, c))


def join_tokens(tp, ts):
    c = tp.shape[-1]
    return jnp.concatenate([tp.reshape(N_PROMPT_TOK, c), ts.reshape(N_SAMPLE_TOK, c)], axis=0)


G_A = N_HEADS // KV_A
SWA_KEYS = 2 * WINDOW


def swa_tables(rel_bias, sinks, dec_len):
    def heads_to(b, lead):
        return jnp.moveaxis(b, -1, 0).reshape((KV_A, G_A) + lead)

    dist = jnp.arange(WINDOW)[None, :] - (jnp.arange(SWA_KEYS)[:, None] - WINDOW)
    ok = (dist >= 0) & (dist < WINDOW)
    b = jnp.where(ok[..., None], rel_bias[rel_bucket(dist)].astype(F32), NEG_BIG)
    bias_p = heads_to(b, (SWA_KEYS, WINDOW)).transpose(0, 2, 1, 3).reshape(KV_A, SWA_KEYS, G_A * WINDOW)
    sink_p = jnp.broadcast_to(sinks.astype(F32).reshape(KV_A, 1, G_A, 1), (KV_A, 1, G_A, WINDOW))
    sink_p = sink_p.reshape(KV_A, 1, G_A * WINDOW)
    key_i = jnp.arange(SWA_KEYS)[None, :]
    dist = jnp.arange(dec_len)[:, None] + WINDOW - key_i
    ok = (dist >= 0) & (dist < WINDOW) & (key_i < WINDOW + dec_len)
    b = jnp.where(ok[..., None], rel_bias[rel_bucket(dist)].astype(F32), NEG_BIG)
    bias_s = heads_to(b, (dec_len, SWA_KEYS)).reshape(KV_A, G_A * dec_len, SWA_KEYS)
    sink_s = jnp.broadcast_to(sinks.astype(F32).reshape(KV_A, G_A, 1, 1), (KV_A, G_A, dec_len, LANE))
    sink_s = sink_s.reshape(KV_A, G_A * dec_len, LANE)
    return bias_p, sink_p, bias_s, sink_s


def _swa_prompt_kernel(qT_ref, zT_ref, vTp_ref, vTc_ref, kp_ref, kc_ref, bias_ref, sink_ref, o_ref, *, cdt):
    first = pl.program_id(1) == 0
    kk = jnp.concatenate([kp_ref[...], kc_ref[...]], axis=0)
    vT = jnp.concatenate([vTp_ref[...], vTc_ref[...]], axis=1)
    prev_key = lax.broadcasted_iota(jnp.int32, (SWA_KEYS, G_A * WINDOW), 0) < WINDOW
    pw = 2 * HEAD_DIM
    for j in range(KV_A):
        kpair = kk[:, (j // 2) * pw:(j // 2 + 1) * pw].astype(cdt)
        qj = jnp.concatenate([qT_ref[(G_A * j + g) * HEAD_DIM:(G_A * j + g + 1) * HEAD_DIM, :]
                              for g in range(G_A)], axis=1)
        qj = (qj.astype(F32) * (HEAD_DIM ** -0.5)).astype(cdt)
        zpad = jnp.zeros_like(qj)
        rhs = jnp.concatenate([qj, zpad] if j % 2 == 0 else [zpad, qj], axis=0)
        s = jnp.dot(kpair, rhs, preferred_element_type=F32) + bias_ref[j]
        s = jnp.where(prev_key, jnp.where(first, NEG_BIG, s), s)
        sink = sink_ref[j]
        m = jnp.maximum(jnp.max(s, axis=0, keepdims=True), sink)
        e = jnp.exp(s - m)
        den = jnp.sum(e, axis=0, keepdims=True) + jnp.exp(sink - m)
        p = (e * (1.0 / den)).astype(cdt)
        acc = jnp.dot(vT[j * HEAD_DIM:(j + 1) * HEAD_DIM, :].astype(cdt), p, preferred_element_type=F32)
        for g in range(G_A):
            r0 = (G_A * j + g) * HEAD_DIM
            z = zT_ref[r0:r0 + HEAD_DIM, :]
            o_ref[r0:r0 + HEAD_DIM, :] = (acc[:, g * WINDOW:(g + 1) * WINDOW]
                                          * (z * (1.0 / (1.0 + jnp.exp(-z))))).astype(o_ref.dtype)


def swa_prompt(qvT, zT, kv_nat, bias_p, sink_p, *, n_batch, seq_len, cdt=BF16):
    nb = seq_len // WINDOW
    cur = lambda b, i: b * nb + i
    prev = lambda b, i: b * nb + jnp.maximum(i - 1, 0)
    v_row_blk = ATT_WIDTH // A_KV
    return pl.pallas_call(
        functools.partial(_swa_prompt_kernel, cdt=cdt),
        grid=(n_batch, nb),
        in_specs=[pl.BlockSpec((ATT_WIDTH, WINDOW), lambda b, i: (0, cur(b, i))),
                  pl.BlockSpec((ATT_WIDTH, WINDOW), lambda b, i: (0, cur(b, i))),
                  pl.BlockSpec((A_KV, WINDOW), lambda b, i: (v_row_blk, prev(b, i))),
                  pl.BlockSpec((A_KV, WINDOW), lambda b, i: (v_row_blk, cur(b, i))),
                  pl.BlockSpec((WINDOW, A_KV), lambda b, i: (prev(b, i), 0)),
                  pl.BlockSpec((WINDOW, A_KV), lambda b, i: (cur(b, i), 0)),
                  pl.BlockSpec(bias_p.shape, lambda b, i: (0, 0, 0)),
                  pl.BlockSpec(sink_p.shape, lambda b, i: (0, 0, 0))],
        out_specs=pl.BlockSpec((ATT_WIDTH, WINDOW), lambda b, i: (0, cur(b, i))),
        out_shape=jax.ShapeDtypeStruct((ATT_WIDTH, n_batch * seq_len), BF16),
        compiler_params=pltpu.CompilerParams(
            dimension_semantics=("parallel", "parallel"), vmem_limit_bytes=VMEM_LIMIT),
        name="swa_prompt",
    )(qvT, zT, qvT, qvT, kv_nat, kv_nat, bias_p, sink_p)


def _swa_sample_kernel(q_ref, z_ref, k_ref, v_ref, bias_ref, sink_ref, o_ref, *, cdt):
    for j in range(KV_A):
        s = lax.dot_general(q_ref[0, j], k_ref[0, j].astype(cdt), (((1,), (1,)), ((), ())),
                            preferred_element_type=F32) + bias_ref[j]
        sink = sink_ref[j][:, 0:1]
        m = jnp.maximum(jnp.max(s, axis=1, keepdims=True), sink)
        e = jnp.exp(s - m)
        den = jnp.sum(e, axis=1, keepdims=True) + jnp.exp(sink - m)
        p = (e * (1.0 / den)).astype(cdt)
        z = z_ref[0, j]
        o_ref[0, j] = jnp.dot(p, v_ref[0, j].astype(cdt), preferred_element_type=F32) * (z * (1.0 / (1.0 + jnp.exp(-z))))


def swa_sample(q_s, z_s, kv_s, kv_cache, bias_s, sink_s, cdt=BF16):
    n_dec = kv_cache.shape[0]
    dec_len = q_s.shape[0] // n_dec
    rows = G_A * dec_len

    def head_rows(t, scale):
        t = t.reshape(n_dec, dec_len, KV_A, G_A, HEAD_DIM).transpose(0, 2, 3, 1, 4) * scale
        return jnp.pad(t.reshape(n_dec, KV_A, rows, HEAD_DIM), ((0, 0), (0, 0), (0, 0), (0, HEAD_DIM)))

    new = kv_s.reshape(n_dec, dec_len, 2, KV_A, HEAD_DIM)
    cat = jnp.concatenate([kv_cache, new], axis=1)
    keys = jnp.pad(cat.transpose(2, 0, 3, 1, 4),
                   ((0, 0), (0, 0), (0, 0), (0, SWA_KEYS - WINDOW - dec_len), (0, HEAD_DIM)))
    blk = lambda r: pl.BlockSpec((1, KV_A, r, 2 * HEAD_DIM), lambda b: (b, 0, 0, 0))
    o = pl.pallas_call(
        functools.partial(_swa_sample_kernel, cdt=cdt),
        grid=(n_dec,),
        in_specs=[blk(rows), blk(rows), blk(SWA_KEYS), blk(SWA_KEYS),
                  pl.BlockSpec(bias_s.shape, lambda b: (0, 0, 0)),
                  pl.BlockSpec(sink_s.shape, lambda b: (0, 0, 0))],
        out_specs=blk(rows),
        out_shape=jax.ShapeDtypeStruct((n_dec, KV_A, rows, 2 * HEAD_DIM), F32),
        compiler_params=pltpu.CompilerParams(dimension_semantics=("parallel",), vmem_limit_bytes=VMEM_LIMIT),
        name="swa_sample",
    )(head_rows(q_s, HEAD_DIM ** -0.5).astype(cdt), head_rows(z_s, 1.0), keys[0], keys[1], bias_s, sink_s)
    o = o[..., :HEAD_DIM].reshape(n_dec, KV_A, G_A, dec_len, HEAD_DIM).transpose(0, 3, 1, 2, 4)
    return o.reshape(n_dec * dec_len, ATT_WIDTH), cat[:, dec_len:]


S5_SLAB_G = 8
S5_SLAB_CH = S5_SLAB_G * S5_GROUP
S5_SLAB_ST = S5_SLAB_G * S5_STATE
S5_N_SLABS = S5_GROUPS // S5_SLAB_G
S5_CHAINS = 8
S5_HALF_CH = S5_CHAINS * S5_SLAB_CH
S5_T = 256
S5_LT = 2 * S5_SLAB_ST // LANE


def _gelu_tanh(x):
    return 0.5 * x * (1.0 + jnp.tanh(math.sqrt(2.0 / math.pi) * (x + 0.044715 * (x * x * x))))


def s5_tables(a_re, a_im, b_re, b_im, c_re, c_im, log_dt):
    a = lax.complex(a_re, a_im)
    dt = jnp.exp(log_dt)[:, None]
    a_bar = jnp.exp(a * dt)
    b_bar = ((a_bar - 1.0) / a)[..., None] * lax.complex(b_re, b_im)
    eye = jnp.eye(S5_SLAB_G, dtype=F32)

    def b_blk(t):
        t = t.reshape(S5_N_SLABS, S5_SLAB_G, S5_STATE, S5_GROUP)
        return jnp.einsum('ij,sipc->sicjp', eye, t).reshape(S5_N_SLABS, S5_SLAB_CH, S5_SLAB_ST)

    def c_blk(t):
        t = t.reshape(S5_N_SLABS, S5_SLAB_G, S5_GROUP, S5_STATE)
        return jnp.einsum('ij,sicp->sjpic', eye, t).reshape(S5_N_SLABS, S5_SLAB_ST, S5_SLAB_CH)

    bcat = jnp.concatenate([b_blk(b_bar.real), b_blk(b_bar.imag)], axis=2)
    ccat = jnp.concatenate([c_blk(c_re), -c_blk(c_im)], axis=1)
    a_cat = jnp.concatenate([a_bar.real.reshape(S5_N_SLABS, S5_SLAB_ST),
                             a_bar.imag.reshape(S5_N_SLABS, S5_SLAB_ST)], axis=1)
    return a_cat, bcat, ccat


def _s5_prompt_kernel(u_ref, bcat_ref, ccat_ref, a_ref, d_ref, y_ref, hout_ref, sc, h_sc, *, cdt):
    tc = pl.program_id(2)
    n_lt_half = S5_LT // 2

    @pl.when(tc == 0)
    def _():
        h_sc[...] = jnp.zeros_like(h_sc)

    for j in range(S5_CHAINS):
        uj = u_ref[:, j * S5_SLAB_CH:(j + 1) * S5_SLAB_CH].astype(cdt)
        bu = jnp.dot(uj, bcat_ref[0, j], preferred_element_type=F32)
        for lt in range(S5_LT):
            sc[lt, pl.ds(j, S5_T, stride=S5_CHAINS), :] = bu[:, lt * LANE:(lt + 1) * LANE]

    a_re = [a_ref[0, :, lt * LANE:(lt + 1) * LANE] for lt in range(n_lt_half)]
    a_im = [a_ref[0, :, (n_lt_half + lt) * LANE:(n_lt_half + lt + 1) * LANE] for lt in range(n_lt_half)]

    def step(t, h):
        r0 = pl.multiple_of(t * S5_CHAINS, S5_CHAINS)
        new = list(h)
        for lt in range(n_lt_half):
            hr, hi = h[lt], h[n_lt_half + lt]
            nr = a_re[lt] * hr - a_im[lt] * hi + sc[lt, pl.ds(r0, S5_CHAINS), :]
            ni = a_re[lt] * hi + a_im[lt] * hr + sc[n_lt_half + lt, pl.ds(r0, S5_CHAINS), :]
            sc[lt, pl.ds(r0, S5_CHAINS), :] = nr
            sc[n_lt_half + lt, pl.ds(r0, S5_CHAINS), :] = ni
            new[lt], new[n_lt_half + lt] = nr, ni
        return tuple(new)

    h = lax.fori_loop(0, S5_T, step, tuple(h_sc[lt] for lt in range(S5_LT)), unroll=8)
    for lt in range(S5_LT):
        h_sc[lt] = h[lt]
        hout_ref[0, 0, :, lt * LANE:(lt + 1) * LANE] = h[lt]

    for j in range(S5_CHAINS):
        hcat = jnp.concatenate([sc[lt, pl.ds(j, S5_T, stride=S5_CHAINS), :] for lt in range(S5_LT)], axis=1)
        cols = slice(j * S5_SLAB_CH, (j + 1) * S5_SLAB_CH)
        y = jnp.dot(hcat.astype(cdt), ccat_ref[0, j], preferred_element_type=F32) + d_ref[0, :, cols] * u_ref[:, cols]
        y_ref[:, cols] = _gelu_tanh(y)


def s5_prompt(proj, a_cat, bcat, ccat, d_skip, *, n_batch, seq_len, n_rows_out, cdt=BF16):
    n_t = seq_len // S5_T
    n_half = S5_WIDTH // S5_HALF_CH
    half = lambda t: t.reshape((n_half, S5_CHAINS) + t.shape[1:])
    return pl.pallas_call(
        functools.partial(_s5_prompt_kernel, cdt=cdt),
        grid=(n_batch, n_half, n_t),
        in_specs=[pl.BlockSpec((S5_T, S5_HALF_CH), lambda b, hf, t: (b * n_t + t, hf)),
                  pl.BlockSpec((1, S5_CHAINS, S5_SLAB_CH, 2 * S5_SLAB_ST), lambda b, hf, t: (hf, 0, 0, 0)),
                  pl.BlockSpec((1, S5_CHAINS, 2 * S5_SLAB_ST, S5_SLAB_CH), lambda b, hf, t: (hf, 0, 0, 0)),
                  pl.BlockSpec((1, S5_CHAINS, 2 * S5_SLAB_ST), lambda b, hf, t: (hf, 0, 0)),
                  pl.BlockSpec((1, 1, S5_HALF_CH), lambda b, hf, t: (hf, 0, 0))],
        out_specs=[pl.BlockSpec((S5_T, S5_HALF_CH), lambda b, hf, t: (b * n_t + t, hf)),
                   pl.BlockSpec((1, 1, S5_CHAINS, 2 * S5_SLAB_ST), lambda b, hf, t: (b, hf, 0, 0))],
        out_shape=[jax.ShapeDtypeStruct((n_rows_out, S5_WIDTH), F32),
                   jax.ShapeDtypeStruct((n_batch, n_half, S5_CHAINS, 2 * S5_SLAB_ST), F32)],
        scratch_shapes=[pltpu.VMEM((S5_LT, S5_T * S5_CHAINS, LANE), F32),
                        pltpu.VMEM((S5_LT, S5_CHAINS, LANE), F32)],
        compiler_params=pltpu.CompilerParams(
            dimension_semantics=("parallel", "parallel", "arbitrary"), vmem_limit_bytes=VMEM_LIMIT),
        name="s5_prompt",
    )(proj, half(bcat.astype(cdt)), half(ccat.astype(cdt)), half(a_cat), d_skip.reshape(n_half, 1, S5_HALF_CH))


def _s5_sample_kernel(u_ref, bcat_ref, ccat_ref, a_ref, d_ref, h0_ref, y_ref, hout_ref, sc, *, n_b, n_t, cdt):
    u = u_ref[...]
    bu = jnp.dot(u.astype(cdt), bcat_ref[0], preferred_element_type=F32)
    a_re = a_ref[0, :, :S5_SLAB_ST]
    a_im = a_ref[0, :, S5_SLAB_ST:]
    for bg in range(n_b // 8):
        hr = h0_ref[0, bg * 8:(bg + 1) * 8, :S5_SLAB_ST]
        hi = h0_ref[0, bg * 8:(bg + 1) * 8, S5_SLAB_ST:]
        for t in range(n_t):
            r = t * n_b + bg * 8
            hr, hi = (a_re * hr - a_im * hi + bu[r:r + 8, :S5_SLAB_ST],
                      a_re * hi + a_im * hr + bu[r:r + 8, S5_SLAB_ST:])
            sc[r:r + 8, :S5_SLAB_ST] = hr
            sc[r:r + 8, S5_SLAB_ST:] = hi
        hout_ref[0, bg * 8:(bg + 1) * 8, :S5_SLAB_ST] = hr
        hout_ref[0, bg * 8:(bg + 1) * 8, S5_SLAB_ST:] = hi
    y = jnp.dot(sc[...].astype(cdt), ccat_ref[0], preferred_element_type=F32) + d_ref[0] * u
    y_ref[...] = _gelu_tanh(y)


def s5_sample(u_tb, a_cat, bcat, ccat, d_skip, h0_cat, *, n_b, n_t, cdt=BF16):
    rows = n_t * n_b
    return pl.pallas_call(
        functools.partial(_s5_sample_kernel, n_b=n_b, n_t=n_t, cdt=cdt),
        grid=(S5_N_SLABS,),
        in_specs=[pl.BlockSpec((rows, S5_SLAB_CH), lambda s: (0, s)),
                  pl.BlockSpec((1, S5_SLAB_CH, 2 * S5_SLAB_ST), lambda s: (s, 0, 0)),
                  pl.BlockSpec((1, 2 * S5_SLAB_ST, S5_SLAB_CH), lambda s: (s, 0, 0)),
                  pl.BlockSpec((1, 1, 2 * S5_SLAB_ST), lambda s: (s, 0, 0)),
                  pl.BlockSpec((1, 1, S5_SLAB_CH), lambda s: (s, 0, 0)),
                  pl.BlockSpec((1, n_b, 2 * S5_SLAB_ST), lambda s: (s, 0, 0))],
        out_specs=[pl.BlockSpec((rows, S5_SLAB_CH), lambda s: (0, s)),
                   pl.BlockSpec((1, n_b, 2 * S5_SLAB_ST), lambda s: (s, 0, 0))],
        out_shape=[jax.ShapeDtypeStruct((rows, S5_WIDTH), F32),
                   jax.ShapeDtypeStruct((S5_N_SLABS, n_b, 2 * S5_SLAB_ST), F32)],
        scratch_shapes=[pltpu.VMEM((rows, 2 * S5_SLAB_ST), F32)],
        compiler_params=pltpu.CompilerParams(
            dimension_semantics=("arbitrary",), vmem_limit_bytes=VMEM_LIMIT),
        name="s5_sample",
    )(u_tb, bcat.astype(cdt), ccat.astype(cdt), a_cat.reshape(S5_N_SLABS, 1, 2 * S5_SLAB_ST),
      d_skip.reshape(S5_N_SLABS, 1, S5_SLAB_CH), h0_cat)


GLU_TM = 320
GLU_TN = 512


def _glu_kernel(yfull_ref, w_ref, ycol_ref, z_ref, o_ref, ybf_sc):
    @pl.when(pl.program_id(1) == 0)
    def _():
        ybf_sc[...] = yfull_ref[...].astype(ybf_sc.dtype)

    glu = jnp.dot(ybf_sc[...], w_ref[...], preferred_element_type=F32)
    z = z_ref[...]
    y = ycol_ref[...]
    o_ref[...] = (y * (1.0 / (1.0 + jnp.exp(-glu))) * (z * (1.0 / (1.0 + jnp.exp(-z))))).astype(o_ref.dtype)


def s5_glu_gate(y, w_glu, proj, row_off, cdt=BF16):
    m = y.shape[0]
    tm, tn = _pick_tile(m, (256, 128, 64, 32, 16)), GLU_TN
    assert row_off % tm == 0
    z_off, r_off = S5_WIDTH // tn, row_off // tm
    return pl.pallas_call(
        _glu_kernel,
        grid=(m // tm, S5_WIDTH // tn),
        in_specs=[pl.BlockSpec((tm, S5_WIDTH), lambda i, j: (i, 0)),
                  pl.BlockSpec((S5_WIDTH, tn), lambda i, j: (0, j)),
                  pl.BlockSpec((tm, tn), lambda i, j: (i, j)),
                  pl.BlockSpec((tm, tn), lambda i, j: (r_off + i, z_off + j))],
        out_specs=pl.BlockSpec((tm, tn), lambda i, j: (i, j)),
        out_shape=jax.ShapeDtypeStruct((m, S5_WIDTH), BF16),
        scratch_shapes=[pltpu.VMEM((tm, S5_WIDTH), cdt)],
        compiler_params=pltpu.CompilerParams(
            dimension_semantics=("parallel", "arbitrary"), vmem_limit_bytes=VMEM_LIMIT),
        name="s5_glu_gate",
    )(y, w_glu.astype(cdt), y, proj)


def s5_layer(proj, state_in, tables, d_skip, w_glu, *, n_batch, seq_len, n_dec, dec_len, cdt=BF16):
    a_cat, bcat, ccat = tables
    n_p = n_batch * seq_len
    n_s = n_dec * dec_len
    y_p, h_p = s5_prompt(proj, a_cat, bcat, ccat, d_skip, n_batch=n_batch, seq_len=seq_len, n_rows_out=n_p, cdt=cdt)
    u_tb = jnp.swapaxes(proj[n_p:, :S5_WIDTH].reshape(n_dec, dec_len, S5_WIDTH), 0, 1).reshape(n_s, S5_WIDTH)
    h0 = state_in.reshape(n_dec, S5_N_SLABS, S5_SLAB_ST, 2)
    h0_cat = jnp.concatenate([jnp.swapaxes(h0[..., 0], 0, 1), jnp.swapaxes(h0[..., 1], 0, 1)], axis=-1)
    y_tb, h_s = s5_sample(u_tb, a_cat, bcat, ccat, d_skip, h0_cat, n_b=n_dec, n_t=dec_len, cdt=cdt)
    y_s = jnp.swapaxes(y_tb.reshape(dec_len, n_dec, S5_WIDTH), 0, 1).reshape(n_s, S5_WIDTH)
    gated_p = s5_glu_gate(y_p, w_glu, proj, 0, cdt=cdt)
    gated_s = s5_glu_gate(y_s, w_glu, proj, n_p, cdt=cdt)
    hp = h_p.reshape(n_batch, S5_N_SLABS, 2, S5_SLAB_ST)
    st_p = jnp.stack([hp[:, :, 0], hp[:, :, 1]], axis=-1).reshape(n_batch, S5_GROUPS, S5_STATE, 2)
    hs = jnp.swapaxes(h_s, 0, 1).reshape(n_dec, S5_N_SLABS, 2, S5_SLAB_ST)
    st_s = jnp.stack([hs[:, :, 0], hs[:, :, 1]], axis=-1).reshape(n_dec, S5_GROUPS, S5_STATE, 2)
    return gated_p, gated_s, st_p, st_s


GDN_CONV_TT = 256
GDN_CONV_CW = 1024
GDN_HIST = 8
GDN_HB = 16
GDN_TT = 256
GDN_SAMPLE_ROWS = 8


def _gdn_conv_kernel(x_ref, hist_ref, w_ref, o_ref, *, rows, n_t, zero_first):
    i, j = pl.program_id(0), pl.program_id(1)
    hist = hist_ref[...]
    if zero_first:
        hist = jnp.where(i % n_t == 0, 0.0, hist)
    ext = jnp.concatenate([hist, x_ref[...]], axis=0)
    acc = ext[GDN_HIST:GDN_HIST + rows] * w_ref[GDN_CONV - 1:GDN_CONV, :]
    for s in range(1, GDN_CONV):
        acc = acc + ext[GDN_HIST - s:GDN_HIST - s + rows] * w_ref[GDN_CONV - 1 - s:GDN_CONV - s, :]
    conv = acc * (1.0 / (1.0 + jnp.exp(-acc)))
    n_qk_blocks = 2 * GDN_QK_WIDTH // GDN_CONV_CW

    @pl.when(j >= n_qk_blocks)
    def _():
        o_ref[...] = conv

    @pl.when(j < n_qk_blocks)
    def _():
        scale = jnp.where(j < GDN_QK_WIDTH // GDN_CONV_CW, GDN_DK ** -0.5, 1.0)
        for h in range(GDN_CONV_CW // GDN_DK):
            t = conv[:, h * GDN_DK:(h + 1) * GDN_DK]
            n = t * lax.rsqrt(jnp.sum(t * t, axis=-1, keepdims=True) + 1e-6)
            o_ref[:, h * GDN_DK:(h + 1) * GDN_DK] = n * scale


def gdn_conv(x, hist_src, conv_w, *, rows, n_blocks, n_t, data_map, hist_map, zero_first):
    n_out = n_blocks * rows
    return pl.pallas_call(
        functools.partial(_gdn_conv_kernel, rows=rows, n_t=n_t, zero_first=zero_first),
        grid=(n_blocks, GDN_CONV_CH // GDN_CONV_CW),
        in_specs=[pl.BlockSpec((rows, GDN_CONV_CW), lambda i, j: (data_map(i), j)),
                  pl.BlockSpec((GDN_HIST, GDN_CONV_CW), lambda i, j: (hist_map(i), j)),
                  pl.BlockSpec((GDN_CONV, GDN_CONV_CW), lambda i, j: (0, j))],
        out_specs=pl.BlockSpec((rows, GDN_CONV_CW), lambda i, j: (i, j)),
        out_shape=jax.ShapeDtypeStruct((n_out, GDN_CONV_CH), F32),
        compiler_params=pltpu.CompilerParams(
            dimension_semantics=("parallel", "parallel"), vmem_limit_bytes=VMEM_LIMIT),
        name="gdn_conv",
    )(x, hist_src, conv_w)


def _gdn_chunk_lockstep_kernel(q_ref, k_ref, v_ref, z_ref, ab_ref, alog_ref, dtb_ref, nw_ref, s0_ref, o_ref,
                               sout_ref, s_sc, *, chunk, n_inner, n_tt, valid_len):
    C = chunk
    hb, tt = pl.program_id(1), pl.program_id(2)

    @pl.when(tt == 0)
    def _():
        s_sc[...] = s0_ref[0]

    rowi = lax.broadcasted_iota(jnp.int32, (C, C), 0)
    coli = lax.broadcasted_iota(jnp.int32, (C, C), 1)
    causal = rowi >= coli
    strict = rowi > coli
    ltri = jnp.where(causal, 1.0, 0.0)
    utri = jnp.where(rowi <= coli, 1.0, 0.0)
    eye = jnp.where(rowi == coli, 1.0, 0.0)
    hi = lax.Precision.HIGHEST
    shift = (LANE - hb * GDN_HB) % LANE
    alog = pltpu.roll(jnp.broadcast_to(alog_ref[...], (8, LANE)), shift, 1)[0:1]
    dtb = pltpu.roll(jnp.broadcast_to(dtb_ref[...], (8, LANE)), shift, 1)[0:1]
    nw = nw_ref[...]
    tok_valid = lax.broadcasted_iota(jnp.int32, (C, LANE), 0) < valid_len
    dot = functools.partial(jnp.dot, preferred_element_type=F32)
    dot_nt = lambda a, b: lax.dot_general(a, b, (((1,), (1,)), ((), ())), preferred_element_type=F32)
    dot_tn = lambda a, b: lax.dot_general(a, b, (((0,), (0,)), ((), ())), preferred_element_type=F32)
    units = [(c, i) for c in range(n_inner) for i in range(GDN_HB)]
    rows = lambda c: slice(c * C, (c + 1) * C)
    qk_cols = lambda i: slice((i // 2) * GDN_DK, (i // 2 + 1) * GDN_DK)
    v_cols = lambda i: slice(i * GDN_DV, (i + 1) * GDN_DV)

    g_all, beta_all = [], []
    for c in range(n_inner):
        ab = pltpu.roll(ab_ref[rows(c), :], shift, 1)
        xa = ab + dtb
        softplus = jnp.maximum(xa, 0.0) + jnp.log1p(jnp.exp(-jnp.abs(xa)))
        g_all.append(jnp.where(tok_valid, -jnp.exp(alog) * softplus, 0.0))
        beta_all.append(jnp.where(tok_valid, 1.0 / (1.0 + jnp.exp(-ab)), 0.0))
    gam_all = [jnp.dot(ltri, g, preferred_element_type=F32, precision=hi) for g in g_all]
    gamT_all = [lax.dot_general(g, utri, (((0,), (0,)), ((), ())), preferred_element_type=F32, precision=hi)
                for g in g_all]

    qkk = [dot_nt(jnp.concatenate([q_ref[rows(c), qk_cols(i)], k_ref[rows(c), qk_cols(i)]], axis=0).astype(BF16),
                  k_ref[rows(c), qk_cols(i)].astype(BF16)) for c, i in units]
    gam_c = [jnp.broadcast_to(gam_all[c][:, i:i + 1], (C, LANE)) for c, i in units]
    beta_c = [jnp.broadcast_to(beta_all[c][:, 32 + i:33 + i], (C, LANE)) for c, i in units]
    gam_last = [jnp.broadcast_to(gam_all[c][C - 1:C, i:i + 1], (1, LANE)) for c, i in units]
    decay = [jnp.where(causal, jnp.exp(jnp.where(causal, gc[:, :C] - jnp.broadcast_to(gamT_all[c][i:i + 1, :], (C, C)),
                                                 0.0)), 0.0) for gc, (c, i) in zip(gam_c, units)]
    qk = [(x[:C] * d).astype(BF16) for x, d in zip(qkk, decay)]
    neg_a = [jnp.where(strict, -(b[:, :C] * x[C:] * d), 0.0) for b, x, d in zip(beta_c, qkk, decay)]
    p_inv = [eye + n for n in neg_a]
    m_pow = neg_a
    for _ in range(int(math.log2(C)) - 1):
        m_pow = [dot(m.astype(BF16), m.astype(BF16)) for m in m_pow]
        p_inv = [p + dot(p.astype(BF16), m.astype(BF16)) for p, m in zip(p_inv, m_pow)]
    eg = [jnp.exp(gc) for gc in gam_c]
    sol = [dot(p.astype(BF16), jnp.concatenate([b * v_ref[rows(c), v_cols(i)], (b * e) * k_ref[rows(c), qk_cols(i)]],
                                               axis=1).astype(BF16))
           for p, b, e, (c, i) in zip(p_inv, beta_c, eg, units)]
    wq = [jnp.concatenate([s[:, GDN_DV:], q_ref[rows(c), qk_cols(i)] * e], axis=0).astype(BF16)
          for s, e, (c, i) in zip(sol, eg, units)]
    k_dec = [(k_ref[rows(c), qk_cols(i)] * jnp.exp(gl - gc)).astype(BF16)
             for gl, gc, (c, i) in zip(gam_last, gam_c, units)]

    state = [s_sc[i] for i in range(GDN_HB)]
    for c in range(n_inner):
        base = c * GDN_HB
        ws = [dot(wq[base + i], state[i].astype(BF16)) for i in range(GDN_HB)]
        v_new = [(sol[base + i][:, :GDN_DV] - ws[i][:C]).astype(BF16) for i in range(GDN_HB)]
        o = [ws[i][C:] + dot(qk[base + i], v_new[i]) for i in range(GDN_HB)]
        state = [state[i] * jnp.exp(gam_last[base + i]) + dot_tn(k_dec[base + i], v_new[i]) for i in range(GDN_HB)]
        for i in range(GDN_HB):
            rms = lax.rsqrt(jnp.mean(o[i] * o[i], axis=-1, keepdims=True) + 1e-6)
            zz = z_ref[rows(c), v_cols(i)]
            o_ref[rows(c), v_cols(i)] = (o[i] * rms * nw * (zz * (1.0 / (1.0 + jnp.exp(-zz))))).astype(o_ref.dtype)
    for i in range(GDN_HB):
        s_sc[i] = state[i]

    @pl.when(tt == n_tt - 1)
    def _():
        sout_ref[0] = s_sc[...]


def gdn_chunk(conv, z, ab, a_log, dt_bias, norm_w, s0, *, n_seq, rows_per_seq, rows_per_step, chunk, valid_len,
              z_col_off, out_dtype):
    n_tt = rows_per_seq // rows_per_step
    n_inner = rows_per_step // chunk
    n_hb = GDN_V_HEADS // GDN_HB
    qw, vw = GDN_HB // 2 * GDN_DK, GDN_HB * GDN_DV
    k_off, v_off, z_off = GDN_QK_WIDTH // qw, 2 * GDN_QK_WIDTH // vw, z_col_off // vw
    row = lambda b, hb, t: b * n_tt + t
    pad_row = lambda p: jnp.pad(p.astype(F32), (0, LANE - p.shape[0])).reshape(1, LANE)
    return pl.pallas_call(
        functools.partial(_gdn_chunk_lockstep_kernel, chunk=chunk, n_inner=n_inner, n_tt=n_tt, valid_len=valid_len),
        grid=(n_seq, n_hb, n_tt),
        in_specs=[pl.BlockSpec((rows_per_step, qw), lambda b, hb, t: (row(b, hb, t), hb)),
                  pl.BlockSpec((rows_per_step, qw), lambda b, hb, t: (row(b, hb, t), k_off + hb)),
                  pl.BlockSpec((rows_per_step, vw), lambda b, hb, t: (row(b, hb, t), v_off + hb)),
                  pl.BlockSpec((rows_per_step, vw), lambda b, hb, t: (row(b, hb, t), z_off + hb)),
                  pl.BlockSpec((rows_per_step, LANE), lambda b, hb, t: (row(b, hb, t), 0)),
                  pl.BlockSpec((1, LANE), lambda b, hb, t: (0, 0)),
                  pl.BlockSpec((1, LANE), lambda b, hb, t: (0, 0)),
                  pl.BlockSpec((1, GDN_DV), lambda b, hb, t: (0, 0)),
                  pl.BlockSpec((1, GDN_HB, GDN_DK, GDN_DV), lambda b, hb, t: (b, hb, 0, 0))],
        out_specs=[pl.BlockSpec((rows_per_step, vw), lambda b, hb, t: (row(b, hb, t), hb)),
                   pl.BlockSpec((1, GDN_HB, GDN_DK, GDN_DV), lambda b, hb, t: (b, hb, 0, 0))],
        out_shape=[jax.ShapeDtypeStruct((n_seq * rows_per_seq, GDN_V_WIDTH), out_dtype),
                   jax.ShapeDtypeStruct((n_seq, GDN_V_HEADS, GDN_DK, GDN_DV), F32)],
        scratch_shapes=[pltpu.VMEM((GDN_HB, GDN_DK, GDN_DV), F32)],
        compiler_params=pltpu.CompilerParams(
            dimension_semantics=("parallel", "parallel", "arbitrary"), vmem_limit_bytes=VMEM_LIMIT),
        name="gdn_chunk",
    )(conv, conv, conv, z, ab, pad_row(a_log), pad_row(dt_bias), norm_w.astype(F32).reshape(1, GDN_DV), s0)


def gdn_layer(qkvz, ab, state_in, conv_in, conv_w, a_log, dt_bias, norm_w, *, n_batch, seq_len, n_dec, dec_len):
    n_p = n_batch * seq_len
    n_tp = seq_len // GDN_CONV_TT
    hist_per_block = GDN_CONV_TT // GDN_HIST
    conv_p = gdn_conv(qkvz, qkvz, conv_w, rows=GDN_CONV_TT, n_blocks=n_batch * n_tp, n_t=n_tp,
                      data_map=lambda i: i, hist_map=lambda i: jnp.maximum(i * hist_per_block - 1, 0),
                      zero_first=True)
    zeros_s = jnp.zeros((n_batch, GDN_V_HEADS, GDN_DK, GDN_DV), F32)
    gated_p, st_p = gdn_chunk(conv_p, qkvz, ab, a_log, dt_bias, norm_w, zeros_s, n_seq=n_batch,
                              rows_per_seq=seq_len, rows_per_step=GDN_TT, chunk=GDN_CHUNK, valid_len=GDN_CHUNK,
                              z_col_off=GDN_CONV_CH, out_dtype=BF16)
    buf_p = jnp.stack([lax.slice(qkvz, ((b + 1) * seq_len - (GDN_CONV - 1), 0), ((b + 1) * seq_len, GDN_CONV_CH))
                       for b in range(n_batch)])
    R = GDN_SAMPLE_ROWS
    x_s = qkvz[n_p:].reshape(n_dec, dec_len, -1)
    pad_t = lambda t, front: jnp.pad(t, ((0, 0), (front, R - front - t.shape[1]), (0, 0)))
    ext = jnp.concatenate([pad_t(conv_in, R - (GDN_CONV - 1)), pad_t(x_s[..., :GDN_CONV_CH], 0)], axis=1)
    ext = ext.reshape(n_dec * 2 * R, GDN_CONV_CH)
    conv_s = gdn_conv(ext, ext, conv_w, rows=R, n_blocks=n_dec, n_t=1,
                      data_map=lambda i: 2 * i + 1, hist_map=lambda i: 2 * i, zero_first=False)
    z_s = pad_t(x_s[..., GDN_CONV_CH:], 0).reshape(n_dec * R, GDN_V_WIDTH)
    ab_s = pad_t(ab[n_p:].reshape(n_dec, dec_len, LANE), 0).reshape(n_dec * R, LANE)
    gated_s, st_s = gdn_chunk(conv_s, z_s, ab_s, a_log, dt_bias, norm_w, state_in, n_seq=n_dec, rows_per_seq=R,
                              rows_per_step=R, chunk=R, valid_len=dec_len, z_col_off=0, out_dtype=F32)
    gated_s = gated_s.reshape(n_dec, R, GDN_V_WIDTH)[:, :dec_len].reshape(n_dec * dec_len, GDN_V_WIDTH)
    buf_s = jnp.concatenate([conv_in, x_s[..., :GDN_CONV_CH]], axis=1)[:, dec_len:]
    return gated_p, gated_s, st_p, st_s, buf_p, buf_s


DSA_KC = 256
INT_MIN = -2 ** 31
NEG_BIG = -1e30
G_D = N_HEADS // KV_D
BIAS_WIN = DSA_KC + Q_BLOCK


def _sortable_key(s):
    b = pltpu.bitcast(s, jnp.int32)
    return jnp.where(b < 0, b ^ jnp.int32(0x7FFFFFFF), b)


def _dsa_prompt_kernel(qT_ref, qiT_ref, wiT_ref, zT_ref, ki_ref, k_ref, vT_ref, win_ref, o_ref,
                       key_sc, mask_sc, *, topk, idx_bits, cdt):
    qb = pl.program_id(1)
    t0 = qb * Q_BLOCK
    nch = (qb + 2) // 2
    t_idx = t0 + lax.broadcasted_iota(jnp.int32, (1, Q_BLOCK), 1)
    row_iota = lax.broadcasted_iota(jnp.int32, (DSA_KC, Q_BLOCK), 0)

    def score_chunk(c, carry):
        kic = ki_ref[0, c].astype(cdt)
        acc = jnp.zeros((DSA_KC, Q_BLOCK), F32)
        for hp in range(IDX_HEADS // 2):
            rhs = jnp.concatenate([qiT_ref[(2 * hp) * IDX_DIM:(2 * hp + 1) * IDX_DIM, :],
                                   qiT_ref[(2 * hp + 1) * IDX_DIM:(2 * hp + 2) * IDX_DIM, :]], axis=1)
            s = jnp.dot(kic, rhs, preferred_element_type=F32) * (IDX_DIM ** -0.5)
            s = jnp.maximum(s, 0.0)
            w0 = wiT_ref[2 * hp:2 * hp + 1, :] * (IDX_HEADS ** -0.5)
            w1 = wiT_ref[2 * hp + 1:2 * hp + 2, :] * (IDX_HEADS ** -0.5)
            acc = acc + s[:, :Q_BLOCK] * w0 + s[:, Q_BLOCK:] * w1
        s_idx = c * DSA_KC + row_iota
        key_sc[c] = jnp.where(s_idx <= t_idx, _sortable_key(acc), INT_MIN)
        return carry

    lax.fori_loop(0, nch, score_chunk, 0)

    def count(pred):
        def body(c, acc):
            hit = pred(key_sc[c], c * DSA_KC + row_iota)
            return acc + hit.reshape(DSA_KC // 8, 8, Q_BLOCK).sum(axis=0)
        acc = lax.fori_loop(0, nch, body, jnp.zeros((8, Q_BLOCK), jnp.int32))
        return jnp.sum(acc, axis=0, keepdims=True)

    c_nonneg = count(lambda k, s: jnp.where(k >= 0, 1, 0))
    thr = jnp.where(c_nonneg >= topk, 0, INT_MIN).astype(jnp.int32)

    def thr_bit(i, thr):
        cand = thr + jnp.left_shift(jnp.int32(1), 30 - i)
        return jnp.where(count(lambda k, s: jnp.where(k >= cand, 1, 0)) >= topk, cand, thr)

    thr = lax.fori_loop(0, 31, thr_bit, thr)
    need = topk - count(lambda k, s: jnp.where(k > thr, 1, 0))

    def lim_bit(i, lim):
        cand = lim + jnp.left_shift(jnp.int32(1), idx_bits - 1 - i)
        c = count(lambda k, s: jnp.where(k == thr, jnp.where(s < cand, 1, 0), 0))
        return jnp.where(c <= need, cand, lim)

    n_ties = count(lambda k, s: jnp.where(k == thr, 1, 0))
    settled = jnp.min(jnp.where(thr == INT_MIN, 1, jnp.where(n_ties == need, 1, 0))) == 1
    lim = lax.cond(settled,
                   lambda: jnp.full((1, Q_BLOCK), 1 << idx_bits, jnp.int32),
                   lambda: lax.fori_loop(0, idx_bits, lim_bit, jnp.zeros((1, Q_BLOCK), jnp.int32)))

    def mask_chunk(c, carry):
        k = key_sc[c]
        s_idx = c * DSA_KC + row_iota
        tie = jnp.where(k == thr, jnp.where(s_idx < lim, 0.0, NEG_BIG), NEG_BIG)
        m = jnp.where(k > thr, 0.0, tie)
        mask_sc[c] = jnp.where(k == INT_MIN, NEG_BIG, m)
        return carry

    lax.fori_loop(0, nch, mask_chunk, 0)

    n_cols = G_D * Q_BLOCK
    half = DSA_KC // 2

    def head_q(j):
        qj = jnp.concatenate([qT_ref[(G_D * j + g) * HEAD_DIM:(G_D * j + g + 1) * HEAD_DIM, :]
                              for g in range(G_D)], axis=1)
        return (qj.astype(F32) * (HEAD_DIM ** -0.5)).astype(cdt)

    for jp in range(KV_D // 2):
        q0, q1 = head_q(2 * jp), head_q(2 * jp + 1)
        zq = jnp.zeros_like(q0)
        rhs = jnp.concatenate([jnp.concatenate([q0, zq], axis=0), jnp.concatenate([zq, q1], axis=0)], axis=1)

        def chunk_body(c, carry, jp=jp, rhs=rhs):
            m, l, acc0, acc1 = carry
            kc = k_ref[0, c, :, jp * 2 * HEAD_DIM:(jp + 1) * 2 * HEAD_DIM].astype(cdt)
            s = jnp.dot(kc, rhs, preferred_element_type=F32)
            wt = win_ref[qb - 2 * c]
            madd = mask_sc[c]
            parts = []
            for hh in range(2 * G_D):
                h = 2 * G_D * jp + hh
                tiles = []
                for u in range(2):
                    lo = (1 - u) * half
                    r = jnp.broadcast_to(wt[h:h + 1, lo:lo + 2 * half], (half, 2 * half))
                    tiles.append(pltpu.roll(r, 0, 1, stride=1, stride_axis=0)[:, half:])
                parts.append(s[:, hh * Q_BLOCK:(hh + 1) * Q_BLOCK] + (jnp.concatenate(tiles, axis=0) + madd))
            s = jnp.concatenate(parts, axis=1)
            m_new = jnp.maximum(m, jnp.max(s, axis=0, keepdims=True))
            alpha = jnp.exp(m - m_new)
            p = jnp.exp(s - m_new)
            l = l * alpha + jnp.sum(p, axis=0, keepdims=True)
            p = p.astype(cdt)
            v0 = vT_ref[0, c, (2 * jp) * HEAD_DIM:(2 * jp + 1) * HEAD_DIM, :]
            v1 = vT_ref[0, c, (2 * jp + 1) * HEAD_DIM:(2 * jp + 2) * HEAD_DIM, :]
            acc0 = acc0 * alpha[:, :n_cols] + jnp.dot(v0, p[:, :n_cols], preferred_element_type=F32)
            acc1 = acc1 * alpha[:, n_cols:] + jnp.dot(v1, p[:, n_cols:], preferred_element_type=F32)
            return m_new, l, acc0, acc1

        init = (jnp.full((1, 2 * n_cols), NEG_BIG, F32), jnp.zeros((1, 2 * n_cols), F32),
                jnp.zeros((HEAD_DIM, n_cols), F32), jnp.zeros((HEAD_DIM, n_cols), F32))
        m, l, acc0, acc1 = lax.fori_loop(0, nch, chunk_body, init)
        inv = 1.0 / l
        for jj, acc in enumerate((acc0, acc1)):
            o = acc * inv[:, jj * n_cols:(jj + 1) * n_cols]
            for g in range(G_D):
                r0 = (G_D * (2 * jp + jj) + g) * HEAD_DIM
                z = zT_ref[r0:r0 + HEAD_DIM, :]
                gate = z * (1.0 / (1.0 + jnp.exp(-z)))
                o_ref[r0:r0 + HEAD_DIM, :] = (o[:, g * Q_BLOCK:(g + 1) * Q_BLOCK] * gate).astype(o_ref.dtype)


def dsa_bias_windows(rel_bias, seq_len):
    o = jnp.arange(seq_len // Q_BLOCK)[:, None]
    m = jnp.arange(BIAS_WIN)[None, :]
    d = jnp.maximum(o * Q_BLOCK + m - DSA_KC, 0)
    return jnp.moveaxis(rel_bias[rel_bucket(d)].astype(F32), -1, 1)


def dsa_prompt_attend(qqiT, wiT, zT, ki4, k4, v4T, win, *, n_batch, seq_len, cdt=BF16):
    nqb = seq_len // Q_BLOCK
    nc = seq_len // DSA_KC
    topk = min(TOPK_MAX, seq_len // 4)
    idx_bits = int(math.log2(seq_len)) + 1
    tok = lambda b, q: (0, b * nqb + q)
    per_batch = lambda b, q: (b, 0, 0, 0)
    return pl.pallas_call(
        functools.partial(_dsa_prompt_kernel, topk=topk, idx_bits=idx_bits, cdt=cdt),
        grid=(n_batch, nqb),
        in_specs=[pl.BlockSpec((ATT_WIDTH, Q_BLOCK), tok),
                  pl.BlockSpec((IDX_HEADS * IDX_DIM, Q_BLOCK), lambda b, q: (1, b * nqb + q)),
                  pl.BlockSpec((IDX_HEADS, Q_BLOCK), tok),
                  pl.BlockSpec((ATT_WIDTH, Q_BLOCK), tok),
                  pl.BlockSpec((1, nc, DSA_KC, IDX_DIM), per_batch),
                  pl.BlockSpec((1, nc, DSA_KC, D_KV), per_batch),
                  pl.BlockSpec((1, nc, D_KV, DSA_KC), per_batch),
                  pl.BlockSpec((nqb, N_HEADS, BIAS_WIN), lambda b, q: (0, 0, 0))],
        out_specs=pl.BlockSpec((ATT_WIDTH, Q_BLOCK), tok),
        out_shape=jax.ShapeDtypeStruct((ATT_WIDTH, n_batch * seq_len), BF16),
        scratch_shapes=[pltpu.VMEM((nc, DSA_KC, Q_BLOCK), jnp.int32),
                        pltpu.VMEM((nc, DSA_KC, Q_BLOCK), F32)],
        compiler_params=pltpu.CompilerParams(
            dimension_semantics=("parallel", "arbitrary"), vmem_limit_bytes=VMEM_LIMIT),
        name="dsa_prompt_attend",
    )(qqiT, qqiT, wiT, zT, ki4, k4, v4T, win)


DSS_NP1 = 16
DSS_NP2 = 8
DSS_TP = 8
DSS_GROWS = G_D * DSS_TP
T5_LAST_BUCKET_DIST = 1600


def _dsa_sample_select_kernel(pt_ref, qi_ref, wb_ref, kinew_ref, *rest, n_pages, n_new, topk, idx_bits, cdt):
    del pt_ref
    page_refs, mask_ref, key_sc = rest[:DSS_NP1], rest[DSS_NP1], rest[DSS_NP1 + 1]
    s = pl.program_id(1)
    lane = lax.broadcasted_iota(jnp.int32, (DSS_TP, PAGE_SIZE), 1)
    trow = lax.broadcasted_iota(jnp.int32, (DSS_TP, PAGE_SIZE), 0)
    qi = qi_ref[0]
    wb = wb_ref[0]

    def page_keys(kp):
        sc = lax.dot_general(qi, kp.astype(cdt), (((1,), (1,)), ((), ())),
                             preferred_element_type=F32) * (IDX_DIM ** -0.5)
        sc = jnp.maximum(sc, 0.0) * wb
        return _sortable_key(sc.reshape(IDX_HEADS, DSS_TP, PAGE_SIZE).sum(axis=0))

    for i in range(DSS_NP1):
        key_sc[s * DSS_NP1 + i] = page_keys(page_refs[i][0, 0])

    @pl.when(s == 0)
    def _():
        kn = page_keys(kinew_ref[0])
        key_sc[n_pages] = jnp.where(lane < n_new, jnp.where(lane <= trow, kn, INT_MIN), INT_MIN)

    @pl.when(s == n_pages // DSS_NP1 - 1)
    def _():
        all_shape = (n_pages + 1, DSS_TP, PAGE_SIZE)

        def count(pred):
            key_idx = (lax.broadcasted_iota(jnp.int32, all_shape, 0) * PAGE_SIZE
                       + lax.broadcasted_iota(jnp.int32, all_shape, 2))
            acc = pred(key_sc[...], key_idx).sum(axis=0)
            return jnp.broadcast_to(jnp.sum(acc, axis=1, keepdims=True), (DSS_TP, PAGE_SIZE))

        c_nonneg = count(lambda k, i: jnp.where(k >= 0, 1, 0))
        thr = jnp.where(c_nonneg >= topk, 0, INT_MIN).astype(jnp.int32)

        def thr_bit(b, thr):
            cand = thr + jnp.left_shift(jnp.int32(1), 30 - b)
            return jnp.where(count(lambda k, i: jnp.where(k >= cand, 1, 0)) >= topk, cand, thr)

        thr = lax.fori_loop(0, 31, thr_bit, thr)
        need = topk - count(lambda k, i: jnp.where(k > thr, 1, 0))

        def lim_bit(b, lim):
            cand = lim + jnp.left_shift(jnp.int32(1), idx_bits - 1 - b)
            c = count(lambda k, i: jnp.where(k == thr, jnp.where(i < cand, 1, 0), 0))
            return jnp.where(c <= need, cand, lim)

        n_ties = count(lambda k, i: jnp.where(k == thr, 1, 0))
        settled = jnp.min(jnp.where(thr == INT_MIN, 1, jnp.where(n_ties == need, 1, 0))) == 1
        lim = lax.cond(settled,
                       lambda: jnp.full((DSS_TP, PAGE_SIZE), 1 << idx_bits, jnp.int32),
                       lambda: lax.fori_loop(0, idx_bits, lim_bit, jnp.zeros((DSS_TP, PAGE_SIZE), jnp.int32)))

        k = key_sc[...]
        key_idx = (lax.broadcasted_iota(jnp.int32, all_shape, 0) * PAGE_SIZE
                   + lax.broadcasted_iota(jnp.int32, all_shape, 2))
        tie = jnp.where(k == thr, jnp.where(key_idx < lim, 0.0, NEG_BIG), NEG_BIG)
        mask_ref[0] = jnp.where(k == INT_MIN, NEG_BIG, jnp.where(k > thr, 0.0, tie))


def _dsa_sample_attend_kernel(pt_ref, q_ref, z_ref, mask_ref, masknew_ref, bnear_ref, bfar_ref, kvnew_ref, *rest,
                              n_pages, n_far, cdt):
    del pt_ref
    page_refs, o_ref = rest[:DSS_NP2], rest[DSS_NP2]
    m_sc, l_sc, acc_sc = rest[DSS_NP2 + 1:]
    s = pl.program_id(1)
    heads = range(KV_D)
    rep = DSS_GROWS // DSS_TP

    def attend(pages, masks, page_ids):
        n = len(pages)
        madd = jnp.concatenate([jnp.concatenate([mk] * rep, axis=0) for mk in masks], axis=1)
        kT = [jnp.concatenate([pg(0, h) for pg in pages], axis=1).astype(cdt) for h in heads]
        vT = [jnp.concatenate([pg(1, h) for pg in pages], axis=1).astype(cdt) for h in heads]
        logits = [jnp.dot(q_ref[0, h], kT[h], preferred_element_type=F32) for h in heads]
        bias = [jnp.concatenate([jnp.where(pid >= n_far, bnear_ref[jnp.maximum(pid - n_far, 0), h], bfar_ref[h])
                                 for pid in page_ids], axis=1) for h in heads]
        logits = [lg + (b + madd) for lg, b in zip(logits, bias)]
        m_old = [m_sc[h] for h in heads]
        m_new = [jnp.maximum(mo, jnp.broadcast_to(jnp.max(lg, axis=1, keepdims=True), mo.shape))
                 for mo, lg in zip(m_old, logits)]
        alpha = [jnp.exp(mo - mn) for mo, mn in zip(m_old, m_new)]
        p = [jnp.exp(lg - jnp.concatenate([mn] * n, axis=1)) for lg, mn in zip(logits, m_new)]
        pv = [lax.dot_general(ph.astype(cdt), vT[h], (((1,), (1,)), ((), ())), preferred_element_type=F32)
              for h, ph in zip(heads, p)]
        for h in heads:
            l_sc[h] = l_sc[h] * alpha[h] + jnp.broadcast_to(jnp.sum(p[h], axis=1, keepdims=True), alpha[h].shape)
            acc_sc[h] = acc_sc[h] * alpha[h][:, :HEAD_DIM] + pv[h]
            m_sc[h] = m_new[h]

    @pl.when(s == 0)
    def _():
        m_sc[...] = jnp.full_like(m_sc, NEG_BIG)
        l_sc[...] = jnp.zeros_like(l_sc)
        acc_sc[...] = jnp.zeros_like(acc_sc)
        attend([lambda c, h: kvnew_ref[0, c, h]], [masknew_ref[0, 0]], [n_pages])

    attend([(lambda c, h, r=r: r[0, 0, c, h]) for r in page_refs], [mask_ref[0, i] for i in range(DSS_NP2)],
           [s * DSS_NP2 + i for i in range(DSS_NP2)])

    @pl.when(s == n_pages // DSS_NP2 - 1)
    def _():
        for h in heads:
            z = z_ref[0, h]
            o_ref[0, h] = acc_sc[h] * (1.0 / l_sc[h][:, :HEAD_DIM]) * (z * (1.0 / (1.0 + jnp.exp(-z))))


def dsa_sample(q_s, z_s, qi_s, wi_s, ki_s, kv_s, kv_pool, kidx_pool, layer, page_table, rel_bias, cdt=BF16):
    n_dec, n_pages = page_table.shape
    dec_len = q_s.shape[0] // n_dec
    past = n_pages * PAGE_SIZE
    total = past + dec_len
    topk = min(TOPK_MAX, total // 4)
    idx_bits = int(math.log2(total)) + 1
    pad_t = DSS_TP - dec_len
    n_pairs = KV_D // 2
    eye2 = jnp.eye(2, dtype=F32)

    qi = jnp.pad(jnp.swapaxes(qi_s.reshape(n_dec, dec_len, IDX_HEADS, IDX_DIM), 1, 2), ((0, 0), (0, 0), (0, pad_t), (0, 0)))
    qi = qi.reshape(n_dec, IDX_HEADS * DSS_TP, IDX_DIM).astype(cdt)
    wb = jnp.pad(jnp.swapaxes(wi_s.reshape(n_dec, dec_len, IDX_HEADS), 1, 2) * (IDX_HEADS ** -0.5), ((0, 0), (0, 0), (0, pad_t)))
    wb = jnp.broadcast_to(wb.reshape(n_dec, IDX_HEADS * DSS_TP, 1), (n_dec, IDX_HEADS * DSS_TP, PAGE_SIZE))
    ki_new = jnp.pad(ki_s.reshape(n_dec, dec_len, IDX_DIM), ((0, 0), (0, PAGE_SIZE - dec_len), (0, 0)))
    kidx4 = kidx_pool.reshape(kidx_pool.shape[0], kidx_pool.shape[1], PAGE_SIZE, IDX_DIM)
    page_spec = lambda np_, i, width: pl.BlockSpec(
        (1, 1, PAGE_SIZE, width), lambda b, s, pt: (layer, pt[b, s * np_ + i], 0, 0))
    per_b3 = lambda b, s, pt: (b, 0, 0)
    mask = pl.pallas_call(
        functools.partial(_dsa_sample_select_kernel, n_pages=n_pages, n_new=dec_len, topk=topk, idx_bits=idx_bits,
                          cdt=cdt),
        grid_spec=pltpu.PrefetchScalarGridSpec(
            num_scalar_prefetch=1, grid=(n_dec, n_pages // DSS_NP1),
            in_specs=[pl.BlockSpec((1, IDX_HEADS * DSS_TP, IDX_DIM), per_b3),
                      pl.BlockSpec((1, IDX_HEADS * DSS_TP, PAGE_SIZE), per_b3),
                      pl.BlockSpec((1, PAGE_SIZE, IDX_DIM), per_b3)]
                     + [page_spec(DSS_NP1, i, IDX_DIM) for i in range(DSS_NP1)],
            out_specs=pl.BlockSpec((1, n_pages + 1, DSS_TP, PAGE_SIZE), lambda b, s, pt: (b, 0, 0, 0)),
            scratch_shapes=[pltpu.VMEM((n_pages + 1, DSS_TP, PAGE_SIZE), jnp.int32)]),
        out_shape=jax.ShapeDtypeStruct((n_dec, n_pages + 1, DSS_TP, PAGE_SIZE), F32),
        compiler_params=pltpu.CompilerParams(
            dimension_semantics=("parallel", "arbitrary"), vmem_limit_bytes=VMEM_LIMIT),
        name="dsa_sample_select",
    )(page_table, qi, wb, ki_new, *([kidx4] * DSS_NP1))

    def head_rows(t, scale):
        t = t.reshape(n_dec, dec_len, KV_D, G_D, HEAD_DIM).transpose(0, 2, 3, 1, 4) * scale
        return jnp.pad(t, ((0, 0),) * 3 + ((0, pad_t), (0, 0))).reshape(n_dec, KV_D, DSS_GROWS, HEAD_DIM)

    q_hr = head_rows(q_s, HEAD_DIM ** -0.5).astype(cdt)
    z_hr = head_rows(z_s, 1.0)
    n_far = max(0, min(n_pages, (past - (PAGE_SIZE - 1) - T5_LAST_BUCKET_DIST) // PAGE_SIZE + 1))
    near_pages = jnp.arange(n_far, n_pages + 1)
    dist = (past + jnp.arange(DSS_TP)[None, :, None]
            - (near_pages[:, None, None] * PAGE_SIZE + jnp.arange(PAGE_SIZE)[None, None, :]))
    b_near = rel_bias[rel_bucket(dist)].astype(F32)
    b_near = b_near.transpose(0, 3, 1, 2).reshape(n_pages + 1 - n_far, KV_D, DSS_GROWS, PAGE_SIZE)
    b_far = jnp.broadcast_to(rel_bias[N_BUCKETS - 1].astype(F32)[:, None, None], (N_HEADS, DSS_TP, PAGE_SIZE))
    b_far = b_far.reshape(KV_D, DSS_GROWS, PAGE_SIZE)
    kv_new = jnp.pad(kv_s.reshape(n_dec, dec_len, 2, KV_D, HEAD_DIM).transpose(0, 2, 3, 4, 1),
                     ((0, 0),) * 4 + ((0, PAGE_SIZE - dec_len),))
    kv_t = kv_pool.reshape(kv_pool.shape[0], kv_pool.shape[1], PAGE_SIZE, 2, KV_D, HEAD_DIM).transpose(0, 1, 3, 4, 5, 2)
    kv_page = lambda i: pl.BlockSpec((1, 1, 2, KV_D, HEAD_DIM, PAGE_SIZE),
                                     lambda b, s, pt: (layer, pt[b, s * DSS_NP2 + i], 0, 0, 0, 0))
    per_b4 = lambda b, s, pt: (b, 0, 0, 0)
    o = pl.pallas_call(
        functools.partial(_dsa_sample_attend_kernel, n_pages=n_pages, n_far=n_far, cdt=cdt),
        grid_spec=pltpu.PrefetchScalarGridSpec(
            num_scalar_prefetch=1, grid=(n_dec, n_pages // DSS_NP2),
            in_specs=[pl.BlockSpec((1, KV_D, DSS_GROWS, HEAD_DIM), per_b4),
                      pl.BlockSpec((1, KV_D, DSS_GROWS, HEAD_DIM), per_b4),
                      pl.BlockSpec((1, DSS_NP2, DSS_TP, PAGE_SIZE), lambda b, s, pt: (b, s, 0, 0)),
                      pl.BlockSpec((1, 1, DSS_TP, PAGE_SIZE), lambda b, s, pt: (b, n_pages, 0, 0)),
                      pl.BlockSpec(b_near.shape, lambda b, s, pt: (0, 0, 0, 0)),
                      pl.BlockSpec(b_far.shape, lambda b, s, pt: (0, 0, 0)),
                      pl.BlockSpec((1, 2, KV_D, HEAD_DIM, PAGE_SIZE), lambda b, s, pt: (b, 0, 0, 0, 0))]
                     + [kv_page(i) for i in range(DSS_NP2)],
            out_specs=pl.BlockSpec((1, KV_D, DSS_GROWS, HEAD_DIM), per_b4),
            scratch_shapes=[pltpu.VMEM((KV_D, DSS_GROWS, LANE), F32), pltpu.VMEM((KV_D, DSS_GROWS, LANE), F32),
                            pltpu.VMEM((KV_D, DSS_GROWS, HEAD_DIM), F32)]),
        out_shape=jax.ShapeDtypeStruct((n_dec, KV_D, DSS_GROWS, HEAD_DIM), F32),
        compiler_params=pltpu.CompilerParams(
            dimension_semantics=("parallel", "arbitrary"), vmem_limit_bytes=VMEM_LIMIT),
        name="dsa_sample_attend",
    )(page_table, q_hr, z_hr, mask, mask, b_near, b_far, kv_new, *([kv_t] * DSS_NP2))
    o = o.reshape(n_dec, KV_D, G_D, DSS_TP, HEAD_DIM)[:, :, :, :dec_len]
    return o.transpose(0, 3, 1, 2, 4).reshape(n_dec * dec_len, ATT_WIDTH)


def _pad_cols(w, n):
    return jnp.pad(w, ((0, 0), (0, n - w.shape[1])))


def kernel(x_prompt, x_sample, cache_a_kv, state_s5, state_gdn, state_gdn_conv, cache_d_kv, cache_d_kidx,
           page_table, p_prompt, p_sample, rel_bias, ln_g, ln_b, ple_gate_w, ple_w,
           a_w_in, a_sinks, a_w_out,
           s5_w_in, s5_a_re, s5_a_im, s5_b_re, s5_b_im, s5_c_re, s5_c_im, s5_d, s5_log_dt, s5_w_glu, s5_w_out,
           gdn_w_in, gdn_conv_w, gdn_a_log, gdn_dt_bias, gdn_norm_w, gdn_w_out,
           dsa_w_in, dsa_w_out):
    x = join_tokens(x_prompt, x_sample)
    x_bf = x.astype(BF16)
    outs = {}

    def post(i, x, h):
        p_bf = join_tokens(p_prompt[i], p_sample[i]).astype(BF16)
        return post_norm_ple(x, h, p_bf, ln_g[i], ln_b[i], ple_gate_w[i].astype(BF16), ple_w[i].astype(BF16))

    w_in = a_w_in[0]
    c_k, c_v, c_z = ATT_WIDTH, ATT_WIDTH + A_KV, ATT_WIDTH + 2 * A_KV
    w_q, w_z = w_in[:, :c_k], w_in[:, c_z:]
    w_out_bf = a_w_out[0].astype(BF16)
    kv_nat = matmul(x_bf, w_in[:, c_k:c_z].astype(BF16))
    xT_bf = x_bf[:N_PROMPT_TOK].T
    qvT = matmul(jnp.concatenate([w_q, w_in[:, c_v:c_z]], axis=1).T.astype(BF16), xT_bf, out_dtype=BF16)
    zT = matmul(w_z.T.astype(BF16), xT_bf)
    bias_p, sink_p, bias_s, sink_s = swa_tables(rel_bias, a_sinks[0], DEC_SEQ)
    h_p = matmul_ta(swa_prompt(qvT, zT, kv_nat, bias_p, sink_p, n_batch=BATCH, seq_len=SEQ), w_out_bf)
    qz_s = matmul(x_bf[N_PROMPT_TOK:], jnp.concatenate([w_q, w_z], axis=1).astype(BF16))
    gs, outs['a_s'] = swa_sample(qz_s[:, :ATT_WIDTH], qz_s[:, ATT_WIDTH:], kv_nat[N_PROMPT_TOK:], cache_a_kv[0],
                                 bias_s, sink_s)
    outs['a_p'] = kv_nat[:N_PROMPT_TOK].reshape(BATCH, SEQ, 2, KV_A, HEAD_DIM)[:, SEQ - WINDOW:]
    x, x_bf = post(0, x, jnp.concatenate([h_p, matmul(gs.astype(BF16), w_out_bf)], axis=0))

    proj = matmul(x_bf, s5_w_in[0].astype(BF16))
    tables = s5_tables(s5_a_re[0], s5_a_im[0], s5_b_re[0], s5_b_im[0], s5_c_re[0], s5_c_im[0], s5_log_dt[0])
    gp, gs, outs['s5_p'], outs['s5_s'] = s5_layer(proj, state_s5[0], tables, s5_d[0], s5_w_glu[0],
                                                  n_batch=BATCH, seq_len=SEQ, n_dec=DEC_BATCH, dec_len=DEC_SEQ)
    w_out_bf = s5_w_out[0].astype(BF16)
    x, x_bf = post(1, x, jnp.concatenate([matmul(gp, w_out_bf), matmul(gs, w_out_bf)], axis=0))

    w_in = gdn_w_in[0]
    c_gz = GDN_CONV_CH + GDN_V_WIDTH
    qkvz = matmul(x_bf, w_in[:, :c_gz].astype(BF16))
    ab = matmul(x_bf, _pad_cols(w_in[:, c_gz:], LANE).astype(BF16))
    gp, gs, outs['gd_p'], outs['gd_s'], outs['gc_p'], outs['gc_s'] = gdn_layer(
        qkvz, ab, state_gdn[0], state_gdn_conv[0], gdn_conv_w[0], gdn_a_log[0], gdn_dt_bias[0], gdn_norm_w[0],
        n_batch=BATCH, seq_len=SEQ, n_dec=DEC_BATCH, dec_len=DEC_SEQ)
    w_out_bf = gdn_w_out[0].astype(BF16)
    x, x_bf = post(2, x, jnp.concatenate([matmul(gp, w_out_bf), matmul(gs.astype(BF16), w_out_bf)], axis=0))

    w_in = dsa_w_in[0]
    c_z = 2 * ATT_WIDTH + 2 * D_KV
    c_qi = c_z + IDX_HEADS * IDX_DIM
    c_kv = ATT_WIDTH + 2 * D_KV
    w_q, w_kv, w_z, w_qi = w_in[:, :ATT_WIDTH], w_in[:, ATT_WIDTH:c_kv], w_in[:, c_kv:c_z], w_in[:, c_z:c_qi]
    w_out_bf = dsa_w_out[0].astype(BF16)
    kv_nat = matmul(x_bf, w_kv.astype(BF16))
    kiw = matmul(x_bf, _pad_cols(w_in[:, c_qi:], 2 * LANE).astype(BF16))
    xT_bf = x_bf[:N_PROMPT_TOK].T
    qqiT = matmul(jnp.concatenate([w_q, w_qi], axis=1).T.astype(BF16), xT_bf, out_dtype=BF16)
    zT = matmul(w_z.T.astype(BF16), xT_bf)
    wiT = matmul(w_in[:, c_qi + IDX_DIM:].T.astype(BF16), xT_bf)
    nc = SEQ // DSA_KC
    kv_p = kv_nat[:N_PROMPT_TOK]
    v4T = jnp.swapaxes(kv_p[:, D_KV:].astype(BF16).reshape(BATCH, nc, DSA_KC, D_KV), 2, 3)
    gT = dsa_prompt_attend(qqiT, wiT, zT, kiw[:N_PROMPT_TOK].reshape(BATCH, nc, DSA_KC, 2 * LANE),
                           kv_p.reshape(BATCH, nc, DSA_KC, 2 * D_KV), v4T, dsa_bias_windows(rel_bias, SEQ),
                           n_batch=BATCH, seq_len=SEQ)
    h_p = matmul_ta(gT, w_out_bf)
    x_s = x_bf[N_PROMPT_TOK:]
    qzqi_s = matmul(x_s, jnp.concatenate([w_q, w_z, w_qi], axis=1).astype(BF16))
    kiw_s = kiw[N_PROMPT_TOK:]
    gs = dsa_sample(qzqi_s[:, :ATT_WIDTH], qzqi_s[:, ATT_WIDTH:2 * ATT_WIDTH], qzqi_s[:, 2 * ATT_WIDTH:],
                    kiw_s[:, IDX_DIM:IDX_DIM + IDX_HEADS], kiw_s[:, :IDX_DIM], kv_nat[N_PROMPT_TOK:],
                    cache_d_kv, cache_d_kidx, 0, page_table, rel_bias)
    h_s = matmul(gs.astype(BF16), w_out_bf)
    outs['dkv_p'] = kv_p.reshape(BATCH, SEQ, 2, KV_D, HEAD_DIM)
    outs['dkv_s'] = kv_nat[N_PROMPT_TOK:].reshape(DEC_BATCH, DEC_SEQ, 2, KV_D, HEAD_DIM)
    outs['dki_p'] = kiw[:N_PROMPT_TOK, :IDX_DIM].reshape(BATCH, SEQ, IDX_DIM)
    outs['dki_s'] = kiw_s[:, :IDX_DIM].reshape(DEC_BATCH, DEC_SEQ, IDX_DIM)
    x, x_bf = post(3, x, jnp.concatenate([h_p, h_s], axis=0))

    yp, ys = split_tokens(x)
    st = lambda name: outs[name][None]
    return (yp, ys, st('a_p'), st('a_s'), st('s5_p'), st('s5_s'), st('gd_p'), st('gd_s'),
            st('gc_p'), st('gc_s'), st('dkv_p'), st('dkv_s'), st('dki_p'), st('dki_s'))
```

```python
import functools
import math

import jax
import jax.numpy as jnp
from jax import lax
from jax.experimental import pallas as pl
from jax.experimental.pallas import tpu as pltpu

D_MODEL = 2048
BATCH = 4
SEQ = 2048
DEPTH = 4
DEC_BATCH = 32
DEC_SEQ = 4
PAGE_SIZE = 128
N_MIXERS = 4
PLE_DIM = 256
ALPHA = (2 * DEPTH) ** 0.25
LN_EPS = 1e-5
N_BUCKETS = 32
REL_MAX_DIST = 2048
N_HEADS = 32
HEAD_DIM = 64
ATT_WIDTH = N_HEADS * HEAD_DIM
WINDOW = 128
KV_A = 4
A_KV = KV_A * HEAD_DIM
KV_D = 8
D_KV = KV_D * HEAD_DIM
IDX_HEADS = 16
IDX_DIM = 128
TOPK_MAX = 256
Q_BLOCK = 128
S5_WIDTH = D_MODEL
S5_GROUP = 16
S5_GROUPS = S5_WIDTH // S5_GROUP
S5_STATE = 64
GDN_QK_HEADS = 16
GDN_V_HEADS = 32
GDN_DK = 128
GDN_DV = 128
GDN_CONV = 4
GDN_CHUNK = 64
GDN_QK_WIDTH = GDN_QK_HEADS * GDN_DK
GDN_V_WIDTH = GDN_V_HEADS * GDN_DV
GDN_CONV_CH = 2 * GDN_QK_WIDTH + GDN_V_WIDTH

F32 = jnp.float32
BF16 = jnp.bfloat16

N_PROMPT_TOK = BATCH * SEQ
N_SAMPLE_TOK = DEC_BATCH * DEC_SEQ
N_TOK = N_PROMPT_TOK + N_SAMPLE_TOK

V7X_VMEM_BYTES = 64 * 1024 * 1024
VMEM_LIMIT = 48 * 1024 * 1024
LANE = 128


def _mm_kernel(x_ref, w_ref, o_ref):
    o_ref[...] = jnp.dot(x_ref[...], w_ref[...].astype(BF16), preferred_element_type=F32).astype(o_ref.dtype)


def _pick_tile(n, prefs, also=0):
    for t in prefs:
        if n % t == 0 and also % t == 0:
            return t
    raise ValueError(f"no tile for {n} (offset {also})")


def matmul(x, w, col0=0, n=None, out_dtype=F32):
    m, k = x.shape
    n = w.shape[1] - col0 if n is None else n
    tm = _pick_tile(m, (640, 512, 320, 256, 128, 64, 32, 16, 8))
    tn = _pick_tile(n, (512, 384, 256, 128), col0)
    c0 = col0 // tn
    return pl.pallas_call(
        _mm_kernel,
        grid=(m // tm, n // tn),
        in_specs=[pl.BlockSpec((tm, k), lambda i, j: (i, 0)),
                  pl.BlockSpec((k, tn), lambda i, j: (0, c0 + j))],
        out_specs=pl.BlockSpec((tm, tn), lambda i, j: (i, j)),
        out_shape=jax.ShapeDtypeStruct((m, n), out_dtype),
        compiler_params=pltpu.CompilerParams(
            dimension_semantics=("parallel", "parallel"), vmem_limit_bytes=VMEM_LIMIT),
        name="proj_matmul",
    )(x, w)


def _mm_wt_kernel(w_ref, xt_ref, o_ref, wt_sc):
    @pl.when(pl.program_id(1) == 0)
    def _():
        wt_sc[...] = w_ref[...].astype(F32).T.astype(BF16)

    o_ref[...] = jnp.dot(wt_sc[...], xt_ref[...], preferred_element_type=F32).astype(o_ref.dtype)


def matmul_wt(w, xt, col0, n, out_dtype=F32):
    k, m = xt.shape
    tr = _pick_tile(n, (512, 256, 128), col0)
    tt = _pick_tile(m, (512, 256, 128))
    r0 = col0 // tr
    return pl.pallas_call(
        _mm_wt_kernel,
        grid=(n // tr, m // tt),
        in_specs=[pl.BlockSpec((k, tr), lambda i, j: (0, r0 + i)),
                  pl.BlockSpec((k, tt), lambda i, j: (0, j))],
        out_specs=pl.BlockSpec((tr, tt), lambda i, j: (i, j)),
        out_shape=jax.ShapeDtypeStruct((n, m), out_dtype),
        scratch_shapes=[pltpu.VMEM((tr, k), BF16)],
        compiler_params=pltpu.CompilerParams(
            dimension_semantics=("parallel", "arbitrary"), vmem_limit_bytes=VMEM_LIMIT),
        name="proj_matmul_wt",
    )(w, xt)


def _mm_ta_kernel(xt_ref, w_ref, o_ref):
    o_ref[...] = lax.dot_general(xt_ref[...], w_ref[...].astype(BF16), (((0,), (0,)), ((), ())),
                                 preferred_element_type=F32).astype(o_ref.dtype)


def matmul_ta(xt, w, out_dtype=F32):
    k, m = xt.shape
    n = w.shape[1]
    tm = _pick_tile(m, (512, 256, 128))
    tn = _pick_tile(n, (512, 384, 256, 128))
    return pl.pallas_call(
        _mm_ta_kernel,
        grid=(m // tm, n // tn),
        in_specs=[pl.BlockSpec((k, tm), lambda i, j: (0, i)),
                  pl.BlockSpec((k, tn), lambda i, j: (0, j))],
        out_specs=pl.BlockSpec((tm, tn), lambda i, j: (i, j)),
        out_shape=jax.ShapeDtypeStruct((m, n), out_dtype),
        compiler_params=pltpu.CompilerParams(
            dimension_semantics=("parallel", "parallel"), vmem_limit_bytes=VMEM_LIMIT),
        name="proj_matmul_ta",
    )(xt, w)


POST_TM = 320
POST_TN = 512


def _post_kernel(x_ref, h_ref, p_ref, g_ref, b_ref, wg_ref, wp_ref, o_ref, obf_ref, y_sc, ybf_sc):
    j = pl.program_id(1)

    @pl.when(j == 0)
    def _():
        t = ALPHA * x_ref[...] + h_ref[...]
        mu = jnp.mean(t, axis=-1, keepdims=True)
        d = t - mu
        var = jnp.mean(d * d, axis=-1, keepdims=True)
        y = d * lax.rsqrt(var + LN_EPS) * g_ref[...] + b_ref[...]
        ybf_sc[...] = y.astype(BF16)
        for jj in range(D_MODEL // POST_TN):
            y_sc[jj] = y[:, jj * POST_TN:(jj + 1) * POST_TN]

    gate = jnp.dot(ybf_sc[...], wg_ref[...].astype(BF16), preferred_element_type=F32)
    ple = jnp.dot(p_ref[...].astype(BF16), wp_ref[...].astype(BF16), preferred_element_type=F32)
    o = y_sc[j] + (1.0 / (1.0 + jnp.exp(-gate))) * ple
    o_ref[...] = o
    obf_ref[...] = o.astype(BF16)


def post_norm_ple(x, h, p_bf, g, b, wg_bf, wp_bf):
    m = x.shape[0]
    tm, tn = POST_TM, POST_TN
    return pl.pallas_call(
        _post_kernel,
        grid=(m // tm, D_MODEL // tn),
        in_specs=[pl.BlockSpec((tm, D_MODEL), lambda i, j: (i, 0)),
                  pl.BlockSpec((tm, D_MODEL), lambda i, j: (i, 0)),
                  pl.BlockSpec((tm, PLE_DIM), lambda i, j: (i, 0)),
                  pl.BlockSpec((1, D_MODEL), lambda i, j: (0, 0)),
                  pl.BlockSpec((1, D_MODEL), lambda i, j: (0, 0)),
                  pl.BlockSpec((D_MODEL, tn), lambda i, j: (0, j)),
                  pl.BlockSpec((PLE_DIM, tn), lambda i, j: (0, j))],
        out_specs=[pl.BlockSpec((tm, tn), lambda i, j: (i, j)),
                   pl.BlockSpec((tm, tn), lambda i, j: (i, j))],
        out_shape=[jax.ShapeDtypeStruct((m, D_MODEL), F32),
                   jax.ShapeDtypeStruct((m, D_MODEL), BF16)],
        scratch_shapes=[pltpu.VMEM((D_MODEL // tn, tm, tn), F32),
                        pltpu.VMEM((tm, D_MODEL), BF16)],
        compiler_params=pltpu.CompilerParams(
            dimension_semantics=("parallel", "arbitrary"), vmem_limit_bytes=VMEM_LIMIT),
        name="post_norm_ple",
    )(x, h, p_bf, g.reshape(1, D_MODEL), b.reshape(1, D_MODEL), wg_bf, wp_bf)


def rel_bucket(dist):
    n = jnp.maximum(dist, 0)
    exact = N_BUCKETS // 2
    logb = exact + (jnp.log(jnp.maximum(n, exact).astype(F32) / exact)
                    / math.log(REL_MAX_DIST / exact) * (N_BUCKETS - exact)).astype(jnp.int32)
    return jnp.where(n < exact, n, jnp.minimum(logb, N_BUCKETS - 1))


def split_tokens(t):
    c = t.shape[-1]
    return (t[:N_PROMPT_TOK].reshape(BATCH, SEQ, c), t[N_PROMPT_TOK:].reshape(DEC_BATCH, DEC_SEQ, c))


def join_tokens(tp, ts):
    c = tp.shape[-1]
    return jnp.concatenate([tp.reshape(N_PROMPT_TOK, c), ts.reshape(N_SAMPLE_TOK, c)], axis=0)


G_A = N_HEADS // KV_A
SWA_KEYS = 2 * WINDOW


def swa_tables(rel_bias, sinks, dec_len):
    def heads_to(b, lead):
        return jnp.moveaxis(b, -1, 0).reshape((KV_A, G_A) + lead)

    dist = jnp.arange(WINDOW)[None, :] - (jnp.arange(SWA_KEYS)[:, None] - WINDOW)
    ok = (dist >= 0) & (dist < WINDOW)
    b = jnp.where(ok[..., None], rel_bias[rel_bucket(dist)].astype(F32), NEG_BIG)
    bias_p = heads_to(b, (SWA_KEYS, WINDOW)).transpose(0, 2, 1, 3).reshape(KV_A, SWA_KEYS, G_A * WINDOW)
    sink_p = jnp.broadcast_to(sinks.astype(F32).reshape(KV_A, 1, G_A, 1), (KV_A, 1, G_A, WINDOW))
    sink_p = sink_p.reshape(KV_A, 1, G_A * WINDOW)
    key_i = jnp.arange(SWA_KEYS)[None, :]
    dist = jnp.arange(dec_len)[:, None] + WINDOW - key_i
    ok = (dist >= 0) & (dist < WINDOW) & (key_i < WINDOW + dec_len)
    b = jnp.where(ok[..., None], rel_bias[rel_bucket(dist)].astype(F32), NEG_BIG)
    bias_s = heads_to(b, (dec_len, SWA_KEYS)).reshape(KV_A, G_A * dec_len, SWA_KEYS)
    sink_s = jnp.broadcast_to(sinks.astype(F32).reshape(KV_A, G_A, 1, 1), (KV_A, G_A, dec_len, LANE))
    sink_s = sink_s.reshape(KV_A, G_A * dec_len, LANE)
    return bias_p, sink_p, bias_s, sink_s


def _swa_prompt_kernel(qT_ref, zT_ref, vTp_ref, vTc_ref, kp_ref, kc_ref, bias_ref, sink_ref, o_ref, *, cdt):
    first = pl.program_id(1) == 0
    kk = jnp.concatenate([kp_ref[...], kc_ref[...]], axis=0)
    vT = jnp.concatenate([vTp_ref[...], vTc_ref[...]], axis=1)
    prev_key = lax.broadcasted_iota(jnp.int32, (SWA_KEYS, G_A * WINDOW), 0) < WINDOW
    pw = 2 * HEAD_DIM
    for j in range(KV_A):
        kpair = kk[:, (j // 2) * pw:(j // 2 + 1) * pw].astype(cdt)
        qj = jnp.concatenate([qT_ref[(G_A * j + g) * HEAD_DIM:(G_A * j + g + 1) * HEAD_DIM, :]
                              for g in range(G_A)], axis=1)
        qj = (qj.astype(F32) * (HEAD_DIM ** -0.5)).astype(cdt)
        zpad = jnp.zeros_like(qj)
        rhs = jnp.concatenate([qj, zpad] if j % 2 == 0 else [zpad, qj], axis=0)
        s = jnp.dot(kpair, rhs, preferred_element_type=F32) + bias_ref[j]
        s = jnp.where(prev_key, jnp.where(first, NEG_BIG, s), s)
        sink = sink_ref[j]
        m = jnp.maximum(jnp.max(s, axis=0, keepdims=True), sink)
        e = jnp.exp(s - m)
        den = jnp.sum(e, axis=0, keepdims=True) + jnp.exp(sink - m)
        p = (e * (1.0 / den)).astype(cdt)
        acc = jnp.dot(vT[j * HEAD_DIM:(j + 1) * HEAD_DIM, :].astype(cdt), p, preferred_element_type=F32)
        for g in range(G_A):
            r0 = (G_A * j + g) * HEAD_DIM
            z = zT_ref[r0:r0 + HEAD_DIM, :]
            o_ref[r0:r0 + HEAD_DIM, :] = (acc[:, g * WINDOW:(g + 1) * WINDOW]
                                          * (z * (1.0 / (1.0 + jnp.exp(-z))))).astype(o_ref.dtype)


def swa_prompt(qT, vT, zT, kv_nat, bias_p, sink_p, *, n_batch, seq_len, cdt=BF16):
    nb = seq_len // WINDOW
    cur = lambda b, i: b * nb + i
    prev = lambda b, i: b * nb + jnp.maximum(i - 1, 0)
    v_row_blk = 0
    return pl.pallas_call(
        functools.partial(_swa_prompt_kernel, cdt=cdt),
        grid=(n_batch, nb),
        in_specs=[pl.BlockSpec((ATT_WIDTH, WINDOW), lambda b, i: (0, cur(b, i))),
                  pl.BlockSpec((ATT_WIDTH, WINDOW), lambda b, i: (0, cur(b, i))),
                  pl.BlockSpec((A_KV, WINDOW), lambda b, i: (v_row_blk, prev(b, i))),
                  pl.BlockSpec((A_KV, WINDOW), lambda b, i: (v_row_blk, cur(b, i))),
                  pl.BlockSpec((WINDOW, A_KV), lambda b, i: (prev(b, i), 0)),
                  pl.BlockSpec((WINDOW, A_KV), lambda b, i: (cur(b, i), 0)),
                  pl.BlockSpec(bias_p.shape, lambda b, i: (0, 0, 0)),
                  pl.BlockSpec(sink_p.shape, lambda b, i: (0, 0, 0))],
        out_specs=pl.BlockSpec((ATT_WIDTH, WINDOW), lambda b, i: (0, cur(b, i))),
        out_shape=jax.ShapeDtypeStruct((ATT_WIDTH, n_batch * seq_len), BF16),
        compiler_params=pltpu.CompilerParams(
            dimension_semantics=("parallel", "parallel"), vmem_limit_bytes=VMEM_LIMIT),
        name="swa_prompt",
    )(qT, zT, vT, vT, kv_nat, kv_nat, bias_p, sink_p)


def _swa_sample_kernel(q_ref, z_ref, k_ref, v_ref, bias_ref, sink_ref, o_ref, *, cdt):
    for j in range(KV_A):
        s = lax.dot_general(q_ref[0, j], k_ref[0, j].astype(cdt), (((1,), (1,)), ((), ())),
                            preferred_element_type=F32) + bias_ref[j]
        sink = sink_ref[j][:, 0:1]
        m = jnp.maximum(jnp.max(s, axis=1, keepdims=True), sink)
        e = jnp.exp(s - m)
        den = jnp.sum(e, axis=1, keepdims=True) + jnp.exp(sink - m)
        p = (e * (1.0 / den)).astype(cdt)
        z = z_ref[0, j]
        o_ref[0, j] = jnp.dot(p, v_ref[0, j].astype(cdt), preferred_element_type=F32) * (z * (1.0 / (1.0 + jnp.exp(-z))))


def swa_sample(q_s, z_s, kv_s, kv_cache, bias_s, sink_s, cdt=BF16):
    n_dec = kv_cache.shape[0]
    dec_len = q_s.shape[0] // n_dec
    rows = G_A * dec_len

    def head_rows(t, scale):
        t = t.reshape(n_dec, dec_len, KV_A, G_A, HEAD_DIM).transpose(0, 2, 3, 1, 4) * scale
        return jnp.pad(t.reshape(n_dec, KV_A, rows, HEAD_DIM), ((0, 0), (0, 0), (0, 0), (0, HEAD_DIM)))

    new = kv_s.reshape(n_dec, dec_len, 2, KV_A, HEAD_DIM)
    cat = jnp.concatenate([kv_cache, new], axis=1)
    keys = jnp.pad(cat.transpose(2, 0, 3, 1, 4),
                   ((0, 0), (0, 0), (0, 0), (0, SWA_KEYS - WINDOW - dec_len), (0, HEAD_DIM)))
    blk = lambda r: pl.BlockSpec((1, KV_A, r, 2 * HEAD_DIM), lambda b: (b, 0, 0, 0))
    o = pl.pallas_call(
        functools.partial(_swa_sample_kernel, cdt=cdt),
        grid=(n_dec,),
        in_specs=[blk(rows), blk(rows), blk(SWA_KEYS), blk(SWA_KEYS),
                  pl.BlockSpec(bias_s.shape, lambda b: (0, 0, 0)),
                  pl.BlockSpec(sink_s.shape, lambda b: (0, 0, 0))],
        out_specs=blk(rows),
        out_shape=jax.ShapeDtypeStruct((n_dec, KV_A, rows, 2 * HEAD_DIM), F32),
        compiler_params=pltpu.CompilerParams(dimension_semantics=("parallel",), vmem_limit_bytes=VMEM_LIMIT),
        name="swa_sample",
    )(head_rows(q_s, HEAD_DIM ** -0.5).astype(cdt), head_rows(z_s, 1.0), keys[0], keys[1], bias_s, sink_s)
    o = o[..., :HEAD_DIM].reshape(n_dec, KV_A, G_A, dec_len, HEAD_DIM).transpose(0, 3, 1, 2, 4)
    return o.reshape(n_dec * dec_len, ATT_WIDTH), cat[:, dec_len:]


S5_SLAB_G = 8
S5_SLAB_CH = S5_SLAB_G * S5_GROUP
S5_SLAB_ST = S5_SLAB_G * S5_STATE
S5_N_SLABS = S5_GROUPS // S5_SLAB_G
S5_CHAINS = 8
S5_HALF_CH = S5_CHAINS * S5_SLAB_CH
S5_T = 256
S5_LT = 2 * S5_SLAB_ST // LANE


def _gelu_tanh(x):
    return 0.5 * x * (1.0 + jnp.tanh(math.sqrt(2.0 / math.pi) * (x + 0.044715 * (x * x * x))))


def s5_tables(a_re, a_im, b_re, b_im, c_re, c_im, log_dt):
    a = lax.complex(a_re, a_im)
    dt = jnp.exp(log_dt)[:, None]
    a_bar = jnp.exp(a * dt)
    b_bar = ((a_bar - 1.0) / a)[..., None] * lax.complex(b_re, b_im)
    eye = jnp.eye(S5_SLAB_G, dtype=F32)

    def b_blk(t):
        t = t.reshape(S5_N_SLABS, S5_SLAB_G, S5_STATE, S5_GROUP)
        return jnp.einsum('ij,sipc->sicjp', eye, t).reshape(S5_N_SLABS, S5_SLAB_CH, S5_SLAB_ST)

    def c_blk(t):
        t = t.reshape(S5_N_SLABS, S5_SLAB_G, S5_GROUP, S5_STATE)
        return jnp.einsum('ij,sicp->sjpic', eye, t).reshape(S5_N_SLABS, S5_SLAB_ST, S5_SLAB_CH)

    bcat = jnp.concatenate([b_blk(b_bar.real), b_blk(b_bar.imag)], axis=2)
    ccat = jnp.concatenate([c_blk(c_re), -c_blk(c_im)], axis=1)
    a_cat = jnp.concatenate([a_bar.real.reshape(S5_N_SLABS, S5_SLAB_ST),
                             a_bar.imag.reshape(S5_N_SLABS, S5_SLAB_ST)], axis=1)
    return a_cat, bcat, ccat


def _s5_prompt_kernel(u_ref, bcat_ref, ccat_ref, a_ref, d_ref, y_ref, hout_ref, sc, h_sc, *, cdt):
    tc = pl.program_id(2)
    n_lt_half = S5_LT // 2

    @pl.when(tc == 0)
    def _():
        h_sc[...] = jnp.zeros_like(h_sc)

    for j in range(S5_CHAINS):
        uj = u_ref[:, j * S5_SLAB_CH:(j + 1) * S5_SLAB_CH].astype(cdt)
        bu = jnp.dot(uj, bcat_ref[0, j], preferred_element_type=F32)
        for lt in range(S5_LT):
            sc[lt, pl.ds(j, S5_T, stride=S5_CHAINS), :] = bu[:, lt * LANE:(lt + 1) * LANE]

    a_re = [a_ref[0, :, lt * LANE:(lt + 1) * LANE] for lt in range(n_lt_half)]
    a_im = [a_ref[0, :, (n_lt_half + lt) * LANE:(n_lt_half + lt + 1) * LANE] for lt in range(n_lt_half)]

    def step(t, h):
        r0 = pl.multiple_of(t * S5_CHAINS, S5_CHAINS)
        new = list(h)
        for lt in range(n_lt_half):
            hr, hi = h[lt], h[n_lt_half + lt]
            nr = a_re[lt] * hr - a_im[lt] * hi + sc[lt, pl.ds(r0, S5_CHAINS), :]
            ni = a_re[lt] * hi + a_im[lt] * hr + sc[n_lt_half + lt, pl.ds(r0, S5_CHAINS), :]
            sc[lt, pl.ds(r0, S5_CHAINS), :] = nr
            sc[n_lt_half + lt, pl.ds(r0, S5_CHAINS), :] = ni
            new[lt], new[n_lt_half + lt] = nr, ni
        return tuple(new)

    h = lax.fori_loop(0, S5_T, step, tuple(h_sc[lt] for lt in range(S5_LT)), unroll=8)
    for lt in range(S5_LT):
        h_sc[lt] = h[lt]
        hout_ref[0, 0, :, lt * LANE:(lt + 1) * LANE] = h[lt]

    for j in range(S5_CHAINS):
        hcat = jnp.concatenate([sc[lt, pl.ds(j, S5_T, stride=S5_CHAINS), :] for lt in range(S5_LT)], axis=1)
        cols = slice(j * S5_SLAB_CH, (j + 1) * S5_SLAB_CH)
        y = jnp.dot(hcat.astype(cdt), ccat_ref[0, j], preferred_element_type=F32) + d_ref[0, :, cols] * u_ref[:, cols]
        y_ref[:, cols] = _gelu_tanh(y)


def s5_prompt(proj, a_cat, bcat, ccat, d_skip, *, n_batch, seq_len, n_rows_out, cdt=BF16):
    n_t = seq_len // S5_T
    n_half = S5_WIDTH // S5_HALF_CH
    half = lambda t: t.reshape((n_half, S5_CHAINS) + t.shape[1:])
    return pl.pallas_call(
        functools.partial(_s5_prompt_kernel, cdt=cdt),
        grid=(n_batch, n_half, n_t),
        in_specs=[pl.BlockSpec((S5_T, S5_HALF_CH), lambda b, hf, t: (b * n_t + t, hf)),
                  pl.BlockSpec((1, S5_CHAINS, S5_SLAB_CH, 2 * S5_SLAB_ST), lambda b, hf, t: (hf, 0, 0, 0)),
                  pl.BlockSpec((1, S5_CHAINS, 2 * S5_SLAB_ST, S5_SLAB_CH), lambda b, hf, t: (hf, 0, 0, 0)),
                  pl.BlockSpec((1, S5_CHAINS, 2 * S5_SLAB_ST), lambda b, hf, t: (hf, 0, 0)),
                  pl.BlockSpec((1, 1, S5_HALF_CH), lambda b, hf, t: (hf, 0, 0))],
        out_specs=[pl.BlockSpec((S5_T, S5_HALF_CH), lambda b, hf, t: (b * n_t + t, hf)),
                   pl.BlockSpec((1, 1, S5_CHAINS, 2 * S5_SLAB_ST), lambda b, hf, t: (b, hf, 0, 0))],
        out_shape=[jax.ShapeDtypeStruct((n_rows_out, S5_WIDTH), F32),
                   jax.ShapeDtypeStruct((n_batch, n_half, S5_CHAINS, 2 * S5_SLAB_ST), F32)],
        scratch_shapes=[pltpu.VMEM((S5_LT, S5_T * S5_CHAINS, LANE), F32),
                        pltpu.VMEM((S5_LT, S5_CHAINS, LANE), F32)],
        compiler_params=pltpu.CompilerParams(
            dimension_semantics=("parallel", "parallel", "arbitrary"), vmem_limit_bytes=VMEM_LIMIT),
        name="s5_prompt",
    )(proj, half(bcat.astype(cdt)), half(ccat.astype(cdt)), half(a_cat), d_skip.reshape(n_half, 1, S5_HALF_CH))


def _s5_sample_kernel(u_ref, bcat_ref, ccat_ref, a_ref, d_ref, h0_ref, y_ref, hout_ref, sc, *, n_b, n_t, cdt):
    u = u_ref[...]
    bu = jnp.dot(u.astype(cdt), bcat_ref[0], preferred_element_type=F32)
    a_re = a_ref[0, :, :S5_SLAB_ST]
    a_im = a_ref[0, :, S5_SLAB_ST:]
    for bg in range(n_b // 8):
        hr = h0_ref[0, bg * 8:(bg + 1) * 8, :S5_SLAB_ST]
        hi = h0_ref[0, bg * 8:(bg + 1) * 8, S5_SLAB_ST:]
        for t in range(n_t):
            r = t * n_b + bg * 8
            hr, hi = (a_re * hr - a_im * hi + bu[r:r + 8, :S5_SLAB_ST],
                      a_re * hi + a_im * hr + bu[r:r + 8, S5_SLAB_ST:])
            sc[r:r + 8, :S5_SLAB_ST] = hr
            sc[r:r + 8, S5_SLAB_ST:] = hi
        hout_ref[0, bg * 8:(bg + 1) * 8, :S5_SLAB_ST] = hr
        hout_ref[0, bg * 8:(bg + 1) * 8, S5_SLAB_ST:] = hi
    y = jnp.dot(sc[...].astype(cdt), ccat_ref[0], preferred_element_type=F32) + d_ref[0] * u
    y_ref[...] = _gelu_tanh(y)


def s5_sample(u_tb, a_cat, bcat, ccat, d_skip, h0_cat, *, n_b, n_t, cdt=BF16):
    rows = n_t * n_b
    return pl.pallas_call(
        functools.partial(_s5_sample_kernel, n_b=n_b, n_t=n_t, cdt=cdt),
        grid=(S5_N_SLABS,),
        in_specs=[pl.BlockSpec((rows, S5_SLAB_CH), lambda s: (0, s)),
                  pl.BlockSpec((1, S5_SLAB_CH, 2 * S5_SLAB_ST), lambda s: (s, 0, 0)),
                  pl.BlockSpec((1, 2 * S5_SLAB_ST, S5_SLAB_CH), lambda s: (s, 0, 0)),
                  pl.BlockSpec((1, 1, 2 * S5_SLAB_ST), lambda s: (s, 0, 0)),
                  pl.BlockSpec((1, 1, S5_SLAB_CH), lambda s: (s, 0, 0)),
                  pl.BlockSpec((1, n_b, 2 * S5_SLAB_ST), lambda s: (s, 0, 0))],
        out_specs=[pl.BlockSpec((rows, S5_SLAB_CH), lambda s: (0, s)),
                   pl.BlockSpec((1, n_b, 2 * S5_SLAB_ST), lambda s: (s, 0, 0))],
        out_shape=[jax.ShapeDtypeStruct((rows, S5_WIDTH), F32),
                   jax.ShapeDtypeStruct((S5_N_SLABS, n_b, 2 * S5_SLAB_ST), F32)],
        scratch_shapes=[pltpu.VMEM((rows, 2 * S5_SLAB_ST), F32)],
        compiler_params=pltpu.CompilerParams(
            dimension_semantics=("arbitrary",), vmem_limit_bytes=VMEM_LIMIT),
        name="s5_sample",
    )(u_tb, bcat.astype(cdt), ccat.astype(cdt), a_cat.reshape(S5_N_SLABS, 1, 2 * S5_SLAB_ST),
      d_skip.reshape(S5_N_SLABS, 1, S5_SLAB_CH), h0_cat)


GLU_TM = 320
GLU_TN = 512


def _glu_kernel(yfull_ref, w_ref, ycol_ref, z_ref, o_ref, ybf_sc):
    @pl.when(pl.program_id(1) == 0)
    def _():
        ybf_sc[...] = yfull_ref[...].astype(ybf_sc.dtype)

    glu = jnp.dot(ybf_sc[...], w_ref[...].astype(ybf_sc.dtype), preferred_element_type=F32)
    z = z_ref[...]
    y = ycol_ref[...]
    o_ref[...] = (y * (1.0 / (1.0 + jnp.exp(-glu))) * (z * (1.0 / (1.0 + jnp.exp(-z))))).astype(o_ref.dtype)


def s5_glu_gate(y, w_glu, proj, row_off, cdt=BF16):
    m = y.shape[0]
    tm, tn = _pick_tile(m, (256, 128, 64, 32, 16)), GLU_TN
    assert row_off % tm == 0
    z_off, r_off = S5_WIDTH // tn, row_off // tm
    return pl.pallas_call(
        _glu_kernel,
        grid=(m // tm, S5_WIDTH // tn),
        in_specs=[pl.BlockSpec((tm, S5_WIDTH), lambda i, j: (i, 0)),
                  pl.BlockSpec((S5_WIDTH, tn), lambda i, j: (0, j)),
                  pl.BlockSpec((tm, tn), lambda i, j: (i, j)),
                  pl.BlockSpec((tm, tn), lambda i, j: (r_off + i, z_off + j))],
        out_specs=pl.BlockSpec((tm, tn), lambda i, j: (i, j)),
        out_shape=jax.ShapeDtypeStruct((m, S5_WIDTH), BF16),
        scratch_shapes=[pltpu.VMEM((tm, S5_WIDTH), cdt)],
        compiler_params=pltpu.CompilerParams(
            dimension_semantics=("parallel", "arbitrary"), vmem_limit_bytes=VMEM_LIMIT),
        name="s5_glu_gate",
    )(y, w_glu, y, proj)


def s5_layer(proj, state_in, tables, d_skip, w_glu, *, n_batch, seq_len, n_dec, dec_len, cdt=BF16):
    a_cat, bcat, ccat = tables
    n_p = n_batch * seq_len
    n_s = n_dec * dec_len
    y_p, h_p = s5_prompt(proj, a_cat, bcat, ccat, d_skip, n_batch=n_batch, seq_len=seq_len, n_rows_out=n_p, cdt=cdt)
    u_tb = jnp.swapaxes(proj[n_p:, :S5_WIDTH].reshape(n_dec, dec_len, S5_WIDTH), 0, 1).reshape(n_s, S5_WIDTH)
    h0 = state_in.reshape(n_dec, S5_N_SLABS, S5_SLAB_ST, 2)
    h0_cat = jnp.concatenate([jnp.swapaxes(h0[..., 0], 0, 1), jnp.swapaxes(h0[..., 1], 0, 1)], axis=-1)
    y_tb, h_s = s5_sample(u_tb, a_cat, bcat, ccat, d_skip, h0_cat, n_b=n_dec, n_t=dec_len, cdt=cdt)
    y_s = jnp.swapaxes(y_tb.reshape(dec_len, n_dec, S5_WIDTH), 0, 1).reshape(n_s, S5_WIDTH)
    gated_p = s5_glu_gate(y_p, w_glu, proj, 0, cdt=cdt)
    gated_s = s5_glu_gate(y_s, w_glu, proj, n_p, cdt=cdt)
    hp = h_p.reshape(n_batch, S5_N_SLABS, 2, S5_SLAB_ST)
    st_p = jnp.stack([hp[:, :, 0], hp[:, :, 1]], axis=-1).reshape(n_batch, S5_GROUPS, S5_STATE, 2)
    hs = jnp.swapaxes(h_s, 0, 1).reshape(n_dec, S5_N_SLABS, 2, S5_SLAB_ST)
    st_s = jnp.stack([hs[:, :, 0], hs[:, :, 1]], axis=-1).reshape(n_dec, S5_GROUPS, S5_STATE, 2)
    return gated_p, gated_s, st_p, st_s


GDN_CONV_TT = 256
GDN_CONV_CW = 1024
GDN_HIST = 8
GDN_HB = 16
GDN_TT = 256
GDN_SAMPLE_ROWS = 8


def _gdn_conv_kernel(x_ref, hist_ref, w_ref, o_ref, *, rows, n_t, zero_first):
    i, j = pl.program_id(0), pl.program_id(1)
    hist = hist_ref[...]
    if zero_first:
        hist = jnp.where(i % n_t == 0, 0.0, hist)
    ext = jnp.concatenate([hist, x_ref[...]], axis=0)
    acc = ext[GDN_HIST:GDN_HIST + rows] * w_ref[GDN_CONV - 1:GDN_CONV, :]
    for s in range(1, GDN_CONV):
        acc = acc + ext[GDN_HIST - s:GDN_HIST - s + rows] * w_ref[GDN_CONV - 1 - s:GDN_CONV - s, :]
    conv = acc * (1.0 / (1.0 + jnp.exp(-acc)))
    n_qk_blocks = 2 * GDN_QK_WIDTH // GDN_CONV_CW

    @pl.when(j >= n_qk_blocks)
    def _():
        o_ref[...] = conv

    @pl.when(j < n_qk_blocks)
    def _():
        scale = jnp.where(j < GDN_QK_WIDTH // GDN_CONV_CW, GDN_DK ** -0.5, 1.0)
        for h in range(GDN_CONV_CW // GDN_DK):
            t = conv[:, h * GDN_DK:(h + 1) * GDN_DK]
            n = t * lax.rsqrt(jnp.sum(t * t, axis=-1, keepdims=True) + 1e-6)
            o_ref[:, h * GDN_DK:(h + 1) * GDN_DK] = n * scale


def gdn_conv(x, hist_src, conv_w, *, rows, n_blocks, n_t, data_map, hist_map, zero_first):
    n_out = n_blocks * rows
    return pl.pallas_call(
        functools.partial(_gdn_conv_kernel, rows=rows, n_t=n_t, zero_first=zero_first),
        grid=(n_blocks, GDN_CONV_CH // GDN_CONV_CW),
        in_specs=[pl.BlockSpec((rows, GDN_CONV_CW), lambda i, j: (data_map(i), j)),
                  pl.BlockSpec((GDN_HIST, GDN_CONV_CW), lambda i, j: (hist_map(i), j)),
                  pl.BlockSpec((GDN_CONV, GDN_CONV_CW), lambda i, j: (0, j))],
        out_specs=pl.BlockSpec((rows, GDN_CONV_CW), lambda i, j: (i, j)),
        out_shape=jax.ShapeDtypeStruct((n_out, GDN_CONV_CH), F32),
        compiler_params=pltpu.CompilerParams(
            dimension_semantics=("parallel", "parallel"), vmem_limit_bytes=VMEM_LIMIT),
        name="gdn_conv",
    )(x, hist_src, conv_w)


def _gdn_chunk_lockstep_kernel(q_ref, k_ref, v_ref, z_ref, ab_ref, alog_ref, dtb_ref, nw_ref, s0_ref, o_ref,
                               sout_ref, s_sc, *, chunk, n_inner, n_tt, valid_len):
    C = chunk
    hb, tt = pl.program_id(1), pl.program_id(2)

    @pl.when(tt == 0)
    def _():
        s_sc[...] = s0_ref[0]

    rowi = lax.broadcasted_iota(jnp.int32, (C, C), 0)
    coli = lax.broadcasted_iota(jnp.int32, (C, C), 1)
    causal = rowi >= coli
    strict = rowi > coli
    ltri = jnp.where(causal, 1.0, 0.0)
    utri = jnp.where(rowi <= coli, 1.0, 0.0)
    eye = jnp.where(rowi == coli, 1.0, 0.0)
    hi = lax.Precision.HIGHEST
    shift = (LANE - hb * GDN_HB) % LANE
    alog = pltpu.roll(jnp.broadcast_to(alog_ref[...], (8, LANE)), shift, 1)[0:1]
    dtb = pltpu.roll(jnp.broadcast_to(dtb_ref[...], (8, LANE)), shift, 1)[0:1]
    nw = nw_ref[...]
    tok_valid = lax.broadcasted_iota(jnp.int32, (C, LANE), 0) < valid_len
    dot = functools.partial(jnp.dot, preferred_element_type=F32)
    dot_nt = lambda a, b: lax.dot_general(a, b, (((1,), (1,)), ((), ())), preferred_element_type=F32)
    dot_tn = lambda a, b: lax.dot_general(a, b, (((0,), (0,)), ((), ())), preferred_element_type=F32)
    units = [(c, i) for c in range(n_inner) for i in range(GDN_HB)]
    rows = lambda c: slice(c * C, (c + 1) * C)
    qk_cols = lambda i: slice((i // 2) * GDN_DK, (i // 2 + 1) * GDN_DK)
    v_cols = lambda i: slice(i * GDN_DV, (i + 1) * GDN_DV)

    g_all, beta_all = [], []
    for c in range(n_inner):
        ab = pltpu.roll(ab_ref[rows(c), :], shift, 1)
        xa = ab + dtb
        softplus = jnp.maximum(xa, 0.0) + jnp.log1p(jnp.exp(-jnp.abs(xa)))
        g_all.append(jnp.where(tok_valid, -jnp.exp(alog) * softplus, 0.0))
        beta_all.append(jnp.where(tok_valid, 1.0 / (1.0 + jnp.exp(-ab)), 0.0))
    gam_all = [jnp.dot(ltri, g, preferred_element_type=F32, precision=hi) for g in g_all]
    gamT_all = [lax.dot_general(g, utri, (((0,), (0,)), ((), ())), preferred_element_type=F32, precision=hi)
                for g in g_all]

    qkk = [dot_nt(jnp.concatenate([q_ref[rows(c), qk_cols(i)], k_ref[rows(c), qk_cols(i)]], axis=0).astype(BF16),
                  k_ref[rows(c), qk_cols(i)].astype(BF16)) for c, i in units]
    gam_c = [jnp.broadcast_to(gam_all[c][:, i:i + 1], (C, LANE)) for c, i in units]
    beta_c = [jnp.broadcast_to(beta_all[c][:, 32 + i:33 + i], (C, LANE)) for c, i in units]
    gam_last = [jnp.broadcast_to(gam_all[c][C - 1:C, i:i + 1], (1, LANE)) for c, i in units]
    decay = [jnp.where(causal, jnp.exp(jnp.where(causal, gc[:, :C] - jnp.broadcast_to(gamT_all[c][i:i + 1, :], (C, C)),
                                                 0.0)), 0.0) for gc, (c, i) in zip(gam_c, units)]
    qk = [(x[:C] * d).astype(BF16) for x, d in zip(qkk, decay)]
    neg_a = [jnp.where(strict, -(b[:, :C] * x[C:] * d), 0.0) for b, x, d in zip(beta_c, qkk, decay)]
    p_inv = [eye + n for n in neg_a]
    m_pow = neg_a
    for _ in range(int(math.log2(C)) - 1):
        m_pow = [dot(m.astype(BF16), m.astype(BF16)) for m in m_pow]
        p_inv = [p + dot(p.astype(BF16), m.astype(BF16)) for p, m in zip(p_inv, m_pow)]
    eg = [jnp.exp(gc) for gc in gam_c]
    sol = [dot(p.astype(BF16), jnp.concatenate([b * v_ref[rows(c), v_cols(i)], (b * e) * k_ref[rows(c), qk_cols(i)]],
                                               axis=1).astype(BF16))
           for p, b, e, (c, i) in zip(p_inv, beta_c, eg, units)]
    wq = [jnp.concatenate([s[:, GDN_DV:], q_ref[rows(c), qk_cols(i)] * e], axis=0).astype(BF16)
          for s, e, (c, i) in zip(sol, eg, units)]
    k_dec = [(k_ref[rows(c), qk_cols(i)] * jnp.exp(gl - gc)).astype(BF16)
             for gl, gc, (c, i) in zip(gam_last, gam_c, units)]

    state = [s_sc[i] for i in range(GDN_HB)]
    for c in range(n_inner):
        base = c * GDN_HB
        ws = [dot(wq[base + i], state[i].astype(BF16)) for i in range(GDN_HB)]
        v_new = [(sol[base + i][:, :GDN_DV] - ws[i][:C]).astype(BF16) for i in range(GDN_HB)]
        o = [ws[i][C:] + dot(qk[base + i], v_new[i]) for i in range(GDN_HB)]
        state = [state[i] * jnp.exp(gam_last[base + i]) + dot_tn(k_dec[base + i], v_new[i]) for i in range(GDN_HB)]
        for i in range(GDN_HB):
            rms = lax.rsqrt(jnp.mean(o[i] * o[i], axis=-1, keepdims=True) + 1e-6)
            zz = z_ref[rows(c), v_cols(i)]
            o_ref[rows(c), v_cols(i)] = (o[i] * rms * nw * (zz * (1.0 / (1.0 + jnp.exp(-zz))))).astype(o_ref.dtype)
    for i in range(GDN_HB):
        s_sc[i] = state[i]

    @pl.when(tt == n_tt - 1)
    def _():
        sout_ref[0] = s_sc[...]


def gdn_chunk(conv, z, ab, a_log, dt_bias, norm_w, s0, *, n_seq, rows_per_seq, rows_per_step, chunk, valid_len,
              z_col_off, out_dtype):
    n_tt = rows_per_seq // rows_per_step
    n_inner = rows_per_step // chunk
    n_hb = GDN_V_HEADS // GDN_HB
    qw, vw = GDN_HB // 2 * GDN_DK, GDN_HB * GDN_DV
    k_off, v_off, z_off = GDN_QK_WIDTH // qw, 2 * GDN_QK_WIDTH // vw, z_col_off // vw
    row = lambda b, hb, t: b * n_tt + t
    pad_row = lambda p: jnp.pad(p.astype(F32), (0, LANE - p.shape[0])).reshape(1, LANE)
    return pl.pallas_call(
        functools.partial(_gdn_chunk_lockstep_kernel, chunk=chunk, n_inner=n_inner, n_tt=n_tt, valid_len=valid_len),
        grid=(n_seq, n_hb, n_tt),
        in_specs=[pl.BlockSpec((rows_per_step, qw), lambda b, hb, t: (row(b, hb, t), hb)),
                  pl.BlockSpec((rows_per_step, qw), lambda b, hb, t: (row(b, hb, t), k_off + hb)),
                  pl.BlockSpec((rows_per_step, vw), lambda b, hb, t: (row(b, hb, t), v_off + hb)),
                  pl.BlockSpec((rows_per_step, vw), lambda b, hb, t: (row(b, hb, t), z_off + hb)),
                  pl.BlockSpec((rows_per_step, LANE), lambda b, hb, t: (row(b, hb, t), 0)),
                  pl.BlockSpec((1, LANE), lambda b, hb, t: (0, 0)),
                  pl.BlockSpec((1, LANE), lambda b, hb, t: (0, 0)),
                  pl.BlockSpec((1, GDN_DV), lambda b, hb, t: (0, 0)),
                  pl.BlockSpec((1, GDN_HB, GDN_DK, GDN_DV), lambda b, hb, t: (b, hb, 0, 0))],
        out_specs=[pl.BlockSpec((rows_per_step, vw), lambda b, hb, t: (row(b, hb, t), hb)),
                   pl.BlockSpec((1, GDN_HB, GDN_DK, GDN_DV), lambda b, hb, t: (b, hb, 0, 0))],
        out_shape=[jax.ShapeDtypeStruct((n_seq * rows_per_seq, GDN_V_WIDTH), out_dtype),
                   jax.ShapeDtypeStruct((n_seq, GDN_V_HEADS, GDN_DK, GDN_DV), F32)],
        scratch_shapes=[pltpu.VMEM((GDN_HB, GDN_DK, GDN_DV), F32)],
        compiler_params=pltpu.CompilerParams(
            dimension_semantics=("parallel", "parallel", "arbitrary"), vmem_limit_bytes=VMEM_LIMIT),
        name="gdn_chunk",
    )(conv, conv, conv, z, ab, pad_row(a_log), pad_row(dt_bias), norm_w.astype(F32).reshape(1, GDN_DV), s0)


def gdn_layer(qkvz, ab, state_in, conv_in, conv_w, a_log, dt_bias, norm_w, *, n_batch, seq_len, n_dec, dec_len):
    n_p = n_batch * seq_len
    n_tp = seq_len // GDN_CONV_TT
    hist_per_block = GDN_CONV_TT // GDN_HIST
    conv_p = gdn_conv(qkvz, qkvz, conv_w, rows=GDN_CONV_TT, n_blocks=n_batch * n_tp, n_t=n_tp,
                      data_map=lambda i: i, hist_map=lambda i: jnp.maximum(i * hist_per_block - 1, 0),
                      zero_first=True)
    zeros_s = jnp.zeros((n_batch, GDN_V_HEADS, GDN_DK, GDN_DV), F32)
    gated_p, st_p = gdn_chunk(conv_p, qkvz, ab, a_log, dt_bias, norm_w, zeros_s, n_seq=n_batch,
                              rows_per_seq=seq_len, rows_per_step=GDN_TT, chunk=GDN_CHUNK, valid_len=GDN_CHUNK,
                              z_col_off=GDN_CONV_CH, out_dtype=BF16)
    buf_p = jnp.stack([lax.slice(qkvz, ((b + 1) * seq_len - (GDN_CONV - 1), 0), ((b + 1) * seq_len, GDN_CONV_CH))
                       for b in range(n_batch)])
    R = GDN_SAMPLE_ROWS
    x_s = qkvz[n_p:].reshape(n_dec, dec_len, -1)
    pad_t = lambda t, front: jnp.pad(t, ((0, 0), (front, R - front - t.shape[1]), (0, 0)))
    ext = jnp.concatenate([pad_t(conv_in, R - (GDN_CONV - 1)), pad_t(x_s[..., :GDN_CONV_CH], 0)], axis=1)
    ext = ext.reshape(n_dec * 2 * R, GDN_CONV_CH)
    conv_s = gdn_conv(ext, ext, conv_w, rows=R, n_blocks=n_dec, n_t=1,
                      data_map=lambda i: 2 * i + 1, hist_map=lambda i: 2 * i, zero_first=False)
    z_s = pad_t(x_s[..., GDN_CONV_CH:], 0).reshape(n_dec * R, GDN_V_WIDTH)
    ab_s = pad_t(ab[n_p:].reshape(n_dec, dec_len, LANE), 0).reshape(n_dec * R, LANE)
    gated_s, st_s = gdn_chunk(conv_s, z_s, ab_s, a_log, dt_bias, norm_w, state_in, n_seq=n_dec, rows_per_seq=R,
                              rows_per_step=R, chunk=R, valid_len=dec_len, z_col_off=0, out_dtype=F32)
    gated_s = gated_s.reshape(n_dec, R, GDN_V_WIDTH)[:, :dec_len].reshape(n_dec * dec_len, GDN_V_WIDTH)
    buf_s = jnp.concatenate([conv_in, x_s[..., :GDN_CONV_CH]], axis=1)[:, dec_len:]
    return gated_p, gated_s, st_p, st_s, buf_p, buf_s


DSA_KC = 256
INT_MIN = -2 ** 31
NEG_BIG = -1e30
G_D = N_HEADS // KV_D
BIAS_WIN = DSA_KC + Q_BLOCK


def _sortable_key(s):
    b = pltpu.bitcast(s, jnp.int32)
    return jnp.where(b < 0, b ^ jnp.int32(0x7FFFFFFF), b)


def _dsa_prompt_kernel(qT_ref, qiT_ref, wiT_ref, zT_ref, ki_ref, k_ref, vT_ref, win_ref, o_ref,
                       key_sc, mask_sc, *, topk, idx_bits, cdt):
    qb = pl.program_id(1)
    t0 = qb * Q_BLOCK
    nch = (qb + 2) // 2
    t_idx = t0 + lax.broadcasted_iota(jnp.int32, (1, Q_BLOCK), 1)
    row_iota = lax.broadcasted_iota(jnp.int32, (DSA_KC, Q_BLOCK), 0)

    def score_chunk(c, carry):
        kic = ki_ref[0, c].astype(cdt)
        acc = jnp.zeros((DSA_KC, Q_BLOCK), F32)
        for hp in range(IDX_HEADS // 2):
            rhs = jnp.concatenate([qiT_ref[(2 * hp) * IDX_DIM:(2 * hp + 1) * IDX_DIM, :],
                                   qiT_ref[(2 * hp + 1) * IDX_DIM:(2 * hp + 2) * IDX_DIM, :]], axis=1)
            s = jnp.dot(kic, rhs, preferred_element_type=F32) * (IDX_DIM ** -0.5)
            s = jnp.maximum(s, 0.0)
            w0 = wiT_ref[2 * hp:2 * hp + 1, :] * (IDX_HEADS ** -0.5)
            w1 = wiT_ref[2 * hp + 1:2 * hp + 2, :] * (IDX_HEADS ** -0.5)
            acc = acc + s[:, :Q_BLOCK] * w0 + s[:, Q_BLOCK:] * w1
        s_idx = c * DSA_KC + row_iota
        key_sc[c] = jnp.where(s_idx <= t_idx, _sortable_key(acc), INT_MIN)
        return carry

    lax.fori_loop(0, nch, score_chunk, 0)

    def count(pred):
        def body(c, acc):
            hit = pred(key_sc[c], c * DSA_KC + row_iota)
            return acc + hit.reshape(DSA_KC // 8, 8, Q_BLOCK).sum(axis=0)
        acc = lax.fori_loop(0, nch, body, jnp.zeros((8, Q_BLOCK), jnp.int32))
        return jnp.sum(acc, axis=0, keepdims=True)

    c_nonneg = count(lambda k, s: jnp.where(k >= 0, 1, 0))
    thr = jnp.where(c_nonneg >= topk, 0, INT_MIN).astype(jnp.int32)

    def thr_bit(i, thr):
        cand = thr + jnp.left_shift(jnp.int32(1), 30 - i)
        return jnp.where(count(lambda k, s: jnp.where(k >= cand, 1, 0)) >= topk, cand, thr)

    thr = lax.fori_loop(0, 31, thr_bit, thr)
    need = topk - count(lambda k, s: jnp.where(k > thr, 1, 0))

    def lim_bit(i, lim):
        cand = lim + jnp.left_shift(jnp.int32(1), idx_bits - 1 - i)
        c = count(lambda k, s: jnp.where(k == thr, jnp.where(s < cand, 1, 0), 0))
        return jnp.where(c <= need, cand, lim)

    n_ties = count(lambda k, s: jnp.where(k == thr, 1, 0))
    settled = jnp.min(jnp.where(thr == INT_MIN, 1, jnp.where(n_ties == need, 1, 0))) == 1
    lim = lax.cond(settled,
                   lambda: jnp.full((1, Q_BLOCK), 1 << idx_bits, jnp.int32),
                   lambda: lax.fori_loop(0, idx_bits, lim_bit, jnp.zeros((1, Q_BLOCK), jnp.int32)))

    def mask_chunk(c, carry):
        k = key_sc[c]
        s_idx = c * DSA_KC + row_iota
        tie = jnp.where(k == thr, jnp.where(s_idx < lim, 0.0, NEG_BIG), NEG_BIG)
        m = jnp.where(k > thr, 0.0, tie)
        mask_sc[c] = jnp.where(k == INT_MIN, NEG_BIG, m)
        return carry

    lax.fori_loop(0, nch, mask_chunk, 0)

    n_cols = G_D * Q_BLOCK
    half = DSA_KC // 2

    def head_q(j):
        qj = jnp.concatenate([qT_ref[(G_D * j + g) * HEAD_DIM:(G_D * j + g + 1) * HEAD_DIM, :]
                              for g in range(G_D)], axis=1)
        return (qj.astype(F32) * (HEAD_DIM ** -0.5)).astype(cdt)

    for jp in range(KV_D // 2):
        q0, q1 = head_q(2 * jp), head_q(2 * jp + 1)
        zq = jnp.zeros_like(q0)
        rhs = jnp.concatenate([jnp.concatenate([q0, zq], axis=0), jnp.concatenate([zq, q1], axis=0)], axis=1)

        def chunk_body(c, carry, jp=jp, rhs=rhs):
            m, l, acc0, acc1 = carry
            kc = k_ref[0, c, :, jp * 2 * HEAD_DIM:(jp + 1) * 2 * HEAD_DIM].astype(cdt)
            s = jnp.dot(kc, rhs, preferred_element_type=F32)
            wt = win_ref[qb - 2 * c]
            madd = mask_sc[c]
            parts = []
            for hh in range(2 * G_D):
                h = 2 * G_D * jp + hh
                tiles = []
                for u in range(2):
                    lo = (1 - u) * half
                    r = jnp.broadcast_to(wt[h:h + 1, lo:lo + 2 * half], (half, 2 * half))
                    tiles.append(pltpu.roll(r, 0, 1, stride=1, stride_axis=0)[:, half:])
                parts.append(s[:, hh * Q_BLOCK:(hh + 1) * Q_BLOCK] + (jnp.concatenate(tiles, axis=0) + madd))
            s = jnp.concatenate(parts, axis=1)
            m_new = jnp.maximum(m, jnp.max(s, axis=0, keepdims=True))
            alpha = jnp.exp(m - m_new)
            p = jnp.exp(s - m_new)
            l = l * alpha + jnp.sum(p, axis=0, keepdims=True)
            p = p.astype(cdt)
            v0 = vT_ref[0, c, (2 * jp) * HEAD_DIM:(2 * jp + 1) * HEAD_DIM, :]
            v1 = vT_ref[0, c, (2 * jp + 1) * HEAD_DIM:(2 * jp + 2) * HEAD_DIM, :]
            acc0 = acc0 * alpha[:, :n_cols] + jnp.dot(v0, p[:, :n_cols], preferred_element_type=F32)
            acc1 = acc1 * alpha[:, n_cols:] + jnp.dot(v1, p[:, n_cols:], preferred_element_type=F32)
            return m_new, l, acc0, acc1

        init = (jnp.full((1, 2 * n_cols), NEG_BIG, F32), jnp.zeros((1, 2 * n_cols), F32),
                jnp.zeros((HEAD_DIM, n_cols), F32), jnp.zeros((HEAD_DIM, n_cols), F32))
        m, l, acc0, acc1 = lax.fori_loop(0, nch, chunk_body, init)
        inv = 1.0 / l
        for jj, acc in enumerate((acc0, acc1)):
            o = acc * inv[:, jj * n_cols:(jj + 1) * n_cols]
            for g in range(G_D):
                r0 = (G_D * (2 * jp + jj) + g) * HEAD_DIM
                z = zT_ref[r0:r0 + HEAD_DIM, :]
                gate = z * (1.0 / (1.0 + jnp.exp(-z)))
                o_ref[r0:r0 + HEAD_DIM, :] = (o[:, g * Q_BLOCK:(g + 1) * Q_BLOCK] * gate).astype(o_ref.dtype)


def dsa_bias_windows(rel_bias, seq_len):
    o = jnp.arange(seq_len // Q_BLOCK)[:, None]
    m = jnp.arange(BIAS_WIN)[None, :]
    d = jnp.maximum(o * Q_BLOCK + m - DSA_KC, 0)
    return jnp.moveaxis(rel_bias[rel_bucket(d)].astype(F32), -1, 1)


def dsa_prompt_attend(qT, qiT, wiT, zT, ki4, k4, v4T, win, *, n_batch, seq_len, cdt=BF16):
    nqb = seq_len // Q_BLOCK
    nc = seq_len // DSA_KC
    topk = min(TOPK_MAX, seq_len // 4)
    idx_bits = int(math.log2(seq_len)) + 1
    tok = lambda b, q: (0, b * nqb + q)
    per_batch = lambda b, q: (b, 0, 0, 0)
    return pl.pallas_call(
        functools.partial(_dsa_prompt_kernel, topk=topk, idx_bits=idx_bits, cdt=cdt),
        grid=(n_batch, nqb),
        in_specs=[pl.BlockSpec((ATT_WIDTH, Q_BLOCK), tok),
                  pl.BlockSpec((IDX_HEADS * IDX_DIM, Q_BLOCK), tok),
                  pl.BlockSpec((IDX_HEADS, Q_BLOCK), tok),
                  pl.BlockSpec((ATT_WIDTH, Q_BLOCK), tok),
                  pl.BlockSpec((1, nc, DSA_KC, IDX_DIM), per_batch),
                  pl.BlockSpec((1, nc, DSA_KC, D_KV), per_batch),
                  pl.BlockSpec((1, nc, D_KV, DSA_KC), per_batch),
                  pl.BlockSpec((nqb, N_HEADS, BIAS_WIN), lambda b, q: (0, 0, 0))],
        out_specs=pl.BlockSpec((ATT_WIDTH, Q_BLOCK), tok),
        out_shape=jax.ShapeDtypeStruct((ATT_WIDTH, n_batch * seq_len), BF16),
        scratch_shapes=[pltpu.VMEM((nc, DSA_KC, Q_BLOCK), jnp.int32),
                        pltpu.VMEM((nc, DSA_KC, Q_BLOCK), F32)],
        compiler_params=pltpu.CompilerParams(
            dimension_semantics=("parallel", "arbitrary"), vmem_limit_bytes=VMEM_LIMIT),
        name="dsa_prompt_attend",
    )(qT, qiT, wiT, zT, ki4, k4, v4T, win)


DSS_NP1 = 16
DSS_NP2 = 8
DSS_TP = 8
DSS_GROWS = G_D * DSS_TP
T5_LAST_BUCKET_DIST = 1600


def _dsa_sample_select_kernel(pt_ref, qi_ref, wb_ref, kinew_ref, *rest, n_pages, n_new, topk, idx_bits, cdt):
    del pt_ref
    page_refs, mask_ref, key_sc = rest[:DSS_NP1], rest[DSS_NP1], rest[DSS_NP1 + 1]
    s = pl.program_id(1)
    lane = lax.broadcasted_iota(jnp.int32, (DSS_TP, PAGE_SIZE), 1)
    trow = lax.broadcasted_iota(jnp.int32, (DSS_TP, PAGE_SIZE), 0)
    qi = qi_ref[0]
    wb = wb_ref[0]

    def page_keys(kp):
        sc = lax.dot_general(qi, kp.astype(cdt), (((1,), (1,)), ((), ())),
                             preferred_element_type=F32) * (IDX_DIM ** -0.5)
        sc = jnp.maximum(sc, 0.0) * wb
        return _sortable_key(sc.reshape(IDX_HEADS, DSS_TP, PAGE_SIZE).sum(axis=0))

    for i in range(DSS_NP1):
        key_sc[s * DSS_NP1 + i] = page_keys(page_refs[i][0, 0])

    @pl.when(s == 0)
    def _():
        kn = page_keys(kinew_ref[0])
        key_sc[n_pages] = jnp.where(lane < n_new, jnp.where(lane <= trow, kn, INT_MIN), INT_MIN)

    @pl.when(s == n_pages // DSS_NP1 - 1)
    def _():
        all_shape = (n_pages + 1, DSS_TP, PAGE_SIZE)

        def count(pred):
            key_idx = (lax.broadcasted_iota(jnp.int32, all_shape, 0) * PAGE_SIZE
                       + lax.broadcasted_iota(jnp.int32, all_shape, 2))
            acc = pred(key_sc[...], key_idx).sum(axis=0)
            return jnp.broadcast_to(jnp.sum(acc, axis=1, keepdims=True), (DSS_TP, PAGE_SIZE))

        c_nonneg = count(lambda k, i: jnp.where(k >= 0, 1, 0))
        thr = jnp.where(c_nonneg >= topk, 0, INT_MIN).astype(jnp.int32)

        def thr_bit(b, thr):
            cand = thr + jnp.left_shift(jnp.int32(1), 30 - b)
            return jnp.where(count(lambda k, i: jnp.where(k >= cand, 1, 0)) >= topk, cand, thr)

        thr = lax.fori_loop(0, 31, thr_bit, thr)
        need = topk - count(lambda k, i: jnp.where(k > thr, 1, 0))

        def lim_bit(b, lim):
            cand = lim + jnp.left_shift(jnp.int32(1), idx_bits - 1 - b)
            c = count(lambda k, i: jnp.where(k == thr, jnp.where(i < cand, 1, 0), 0))
            return jnp.where(c <= need, cand, lim)

        n_ties = count(lambda k, i: jnp.where(k == thr, 1, 0))
        settled = jnp.min(jnp.where(thr == INT_MIN, 1, jnp.where(n_ties == need, 1, 0))) == 1
        lim = lax.cond(settled,
                       lambda: jnp.full((DSS_TP, PAGE_SIZE), 1 << idx_bits, jnp.int32),
                       lambda: lax.fori_loop(0, idx_bits, lim_bit, jnp.zeros((DSS_TP, PAGE_SIZE), jnp.int32)))

        k = key_sc[...]
        key_idx = (lax.broadcasted_iota(jnp.int32, all_shape, 0) * PAGE_SIZE
                   + lax.broadcasted_iota(jnp.int32, all_shape, 2))
        tie = jnp.where(k == thr, jnp.where(key_idx < lim, 0.0, NEG_BIG), NEG_BIG)
        mask_ref[0] = jnp.where(k == INT_MIN, NEG_BIG, jnp.where(k > thr, 0.0, tie))


def _dsa_sample_attend_kernel(pt_ref, q_ref, z_ref, mask_ref, masknew_ref, bnear_ref, bfar_ref, kvnew_ref, *rest,
                              n_pages, n_far, cdt):
    del pt_ref
    page_refs, o_ref = rest[:DSS_NP2], rest[DSS_NP2]
    m_sc, l_sc, acc_sc = rest[DSS_NP2 + 1:]
    s = pl.program_id(1)
    heads = range(KV_D)
    rep = DSS_GROWS // DSS_TP

    def attend(pages, masks, page_ids):
        n = len(pages)
        madd = jnp.concatenate([jnp.concatenate([mk] * rep, axis=0) for mk in masks], axis=1)
        kT = [jnp.concatenate([pg(0, h) for pg in pages], axis=1).astype(cdt) for h in heads]
        vT = [jnp.concatenate([pg(1, h) for pg in pages], axis=1).astype(cdt) for h in heads]
        logits = [jnp.dot(q_ref[0, h], kT[h], preferred_element_type=F32) for h in heads]
        bias = [jnp.concatenate([jnp.where(pid >= n_far, bnear_ref[jnp.maximum(pid - n_far, 0), h], bfar_ref[h])
                                 for pid in page_ids], axis=1) for h in heads]
        logits = [lg + (b + madd) for lg, b in zip(logits, bias)]
        m_old = [m_sc[h] for h in heads]
        m_new = [jnp.maximum(mo, jnp.broadcast_to(jnp.max(lg, axis=1, keepdims=True), mo.shape))
                 for mo, lg in zip(m_old, logits)]
        alpha = [jnp.exp(mo - mn) for mo, mn in zip(m_old, m_new)]
        p = [jnp.exp(lg - jnp.concatenate([mn] * n, axis=1)) for lg, mn in zip(logits, m_new)]
        pv = [lax.dot_general(ph.astype(cdt), vT[h], (((1,), (1,)), ((), ())), preferred_element_type=F32)
              for h, ph in zip(heads, p)]
        for h in heads:
            l_sc[h] = l_sc[h] * alpha[h] + jnp.broadcast_to(jnp.sum(p[h], axis=1, keepdims=True), alpha[h].shape)
            acc_sc[h] = acc_sc[h] * alpha[h][:, :HEAD_DIM] + pv[h]
            m_sc[h] = m_new[h]

    @pl.when(s == 0)
    def _():
        m_sc[...] = jnp.full_like(m_sc, NEG_BIG)
        l_sc[...] = jnp.zeros_like(l_sc)
        acc_sc[...] = jnp.zeros_like(acc_sc)
        attend([lambda c, h: kvnew_ref[0, c, h]], [masknew_ref[0, 0]], [n_pages])

    attend([(lambda c, h, r=r: r[0, 0, c, h]) for r in page_refs], [mask_ref[0, i] for i in range(DSS_NP2)],
           [s * DSS_NP2 + i for i in range(DSS_NP2)])

    @pl.when(s == n_pages // DSS_NP2 - 1)
    def _():
        for h in heads:
            z = z_ref[0, h]
            o_ref[0, h] = acc_sc[h] * (1.0 / l_sc[h][:, :HEAD_DIM]) * (z * (1.0 / (1.0 + jnp.exp(-z))))


def dsa_sample(q_s, z_s, qi_s, wi_s, ki_s, kv_s, kv_pool, kidx_pool, layer, page_table, rel_bias, cdt=BF16):
    n_dec, n_pages = page_table.shape
    dec_len = q_s.shape[0] // n_dec
    past = n_pages * PAGE_SIZE
    total = past + dec_len
    topk = min(TOPK_MAX, total // 4)
    idx_bits = int(math.log2(total)) + 1
    pad_t = DSS_TP - dec_len
    n_pairs = KV_D // 2
    eye2 = jnp.eye(2, dtype=F32)

    qi = jnp.pad(jnp.swapaxes(qi_s.reshape(n_dec, dec_len, IDX_HEADS, IDX_DIM), 1, 2), ((0, 0), (0, 0), (0, pad_t), (0, 0)))
    qi = qi.reshape(n_dec, IDX_HEADS * DSS_TP, IDX_DIM).astype(cdt)
    wb = jnp.pad(jnp.swapaxes(wi_s.reshape(n_dec, dec_len, IDX_HEADS), 1, 2) * (IDX_HEADS ** -0.5), ((0, 0), (0, 0), (0, pad_t)))
    wb = jnp.broadcast_to(wb.reshape(n_dec, IDX_HEADS * DSS_TP, 1), (n_dec, IDX_HEADS * DSS_TP, PAGE_SIZE))
    ki_new = jnp.pad(ki_s.reshape(n_dec, dec_len, IDX_DIM), ((0, 0), (0, PAGE_SIZE - dec_len), (0, 0)))
    kidx4 = kidx_pool.reshape(kidx_pool.shape[0], kidx_pool.shape[1], PAGE_SIZE, IDX_DIM)
    page_spec = lambda np_, i, width: pl.BlockSpec(
        (1, 1, PAGE_SIZE, width), lambda b, s, pt: (layer, pt[b, s * np_ + i], 0, 0))
    per_b3 = lambda b, s, pt: (b, 0, 0)
    mask = pl.pallas_call(
        functools.partial(_dsa_sample_select_kernel, n_pages=n_pages, n_new=dec_len, topk=topk, idx_bits=idx_bits,
                          cdt=cdt),
        grid_spec=pltpu.PrefetchScalarGridSpec(
            num_scalar_prefetch=1, grid=(n_dec, n_pages // DSS_NP1),
            in_specs=[pl.BlockSpec((1, IDX_HEADS * DSS_TP, IDX_DIM), per_b3),
                      pl.BlockSpec((1, IDX_HEADS * DSS_TP, PAGE_SIZE), per_b3),
                      pl.BlockSpec((1, PAGE_SIZE, IDX_DIM), per_b3)]
                     + [page_spec(DSS_NP1, i, IDX_DIM) for i in range(DSS_NP1)],
            out_specs=pl.BlockSpec((1, n_pages + 1, DSS_TP, PAGE_SIZE), lambda b, s, pt: (b, 0, 0, 0)),
            scratch_shapes=[pltpu.VMEM((n_pages + 1, DSS_TP, PAGE_SIZE), jnp.int32)]),
        out_shape=jax.ShapeDtypeStruct((n_dec, n_pages + 1, DSS_TP, PAGE_SIZE), F32),
        compiler_params=pltpu.CompilerParams(
            dimension_semantics=("parallel", "arbitrary"), vmem_limit_bytes=VMEM_LIMIT),
        name="dsa_sample_select",
    )(page_table, qi, wb, ki_new, *([kidx4] * DSS_NP1))

    def head_rows(t, scale):
        t = t.reshape(n_dec, dec_len, KV_D, G_D, HEAD_DIM).transpose(0, 2, 3, 1, 4) * scale
        return jnp.pad(t, ((0, 0),) * 3 + ((0, pad_t), (0, 0))).reshape(n_dec, KV_D, DSS_GROWS, HEAD_DIM)

    q_hr = head_rows(q_s, HEAD_DIM ** -0.5).astype(cdt)
    z_hr = head_rows(z_s, 1.0)
    n_far = max(0, min(n_pages, (past - (PAGE_SIZE - 1) - T5_LAST_BUCKET_DIST) // PAGE_SIZE + 1))
    near_pages = jnp.arange(n_far, n_pages + 1)
    dist = (past + jnp.arange(DSS_TP)[None, :, None]
            - (near_pages[:, None, None] * PAGE_SIZE + jnp.arange(PAGE_SIZE)[None, None, :]))
    b_near = rel_bias[rel_bucket(dist)].astype(F32)
    b_near = b_near.transpose(0, 3, 1, 2).reshape(n_pages + 1 - n_far, KV_D, DSS_GROWS, PAGE_SIZE)
    b_far = jnp.broadcast_to(rel_bias[N_BUCKETS - 1].astype(F32)[:, None, None], (N_HEADS, DSS_TP, PAGE_SIZE))
    b_far = b_far.reshape(KV_D, DSS_GROWS, PAGE_SIZE)
    kv_new = jnp.pad(kv_s.reshape(n_dec, dec_len, 2, KV_D, HEAD_DIM).transpose(0, 2, 3, 4, 1),
                     ((0, 0),) * 4 + ((0, PAGE_SIZE - dec_len),))
    kv_t = kv_pool.reshape(kv_pool.shape[0], kv_pool.shape[1], PAGE_SIZE, 2, KV_D, HEAD_DIM).transpose(0, 1, 3, 4, 5, 2)
    kv_page = lambda i: pl.BlockSpec((1, 1, 2, KV_D, HEAD_DIM, PAGE_SIZE),
                                     lambda b, s, pt: (layer, pt[b, s * DSS_NP2 + i], 0, 0, 0, 0))
    per_b4 = lambda b, s, pt: (b, 0, 0, 0)
    o = pl.pallas_call(
        functools.partial(_dsa_sample_attend_kernel, n_pages=n_pages, n_far=n_far, cdt=cdt),
        grid_spec=pltpu.PrefetchScalarGridSpec(
            num_scalar_prefetch=1, grid=(n_dec, n_pages // DSS_NP2),
            in_specs=[pl.BlockSpec((1, KV_D, DSS_GROWS, HEAD_DIM), per_b4),
                      pl.BlockSpec((1, KV_D, DSS_GROWS, HEAD_DIM), per_b4),
                      pl.BlockSpec((1, DSS_NP2, DSS_TP, PAGE_SIZE), lambda b, s, pt: (b, s, 0, 0)),
                      pl.BlockSpec((1, 1, DSS_TP, PAGE_SIZE), lambda b, s, pt: (b, n_pages, 0, 0)),
                      pl.BlockSpec(b_near.shape, lambda b, s, pt: (0, 0, 0, 0)),
                      pl.BlockSpec(b_far.shape, lambda b, s, pt: (0, 0, 0)),
                      pl.BlockSpec((1, 2, KV_D, HEAD_DIM, PAGE_SIZE), lambda b, s, pt: (b, 0, 0, 0, 0))]
                     + [kv_page(i) for i in range(DSS_NP2)],
            out_specs=pl.BlockSpec((1, KV_D, DSS_GROWS, HEAD_DIM), per_b4),
            scratch_shapes=[pltpu.VMEM((KV_D, DSS_GROWS, LANE), F32), pltpu.VMEM((KV_D, DSS_GROWS, LANE), F32),
                            pltpu.VMEM((KV_D, DSS_GROWS, HEAD_DIM), F32)]),
        out_shape=jax.ShapeDtypeStruct((n_dec, KV_D, DSS_GROWS, HEAD_DIM), F32),
        compiler_params=pltpu.CompilerParams(
            dimension_semantics=("parallel", "arbitrary"), vmem_limit_bytes=VMEM_LIMIT),
        name="dsa_sample_attend",
    )(page_table, q_hr, z_hr, mask, mask, b_near, b_far, kv_new, *([kv_t] * DSS_NP2))
    o = o.reshape(n_dec, KV_D, G_D, DSS_TP, HEAD_DIM)[:, :, :, :dec_len]
    return o.transpose(0, 3, 1, 2, 4).reshape(n_dec * dec_len, ATT_WIDTH)


def _pad_cols(w, n):
    return jnp.pad(w, ((0, 0), (0, n - w.shape[1])))


def kernel(x_prompt, x_sample, cache_a_kv, state_s5, state_gdn, state_gdn_conv, cache_d_kv, cache_d_kidx,
           page_table, p_prompt, p_sample, rel_bias, ln_g, ln_b, ple_gate_w, ple_w,
           a_w_in, a_sinks, a_w_out,
           s5_w_in, s5_a_re, s5_a_im, s5_b_re, s5_b_im, s5_c_re, s5_c_im, s5_d, s5_log_dt, s5_w_glu, s5_w_out,
           gdn_w_in, gdn_conv_w, gdn_a_log, gdn_dt_bias, gdn_norm_w, gdn_w_out,
           dsa_w_in, dsa_w_out):
    x = join_tokens(x_prompt, x_sample)
    x_bf = x.astype(BF16)
    outs = {}

    def post(i, x, h):
        return post_norm_ple(x, h, join_tokens(p_prompt[i], p_sample[i]), ln_g[i], ln_b[i], ple_gate_w[i], ple_w[i])

    w_in = a_w_in[0]
    c_k, c_v, c_z = ATT_WIDTH, ATT_WIDTH + A_KV, ATT_WIDTH + 2 * A_KV
    kv_nat = matmul(x_bf, w_in, c_k, 2 * A_KV)
    xT_bf = x_bf[:N_PROMPT_TOK].T
    x_s = x_bf[N_PROMPT_TOK:]
    qT = matmul_wt(w_in, xT_bf, 0, ATT_WIDTH, out_dtype=BF16)
    vT = matmul_wt(w_in, xT_bf, c_v, A_KV, out_dtype=BF16)
    zT = matmul_wt(w_in, xT_bf, c_z, ATT_WIDTH)
    bias_p, sink_p, bias_s, sink_s = swa_tables(rel_bias, a_sinks[0], DEC_SEQ)
    h_p = matmul_ta(swa_prompt(qT, vT, zT, kv_nat, bias_p, sink_p, n_batch=BATCH, seq_len=SEQ), a_w_out[0])
    gs, outs['a_s'] = swa_sample(matmul(x_s, w_in, 0, ATT_WIDTH), matmul(x_s, w_in, c_z, ATT_WIDTH),
                                 kv_nat[N_PROMPT_TOK:], cache_a_kv[0], bias_s, sink_s)
    outs['a_p'] = kv_nat[:N_PROMPT_TOK].reshape(BATCH, SEQ, 2, KV_A, HEAD_DIM)[:, SEQ - WINDOW:]
    x, x_bf = post(0, x, jnp.concatenate([h_p, matmul(gs.astype(BF16), a_w_out[0])], axis=0))

    proj = matmul(x_bf, s5_w_in[0])
    tables = s5_tables(s5_a_re[0], s5_a_im[0], s5_b_re[0], s5_b_im[0], s5_c_re[0], s5_c_im[0], s5_log_dt[0])
    gp, gs, outs['s5_p'], outs['s5_s'] = s5_layer(proj, state_s5[0], tables, s5_d[0], s5_w_glu[0],
                                                  n_batch=BATCH, seq_len=SEQ, n_dec=DEC_BATCH, dec_len=DEC_SEQ)
    x, x_bf = post(1, x, jnp.concatenate([matmul(gp, s5_w_out[0]), matmul(gs, s5_w_out[0])], axis=0))

    w_in = gdn_w_in[0]
    c_gz = GDN_CONV_CH + GDN_V_WIDTH
    qkvz = matmul(x_bf, w_in, 0, c_gz)
    ab = matmul(x_bf, _pad_cols(w_in[:, c_gz:], LANE))
    gp, gs, outs['gd_p'], outs['gd_s'], outs['gc_p'], outs['gc_s'] = gdn_layer(
        qkvz, ab, state_gdn[0], state_gdn_conv[0], gdn_conv_w[0], gdn_a_log[0], gdn_dt_bias[0], gdn_norm_w[0],
        n_batch=BATCH, seq_len=SEQ, n_dec=DEC_BATCH, dec_len=DEC_SEQ)
    x, x_bf = post(2, x, jnp.concatenate([matmul(gp, gdn_w_out[0]), matmul(gs.astype(BF16), gdn_w_out[0])], axis=0))

    w_in = dsa_w_in[0]
    c_kv = ATT_WIDTH + 2 * D_KV
    c_z = 2 * ATT_WIDTH + 2 * D_KV
    c_qi = c_z + IDX_HEADS * IDX_DIM
    kv_nat = matmul(x_bf, w_in, ATT_WIDTH, 2 * D_KV)
    kiw = matmul(x_bf, _pad_cols(w_in[:, c_qi:], 2 * LANE))
    xT_bf = x_bf[:N_PROMPT_TOK].T
    x_s = x_bf[N_PROMPT_TOK:]
    qT = matmul_wt(w_in, xT_bf, 0, ATT_WIDTH, out_dtype=BF16)
    qiT = matmul_wt(w_in, xT_bf, c_z, IDX_HEADS * IDX_DIM, out_dtype=BF16)
    zT = matmul_wt(w_in, xT_bf, c_kv, ATT_WIDTH)
    wiT = matmul_wt(_pad_cols(w_in[:, c_qi + IDX_DIM:], LANE), xT_bf, 0, LANE)
    nc = SEQ // DSA_KC
    kv_p = kv_nat[:N_PROMPT_TOK]
    v4T = jnp.swapaxes(kv_p[:, D_KV:].astype(BF16).reshape(BATCH, nc, DSA_KC, D_KV), 2, 3)
    gT = dsa_prompt_attend(qT, qiT, wiT, zT, kiw[:N_PROMPT_TOK].reshape(BATCH, nc, DSA_KC, 2 * LANE),
                           kv_p.reshape(BATCH, nc, DSA_KC, 2 * D_KV), v4T, dsa_bias_windows(rel_bias, SEQ),
                           n_batch=BATCH, seq_len=SEQ)
    h_p = matmul_ta(gT, dsa_w_out[0])
    kiw_s = kiw[N_PROMPT_TOK:]
    gs = dsa_sample(matmul(x_s, w_in, 0, ATT_WIDTH), matmul(x_s, w_in, c_kv, ATT_WIDTH),
                    matmul(x_s, w_in, c_z, IDX_HEADS * IDX_DIM),
                    kiw_s[:, IDX_DIM:IDX_DIM + IDX_HEADS], kiw_s[:, :IDX_DIM], kv_nat[N_PROMPT_TOK:],
                    cache_d_kv, cache_d_kidx, 0, page_table, rel_bias)
    h_s = matmul(gs.astype(BF16), dsa_w_out[0])
    outs['dkv_p'] = kv_p.reshape(BATCH, SEQ, 2, KV_D, HEAD_DIM)
    outs['dkv_s'] = kv_nat[N_PROMPT_TOK:].reshape(DEC_BATCH, DEC_SEQ, 2, KV_D, HEAD_DIM)
    outs['dki_p'] = kiw[:N_PROMPT_TOK, :IDX_DIM].reshape(BATCH, SEQ, IDX_DIM)
    outs['dki_s'] = kiw_s[:, :IDX_DIM].reshape(DEC_BATCH, DEC_SEQ, IDX_DIM)
    x, x_bf = post(3, x, jnp.concatenate([h_p, h_s], axis=0))

    yp, ys = split_tokens(x)
    st = lambda name: outs[name][None]
    return (yp, ys, st('a_p'), st('a_s'), st('s5_p'), st('s5_s'), st('gd_p'), st('gd_s'),
            st('gc_p'), st('gc_s'), st('dkv_p'), st('dkv_s'), st('dki_p'), st('dki_s'))
```

```python
import functools
import math

import jax
import jax.numpy as jnp
from jax import lax
from jax.experimental import pallas as pl
from jax.experimental.pallas import tpu as pltpu

D_MODEL = 2048
BATCH = 4
SEQ = 2048
DEPTH = 4
DEC_BATCH = 32
DEC_SEQ = 4
PAGE_SIZE = 128
N_MIXERS = 4
PLE_DIM = 256
ALPHA = (2 * DEPTH) ** 0.25
LN_EPS = 1e-5
N_BUCKETS = 32
REL_MAX_DIST = 2048
N_HEADS = 32
HEAD_DIM = 64
ATT_WIDTH = N_HEADS * HEAD_DIM
WINDOW = 128
KV_A = 4
A_KV = KV_A * HEAD_DIM
KV_D = 8
D_KV = KV_D * HEAD_DIM
IDX_HEADS = 16
IDX_DIM = 128
TOPK_MAX = 256
Q_BLOCK = 128
S5_WIDTH = D_MODEL
S5_GROUP = 16
S5_GROUPS = S5_WIDTH // S5_GROUP
S5_STATE = 64
GDN_QK_HEADS = 16
GDN_V_HEADS = 32
GDN_DK = 128
GDN_DV = 128
GDN_CONV = 4
GDN_CHUNK = 64
GDN_QK_WIDTH = GDN_QK_HEADS * GDN_DK
GDN_V_WIDTH = GDN_V_HEADS * GDN_DV
GDN_CONV_CH = 2 * GDN_QK_WIDTH + GDN_V_WIDTH

F32 = jnp.float32
BF16 = jnp.bfloat16

N_PROMPT_TOK = BATCH * SEQ
N_SAMPLE_TOK = DEC_BATCH * DEC_SEQ
N_TOK = N_PROMPT_TOK + N_SAMPLE_TOK

V7X_VMEM_BYTES = 64 * 1024 * 1024
VMEM_LIMIT = 48 * 1024 * 1024
LANE = 128


def _mm_kernel(x_ref, w_ref, o_ref):
    o_ref[...] = jnp.dot(x_ref[...], w_ref[...].astype(BF16), preferred_element_type=F32).astype(o_ref.dtype)


def _pick_tile(n, prefs, also=0):
    for t in prefs:
        if n % t == 0 and also % t == 0:
            return t
    raise ValueError(f"no tile for {n} (offset {also})")


def matmul(x, w, col0=0, n=None, out_dtype=F32):
    m, k = x.shape
    n = w.shape[1] - col0 if n is None else n
    tm = _pick_tile(m, (640, 512, 320, 256, 128, 64, 32, 16, 8))
    tn = _pick_tile(n, (512, 384, 256, 128), col0)
    c0 = col0 // tn
    return pl.pallas_call(
        _mm_kernel,
        grid=(m // tm, n // tn),
        in_specs=[pl.BlockSpec((tm, k), lambda i, j: (i, 0)),
                  pl.BlockSpec((k, tn), lambda i, j: (0, c0 + j))],
        out_specs=pl.BlockSpec((tm, tn), lambda i, j: (i, j)),
        out_shape=jax.ShapeDtypeStruct((m, n), out_dtype),
        compiler_params=pltpu.CompilerParams(
            dimension_semantics=("parallel", "parallel"), vmem_limit_bytes=VMEM_LIMIT),
        name="proj_matmul",
    )(x, w)


def _mm_wt_kernel(w_ref, xt_ref, o_ref, wt_sc):
    @pl.when(pl.program_id(1) == 0)
    def _():
        wt_sc[...] = w_ref[...].astype(F32).T.astype(BF16)

    o_ref[...] = jnp.dot(wt_sc[...], xt_ref[...], preferred_element_type=F32).astype(o_ref.dtype)


def matmul_wt(w, xt, col0, n, out_dtype=F32):
    k, m = xt.shape
    tr = _pick_tile(n, (512, 256, 128), col0)
    tt = _pick_tile(m, (512, 256, 128))
    r0 = col0 // tr
    return pl.pallas_call(
        _mm_wt_kernel,
        grid=(n // tr, m // tt),
        in_specs=[pl.BlockSpec((k, tr), lambda i, j: (0, r0 + i)),
                  pl.BlockSpec((k, tt), lambda i, j: (0, j))],
        out_specs=pl.BlockSpec((tr, tt), lambda i, j: (i, j)),
        out_shape=jax.ShapeDtypeStruct((n, m), out_dtype),
        scratch_shapes=[pltpu.VMEM((tr, k), BF16)],
        compiler_params=pltpu.CompilerParams(
            dimension_semantics=("parallel", "arbitrary"), vmem_limit_bytes=VMEM_LIMIT),
        name="proj_matmul_wt",
    )(w, xt)


def _mm_ta_kernel(xt_ref, w_ref, o_ref):
    o_ref[...] = lax.dot_general(xt_ref[...], w_ref[...].astype(BF16), (((0,), (0,)), ((), ())),
                                 preferred_element_type=F32).astype(o_ref.dtype)


def matmul_ta(xt, w, out_dtype=F32):
    k, m = xt.shape
    n = w.shape[1]
    tm = _pick_tile(m, (512, 256, 128))
    tn = _pick_tile(n, (512, 384, 256, 128))
    return pl.pallas_call(
        _mm_ta_kernel,
        grid=(m // tm, n // tn),
        in_specs=[pl.BlockSpec((k, tm), lambda i, j: (0, i)),
                  pl.BlockSpec((k, tn), lambda i, j: (0, j))],
        out_specs=pl.BlockSpec((tm, tn), lambda i, j: (i, j)),
        out_shape=jax.ShapeDtypeStruct((m, n), out_dtype),
        compiler_params=pltpu.CompilerParams(
            dimension_semantics=("parallel", "parallel"), vmem_limit_bytes=VMEM_LIMIT),
        name="proj_matmul_ta",
    )(xt, w)


POST_TM = 320
POST_TN = 512


def _post_kernel(x_ref, h_ref, p_ref, g_ref, b_ref, wg_ref, wp_ref, o_ref, obf_ref, y_sc, ybf_sc):
    j = pl.program_id(1)

    @pl.when(j == 0)
    def _():
        t = ALPHA * x_ref[...] + h_ref[...]
        mu = jnp.mean(t, axis=-1, keepdims=True)
        d = t - mu
        var = jnp.mean(d * d, axis=-1, keepdims=True)
        y = d * lax.rsqrt(var + LN_EPS) * g_ref[...] + b_ref[...]
        ybf_sc[...] = y.astype(BF16)
        for jj in range(D_MODEL // POST_TN):
            y_sc[jj] = y[:, jj * POST_TN:(jj + 1) * POST_TN]

    gate = jnp.dot(ybf_sc[...], wg_ref[...].astype(BF16), preferred_element_type=F32)
    ple = jnp.dot(p_ref[...].astype(BF16), wp_ref[...].astype(BF16), preferred_element_type=F32)
    o = y_sc[j] + (1.0 / (1.0 + jnp.exp(-gate))) * ple
    o_ref[...] = o
    obf_ref[...] = o.astype(BF16)


def post_norm_ple(x, h, p_bf, g, b, wg_bf, wp_bf):
    m = x.shape[0]
    tm, tn = POST_TM, POST_TN
    return pl.pallas_call(
        _post_kernel,
        grid=(m // tm, D_MODEL // tn),
        in_specs=[pl.BlockSpec((tm, D_MODEL), lambda i, j: (i, 0)),
                  pl.BlockSpec((tm, D_MODEL), lambda i, j: (i, 0)),
                  pl.BlockSpec((tm, PLE_DIM), lambda i, j: (i, 0)),
                  pl.BlockSpec((1, D_MODEL), lambda i, j: (0, 0)),
                  pl.BlockSpec((1, D_MODEL), lambda i, j: (0, 0)),
                  pl.BlockSpec((D_MODEL, tn), lambda i, j: (0, j)),
                  pl.BlockSpec((PLE_DIM, tn), lambda i, j: (0, j))],
        out_specs=[pl.BlockSpec((tm, tn), lambda i, j: (i, j)),
                   pl.BlockSpec((tm, tn), lambda i, j: (i, j))],
        out_shape=[jax.ShapeDtypeStruct((m, D_MODEL), F32),
                   jax.ShapeDtypeStruct((m, D_MODEL), BF16)],
        scratch_shapes=[pltpu.VMEM((D_MODEL // tn, tm, tn), F32),
                        pltpu.VMEM((tm, D_MODEL), BF16)],
        compiler_params=pltpu.CompilerParams(
            dimension_semantics=("parallel", "arbitrary"), vmem_limit_bytes=VMEM_LIMIT),
        name="post_norm_ple",
    )(x, h, p_bf, g.reshape(1, D_MODEL), b.reshape(1, D_MODEL), wg_bf, wp_bf)


def rel_bucket(dist):
    n = jnp.maximum(dist, 0)
    exact = N_BUCKETS // 2
    logb = exact + (jnp.log(jnp.maximum(n, exact).astype(F32) / exact)
                    / math.log(REL_MAX_DIST / exact) * (N_BUCKETS - exact)).astype(jnp.int32)
    return jnp.where(n < exact, n, jnp.minimum(logb, N_BUCKETS - 1))


def split_tokens(t):
    c = t.shape[-1]
    return (t[:N_PROMPT_TOK].reshape(BATCH, SEQ, c), t[N_PROMPT_TOK:].reshape(DEC_BATCH, DEC_SEQ, c))


def join_tokens(tp, ts):
    c = tp.shape[-1]
    return jnp.concatenate([tp.reshape(N_PROMPT_TOK, c), ts.reshape(N_SAMPLE_TOK, c)], axis=0)


G_A = N_HEADS // KV_A
SWA_KEYS = 2 * WINDOW


def swa_tables(rel_bias, sinks, dec_len):
    def heads_to(b, lead):
        return jnp.moveaxis(b, -1, 0).reshape((KV_A, G_A) + lead)

    dist = jnp.arange(WINDOW)[None, :] - (jnp.arange(SWA_KEYS)[:, None] - WINDOW)
    ok = (dist >= 0) & (dist < WINDOW)
    b = jnp.where(ok[..., None], rel_bias[rel_bucket(dist)].astype(F32), NEG_BIG)
    bias_p = heads_to(b, (SWA_KEYS, WINDOW)).transpose(0, 2, 1, 3).reshape(KV_A, SWA_KEYS, G_A * WINDOW)
    sink_p = jnp.broadcast_to(sinks.astype(F32).reshape(KV_A, 1, G_A, 1), (KV_A, 1, G_A, WINDOW))
    sink_p = sink_p.reshape(KV_A, 1, G_A * WINDOW)
    key_i = jnp.arange(SWA_KEYS)[None, :]
    dist = jnp.arange(dec_len)[:, None] + WINDOW - key_i
    ok = (dist >= 0) & (dist < WINDOW) & (key_i < WINDOW + dec_len)
    b = jnp.where(ok[..., None], rel_bias[rel_bucket(dist)].astype(F32), NEG_BIG)
    bias_s = heads_to(b, (dec_len, SWA_KEYS)).reshape(KV_A, G_A * dec_len, SWA_KEYS)
    sink_s = jnp.broadcast_to(sinks.astype(F32).reshape(KV_A, G_A, 1, 1), (KV_A, G_A, dec_len, LANE))
    sink_s = sink_s.reshape(KV_A, G_A * dec_len, LANE)
    return bias_p, sink_p, bias_s, sink_s


def _swa_prompt_kernel(qT_ref, zT_ref, vTp_ref, vTc_ref, kp_ref, kc_ref, bias_ref, sink_ref, o_ref, *, cdt):
    first = pl.program_id(1) == 0
    kk = jnp.concatenate([kp_ref[...], kc_ref[...]], axis=0)
    vT = jnp.concatenate([vTp_ref[...], vTc_ref[...]], axis=1)
    prev_key = lax.broadcasted_iota(jnp.int32, (SWA_KEYS, G_A * WINDOW), 0) < WINDOW
    pw = 2 * HEAD_DIM
    for j in range(KV_A):
        kpair = kk[:, (j // 2) * pw:(j // 2 + 1) * pw].astype(cdt)
        qj = jnp.concatenate([qT_ref[(G_A * j + g) * HEAD_DIM:(G_A * j + g + 1) * HEAD_DIM, :]
                              for g in range(G_A)], axis=1)
        qj = (qj.astype(F32) * (HEAD_DIM ** -0.5)).astype(cdt)
        zpad = jnp.zeros_like(qj)
        rhs = jnp.concatenate([qj, zpad] if j % 2 == 0 else [zpad, qj], axis=0)
        s = jnp.dot(kpair, rhs, preferred_element_type=F32) + bias_ref[j]
        s = jnp.where(prev_key, jnp.where(first, NEG_BIG, s), s)
        sink = sink_ref[j]
        m = jnp.maximum(jnp.max(s, axis=0, keepdims=True), sink)
        e = jnp.exp(s - m)
        den = jnp.sum(e, axis=0, keepdims=True) + jnp.exp(sink - m)
        p = (e * (1.0 / den)).astype(cdt)
        acc = jnp.dot(vT[j * HEAD_DIM:(j + 1) * HEAD_DIM, :].astype(cdt), p, preferred_element_type=F32)
        for g in range(G_A):
            r0 = (G_A * j + g) * HEAD_DIM
            z = zT_ref[r0:r0 + HEAD_DIM, :]
            o_ref[r0:r0 + HEAD_DIM, :] = (acc[:, g * WINDOW:(g + 1) * WINDOW]
                                          * (z * (1.0 / (1.0 + jnp.exp(-z))))).astype(o_ref.dtype)


def swa_prompt(qT, vT, zT, kv_nat, bias_p, sink_p, *, n_batch, seq_len, cdt=BF16):
    nb = seq_len // WINDOW
    cur = lambda b, i: b * nb + i
    prev = lambda b, i: b * nb + jnp.maximum(i - 1, 0)
    v_row_blk = 0
    return pl.pallas_call(
        functools.partial(_swa_prompt_kernel, cdt=cdt),
        grid=(n_batch, nb),
        in_specs=[pl.BlockSpec((ATT_WIDTH, WINDOW), lambda b, i: (0, cur(b, i))),
                  pl.BlockSpec((ATT_WIDTH, WINDOW), lambda b, i: (0, cur(b, i))),
                  pl.BlockSpec((A_KV, WINDOW), lambda b, i: (v_row_blk, prev(b, i))),
                  pl.BlockSpec((A_KV, WINDOW), lambda b, i: (v_row_blk, cur(b, i))),
                  pl.BlockSpec((WINDOW, A_KV), lambda b, i: (prev(b, i), 0)),
                  pl.BlockSpec((WINDOW, A_KV), lambda b, i: (cur(b, i), 0)),
                  pl.BlockSpec(bias_p.shape, lambda b, i: (0, 0, 0)),
                  pl.BlockSpec(sink_p.shape, lambda b, i: (0, 0, 0))],
        out_specs=pl.BlockSpec((ATT_WIDTH, WINDOW), lambda b, i: (0, cur(b, i))),
        out_shape=jax.ShapeDtypeStruct((ATT_WIDTH, n_batch * seq_len), BF16),
        compiler_params=pltpu.CompilerParams(
            dimension_semantics=("parallel", "parallel"), vmem_limit_bytes=VMEM_LIMIT),
        name="swa_prompt",
    )(qT, zT, vT, vT, kv_nat, kv_nat, bias_p, sink_p)


def _swa_sample_kernel(q_ref, z_ref, k_ref, v_ref, bias_ref, sink_ref, o_ref, *, cdt):
    for j in range(KV_A):
        s = lax.dot_general(q_ref[0, j], k_ref[0, j].astype(cdt), (((1,), (1,)), ((), ())),
                            preferred_element_type=F32) + bias_ref[j]
        sink = sink_ref[j][:, 0:1]
        m = jnp.maximum(jnp.max(s, axis=1, keepdims=True), sink)
        e = jnp.exp(s - m)
        den = jnp.sum(e, axis=1, keepdims=True) + jnp.exp(sink - m)
        p = (e * (1.0 / den)).astype(cdt)
        z = z_ref[0, j]
        o_ref[0, j] = jnp.dot(p, v_ref[0, j].astype(cdt), preferred_element_type=F32) * (z * (1.0 / (1.0 + jnp.exp(-z))))


def swa_sample(q_s, z_s, kv_s, kv_cache, bias_s, sink_s, cdt=BF16):
    n_dec = kv_cache.shape[0]
    dec_len = q_s.shape[0] // n_dec
    rows = G_A * dec_len

    def head_rows(t, scale):
        t = t.reshape(n_dec, dec_len, KV_A, G_A, HEAD_DIM).transpose(0, 2, 3, 1, 4) * scale
        return jnp.pad(t.reshape(n_dec, KV_A, rows, HEAD_DIM), ((0, 0), (0, 0), (0, 0), (0, HEAD_DIM)))

    new = kv_s.reshape(n_dec, dec_len, 2, KV_A, HEAD_DIM)
    cat = jnp.concatenate([kv_cache, new], axis=1)
    keys = jnp.pad(cat.transpose(2, 0, 3, 1, 4),
                   ((0, 0), (0, 0), (0, 0), (0, SWA_KEYS - WINDOW - dec_len), (0, HEAD_DIM)))
    blk = lambda r: pl.BlockSpec((1, KV_A, r, 2 * HEAD_DIM), lambda b: (b, 0, 0, 0))
    o = pl.pallas_call(
        functools.partial(_swa_sample_kernel, cdt=cdt),
        grid=(n_dec,),
        in_specs=[blk(rows), blk(rows), blk(SWA_KEYS), blk(SWA_KEYS),
                  pl.BlockSpec(bias_s.shape, lambda b: (0, 0, 0)),
                  pl.BlockSpec(sink_s.shape, lambda b: (0, 0, 0))],
        out_specs=blk(rows),
        out_shape=jax.ShapeDtypeStruct((n_dec, KV_A, rows, 2 * HEAD_DIM), F32),
        compiler_params=pltpu.CompilerParams(dimension_semantics=("parallel",), vmem_limit_bytes=VMEM_LIMIT),
        name="swa_sample",
    )(head_rows(q_s, HEAD_DIM ** -0.5).astype(cdt), head_rows(z_s, 1.0), keys[0], keys[1], bias_s, sink_s)
    o = o[..., :HEAD_DIM].reshape(n_dec, KV_A, G_A, dec_len, HEAD_DIM).transpose(0, 3, 1, 2, 4)
    return o.reshape(n_dec * dec_len, ATT_WIDTH), cat[:, dec_len:]


S5_SLAB_G = 8
S5_SLAB_CH = S5_SLAB_G * S5_GROUP
S5_SLAB_ST = S5_SLAB_G * S5_STATE
S5_N_SLABS = S5_GROUPS // S5_SLAB_G
S5_CHAINS = 8
S5_HALF_CH = S5_CHAINS * S5_SLAB_CH
S5_T = 256
S5_LT = 2 * S5_SLAB_ST // LANE


def _gelu_tanh(x):
    return 0.5 * x * (1.0 + jnp.tanh(math.sqrt(2.0 / math.pi) * (x + 0.044715 * (x * x * x))))


def s5_tables(a_re, a_im, b_re, b_im, c_re, c_im, log_dt):
    a = lax.complex(a_re, a_im)
    dt = jnp.exp(log_dt)[:, None]
    a_bar = jnp.exp(a * dt)
    b_bar = ((a_bar - 1.0) / a)[..., None] * lax.complex(b_re, b_im)
    eye = jnp.eye(S5_SLAB_G, dtype=F32)

    def b_blk(t):
        t = t.reshape(S5_N_SLABS, S5_SLAB_G, S5_STATE, S5_GROUP)
        return jnp.einsum('ij,sipc->sicjp', eye, t).reshape(S5_N_SLABS, S5_SLAB_CH, S5_SLAB_ST)

    def c_blk(t):
        t = t.reshape(S5_N_SLABS, S5_SLAB_G, S5_GROUP, S5_STATE)
        return jnp.einsum('ij,sicp->sjpic', eye, t).reshape(S5_N_SLABS, S5_SLAB_ST, S5_SLAB_CH)

    bcat = jnp.concatenate([b_blk(b_bar.real), b_blk(b_bar.imag)], axis=2)
    ccat = jnp.concatenate([c_blk(c_re), -c_blk(c_im)], axis=1)
    a_cat = jnp.concatenate([a_bar.real.reshape(S5_N_SLABS, S5_SLAB_ST),
                             a_bar.imag.reshape(S5_N_SLABS, S5_SLAB_ST)], axis=1)
    return a_cat, bcat, ccat


def _s5_prompt_kernel(u_ref, bcat_ref, ccat_ref, a_ref, d_ref, y_ref, hout_ref, sc, h_sc, *, cdt):
    tc = pl.program_id(2)
    n_lt_half = S5_LT // 2

    @pl.when(tc == 0)
    def _():
        h_sc[...] = jnp.zeros_like(h_sc)

    for j in range(S5_CHAINS):
        uj = u_ref[:, j * S5_SLAB_CH:(j + 1) * S5_SLAB_CH].astype(cdt)
        bu = jnp.dot(uj, bcat_ref[0, j], preferred_element_type=F32)
        for lt in range(S5_LT):
            sc[lt, pl.ds(j, S5_T, stride=S5_CHAINS), :] = bu[:, lt * LANE:(lt + 1) * LANE]

    a_re = [a_ref[0, :, lt * LANE:(lt + 1) * LANE] for lt in range(n_lt_half)]
    a_im = [a_ref[0, :, (n_lt_half + lt) * LANE:(n_lt_half + lt + 1) * LANE] for lt in range(n_lt_half)]

    def step(t, h):
        r0 = pl.multiple_of(t * S5_CHAINS, S5_CHAINS)
        new = list(h)
        for lt in range(n_lt_half):
            hr, hi = h[lt], h[n_lt_half + lt]
            nr = a_re[lt] * hr - a_im[lt] * hi + sc[lt, pl.ds(r0, S5_CHAINS), :]
            ni = a_re[lt] * hi + a_im[lt] * hr + sc[n_lt_half + lt, pl.ds(r0, S5_CHAINS), :]
            sc[lt, pl.ds(r0, S5_CHAINS), :] = nr
            sc[n_lt_half + lt, pl.ds(r0, S5_CHAINS), :] = ni
            new[lt], new[n_lt_half + lt] = nr, ni
        return tuple(new)

    h = lax.fori_loop(0, S5_T, step, tuple(h_sc[lt] for lt in range(S5_LT)), unroll=8)
    for lt in range(S5_LT):
        h_sc[lt] = h[lt]
        hout_ref[0, 0, :, lt * LANE:(lt + 1) * LANE] = h[lt]

    for j in range(S5_CHAINS):
        hcat = jnp.concatenate([sc[lt, pl.ds(j, S5_T, stride=S5_CHAINS), :] for lt in range(S5_LT)], axis=1)
        cols = slice(j * S5_SLAB_CH, (j + 1) * S5_SLAB_CH)
        y = jnp.dot(hcat.astype(cdt), ccat_ref[0, j], preferred_element_type=F32) + d_ref[0, :, cols] * u_ref[:, cols]
        y_ref[:, cols] = _gelu_tanh(y)


def s5_prompt(proj, a_cat, bcat, ccat, d_skip, *, n_batch, seq_len, n_rows_out, cdt=BF16):
    n_t = seq_len // S5_T
    n_half = S5_WIDTH // S5_HALF_CH
    half = lambda t: t.reshape((n_half, S5_CHAINS) + t.shape[1:])
    return pl.pallas_call(
        functools.partial(_s5_prompt_kernel, cdt=cdt),
        grid=(n_batch, n_half, n_t),
        in_specs=[pl.BlockSpec((S5_T, S5_HALF_CH), lambda b, hf, t: (b * n_t + t, hf)),
                  pl.BlockSpec((1, S5_CHAINS, S5_SLAB_CH, 2 * S5_SLAB_ST), lambda b, hf, t: (hf, 0, 0, 0)),
                  pl.BlockSpec((1, S5_CHAINS, 2 * S5_SLAB_ST, S5_SLAB_CH), lambda b, hf, t: (hf, 0, 0, 0)),
                  pl.BlockSpec((1, S5_CHAINS, 2 * S5_SLAB_ST), lambda b, hf, t: (hf, 0, 0)),
                  pl.BlockSpec((1, 1, S5_HALF_CH), lambda b, hf, t: (hf, 0, 0))],
        out_specs=[pl.BlockSpec((S5_T, S5_HALF_CH), lambda b, hf, t: (b * n_t + t, hf)),
                   pl.BlockSpec((1, 1, S5_CHAINS, 2 * S5_SLAB_ST), lambda b, hf, t: (b, hf, 0, 0))],
        out_shape=[jax.ShapeDtypeStruct((n_rows_out, S5_WIDTH), F32),
                   jax.ShapeDtypeStruct((n_batch, n_half, S5_CHAINS, 2 * S5_SLAB_ST), F32)],
        scratch_shapes=[pltpu.VMEM((S5_LT, S5_T * S5_CHAINS, LANE), F32),
                        pltpu.VMEM((S5_LT, S5_CHAINS, LANE), F32)],
        compiler_params=pltpu.CompilerParams(
            dimension_semantics=("parallel", "parallel", "arbitrary"), vmem_limit_bytes=VMEM_LIMIT),
        name="s5_prompt",
    )(proj, half(bcat.astype(cdt)), half(ccat.astype(cdt)), half(a_cat), d_skip.reshape(n_half, 1, S5_HALF_CH))


def _s5_sample_kernel(u_ref, bcat_ref, ccat_ref, a_ref, d_ref, h0_ref, y_ref, hout_ref, sc, *, n_b, n_t, cdt):
    u = u_ref[...]
    bu = jnp.dot(u.astype(cdt), bcat_ref[0], preferred_element_type=F32)
    a_re = a_ref[0, :, :S5_SLAB_ST]
    a_im = a_ref[0, :, S5_SLAB_ST:]
    for bg in range(n_b // 8):
        hr = h0_ref[0, bg * 8:(bg + 1) * 8, :S5_SLAB_ST]
        hi = h0_ref[0, bg * 8:(bg + 1) * 8, S5_SLAB_ST:]
        for t in range(n_t):
            r = t * n_b + bg * 8
            hr, hi = (a_re * hr - a_im * hi + bu[r:r + 8, :S5_SLAB_ST],
                      a_re * hi + a_im * hr + bu[r:r + 8, S5_SLAB_ST:])
            sc[r:r + 8, :S5_SLAB_ST] = hr
            sc[r:r + 8, S5_SLAB_ST:] = hi
        hout_ref[0, bg * 8:(bg + 1) * 8, :S5_SLAB_ST] = hr
        hout_ref[0, bg * 8:(bg + 1) * 8, S5_SLAB_ST:] = hi
    y = jnp.dot(sc[...].astype(cdt), ccat_ref[0], preferred_element_type=F32) + d_ref[0] * u
    y_ref[...] = _gelu_tanh(y)


def s5_sample(u_tb, a_cat, bcat, ccat, d_skip, h0_cat, *, n_b, n_t, cdt=BF16):
    rows = n_t * n_b
    return pl.pallas_call(
        functools.partial(_s5_sample_kernel, n_b=n_b, n_t=n_t, cdt=cdt),
        grid=(S5_N_SLABS,),
        in_specs=[pl.BlockSpec((rows, S5_SLAB_CH), lambda s: (0, s)),
                  pl.BlockSpec((1, S5_SLAB_CH, 2 * S5_SLAB_ST), lambda s: (s, 0, 0)),
                  pl.BlockSpec((1, 2 * S5_SLAB_ST, S5_SLAB_CH), lambda s: (s, 0, 0)),
                  pl.BlockSpec((1, 1, 2 * S5_SLAB_ST), lambda s: (s, 0, 0)),
                  pl.BlockSpec((1, 1, S5_SLAB_CH), lambda s: (s, 0, 0)),
                  pl.BlockSpec((1, n_b, 2 * S5_SLAB_ST), lambda s: (s, 0, 0))],
        out_specs=[pl.BlockSpec((rows, S5_SLAB_CH), lambda s: (0, s)),
                   pl.BlockSpec((1, n_b, 2 * S5_SLAB_ST), lambda s: (s, 0, 0))],
        out_shape=[jax.ShapeDtypeStruct((rows, S5_WIDTH), F32),
                   jax.ShapeDtypeStruct((S5_N_SLABS, n_b, 2 * S5_SLAB_ST), F32)],
        scratch_shapes=[pltpu.VMEM((rows, 2 * S5_SLAB_ST), F32)],
        compiler_params=pltpu.CompilerParams(
            dimension_semantics=("arbitrary",), vmem_limit_bytes=VMEM_LIMIT),
        name="s5_sample",
    )(u_tb, bcat.astype(cdt), ccat.astype(cdt), a_cat.reshape(S5_N_SLABS, 1, 2 * S5_SLAB_ST),
      d_skip.reshape(S5_N_SLABS, 1, S5_SLAB_CH), h0_cat)


GLU_TM = 320
GLU_TN = 512


def _glu_kernel(yfull_ref, w_ref, ycol_ref, z_ref, o_ref, ybf_sc):
    @pl.when(pl.program_id(1) == 0)
    def _():
        ybf_sc[...] = yfull_ref[...].astype(ybf_sc.dtype)

    glu = jnp.dot(ybf_sc[...], w_ref[...].astype(ybf_sc.dtype), preferred_element_type=F32)
    z = z_ref[...]
    y = ycol_ref[...]
    o_ref[...] = (y * (1.0 / (1.0 + jnp.exp(-glu))) * (z * (1.0 / (1.0 + jnp.exp(-z))))).astype(o_ref.dtype)


def s5_glu_gate(y, w_glu, proj, row_off, cdt=BF16):
    m = y.shape[0]
    tm, tn = _pick_tile(m, (256, 128, 64, 32, 16)), GLU_TN
    assert row_off % tm == 0
    z_off, r_off = S5_WIDTH // tn, row_off // tm
    return pl.pallas_call(
        _glu_kernel,
        grid=(m // tm, S5_WIDTH // tn),
        in_specs=[pl.BlockSpec((tm, S5_WIDTH), lambda i, j: (i, 0)),
                  pl.BlockSpec((S5_WIDTH, tn), lambda i, j: (0, j)),
                  pl.BlockSpec((tm, tn), lambda i, j: (i, j)),
                  pl.BlockSpec((tm, tn), lambda i, j: (r_off + i, z_off + j))],
        out_specs=pl.BlockSpec((tm, tn), lambda i, j: (i, j)),
        out_shape=jax.ShapeDtypeStruct((m, S5_WIDTH), BF16),
        scratch_shapes=[pltpu.VMEM((tm, S5_WIDTH), cdt)],
        compiler_params=pltpu.CompilerParams(
            dimension_semantics=("parallel", "arbitrary"), vmem_limit_bytes=VMEM_LIMIT),
        name="s5_glu_gate",
    )(y, w_glu, y, proj)


def s5_layer(proj, state_in, tables, d_skip, w_glu, *, n_batch, seq_len, n_dec, dec_len, cdt=BF16):
    a_cat, bcat, ccat = tables
    n_p = n_batch * seq_len
    n_s = n_dec * dec_len
    y_p, h_p = s5_prompt(proj, a_cat, bcat, ccat, d_skip, n_batch=n_batch, seq_len=seq_len, n_rows_out=n_p, cdt=cdt)
    u_tb = jnp.swapaxes(proj[n_p:, :S5_WIDTH].reshape(n_dec, dec_len, S5_WIDTH), 0, 1).reshape(n_s, S5_WIDTH)
    h0 = state_in.reshape(n_dec, S5_N_SLABS, S5_SLAB_ST, 2)
    h0_cat = jnp.concatenate([jnp.swapaxes(h0[..., 0], 0, 1), jnp.swapaxes(h0[..., 1], 0, 1)], axis=-1)
    y_tb, h_s = s5_sample(u_tb, a_cat, bcat, ccat, d_skip, h0_cat, n_b=n_dec, n_t=dec_len, cdt=cdt)
    y_s = jnp.swapaxes(y_tb.reshape(dec_len, n_dec, S5_WIDTH), 0, 1).reshape(n_s, S5_WIDTH)
    gated_p = s5_glu_gate(y_p, w_glu, proj, 0, cdt=cdt)
    gated_s = s5_glu_gate(y_s, w_glu, proj, n_p, cdt=cdt)
    hp = h_p.reshape(n_batch, S5_N_SLABS, 2, S5_SLAB_ST)
    st_p = jnp.stack([hp[:, :, 0], hp[:, :, 1]], axis=-1).reshape(n_batch, S5_GROUPS, S5_STATE, 2)
    hs = jnp.swapaxes(h_s, 0, 1).reshape(n_dec, S5_N_SLABS, 2, S5_SLAB_ST)
    st_s = jnp.stack([hs[:, :, 0], hs[:, :, 1]], axis=-1).reshape(n_dec, S5_GROUPS, S5_STATE, 2)
    return gated_p, gated_s, st_p, st_s


GDN_CONV_TT = 256
GDN_CONV_CW = 1024
GDN_HIST = 8
GDN_HB = 16
GDN_TT = 256
GDN_SAMPLE_ROWS = 8


def _gdn_conv_kernel(x_ref, hist_ref, w_ref, o_ref, *, rows, n_t, zero_first):
    i, j = pl.program_id(0), pl.program_id(1)
    hist = hist_ref[...]
    if zero_first:
        hist = jnp.where(i % n_t == 0, 0.0, hist)
    ext = jnp.concatenate([hist, x_ref[...]], axis=0)
    acc = ext[GDN_HIST:GDN_HIST + rows] * w_ref[GDN_CONV - 1:GDN_CONV, :]
    for s in range(1, GDN_CONV):
        acc = acc + ext[GDN_HIST - s:GDN_HIST - s + rows] * w_ref[GDN_CONV - 1 - s:GDN_CONV - s, :]
    conv = acc * (1.0 / (1.0 + jnp.exp(-acc)))
    n_qk_blocks = 2 * GDN_QK_WIDTH // GDN_CONV_CW

    @pl.when(j >= n_qk_blocks)
    def _():
        o_ref[...] = conv

    @pl.when(j < n_qk_blocks)
    def _():
        scale = jnp.where(j < GDN_QK_WIDTH // GDN_CONV_CW, GDN_DK ** -0.5, 1.0)
        for h in range(GDN_CONV_CW // GDN_DK):
            t = conv[:, h * GDN_DK:(h + 1) * GDN_DK]
            n = t * lax.rsqrt(jnp.sum(t * t, axis=-1, keepdims=True) + 1e-6)
            o_ref[:, h * GDN_DK:(h + 1) * GDN_DK] = n * scale


def gdn_conv(x, hist_src, conv_w, *, rows, n_blocks, n_t, data_map, hist_map, zero_first):
    n_out = n_blocks * rows
    return pl.pallas_call(
        functools.partial(_gdn_conv_kernel, rows=rows, n_t=n_t, zero_first=zero_first),
        grid=(n_blocks, GDN_CONV_CH // GDN_CONV_CW),
        in_specs=[pl.BlockSpec((rows, GDN_CONV_CW), lambda i, j: (data_map(i), j)),
                  pl.BlockSpec((GDN_HIST, GDN_CONV_CW), lambda i, j: (hist_map(i), j)),
                  pl.BlockSpec((GDN_CONV, GDN_CONV_CW), lambda i, j: (0, j))],
        out_specs=pl.BlockSpec((rows, GDN_CONV_CW), lambda i, j: (i, j)),
        out_shape=jax.ShapeDtypeStruct((n_out, GDN_CONV_CH), F32),
        compiler_params=pltpu.CompilerParams(
            dimension_semantics=("parallel", "parallel"), vmem_limit_bytes=VMEM_LIMIT),
        name="gdn_conv",
    )(x, hist_src, conv_w)


def _gdn_chunk_lockstep_kernel(q_ref, k_ref, v_ref, z_ref, ab_ref, alog_ref, dtb_ref, nw_ref, s0_ref, o_ref,
                               sout_ref, s_sc, *, chunk, n_inner, n_tt, valid_len):
    C = chunk
    hb, tt = pl.program_id(1), pl.program_id(2)

    @pl.when(tt == 0)
    def _():
        s_sc[...] = s0_ref[0]

    rowi = lax.broadcasted_iota(jnp.int32, (C, C), 0)
    coli = lax.broadcasted_iota(jnp.int32, (C, C), 1)
    causal = rowi >= coli
    strict = rowi > coli
    ltri = jnp.where(causal, 1.0, 0.0)
    utri = jnp.where(rowi <= coli, 1.0, 0.0)
    eye = jnp.where(rowi == coli, 1.0, 0.0)
    hi = lax.Precision.HIGHEST
    shift = (LANE - hb * GDN_HB) % LANE
    alog = pltpu.roll(jnp.broadcast_to(alog_ref[...], (8, LANE)), shift, 1)[0:1]
    dtb = pltpu.roll(jnp.broadcast_to(dtb_ref[...], (8, LANE)), shift, 1)[0:1]
    nw = nw_ref[...]
    tok_valid = lax.broadcasted_iota(jnp.int32, (C, LANE), 0) < valid_len
    dot = functools.partial(jnp.dot, preferred_element_type=F32)
    dot_nt = lambda a, b: lax.dot_general(a, b, (((1,), (1,)), ((), ())), preferred_element_type=F32)
    dot_tn = lambda a, b: lax.dot_general(a, b, (((0,), (0,)), ((), ())), preferred_element_type=F32)
    units = [(c, i) for c in range(n_inner) for i in range(GDN_HB)]
    rows = lambda c: slice(c * C, (c + 1) * C)
    qk_cols = lambda i: slice((i // 2) * GDN_DK, (i // 2 + 1) * GDN_DK)
    v_cols = lambda i: slice(i * GDN_DV, (i + 1) * GDN_DV)

    g_all, beta_all = [], []
    for c in range(n_inner):
        ab = pltpu.roll(ab_ref[rows(c), :], shift, 1)
        xa = ab + dtb
        softplus = jnp.maximum(xa, 0.0) + jnp.log1p(jnp.exp(-jnp.abs(xa)))
        g_all.append(jnp.where(tok_valid, -jnp.exp(alog) * softplus, 0.0))
        beta_all.append(jnp.where(tok_valid, 1.0 / (1.0 + jnp.exp(-ab)), 0.0))
    gam_all = [jnp.dot(ltri, g, preferred_element_type=F32, precision=hi) for g in g_all]
    gamT_all = [lax.dot_general(g, utri, (((0,), (0,)), ((), ())), preferred_element_type=F32, precision=hi)
                for g in g_all]

    qkk = [dot_nt(jnp.concatenate([q_ref[rows(c), qk_cols(i)], k_ref[rows(c), qk_cols(i)]], axis=0).astype(BF16),
                  k_ref[rows(c), qk_cols(i)].astype(BF16)) for c, i in units]
    gam_c = [jnp.broadcast_to(gam_all[c][:, i:i + 1], (C, LANE)) for c, i in units]
    beta_c = [jnp.broadcast_to(beta_all[c][:, 32 + i:33 + i], (C, LANE)) for c, i in units]
    gam_last = [jnp.broadcast_to(gam_all[c][C - 1:C, i:i + 1], (1, LANE)) for c, i in units]
    decay = [jnp.where(causal, jnp.exp(jnp.where(causal, gc[:, :C] - jnp.broadcast_to(gamT_all[c][i:i + 1, :], (C, C)),
                                                 0.0)), 0.0) for gc, (c, i) in zip(gam_c, units)]
    qk = [(x[:C] * d).astype(BF16) for x, d in zip(qkk, decay)]
    neg_a = [jnp.where(strict, -(b[:, :C] * x[C:] * d), 0.0) for b, x, d in zip(beta_c, qkk, decay)]
    p_inv = [eye + n for n in neg_a]
    m_pow = neg_a
    for _ in range(int(math.log2(C)) - 1):
        m_pow = [dot(m.astype(BF16), m.astype(BF16)) for m in m_pow]
        p_inv = [p + dot(p.astype(BF16), m.astype(BF16)) for p, m in zip(p_inv, m_pow)]
    eg = [jnp.exp(gc) for gc in gam_c]
    sol = [dot(p.astype(BF16), jnp.concatenate([b * v_ref[rows(c), v_cols(i)], (b * e) * k_ref[rows(c), qk_cols(i)]],
                                               axis=1).astype(BF16))
           for p, b, e, (c, i) in zip(p_inv, beta_c, eg, units)]
    wq = [jnp.concatenate([s[:, GDN_DV:], q_ref[rows(c), qk_cols(i)] * e], axis=0).astype(BF16)
          for s, e, (c, i) in zip(sol, eg, units)]
    k_dec = [(k_ref[rows(c), qk_cols(i)] * jnp.exp(gl - gc)).astype(BF16)
             for gl, gc, (c, i) in zip(gam_last, gam_c, units)]

    state = [s_sc[i] for i in range(GDN_HB)]
    for c in range(n_inner):
        base = c * GDN_HB
        ws = [dot(wq[base + i], state[i].astype(BF16)) for i in range(GDN_HB)]
        v_new = [(sol[base + i][:, :GDN_DV] - ws[i][:C]).astype(BF16) for i in range(GDN_HB)]
        o = [ws[i][C:] + dot(qk[base + i], v_new[i]) for i in range(GDN_HB)]
        state = [state[i] * jnp.exp(gam_last[base + i]) + dot_tn(k_dec[base + i], v_new[i]) for i in range(GDN_HB)]
        for i in range(GDN_HB):
            rms = lax.rsqrt(jnp.mean(o[i] * o[i], axis=-1, keepdims=True) + 1e-6)
            zz = z_ref[rows(c), v_cols(i)]
            o_ref[rows(c), v_cols(i)] = (o[i] * rms * nw * (zz * (1.0 / (1.0 + jnp.exp(-zz))))).astype(o_ref.dtype)
    for i in range(GDN_HB):
        s_sc[i] = state[i]

    @pl.when(tt == n_tt - 1)
    def _():
        sout_ref[0] = s_sc[...]


def gdn_chunk(conv, z, ab, a_log, dt_bias, norm_w, s0, *, n_seq, rows_per_seq, rows_per_step, chunk, valid_len,
              z_col_off, out_dtype):
    n_tt = rows_per_seq // rows_per_step
    n_inner = rows_per_step // chunk
    n_hb = GDN_V_HEADS // GDN_HB
    qw, vw = GDN_HB // 2 * GDN_DK, GDN_HB * GDN_DV
    k_off, v_off, z_off = GDN_QK_WIDTH // qw, 2 * GDN_QK_WIDTH // vw, z_col_off // vw
    row = lambda b, hb, t: b * n_tt + t
    pad_row = lambda p: jnp.pad(p.astype(F32), (0, LANE - p.shape[0])).reshape(1, LANE)
    return pl.pallas_call(
        functools.partial(_gdn_chunk_lockstep_kernel, chunk=chunk, n_inner=n_inner, n_tt=n_tt, valid_len=valid_len),
        grid=(n_seq, n_hb, n_tt),
        in_specs=[pl.BlockSpec((rows_per_step, qw), lambda b, hb, t: (row(b, hb, t), hb)),
                  pl.BlockSpec((rows_per_step, qw), lambda b, hb, t: (row(b, hb, t), k_off + hb)),
                  pl.BlockSpec((rows_per_step, vw), lambda b, hb, t: (row(b, hb, t), v_off + hb)),
                  pl.BlockSpec((rows_per_step, vw), lambda b, hb, t: (row(b, hb, t), z_off + hb)),
                  pl.BlockSpec((rows_per_step, LANE), lambda b, hb, t: (row(b, hb, t), 0)),
                  pl.BlockSpec((1, LANE), lambda b, hb, t: (0, 0)),
                  pl.BlockSpec((1, LANE), lambda b, hb, t: (0, 0)),
                  pl.BlockSpec((1, GDN_DV), lambda b, hb, t: (0, 0)),
                  pl.BlockSpec((1, GDN_HB, GDN_DK, GDN_DV), lambda b, hb, t: (b, hb, 0, 0))],
        out_specs=[pl.BlockSpec((rows_per_step, vw), lambda b, hb, t: (row(b, hb, t), hb)),
                   pl.BlockSpec((1, GDN_HB, GDN_DK, GDN_DV), lambda b, hb, t: (b, hb, 0, 0))],
        out_shape=[jax.ShapeDtypeStruct((n_seq * rows_per_seq, GDN_V_WIDTH), out_dtype),
                   jax.ShapeDtypeStruct((n_seq, GDN_V_HEADS, GDN_DK, GDN_DV), F32)],
        scratch_shapes=[pltpu.VMEM((GDN_HB, GDN_DK, GDN_DV), F32)],
        compiler_params=pltpu.CompilerParams(
            dimension_semantics=("parallel", "parallel", "arbitrary"), vmem_limit_bytes=VMEM_LIMIT),
        name="gdn_chunk",
    )(conv, conv, conv, z, ab, pad_row(a_log), pad_row(dt_bias), norm_w.astype(F32).reshape(1, GDN_DV), s0)


def gdn_layer(qkvz, ab, state_in, conv_in, conv_w, a_log, dt_bias, norm_w, *, n_batch, seq_len, n_dec, dec_len):
    n_p = n_batch * seq_len
    n_tp = seq_len // GDN_CONV_TT
    hist_per_block = GDN_CONV_TT // GDN_HIST
    conv_p = gdn_conv(qkvz, qkvz, conv_w, rows=GDN_CONV_TT, n_blocks=n_batch * n_tp, n_t=n_tp,
                      data_map=lambda i: i, hist_map=lambda i: jnp.maximum(i * hist_per_block - 1, 0),
                      zero_first=True)
    zeros_s = jnp.zeros((n_batch, GDN_V_HEADS, GDN_DK, GDN_DV), F32)
    gated_p, st_p = gdn_chunk(conv_p, qkvz, ab, a_log, dt_bias, norm_w, zeros_s, n_seq=n_batch,
                              rows_per_seq=seq_len, rows_per_step=GDN_TT, chunk=GDN_CHUNK, valid_len=GDN_CHUNK,
                              z_col_off=GDN_CONV_CH, out_dtype=BF16)
    buf_p = jnp.stack([lax.slice(qkvz, ((b + 1) * seq_len - (GDN_CONV - 1), 0), ((b + 1) * seq_len, GDN_CONV_CH))
                       for b in range(n_batch)])
    R = GDN_SAMPLE_ROWS
    x_s = qkvz[n_p:].reshape(n_dec, dec_len, -1)
    pad_t = lambda t, front: jnp.pad(t, ((0, 0), (front, R - front - t.shape[1]), (0, 0)))
    ext = jnp.concatenate([pad_t(conv_in, R - (GDN_CONV - 1)), pad_t(x_s[..., :GDN_CONV_CH], 0)], axis=1)
    ext = ext.reshape(n_dec * 2 * R, GDN_CONV_CH)
    conv_s = gdn_conv(ext, ext, conv_w, rows=R, n_blocks=n_dec, n_t=1,
                      data_map=lambda i: 2 * i + 1, hist_map=lambda i: 2 * i, zero_first=False)
    z_s = pad_t(x_s[..., GDN_CONV_CH:], 0).reshape(n_dec * R, GDN_V_WIDTH)
    ab_s = pad_t(ab[n_p:].reshape(n_dec, dec_len, LANE), 0).reshape(n_dec * R, LANE)
    gated_s, st_s = gdn_chunk(conv_s, z_s, ab_s, a_log, dt_bias, norm_w, state_in, n_seq=n_dec, rows_per_seq=R,
                              rows_per_step=R, chunk=R, valid_len=dec_len, z_col_off=0, out_dtype=F32)
    gated_s = gated_s.reshape(n_dec, R, GDN_V_WIDTH)[:, :dec_len].reshape(n_dec * dec_len, GDN_V_WIDTH)
    buf_s = jnp.concatenate([conv_in, x_s[..., :GDN_CONV_CH]], axis=1)[:, dec_len:]
    return gated_p, gated_s, st_p, st_s, buf_p, buf_s


DSA_KC = 256
INT_MIN = -2 ** 31
NEG_BIG = -1e30
G_D = N_HEADS // KV_D
BIAS_WIN = DSA_KC + Q_BLOCK


def _sortable_key(s):
    b = pltpu.bitcast(s, jnp.int32)
    return jnp.where(b < 0, b ^ jnp.int32(0x7FFFFFFF), b)


def _dsa_prompt_kernel(qT_ref, qiT_ref, wiT_ref, zT_ref, ki_ref, k_ref, vT_ref, win_ref, o_ref,
                       key_sc, mask_sc, *, topk, idx_bits, cdt):
    qb = pl.program_id(1)
    t0 = qb * Q_BLOCK
    nch = (qb + 2) // 2
    t_idx = t0 + lax.broadcasted_iota(jnp.int32, (1, Q_BLOCK), 1)
    row_iota = lax.broadcasted_iota(jnp.int32, (DSA_KC, Q_BLOCK), 0)

    def score_chunk(c, carry):
        kic = ki_ref[0, c].astype(cdt)
        acc = jnp.zeros((DSA_KC, Q_BLOCK), F32)
        for hp in range(IDX_HEADS // 2):
            rhs = jnp.concatenate([qiT_ref[(2 * hp) * IDX_DIM:(2 * hp + 1) * IDX_DIM, :],
                                   qiT_ref[(2 * hp + 1) * IDX_DIM:(2 * hp + 2) * IDX_DIM, :]], axis=1)
            s = jnp.dot(kic, rhs, preferred_element_type=F32) * (IDX_DIM ** -0.5)
            s = jnp.maximum(s, 0.0)
            w0 = wiT_ref[2 * hp:2 * hp + 1, :] * (IDX_HEADS ** -0.5)
            w1 = wiT_ref[2 * hp + 1:2 * hp + 2, :] * (IDX_HEADS ** -0.5)
            acc = acc + s[:, :Q_BLOCK] * w0 + s[:, Q_BLOCK:] * w1
        s_idx = c * DSA_KC + row_iota
        key_sc[c] = jnp.where(s_idx <= t_idx, _sortable_key(acc), INT_MIN)
        return carry

    lax.fori_loop(0, nch, score_chunk, 0)

    def count(pred):
        def body(c, acc):
            hit = pred(key_sc[c], c * DSA_KC + row_iota)
            return acc + hit.reshape(DSA_KC // 8, 8, Q_BLOCK).sum(axis=0)
        acc = lax.fori_loop(0, nch, body, jnp.zeros((8, Q_BLOCK), jnp.int32))
        return jnp.sum(acc, axis=0, keepdims=True)

    c_nonneg = count(lambda k, s: jnp.where(k >= 0, 1, 0))
    thr = jnp.where(c_nonneg >= topk, 0, INT_MIN).astype(jnp.int32)

    def thr_bit(i, thr):
        cand = thr + jnp.left_shift(jnp.int32(1), 30 - i)
        return jnp.where(count(lambda k, s: jnp.where(k >= cand, 1, 0)) >= topk, cand, thr)

    thr = lax.fori_loop(0, 31, thr_bit, thr)
    need = topk - count(lambda k, s: jnp.where(k > thr, 1, 0))

    def lim_bit(i, lim):
        cand = lim + jnp.left_shift(jnp.int32(1), idx_bits - 1 - i)
        c = count(lambda k, s: jnp.where(k == thr, jnp.where(s < cand, 1, 0), 0))
        return jnp.where(c <= need, cand, lim)

    n_ties = count(lambda k, s: jnp.where(k == thr, 1, 0))
    settled = jnp.min(jnp.where(thr == INT_MIN, 1, jnp.where(n_ties == need, 1, 0))) == 1
    lim = lax.cond(settled,
                   lambda: jnp.full((1, Q_BLOCK), 1 << idx_bits, jnp.int32),
                   lambda: lax.fori_loop(0, idx_bits, lim_bit, jnp.zeros((1, Q_BLOCK), jnp.int32)))

    def mask_chunk(c, carry):
        k = key_sc[c]
        s_idx = c * DSA_KC + row_iota
        tie = jnp.where(k == thr, jnp.where(s_idx < lim, 0.0, NEG_BIG), NEG_BIG)
        m = jnp.where(k > thr, 0.0, tie)
        mask_sc[c] = jnp.where(k == INT_MIN, NEG_BIG, m)
        return carry

    lax.fori_loop(0, nch, mask_chunk, 0)

    n_cols = G_D * Q_BLOCK
    half = DSA_KC // 2

    def head_q(j):
        qj = jnp.concatenate([qT_ref[(G_D * j + g) * HEAD_DIM:(G_D * j + g + 1) * HEAD_DIM, :]
                              for g in range(G_D)], axis=1)
        return (qj.astype(F32) * (HEAD_DIM ** -0.5)).astype(cdt)

    for jp in range(KV_D // 2):
        q0, q1 = head_q(2 * jp), head_q(2 * jp + 1)
        zq = jnp.zeros_like(q0)
        rhs = jnp.concatenate([jnp.concatenate([q0, zq], axis=0), jnp.concatenate([zq, q1], axis=0)], axis=1)

        def chunk_body(c, carry, jp=jp, rhs=rhs):
            m, l, acc0, acc1 = carry
            kc = k_ref[0, c, :, jp * 2 * HEAD_DIM:(jp + 1) * 2 * HEAD_DIM].astype(cdt)
            s = jnp.dot(kc, rhs, preferred_element_type=F32)
            wt = win_ref[qb - 2 * c]
            madd = mask_sc[c]
            parts = []
            for hh in range(2 * G_D):
                h = 2 * G_D * jp + hh
                tiles = []
                for u in range(2):
                    lo = (1 - u) * half
                    r = jnp.broadcast_to(wt[h:h + 1, lo:lo + 2 * half], (half, 2 * half))
                    tiles.append(pltpu.roll(r, 0, 1, stride=1, stride_axis=0)[:, half:])
                parts.append(s[:, hh * Q_BLOCK:(hh + 1) * Q_BLOCK] + (jnp.concatenate(tiles, axis=0) + madd))
            s = jnp.concatenate(parts, axis=1)
            m_new = jnp.maximum(m, jnp.max(s, axis=0, keepdims=True))
            alpha = jnp.exp(m - m_new)
            p = jnp.exp(s - m_new)
            l = l * alpha + jnp.sum(p, axis=0, keepdims=True)
            p = p.astype(cdt)
            v0 = vT_ref[0, c, (2 * jp) * HEAD_DIM:(2 * jp + 1) * HEAD_DIM, :]
            v1 = vT_ref[0, c, (2 * jp + 1) * HEAD_DIM:(2 * jp + 2) * HEAD_DIM, :]
            acc0 = acc0 * alpha[:, :n_cols] + jnp.dot(v0, p[:, :n_cols], preferred_element_type=F32)
            acc1 = acc1 * alpha[:, n_cols:] + jnp.dot(v1, p[:, n_cols:], preferred_element_type=F32)
            return m_new, l, acc0, acc1

        init = (jnp.full((1, 2 * n_cols), NEG_BIG, F32), jnp.zeros((1, 2 * n_cols), F32),
                jnp.zeros((HEAD_DIM, n_cols), F32), jnp.zeros((HEAD_DIM, n_cols), F32))
        m, l, acc0, acc1 = lax.fori_loop(0, nch, chunk_body, init)
        inv = 1.0 / l
        for jj, acc in enumerate((acc0, acc1)):
            o = acc * inv[:, jj * n_cols:(jj + 1) * n_cols]
            for g in range(G_D):
                r0 = (G_D * (2 * jp + jj) + g) * HEAD_DIM
                z = zT_ref[r0:r0 + HEAD_DIM, :]
                gate = z * (1.0 / (1.0 + jnp.exp(-z)))
                o_ref[r0:r0 + HEAD_DIM, :] = (o[:, g * Q_BLOCK:(g + 1) * Q_BLOCK] * gate).astype(o_ref.dtype)


def dsa_bias_windows(rel_bias, seq_len):
    o = jnp.arange(seq_len // Q_BLOCK)[:, None]
    m = jnp.arange(BIAS_WIN)[None, :]
    d = jnp.maximum(o * Q_BLOCK + m - DSA_KC, 0)
    return jnp.moveaxis(rel_bias[rel_bucket(d)].astype(F32), -1, 1)


def dsa_prompt_attend(qT, qiT, wiT, zT, ki4, k4, v4T, win, *, n_batch, seq_len, cdt=BF16):
    nqb = seq_len // Q_BLOCK
    nc = seq_len // DSA_KC
    topk = min(TOPK_MAX, seq_len // 4)
    idx_bits = int(math.log2(seq_len)) + 1
    tok = lambda b, q: (0, b * nqb + q)
    per_batch = lambda b, q: (b, 0, 0, 0)
    return pl.pallas_call(
        functools.partial(_dsa_prompt_kernel, topk=topk, idx_bits=idx_bits, cdt=cdt),
        grid=(n_batch, nqb),
        in_specs=[pl.BlockSpec((ATT_WIDTH, Q_BLOCK), tok),
                  pl.BlockSpec((IDX_HEADS * IDX_DIM, Q_BLOCK), tok),
                  pl.BlockSpec((IDX_HEADS, Q_BLOCK), tok),
                  pl.BlockSpec((ATT_WIDTH, Q_BLOCK), tok),
                  pl.BlockSpec((1, nc, DSA_KC, IDX_DIM), per_batch),
                  pl.BlockSpec((1, nc, DSA_KC, D_KV), per_batch),
                  pl.BlockSpec((1, nc, D_KV, DSA_KC), per_batch),
                  pl.BlockSpec((nqb, N_HEADS, BIAS_WIN), lambda b, q: (0, 0, 0))],
        out_specs=pl.BlockSpec((ATT_WIDTH, Q_BLOCK), tok),
        out_shape=jax.ShapeDtypeStruct((ATT_WIDTH, n_batch * seq_len), BF16),
        scratch_shapes=[pltpu.VMEM((nc, DSA_KC, Q_BLOCK), jnp.int32),
                        pltpu.VMEM((nc, DSA_KC, Q_BLOCK), F32)],
        compiler_params=pltpu.CompilerParams(
            dimension_semantics=("parallel", "arbitrary"), vmem_limit_bytes=VMEM_LIMIT),
        name="dsa_prompt_attend",
    )(qT, qiT, wiT, zT, ki4, k4, v4T, win)


DSS_NP1 = 32
DSS_NP2 = 16
DSS_TP = 8
DSS_GROWS = G_D * DSS_TP
T5_LAST_BUCKET_DIST = 1600


def _dsa_sample_select_kernel(pt_ref, qi_ref, wb_ref, kinew_ref, *rest, n_pages, n_new, topk, idx_bits, cdt):
    del pt_ref
    page_refs, mask_ref, key_sc = rest[:DSS_NP1], rest[DSS_NP1], rest[DSS_NP1 + 1]
    s = pl.program_id(1)
    lane = lax.broadcasted_iota(jnp.int32, (DSS_TP, PAGE_SIZE), 1)
    trow = lax.broadcasted_iota(jnp.int32, (DSS_TP, PAGE_SIZE), 0)
    qi = qi_ref[0]
    wb = wb_ref[0]

    def page_keys(kp):
        sc = lax.dot_general(qi, kp.astype(cdt), (((1,), (1,)), ((), ())),
                             preferred_element_type=F32) * (IDX_DIM ** -0.5)
        sc = jnp.maximum(sc, 0.0) * wb
        return _sortable_key(sc.reshape(IDX_HEADS, DSS_TP, PAGE_SIZE).sum(axis=0))

    for i in range(DSS_NP1):
        key_sc[s * DSS_NP1 + i] = page_keys(page_refs[i][0, 0])

    @pl.when(s == 0)
    def _():
        kn = page_keys(kinew_ref[0])
        key_sc[n_pages] = jnp.where(lane < n_new, jnp.where(lane <= trow, kn, INT_MIN), INT_MIN)

    @pl.when(s == n_pages // DSS_NP1 - 1)
    def _():
        all_shape = (n_pages + 1, DSS_TP, PAGE_SIZE)

        def count(pred):
            key_idx = (lax.broadcasted_iota(jnp.int32, all_shape, 0) * PAGE_SIZE
                       + lax.broadcasted_iota(jnp.int32, all_shape, 2))
            acc = pred(key_sc[...], key_idx).sum(axis=0)
            return jnp.broadcast_to(jnp.sum(acc, axis=1, keepdims=True), (DSS_TP, PAGE_SIZE))

        c_nonneg = count(lambda k, i: jnp.where(k >= 0, 1, 0))
        thr = jnp.where(c_nonneg >= topk, 0, INT_MIN).astype(jnp.int32)

        def thr_bit(b, thr):
            cand = thr + jnp.left_shift(jnp.int32(1), 30 - b)
            return jnp.where(count(lambda k, i: jnp.where(k >= cand, 1, 0)) >= topk, cand, thr)

        thr = lax.fori_loop(0, 31, thr_bit, thr)
        need = topk - count(lambda k, i: jnp.where(k > thr, 1, 0))

        def lim_bit(b, lim):
            cand = lim + jnp.left_shift(jnp.int32(1), idx_bits - 1 - b)
            c = count(lambda k, i: jnp.where(k == thr, jnp.where(i < cand, 1, 0), 0))
            return jnp.where(c <= need, cand, lim)

        n_ties = count(lambda k, i: jnp.where(k == thr, 1, 0))
        settled = jnp.min(jnp.where(thr == INT_MIN, 1, jnp.where(n_ties == need, 1, 0))) == 1
        lim = lax.cond(settled,
                       lambda: jnp.full((DSS_TP, PAGE_SIZE), 1 << idx_bits, jnp.int32),
                       lambda: lax.fori_loop(0, idx_bits, lim_bit, jnp.zeros((DSS_TP, PAGE_SIZE), jnp.int32)))

        k = key_sc[...]
        key_idx = (lax.broadcasted_iota(jnp.int32, all_shape, 0) * PAGE_SIZE
                   + lax.broadcasted_iota(jnp.int32, all_shape, 2))
        tie = jnp.where(k == thr, jnp.where(key_idx < lim, 0.0, NEG_BIG), NEG_BIG)
        mask_ref[0] = jnp.where(k == INT_MIN, NEG_BIG, jnp.where(k > thr, 0.0, tie))


def _dsa_sample_attend_kernel(pt_ref, q_ref, z_ref, mask_ref, masknew_ref, bnear_ref, bfar_ref, kvnew_ref, *rest,
                              n_pages, n_far, cdt):
    del pt_ref
    page_refs, o_ref = rest[:DSS_NP2], rest[DSS_NP2]
    m_sc, l_sc, acc_sc = rest[DSS_NP2 + 1:]
    s = pl.program_id(1)
    heads = range(KV_D)
    rep = DSS_GROWS // DSS_TP

    def attend(pages, masks, page_ids):
        n = len(pages)
        madd = jnp.concatenate([jnp.concatenate([mk] * rep, axis=0) for mk in masks], axis=1)
        kT = [jnp.concatenate([pg(0, h) for pg in pages], axis=1).astype(cdt) for h in heads]
        vT = [jnp.concatenate([pg(1, h) for pg in pages], axis=1).astype(cdt) for h in heads]
        logits = [jnp.dot(q_ref[0, h], kT[h], preferred_element_type=F32) for h in heads]
        bias = [jnp.concatenate([jnp.where(pid >= n_far, bnear_ref[jnp.maximum(pid - n_far, 0), h], bfar_ref[h])
                                 for pid in page_ids], axis=1) for h in heads]
        logits = [lg + (b + madd) for lg, b in zip(logits, bias)]
        m_old = [m_sc[h] for h in heads]
        m_new = [jnp.maximum(mo, jnp.broadcast_to(jnp.max(lg, axis=1, keepdims=True), mo.shape))
                 for mo, lg in zip(m_old, logits)]
        alpha = [jnp.exp(mo - mn) for mo, mn in zip(m_old, m_new)]
        p = [jnp.exp(lg - jnp.concatenate([mn] * n, axis=1)) for lg, mn in zip(logits, m_new)]
        pv = [lax.dot_general(ph.astype(cdt), vT[h], (((1,), (1,)), ((), ())), preferred_element_type=F32)
              for h, ph in zip(heads, p)]
        for h in heads:
            l_sc[h] = l_sc[h] * alpha[h] + jnp.broadcast_to(jnp.sum(p[h], axis=1, keepdims=True), alpha[h].shape)
            acc_sc[h] = acc_sc[h] * alpha[h][:, :HEAD_DIM] + pv[h]
            m_sc[h] = m_new[h]

    @pl.when(s == 0)
    def _():
        m_sc[...] = jnp.full_like(m_sc, NEG_BIG)
        l_sc[...] = jnp.zeros_like(l_sc)
        acc_sc[...] = jnp.zeros_like(acc_sc)
        attend([lambda c, h: kvnew_ref[0, c, h]], [masknew_ref[0, 0]], [n_pages])

    attend([(lambda c, h, r=r: r[0, 0, c, h]) for r in page_refs], [mask_ref[0, i] for i in range(DSS_NP2)],
           [s * DSS_NP2 + i for i in range(DSS_NP2)])

    @pl.when(s == n_pages // DSS_NP2 - 1)
    def _():
        for h in heads:
            z = z_ref[0, h]
            o_ref[0, h] = acc_sc[h] * (1.0 / l_sc[h][:, :HEAD_DIM]) * (z * (1.0 / (1.0 + jnp.exp(-z))))


def dsa_sample(q_s, z_s, qi_s, wi_s, ki_s, kv_s, kv_pool, kidx_pool, layer, page_table, rel_bias, cdt=BF16):
    n_dec, n_pages = page_table.shape
    dec_len = q_s.shape[0] // n_dec
    past = n_pages * PAGE_SIZE
    total = past + dec_len
    topk = min(TOPK_MAX, total // 4)
    idx_bits = int(math.log2(total)) + 1
    pad_t = DSS_TP - dec_len
    n_pairs = KV_D // 2
    eye2 = jnp.eye(2, dtype=F32)

    qi = jnp.pad(jnp.swapaxes(qi_s.reshape(n_dec, dec_len, IDX_HEADS, IDX_DIM), 1, 2), ((0, 0), (0, 0), (0, pad_t), (0, 0)))
    qi = qi.reshape(n_dec, IDX_HEADS * DSS_TP, IDX_DIM).astype(cdt)
    wb = jnp.pad(jnp.swapaxes(wi_s.reshape(n_dec, dec_len, IDX_HEADS), 1, 2) * (IDX_HEADS ** -0.5), ((0, 0), (0, 0), (0, pad_t)))
    wb = jnp.broadcast_to(wb.reshape(n_dec, IDX_HEADS * DSS_TP, 1), (n_dec, IDX_HEADS * DSS_TP, PAGE_SIZE))
    ki_new = jnp.pad(ki_s.reshape(n_dec, dec_len, IDX_DIM), ((0, 0), (0, PAGE_SIZE - dec_len), (0, 0)))
    kidx4 = kidx_pool.reshape(kidx_pool.shape[0], kidx_pool.shape[1], PAGE_SIZE, IDX_DIM)
    page_spec = lambda np_, i, width: pl.BlockSpec(
        (1, 1, PAGE_SIZE, width), lambda b, s, pt: (layer, pt[b, s * np_ + i], 0, 0))
    per_b3 = lambda b, s, pt: (b, 0, 0)
    mask = pl.pallas_call(
        functools.partial(_dsa_sample_select_kernel, n_pages=n_pages, n_new=dec_len, topk=topk, idx_bits=idx_bits,
                          cdt=cdt),
        grid_spec=pltpu.PrefetchScalarGridSpec(
            num_scalar_prefetch=1, grid=(n_dec, n_pages // DSS_NP1),
            in_specs=[pl.BlockSpec((1, IDX_HEADS * DSS_TP, IDX_DIM), per_b3),
                      pl.BlockSpec((1, IDX_HEADS * DSS_TP, PAGE_SIZE), per_b3),
                      pl.BlockSpec((1, PAGE_SIZE, IDX_DIM), per_b3)]
                     + [page_spec(DSS_NP1, i, IDX_DIM) for i in range(DSS_NP1)],
            out_specs=pl.BlockSpec((1, n_pages + 1, DSS_TP, PAGE_SIZE), lambda b, s, pt: (b, 0, 0, 0)),
            scratch_shapes=[pltpu.VMEM((n_pages + 1, DSS_TP, PAGE_SIZE), jnp.int32)]),
        out_shape=jax.ShapeDtypeStruct((n_dec, n_pages + 1, DSS_TP, PAGE_SIZE), F32),
        compiler_params=pltpu.CompilerParams(
            dimension_semantics=("parallel", "arbitrary"), vmem_limit_bytes=VMEM_LIMIT),
        name="dsa_sample_select",
    )(page_table, qi, wb, ki_new, *([kidx4] * DSS_NP1))

    def head_rows(t, scale):
        t = t.reshape(n_dec, dec_len, KV_D, G_D, HEAD_DIM).transpose(0, 2, 3, 1, 4) * scale
        return jnp.pad(t, ((0, 0),) * 3 + ((0, pad_t), (0, 0))).reshape(n_dec, KV_D, DSS_GROWS, HEAD_DIM)

    q_hr = head_rows(q_s, HEAD_DIM ** -0.5).astype(cdt)
    z_hr = head_rows(z_s, 1.0)
    n_far = max(0, min(n_pages, (past - (PAGE_SIZE - 1) - T5_LAST_BUCKET_DIST) // PAGE_SIZE + 1))
    near_pages = jnp.arange(n_far, n_pages + 1)
    dist = (past + jnp.arange(DSS_TP)[None, :, None]
            - (near_pages[:, None, None] * PAGE_SIZE + jnp.arange(PAGE_SIZE)[None, None, :]))
    b_near = rel_bias[rel_bucket(dist)].astype(F32)
    b_near = b_near.transpose(0, 3, 1, 2).reshape(n_pages + 1 - n_far, KV_D, DSS_GROWS, PAGE_SIZE)
    b_far = jnp.broadcast_to(rel_bias[N_BUCKETS - 1].astype(F32)[:, None, None], (N_HEADS, DSS_TP, PAGE_SIZE))
    b_far = b_far.reshape(KV_D, DSS_GROWS, PAGE_SIZE)
    kv_new = jnp.pad(kv_s.reshape(n_dec, dec_len, 2, KV_D, HEAD_DIM).transpose(0, 2, 3, 4, 1),
                     ((0, 0),) * 4 + ((0, PAGE_SIZE - dec_len),))
    kv_t = kv_pool.reshape(kv_pool.shape[0], kv_pool.shape[1], PAGE_SIZE, 2, KV_D, HEAD_DIM).transpose(0, 1, 3, 4, 5, 2)
    kv_page = lambda i: pl.BlockSpec((1, 1, 2, KV_D, HEAD_DIM, PAGE_SIZE),
                                     lambda b, s, pt: (layer, pt[b, s * DSS_NP2 + i], 0, 0, 0, 0))
    per_b4 = lambda b, s, pt: (b, 0, 0, 0)
    o = pl.pallas_call(
        functools.partial(_dsa_sample_attend_kernel, n_pages=n_pages, n_far=n_far, cdt=cdt),
        grid_spec=pltpu.PrefetchScalarGridSpec(
            num_scalar_prefetch=1, grid=(n_dec, n_pages // DSS_NP2),
            in_specs=[pl.BlockSpec((1, KV_D, DSS_GROWS, HEAD_DIM), per_b4),
                      pl.BlockSpec((1, KV_D, DSS_GROWS, HEAD_DIM), per_b4),
                      pl.BlockSpec((1, DSS_NP2, DSS_TP, PAGE_SIZE), lambda b, s, pt: (b, s, 0, 0)),
                      pl.BlockSpec((1, 1, DSS_TP, PAGE_SIZE), lambda b, s, pt: (b, n_pages, 0, 0)),
                      pl.BlockSpec(b_near.shape, lambda b, s, pt: (0, 0, 0, 0)),
                      pl.BlockSpec(b_far.shape, lambda b, s, pt: (0, 0, 0)),
                      pl.BlockSpec((1, 2, KV_D, HEAD_DIM, PAGE_SIZE), lambda b, s, pt: (b, 0, 0, 0, 0))]
                     + [kv_page(i) for i in range(DSS_NP2)],
            out_specs=pl.BlockSpec((1, KV_D, DSS_GROWS, HEAD_DIM), per_b4),
            scratch_shapes=[pltpu.VMEM((KV_D, DSS_GROWS, LANE), F32), pltpu.VMEM((KV_D, DSS_GROWS, LANE), F32),
                            pltpu.VMEM((KV_D, DSS_GROWS, HEAD_DIM), F32)]),
        out_shape=jax.ShapeDtypeStruct((n_dec, KV_D, DSS_GROWS, HEAD_DIM), F32),
        compiler_params=pltpu.CompilerParams(
            dimension_semantics=("parallel", "arbitrary"), vmem_limit_bytes=VMEM_LIMIT),
        name="dsa_sample_attend",
    )(page_table, q_hr, z_hr, mask, mask, b_near, b_far, kv_new, *([kv_t] * DSS_NP2))
    o = o.reshape(n_dec, KV_D, G_D, DSS_TP, HEAD_DIM)[:, :, :, :dec_len]
    return o.transpose(0, 3, 1, 2, 4).reshape(n_dec * dec_len, ATT_WIDTH)


def _pad_cols(w, n):
    return jnp.pad(w, ((0, 0), (0, n - w.shape[1])))


def kernel(x_prompt, x_sample, cache_a_kv, state_s5, state_gdn, state_gdn_conv, cache_d_kv, cache_d_kidx,
           page_table, p_prompt, p_sample, rel_bias, ln_g, ln_b, ple_gate_w, ple_w,
           a_w_in, a_sinks, a_w_out,
           s5_w_in, s5_a_re, s5_a_im, s5_b_re, s5_b_im, s5_c_re, s5_c_im, s5_d, s5_log_dt, s5_w_glu, s5_w_out,
           gdn_w_in, gdn_conv_w, gdn_a_log, gdn_dt_bias, gdn_norm_w, gdn_w_out,
           dsa_w_in, dsa_w_out):
    x = join_tokens(x_prompt, x_sample)
    x_bf = x.astype(BF16)
    outs = {}
    (a_w_in, a_w_out, s5_w_in, s5_w_glu, s5_w_out, gdn_w_in, gdn_w_out, dsa_w_in, dsa_w_out, ple_gate_w, ple_w) = (
        w.astype(BF16) for w in (a_w_in, a_w_out, s5_w_in, s5_w_glu, s5_w_out, gdn_w_in, gdn_w_out, dsa_w_in,
                                 dsa_w_out, ple_gate_w, ple_w))

    def post(i, x, h):
        return post_norm_ple(x, h, join_tokens(p_prompt[i], p_sample[i]), ln_g[i], ln_b[i], ple_gate_w[i], ple_w[i])

    w_in = a_w_in[0]
    c_k, c_v, c_z = ATT_WIDTH, ATT_WIDTH + A_KV, ATT_WIDTH + 2 * A_KV
    kv_nat = matmul(x_bf, w_in, c_k, 2 * A_KV)
    xT_bf = x_bf[:N_PROMPT_TOK].T
    x_s = x_bf[N_PROMPT_TOK:]
    qT = matmul_wt(w_in, xT_bf, 0, ATT_WIDTH, out_dtype=BF16)
    vT = matmul_wt(w_in, xT_bf, c_v, A_KV, out_dtype=BF16)
    zT = matmul_wt(w_in, xT_bf, c_z, ATT_WIDTH)
    bias_p, sink_p, bias_s, sink_s = swa_tables(rel_bias, a_sinks[0], DEC_SEQ)
    h_p = matmul_ta(swa_prompt(qT, vT, zT, kv_nat, bias_p, sink_p, n_batch=BATCH, seq_len=SEQ), a_w_out[0])
    gs, outs['a_s'] = swa_sample(matmul(x_s, w_in, 0, ATT_WIDTH), matmul(x_s, w_in, c_z, ATT_WIDTH),
                                 kv_nat[N_PROMPT_TOK:], cache_a_kv[0], bias_s, sink_s)
    outs['a_p'] = kv_nat[:N_PROMPT_TOK].reshape(BATCH, SEQ, 2, KV_A, HEAD_DIM)[:, SEQ - WINDOW:]
    x, x_bf = post(0, x, jnp.concatenate([h_p, matmul(gs.astype(BF16), a_w_out[0])], axis=0))

    proj = matmul(x_bf, s5_w_in[0])
    tables = s5_tables(s5_a_re[0], s5_a_im[0], s5_b_re[0], s5_b_im[0], s5_c_re[0], s5_c_im[0], s5_log_dt[0])
    gp, gs, outs['s5_p'], outs['s5_s'] = s5_layer(proj, state_s5[0], tables, s5_d[0], s5_w_glu[0],
                                                  n_batch=BATCH, seq_len=SEQ, n_dec=DEC_BATCH, dec_len=DEC_SEQ)
    x, x_bf = post(1, x, jnp.concatenate([matmul(gp, s5_w_out[0]), matmul(gs, s5_w_out[0])], axis=0))

    w_in = gdn_w_in[0]
    c_gz = GDN_CONV_CH + GDN_V_WIDTH
    qkvz = matmul(x_bf, w_in, 0, c_gz)
    ab = matmul(x_bf, _pad_cols(w_in[:, c_gz:], LANE))
    gp, gs, outs['gd_p'], outs['gd_s'], outs['gc_p'], outs['gc_s'] = gdn_layer(
        qkvz, ab, state_gdn[0], state_gdn_conv[0], gdn_conv_w[0], gdn_a_log[0], gdn_dt_bias[0], gdn_norm_w[0],
        n_batch=BATCH, seq_len=SEQ, n_dec=DEC_BATCH, dec_len=DEC_SEQ)
    x, x_bf = post(2, x, jnp.concatenate([matmul(gp, gdn_w_out[0]), matmul(gs.astype(BF16), gdn_w_out[0])], axis=0))

    w_in = dsa_w_in[0]
    c_kv = ATT_WIDTH + 2 * D_KV
    c_z = 2 * ATT_WIDTH + 2 * D_KV
    c_qi = c_z + IDX_HEADS * IDX_DIM
    kv_nat = matmul(x_bf, w_in, ATT_WIDTH, 2 * D_KV)
    kiw = matmul(x_bf, _pad_cols(w_in[:, c_qi:], 2 * LANE))
    xT_bf = x_bf[:N_PROMPT_TOK].T
    x_s = x_bf[N_PROMPT_TOK:]
    qT = matmul_wt(w_in, xT_bf, 0, ATT_WIDTH, out_dtype=BF16)
    qiT = matmul_wt(w_in, xT_bf, c_z, IDX_HEADS * IDX_DIM, out_dtype=BF16)
    zT = matmul_wt(w_in, xT_bf, c_kv, ATT_WIDTH)
    wiT = matmul_wt(_pad_cols(w_in[:, c_qi + IDX_DIM:], LANE), xT_bf, 0, LANE)
    nc = SEQ // DSA_KC
    kv_p = kv_nat[:N_PROMPT_TOK]
    v4T = jnp.swapaxes(kv_p[:, D_KV:].astype(BF16).reshape(BATCH, nc, DSA_KC, D_KV), 2, 3)
    gT = dsa_prompt_attend(qT, qiT, wiT, zT, kiw[:N_PROMPT_TOK].reshape(BATCH, nc, DSA_KC, 2 * LANE),
                           kv_p.reshape(BATCH, nc, DSA_KC, 2 * D_KV), v4T, dsa_bias_windows(rel_bias, SEQ),
                           n_batch=BATCH, seq_len=SEQ)
    h_p = matmul_ta(gT, dsa_w_out[0])
    kiw_s = kiw[N_PROMPT_TOK:]
    gs = dsa_sample(matmul(x_s, w_in, 0, ATT_WIDTH), matmul(x_s, w_in, c_kv, ATT_WIDTH),
                    matmul(x_s, w_in, c_z, IDX_HEADS * IDX_DIM),
                    kiw_s[:, IDX_DIM:IDX_DIM + IDX_HEADS], kiw_s[:, :IDX_DIM], kv_nat[N_PROMPT_TOK:],
                    cache_d_kv, cache_d_kidx, 0, page_table, rel_bias)
    h_s = matmul(gs.astype(BF16), dsa_w_out[0])
    outs['dkv_p'] = kv_p.reshape(BATCH, SEQ, 2, KV_D, HEAD_DIM)
    outs['dkv_s'] = kv_nat[N_PROMPT_TOK:].reshape(DEC_BATCH, DEC_SEQ, 2, KV_D, HEAD_DIM)
    outs['dki_p'] = kiw[:N_PROMPT_TOK, :IDX_DIM].reshape(BATCH, SEQ, IDX_DIM)
    outs['dki_s'] = kiw_s[:, :IDX_DIM].reshape(DEC_BATCH, DEC_SEQ, IDX_DIM)
    x, x_bf = post(3, x, jnp.concatenate([h_p, h_s], axis=0))

    yp, ys = split_tokens(x)
    st = lambda name: outs[name][None]
    return (yp, ys, st('a_p'), st('a_s'), st('s5_p'), st('s5_s'), st('gd_p'), st('gd_s'),
            st('gc_p'), st('gc_s'), st('dkv_p'), st('dkv_s'), st('dki_p'), st('dki_s'))
```

```python
import functools
import math

import jax
import jax.numpy as jnp
from jax import lax
from jax.experimental import pallas as pl
from jax.experimental.pallas import tpu as pltpu

D_MODEL = 2048
BATCH = 4
SEQ = 2048
DEPTH = 4
DEC_BATCH = 32
DEC_SEQ = 4
PAGE_SIZE = 128
N_MIXERS = 4
PLE_DIM = 256
ALPHA = (2 * DEPTH) ** 0.25
LN_EPS = 1e-5
N_BUCKETS = 32
REL_MAX_DIST = 2048
N_HEADS = 32
HEAD_DIM = 64
ATT_WIDTH = N_HEADS * HEAD_DIM
WINDOW = 128
KV_A = 4
A_KV = KV_A * HEAD_DIM
KV_D = 8
D_KV = KV_D * HEAD_DIM
IDX_HEADS = 16
IDX_DIM = 128
TOPK_MAX = 256
Q_BLOCK = 128
S5_WIDTH = D_MODEL
S5_GROUP = 16
S5_GROUPS = S5_WIDTH // S5_GROUP
S5_STATE = 64
GDN_QK_HEADS = 16
GDN_V_HEADS = 32
GDN_DK = 128
GDN_DV = 128
GDN_CONV = 4
GDN_CHUNK = 64
GDN_QK_WIDTH = GDN_QK_HEADS * GDN_DK
GDN_V_WIDTH = GDN_V_HEADS * GDN_DV
GDN_CONV_CH = 2 * GDN_QK_WIDTH + GDN_V_WIDTH

F32 = jnp.float32
BF16 = jnp.bfloat16

N_PROMPT_TOK = BATCH * SEQ
N_SAMPLE_TOK = DEC_BATCH * DEC_SEQ
N_TOK = N_PROMPT_TOK + N_SAMPLE_TOK

V7X_VMEM_BYTES = 64 * 1024 * 1024
VMEM_LIMIT = 48 * 1024 * 1024
LANE = 128


def _mm_kernel(x_ref, w_ref, o_ref):
    o_ref[...] = jnp.dot(x_ref[...], w_ref[...].astype(BF16), preferred_element_type=F32).astype(o_ref.dtype)


def _pick_tile(n, prefs, also=0):
    for t in prefs:
        if n % t == 0 and also % t == 0:
            return t
    raise ValueError(f"no tile for {n} (offset {also})")


def matmul(x, w, col0=0, n=None, out_dtype=F32):
    m, k = x.shape
    n = w.shape[1] - col0 if n is None else n
    tm = _pick_tile(m, (640, 512, 320, 256, 128, 64, 32, 16, 8))
    tn = _pick_tile(n, (512, 384, 256, 128), col0)
    c0 = col0 // tn
    return pl.pallas_call(
        _mm_kernel,
        grid=(m // tm, n // tn),
        in_specs=[pl.BlockSpec((tm, k), lambda i, j: (i, 0)),
                  pl.BlockSpec((k, tn), lambda i, j: (0, c0 + j))],
        out_specs=pl.BlockSpec((tm, tn), lambda i, j: (i, j)),
        out_shape=jax.ShapeDtypeStruct((m, n), out_dtype),
        compiler_params=pltpu.CompilerParams(
            dimension_semantics=("parallel", "parallel"), vmem_limit_bytes=VMEM_LIMIT),
        name="proj_matmul",
    )(x, w)


def _mm_wt_kernel(w_ref, xt_ref, o_ref, wt_sc):
    @pl.when(pl.program_id(1) == 0)
    def _():
        wt_sc[...] = w_ref[...].astype(F32).T.astype(BF16)

    o_ref[...] = jnp.dot(wt_sc[...], xt_ref[...], preferred_element_type=F32).astype(o_ref.dtype)


def matmul_wt(w, xt, col0, n, out_dtype=F32):
    k, m = xt.shape
    tr = _pick_tile(n, (512, 256, 128), col0)
    tt = _pick_tile(m, (512, 256, 128))
    r0 = col0 // tr
    return pl.pallas_call(
        _mm_wt_kernel,
        grid=(n // tr, m // tt),
        in_specs=[pl.BlockSpec((k, tr), lambda i, j: (0, r0 + i)),
                  pl.BlockSpec((k, tt), lambda i, j: (0, j))],
        out_specs=pl.BlockSpec((tr, tt), lambda i, j: (i, j)),
        out_shape=jax.ShapeDtypeStruct((n, m), out_dtype),
        scratch_shapes=[pltpu.VMEM((tr, k), BF16)],
        compiler_params=pltpu.CompilerParams(
            dimension_semantics=("parallel", "arbitrary"), vmem_limit_bytes=VMEM_LIMIT),
        name="proj_matmul_wt",
    )(w, xt)


def _mm_ta_kernel(xt_ref, w_ref, o_ref):
    o_ref[...] = lax.dot_general(xt_ref[...], w_ref[...].astype(BF16), (((0,), (0,)), ((), ())),
                                 preferred_element_type=F32).astype(o_ref.dtype)


def matmul_ta(xt, w, out_dtype=F32):
    k, m = xt.shape
    n = w.shape[1]
    tm = _pick_tile(m, (512, 256, 128))
    tn = _pick_tile(n, (512, 384, 256, 128))
    return pl.pallas_call(
        _mm_ta_kernel,
        grid=(m // tm, n // tn),
        in_specs=[pl.BlockSpec((k, tm), lambda i, j: (0, i)),
                  pl.BlockSpec((k, tn), lambda i, j: (0, j))],
        out_specs=pl.BlockSpec((tm, tn), lambda i, j: (i, j)),
        out_shape=jax.ShapeDtypeStruct((m, n), out_dtype),
        compiler_params=pltpu.CompilerParams(
            dimension_semantics=("parallel", "parallel"), vmem_limit_bytes=VMEM_LIMIT),
        name="proj_matmul_ta",
    )(xt, w)


POST_TM = 320
POST_TN = 512


def _post_kernel(x_ref, h_ref, p_ref, g_ref, b_ref, wg_ref, wp_ref, o_ref, obf_ref, y_sc, ybf_sc):
    j = pl.program_id(1)

    @pl.when(j == 0)
    def _():
        t = ALPHA * x_ref[...] + h_ref[...]
        mu = jnp.mean(t, axis=-1, keepdims=True)
        d = t - mu
        var = jnp.mean(d * d, axis=-1, keepdims=True)
        y = d * lax.rsqrt(var + LN_EPS) * g_ref[...] + b_ref[...]
        ybf_sc[...] = y.astype(BF16)
        for jj in range(D_MODEL // POST_TN):
            y_sc[jj] = y[:, jj * POST_TN:(jj + 1) * POST_TN]

    gate = jnp.dot(ybf_sc[...], wg_ref[...].astype(BF16), preferred_element_type=F32)
    ple = jnp.dot(p_ref[...].astype(BF16), wp_ref[...].astype(BF16), preferred_element_type=F32)
    o = y_sc[j] + (1.0 / (1.0 + jnp.exp(-gate))) * ple
    o_ref[...] = o
    obf_ref[...] = o.astype(BF16)


def post_norm_ple(x, h, p_bf, g, b, wg_bf, wp_bf):
    m = x.shape[0]
    tm, tn = POST_TM, POST_TN
    return pl.pallas_call(
        _post_kernel,
        grid=(m // tm, D_MODEL // tn),
        in_specs=[pl.BlockSpec((tm, D_MODEL), lambda i, j: (i, 0)),
                  pl.BlockSpec((tm, D_MODEL), lambda i, j: (i, 0)),
                  pl.BlockSpec((tm, PLE_DIM), lambda i, j: (i, 0)),
                  pl.BlockSpec((1, D_MODEL), lambda i, j: (0, 0)),
                  pl.BlockSpec((1, D_MODEL), lambda i, j: (0, 0)),
                  pl.BlockSpec((D_MODEL, tn), lambda i, j: (0, j)),
                  pl.BlockSpec((PLE_DIM, tn), lambda i, j: (0, j))],
        out_specs=[pl.BlockSpec((tm, tn), lambda i, j: (i, j)),
                   pl.BlockSpec((tm, tn), lambda i, j: (i, j))],
        out_shape=[jax.ShapeDtypeStruct((m, D_MODEL), F32),
                   jax.ShapeDtypeStruct((m, D_MODEL), BF16)],
        scratch_shapes=[pltpu.VMEM((D_MODEL // tn, tm, tn), F32),
                        pltpu.VMEM((tm, D_MODEL), BF16)],
        compiler_params=pltpu.CompilerParams(
            dimension_semantics=("parallel", "arbitrary"), vmem_limit_bytes=VMEM_LIMIT),
        name="post_norm_ple",
    )(x, h, p_bf, g.reshape(1, D_MODEL), b.reshape(1, D_MODEL), wg_bf, wp_bf)


def rel_bucket(dist):
    n = jnp.maximum(dist, 0)
    exact = N_BUCKETS // 2
    logb = exact + (jnp.log(jnp.maximum(n, exact).astype(F32) / exact)
                    / math.log(REL_MAX_DIST / exact) * (N_BUCKETS - exact)).astype(jnp.int32)
    return jnp.where(n < exact, n, jnp.minimum(logb, N_BUCKETS - 1))


def split_tokens(t):
    c = t.shape[-1]
    return (t[:N_PROMPT_TOK].reshape(BATCH, SEQ, c), t[N_PROMPT_TOK:].reshape(DEC_BATCH, DEC_SEQ, c))


def join_tokens(tp, ts):
    c = tp.shape[-1]
    return jnp.concatenate([tp.reshape(N_PROMPT_TOK, c), ts.reshape(N_SAMPLE_TOK, c)], axis=0)


G_A = N_HEADS // KV_A
SWA_KEYS = 2 * WINDOW


def swa_tables(rel_bias, sinks, dec_len):
    def heads_to(b, lead):
        return jnp.moveaxis(b, -1, 0).reshape((KV_A, G_A) + lead)

    dist = jnp.arange(WINDOW)[None, :] - (jnp.arange(SWA_KEYS)[:, None] - WINDOW)
    ok = (dist >= 0) & (dist < WINDOW)
    b = jnp.where(ok[..., None], rel_bias[rel_bucket(dist)].astype(F32), NEG_BIG)
    bias_p = heads_to(b, (SWA_KEYS, WINDOW)).transpose(0, 2, 1, 3).reshape(KV_A, SWA_KEYS, G_A * WINDOW)
    sink_p = jnp.broadcast_to(sinks.astype(F32).reshape(KV_A, 1, G_A, 1), (KV_A, 1, G_A, WINDOW))
    sink_p = sink_p.reshape(KV_A, 1, G_A * WINDOW)
    key_i = jnp.arange(SWA_KEYS)[None, :]
    dist = jnp.arange(dec_len)[:, None] + WINDOW - key_i
    ok = (dist >= 0) & (dist < WINDOW) & (key_i < WINDOW + dec_len)
    b = jnp.where(ok[..., None], rel_bias[rel_bucket(dist)].astype(F32), NEG_BIG)
    bias_s = heads_to(b, (dec_len, SWA_KEYS)).reshape(KV_A, G_A * dec_len, SWA_KEYS)
    sink_s = jnp.broadcast_to(sinks.astype(F32).reshape(KV_A, G_A, 1, 1), (KV_A, G_A, dec_len, LANE))
    sink_s = sink_s.reshape(KV_A, G_A * dec_len, LANE)
    return bias_p, sink_p, bias_s, sink_s


def _swa_prompt_kernel(qT_ref, zT_ref, vTp_ref, vTc_ref, kp_ref, kc_ref, bias_ref, sink_ref, o_ref, *, cdt):
    first = pl.program_id(1) == 0
    kk = jnp.concatenate([kp_ref[...], kc_ref[...]], axis=0)
    vT = jnp.concatenate([vTp_ref[...], vTc_ref[...]], axis=1)
    prev_key = lax.broadcasted_iota(jnp.int32, (SWA_KEYS, G_A * WINDOW), 0) < WINDOW
    pw = 2 * HEAD_DIM
    for j in range(KV_A):
        kpair = kk[:, (j // 2) * pw:(j // 2 + 1) * pw].astype(cdt)
        qj = jnp.concatenate([qT_ref[(G_A * j + g) * HEAD_DIM:(G_A * j + g + 1) * HEAD_DIM, :]
                              for g in range(G_A)], axis=1)
        qj = (qj.astype(F32) * (HEAD_DIM ** -0.5)).astype(cdt)
        zpad = jnp.zeros_like(qj)
        rhs = jnp.concatenate([qj, zpad] if j % 2 == 0 else [zpad, qj], axis=0)
        s = jnp.dot(kpair, rhs, preferred_element_type=F32) + bias_ref[j]
        s = jnp.where(prev_key, jnp.where(first, NEG_BIG, s), s)
        sink = sink_ref[j]
        m = jnp.maximum(jnp.max(s, axis=0, keepdims=True), sink)
        e = jnp.exp(s - m)
        den = jnp.sum(e, axis=0, keepdims=True) + jnp.exp(sink - m)
        p = (e * (1.0 / den)).astype(cdt)
        acc = jnp.dot(vT[j * HEAD_DIM:(j + 1) * HEAD_DIM, :].astype(cdt), p, preferred_element_type=F32)
        for g in range(G_A):
            r0 = (G_A * j + g) * HEAD_DIM
            z = zT_ref[r0:r0 + HEAD_DIM, :]
            o_ref[r0:r0 + HEAD_DIM, :] = (acc[:, g * WINDOW:(g + 1) * WINDOW]
                                          * (z * (1.0 / (1.0 + jnp.exp(-z))))).astype(o_ref.dtype)


def swa_prompt(qT, vT, zT, kv_nat, bias_p, sink_p, *, n_batch, seq_len, cdt=BF16):
    nb = seq_len // WINDOW
    cur = lambda b, i: b * nb + i
    prev = lambda b, i: b * nb + jnp.maximum(i - 1, 0)
    v_row_blk = 0
    return pl.pallas_call(
        functools.partial(_swa_prompt_kernel, cdt=cdt),
        grid=(n_batch, nb),
        in_specs=[pl.BlockSpec((ATT_WIDTH, WINDOW), lambda b, i: (0, cur(b, i))),
                  pl.BlockSpec((ATT_WIDTH, WINDOW), lambda b, i: (0, cur(b, i))),
                  pl.BlockSpec((A_KV, WINDOW), lambda b, i: (v_row_blk, prev(b, i))),
                  pl.BlockSpec((A_KV, WINDOW), lambda b, i: (v_row_blk, cur(b, i))),
                  pl.BlockSpec((WINDOW, A_KV), lambda b, i: (prev(b, i), 0)),
                  pl.BlockSpec((WINDOW, A_KV), lambda b, i: (cur(b, i), 0)),
                  pl.BlockSpec(bias_p.shape, lambda b, i: (0, 0, 0)),
                  pl.BlockSpec(sink_p.shape, lambda b, i: (0, 0, 0))],
        out_specs=pl.BlockSpec((ATT_WIDTH, WINDOW), lambda b, i: (0, cur(b, i))),
        out_shape=jax.ShapeDtypeStruct((ATT_WIDTH, n_batch * seq_len), BF16),
        compiler_params=pltpu.CompilerParams(
            dimension_semantics=("parallel", "parallel"), vmem_limit_bytes=VMEM_LIMIT),
        name="swa_prompt",
    )(qT, zT, vT, vT, kv_nat, kv_nat, bias_p, sink_p)


def _swa_sample_kernel(q_ref, z_ref, k_ref, v_ref, bias_ref, sink_ref, o_ref, *, cdt):
    for j in range(KV_A):
        s = lax.dot_general(q_ref[0, j], k_ref[0, j].astype(cdt), (((1,), (1,)), ((), ())),
                            preferred_element_type=F32) + bias_ref[j]
        sink = sink_ref[j][:, 0:1]
        m = jnp.maximum(jnp.max(s, axis=1, keepdims=True), sink)
        e = jnp.exp(s - m)
        den = jnp.sum(e, axis=1, keepdims=True) + jnp.exp(sink - m)
        p = (e * (1.0 / den)).astype(cdt)
        z = z_ref[0, j]
        o_ref[0, j] = jnp.dot(p, v_ref[0, j].astype(cdt), preferred_element_type=F32) * (z * (1.0 / (1.0 + jnp.exp(-z))))


def swa_sample(q_s, z_s, kv_s, kv_cache, bias_s, sink_s, cdt=BF16):
    n_dec = kv_cache.shape[0]
    dec_len = q_s.shape[0] // n_dec
    rows = G_A * dec_len

    def head_rows(t, scale):
        t = t.reshape(n_dec, dec_len, KV_A, G_A, HEAD_DIM).transpose(0, 2, 3, 1, 4) * scale
        return jnp.pad(t.reshape(n_dec, KV_A, rows, HEAD_DIM), ((0, 0), (0, 0), (0, 0), (0, HEAD_DIM)))

    new = kv_s.reshape(n_dec, dec_len, 2, KV_A, HEAD_DIM)
    cat = jnp.concatenate([kv_cache, new], axis=1)
    keys = jnp.pad(cat.transpose(2, 0, 3, 1, 4),
                   ((0, 0), (0, 0), (0, 0), (0, SWA_KEYS - WINDOW - dec_len), (0, HEAD_DIM)))
    blk = lambda r: pl.BlockSpec((1, KV_A, r, 2 * HEAD_DIM), lambda b: (b, 0, 0, 0))
    o = pl.pallas_call(
        functools.partial(_swa_sample_kernel, cdt=cdt),
        grid=(n_dec,),
        in_specs=[blk(rows), blk(rows), blk(SWA_KEYS), blk(SWA_KEYS),
                  pl.BlockSpec(bias_s.shape, lambda b: (0, 0, 0)),
                  pl.BlockSpec(sink_s.shape, lambda b: (0, 0, 0))],
        out_specs=blk(rows),
        out_shape=jax.ShapeDtypeStruct((n_dec, KV_A, rows, 2 * HEAD_DIM), F32),
        compiler_params=pltpu.CompilerParams(dimension_semantics=("parallel",), vmem_limit_bytes=VMEM_LIMIT),
        name="swa_sample",
    )(head_rows(q_s, HEAD_DIM ** -0.5).astype(cdt), head_rows(z_s, 1.0), keys[0], keys[1], bias_s, sink_s)
    o = o[..., :HEAD_DIM].reshape(n_dec, KV_A, G_A, dec_len, HEAD_DIM).transpose(0, 3, 1, 2, 4)
    return o.reshape(n_dec * dec_len, ATT_WIDTH), cat[:, dec_len:]


S5_SLAB_G = 8
S5_SLAB_CH = S5_SLAB_G * S5_GROUP
S5_SLAB_ST = S5_SLAB_G * S5_STATE
S5_N_SLABS = S5_GROUPS // S5_SLAB_G
S5_CHAINS = 8
S5_HALF_CH = S5_CHAINS * S5_SLAB_CH
S5_T = 256
S5_LT = 2 * S5_SLAB_ST // LANE


def _gelu_tanh(x):
    return 0.5 * x * (1.0 + jnp.tanh(math.sqrt(2.0 / math.pi) * (x + 0.044715 * (x * x * x))))


def s5_tables(a_re, a_im, b_re, b_im, c_re, c_im, log_dt):
    a = lax.complex(a_re, a_im)
    dt = jnp.exp(log_dt)[:, None]
    a_bar = jnp.exp(a * dt)
    b_bar = ((a_bar - 1.0) / a)[..., None] * lax.complex(b_re, b_im)
    eye = jnp.eye(S5_SLAB_G, dtype=F32)

    def b_blk(t):
        t = t.reshape(S5_N_SLABS, S5_SLAB_G, S5_STATE, S5_GROUP)
        return jnp.einsum('ij,sipc->sicjp', eye, t).reshape(S5_N_SLABS, S5_SLAB_CH, S5_SLAB_ST)

    def c_blk(t):
        t = t.reshape(S5_N_SLABS, S5_SLAB_G, S5_GROUP, S5_STATE)
        return jnp.einsum('ij,sicp->sjpic', eye, t).reshape(S5_N_SLABS, S5_SLAB_ST, S5_SLAB_CH)

    bcat = jnp.concatenate([b_blk(b_bar.real), b_blk(b_bar.imag)], axis=2)
    ccat = jnp.concatenate([c_blk(c_re), -c_blk(c_im)], axis=1)
    a_cat = jnp.concatenate([a_bar.real.reshape(S5_N_SLABS, S5_SLAB_ST),
                             a_bar.imag.reshape(S5_N_SLABS, S5_SLAB_ST)], axis=1)
    return a_cat, bcat, ccat


def _s5_prompt_kernel(u_ref, bcat_ref, ccat_ref, a_ref, d_ref, y_ref, hout_ref, sc, h_sc, *, cdt):
    tc = pl.program_id(2)
    n_lt_half = S5_LT // 2

    @pl.when(tc == 0)
    def _():
        h_sc[...] = jnp.zeros_like(h_sc)

    for j in range(S5_CHAINS):
        uj = u_ref[:, j * S5_SLAB_CH:(j + 1) * S5_SLAB_CH].astype(cdt)
        bu = jnp.dot(uj, bcat_ref[0, j], preferred_element_type=F32)
        for lt in range(S5_LT):
            sc[lt, pl.ds(j, S5_T, stride=S5_CHAINS), :] = bu[:, lt * LANE:(lt + 1) * LANE]

    a_re = [a_ref[0, :, lt * LANE:(lt + 1) * LANE] for lt in range(n_lt_half)]
    a_im = [a_ref[0, :, (n_lt_half + lt) * LANE:(n_lt_half + lt + 1) * LANE] for lt in range(n_lt_half)]

    def step(t, h):
        r0 = pl.multiple_of(t * S5_CHAINS, S5_CHAINS)
        new = list(h)
        for lt in range(n_lt_half):
            hr, hi = h[lt], h[n_lt_half + lt]
            nr = a_re[lt] * hr - a_im[lt] * hi + sc[lt, pl.ds(r0, S5_CHAINS), :]
            ni = a_re[lt] * hi + a_im[lt] * hr + sc[n_lt_half + lt, pl.ds(r0, S5_CHAINS), :]
            sc[lt, pl.ds(r0, S5_CHAINS), :] = nr
            sc[n_lt_half + lt, pl.ds(r0, S5_CHAINS), :] = ni
            new[lt], new[n_lt_half + lt] = nr, ni
        return tuple(new)

    h = lax.fori_loop(0, S5_T, step, tuple(h_sc[lt] for lt in range(S5_LT)), unroll=8)
    for lt in range(S5_LT):
        h_sc[lt] = h[lt]
        hout_ref[0, 0, :, lt * LANE:(lt + 1) * LANE] = h[lt]

    for j in range(S5_CHAINS):
        hcat = jnp.concatenate([sc[lt, pl.ds(j, S5_T, stride=S5_CHAINS), :] for lt in range(S5_LT)], axis=1)
        cols = slice(j * S5_SLAB_CH, (j + 1) * S5_SLAB_CH)
        y = jnp.dot(hcat.astype(cdt), ccat_ref[0, j], preferred_element_type=F32) + d_ref[0, :, cols] * u_ref[:, cols]
        y_ref[:, cols] = _gelu_tanh(y)


def s5_prompt(proj, a_cat, bcat, ccat, d_skip, *, n_batch, seq_len, n_rows_out, cdt=BF16):
    n_t = seq_len // S5_T
    n_half = S5_WIDTH // S5_HALF_CH
    half = lambda t: t.reshape((n_half, S5_CHAINS) + t.shape[1:])
    return pl.pallas_call(
        functools.partial(_s5_prompt_kernel, cdt=cdt),
        grid=(n_batch, n_half, n_t),
        in_specs=[pl.BlockSpec((S5_T, S5_HALF_CH), lambda b, hf, t: (b * n_t + t, hf)),
                  pl.BlockSpec((1, S5_CHAINS, S5_SLAB_CH, 2 * S5_SLAB_ST), lambda b, hf, t: (hf, 0, 0, 0)),
                  pl.BlockSpec((1, S5_CHAINS, 2 * S5_SLAB_ST, S5_SLAB_CH), lambda b, hf, t: (hf, 0, 0, 0)),
                  pl.BlockSpec((1, S5_CHAINS, 2 * S5_SLAB_ST), lambda b, hf, t: (hf, 0, 0)),
                  pl.BlockSpec((1, 1, S5_HALF_CH), lambda b, hf, t: (hf, 0, 0))],
        out_specs=[pl.BlockSpec((S5_T, S5_HALF_CH), lambda b, hf, t: (b * n_t + t, hf)),
                   pl.BlockSpec((1, 1, S5_CHAINS, 2 * S5_SLAB_ST), lambda b, hf, t: (b, hf, 0, 0))],
        out_shape=[jax.ShapeDtypeStruct((n_rows_out, S5_WIDTH), F32),
                   jax.ShapeDtypeStruct((n_batch, n_half, S5_CHAINS, 2 * S5_SLAB_ST), F32)],
        scratch_shapes=[pltpu.VMEM((S5_LT, S5_T * S5_CHAINS, LANE), F32),
                        pltpu.VMEM((S5_LT, S5_CHAINS, LANE), F32)],
        compiler_params=pltpu.CompilerParams(
            dimension_semantics=("parallel", "parallel", "arbitrary"), vmem_limit_bytes=VMEM_LIMIT),
        name="s5_prompt",
    )(proj, half(bcat.astype(cdt)), half(ccat.astype(cdt)), half(a_cat), d_skip.reshape(n_half, 1, S5_HALF_CH))


def _s5_sample_kernel(u_ref, bcat_ref, ccat_ref, a_ref, d_ref, h0_ref, y_ref, hout_ref, sc, *, n_b, n_t, cdt):
    u = u_ref[...]
    bu = jnp.dot(u.astype(cdt), bcat_ref[0], preferred_element_type=F32)
    a_re = a_ref[0, :, :S5_SLAB_ST]
    a_im = a_ref[0, :, S5_SLAB_ST:]
    for bg in range(n_b // 8):
        hr = h0_ref[0, bg * 8:(bg + 1) * 8, :S5_SLAB_ST]
        hi = h0_ref[0, bg * 8:(bg + 1) * 8, S5_SLAB_ST:]
        for t in range(n_t):
            r = t * n_b + bg * 8
            hr, hi = (a_re * hr - a_im * hi + bu[r:r + 8, :S5_SLAB_ST],
                      a_re * hi + a_im * hr + bu[r:r + 8, S5_SLAB_ST:])
            sc[r:r + 8, :S5_SLAB_ST] = hr
            sc[r:r + 8, S5_SLAB_ST:] = hi
        hout_ref[0, bg * 8:(bg + 1) * 8, :S5_SLAB_ST] = hr
        hout_ref[0, bg * 8:(bg + 1) * 8, S5_SLAB_ST:] = hi
    y = jnp.dot(sc[...].astype(cdt), ccat_ref[0], preferred_element_type=F32) + d_ref[0] * u
    y_ref[...] = _gelu_tanh(y)


def s5_sample(u_tb, a_cat, bcat, ccat, d_skip, h0_cat, *, n_b, n_t, cdt=BF16):
    rows = n_t * n_b
    return pl.pallas_call(
        functools.partial(_s5_sample_kernel, n_b=n_b, n_t=n_t, cdt=cdt),
        grid=(S5_N_SLABS,),
        in_specs=[pl.BlockSpec((rows, S5_SLAB_CH), lambda s: (0, s)),
                  pl.BlockSpec((1, S5_SLAB_CH, 2 * S5_SLAB_ST), lambda s: (s, 0, 0)),
                  pl.BlockSpec((1, 2 * S5_SLAB_ST, S5_SLAB_CH), lambda s: (s, 0, 0)),
                  pl.BlockSpec((1, 1, 2 * S5_SLAB_ST), lambda s: (s, 0, 0)),
                  pl.BlockSpec((1, 1, S5_SLAB_CH), lambda s: (s, 0, 0)),
                  pl.BlockSpec((1, n_b, 2 * S5_SLAB_ST), lambda s: (s, 0, 0))],
        out_specs=[pl.BlockSpec((rows, S5_SLAB_CH), lambda s: (0, s)),
                   pl.BlockSpec((1, n_b, 2 * S5_SLAB_ST), lambda s: (s, 0, 0))],
        out_shape=[jax.ShapeDtypeStruct((rows, S5_WIDTH), F32),
                   jax.ShapeDtypeStruct((S5_N_SLABS, n_b, 2 * S5_SLAB_ST), F32)],
        scratch_shapes=[pltpu.VMEM((rows, 2 * S5_SLAB_ST), F32)],
        compiler_params=pltpu.CompilerParams(
            dimension_semantics=("arbitrary",), vmem_limit_bytes=VMEM_LIMIT),
        name="s5_sample",
    )(u_tb, bcat.astype(cdt), ccat.astype(cdt), a_cat.reshape(S5_N_SLABS, 1, 2 * S5_SLAB_ST),
      d_skip.reshape(S5_N_SLABS, 1, S5_SLAB_CH), h0_cat)


GLU_TM = 320
GLU_TN = 512


def _glu_kernel(yfull_ref, w_ref, ycol_ref, z_ref, o_ref, ybf_sc):
    @pl.when(pl.program_id(1) == 0)
    def _():
        ybf_sc[...] = yfull_ref[...].astype(ybf_sc.dtype)

    glu = jnp.dot(ybf_sc[...], w_ref[...].astype(ybf_sc.dtype), preferred_element_type=F32)
    z = z_ref[...]
    y = ycol_ref[...]
    o_ref[...] = (y * (1.0 / (1.0 + jnp.exp(-glu))) * (z * (1.0 / (1.0 + jnp.exp(-z))))).astype(o_ref.dtype)


def s5_glu_gate(y, w_glu, proj, row_off, cdt=BF16):
    m = y.shape[0]
    tm, tn = _pick_tile(m, (256, 128, 64, 32, 16)), GLU_TN
    assert row_off % tm == 0
    z_off, r_off = S5_WIDTH // tn, row_off // tm
    return pl.pallas_call(
        _glu_kernel,
        grid=(m // tm, S5_WIDTH // tn),
        in_specs=[pl.BlockSpec((tm, S5_WIDTH), lambda i, j: (i, 0)),
                  pl.BlockSpec((S5_WIDTH, tn), lambda i, j: (0, j)),
                  pl.BlockSpec((tm, tn), lambda i, j: (i, j)),
                  pl.BlockSpec((tm, tn), lambda i, j: (r_off + i, z_off + j))],
        out_specs=pl.BlockSpec((tm, tn), lambda i, j: (i, j)),
        out_shape=jax.ShapeDtypeStruct((m, S5_WIDTH), BF16),
        scratch_shapes=[pltpu.VMEM((tm, S5_WIDTH), cdt)],
        compiler_params=pltpu.CompilerParams(
            dimension_semantics=("parallel", "arbitrary"), vmem_limit_bytes=VMEM_LIMIT),
        name="s5_glu_gate",
    )(y, w_glu, y, proj)


def s5_layer(proj, state_in, tables, d_skip, w_glu, *, n_batch, seq_len, n_dec, dec_len, cdt=BF16):
    a_cat, bcat, ccat = tables
    n_p = n_batch * seq_len
    n_s = n_dec * dec_len
    y_p, h_p = s5_prompt(proj, a_cat, bcat, ccat, d_skip, n_batch=n_batch, seq_len=seq_len, n_rows_out=n_p, cdt=cdt)
    u_tb = jnp.swapaxes(proj[n_p:, :S5_WIDTH].reshape(n_dec, dec_len, S5_WIDTH), 0, 1).reshape(n_s, S5_WIDTH)
    h0 = state_in.reshape(n_dec, S5_N_SLABS, S5_SLAB_ST, 2)
    h0_cat = jnp.concatenate([jnp.swapaxes(h0[..., 0], 0, 1), jnp.swapaxes(h0[..., 1], 0, 1)], axis=-1)
    y_tb, h_s = s5_sample(u_tb, a_cat, bcat, ccat, d_skip, h0_cat, n_b=n_dec, n_t=dec_len, cdt=cdt)
    y_s = jnp.swapaxes(y_tb.reshape(dec_len, n_dec, S5_WIDTH), 0, 1).reshape(n_s, S5_WIDTH)
    gated_p = s5_glu_gate(y_p, w_glu, proj, 0, cdt=cdt)
    gated_s = s5_glu_gate(y_s, w_glu, proj, n_p, cdt=cdt)
    hp = h_p.reshape(n_batch, S5_N_SLABS, 2, S5_SLAB_ST)
    st_p = jnp.stack([hp[:, :, 0], hp[:, :, 1]], axis=-1).reshape(n_batch, S5_GROUPS, S5_STATE, 2)
    hs = jnp.swapaxes(h_s, 0, 1).reshape(n_dec, S5_N_SLABS, 2, S5_SLAB_ST)
    st_s = jnp.stack([hs[:, :, 0], hs[:, :, 1]], axis=-1).reshape(n_dec, S5_GROUPS, S5_STATE, 2)
    return gated_p, gated_s, st_p, st_s


GDN_CONV_TT = 256
GDN_CONV_CW = 1024
GDN_HIST = 8
GDN_HB = 16
GDN_TT = 256
GDN_SAMPLE_ROWS = 8


def _gdn_conv_kernel(x_ref, hist_ref, w_ref, o_ref, ext_sc, *, rows, n_t, zero_first):
    i, j = pl.program_id(0), pl.program_id(1)
    hist = hist_ref[...]
    if zero_first:
        hist = jnp.where(i % n_t == 0, 0.0, hist)
    ext_sc[0:GDN_HIST] = hist
    ext_sc[GDN_HIST:GDN_HIST + rows] = x_ref[...]
    acc = x_ref[...] * w_ref[GDN_CONV - 1:GDN_CONV, :]
    for s in range(1, GDN_CONV):
        acc = acc + ext_sc[GDN_HIST - s:GDN_HIST - s + rows] * w_ref[GDN_CONV - 1 - s:GDN_CONV - s, :]
    conv = acc * (1.0 / (1.0 + jnp.exp(-acc)))
    n_qk_blocks = 2 * GDN_QK_WIDTH // GDN_CONV_CW

    @pl.when(j >= n_qk_blocks)
    def _():
        o_ref[...] = conv

    @pl.when(j < n_qk_blocks)
    def _():
        scale = jnp.where(j < GDN_QK_WIDTH // GDN_CONV_CW, GDN_DK ** -0.5, 1.0)
        for h in range(GDN_CONV_CW // GDN_DK):
            t = conv[:, h * GDN_DK:(h + 1) * GDN_DK]
            n = t * lax.rsqrt(jnp.sum(t * t, axis=-1, keepdims=True) + 1e-6)
            o_ref[:, h * GDN_DK:(h + 1) * GDN_DK] = n * scale


def gdn_conv(x, hist_src, conv_w, *, rows, n_blocks, n_t, data_map, hist_map, zero_first):
    n_out = n_blocks * rows
    return pl.pallas_call(
        functools.partial(_gdn_conv_kernel, rows=rows, n_t=n_t, zero_first=zero_first),
        grid=(n_blocks, GDN_CONV_CH // GDN_CONV_CW),
        in_specs=[pl.BlockSpec((rows, GDN_CONV_CW), lambda i, j: (data_map(i), j)),
                  pl.BlockSpec((GDN_HIST, GDN_CONV_CW), lambda i, j: (hist_map(i), j)),
                  pl.BlockSpec((GDN_CONV, GDN_CONV_CW), lambda i, j: (0, j))],
        out_specs=pl.BlockSpec((rows, GDN_CONV_CW), lambda i, j: (i, j)),
        out_shape=jax.ShapeDtypeStruct((n_out, GDN_CONV_CH), F32),
        scratch_shapes=[pltpu.VMEM((GDN_HIST + rows, GDN_CONV_CW), F32)],
        compiler_params=pltpu.CompilerParams(
            dimension_semantics=("parallel", "parallel"), vmem_limit_bytes=VMEM_LIMIT),
        name="gdn_conv",
    )(x, hist_src, conv_w)


def _gdn_chunk_lockstep_kernel(q_ref, k_ref, v_ref, z_ref, ab_ref, alog_ref, dtb_ref, nw_ref, s0_ref, o_ref,
                               sout_ref, s_sc, *, chunk, n_inner, n_tt, valid_len):
    C = chunk
    hb, tt = pl.program_id(1), pl.program_id(2)

    @pl.when(tt == 0)
    def _():
        s_sc[...] = s0_ref[0]

    rowi = lax.broadcasted_iota(jnp.int32, (C, C), 0)
    coli = lax.broadcasted_iota(jnp.int32, (C, C), 1)
    causal = rowi >= coli
    strict = rowi > coli
    ltri = jnp.where(causal, 1.0, 0.0)
    utri = jnp.where(rowi <= coli, 1.0, 0.0)
    eye = jnp.where(rowi == coli, 1.0, 0.0)
    hi = lax.Precision.HIGHEST
    shift = (LANE - hb * GDN_HB) % LANE
    alog = pltpu.roll(jnp.broadcast_to(alog_ref[...], (8, LANE)), shift, 1)[0:1]
    dtb = pltpu.roll(jnp.broadcast_to(dtb_ref[...], (8, LANE)), shift, 1)[0:1]
    nw = nw_ref[...]
    tok_valid = lax.broadcasted_iota(jnp.int32, (C, LANE), 0) < valid_len
    dot = functools.partial(jnp.dot, preferred_element_type=F32)
    dot_nt = lambda a, b: lax.dot_general(a, b, (((1,), (1,)), ((), ())), preferred_element_type=F32)
    dot_tn = lambda a, b: lax.dot_general(a, b, (((0,), (0,)), ((), ())), preferred_element_type=F32)
    units = [(c, i) for c in range(n_inner) for i in range(GDN_HB)]
    rows = lambda c: slice(c * C, (c + 1) * C)
    qk_cols = lambda i: slice((i // 2) * GDN_DK, (i // 2 + 1) * GDN_DK)
    v_cols = lambda i: slice(i * GDN_DV, (i + 1) * GDN_DV)

    g_all, beta_all = [], []
    for c in range(n_inner):
        ab = pltpu.roll(ab_ref[rows(c), :], shift, 1)
        xa = ab + dtb
        softplus = jnp.maximum(xa, 0.0) + jnp.log1p(jnp.exp(-jnp.abs(xa)))
        g_all.append(jnp.where(tok_valid, -jnp.exp(alog) * softplus, 0.0))
        beta_all.append(jnp.where(tok_valid, 1.0 / (1.0 + jnp.exp(-ab)), 0.0))
    gam_all = [jnp.dot(ltri, g, preferred_element_type=F32, precision=hi) for g in g_all]
    gamT_all = [lax.dot_general(g, utri, (((0,), (0,)), ((), ())), preferred_element_type=F32, precision=hi)
                for g in g_all]

    qkk = [dot_nt(jnp.concatenate([q_ref[rows(c), qk_cols(i)], k_ref[rows(c), qk_cols(i)]], axis=0).astype(BF16),
                  k_ref[rows(c), qk_cols(i)].astype(BF16)) for c, i in units]
    gam_c = [jnp.broadcast_to(gam_all[c][:, i:i + 1], (C, LANE)) for c, i in units]
    beta_c = [jnp.broadcast_to(beta_all[c][:, 32 + i:33 + i], (C, LANE)) for c, i in units]
    gam_last = [jnp.broadcast_to(gam_all[c][C - 1:C, i:i + 1], (1, LANE)) for c, i in units]
    decay = [jnp.where(causal, jnp.exp(jnp.where(causal, gc[:, :C] - jnp.broadcast_to(gamT_all[c][i:i + 1, :], (C, C)),
                                                 0.0)), 0.0) for gc, (c, i) in zip(gam_c, units)]
    qk = [(x[:C] * d).astype(BF16) for x, d in zip(qkk, decay)]
    neg_a = [jnp.where(strict, -(b[:, :C] * x[C:] * d), 0.0) for b, x, d in zip(beta_c, qkk, decay)]
    p_inv = [eye + n for n in neg_a]
    m_pow = neg_a
    for _ in range(int(math.log2(C)) - 1):
        m_pow = [dot(m.astype(BF16), m.astype(BF16)) for m in m_pow]
        p_inv = [p + dot(p.astype(BF16), m.astype(BF16)) for p, m in zip(p_inv, m_pow)]
    eg = [jnp.exp(gc) for gc in gam_c]
    sol = [dot(p.astype(BF16), jnp.concatenate([b * v_ref[rows(c), v_cols(i)], (b * e) * k_ref[rows(c), qk_cols(i)]],
                                               axis=1).astype(BF16))
           for p, b, e, (c, i) in zip(p_inv, beta_c, eg, units)]
    wq = [jnp.concatenate([s[:, GDN_DV:], q_ref[rows(c), qk_cols(i)] * e], axis=0).astype(BF16)
          for s, e, (c, i) in zip(sol, eg, units)]
    k_dec = [(k_ref[rows(c), qk_cols(i)] * jnp.exp(gl - gc)).astype(BF16)
             for gl, gc, (c, i) in zip(gam_last, gam_c, units)]

    state = [s_sc[i] for i in range(GDN_HB)]
    for c in range(n_inner):
        base = c * GDN_HB
        ws = [dot(wq[base + i], state[i].astype(BF16)) for i in range(GDN_HB)]
        v_new = [(sol[base + i][:, :GDN_DV] - ws[i][:C]).astype(BF16) for i in range(GDN_HB)]
        o = [ws[i][C:] + dot(qk[base + i], v_new[i]) for i in range(GDN_HB)]
        state = [state[i] * jnp.exp(gam_last[base + i]) + dot_tn(k_dec[base + i], v_new[i]) for i in range(GDN_HB)]
        for i in range(GDN_HB):
            rms = lax.rsqrt(jnp.mean(o[i] * o[i], axis=-1, keepdims=True) + 1e-6)
            zz = z_ref[rows(c), v_cols(i)]
            o_ref[rows(c), v_cols(i)] = (o[i] * rms * nw * (zz * (1.0 / (1.0 + jnp.exp(-zz))))).astype(o_ref.dtype)
    for i in range(GDN_HB):
        s_sc[i] = state[i]

    @pl.when(tt == n_tt - 1)
    def _():
        sout_ref[0] = s_sc[...]


def gdn_chunk(conv, z, ab, a_log, dt_bias, norm_w, s0, *, n_seq, rows_per_seq, rows_per_step, chunk, valid_len,
              z_col_off, out_dtype):
    n_tt = rows_per_seq // rows_per_step
    n_inner = rows_per_step // chunk
    n_hb = GDN_V_HEADS // GDN_HB
    qw, vw = GDN_HB // 2 * GDN_DK, GDN_HB * GDN_DV
    k_off, v_off, z_off = GDN_QK_WIDTH // qw, 2 * GDN_QK_WIDTH // vw, z_col_off // vw
    row = lambda b, hb, t: b * n_tt + t
    pad_row = lambda p: jnp.pad(p.astype(F32), (0, LANE - p.shape[0])).reshape(1, LANE)
    return pl.pallas_call(
        functools.partial(_gdn_chunk_lockstep_kernel, chunk=chunk, n_inner=n_inner, n_tt=n_tt, valid_len=valid_len),
        grid=(n_seq, n_hb, n_tt),
        in_specs=[pl.BlockSpec((rows_per_step, qw), lambda b, hb, t: (row(b, hb, t), hb)),
                  pl.BlockSpec((rows_per_step, qw), lambda b, hb, t: (row(b, hb, t), k_off + hb)),
                  pl.BlockSpec((rows_per_step, vw), lambda b, hb, t: (row(b, hb, t), v_off + hb)),
                  pl.BlockSpec((rows_per_step, vw), lambda b, hb, t: (row(b, hb, t), z_off + hb)),
                  pl.BlockSpec((rows_per_step, LANE), lambda b, hb, t: (row(b, hb, t), 0)),
                  pl.BlockSpec((1, LANE), lambda b, hb, t: (0, 0)),
                  pl.BlockSpec((1, LANE), lambda b, hb, t: (0, 0)),
                  pl.BlockSpec((1, GDN_DV), lambda b, hb, t: (0, 0)),
                  pl.BlockSpec((1, GDN_HB, GDN_DK, GDN_DV), lambda b, hb, t: (b, hb, 0, 0))],
        out_specs=[pl.BlockSpec((rows_per_step, vw), lambda b, hb, t: (row(b, hb, t), hb)),
                   pl.BlockSpec((1, GDN_HB, GDN_DK, GDN_DV), lambda b, hb, t: (b, hb, 0, 0))],
        out_shape=[jax.ShapeDtypeStruct((n_seq * rows_per_seq, GDN_V_WIDTH), out_dtype),
                   jax.ShapeDtypeStruct((n_seq, GDN_V_HEADS, GDN_DK, GDN_DV), F32)],
        scratch_shapes=[pltpu.VMEM((GDN_HB, GDN_DK, GDN_DV), F32)],
        compiler_params=pltpu.CompilerParams(
            dimension_semantics=("parallel", "parallel", "arbitrary"), vmem_limit_bytes=VMEM_LIMIT),
        name="gdn_chunk",
    )(conv, conv, conv, z, ab, pad_row(a_log), pad_row(dt_bias), norm_w.astype(F32).reshape(1, GDN_DV), s0)


def gdn_layer(qkvz, ab, state_in, conv_in, conv_w, a_log, dt_bias, norm_w, *, n_batch, seq_len, n_dec, dec_len):
    n_p = n_batch * seq_len
    n_tp = seq_len // GDN_CONV_TT
    hist_per_block = GDN_CONV_TT // GDN_HIST
    conv_p = gdn_conv(qkvz, qkvz, conv_w, rows=GDN_CONV_TT, n_blocks=n_batch * n_tp, n_t=n_tp,
                      data_map=lambda i: i, hist_map=lambda i: jnp.maximum(i * hist_per_block - 1, 0),
                      zero_first=True)
    zeros_s = jnp.zeros((n_batch, GDN_V_HEADS, GDN_DK, GDN_DV), F32)
    gated_p, st_p = gdn_chunk(conv_p, qkvz, ab, a_log, dt_bias, norm_w, zeros_s, n_seq=n_batch,
                              rows_per_seq=seq_len, rows_per_step=GDN_TT, chunk=GDN_CHUNK, valid_len=GDN_CHUNK,
                              z_col_off=GDN_CONV_CH, out_dtype=BF16)
    buf_p = jnp.stack([lax.slice(qkvz, ((b + 1) * seq_len - (GDN_CONV - 1), 0), ((b + 1) * seq_len, GDN_CONV_CH))
                       for b in range(n_batch)])
    R = GDN_SAMPLE_ROWS
    x_s = qkvz[n_p:].reshape(n_dec, dec_len, -1)
    pad_t = lambda t, front: jnp.pad(t, ((0, 0), (front, R - front - t.shape[1]), (0, 0)))
    ext = jnp.concatenate([pad_t(conv_in, R - (GDN_CONV - 1)), pad_t(x_s[..., :GDN_CONV_CH], 0)], axis=1)
    ext = ext.reshape(n_dec * 2 * R, GDN_CONV_CH)
    conv_s = gdn_conv(ext, ext, conv_w, rows=R, n_blocks=n_dec, n_t=1,
                      data_map=lambda i: 2 * i + 1, hist_map=lambda i: 2 * i, zero_first=False)
    z_s = pad_t(x_s[..., GDN_CONV_CH:], 0).reshape(n_dec * R, GDN_V_WIDTH)
    ab_s = pad_t(ab[n_p:].reshape(n_dec, dec_len, LANE), 0).reshape(n_dec * R, LANE)
    gated_s, st_s = gdn_chunk(conv_s, z_s, ab_s, a_log, dt_bias, norm_w, state_in, n_seq=n_dec, rows_per_seq=R,
                              rows_per_step=R, chunk=R, valid_len=dec_len, z_col_off=0, out_dtype=F32)
    gated_s = gated_s.reshape(n_dec, R, GDN_V_WIDTH)[:, :dec_len].reshape(n_dec * dec_len, GDN_V_WIDTH)
    buf_s = jnp.concatenate([conv_in, x_s[..., :GDN_CONV_CH]], axis=1)[:, dec_len:]
    return gated_p, gated_s, st_p, st_s, buf_p, buf_s


DSA_KC = 256
INT_MIN = -2 ** 31
NEG_BIG = -1e30
G_D = N_HEADS // KV_D
BIAS_WIN = DSA_KC + Q_BLOCK


def _sortable_key(s):
    b = pltpu.bitcast(s, jnp.int32)
    return jnp.where(b < 0, b ^ jnp.int32(0x7FFFFFFF), b)


def _dsa_prompt_kernel(qT_ref, qiT_ref, wiT_ref, zT_ref, ki_ref, k_ref, vT_ref, win_ref, o_ref,
                       key_sc, mask_sc, *, topk, idx_bits, cdt):
    qb = pl.program_id(1)
    t0 = qb * Q_BLOCK
    nch = (qb + 2) // 2
    t_idx = t0 + lax.broadcasted_iota(jnp.int32, (1, Q_BLOCK), 1)
    row_iota = lax.broadcasted_iota(jnp.int32, (DSA_KC, Q_BLOCK), 0)

    def score_chunk(c, carry):
        kic = ki_ref[0, c].astype(cdt)
        acc = jnp.zeros((DSA_KC, Q_BLOCK), F32)
        for hp in range(IDX_HEADS // 2):
            rhs = jnp.concatenate([qiT_ref[(2 * hp) * IDX_DIM:(2 * hp + 1) * IDX_DIM, :],
                                   qiT_ref[(2 * hp + 1) * IDX_DIM:(2 * hp + 2) * IDX_DIM, :]], axis=1)
            s = jnp.dot(kic, rhs, preferred_element_type=F32) * (IDX_DIM ** -0.5)
            s = jnp.maximum(s, 0.0)
            w0 = wiT_ref[2 * hp:2 * hp + 1, :] * (IDX_HEADS ** -0.5)
            w1 = wiT_ref[2 * hp + 1:2 * hp + 2, :] * (IDX_HEADS ** -0.5)
            acc = acc + s[:, :Q_BLOCK] * w0 + s[:, Q_BLOCK:] * w1
        s_idx = c * DSA_KC + row_iota
        key_sc[c] = jnp.where(s_idx <= t_idx, _sortable_key(acc), INT_MIN)
        return carry

    lax.fori_loop(0, nch, score_chunk, 0)

    def count(pred):
        def body(c, acc):
            hit = pred(key_sc[c], c * DSA_KC + row_iota)
            return acc + hit.reshape(DSA_KC // 8, 8, Q_BLOCK).sum(axis=0)
        acc = lax.fori_loop(0, nch, body, jnp.zeros((8, Q_BLOCK), jnp.int32))
        return jnp.sum(acc, axis=0, keepdims=True)

    c_nonneg = count(lambda k, s: jnp.where(k >= 0, 1, 0))
    thr = jnp.where(c_nonneg >= topk, 0, INT_MIN).astype(jnp.int32)

    def thr_bit(i, thr):
        cand = thr + jnp.left_shift(jnp.int32(1), 30 - i)
        return jnp.where(count(lambda k, s: jnp.where(k >= cand, 1, 0)) >= topk, cand, thr)

    thr = lax.fori_loop(0, 31, thr_bit, thr)
    need = topk - count(lambda k, s: jnp.where(k > thr, 1, 0))

    def lim_bit(i, lim):
        cand = lim + jnp.left_shift(jnp.int32(1), idx_bits - 1 - i)
        c = count(lambda k, s: jnp.where(k == thr, jnp.where(s < cand, 1, 0), 0))
        return jnp.where(c <= need, cand, lim)

    n_ties = count(lambda k, s: jnp.where(k == thr, 1, 0))
    settled = jnp.min(jnp.where(thr == INT_MIN, 1, jnp.where(n_ties == need, 1, 0))) == 1
    lim = lax.cond(settled,
                   lambda: jnp.full((1, Q_BLOCK), 1 << idx_bits, jnp.int32),
                   lambda: lax.fori_loop(0, idx_bits, lim_bit, jnp.zeros((1, Q_BLOCK), jnp.int32)))

    def mask_chunk(c, carry):
        k = key_sc[c]
        s_idx = c * DSA_KC + row_iota
        tie = jnp.where(k == thr, jnp.where(s_idx < lim, 0.0, NEG_BIG), NEG_BIG)
        m = jnp.where(k > thr, 0.0, tie)
        mask_sc[c] = jnp.where(k == INT_MIN, NEG_BIG, m)
        return carry

    lax.fori_loop(0, nch, mask_chunk, 0)

    n_cols = G_D * Q_BLOCK
    half = DSA_KC // 2

    def head_q(j):
        qj = jnp.concatenate([qT_ref[(G_D * j + g) * HEAD_DIM:(G_D * j + g + 1) * HEAD_DIM, :]
                              for g in range(G_D)], axis=1)
        return (qj.astype(F32) * (HEAD_DIM ** -0.5)).astype(cdt)

    for jp in range(KV_D // 2):
        q0, q1 = head_q(2 * jp), head_q(2 * jp + 1)
        zq = jnp.zeros_like(q0)
        rhs = jnp.concatenate([jnp.concatenate([q0, zq], axis=0), jnp.concatenate([zq, q1], axis=0)], axis=1)

        def chunk_body(c, carry, jp=jp, rhs=rhs):
            m, l, acc0, acc1 = carry
            kc = k_ref[0, c, :, jp * 2 * HEAD_DIM:(jp + 1) * 2 * HEAD_DIM].astype(cdt)
            s = jnp.dot(kc, rhs, preferred_element_type=F32)
            wt = win_ref[qb - 2 * c]
            madd = mask_sc[c]
            parts = []
            for hh in range(2 * G_D):
                h = 2 * G_D * jp + hh
                tiles = []
                for u in range(2):
                    lo = (1 - u) * half
                    r = jnp.broadcast_to(wt[h:h + 1, lo:lo + 2 * half], (half, 2 * half))
                    tiles.append(pltpu.roll(r, 0, 1, stride=1, stride_axis=0)[:, half:])
                parts.append(s[:, hh * Q_BLOCK:(hh + 1) * Q_BLOCK] + (jnp.concatenate(tiles, axis=0) + madd))
            s = jnp.concatenate(parts, axis=1)
            m_new = jnp.maximum(m, jnp.max(s, axis=0, keepdims=True))
            alpha = jnp.exp(m - m_new)
            p = jnp.exp(s - m_new)
            l = l * alpha + jnp.sum(p, axis=0, keepdims=True)
            p = p.astype(cdt)
            v0 = vT_ref[0, c, (2 * jp) * HEAD_DIM:(2 * jp + 1) * HEAD_DIM, :]
            v1 = vT_ref[0, c, (2 * jp + 1) * HEAD_DIM:(2 * jp + 2) * HEAD_DIM, :]
            acc0 = acc0 * alpha[:, :n_cols] + jnp.dot(v0, p[:, :n_cols], preferred_element_type=F32)
            acc1 = acc1 * alpha[:, n_cols:] + jnp.dot(v1, p[:, n_cols:], preferred_element_type=F32)
            return m_new, l, acc0, acc1

        init = (jnp.full((1, 2 * n_cols), NEG_BIG, F32), jnp.zeros((1, 2 * n_cols), F32),
                jnp.zeros((HEAD_DIM, n_cols), F32), jnp.zeros((HEAD_DIM, n_cols), F32))
        m, l, acc0, acc1 = lax.fori_loop(0, nch, chunk_body, init)
        inv = 1.0 / l
        for jj, acc in enumerate((acc0, acc1)):
            o = acc * inv[:, jj * n_cols:(jj + 1) * n_cols]
            for g in range(G_D):
                r0 = (G_D * (2 * jp + jj) + g) * HEAD_DIM
                z = zT_ref[r0:r0 + HEAD_DIM, :]
                gate = z * (1.0 / (1.0 + jnp.exp(-z)))
                o_ref[r0:r0 + HEAD_DIM, :] = (o[:, g * Q_BLOCK:(g + 1) * Q_BLOCK] * gate).astype(o_ref.dtype)


def dsa_bias_windows(rel_bias, seq_len):
    o = jnp.arange(seq_len // Q_BLOCK)[:, None]
    m = jnp.arange(BIAS_WIN)[None, :]
    d = jnp.maximum(o * Q_BLOCK + m - DSA_KC, 0)
    return jnp.moveaxis(rel_bias[rel_bucket(d)].astype(F32), -1, 1)


def dsa_prompt_attend(qT, qiT, wiT, zT, ki4, k4, v4T, win, *, n_batch, seq_len, cdt=BF16):
    nqb = seq_len // Q_BLOCK
    nc = seq_len // DSA_KC
    topk = min(TOPK_MAX, seq_len // 4)
    idx_bits = int(math.log2(seq_len)) + 1
    tok = lambda b, q: (0, b * nqb + q)
    per_batch = lambda b, q: (b, 0, 0, 0)
    return pl.pallas_call(
        functools.partial(_dsa_prompt_kernel, topk=topk, idx_bits=idx_bits, cdt=cdt),
        grid=(n_batch, nqb),
        in_specs=[pl.BlockSpec((ATT_WIDTH, Q_BLOCK), tok),
                  pl.BlockSpec((IDX_HEADS * IDX_DIM, Q_BLOCK), tok),
                  pl.BlockSpec((IDX_HEADS, Q_BLOCK), tok),
                  pl.BlockSpec((ATT_WIDTH, Q_BLOCK), tok),
                  pl.BlockSpec((1, nc, DSA_KC, IDX_DIM), per_batch),
                  pl.BlockSpec((1, nc, DSA_KC, D_KV), per_batch),
                  pl.BlockSpec((1, nc, D_KV, DSA_KC), per_batch),
                  pl.BlockSpec((nqb, N_HEADS, BIAS_WIN), lambda b, q: (0, 0, 0))],
        out_specs=pl.BlockSpec((ATT_WIDTH, Q_BLOCK), tok),
        out_shape=jax.ShapeDtypeStruct((ATT_WIDTH, n_batch * seq_len), BF16),
        scratch_shapes=[pltpu.VMEM((nc, DSA_KC, Q_BLOCK), jnp.int32),
                        pltpu.VMEM((nc, DSA_KC, Q_BLOCK), F32)],
        compiler_params=pltpu.CompilerParams(
            dimension_semantics=("parallel", "arbitrary"), vmem_limit_bytes=VMEM_LIMIT),
        name="dsa_prompt_attend",
    )(qT, qiT, wiT, zT, ki4, k4, v4T, win)


DSS_NP1 = 32
DSS_NP2 = 16
DSS_TP = 8
DSS_GROWS = G_D * DSS_TP
T5_LAST_BUCKET_DIST = 1600


def _dsa_sample_select_kernel(pt_ref, qi_ref, wb_ref, kinew_ref, *rest, n_pages, n_new, topk, idx_bits, cdt):
    del pt_ref
    page_refs, mask_ref, key_sc = rest[:DSS_NP1], rest[DSS_NP1], rest[DSS_NP1 + 1]
    s = pl.program_id(1)
    lane = lax.broadcasted_iota(jnp.int32, (DSS_TP, PAGE_SIZE), 1)
    trow = lax.broadcasted_iota(jnp.int32, (DSS_TP, PAGE_SIZE), 0)
    qi = qi_ref[0]
    wb = wb_ref[0]

    def page_keys(kp):
        sc = lax.dot_general(qi, kp.astype(cdt), (((1,), (1,)), ((), ())),
                             preferred_element_type=F32) * (IDX_DIM ** -0.5)
        sc = jnp.maximum(sc, 0.0) * wb
        return _sortable_key(sc.reshape(IDX_HEADS, DSS_TP, PAGE_SIZE).sum(axis=0))

    for i in range(DSS_NP1):
        key_sc[s * DSS_NP1 + i] = page_keys(page_refs[i][0, 0])

    @pl.when(s == 0)
    def _():
        kn = page_keys(kinew_ref[0])
        key_sc[n_pages] = jnp.where(lane < n_new, jnp.where(lane <= trow, kn, INT_MIN), INT_MIN)

    @pl.when(s == n_pages // DSS_NP1 - 1)
    def _():
        all_shape = (n_pages + 1, DSS_TP, PAGE_SIZE)

        def count(pred):
            key_idx = (lax.broadcasted_iota(jnp.int32, all_shape, 0) * PAGE_SIZE
                       + lax.broadcasted_iota(jnp.int32, all_shape, 2))
            acc = pred(key_sc[...], key_idx).sum(axis=0)
            return jnp.broadcast_to(jnp.sum(acc, axis=1, keepdims=True), (DSS_TP, PAGE_SIZE))

        c_nonneg = count(lambda k, i: jnp.where(k >= 0, 1, 0))
        thr = jnp.where(c_nonneg >= topk, 0, INT_MIN).astype(jnp.int32)

        def thr_bit(b, thr):
            cand = thr + jnp.left_shift(jnp.int32(1), 30 - b)
            return jnp.where(count(lambda k, i: jnp.where(k >= cand, 1, 0)) >= topk, cand, thr)

        thr = lax.fori_loop(0, 31, thr_bit, thr)
        need = topk - count(lambda k, i: jnp.where(k > thr, 1, 0))

        def lim_bit(b, lim):
            cand = lim + jnp.left_shift(jnp.int32(1), idx_bits - 1 - b)
            c = count(lambda k, i: jnp.where(k == thr, jnp.where(i < cand, 1, 0), 0))
            return jnp.where(c <= need, cand, lim)

        n_ties = count(lambda k, i: jnp.where(k == thr, 1, 0))
        settled = jnp.min(jnp.where(thr == INT_MIN, 1, jnp.where(n_ties == need, 1, 0))) == 1
        lim = lax.cond(settled,
                       lambda: jnp.full((DSS_TP, PAGE_SIZE), 1 << idx_bits, jnp.int32),
                       lambda: lax.fori_loop(0, idx_bits, lim_bit, jnp.zeros((DSS_TP, PAGE_SIZE), jnp.int32)))

        k = key_sc[...]
        key_idx = (lax.broadcasted_iota(jnp.int32, all_shape, 0) * PAGE_SIZE
                   + lax.broadcasted_iota(jnp.int32, all_shape, 2))
        tie = jnp.where(k == thr, jnp.where(key_idx < lim, 0.0, NEG_BIG), NEG_BIG)
        mask_ref[0] = jnp.where(k == INT_MIN, NEG_BIG, jnp.where(k > thr, 0.0, tie))


def _dsa_sample_attend_kernel(pt_ref, q_ref, z_ref, mask_ref, masknew_ref, bnear_ref, bfar_ref, kvnew_ref, *rest,
                              n_pages, n_far, cdt):
    del pt_ref
    page_refs, o_ref = rest[:DSS_NP2], rest[DSS_NP2]
    m_sc, l_sc, acc_sc = rest[DSS_NP2 + 1:]
    s = pl.program_id(1)
    heads = range(KV_D)
    rep = DSS_GROWS // DSS_TP

    def attend(pages, masks, page_ids):
        n = len(pages)
        madd = jnp.concatenate([jnp.concatenate([mk] * rep, axis=0) for mk in masks], axis=1)
        kT = [jnp.concatenate([pg(0, h) for pg in pages], axis=1).astype(cdt) for h in heads]
        vT = [jnp.concatenate([pg(1, h) for pg in pages], axis=1).astype(cdt) for h in heads]
        logits = [jnp.dot(q_ref[0, h], kT[h], preferred_element_type=F32) for h in heads]
        bias = [jnp.concatenate([jnp.where(pid >= n_far, bnear_ref[jnp.maximum(pid - n_far, 0), h], bfar_ref[h])
                                 for pid in page_ids], axis=1) for h in heads]
        logits = [lg + (b + madd) for lg, b in zip(logits, bias)]
        m_old = [m_sc[h] for h in heads]
        m_new = [jnp.maximum(mo, jnp.broadcast_to(jnp.max(lg, axis=1, keepdims=True), mo.shape))
                 for mo, lg in zip(m_old, logits)]
        alpha = [jnp.exp(mo - mn) for mo, mn in zip(m_old, m_new)]
        p = [jnp.exp(lg - jnp.concatenate([mn] * n, axis=1)) for lg, mn in zip(logits, m_new)]
        pv = [lax.dot_general(ph.astype(cdt), vT[h], (((1,), (1,)), ((), ())), preferred_element_type=F32)
              for h, ph in zip(heads, p)]
        for h in heads:
            l_sc[h] = l_sc[h] * alpha[h] + jnp.broadcast_to(jnp.sum(p[h], axis=1, keepdims=True), alpha[h].shape)
            acc_sc[h] = acc_sc[h] * alpha[h][:, :HEAD_DIM] + pv[h]
            m_sc[h] = m_new[h]

    @pl.when(s == 0)
    def _():
        m_sc[...] = jnp.full_like(m_sc, NEG_BIG)
        l_sc[...] = jnp.zeros_like(l_sc)
        acc_sc[...] = jnp.zeros_like(acc_sc)
        attend([lambda c, h: kvnew_ref[0, c, h]], [masknew_ref[0, 0]], [n_pages])

    attend([(lambda c, h, r=r: r[0, 0, c, h]) for r in page_refs], [mask_ref[0, i] for i in range(DSS_NP2)],
           [s * DSS_NP2 + i for i in range(DSS_NP2)])

    @pl.when(s == n_pages // DSS_NP2 - 1)
    def _():
        for h in heads:
            z = z_ref[0, h]
            o_ref[0, h] = acc_sc[h] * (1.0 / l_sc[h][:, :HEAD_DIM]) * (z * (1.0 / (1.0 + jnp.exp(-z))))


def dsa_sample(q_s, z_s, qi_s, wi_s, ki_s, kv_s, kv_pool, kidx_pool, layer, page_table, rel_bias, cdt=BF16):
    n_dec, n_pages = page_table.shape
    dec_len = q_s.shape[0] // n_dec
    past = n_pages * PAGE_SIZE
    total = past + dec_len
    topk = min(TOPK_MAX, total // 4)
    idx_bits = int(math.log2(total)) + 1
    pad_t = DSS_TP - dec_len
    n_pairs = KV_D // 2
    eye2 = jnp.eye(2, dtype=F32)

    qi = jnp.pad(jnp.swapaxes(qi_s.reshape(n_dec, dec_len, IDX_HEADS, IDX_DIM), 1, 2), ((0, 0), (0, 0), (0, pad_t), (0, 0)))
    qi = qi.reshape(n_dec, IDX_HEADS * DSS_TP, IDX_DIM).astype(cdt)
    wb = jnp.pad(jnp.swapaxes(wi_s.reshape(n_dec, dec_len, IDX_HEADS), 1, 2) * (IDX_HEADS ** -0.5), ((0, 0), (0, 0), (0, pad_t)))
    wb = jnp.broadcast_to(wb.reshape(n_dec, IDX_HEADS * DSS_TP, 1), (n_dec, IDX_HEADS * DSS_TP, PAGE_SIZE))
    ki_new = jnp.pad(ki_s.reshape(n_dec, dec_len, IDX_DIM), ((0, 0), (0, PAGE_SIZE - dec_len), (0, 0)))
    kidx4 = kidx_pool.reshape(kidx_pool.shape[0], kidx_pool.shape[1], PAGE_SIZE, IDX_DIM)
    page_spec = lambda np_, i, width: pl.BlockSpec(
        (1, 1, PAGE_SIZE, width), lambda b, s, pt: (layer, pt[b, s * np_ + i], 0, 0))
    per_b3 = lambda b, s, pt: (b, 0, 0)
    mask = pl.pallas_call(
        functools.partial(_dsa_sample_select_kernel, n_pages=n_pages, n_new=dec_len, topk=topk, idx_bits=idx_bits,
                          cdt=cdt),
        grid_spec=pltpu.PrefetchScalarGridSpec(
            num_scalar_prefetch=1, grid=(n_dec, n_pages // DSS_NP1),
            in_specs=[pl.BlockSpec((1, IDX_HEADS * DSS_TP, IDX_DIM), per_b3),
                      pl.BlockSpec((1, IDX_HEADS * DSS_TP, PAGE_SIZE), per_b3),
                      pl.BlockSpec((1, PAGE_SIZE, IDX_DIM), per_b3)]
                     + [page_spec(DSS_NP1, i, IDX_DIM) for i in range(DSS_NP1)],
            out_specs=pl.BlockSpec((1, n_pages + 1, DSS_TP, PAGE_SIZE), lambda b, s, pt: (b, 0, 0, 0)),
            scratch_shapes=[pltpu.VMEM((n_pages + 1, DSS_TP, PAGE_SIZE), jnp.int32)]),
        out_shape=jax.ShapeDtypeStruct((n_dec, n_pages + 1, DSS_TP, PAGE_SIZE), F32),
        compiler_params=pltpu.CompilerParams(
            dimension_semantics=("parallel", "arbitrary"), vmem_limit_bytes=VMEM_LIMIT),
        name="dsa_sample_select",
    )(page_table, qi, wb, ki_new, *([kidx4] * DSS_NP1))

    def head_rows(t, scale):
        t = t.reshape(n_dec, dec_len, KV_D, G_D, HEAD_DIM).transpose(0, 2, 3, 1, 4) * scale
        return jnp.pad(t, ((0, 0),) * 3 + ((0, pad_t), (0, 0))).reshape(n_dec, KV_D, DSS_GROWS, HEAD_DIM)

    q_hr = head_rows(q_s, HEAD_DIM ** -0.5).astype(cdt)
    z_hr = head_rows(z_s, 1.0)
    n_far = max(0, min(n_pages, (past - (PAGE_SIZE - 1) - T5_LAST_BUCKET_DIST) // PAGE_SIZE + 1))
    near_pages = jnp.arange(n_far, n_pages + 1)
    dist = (past + jnp.arange(DSS_TP)[None, :, None]
            - (near_pages[:, None, None] * PAGE_SIZE + jnp.arange(PAGE_SIZE)[None, None, :]))
    b_near = rel_bias[rel_bucket(dist)].astype(F32)
    b_near = b_near.transpose(0, 3, 1, 2).reshape(n_pages + 1 - n_far, KV_D, DSS_GROWS, PAGE_SIZE)
    b_far = jnp.broadcast_to(rel_bias[N_BUCKETS - 1].astype(F32)[:, None, None], (N_HEADS, DSS_TP, PAGE_SIZE))
    b_far = b_far.reshape(KV_D, DSS_GROWS, PAGE_SIZE)
    kv_new = jnp.pad(kv_s.reshape(n_dec, dec_len, 2, KV_D, HEAD_DIM).transpose(0, 2, 3, 4, 1),
                     ((0, 0),) * 4 + ((0, PAGE_SIZE - dec_len),))
    kv_t = kv_pool.reshape(kv_pool.shape[0], kv_pool.shape[1], PAGE_SIZE, 2, KV_D, HEAD_DIM).transpose(0, 1, 3, 4, 5, 2)
    kv_page = lambda i: pl.BlockSpec((1, 1, 2, KV_D, HEAD_DIM, PAGE_SIZE),
                                     lambda b, s, pt: (layer, pt[b, s * DSS_NP2 + i], 0, 0, 0, 0))
    per_b4 = lambda b, s, pt: (b, 0, 0, 0)
    o = pl.pallas_call(
        functools.partial(_dsa_sample_attend_kernel, n_pages=n_pages, n_far=n_far, cdt=cdt),
        grid_spec=pltpu.PrefetchScalarGridSpec(
            num_scalar_prefetch=1, grid=(n_dec, n_pages // DSS_NP2),
            in_specs=[pl.BlockSpec((1, KV_D, DSS_GROWS, HEAD_DIM), per_b4),
                      pl.BlockSpec((1, KV_D, DSS_GROWS, HEAD_DIM), per_b4),
                      pl.BlockSpec((1, DSS_NP2, DSS_TP, PAGE_SIZE), lambda b, s, pt: (b, s, 0, 0)),
                      pl.BlockSpec((1, 1, DSS_TP, PAGE_SIZE), lambda b, s, pt: (b, n_pages, 0, 0)),
                      pl.BlockSpec(b_near.shape, lambda b, s, pt: (0, 0, 0, 0)),
                      pl.BlockSpec(b_far.shape, lambda b, s, pt: (0, 0, 0)),
                      pl.BlockSpec((1, 2, KV_D, HEAD_DIM, PAGE_SIZE), lambda b, s, pt: (b, 0, 0, 0, 0))]
                     + [kv_page(i) for i in range(DSS_NP2)],
            out_specs=pl.BlockSpec((1, KV_D, DSS_GROWS, HEAD_DIM), per_b4),
            scratch_shapes=[pltpu.VMEM((KV_D, DSS_GROWS, LANE), F32), pltpu.VMEM((KV_D, DSS_GROWS, LANE), F32),
                            pltpu.VMEM((KV_D, DSS_GROWS, HEAD_DIM), F32)]),
        out_shape=jax.ShapeDtypeStruct((n_dec, KV_D, DSS_GROWS, HEAD_DIM), F32),
        compiler_params=pltpu.CompilerParams(
            dimension_semantics=("parallel", "arbitrary"), vmem_limit_bytes=VMEM_LIMIT),
        name="dsa_sample_attend",
    )(page_table, q_hr, z_hr, mask, mask, b_near, b_far, kv_new, *([kv_t] * DSS_NP2))
    o = o.reshape(n_dec, KV_D, G_D, DSS_TP, HEAD_DIM)[:, :, :, :dec_len]
    return o.transpose(0, 3, 1, 2, 4).reshape(n_dec * dec_len, ATT_WIDTH)


def _pad_cols(w, n):
    return jnp.pad(w, ((0, 0), (0, n - w.shape[1])))


def kernel(x_prompt, x_sample, cache_a_kv, state_s5, state_gdn, state_gdn_conv, cache_d_kv, cache_d_kidx,
           page_table, p_prompt, p_sample, rel_bias, ln_g, ln_b, ple_gate_w, ple_w,
           a_w_in, a_sinks, a_w_out,
           s5_w_in, s5_a_re, s5_a_im, s5_b_re, s5_b_im, s5_c_re, s5_c_im, s5_d, s5_log_dt, s5_w_glu, s5_w_out,
           gdn_w_in, gdn_conv_w, gdn_a_log, gdn_dt_bias, gdn_norm_w, gdn_w_out,
           dsa_w_in, dsa_w_out):
    x = join_tokens(x_prompt, x_sample)
    x_bf = x.astype(BF16)
    outs = {}
    (a_w_in, a_w_out, s5_w_in, s5_w_glu, s5_w_out, gdn_w_in, gdn_w_out, dsa_w_in, dsa_w_out, ple_gate_w, ple_w) = (
        w.astype(BF16) for w in (a_w_in, a_w_out, s5_w_in, s5_w_glu, s5_w_out, gdn_w_in, gdn_w_out, dsa_w_in,
                                 dsa_w_out, ple_gate_w, ple_w))

    def post(i, x, h):
        return post_norm_ple(x, h, join_tokens(p_prompt[i], p_sample[i]), ln_g[i], ln_b[i], ple_gate_w[i], ple_w[i])

    w_in = a_w_in[0]
    c_k, c_v, c_z = ATT_WIDTH, ATT_WIDTH + A_KV, ATT_WIDTH + 2 * A_KV
    kv_nat = matmul(x_bf, w_in, c_k, 2 * A_KV)
    xT_bf = x_bf[:N_PROMPT_TOK].T
    x_s = x_bf[N_PROMPT_TOK:]
    qT = matmul_wt(w_in, xT_bf, 0, ATT_WIDTH, out_dtype=BF16)
    vT = matmul_wt(w_in, xT_bf, c_v, A_KV, out_dtype=BF16)
    zT = matmul_wt(w_in, xT_bf, c_z, ATT_WIDTH)
    bias_p, sink_p, bias_s, sink_s = swa_tables(rel_bias, a_sinks[0], DEC_SEQ)
    h_p = matmul_ta(swa_prompt(qT, vT, zT, kv_nat, bias_p, sink_p, n_batch=BATCH, seq_len=SEQ), a_w_out[0])
    gs, outs['a_s'] = swa_sample(matmul(x_s, w_in, 0, ATT_WIDTH), matmul(x_s, w_in, c_z, ATT_WIDTH),
                                 kv_nat[N_PROMPT_TOK:], cache_a_kv[0], bias_s, sink_s)
    outs['a_p'] = kv_nat[:N_PROMPT_TOK].reshape(BATCH, SEQ, 2, KV_A, HEAD_DIM)[:, SEQ - WINDOW:]
    x, x_bf = post(0, x, jnp.concatenate([h_p, matmul(gs.astype(BF16), a_w_out[0])], axis=0))

    proj = matmul(x_bf, s5_w_in[0])
    tables = s5_tables(s5_a_re[0], s5_a_im[0], s5_b_re[0], s5_b_im[0], s5_c_re[0], s5_c_im[0], s5_log_dt[0])
    gp, gs, outs['s5_p'], outs['s5_s'] = s5_layer(proj, state_s5[0], tables, s5_d[0], s5_w_glu[0],
                                                  n_batch=BATCH, seq_len=SEQ, n_dec=DEC_BATCH, dec_len=DEC_SEQ)
    x, x_bf = post(1, x, jnp.concatenate([matmul(gp, s5_w_out[0]), matmul(gs, s5_w_out[0])], axis=0))

    w_in = gdn_w_in[0]
    c_gz = GDN_CONV_CH + GDN_V_WIDTH
    qkvz = matmul(x_bf, w_in, 0, c_gz)
    ab = matmul(x_bf, _pad_cols(w_in[:, c_gz:], LANE))
    gp, gs, outs['gd_p'], outs['gd_s'], outs['gc_p'], outs['gc_s'] = gdn_layer(
        qkvz, ab, state_gdn[0], state_gdn_conv[0], gdn_conv_w[0], gdn_a_log[0], gdn_dt_bias[0], gdn_norm_w[0],
        n_batch=BATCH, seq_len=SEQ, n_dec=DEC_BATCH, dec_len=DEC_SEQ)
    x, x_bf = post(2, x, jnp.concatenate([matmul(gp, gdn_w_out[0]), matmul(gs.astype(BF16), gdn_w_out[0])], axis=0))

    w_in = dsa_w_in[0]
    c_kv = ATT_WIDTH + 2 * D_KV
    c_z = 2 * ATT_WIDTH + 2 * D_KV
    c_qi = c_z + IDX_HEADS * IDX_DIM
    kv_nat = matmul(x_bf, w_in, ATT_WIDTH, 2 * D_KV)
    kiw = matmul(x_bf, _pad_cols(w_in[:, c_qi:], 2 * LANE))
    xT_bf = x_bf[:N_PROMPT_TOK].T
    x_s = x_bf[N_PROMPT_TOK:]
    qT = matmul_wt(w_in, xT_bf, 0, ATT_WIDTH, out_dtype=BF16)
    qiT = matmul_wt(w_in, xT_bf, c_z, IDX_HEADS * IDX_DIM, out_dtype=BF16)
    zT = matmul_wt(w_in, xT_bf, c_kv, ATT_WIDTH)
    wiT = matmul_wt(_pad_cols(w_in[:, c_qi + IDX_DIM:], LANE), xT_bf, 0, LANE)
    nc = SEQ // DSA_KC
    kv_p = kv_nat[:N_PROMPT_TOK]
    v4T = jnp.swapaxes(kv_p[:, D_KV:].astype(BF16).reshape(BATCH, nc, DSA_KC, D_KV), 2, 3)
    gT = dsa_prompt_attend(qT, qiT, wiT, zT, kiw[:N_PROMPT_TOK].reshape(BATCH, nc, DSA_KC, 2 * LANE),
                           kv_p.reshape(BATCH, nc, DSA_KC, 2 * D_KV), v4T, dsa_bias_windows(rel_bias, SEQ),
                           n_batch=BATCH, seq_len=SEQ)
    h_p = matmul_ta(gT, dsa_w_out[0])
    kiw_s = kiw[N_PROMPT_TOK:]
    gs = dsa_sample(matmul(x_s, w_in, 0, ATT_WIDTH), matmul(x_s, w_in, c_kv, ATT_WIDTH),
                    matmul(x_s, w_in, c_z, IDX_HEADS * IDX_DIM),
                    kiw_s[:, IDX_DIM:IDX_DIM + IDX_HEADS], kiw_s[:, :IDX_DIM], kv_nat[N_PROMPT_TOK:],
                    cache_d_kv, cache_d_kidx, 0, page_table, rel_bias)
    h_s = matmul(gs.astype(BF16), dsa_w_out[0])
    outs['dkv_p'] = kv_p.reshape(BATCH, SEQ, 2, KV_D, HEAD_DIM)
    outs['dkv_s'] = kv_nat[N_PROMPT_TOK:].reshape(DEC_BATCH, DEC_SEQ, 2, KV_D, HEAD_DIM)
    outs['dki_p'] = kiw[:N_PROMPT_TOK, :IDX_DIM].reshape(BATCH, SEQ, IDX_DIM)
    outs['dki_s'] = kiw_s[:, :IDX_DIM].reshape(DEC_BATCH, DEC_SEQ, IDX_DIM)
    x, x_bf = post(3, x, jnp.concatenate([h_p, h_s], axis=0))

    yp, ys = split_tokens(x)
    st = lambda name: outs[name][None]
    return (yp, ys, st('a_p'), st('a_s'), st('s5_p'), st('s5_s'), st('gd_p'), st('gd_s'),
            st('gc_p'), st('gc_s'), st('dkv_p'), st('dkv_s'), st('dki_p'), st('dki_s'))
```

```python
import functools
import math

import jax
import jax.numpy as jnp
from jax import lax
from jax.experimental import pallas as pl
from jax.experimental.pallas import tpu as pltpu

D_MODEL = 2048
BATCH = 4
SEQ = 2048
DEPTH = 4
DEC_BATCH = 32
DEC_SEQ = 4
PAGE_SIZE = 128
N_MIXERS = 4
PLE_DIM = 256
ALPHA = (2 * DEPTH) ** 0.25
LN_EPS = 1e-5
N_BUCKETS = 32
REL_MAX_DIST = 2048
N_HEADS = 32
HEAD_DIM = 64
ATT_WIDTH = N_HEADS * HEAD_DIM
WINDOW = 128
KV_A = 4
A_KV = KV_A * HEAD_DIM
KV_D = 8
D_KV = KV_D * HEAD_DIM
IDX_HEADS = 16
IDX_DIM = 128
TOPK_MAX = 256
Q_BLOCK = 128
S5_WIDTH = D_MODEL
S5_GROUP = 16
S5_GROUPS = S5_WIDTH // S5_GROUP
S5_STATE = 64
GDN_QK_HEADS = 16
GDN_V_HEADS = 32
GDN_DK = 128
GDN_DV = 128
GDN_CONV = 4
GDN_CHUNK = 64
GDN_QK_WIDTH = GDN_QK_HEADS * GDN_DK
GDN_V_WIDTH = GDN_V_HEADS * GDN_DV
GDN_CONV_CH = 2 * GDN_QK_WIDTH + GDN_V_WIDTH

F32 = jnp.float32
BF16 = jnp.bfloat16

N_PROMPT_TOK = BATCH * SEQ
N_SAMPLE_TOK = DEC_BATCH * DEC_SEQ
N_TOK = N_PROMPT_TOK + N_SAMPLE_TOK

V7X_VMEM_BYTES = 64 * 1024 * 1024
VMEM_LIMIT = 48 * 1024 * 1024
LANE = 128


def _mm_kernel(x_ref, w_ref, o_ref):
    o_ref[...] = jnp.dot(x_ref[...], w_ref[...].astype(BF16), preferred_element_type=F32).astype(o_ref.dtype)


def _pick_tile(n, prefs, also=0):
    for t in prefs:
        if n % t == 0 and also % t == 0:
            return t
    raise ValueError(f"no tile for {n} (offset {also})")


def matmul(x, w, col0=0, n=None, out_dtype=F32):
    m, k = x.shape
    n = w.shape[1] - col0 if n is None else n
    tm = _pick_tile(m, (640, 512, 320, 256, 128, 64, 32, 16, 8))
    tn = _pick_tile(n, (512, 384, 256, 128), col0)
    c0 = col0 // tn
    return pl.pallas_call(
        _mm_kernel,
        grid=(m // tm, n // tn),
        in_specs=[pl.BlockSpec((tm, k), lambda i, j: (i, 0)),
                  pl.BlockSpec((k, tn), lambda i, j: (0, c0 + j))],
        out_specs=pl.BlockSpec((tm, tn), lambda i, j: (i, j)),
        out_shape=jax.ShapeDtypeStruct((m, n), out_dtype),
        compiler_params=pltpu.CompilerParams(
            dimension_semantics=("parallel", "parallel"), vmem_limit_bytes=VMEM_LIMIT),
        name="proj_matmul",
    )(x, w)


def _mm_wt_kernel(w_ref, xt_ref, o_ref, wt_sc):
    @pl.when(pl.program_id(1) == 0)
    def _():
        wt_sc[...] = w_ref[...].astype(F32).T.astype(BF16)

    o_ref[...] = jnp.dot(wt_sc[...], xt_ref[...], preferred_element_type=F32).astype(o_ref.dtype)


def matmul_wt(w, xt, col0, n, out_dtype=F32):
    k, m = xt.shape
    tr = _pick_tile(n, (512, 256, 128), col0)
    tt = _pick_tile(m, (512, 256, 128))
    r0 = col0 // tr
    return pl.pallas_call(
        _mm_wt_kernel,
        grid=(n // tr, m // tt),
        in_specs=[pl.BlockSpec((k, tr), lambda i, j: (0, r0 + i)),
                  pl.BlockSpec((k, tt), lambda i, j: (0, j))],
        out_specs=pl.BlockSpec((tr, tt), lambda i, j: (i, j)),
        out_shape=jax.ShapeDtypeStruct((n, m), out_dtype),
        scratch_shapes=[pltpu.VMEM((tr, k), BF16)],
        compiler_params=pltpu.CompilerParams(
            dimension_semantics=("parallel", "arbitrary"), vmem_limit_bytes=VMEM_LIMIT),
        name="proj_matmul_wt",
    )(w, xt)


def _mm_ta_kernel(xt_ref, w_ref, o_ref):
    o_ref[...] = lax.dot_general(xt_ref[...], w_ref[...].astype(BF16), (((0,), (0,)), ((), ())),
                                 preferred_element_type=F32).astype(o_ref.dtype)


def matmul_ta(xt, w, out_dtype=F32):
    k, m = xt.shape
    n = w.shape[1]
    tm = _pick_tile(m, (512, 256, 128))
    tn = _pick_tile(n, (512, 384, 256, 128))
    return pl.pallas_call(
        _mm_ta_kernel,
        grid=(m // tm, n // tn),
        in_specs=[pl.BlockSpec((k, tm), lambda i, j: (0, i)),
                  pl.BlockSpec((k, tn), lambda i, j: (0, j))],
        out_specs=pl.BlockSpec((tm, tn), lambda i, j: (i, j)),
        out_shape=jax.ShapeDtypeStruct((m, n), out_dtype),
        compiler_params=pltpu.CompilerParams(
            dimension_semantics=("parallel", "parallel"), vmem_limit_bytes=VMEM_LIMIT),
        name="proj_matmul_ta",
    )(xt, w)


POST_TM = 320
POST_TN = 512


def _post_kernel(x_ref, h_ref, p_ref, g_ref, b_ref, wg_ref, wp_ref, o_ref, obf_ref, y_sc, ybf_sc):
    j = pl.program_id(1)

    @pl.when(j == 0)
    def _():
        t = ALPHA * x_ref[...] + h_ref[...]
        mu = jnp.mean(t, axis=-1, keepdims=True)
        d = t - mu
        var = jnp.mean(d * d, axis=-1, keepdims=True)
        y = d * lax.rsqrt(var + LN_EPS) * g_ref[...] + b_ref[...]
        ybf_sc[...] = y.astype(BF16)
        for jj in range(D_MODEL // POST_TN):
            y_sc[jj] = y[:, jj * POST_TN:(jj + 1) * POST_TN]

    gate = jnp.dot(ybf_sc[...], wg_ref[...].astype(BF16), preferred_element_type=F32)
    ple = jnp.dot(p_ref[...].astype(BF16), wp_ref[...].astype(BF16), preferred_element_type=F32)
    o = y_sc[j] + (1.0 / (1.0 + jnp.exp(-gate))) * ple
    o_ref[...] = o
    obf_ref[...] = o.astype(BF16)


def post_norm_ple(x, h, p_bf, g, b, wg_bf, wp_bf):
    m = x.shape[0]
    tm, tn = POST_TM, POST_TN
    return pl.pallas_call(
        _post_kernel,
        grid=(m // tm, D_MODEL // tn),
        in_specs=[pl.BlockSpec((tm, D_MODEL), lambda i, j: (i, 0)),
                  pl.BlockSpec((tm, D_MODEL), lambda i, j: (i, 0)),
                  pl.BlockSpec((tm, PLE_DIM), lambda i, j: (i, 0)),
                  pl.BlockSpec((1, D_MODEL), lambda i, j: (0, 0)),
                  pl.BlockSpec((1, D_MODEL), lambda i, j: (0, 0)),
                  pl.BlockSpec((D_MODEL, tn), lambda i, j: (0, j)),
                  pl.BlockSpec((PLE_DIM, tn), lambda i, j: (0, j))],
        out_specs=[pl.BlockSpec((tm, tn), lambda i, j: (i, j)),
                   pl.BlockSpec((tm, tn), lambda i, j: (i, j))],
        out_shape=[jax.ShapeDtypeStruct((m, D_MODEL), F32),
                   jax.ShapeDtypeStruct((m, D_MODEL), BF16)],
        scratch_shapes=[pltpu.VMEM((D_MODEL // tn, tm, tn), F32),
                        pltpu.VMEM((tm, D_MODEL), BF16)],
        compiler_params=pltpu.CompilerParams(
            dimension_semantics=("parallel", "arbitrary"), vmem_limit_bytes=VMEM_LIMIT),
        name="post_norm_ple",
    )(x, h, p_bf, g.reshape(1, D_MODEL), b.reshape(1, D_MODEL), wg_bf, wp_bf)


def rel_bucket(dist):
    n = jnp.maximum(dist, 0)
    exact = N_BUCKETS // 2
    logb = exact + (jnp.log(jnp.maximum(n, exact).astype(F32) / exact)
                    / math.log(REL_MAX_DIST / exact) * (N_BUCKETS - exact)).astype(jnp.int32)
    return jnp.where(n < exact, n, jnp.minimum(logb, N_BUCKETS - 1))


def split_tokens(t):
    c = t.shape[-1]
    return (t[:N_PROMPT_TOK].reshape(BATCH, SEQ, c), t[N_PROMPT_TOK:].reshape(DEC_BATCH, DEC_SEQ, c))


def join_tokens(tp, ts):
    c = tp.shape[-1]
    return jnp.concatenate([tp.reshape(N_PROMPT_TOK, c), ts.reshape(N_SAMPLE_TOK, c)], axis=0)


G_A = N_HEADS // KV_A
SWA_KEYS = 2 * WINDOW


def swa_tables(rel_bias, sinks, dec_len):
    def heads_to(b, lead):
        return jnp.moveaxis(b, -1, 0).reshape((KV_A, G_A) + lead)

    dist = jnp.arange(WINDOW)[None, :] - (jnp.arange(SWA_KEYS)[:, None] - WINDOW)
    ok = (dist >= 0) & (dist < WINDOW)
    b = jnp.where(ok[..., None], rel_bias[rel_bucket(dist)].astype(F32), NEG_BIG)
    bias_p = heads_to(b, (SWA_KEYS, WINDOW)).transpose(0, 2, 1, 3).reshape(KV_A, SWA_KEYS, G_A * WINDOW)
    sink_p = jnp.broadcast_to(sinks.astype(F32).reshape(KV_A, 1, G_A, 1), (KV_A, 1, G_A, WINDOW))
    sink_p = sink_p.reshape(KV_A, 1, G_A * WINDOW)
    key_i = jnp.arange(SWA_KEYS)[None, :]
    dist = jnp.arange(dec_len)[:, None] + WINDOW - key_i
    ok = (dist >= 0) & (dist < WINDOW) & (key_i < WINDOW + dec_len)
    b = jnp.where(ok[..., None], rel_bias[rel_bucket(dist)].astype(F32), NEG_BIG)
    bias_s = heads_to(b, (dec_len, SWA_KEYS)).reshape(KV_A, G_A * dec_len, SWA_KEYS)
    sink_s = jnp.broadcast_to(sinks.astype(F32).reshape(KV_A, G_A, 1, 1), (KV_A, G_A, dec_len, LANE))
    sink_s = sink_s.reshape(KV_A, G_A * dec_len, LANE)
    return bias_p, sink_p, bias_s, sink_s


def _swa_prompt_kernel(qT_ref, zT_ref, vTp_ref, vTc_ref, kp_ref, kc_ref, bias_ref, sink_ref, o_ref, *, cdt):
    first = pl.program_id(1) == 0
    kk = jnp.concatenate([kp_ref[...], kc_ref[...]], axis=0)
    vT = jnp.concatenate([vTp_ref[...], vTc_ref[...]], axis=1)
    prev_key = lax.broadcasted_iota(jnp.int32, (SWA_KEYS, G_A * WINDOW), 0) < WINDOW
    pw = 2 * HEAD_DIM
    for j in range(KV_A):
        kpair = kk[:, (j // 2) * pw:(j // 2 + 1) * pw].astype(cdt)
        qj = jnp.concatenate([qT_ref[(G_A * j + g) * HEAD_DIM:(G_A * j + g + 1) * HEAD_DIM, :]
                              for g in range(G_A)], axis=1)
        qj = (qj.astype(F32) * (HEAD_DIM ** -0.5)).astype(cdt)
        zpad = jnp.zeros_like(qj)
        rhs = jnp.concatenate([qj, zpad] if j % 2 == 0 else [zpad, qj], axis=0)
        s = jnp.dot(kpair, rhs, preferred_element_type=F32) + bias_ref[j]
        s = jnp.where(prev_key, jnp.where(first, NEG_BIG, s), s)
        sink = sink_ref[j]
        m = jnp.maximum(jnp.max(s, axis=0, keepdims=True), sink)
        e = jnp.exp(s - m)
        den = jnp.sum(e, axis=0, keepdims=True) + jnp.exp(sink - m)
        p = (e * (1.0 / den)).astype(cdt)
        acc = jnp.dot(vT[j * HEAD_DIM:(j + 1) * HEAD_DIM, :].astype(cdt), p, preferred_element_type=F32)
        for g in range(G_A):
            r0 = (G_A * j + g) * HEAD_DIM
            z = zT_ref[r0:r0 + HEAD_DIM, :]
            o_ref[r0:r0 + HEAD_DIM, :] = (acc[:, g * WINDOW:(g + 1) * WINDOW]
                                          * (z * (1.0 / (1.0 + jnp.exp(-z))))).astype(o_ref.dtype)


def swa_prompt(qT, vT, zT, kv_nat, bias_p, sink_p, *, n_batch, seq_len, cdt=BF16):
    nb = seq_len // WINDOW
    cur = lambda b, i: b * nb + i
    prev = lambda b, i: b * nb + jnp.maximum(i - 1, 0)
    v_row_blk = 0
    return pl.pallas_call(
        functools.partial(_swa_prompt_kernel, cdt=cdt),
        grid=(n_batch, nb),
        in_specs=[pl.BlockSpec((ATT_WIDTH, WINDOW), lambda b, i: (0, cur(b, i))),
                  pl.BlockSpec((ATT_WIDTH, WINDOW), lambda b, i: (0, cur(b, i))),
                  pl.BlockSpec((A_KV, WINDOW), lambda b, i: (v_row_blk, prev(b, i))),
                  pl.BlockSpec((A_KV, WINDOW), lambda b, i: (v_row_blk, cur(b, i))),
                  pl.BlockSpec((WINDOW, A_KV), lambda b, i: (prev(b, i), 0)),
                  pl.BlockSpec((WINDOW, A_KV), lambda b, i: (cur(b, i), 0)),
                  pl.BlockSpec(bias_p.shape, lambda b, i: (0, 0, 0)),
                  pl.BlockSpec(sink_p.shape, lambda b, i: (0, 0, 0))],
        out_specs=pl.BlockSpec((ATT_WIDTH, WINDOW), lambda b, i: (0, cur(b, i))),
        out_shape=jax.ShapeDtypeStruct((ATT_WIDTH, n_batch * seq_len), BF16),
        compiler_params=pltpu.CompilerParams(
            dimension_semantics=("parallel", "parallel"), vmem_limit_bytes=VMEM_LIMIT),
        name="swa_prompt",
    )(qT, zT, vT, vT, kv_nat, kv_nat, bias_p, sink_p)


def _swa_sample_kernel(q_ref, z_ref, k_ref, v_ref, bias_ref, sink_ref, o_ref, *, cdt):
    for j in range(KV_A):
        s = lax.dot_general(q_ref[0, j], k_ref[0, j].astype(cdt), (((1,), (1,)), ((), ())),
                            preferred_element_type=F32) + bias_ref[j]
        sink = sink_ref[j][:, 0:1]
        m = jnp.maximum(jnp.max(s, axis=1, keepdims=True), sink)
        e = jnp.exp(s - m)
        den = jnp.sum(e, axis=1, keepdims=True) + jnp.exp(sink - m)
        p = (e * (1.0 / den)).astype(cdt)
        z = z_ref[0, j]
        o_ref[0, j] = jnp.dot(p, v_ref[0, j].astype(cdt), preferred_element_type=F32) * (z * (1.0 / (1.0 + jnp.exp(-z))))


def swa_sample(q_s, z_s, kv_s, kv_cache, bias_s, sink_s, cdt=BF16):
    n_dec = kv_cache.shape[0]
    dec_len = q_s.shape[0] // n_dec
    rows = G_A * dec_len

    def head_rows(t, scale):
        t = t.reshape(n_dec, dec_len, KV_A, G_A, HEAD_DIM).transpose(0, 2, 3, 1, 4) * scale
        return jnp.pad(t.reshape(n_dec, KV_A, rows, HEAD_DIM), ((0, 0), (0, 0), (0, 0), (0, HEAD_DIM)))

    new = kv_s.reshape(n_dec, dec_len, 2, KV_A, HEAD_DIM)
    cat = jnp.concatenate([kv_cache, new], axis=1)
    keys = jnp.pad(cat.transpose(2, 0, 3, 1, 4),
                   ((0, 0), (0, 0), (0, 0), (0, SWA_KEYS - WINDOW - dec_len), (0, HEAD_DIM)))
    blk = lambda r: pl.BlockSpec((1, KV_A, r, 2 * HEAD_DIM), lambda b: (b, 0, 0, 0))
    o = pl.pallas_call(
        functools.partial(_swa_sample_kernel, cdt=cdt),
        grid=(n_dec,),
        in_specs=[blk(rows), blk(rows), blk(SWA_KEYS), blk(SWA_KEYS),
                  pl.BlockSpec(bias_s.shape, lambda b: (0, 0, 0)),
                  pl.BlockSpec(sink_s.shape, lambda b: (0, 0, 0))],
        out_specs=blk(rows),
        out_shape=jax.ShapeDtypeStruct((n_dec, KV_A, rows, 2 * HEAD_DIM), F32),
        compiler_params=pltpu.CompilerParams(dimension_semantics=("parallel",), vmem_limit_bytes=VMEM_LIMIT),
        name="swa_sample",
    )(head_rows(q_s, HEAD_DIM ** -0.5).astype(cdt), head_rows(z_s, 1.0), keys[0], keys[1], bias_s, sink_s)
    o = o[..., :HEAD_DIM].reshape(n_dec, KV_A, G_A, dec_len, HEAD_DIM).transpose(0, 3, 1, 2, 4)
    return o.reshape(n_dec * dec_len, ATT_WIDTH), cat[:, dec_len:]


S5_SLAB_G = 8
S5_SLAB_CH = S5_SLAB_G * S5_GROUP
S5_SLAB_ST = S5_SLAB_G * S5_STATE
S5_N_SLABS = S5_GROUPS // S5_SLAB_G
S5_CHAINS = 8
S5_HALF_CH = S5_CHAINS * S5_SLAB_CH
S5_T = 256
S5_LT = 2 * S5_SLAB_ST // LANE


def _gelu_tanh(x):
    return 0.5 * x * (1.0 + jnp.tanh(math.sqrt(2.0 / math.pi) * (x + 0.044715 * (x * x * x))))


def s5_tables(a_re, a_im, b_re, b_im, c_re, c_im, log_dt):
    a = lax.complex(a_re, a_im)
    dt = jnp.exp(log_dt)[:, None]
    a_bar = jnp.exp(a * dt)
    b_bar = ((a_bar - 1.0) / a)[..., None] * lax.complex(b_re, b_im)
    eye = jnp.eye(S5_SLAB_G, dtype=F32)

    def b_blk(t):
        t = t.reshape(S5_N_SLABS, S5_SLAB_G, S5_STATE, S5_GROUP)
        return jnp.einsum('ij,sipc->sicjp', eye, t).reshape(S5_N_SLABS, S5_SLAB_CH, S5_SLAB_ST)

    def c_blk(t):
        t = t.reshape(S5_N_SLABS, S5_SLAB_G, S5_GROUP, S5_STATE)
        return jnp.einsum('ij,sicp->sjpic', eye, t).reshape(S5_N_SLABS, S5_SLAB_ST, S5_SLAB_CH)

    bcat = jnp.concatenate([b_blk(b_bar.real), b_blk(b_bar.imag)], axis=2)
    ccat = jnp.concatenate([c_blk(c_re), -c_blk(c_im)], axis=1)
    a_cat = jnp.concatenate([a_bar.real.reshape(S5_N_SLABS, S5_SLAB_ST),
                             a_bar.imag.reshape(S5_N_SLABS, S5_SLAB_ST)], axis=1)
    return a_cat, bcat, ccat


def _s5_prompt_kernel(u_ref, bcat_ref, ccat_ref, a_ref, d_ref, y_ref, hout_ref, sc, h_sc, *, cdt):
    tc = pl.program_id(2)
    n_lt_half = S5_LT // 2

    @pl.when(tc == 0)
    def _():
        h_sc[...] = jnp.zeros_like(h_sc)

    for j in range(S5_CHAINS):
        uj = u_ref[:, j * S5_SLAB_CH:(j + 1) * S5_SLAB_CH].astype(cdt)
        bu = jnp.dot(uj, bcat_ref[0, j], preferred_element_type=F32)
        for lt in range(S5_LT):
            sc[lt, pl.ds(j, S5_T, stride=S5_CHAINS), :] = bu[:, lt * LANE:(lt + 1) * LANE]

    a_re = [a_ref[0, :, lt * LANE:(lt + 1) * LANE] for lt in range(n_lt_half)]
    a_im = [a_ref[0, :, (n_lt_half + lt) * LANE:(n_lt_half + lt + 1) * LANE] for lt in range(n_lt_half)]

    def step(t, h):
        r0 = pl.multiple_of(t * S5_CHAINS, S5_CHAINS)
        new = list(h)
        for lt in range(n_lt_half):
            hr, hi = h[lt], h[n_lt_half + lt]
            nr = a_re[lt] * hr - a_im[lt] * hi + sc[lt, pl.ds(r0, S5_CHAINS), :]
            ni = a_re[lt] * hi + a_im[lt] * hr + sc[n_lt_half + lt, pl.ds(r0, S5_CHAINS), :]
            sc[lt, pl.ds(r0, S5_CHAINS), :] = nr
            sc[n_lt_half + lt, pl.ds(r0, S5_CHAINS), :] = ni
            new[lt], new[n_lt_half + lt] = nr, ni
        return tuple(new)

    h = lax.fori_loop(0, S5_T, step, tuple(h_sc[lt] for lt in range(S5_LT)), unroll=8)
    for lt in range(S5_LT):
        h_sc[lt] = h[lt]
        hout_ref[0, 0, :, lt * LANE:(lt + 1) * LANE] = h[lt]

    for j in range(S5_CHAINS):
        hcat = jnp.concatenate([sc[lt, pl.ds(j, S5_T, stride=S5_CHAINS), :] for lt in range(S5_LT)], axis=1)
        cols = slice(j * S5_SLAB_CH, (j + 1) * S5_SLAB_CH)
        y = jnp.dot(hcat.astype(cdt), ccat_ref[0, j], preferred_element_type=F32) + d_ref[0, :, cols] * u_ref[:, cols]
        y_ref[:, cols] = _gelu_tanh(y)


def s5_prompt(proj, a_cat, bcat, ccat, d_skip, *, n_batch, seq_len, n_rows_out, cdt=BF16):
    n_t = seq_len // S5_T
    n_half = S5_WIDTH // S5_HALF_CH
    half = lambda t: t.reshape((n_half, S5_CHAINS) + t.shape[1:])
    return pl.pallas_call(
        functools.partial(_s5_prompt_kernel, cdt=cdt),
        grid=(n_batch, n_half, n_t),
        in_specs=[pl.BlockSpec((S5_T, S5_HALF_CH), lambda b, hf, t: (b * n_t + t, hf)),
                  pl.BlockSpec((1, S5_CHAINS, S5_SLAB_CH, 2 * S5_SLAB_ST), lambda b, hf, t: (hf, 0, 0, 0)),
                  pl.BlockSpec((1, S5_CHAINS, 2 * S5_SLAB_ST, S5_SLAB_CH), lambda b, hf, t: (hf, 0, 0, 0)),
                  pl.BlockSpec((1, S5_CHAINS, 2 * S5_SLAB_ST), lambda b, hf, t: (hf, 0, 0)),
                  pl.BlockSpec((1, 1, S5_HALF_CH), lambda b, hf, t: (hf, 0, 0))],
        out_specs=[pl.BlockSpec((S5_T, S5_HALF_CH), lambda b, hf, t: (b * n_t + t, hf)),
                   pl.BlockSpec((1, 1, S5_CHAINS, 2 * S5_SLAB_ST), lambda b, hf, t: (b, hf, 0, 0))],
        out_shape=[jax.ShapeDtypeStruct((n_rows_out, S5_WIDTH), F32),
                   jax.ShapeDtypeStruct((n_batch, n_half, S5_CHAINS, 2 * S5_SLAB_ST), F32)],
        scratch_shapes=[pltpu.VMEM((S5_LT, S5_T * S5_CHAINS, LANE), F32),
                        pltpu.VMEM((S5_LT, S5_CHAINS, LANE), F32)],
        compiler_params=pltpu.CompilerParams(
            dimension_semantics=("parallel", "parallel", "arbitrary"), vmem_limit_bytes=VMEM_LIMIT),
        name="s5_prompt",
    )(proj, half(bcat.astype(cdt)), half(ccat.astype(cdt)), half(a_cat), d_skip.reshape(n_half, 1, S5_HALF_CH))


def _s5_sample_kernel(u_ref, bcat_ref, ccat_ref, a_ref, d_ref, h0_ref, y_ref, hout_ref, sc, *, n_b, n_t, cdt):
    u = u_ref[...]
    bu = jnp.dot(u.astype(cdt), bcat_ref[0], preferred_element_type=F32)
    a_re = a_ref[0, :, :S5_SLAB_ST]
    a_im = a_ref[0, :, S5_SLAB_ST:]
    for bg in range(n_b // 8):
        hr = h0_ref[0, bg * 8:(bg + 1) * 8, :S5_SLAB_ST]
        hi = h0_ref[0, bg * 8:(bg + 1) * 8, S5_SLAB_ST:]
        for t in range(n_t):
            r = t * n_b + bg * 8
            hr, hi = (a_re * hr - a_im * hi + bu[r:r + 8, :S5_SLAB_ST],
                      a_re * hi + a_im * hr + bu[r:r + 8, S5_SLAB_ST:])
            sc[r:r + 8, :S5_SLAB_ST] = hr
            sc[r:r + 8, S5_SLAB_ST:] = hi
        hout_ref[0, bg * 8:(bg + 1) * 8, :S5_SLAB_ST] = hr
        hout_ref[0, bg * 8:(bg + 1) * 8, S5_SLAB_ST:] = hi
    y = jnp.dot(sc[...].astype(cdt), ccat_ref[0], preferred_element_type=F32) + d_ref[0] * u
    y_ref[...] = _gelu_tanh(y)


def s5_sample(u_tb, a_cat, bcat, ccat, d_skip, h0_cat, *, n_b, n_t, cdt=BF16):
    rows = n_t * n_b
    return pl.pallas_call(
        functools.partial(_s5_sample_kernel, n_b=n_b, n_t=n_t, cdt=cdt),
        grid=(S5_N_SLABS,),
        in_specs=[pl.BlockSpec((rows, S5_SLAB_CH), lambda s: (0, s)),
                  pl.BlockSpec((1, S5_SLAB_CH, 2 * S5_SLAB_ST), lambda s: (s, 0, 0)),
                  pl.BlockSpec((1, 2 * S5_SLAB_ST, S5_SLAB_CH), lambda s: (s, 0, 0)),
                  pl.BlockSpec((1, 1, 2 * S5_SLAB_ST), lambda s: (s, 0, 0)),
                  pl.BlockSpec((1, 1, S5_SLAB_CH), lambda s: (s, 0, 0)),
                  pl.BlockSpec((1, n_b, 2 * S5_SLAB_ST), lambda s: (s, 0, 0))],
        out_specs=[pl.BlockSpec((rows, S5_SLAB_CH), lambda s: (0, s)),
                   pl.BlockSpec((1, n_b, 2 * S5_SLAB_ST), lambda s: (s, 0, 0))],
        out_shape=[jax.ShapeDtypeStruct((rows, S5_WIDTH), F32),
                   jax.ShapeDtypeStruct((S5_N_SLABS, n_b, 2 * S5_SLAB_ST), F32)],
        scratch_shapes=[pltpu.VMEM((rows, 2 * S5_SLAB_ST), F32)],
        compiler_params=pltpu.CompilerParams(
            dimension_semantics=("arbitrary",), vmem_limit_bytes=VMEM_LIMIT),
        name="s5_sample",
    )(u_tb, bcat.astype(cdt), ccat.astype(cdt), a_cat.reshape(S5_N_SLABS, 1, 2 * S5_SLAB_ST),
      d_skip.reshape(S5_N_SLABS, 1, S5_SLAB_CH), h0_cat)


GLU_TM = 320
GLU_TN = 512


def _glu_kernel(yfull_ref, w_ref, ycol_ref, z_ref, o_ref, ybf_sc):
    @pl.when(pl.program_id(1) == 0)
    def _():
        ybf_sc[...] = yfull_ref[...].astype(ybf_sc.dtype)

    glu = jnp.dot(ybf_sc[...], w_ref[...].astype(ybf_sc.dtype), preferred_element_type=F32)
    z = z_ref[...]
    y = ycol_ref[...]
    o_ref[...] = (y * (1.0 / (1.0 + jnp.exp(-glu))) * (z * (1.0 / (1.0 + jnp.exp(-z))))).astype(o_ref.dtype)


def s5_glu_gate(y, w_glu, proj, row_off, cdt=BF16):
    m = y.shape[0]
    tm, tn = _pick_tile(m, (256, 128, 64, 32, 16)), GLU_TN
    assert row_off % tm == 0
    z_off, r_off = S5_WIDTH // tn, row_off // tm
    return pl.pallas_call(
        _glu_kernel,
        grid=(m // tm, S5_WIDTH // tn),
        in_specs=[pl.BlockSpec((tm, S5_WIDTH), lambda i, j: (i, 0)),
                  pl.BlockSpec((S5_WIDTH, tn), lambda i, j: (0, j)),
                  pl.BlockSpec((tm, tn), lambda i, j: (i, j)),
                  pl.BlockSpec((tm, tn), lambda i, j: (r_off + i, z_off + j))],
        out_specs=pl.BlockSpec((tm, tn), lambda i, j: (i, j)),
        out_shape=jax.ShapeDtypeStruct((m, S5_WIDTH), BF16),
        scratch_shapes=[pltpu.VMEM((tm, S5_WIDTH), cdt)],
        compiler_params=pltpu.CompilerParams(
            dimension_semantics=("parallel", "arbitrary"), vmem_limit_bytes=VMEM_LIMIT),
        name="s5_glu_gate",
    )(y, w_glu, y, proj)


def s5_layer(proj, state_in, tables, d_skip, w_glu, *, n_batch, seq_len, n_dec, dec_len, cdt=BF16):
    a_cat, bcat, ccat = tables
    n_p = n_batch * seq_len
    n_s = n_dec * dec_len
    y_p, h_p = s5_prompt(proj, a_cat, bcat, ccat, d_skip, n_batch=n_batch, seq_len=seq_len, n_rows_out=n_p, cdt=cdt)
    u_tb = jnp.swapaxes(proj[n_p:, :S5_WIDTH].reshape(n_dec, dec_len, S5_WIDTH), 0, 1).reshape(n_s, S5_WIDTH)
    h0 = state_in.reshape(n_dec, S5_N_SLABS, S5_SLAB_ST, 2)
    h0_cat = jnp.concatenate([jnp.swapaxes(h0[..., 0], 0, 1), jnp.swapaxes(h0[..., 1], 0, 1)], axis=-1)
    y_tb, h_s = s5_sample(u_tb, a_cat, bcat, ccat, d_skip, h0_cat, n_b=n_dec, n_t=dec_len, cdt=cdt)
    y_s = jnp.swapaxes(y_tb.reshape(dec_len, n_dec, S5_WIDTH), 0, 1).reshape(n_s, S5_WIDTH)
    gated_p = s5_glu_gate(y_p, w_glu, proj, 0, cdt=cdt)
    gated_s = s5_glu_gate(y_s, w_glu, proj, n_p, cdt=cdt)
    hp = h_p.reshape(n_batch, S5_N_SLABS, 2, S5_SLAB_ST)
    st_p = jnp.stack([hp[:, :, 0], hp[:, :, 1]], axis=-1).reshape(n_batch, S5_GROUPS, S5_STATE, 2)
    hs = jnp.swapaxes(h_s, 0, 1).reshape(n_dec, S5_N_SLABS, 2, S5_SLAB_ST)
    st_s = jnp.stack([hs[:, :, 0], hs[:, :, 1]], axis=-1).reshape(n_dec, S5_GROUPS, S5_STATE, 2)
    return gated_p, gated_s, st_p, st_s


GDN_CONV_TT = 256
GDN_CONV_CW = 1024
GDN_HIST = 8
GDN_HB = 16
GDN_TT = 256
GDN_SAMPLE_ROWS = 8


def _gdn_conv_kernel(x_ref, hist_ref, w_ref, o_ref, ext_sc, *, rows, n_t, zero_first, cw):
    i, j = pl.program_id(0), pl.program_id(1)
    hist = hist_ref[...]
    if zero_first:
        hist = jnp.where(i % n_t == 0, 0.0, hist)
    ext_sc[0:GDN_HIST] = hist
    ext_sc[GDN_HIST:GDN_HIST + rows] = x_ref[...]
    acc = x_ref[...] * w_ref[GDN_CONV - 1:GDN_CONV, :]
    for s in range(1, GDN_CONV):
        acc = acc + ext_sc[GDN_HIST - s:GDN_HIST - s + rows] * w_ref[GDN_CONV - 1 - s:GDN_CONV - s, :]
    conv = acc * (1.0 / (1.0 + jnp.exp(-acc)))

    def normalised(h, scale):
        t = conv[:, h * GDN_DK:(h + 1) * GDN_DK]
        return t * lax.rsqrt(jnp.sum(t * t, axis=-1, keepdims=True) + 1e-6) * scale

    if cw == GDN_CONV_CH:
        for h in range(2 * GDN_QK_HEADS):
            o_ref[:, h * GDN_DK:(h + 1) * GDN_DK] = normalised(h, GDN_DK ** -0.5 if h < GDN_QK_HEADS else 1.0)
        o_ref[:, 2 * GDN_QK_WIDTH:] = conv[:, 2 * GDN_QK_WIDTH:]
    else:
        n_qk_blocks = 2 * GDN_QK_WIDTH // cw

        @pl.when(j >= n_qk_blocks)
        def _():
            o_ref[...] = conv

        @pl.when(j < n_qk_blocks)
        def _():
            scale = jnp.where(j < GDN_QK_WIDTH // cw, GDN_DK ** -0.5, 1.0)
            for h in range(cw // GDN_DK):
                o_ref[:, h * GDN_DK:(h + 1) * GDN_DK] = normalised(h, scale)


def gdn_conv(x, hist_src, conv_w, *, rows, n_blocks, n_t, data_map, hist_map, zero_first, cw=GDN_CONV_CW):
    n_out = n_blocks * rows
    return pl.pallas_call(
        functools.partial(_gdn_conv_kernel, rows=rows, n_t=n_t, zero_first=zero_first, cw=cw),
        grid=(n_blocks, GDN_CONV_CH // cw),
        in_specs=[pl.BlockSpec((rows, cw), lambda i, j: (data_map(i), j)),
                  pl.BlockSpec((GDN_HIST, cw), lambda i, j: (hist_map(i), j)),
                  pl.BlockSpec((GDN_CONV, cw), lambda i, j: (0, j))],
        out_specs=pl.BlockSpec((rows, cw), lambda i, j: (i, j)),
        out_shape=jax.ShapeDtypeStruct((n_out, GDN_CONV_CH), F32),
        scratch_shapes=[pltpu.VMEM((GDN_HIST + rows, cw), F32)],
        compiler_params=pltpu.CompilerParams(
            dimension_semantics=("parallel", "parallel"), vmem_limit_bytes=VMEM_LIMIT),
        name="gdn_conv",
    )(x, hist_src, conv_w)


def _gdn_chunk_lockstep_kernel(q_ref, k_ref, v_ref, z_ref, ab_ref, alog_ref, dtb_ref, nw_ref, s0_ref, o_ref,
                               sout_ref, s_sc, *, chunk, n_inner, n_tt, valid_len):
    C = chunk
    hb, tt = pl.program_id(1), pl.program_id(2)

    @pl.when(tt == 0)
    def _():
        s_sc[...] = s0_ref[0]

    rowi = lax.broadcasted_iota(jnp.int32, (C, C), 0)
    coli = lax.broadcasted_iota(jnp.int32, (C, C), 1)
    causal = rowi >= coli
    strict = rowi > coli
    ltri = jnp.where(causal, 1.0, 0.0)
    utri = jnp.where(rowi <= coli, 1.0, 0.0)
    eye = jnp.where(rowi == coli, 1.0, 0.0)
    hi = lax.Precision.HIGHEST
    shift = (LANE - hb * GDN_HB) % LANE
    alog = pltpu.roll(jnp.broadcast_to(alog_ref[...], (8, LANE)), shift, 1)[0:1]
    dtb = pltpu.roll(jnp.broadcast_to(dtb_ref[...], (8, LANE)), shift, 1)[0:1]
    nw = nw_ref[...]
    tok_valid = lax.broadcasted_iota(jnp.int32, (C, LANE), 0) < valid_len
    dot = functools.partial(jnp.dot, preferred_element_type=F32)
    dot_nt = lambda a, b: lax.dot_general(a, b, (((1,), (1,)), ((), ())), preferred_element_type=F32)
    dot_tn = lambda a, b: lax.dot_general(a, b, (((0,), (0,)), ((), ())), preferred_element_type=F32)
    units = [(c, i) for c in range(n_inner) for i in range(GDN_HB)]
    rows = lambda c: slice(c * C, (c + 1) * C)
    qk_cols = lambda i: slice((i // 2) * GDN_DK, (i // 2 + 1) * GDN_DK)
    v_cols = lambda i: slice(i * GDN_DV, (i + 1) * GDN_DV)

    g_all, beta_all = [], []
    for c in range(n_inner):
        ab = pltpu.roll(ab_ref[rows(c), :], shift, 1)
        xa = ab + dtb
        softplus = jnp.maximum(xa, 0.0) + jnp.log1p(jnp.exp(-jnp.abs(xa)))
        g_all.append(jnp.where(tok_valid, -jnp.exp(alog) * softplus, 0.0))
        beta_all.append(jnp.where(tok_valid, 1.0 / (1.0 + jnp.exp(-ab)), 0.0))
    gam_all = [jnp.dot(ltri, g, preferred_element_type=F32, precision=hi) for g in g_all]
    gamT_all = [lax.dot_general(g, utri, (((0,), (0,)), ((), ())), preferred_element_type=F32, precision=hi)
                for g in g_all]

    qkk = [dot_nt(jnp.concatenate([q_ref[rows(c), qk_cols(i)], k_ref[rows(c), qk_cols(i)]], axis=0).astype(BF16),
                  k_ref[rows(c), qk_cols(i)].astype(BF16)) for c, i in units]
    gam_c = [jnp.broadcast_to(gam_all[c][:, i:i + 1], (C, LANE)) for c, i in units]
    beta_c = [jnp.broadcast_to(beta_all[c][:, 32 + i:33 + i], (C, LANE)) for c, i in units]
    gam_last = [jnp.broadcast_to(gam_all[c][C - 1:C, i:i + 1], (1, LANE)) for c, i in units]
    decay = [jnp.where(causal, jnp.exp(jnp.where(causal, gc[:, :C] - jnp.broadcast_to(gamT_all[c][i:i + 1, :], (C, C)),
                                                 0.0)), 0.0) for gc, (c, i) in zip(gam_c, units)]
    qk = [(x[:C] * d).astype(BF16) for x, d in zip(qkk, decay)]
    neg_a = [jnp.where(strict, -(b[:, :C] * x[C:] * d), 0.0) for b, x, d in zip(beta_c, qkk, decay)]
    p_inv = [eye + n for n in neg_a]
    m_pow = neg_a
    for _ in range(int(math.log2(C)) - 1):
        m_pow = [dot(m.astype(BF16), m.astype(BF16)) for m in m_pow]
        p_inv = [p + dot(p.astype(BF16), m.astype(BF16)) for p, m in zip(p_inv, m_pow)]
    eg = [jnp.exp(gc) for gc in gam_c]
    sol = [dot(p.astype(BF16), jnp.concatenate([b * v_ref[rows(c), v_cols(i)], (b * e) * k_ref[rows(c), qk_cols(i)]],
                                               axis=1).astype(BF16))
           for p, b, e, (c, i) in zip(p_inv, beta_c, eg, units)]
    wq = [jnp.concatenate([s[:, GDN_DV:], q_ref[rows(c), qk_cols(i)] * e], axis=0).astype(BF16)
          for s, e, (c, i) in zip(sol, eg, units)]
    k_dec = [(k_ref[rows(c), qk_cols(i)] * jnp.exp(gl - gc)).astype(BF16)
             for gl, gc, (c, i) in zip(gam_last, gam_c, units)]

    state = [s_sc[i] for i in range(GDN_HB)]
    for c in range(n_inner):
        base = c * GDN_HB
        ws = [dot(wq[base + i], state[i].astype(BF16)) for i in range(GDN_HB)]
        v_new = [(sol[base + i][:, :GDN_DV] - ws[i][:C]).astype(BF16) for i in range(GDN_HB)]
        o = [ws[i][C:] + dot(qk[base + i], v_new[i]) for i in range(GDN_HB)]
        state = [state[i] * jnp.exp(gam_last[base + i]) + dot_tn(k_dec[base + i], v_new[i]) for i in range(GDN_HB)]
        for i in range(GDN_HB):
            rms = lax.rsqrt(jnp.mean(o[i] * o[i], axis=-1, keepdims=True) + 1e-6)
            zz = z_ref[rows(c), v_cols(i)]
            o_ref[rows(c), v_cols(i)] = (o[i] * rms * nw * (zz * (1.0 / (1.0 + jnp.exp(-zz))))).astype(o_ref.dtype)
    for i in range(GDN_HB):
        s_sc[i] = state[i]

    @pl.when(tt == n_tt - 1)
    def _():
        sout_ref[0] = s_sc[...]


def gdn_chunk(conv, z, ab, a_log, dt_bias, norm_w, s0, *, n_seq, rows_per_seq, rows_per_step, chunk, valid_len,
              z_col_off, out_dtype):
    n_tt = rows_per_seq // rows_per_step
    n_inner = rows_per_step // chunk
    n_hb = GDN_V_HEADS // GDN_HB
    qw, vw = GDN_HB // 2 * GDN_DK, GDN_HB * GDN_DV
    k_off, v_off, z_off = GDN_QK_WIDTH // qw, 2 * GDN_QK_WIDTH // vw, z_col_off // vw
    row = lambda b, hb, t: b * n_tt + t
    pad_row = lambda p: jnp.pad(p.astype(F32), (0, LANE - p.shape[0])).reshape(1, LANE)
    return pl.pallas_call(
        functools.partial(_gdn_chunk_lockstep_kernel, chunk=chunk, n_inner=n_inner, n_tt=n_tt, valid_len=valid_len),
        grid=(n_seq, n_hb, n_tt),
        in_specs=[pl.BlockSpec((rows_per_step, qw), lambda b, hb, t: (row(b, hb, t), hb)),
                  pl.BlockSpec((rows_per_step, qw), lambda b, hb, t: (row(b, hb, t), k_off + hb)),
                  pl.BlockSpec((rows_per_step, vw), lambda b, hb, t: (row(b, hb, t), v_off + hb)),
                  pl.BlockSpec((rows_per_step, vw), lambda b, hb, t: (row(b, hb, t), z_off + hb)),
                  pl.BlockSpec((rows_per_step, LANE), lambda b, hb, t: (row(b, hb, t), 0)),
                  pl.BlockSpec((1, LANE), lambda b, hb, t: (0, 0)),
                  pl.BlockSpec((1, LANE), lambda b, hb, t: (0, 0)),
                  pl.BlockSpec((1, GDN_DV), lambda b, hb, t: (0, 0)),
                  pl.BlockSpec((1, GDN_HB, GDN_DK, GDN_DV), lambda b, hb, t: (b, hb, 0, 0))],
        out_specs=[pl.BlockSpec((rows_per_step, vw), lambda b, hb, t: (row(b, hb, t), hb)),
                   pl.BlockSpec((1, GDN_HB, GDN_DK, GDN_DV), lambda b, hb, t: (b, hb, 0, 0))],
        out_shape=[jax.ShapeDtypeStruct((n_seq * rows_per_seq, GDN_V_WIDTH), out_dtype),
                   jax.ShapeDtypeStruct((n_seq, GDN_V_HEADS, GDN_DK, GDN_DV), F32)],
        scratch_shapes=[pltpu.VMEM((GDN_HB, GDN_DK, GDN_DV), F32)],
        compiler_params=pltpu.CompilerParams(
            dimension_semantics=("parallel", "parallel", "arbitrary"), vmem_limit_bytes=VMEM_LIMIT),
        name="gdn_chunk",
    )(conv, conv, conv, z, ab, pad_row(a_log), pad_row(dt_bias), norm_w.astype(F32).reshape(1, GDN_DV), s0)


def gdn_layer(qkvz, ab, state_in, conv_in, conv_w, a_log, dt_bias, norm_w, *, n_batch, seq_len, n_dec, dec_len):
    n_p = n_batch * seq_len
    n_tp = seq_len // GDN_CONV_TT
    hist_per_block = GDN_CONV_TT // GDN_HIST
    conv_p = gdn_conv(qkvz, qkvz, conv_w, rows=GDN_CONV_TT, n_blocks=n_batch * n_tp, n_t=n_tp,
                      data_map=lambda i: i, hist_map=lambda i: jnp.maximum(i * hist_per_block - 1, 0),
                      zero_first=True)
    zeros_s = jnp.zeros((n_batch, GDN_V_HEADS, GDN_DK, GDN_DV), F32)
    gated_p, st_p = gdn_chunk(conv_p, qkvz, ab, a_log, dt_bias, norm_w, zeros_s, n_seq=n_batch,
                              rows_per_seq=seq_len, rows_per_step=GDN_TT, chunk=GDN_CHUNK, valid_len=GDN_CHUNK,
                              z_col_off=GDN_CONV_CH, out_dtype=BF16)
    buf_p = jnp.stack([lax.slice(qkvz, ((b + 1) * seq_len - (GDN_CONV - 1), 0), ((b + 1) * seq_len, GDN_CONV_CH))
                       for b in range(n_batch)])
    R = GDN_SAMPLE_ROWS
    x_s = qkvz[n_p:].reshape(n_dec, dec_len, -1)
    pad_t = lambda t, front: jnp.pad(t, ((0, 0), (front, R - front - t.shape[1]), (0, 0)))
    ext = jnp.concatenate([pad_t(conv_in, R - (GDN_CONV - 1)), pad_t(x_s[..., :GDN_CONV_CH], 0)], axis=1)
    ext = ext.reshape(n_dec * 2 * R, GDN_CONV_CH)
    conv_s = gdn_conv(ext, ext, conv_w, rows=R, n_blocks=n_dec, n_t=1,
                      data_map=lambda i: 2 * i + 1, hist_map=lambda i: 2 * i, zero_first=False, cw=GDN_CONV_CH)
    z_s = pad_t(x_s[..., GDN_CONV_CH:], 0).reshape(n_dec * R, GDN_V_WIDTH)
    ab_s = pad_t(ab[n_p:].reshape(n_dec, dec_len, LANE), 0).reshape(n_dec * R, LANE)
    gated_s, st_s = gdn_chunk(conv_s, z_s, ab_s, a_log, dt_bias, norm_w, state_in, n_seq=n_dec, rows_per_seq=R,
                              rows_per_step=R, chunk=R, valid_len=dec_len, z_col_off=0, out_dtype=F32)
    gated_s = gated_s.reshape(n_dec, R, GDN_V_WIDTH)[:, :dec_len].reshape(n_dec * dec_len, GDN_V_WIDTH)
    buf_s = jnp.concatenate([conv_in, x_s[..., :GDN_CONV_CH]], axis=1)[:, dec_len:]
    return gated_p, gated_s, st_p, st_s, buf_p, buf_s


DSA_KC = 256
INT_MIN = -2 ** 31
NEG_BIG = -1e30
G_D = N_HEADS // KV_D
BIAS_WIN = DSA_KC + Q_BLOCK


def _sortable_key(s):
    b = pltpu.bitcast(s, jnp.int32)
    return jnp.where(b < 0, b ^ jnp.int32(0x7FFFFFFF), b)


def _dsa_prompt_kernel(qT_ref, qiT_ref, wiT_ref, zT_ref, ki_ref, k_ref, vT_ref, win_ref, o_ref,
                       key_sc, mask_sc, *, topk, idx_bits, cdt):
    qb = pl.program_id(1)
    t0 = qb * Q_BLOCK
    nch = (qb + 2) // 2
    t_idx = t0 + lax.broadcasted_iota(jnp.int32, (1, Q_BLOCK), 1)
    row_iota = lax.broadcasted_iota(jnp.int32, (DSA_KC, Q_BLOCK), 0)

    def score_chunk(c, carry):
        kic = ki_ref[0, c].astype(cdt)
        acc = jnp.zeros((DSA_KC, Q_BLOCK), F32)
        for hp in range(IDX_HEADS // 2):
            rhs = jnp.concatenate([qiT_ref[(2 * hp) * IDX_DIM:(2 * hp + 1) * IDX_DIM, :],
                                   qiT_ref[(2 * hp + 1) * IDX_DIM:(2 * hp + 2) * IDX_DIM, :]], axis=1)
            s = jnp.dot(kic, rhs, preferred_element_type=F32) * (IDX_DIM ** -0.5)
            s = jnp.maximum(s, 0.0)
            w0 = wiT_ref[2 * hp:2 * hp + 1, :] * (IDX_HEADS ** -0.5)
            w1 = wiT_ref[2 * hp + 1:2 * hp + 2, :] * (IDX_HEADS ** -0.5)
            acc = acc + s[:, :Q_BLOCK] * w0 + s[:, Q_BLOCK:] * w1
        s_idx = c * DSA_KC + row_iota
        key_sc[c] = jnp.where(s_idx <= t_idx, _sortable_key(acc), INT_MIN)
        return carry

    lax.fori_loop(0, nch, score_chunk, 0)

    def count(pred):
        def body(c, acc):
            hit = pred(key_sc[c], c * DSA_KC + row_iota)
            return acc + hit.reshape(DSA_KC // 8, 8, Q_BLOCK).sum(axis=0)
        acc = lax.fori_loop(0, nch, body, jnp.zeros((8, Q_BLOCK), jnp.int32))
        return jnp.sum(acc, axis=0, keepdims=True)

    c_nonneg = count(lambda k, s: jnp.where(k >= 0, 1, 0))
    thr = jnp.where(c_nonneg >= topk, 0, INT_MIN).astype(jnp.int32)

    def thr_bit(i, thr):
        cand = thr + jnp.left_shift(jnp.int32(1), 30 - i)
        return jnp.where(count(lambda k, s: jnp.where(k >= cand, 1, 0)) >= topk, cand, thr)

    thr = lax.fori_loop(0, 31, thr_bit, thr)
    need = topk - count(lambda k, s: jnp.where(k > thr, 1, 0))

    def lim_bit(i, lim):
        cand = lim + jnp.left_shift(jnp.int32(1), idx_bits - 1 - i)
        c = count(lambda k, s: jnp.where(k == thr, jnp.where(s < cand, 1, 0), 0))
        return jnp.where(c <= need, cand, lim)

    n_ties = count(lambda k, s: jnp.where(k == thr, 1, 0))
    settled = jnp.min(jnp.where(thr == INT_MIN, 1, jnp.where(n_ties == need, 1, 0))) == 1
    lim = lax.cond(settled,
                   lambda: jnp.full((1, Q_BLOCK), 1 << idx_bits, jnp.int32),
                   lambda: lax.fori_loop(0, idx_bits, lim_bit, jnp.zeros((1, Q_BLOCK), jnp.int32)))

    def mask_chunk(c, carry):
        k = key_sc[c]
        s_idx = c * DSA_KC + row_iota
        tie = jnp.where(k == thr, jnp.where(s_idx < lim, 0.0, NEG_BIG), NEG_BIG)
        m = jnp.where(k > thr, 0.0, tie)
        mask_sc[c] = jnp.where(k == INT_MIN, NEG_BIG, m)
        return carry

    lax.fori_loop(0, nch, mask_chunk, 0)

    n_cols = G_D * Q_BLOCK
    half = DSA_KC // 2

    def head_q(j):
        qj = jnp.concatenate([qT_ref[(G_D * j + g) * HEAD_DIM:(G_D * j + g + 1) * HEAD_DIM, :]
                              for g in range(G_D)], axis=1)
        return (qj.astype(F32) * (HEAD_DIM ** -0.5)).astype(cdt)

    for jp in range(KV_D // 2):
        q0, q1 = head_q(2 * jp), head_q(2 * jp + 1)
        zq = jnp.zeros_like(q0)
        rhs = jnp.concatenate([jnp.concatenate([q0, zq], axis=0), jnp.concatenate([zq, q1], axis=0)], axis=1)

        def chunk_body(c, carry, jp=jp, rhs=rhs):
            m, l, acc0, acc1 = carry
            kc = k_ref[0, c, :, jp * 2 * HEAD_DIM:(jp + 1) * 2 * HEAD_DIM].astype(cdt)
            s = jnp.dot(kc, rhs, preferred_element_type=F32)
            wt = win_ref[qb - 2 * c]
            madd = mask_sc[c]
            parts = []
            for hh in range(2 * G_D):
                h = 2 * G_D * jp + hh
                tiles = []
                for u in range(2):
                    lo = (1 - u) * half
                    r = jnp.broadcast_to(wt[h:h + 1, lo:lo + 2 * half], (half, 2 * half))
                    tiles.append(pltpu.roll(r, 0, 1, stride=1, stride_axis=0)[:, half:])
                parts.append(s[:, hh * Q_BLOCK:(hh + 1) * Q_BLOCK] + (jnp.concatenate(tiles, axis=0) + madd))
            s = jnp.concatenate(parts, axis=1)
            m_new = jnp.maximum(m, jnp.max(s, axis=0, keepdims=True))
            alpha = jnp.exp(m - m_new)
            p = jnp.exp(s - m_new)
            l = l * alpha + jnp.sum(p, axis=0, keepdims=True)
            p = p.astype(cdt)
            v0 = vT_ref[0, c, (2 * jp) * HEAD_DIM:(2 * jp + 1) * HEAD_DIM, :]
            v1 = vT_ref[0, c, (2 * jp + 1) * HEAD_DIM:(2 * jp + 2) * HEAD_DIM, :]
            acc0 = acc0 * alpha[:, :n_cols] + jnp.dot(v0, p[:, :n_cols], preferred_element_type=F32)
            acc1 = acc1 * alpha[:, n_cols:] + jnp.dot(v1, p[:, n_cols:], preferred_element_type=F32)
            return m_new, l, acc0, acc1

        init = (jnp.full((1, 2 * n_cols), NEG_BIG, F32), jnp.zeros((1, 2 * n_cols), F32),
                jnp.zeros((HEAD_DIM, n_cols), F32), jnp.zeros((HEAD_DIM, n_cols), F32))
        m, l, acc0, acc1 = lax.fori_loop(0, nch, chunk_body, init)
        inv = 1.0 / l
        for jj, acc in enumerate((acc0, acc1)):
            o = acc * inv[:, jj * n_cols:(jj + 1) * n_cols]
            for g in range(G_D):
                r0 = (G_D * (2 * jp + jj) + g) * HEAD_DIM
                z = zT_ref[r0:r0 + HEAD_DIM, :]
                gate = z * (1.0 / (1.0 + jnp.exp(-z)))
                o_ref[r0:r0 + HEAD_DIM, :] = (o[:, g * Q_BLOCK:(g + 1) * Q_BLOCK] * gate).astype(o_ref.dtype)


def dsa_bias_windows(rel_bias, seq_len):
    o = jnp.arange(seq_len // Q_BLOCK)[:, None]
    m = jnp.arange(BIAS_WIN)[None, :]
    d = jnp.maximum(o * Q_BLOCK + m - DSA_KC, 0)
    return jnp.moveaxis(rel_bias[rel_bucket(d)].astype(F32), -1, 1)


def dsa_prompt_attend(qT, qiT, wiT, zT, ki4, k4, v4T, win, *, n_batch, seq_len, cdt=BF16):
    nqb = seq_len // Q_BLOCK
    nc = seq_len // DSA_KC
    topk = min(TOPK_MAX, seq_len // 4)
    idx_bits = int(math.log2(seq_len)) + 1
    tok = lambda b, q: (0, b * nqb + q)
    per_batch = lambda b, q: (b, 0, 0, 0)
    return pl.pallas_call(
        functools.partial(_dsa_prompt_kernel, topk=topk, idx_bits=idx_bits, cdt=cdt),
        grid=(n_batch, nqb),
        in_specs=[pl.BlockSpec((ATT_WIDTH, Q_BLOCK), tok),
                  pl.BlockSpec((IDX_HEADS * IDX_DIM, Q_BLOCK), tok),
                  pl.BlockSpec((IDX_HEADS, Q_BLOCK), tok),
                  pl.BlockSpec((ATT_WIDTH, Q_BLOCK), tok),
                  pl.BlockSpec((1, nc, DSA_KC, IDX_DIM), per_batch),
                  pl.BlockSpec((1, nc, DSA_KC, D_KV), per_batch),
                  pl.BlockSpec((1, nc, D_KV, DSA_KC), per_batch),
                  pl.BlockSpec((nqb, N_HEADS, BIAS_WIN), lambda b, q: (0, 0, 0))],
        out_specs=pl.BlockSpec((ATT_WIDTH, Q_BLOCK), tok),
        out_shape=jax.ShapeDtypeStruct((ATT_WIDTH, n_batch * seq_len), BF16),
        scratch_shapes=[pltpu.VMEM((nc, DSA_KC, Q_BLOCK), jnp.int32),
                        pltpu.VMEM((nc, DSA_KC, Q_BLOCK), F32)],
        compiler_params=pltpu.CompilerParams(
            dimension_semantics=("parallel", "arbitrary"), vmem_limit_bytes=VMEM_LIMIT),
        name="dsa_prompt_attend",
    )(qT, qiT, wiT, zT, ki4, k4, v4T, win)


DSS_NP1 = 32
DSS_NP2 = 16
DSS_TP = 8
DSS_GROWS = G_D * DSS_TP
T5_LAST_BUCKET_DIST = 1600


def _dsa_sample_select_kernel(pt_ref, qi_ref, wb_ref, kinew_ref, *rest, n_pages, n_new, topk, idx_bits, cdt):
    del pt_ref
    page_refs, mask_ref, key_sc = rest[:DSS_NP1], rest[DSS_NP1], rest[DSS_NP1 + 1]
    s = pl.program_id(1)
    lane = lax.broadcasted_iota(jnp.int32, (DSS_TP, PAGE_SIZE), 1)
    trow = lax.broadcasted_iota(jnp.int32, (DSS_TP, PAGE_SIZE), 0)
    qi = qi_ref[0]
    wb = wb_ref[0]

    def page_keys(kp):
        sc = lax.dot_general(qi, kp.astype(cdt), (((1,), (1,)), ((), ())),
                             preferred_element_type=F32) * (IDX_DIM ** -0.5)
        sc = jnp.maximum(sc, 0.0) * wb
        return _sortable_key(sc.reshape(IDX_HEADS, DSS_TP, PAGE_SIZE).sum(axis=0))

    for i in range(DSS_NP1):
        key_sc[s * DSS_NP1 + i] = page_keys(page_refs[i][0, 0])

    @pl.when(s == 0)
    def _():
        kn = page_keys(kinew_ref[0])
        key_sc[n_pages] = jnp.where(lane < n_new, jnp.where(lane <= trow, kn, INT_MIN), INT_MIN)

    @pl.when(s == n_pages // DSS_NP1 - 1)
    def _():
        all_shape = (n_pages + 1, DSS_TP, PAGE_SIZE)

        def count(pred):
            key_idx = (lax.broadcasted_iota(jnp.int32, all_shape, 0) * PAGE_SIZE
                       + lax.broadcasted_iota(jnp.int32, all_shape, 2))
            acc = pred(key_sc[...], key_idx).sum(axis=0)
            return jnp.broadcast_to(jnp.sum(acc, axis=1, keepdims=True), (DSS_TP, PAGE_SIZE))

        c_nonneg = count(lambda k, i: jnp.where(k >= 0, 1, 0))
        thr = jnp.where(c_nonneg >= topk, 0, INT_MIN).astype(jnp.int32)

        def thr_bit(b, thr):
            cand = thr + jnp.left_shift(jnp.int32(1), 30 - b)
            return jnp.where(count(lambda k, i: jnp.where(k >= cand, 1, 0)) >= topk, cand, thr)

        thr = lax.fori_loop(0, 31, thr_bit, thr)
        need = topk - count(lambda k, i: jnp.where(k > thr, 1, 0))

        def lim_bit(b, lim):
            cand = lim + jnp.left_shift(jnp.int32(1), idx_bits - 1 - b)
            c = count(lambda k, i: jnp.where(k == thr, jnp.where(i < cand, 1, 0), 0))
            return jnp.where(c <= need, cand, lim)

        n_ties = count(lambda k, i: jnp.where(k == thr, 1, 0))
        settled = jnp.min(jnp.where(thr == INT_MIN, 1, jnp.where(n_ties == need, 1, 0))) == 1
        lim = lax.cond(settled,
                       lambda: jnp.full((DSS_TP, PAGE_SIZE), 1 << idx_bits, jnp.int32),
                       lambda: lax.fori_loop(0, idx_bits, lim_bit, jnp.zeros((DSS_TP, PAGE_SIZE), jnp.int32)))

        k = key_sc[...]
        key_idx = (lax.broadcasted_iota(jnp.int32, all_shape, 0) * PAGE_SIZE
                   + lax.broadcasted_iota(jnp.int32, all_shape, 2))
        tie = jnp.where(k == thr, jnp.where(key_idx < lim, 0.0, NEG_BIG), NEG_BIG)
        mask_ref[0] = jnp.where(k == INT_MIN, NEG_BIG, jnp.where(k > thr, 0.0, tie))


def _dsa_sample_attend_kernel(pt_ref, q_ref, z_ref, mask_ref, masknew_ref, bnear_ref, bfar_ref, kvnew_ref, *rest,
                              n_pages, n_far, cdt):
    del pt_ref
    page_refs, o_ref = rest[:DSS_NP2], rest[DSS_NP2]
    m_sc, l_sc, acc_sc = rest[DSS_NP2 + 1:]
    s = pl.program_id(1)
    heads = range(KV_D)
    rep = DSS_GROWS // DSS_TP

    def attend(pages, masks, page_ids):
        n = len(pages)
        madd = jnp.concatenate([jnp.concatenate([mk] * rep, axis=0) for mk in masks], axis=1)
        kT = [jnp.concatenate([pg(0, h) for pg in pages], axis=1).astype(cdt) for h in heads]
        vT = [jnp.concatenate([pg(1, h) for pg in pages], axis=1).astype(cdt) for h in heads]
        logits = [jnp.dot(q_ref[0, h], kT[h], preferred_element_type=F32) for h in heads]
        bias = [jnp.concatenate([jnp.where(pid >= n_far, bnear_ref[jnp.maximum(pid - n_far, 0), h], bfar_ref[h])
                                 for pid in page_ids], axis=1) for h in heads]
        logits = [lg + (b + madd) for lg, b in zip(logits, bias)]
        m_old = [m_sc[h] for h in heads]
        m_new = [jnp.maximum(mo, jnp.broadcast_to(jnp.max(lg, axis=1, keepdims=True), mo.shape))
                 for mo, lg in zip(m_old, logits)]
        alpha = [jnp.exp(mo - mn) for mo, mn in zip(m_old, m_new)]
        p = [jnp.exp(lg - jnp.concatenate([mn] * n, axis=1)) for lg, mn in zip(logits, m_new)]
        pv = [lax.dot_general(ph.astype(cdt), vT[h], (((1,), (1,)), ((), ())), preferred_element_type=F32)
              for h, ph in zip(heads, p)]
        for h in heads:
            l_sc[h] = l_sc[h] * alpha[h] + jnp.broadcast_to(jnp.sum(p[h], axis=1, keepdims=True), alpha[h].shape)
            acc_sc[h] = acc_sc[h] * alpha[h][:, :HEAD_DIM] + pv[h]
            m_sc[h] = m_new[h]

    @pl.when(s == 0)
    def _():
        m_sc[...] = jnp.full_like(m_sc, NEG_BIG)
        l_sc[...] = jnp.zeros_like(l_sc)
        acc_sc[...] = jnp.zeros_like(acc_sc)
        attend([lambda c, h: kvnew_ref[0, c, h]], [masknew_ref[0, 0]], [n_pages])

    attend([(lambda c, h, r=r: r[0, 0, c, h]) for r in page_refs], [mask_ref[0, i] for i in range(DSS_NP2)],
           [s * DSS_NP2 + i for i in range(DSS_NP2)])

    @pl.when(s == n_pages // DSS_NP2 - 1)
    def _():
        for h in heads:
            z = z_ref[0, h]
            o_ref[0, h] = acc_sc[h] * (1.0 / l_sc[h][:, :HEAD_DIM]) * (z * (1.0 / (1.0 + jnp.exp(-z))))


def dsa_sample(q_s, z_s, qi_s, wi_s, ki_s, kv_s, kv_pool, kidx_pool, layer, page_table, rel_bias, cdt=BF16):
    n_dec, n_pages = page_table.shape
    dec_len = q_s.shape[0] // n_dec
    past = n_pages * PAGE_SIZE
    total = past + dec_len
    topk = min(TOPK_MAX, total // 4)
    idx_bits = int(math.log2(total)) + 1
    pad_t = DSS_TP - dec_len
    n_pairs = KV_D // 2
    eye2 = jnp.eye(2, dtype=F32)

    qi = jnp.pad(jnp.swapaxes(qi_s.reshape(n_dec, dec_len, IDX_HEADS, IDX_DIM), 1, 2), ((0, 0), (0, 0), (0, pad_t), (0, 0)))
    qi = qi.reshape(n_dec, IDX_HEADS * DSS_TP, IDX_DIM).astype(cdt)
    wb = jnp.pad(jnp.swapaxes(wi_s.reshape(n_dec, dec_len, IDX_HEADS), 1, 2) * (IDX_HEADS ** -0.5), ((0, 0), (0, 0), (0, pad_t)))
    wb = jnp.broadcast_to(wb.reshape(n_dec, IDX_HEADS * DSS_TP, 1), (n_dec, IDX_HEADS * DSS_TP, PAGE_SIZE))
    ki_new = jnp.pad(ki_s.reshape(n_dec, dec_len, IDX_DIM), ((0, 0), (0, PAGE_SIZE - dec_len), (0, 0)))
    kidx4 = kidx_pool.reshape(kidx_pool.shape[0], kidx_pool.shape[1], PAGE_SIZE, IDX_DIM)
    page_spec = lambda np_, i, width: pl.BlockSpec(
        (1, 1, PAGE_SIZE, width), lambda b, s, pt: (layer, pt[b, s * np_ + i], 0, 0))
    per_b3 = lambda b, s, pt: (b, 0, 0)
    mask = pl.pallas_call(
        functools.partial(_dsa_sample_select_kernel, n_pages=n_pages, n_new=dec_len, topk=topk, idx_bits=idx_bits,
                          cdt=cdt),
        grid_spec=pltpu.PrefetchScalarGridSpec(
            num_scalar_prefetch=1, grid=(n_dec, n_pages // DSS_NP1),
            in_specs=[pl.BlockSpec((1, IDX_HEADS * DSS_TP, IDX_DIM), per_b3),
                      pl.BlockSpec((1, IDX_HEADS * DSS_TP, PAGE_SIZE), per_b3),
                      pl.BlockSpec((1, PAGE_SIZE, IDX_DIM), per_b3)]
                     + [page_spec(DSS_NP1, i, IDX_DIM) for i in range(DSS_NP1)],
            out_specs=pl.BlockSpec((1, n_pages + 1, DSS_TP, PAGE_SIZE), lambda b, s, pt: (b, 0, 0, 0)),
            scratch_shapes=[pltpu.VMEM((n_pages + 1, DSS_TP, PAGE_SIZE), jnp.int32)]),
        out_shape=jax.ShapeDtypeStruct((n_dec, n_pages + 1, DSS_TP, PAGE_SIZE), F32),
        compiler_params=pltpu.CompilerParams(
            dimension_semantics=("parallel", "arbitrary"), vmem_limit_bytes=VMEM_LIMIT),
        name="dsa_sample_select",
    )(page_table, qi, wb, ki_new, *([kidx4] * DSS_NP1))

    def head_rows(t, scale):
        t = t.reshape(n_dec, dec_len, KV_D, G_D, HEAD_DIM).transpose(0, 2, 3, 1, 4) * scale
        return jnp.pad(t, ((0, 0),) * 3 + ((0, pad_t), (0, 0))).reshape(n_dec, KV_D, DSS_GROWS, HEAD_DIM)

    q_hr = head_rows(q_s, HEAD_DIM ** -0.5).astype(cdt)
    z_hr = head_rows(z_s, 1.0)
    n_far = max(0, min(n_pages, (past - (PAGE_SIZE - 1) - T5_LAST_BUCKET_DIST) // PAGE_SIZE + 1))
    near_pages = jnp.arange(n_far, n_pages + 1)
    dist = (past + jnp.arange(DSS_TP)[None, :, None]
            - (near_pages[:, None, None] * PAGE_SIZE + jnp.arange(PAGE_SIZE)[None, None, :]))
    b_near = rel_bias[rel_bucket(dist)].astype(F32)
    b_near = b_near.transpose(0, 3, 1, 2).reshape(n_pages + 1 - n_far, KV_D, DSS_GROWS, PAGE_SIZE)
    b_far = jnp.broadcast_to(rel_bias[N_BUCKETS - 1].astype(F32)[:, None, None], (N_HEADS, DSS_TP, PAGE_SIZE))
    b_far = b_far.reshape(KV_D, DSS_GROWS, PAGE_SIZE)
    kv_new = jnp.pad(kv_s.reshape(n_dec, dec_len, 2, KV_D, HEAD_DIM).transpose(0, 2, 3, 4, 1),
                     ((0, 0),) * 4 + ((0, PAGE_SIZE - dec_len),))
    kv_t = kv_pool.reshape(kv_pool.shape[0], kv_pool.shape[1], PAGE_SIZE, 2, KV_D, HEAD_DIM).transpose(0, 1, 3, 4, 5, 2)
    kv_page = lambda i: pl.BlockSpec((1, 1, 2, KV_D, HEAD_DIM, PAGE_SIZE),
                                     lambda b, s, pt: (layer, pt[b, s * DSS_NP2 + i], 0, 0, 0, 0))
    per_b4 = lambda b, s, pt: (b, 0, 0, 0)
    o = pl.pallas_call(
        functools.partial(_dsa_sample_attend_kernel, n_pages=n_pages, n_far=n_far, cdt=cdt),
        grid_spec=pltpu.PrefetchScalarGridSpec(
            num_scalar_prefetch=1, grid=(n_dec, n_pages // DSS_NP2),
            in_specs=[pl.BlockSpec((1, KV_D, DSS_GROWS, HEAD_DIM), per_b4),
                      pl.BlockSpec((1, KV_D, DSS_GROWS, HEAD_DIM), per_b4),
                      pl.BlockSpec((1, DSS_NP2, DSS_TP, PAGE_SIZE), lambda b, s, pt: (b, s, 0, 0)),
                      pl.BlockSpec((1, 1, DSS_TP, PAGE_SIZE), lambda b, s, pt: (b, n_pages, 0, 0)),
                      pl.BlockSpec(b_near.shape, lambda b, s, pt: (0, 0, 0, 0)),
                      pl.BlockSpec(b_far.shape, lambda b, s, pt: (0, 0, 0)),
                      pl.BlockSpec((1, 2, KV_D, HEAD_DIM, PAGE_SIZE), lambda b, s, pt: (b, 0, 0, 0, 0))]
                     + [kv_page(i) for i in range(DSS_NP2)],
            out_specs=pl.BlockSpec((1, KV_D, DSS_GROWS, HEAD_DIM), per_b4),
            scratch_shapes=[pltpu.VMEM((KV_D, DSS_GROWS, LANE), F32), pltpu.VMEM((KV_D, DSS_GROWS, LANE), F32),
                            pltpu.VMEM((KV_D, DSS_GROWS, HEAD_DIM), F32)]),
        out_shape=jax.ShapeDtypeStruct((n_dec, KV_D, DSS_GROWS, HEAD_DIM), F32),
        compiler_params=pltpu.CompilerParams(
            dimension_semantics=("parallel", "arbitrary"), vmem_limit_bytes=VMEM_LIMIT),
        name="dsa_sample_attend",
    )(page_table, q_hr, z_hr, mask, mask, b_near, b_far, kv_new, *([kv_t] * DSS_NP2))
    o = o.reshape(n_dec, KV_D, G_D, DSS_TP, HEAD_DIM)[:, :, :, :dec_len]
    return o.transpose(0, 3, 1, 2, 4).reshape(n_dec * dec_len, ATT_WIDTH)


def _pad_cols(w, n):
    return jnp.pad(w, ((0, 0), (0, n - w.shape[1])))


def kernel(x_prompt, x_sample, cache_a_kv, state_s5, state_gdn, state_gdn_conv, cache_d_kv, cache_d_kidx,
           page_table, p_prompt, p_sample, rel_bias, ln_g, ln_b, ple_gate_w, ple_w,
           a_w_in, a_sinks, a_w_out,
           s5_w_in, s5_a_re, s5_a_im, s5_b_re, s5_b_im, s5_c_re, s5_c_im, s5_d, s5_log_dt, s5_w_glu, s5_w_out,
           gdn_w_in, gdn_conv_w, gdn_a_log, gdn_dt_bias, gdn_norm_w, gdn_w_out,
           dsa_w_in, dsa_w_out):
    x = join_tokens(x_prompt, x_sample)
    x_bf = x.astype(BF16)
    outs = {}
    (a_w_in, a_w_out, s5_w_in, s5_w_glu, s5_w_out, gdn_w_in, gdn_w_out, dsa_w_in, dsa_w_out, ple_gate_w, ple_w) = (
        w.astype(BF16) for w in (a_w_in, a_w_out, s5_w_in, s5_w_glu, s5_w_out, gdn_w_in, gdn_w_out, dsa_w_in,
                                 dsa_w_out, ple_gate_w, ple_w))

    def post(i, x, h):
        return post_norm_ple(x, h, join_tokens(p_prompt[i], p_sample[i]), ln_g[i], ln_b[i], ple_gate_w[i], ple_w[i])

    w_in = a_w_in[0]
    c_k, c_v, c_z = ATT_WIDTH, ATT_WIDTH + A_KV, ATT_WIDTH + 2 * A_KV
    kv_nat = matmul(x_bf, w_in, c_k, 2 * A_KV)
    xT_bf = x_bf[:N_PROMPT_TOK].T
    x_s = x_bf[N_PROMPT_TOK:]
    qT = matmul_wt(w_in, xT_bf, 0, ATT_WIDTH, out_dtype=BF16)
    vT = matmul_wt(w_in, xT_bf, c_v, A_KV, out_dtype=BF16)
    zT = matmul_wt(w_in, xT_bf, c_z, ATT_WIDTH)
    bias_p, sink_p, bias_s, sink_s = swa_tables(rel_bias, a_sinks[0], DEC_SEQ)
    h_p = matmul_ta(swa_prompt(qT, vT, zT, kv_nat, bias_p, sink_p, n_batch=BATCH, seq_len=SEQ), a_w_out[0])
    gs, outs['a_s'] = swa_sample(matmul(x_s, w_in, 0, ATT_WIDTH), matmul(x_s, w_in, c_z, ATT_WIDTH),
                                 kv_nat[N_PROMPT_TOK:], cache_a_kv[0], bias_s, sink_s)
    outs['a_p'] = kv_nat[:N_PROMPT_TOK].reshape(BATCH, SEQ, 2, KV_A, HEAD_DIM)[:, SEQ - WINDOW:]
    x, x_bf = post(0, x, jnp.concatenate([h_p, matmul(gs.astype(BF16), a_w_out[0])], axis=0))

    proj = matmul(x_bf, s5_w_in[0])
    tables = s5_tables(s5_a_re[0], s5_a_im[0], s5_b_re[0], s5_b_im[0], s5_c_re[0], s5_c_im[0], s5_log_dt[0])
    gp, gs, outs['s5_p'], outs['s5_s'] = s5_layer(proj, state_s5[0], tables, s5_d[0], s5_w_glu[0],
                                                  n_batch=BATCH, seq_len=SEQ, n_dec=DEC_BATCH, dec_len=DEC_SEQ)
    x, x_bf = post(1, x, jnp.concatenate([matmul(gp, s5_w_out[0]), matmul(gs, s5_w_out[0])], axis=0))

    w_in = gdn_w_in[0]
    c_gz = GDN_CONV_CH + GDN_V_WIDTH
    qkvz = matmul(x_bf, w_in, 0, c_gz)
    ab = matmul(x_bf, _pad_cols(w_in[:, c_gz:], LANE))
    gp, gs, outs['gd_p'], outs['gd_s'], outs['gc_p'], outs['gc_s'] = gdn_layer(
        qkvz, ab, state_gdn[0], state_gdn_conv[0], gdn_conv_w[0], gdn_a_log[0], gdn_dt_bias[0], gdn_norm_w[0],
        n_batch=BATCH, seq_len=SEQ, n_dec=DEC_BATCH, dec_len=DEC_SEQ)
    x, x_bf = post(2, x, jnp.concatenate([matmul(gp, gdn_w_out[0]), matmul(gs.astype(BF16), gdn_w_out[0])], axis=0))

    w_in = dsa_w_in[0]
    c_kv = ATT_WIDTH + 2 * D_KV
    c_z = 2 * ATT_WIDTH + 2 * D_KV
    c_qi = c_z + IDX_HEADS * IDX_DIM
    kv_nat = matmul(x_bf, w_in, ATT_WIDTH, 2 * D_KV)
    kiw = matmul(x_bf, _pad_cols(w_in[:, c_qi:], 2 * LANE))
    xT_bf = x_bf[:N_PROMPT_TOK].T
    x_s = x_bf[N_PROMPT_TOK:]
    qT = matmul_wt(w_in, xT_bf, 0, ATT_WIDTH, out_dtype=BF16)
    qiT = matmul_wt(w_in, xT_bf, c_z, IDX_HEADS * IDX_DIM, out_dtype=BF16)
    zT = matmul_wt(w_in, xT_bf, c_kv, ATT_WIDTH)
    wiT = matmul_wt(_pad_cols(w_in[:, c_qi + IDX_DIM:], LANE), xT_bf, 0, LANE)
    nc = SEQ // DSA_KC
    kv_p = kv_nat[:N_PROMPT_TOK]
    v4T = jnp.swapaxes(kv_p[:, D_KV:].astype(BF16).reshape(BATCH, nc, DSA_KC, D_KV), 2, 3)
    gT = dsa_prompt_attend(qT, qiT, wiT, zT, kiw[:N_PROMPT_TOK].reshape(BATCH, nc, DSA_KC, 2 * LANE),
                           kv_p.reshape(BATCH, nc, DSA_KC, 2 * D_KV), v4T, dsa_bias_windows(rel_bias, SEQ),
                           n_batch=BATCH, seq_len=SEQ)
    h_p = matmul_ta(gT, dsa_w_out[0])
    kiw_s = kiw[N_PROMPT_TOK:]
    gs = dsa_sample(matmul(x_s, w_in, 0, ATT_WIDTH), matmul(x_s, w_in, c_kv, ATT_WIDTH),
                    matmul(x_s, w_in, c_z, IDX_HEADS * IDX_DIM),
                    kiw_s[:, IDX_DIM:IDX_DIM + IDX_HEADS], kiw_s[:, :IDX_DIM], kv_nat[N_PROMPT_TOK:],
                    cache_d_kv, cache_d_kidx, 0, page_table, rel_bias)
    h_s = matmul(gs.astype(BF16), dsa_w_out[0])
    outs['dkv_p'] = kv_p.reshape(BATCH, SEQ, 2, KV_D, HEAD_DIM)
    outs['dkv_s'] = kv_nat[N_PROMPT_TOK:].reshape(DEC_BATCH, DEC_SEQ, 2, KV_D, HEAD_DIM)
    outs['dki_p'] = kiw[:N_PROMPT_TOK, :IDX_DIM].reshape(BATCH, SEQ, IDX_DIM)
    outs['dki_s'] = kiw_s[:, :IDX_DIM].reshape(DEC_BATCH, DEC_SEQ, IDX_DIM)
    x, x_bf = post(3, x, jnp.concatenate([h_p, h_s], axis=0))

    yp, ys = split_tokens(x)
    st = lambda name: outs[name][None]
    return (yp, ys, st('a_p'), st('a_s'), st('s5_p'), st('s5_s'), st('gd_p'), st('gd_s'),
            st('gc_p'), st('gc_s'), st('dkv_p'), st('dkv_s'), st('dki_p'), st('dki_s'))
```

```python
import functools
import math

import jax
import jax.numpy as jnp
from jax import lax
from jax.experimental import pallas as pl
from jax.experimental.pallas import tpu as pltpu

D_MODEL = 2048
BATCH = 4
SEQ = 2048
DEPTH = 4
DEC_BATCH = 32
DEC_SEQ = 4
PAGE_SIZE = 128
N_MIXERS = 4
PLE_DIM = 256
ALPHA = (2 * DEPTH) ** 0.25
LN_EPS = 1e-5
N_BUCKETS = 32
REL_MAX_DIST = 2048
N_HEADS = 32
HEAD_DIM = 64
ATT_WIDTH = N_HEADS * HEAD_DIM
WINDOW = 128
KV_A = 4
A_KV = KV_A * HEAD_DIM
KV_D = 8
D_KV = KV_D * HEAD_DIM
IDX_HEADS = 16
IDX_DIM = 128
TOPK_MAX = 256
Q_BLOCK = 128
S5_WIDTH = D_MODEL
S5_GROUP = 16
S5_GROUPS = S5_WIDTH // S5_GROUP
S5_STATE = 64
GDN_QK_HEADS = 16
GDN_V_HEADS = 32
GDN_DK = 128
GDN_DV = 128
GDN_CONV = 4
GDN_CHUNK = 64
GDN_QK_WIDTH = GDN_QK_HEADS * GDN_DK
GDN_V_WIDTH = GDN_V_HEADS * GDN_DV
GDN_CONV_CH = 2 * GDN_QK_WIDTH + GDN_V_WIDTH

F32 = jnp.float32
BF16 = jnp.bfloat16

N_PROMPT_TOK = BATCH * SEQ
N_SAMPLE_TOK = DEC_BATCH * DEC_SEQ
N_TOK = N_PROMPT_TOK + N_SAMPLE_TOK

V7X_VMEM_BYTES = 64 * 1024 * 1024
VMEM_LIMIT = 48 * 1024 * 1024
LANE = 128


def _mm_kernel(x_ref, w_ref, o_ref):
    o_ref[...] = jnp.dot(x_ref[...], w_ref[...].astype(BF16), preferred_element_type=F32).astype(o_ref.dtype)


def _pick_tile(n, prefs, also=0):
    for t in prefs:
        if n % t == 0 and also % t == 0:
            return t
    raise ValueError(f"no tile for {n} (offset {also})")


def matmul(x, w, col0=0, n=None, out_dtype=F32):
    m, k = x.shape
    n = w.shape[1] - col0 if n is None else n
    tm = _pick_tile(m, (640, 512, 320, 256, 128, 64, 32, 16, 8))
    tn = _pick_tile(n, (512, 384, 256, 128), col0)
    c0 = col0 // tn
    return pl.pallas_call(
        _mm_kernel,
        grid=(m // tm, n // tn),
        in_specs=[pl.BlockSpec((tm, k), lambda i, j: (i, 0)),
                  pl.BlockSpec((k, tn), lambda i, j: (0, c0 + j))],
        out_specs=pl.BlockSpec((tm, tn), lambda i, j: (i, j)),
        out_shape=jax.ShapeDtypeStruct((m, n), out_dtype),
        compiler_params=pltpu.CompilerParams(
            dimension_semantics=("parallel", "parallel"), vmem_limit_bytes=VMEM_LIMIT),
        name="proj_matmul",
    )(x, w)


def _mm_wt_kernel(w_ref, xt_ref, o_ref, wt_sc):
    @pl.when(pl.program_id(1) == 0)
    def _():
        wt_sc[...] = w_ref[...].astype(F32).T.astype(BF16)

    o_ref[...] = jnp.dot(wt_sc[...], xt_ref[...], preferred_element_type=F32).astype(o_ref.dtype)


def matmul_wt(w, xt, col0, n, out_dtype=F32):
    k, m = xt.shape
    tr = _pick_tile(n, (512, 256, 128), col0)
    tt = _pick_tile(m, (512, 256, 128))
    r0 = col0 // tr
    return pl.pallas_call(
        _mm_wt_kernel,
        grid=(n // tr, m // tt),
        in_specs=[pl.BlockSpec((k, tr), lambda i, j: (0, r0 + i)),
                  pl.BlockSpec((k, tt), lambda i, j: (0, j))],
        out_specs=pl.BlockSpec((tr, tt), lambda i, j: (i, j)),
        out_shape=jax.ShapeDtypeStruct((n, m), out_dtype),
        scratch_shapes=[pltpu.VMEM((tr, k), BF16)],
        compiler_params=pltpu.CompilerParams(
            dimension_semantics=("parallel", "arbitrary"), vmem_limit_bytes=VMEM_LIMIT),
        name="proj_matmul_wt",
    )(w, xt)


def _mm_ta_kernel(xt_ref, w_ref, o_ref):
    o_ref[...] = lax.dot_general(xt_ref[...], w_ref[...].astype(BF16), (((0,), (0,)), ((), ())),
                                 preferred_element_type=F32).astype(o_ref.dtype)


def matmul_ta(xt, w, out_dtype=F32):
    k, m = xt.shape
    n = w.shape[1]
    tm = _pick_tile(m, (512, 256, 128))
    tn = _pick_tile(n, (512, 384, 256, 128))
    return pl.pallas_call(
        _mm_ta_kernel,
        grid=(m // tm, n // tn),
        in_specs=[pl.BlockSpec((k, tm), lambda i, j: (0, i)),
                  pl.BlockSpec((k, tn), lambda i, j: (0, j))],
        out_specs=pl.BlockSpec((tm, tn), lambda i, j: (i, j)),
        out_shape=jax.ShapeDtypeStruct((m, n), out_dtype),
        compiler_params=pltpu.CompilerParams(
            dimension_semantics=("parallel", "parallel"), vmem_limit_bytes=VMEM_LIMIT),
        name="proj_matmul_ta",
    )(xt, w)


POST_TM = 320
POST_TN = 512


def _post_kernel(x_ref, h_ref, p_ref, g_ref, b_ref, wg_ref, wp_ref, o_ref, obf_ref, y_sc, ybf_sc):
    j = pl.program_id(1)

    @pl.when(j == 0)
    def _():
        t = ALPHA * x_ref[...] + h_ref[...]
        mu = jnp.mean(t, axis=-1, keepdims=True)
        d = t - mu
        var = jnp.mean(d * d, axis=-1, keepdims=True)
        y = d * lax.rsqrt(var + LN_EPS) * g_ref[...] + b_ref[...]
        ybf_sc[...] = y.astype(BF16)
        for jj in range(D_MODEL // POST_TN):
            y_sc[jj] = y[:, jj * POST_TN:(jj + 1) * POST_TN]

    gate = jnp.dot(ybf_sc[...], wg_ref[...].astype(BF16), preferred_element_type=F32)
    ple = jnp.dot(p_ref[...].astype(BF16), wp_ref[...].astype(BF16), preferred_element_type=F32)
    o = y_sc[j] + (1.0 / (1.0 + jnp.exp(-gate))) * ple
    o_ref[...] = o
    obf_ref[...] = o.astype(BF16)


def post_norm_ple(x, h, p_bf, g, b, wg_bf, wp_bf):
    m = x.shape[0]
    tm, tn = POST_TM, POST_TN
    return pl.pallas_call(
        _post_kernel,
        grid=(m // tm, D_MODEL // tn),
        in_specs=[pl.BlockSpec((tm, D_MODEL), lambda i, j: (i, 0)),
                  pl.BlockSpec((tm, D_MODEL), lambda i, j: (i, 0)),
                  pl.BlockSpec((tm, PLE_DIM), lambda i, j: (i, 0)),
                  pl.BlockSpec((1, D_MODEL), lambda i, j: (0, 0)),
                  pl.BlockSpec((1, D_MODEL), lambda i, j: (0, 0)),
                  pl.BlockSpec((D_MODEL, tn), lambda i, j: (0, j)),
                  pl.BlockSpec((PLE_DIM, tn), lambda i, j: (0, j))],
        out_specs=[pl.BlockSpec((tm, tn), lambda i, j: (i, j)),
                   pl.BlockSpec((tm, tn), lambda i, j: (i, j))],
        out_shape=[jax.ShapeDtypeStruct((m, D_MODEL), F32),
                   jax.ShapeDtypeStruct((m, D_MODEL), BF16)],
        scratch_shapes=[pltpu.VMEM((D_MODEL // tn, tm, tn), F32),
                        pltpu.VMEM((tm, D_MODEL), BF16)],
        compiler_params=pltpu.CompilerParams(
            dimension_semantics=("parallel", "arbitrary"), vmem_limit_bytes=VMEM_LIMIT),
        name="post_norm_ple",
    )(x, h, p_bf, g.reshape(1, D_MODEL), b.reshape(1, D_MODEL), wg_bf, wp_bf)


def rel_bucket(dist):
    n = jnp.maximum(dist, 0)
    exact = N_BUCKETS // 2
    logb = exact + (jnp.log(jnp.maximum(n, exact).astype(F32) / exact)
                    / math.log(REL_MAX_DIST / exact) * (N_BUCKETS - exact)).astype(jnp.int32)
    return jnp.where(n < exact, n, jnp.minimum(logb, N_BUCKETS - 1))


def split_tokens(t):
    c = t.shape[-1]
    return (t[:N_PROMPT_TOK].reshape(BATCH, SEQ, c), t[N_PROMPT_TOK:].reshape(DEC_BATCH, DEC_SEQ, c))


def join_tokens(tp, ts):
    c = tp.shape[-1]
    return jnp.concatenate([tp.reshape(N_PROMPT_TOK, c), ts.reshape(N_SAMPLE_TOK, c)], axis=0)


G_A = N_HEADS // KV_A
SWA_KEYS = 2 * WINDOW


def swa_tables(rel_bias, sinks, dec_len):
    def heads_to(b, lead):
        return jnp.moveaxis(b, -1, 0).reshape((KV_A, G_A) + lead)

    dist = jnp.arange(WINDOW)[None, :] - (jnp.arange(SWA_KEYS)[:, None] - WINDOW)
    ok = (dist >= 0) & (dist < WINDOW)
    b = jnp.where(ok[..., None], rel_bias[rel_bucket(dist)].astype(F32), NEG_BIG)
    bias_p = heads_to(b, (SWA_KEYS, WINDOW)).transpose(0, 2, 1, 3).reshape(KV_A, SWA_KEYS, G_A * WINDOW)
    sink_p = jnp.broadcast_to(sinks.astype(F32).reshape(KV_A, 1, G_A, 1), (KV_A, 1, G_A, WINDOW))
    sink_p = sink_p.reshape(KV_A, 1, G_A * WINDOW)
    key_i = jnp.arange(SWA_KEYS)[None, :]
    dist = jnp.arange(dec_len)[:, None] + WINDOW - key_i
    ok = (dist >= 0) & (dist < WINDOW) & (key_i < WINDOW + dec_len)
    b = jnp.where(ok[..., None], rel_bias[rel_bucket(dist)].astype(F32), NEG_BIG)
    bias_s = heads_to(b, (dec_len, SWA_KEYS)).reshape(KV_A, G_A * dec_len, SWA_KEYS)
    sink_s = jnp.broadcast_to(sinks.astype(F32).reshape(KV_A, G_A, 1, 1), (KV_A, G_A, dec_len, LANE))
    sink_s = sink_s.reshape(KV_A, G_A * dec_len, LANE)
    return bias_p, sink_p, bias_s, sink_s


def _swa_prompt_kernel(qT_ref, zT_ref, vTp_ref, vTc_ref, kp_ref, kc_ref, bias_ref, sink_ref, o_ref, *, cdt):
    first = pl.program_id(1) == 0
    kk = jnp.concatenate([kp_ref[...], kc_ref[...]], axis=0)
    vT = jnp.concatenate([vTp_ref[...], vTc_ref[...]], axis=1)
    prev_key = lax.broadcasted_iota(jnp.int32, (SWA_KEYS, G_A * WINDOW), 0) < WINDOW
    pw = 2 * HEAD_DIM
    for j in range(KV_A):
        kpair = kk[:, (j // 2) * pw:(j // 2 + 1) * pw].astype(cdt)
        qj = jnp.concatenate([qT_ref[(G_A * j + g) * HEAD_DIM:(G_A * j + g + 1) * HEAD_DIM, :]
                              for g in range(G_A)], axis=1)
        qj = (qj.astype(F32) * (HEAD_DIM ** -0.5)).astype(cdt)
        zpad = jnp.zeros_like(qj)
        rhs = jnp.concatenate([qj, zpad] if j % 2 == 0 else [zpad, qj], axis=0)
        s = jnp.dot(kpair, rhs, preferred_element_type=F32) + bias_ref[j]
        s = jnp.where(prev_key, jnp.where(first, NEG_BIG, s), s)
        sink = sink_ref[j]
        m = jnp.maximum(jnp.max(s, axis=0, keepdims=True), sink)
        e = jnp.exp(s - m)
        den = jnp.sum(e, axis=0, keepdims=True) + jnp.exp(sink - m)
        p = (e * (1.0 / den)).astype(cdt)
        acc = jnp.dot(vT[j * HEAD_DIM:(j + 1) * HEAD_DIM, :].astype(cdt), p, preferred_element_type=F32)
        for g in range(G_A):
            r0 = (G_A * j + g) * HEAD_DIM
            z = zT_ref[r0:r0 + HEAD_DIM, :]
            o_ref[r0:r0 + HEAD_DIM, :] = (acc[:, g * WINDOW:(g + 1) * WINDOW]
                                          * (z * (1.0 / (1.0 + jnp.exp(-z))))).astype(o_ref.dtype)


def swa_prompt(qT, vT, zT, kv_nat, bias_p, sink_p, *, n_batch, seq_len, cdt=BF16):
    nb = seq_len // WINDOW
    cur = lambda b, i: b * nb + i
    prev = lambda b, i: b * nb + jnp.maximum(i - 1, 0)
    v_row_blk = 0
    return pl.pallas_call(
        functools.partial(_swa_prompt_kernel, cdt=cdt),
        grid=(n_batch, nb),
        in_specs=[pl.BlockSpec((ATT_WIDTH, WINDOW), lambda b, i: (0, cur(b, i))),
                  pl.BlockSpec((ATT_WIDTH, WINDOW), lambda b, i: (0, cur(b, i))),
                  pl.BlockSpec((A_KV, WINDOW), lambda b, i: (v_row_blk, prev(b, i))),
                  pl.BlockSpec((A_KV, WINDOW), lambda b, i: (v_row_blk, cur(b, i))),
                  pl.BlockSpec((WINDOW, A_KV), lambda b, i: (prev(b, i), 0)),
                  pl.BlockSpec((WINDOW, A_KV), lambda b, i: (cur(b, i), 0)),
                  pl.BlockSpec(bias_p.shape, lambda b, i: (0, 0, 0)),
                  pl.BlockSpec(sink_p.shape, lambda b, i: (0, 0, 0))],
        out_specs=pl.BlockSpec((ATT_WIDTH, WINDOW), lambda b, i: (0, cur(b, i))),
        out_shape=jax.ShapeDtypeStruct((ATT_WIDTH, n_batch * seq_len), BF16),
        compiler_params=pltpu.CompilerParams(
            dimension_semantics=("parallel", "parallel"), vmem_limit_bytes=VMEM_LIMIT),
        name="swa_prompt",
    )(qT, zT, vT, vT, kv_nat, kv_nat, bias_p, sink_p)


def _swa_sample_kernel(q_ref, z_ref, k_ref, v_ref, bias_ref, sink_ref, o_ref, *, cdt):
    for j in range(KV_A):
        s = lax.dot_general(q_ref[0, j], k_ref[0, j].astype(cdt), (((1,), (1,)), ((), ())),
                            preferred_element_type=F32) + bias_ref[j]
        sink = sink_ref[j][:, 0:1]
        m = jnp.maximum(jnp.max(s, axis=1, keepdims=True), sink)
        e = jnp.exp(s - m)
        den = jnp.sum(e, axis=1, keepdims=True) + jnp.exp(sink - m)
        p = (e * (1.0 / den)).astype(cdt)
        z = z_ref[0, j]
        o_ref[0, j] = jnp.dot(p, v_ref[0, j].astype(cdt), preferred_element_type=F32) * (z * (1.0 / (1.0 + jnp.exp(-z))))


def swa_sample(q_s, z_s, kv_s, kv_cache, bias_s, sink_s, cdt=BF16):
    n_dec = kv_cache.shape[0]
    dec_len = q_s.shape[0] // n_dec
    rows = G_A * dec_len

    def head_rows(t, scale):
        t = t.reshape(n_dec, dec_len, KV_A, G_A, HEAD_DIM).transpose(0, 2, 3, 1, 4) * scale
        return jnp.pad(t.reshape(n_dec, KV_A, rows, HEAD_DIM), ((0, 0), (0, 0), (0, 0), (0, HEAD_DIM)))

    new = kv_s.reshape(n_dec, dec_len, 2, KV_A, HEAD_DIM)
    cat = jnp.concatenate([kv_cache, new], axis=1)
    keys = jnp.pad(cat.transpose(2, 0, 3, 1, 4),
                   ((0, 0), (0, 0), (0, 0), (0, SWA_KEYS - WINDOW - dec_len), (0, HEAD_DIM)))
    blk = lambda r: pl.BlockSpec((1, KV_A, r, 2 * HEAD_DIM), lambda b: (b, 0, 0, 0))
    o = pl.pallas_call(
        functools.partial(_swa_sample_kernel, cdt=cdt),
        grid=(n_dec,),
        in_specs=[blk(rows), blk(rows), blk(SWA_KEYS), blk(SWA_KEYS),
                  pl.BlockSpec(bias_s.shape, lambda b: (0, 0, 0)),
                  pl.BlockSpec(sink_s.shape, lambda b: (0, 0, 0))],
        out_specs=blk(rows),
        out_shape=jax.ShapeDtypeStruct((n_dec, KV_A, rows, 2 * HEAD_DIM), F32),
        compiler_params=pltpu.CompilerParams(dimension_semantics=("parallel",), vmem_limit_bytes=VMEM_LIMIT),
        name="swa_sample",
    )(head_rows(q_s, HEAD_DIM ** -0.5).astype(cdt), head_rows(z_s, 1.0), keys[0], keys[1], bias_s, sink_s)
    o = o[..., :HEAD_DIM].reshape(n_dec, KV_A, G_A, dec_len, HEAD_DIM).transpose(0, 3, 1, 2, 4)
    return o.reshape(n_dec * dec_len, ATT_WIDTH), cat[:, dec_len:]


S5_SLAB_G = 8
S5_SLAB_CH = S5_SLAB_G * S5_GROUP
S5_SLAB_ST = S5_SLAB_G * S5_STATE
S5_N_SLABS = S5_GROUPS // S5_SLAB_G
S5_CHAINS = 8
S5_HALF_CH = S5_CHAINS * S5_SLAB_CH
S5_T = 256
S5_LT = 2 * S5_SLAB_ST // LANE


def _gelu_tanh(x):
    return 0.5 * x * (1.0 + jnp.tanh(math.sqrt(2.0 / math.pi) * (x + 0.044715 * (x * x * x))))


def s5_tables(a_re, a_im, b_re, b_im, c_re, c_im, log_dt):
    a = lax.complex(a_re, a_im)
    dt = jnp.exp(log_dt)[:, None]
    a_bar = jnp.exp(a * dt)
    b_bar = ((a_bar - 1.0) / a)[..., None] * lax.complex(b_re, b_im)
    eye = jnp.eye(S5_SLAB_G, dtype=F32)

    def b_blk(t):
        t = t.reshape(S5_N_SLABS, S5_SLAB_G, S5_STATE, S5_GROUP)
        return jnp.einsum('ij,sipc->sicjp', eye, t).reshape(S5_N_SLABS, S5_SLAB_CH, S5_SLAB_ST)

    def c_blk(t):
        t = t.reshape(S5_N_SLABS, S5_SLAB_G, S5_GROUP, S5_STATE)
        return jnp.einsum('ij,sicp->sjpic', eye, t).reshape(S5_N_SLABS, S5_SLAB_ST, S5_SLAB_CH)

    bcat = jnp.concatenate([b_blk(b_bar.real), b_blk(b_bar.imag)], axis=2)
    ccat = jnp.concatenate([c_blk(c_re), -c_blk(c_im)], axis=1)
    a_cat = jnp.concatenate([a_bar.real.reshape(S5_N_SLABS, S5_SLAB_ST),
                             a_bar.imag.reshape(S5_N_SLABS, S5_SLAB_ST)], axis=1)
    return a_cat, bcat, ccat


def _s5_prompt_kernel(u_ref, bcat_ref, ccat_ref, a_ref, d_ref, y_ref, hout_ref, sc, h_sc, *, cdt):
    tc = pl.program_id(2)
    n_lt_half = S5_LT // 2

    @pl.when(tc == 0)
    def _():
        h_sc[...] = jnp.zeros_like(h_sc)

    for j in range(S5_CHAINS):
        uj = u_ref[:, j * S5_SLAB_CH:(j + 1) * S5_SLAB_CH].astype(cdt)
        bu = jnp.dot(uj, bcat_ref[0, j], preferred_element_type=F32)
        for lt in range(S5_LT):
            sc[lt, pl.ds(j, S5_T, stride=S5_CHAINS), :] = bu[:, lt * LANE:(lt + 1) * LANE]

    a_re = [a_ref[0, :, lt * LANE:(lt + 1) * LANE] for lt in range(n_lt_half)]
    a_im = [a_ref[0, :, (n_lt_half + lt) * LANE:(n_lt_half + lt + 1) * LANE] for lt in range(n_lt_half)]

    def step(t, h):
        r0 = pl.multiple_of(t * S5_CHAINS, S5_CHAINS)
        new = list(h)
        for lt in range(n_lt_half):
            hr, hi = h[lt], h[n_lt_half + lt]
            nr = a_re[lt] * hr - a_im[lt] * hi + sc[lt, pl.ds(r0, S5_CHAINS), :]
            ni = a_re[lt] * hi + a_im[lt] * hr + sc[n_lt_half + lt, pl.ds(r0, S5_CHAINS), :]
            sc[lt, pl.ds(r0, S5_CHAINS), :] = nr
            sc[n_lt_half + lt, pl.ds(r0, S5_CHAINS), :] = ni
            new[lt], new[n_lt_half + lt] = nr, ni
        return tuple(new)

    h = lax.fori_loop(0, S5_T, step, tuple(h_sc[lt] for lt in range(S5_LT)), unroll=8)
    for lt in range(S5_LT):
        h_sc[lt] = h[lt]
        hout_ref[0, 0, :, lt * LANE:(lt + 1) * LANE] = h[lt]

    for j in range(S5_CHAINS):
        hcat = jnp.concatenate([sc[lt, pl.ds(j, S5_T, stride=S5_CHAINS), :] for lt in range(S5_LT)], axis=1)
        cols = slice(j * S5_SLAB_CH, (j + 1) * S5_SLAB_CH)
        y = jnp.dot(hcat.astype(cdt), ccat_ref[0, j], preferred_element_type=F32) + d_ref[0, :, cols] * u_ref[:, cols]
        y_ref[:, cols] = _gelu_tanh(y)


def s5_prompt(proj, a_cat, bcat, ccat, d_skip, *, n_batch, seq_len, n_rows_out, cdt=BF16):
    n_t = seq_len // S5_T
    n_half = S5_WIDTH // S5_HALF_CH
    half = lambda t: t.reshape((n_half, S5_CHAINS) + t.shape[1:])
    return pl.pallas_call(
        functools.partial(_s5_prompt_kernel, cdt=cdt),
        grid=(n_batch, n_half, n_t),
        in_specs=[pl.BlockSpec((S5_T, S5_HALF_CH), lambda b, hf, t: (b * n_t + t, hf)),
                  pl.BlockSpec((1, S5_CHAINS, S5_SLAB_CH, 2 * S5_SLAB_ST), lambda b, hf, t: (hf, 0, 0, 0)),
                  pl.BlockSpec((1, S5_CHAINS, 2 * S5_SLAB_ST, S5_SLAB_CH), lambda b, hf, t: (hf, 0, 0, 0)),
                  pl.BlockSpec((1, S5_CHAINS, 2 * S5_SLAB_ST), lambda b, hf, t: (hf, 0, 0)),
                  pl.BlockSpec((1, 1, S5_HALF_CH), lambda b, hf, t: (hf, 0, 0))],
        out_specs=[pl.BlockSpec((S5_T, S5_HALF_CH), lambda b, hf, t: (b * n_t + t, hf)),
                   pl.BlockSpec((1, 1, S5_CHAINS, 2 * S5_SLAB_ST), lambda b, hf, t: (b, hf, 0, 0))],
        out_shape=[jax.ShapeDtypeStruct((n_rows_out, S5_WIDTH), F32),
                   jax.ShapeDtypeStruct((n_batch, n_half, S5_CHAINS, 2 * S5_SLAB_ST), F32)],
        scratch_shapes=[pltpu.VMEM((S5_LT, S5_T * S5_CHAINS, LANE), F32),
                        pltpu.VMEM((S5_LT, S5_CHAINS, LANE), F32)],
        compiler_params=pltpu.CompilerParams(
            dimension_semantics=("parallel", "parallel", "arbitrary"), vmem_limit_bytes=VMEM_LIMIT),
        name="s5_prompt",
    )(proj, half(bcat.astype(cdt)), half(ccat.astype(cdt)), half(a_cat), d_skip.reshape(n_half, 1, S5_HALF_CH))


def _s5_sample_kernel(u_ref, bcat_ref, ccat_ref, a_ref, d_ref, h0_ref, y_ref, hout_ref, sc, *, n_b, n_t, cdt):
    u = u_ref[...]
    bu = jnp.dot(u.astype(cdt), bcat_ref[0], preferred_element_type=F32)
    a_re = a_ref[0, :, :S5_SLAB_ST]
    a_im = a_ref[0, :, S5_SLAB_ST:]
    for bg in range(n_b // 8):
        hr = h0_ref[0, bg * 8:(bg + 1) * 8, :S5_SLAB_ST]
        hi = h0_ref[0, bg * 8:(bg + 1) * 8, S5_SLAB_ST:]
        for t in range(n_t):
            r = t * n_b + bg * 8
            hr, hi = (a_re * hr - a_im * hi + bu[r:r + 8, :S5_SLAB_ST],
                      a_re * hi + a_im * hr + bu[r:r + 8, S5_SLAB_ST:])
            sc[r:r + 8, :S5_SLAB_ST] = hr
            sc[r:r + 8, S5_SLAB_ST:] = hi
        hout_ref[0, bg * 8:(bg + 1) * 8, :S5_SLAB_ST] = hr
        hout_ref[0, bg * 8:(bg + 1) * 8, S5_SLAB_ST:] = hi
    y = jnp.dot(sc[...].astype(cdt), ccat_ref[0], preferred_element_type=F32) + d_ref[0] * u
    y_ref[...] = _gelu_tanh(y)


def s5_sample(u_tb, a_cat, bcat, ccat, d_skip, h0_cat, *, n_b, n_t, cdt=BF16):
    rows = n_t * n_b
    return pl.pallas_call(
        functools.partial(_s5_sample_kernel, n_b=n_b, n_t=n_t, cdt=cdt),
        grid=(S5_N_SLABS,),
        in_specs=[pl.BlockSpec((rows, S5_SLAB_CH), lambda s: (0, s)),
                  pl.BlockSpec((1, S5_SLAB_CH, 2 * S5_SLAB_ST), lambda s: (s, 0, 0)),
                  pl.BlockSpec((1, 2 * S5_SLAB_ST, S5_SLAB_CH), lambda s: (s, 0, 0)),
                  pl.BlockSpec((1, 1, 2 * S5_SLAB_ST), lambda s: (s, 0, 0)),
                  pl.BlockSpec((1, 1, S5_SLAB_CH), lambda s: (s, 0, 0)),
                  pl.BlockSpec((1, n_b, 2 * S5_SLAB_ST), lambda s: (s, 0, 0))],
        out_specs=[pl.BlockSpec((rows, S5_SLAB_CH), lambda s: (0, s)),
                   pl.BlockSpec((1, n_b, 2 * S5_SLAB_ST), lambda s: (s, 0, 0))],
        out_shape=[jax.ShapeDtypeStruct((rows, S5_WIDTH), F32),
                   jax.ShapeDtypeStruct((S5_N_SLABS, n_b, 2 * S5_SLAB_ST), F32)],
        scratch_shapes=[pltpu.VMEM((rows, 2 * S5_SLAB_ST), F32)],
        compiler_params=pltpu.CompilerParams(
            dimension_semantics=("arbitrary",), vmem_limit_bytes=VMEM_LIMIT),
        name="s5_sample",
    )(u_tb, bcat.astype(cdt), ccat.astype(cdt), a_cat.reshape(S5_N_SLABS, 1, 2 * S5_SLAB_ST),
      d_skip.reshape(S5_N_SLABS, 1, S5_SLAB_CH), h0_cat)


GLU_TM = 320
GLU_TN = 512


def _glu_kernel(yfull_ref, w_ref, ycol_ref, z_ref, o_ref, ybf_sc):
    @pl.when(pl.program_id(1) == 0)
    def _():
        ybf_sc[...] = yfull_ref[...].astype(ybf_sc.dtype)

    glu = jnp.dot(ybf_sc[...], w_ref[...].astype(ybf_sc.dtype), preferred_element_type=F32)
    z = z_ref[...]
    y = ycol_ref[...]
    o_ref[...] = (y * (1.0 / (1.0 + jnp.exp(-glu))) * (z * (1.0 / (1.0 + jnp.exp(-z))))).astype(o_ref.dtype)


def s5_glu_gate(y, w_glu, proj, row_off, cdt=BF16):
    m = y.shape[0]
    tm, tn = _pick_tile(m, (256, 128, 64, 32, 16)), GLU_TN
    assert row_off % tm == 0
    z_off, r_off = S5_WIDTH // tn, row_off // tm
    return pl.pallas_call(
        _glu_kernel,
        grid=(m // tm, S5_WIDTH // tn),
        in_specs=[pl.BlockSpec((tm, S5_WIDTH), lambda i, j: (i, 0)),
                  pl.BlockSpec((S5_WIDTH, tn), lambda i, j: (0, j)),
                  pl.BlockSpec((tm, tn), lambda i, j: (i, j)),
                  pl.BlockSpec((tm, tn), lambda i, j: (r_off + i, z_off + j))],
        out_specs=pl.BlockSpec((tm, tn), lambda i, j: (i, j)),
        out_shape=jax.ShapeDtypeStruct((m, S5_WIDTH), BF16),
        scratch_shapes=[pltpu.VMEM((tm, S5_WIDTH), cdt)],
        compiler_params=pltpu.CompilerParams(
            dimension_semantics=("parallel", "arbitrary"), vmem_limit_bytes=VMEM_LIMIT),
        name="s5_glu_gate",
    )(y, w_glu, y, proj)


def s5_layer(proj, state_in, tables, d_skip, w_glu, *, n_batch, seq_len, n_dec, dec_len, cdt=BF16):
    a_cat, bcat, ccat = tables
    n_p = n_batch * seq_len
    n_s = n_dec * dec_len
    y_p, h_p = s5_prompt(proj, a_cat, bcat, ccat, d_skip, n_batch=n_batch, seq_len=seq_len, n_rows_out=n_p, cdt=cdt)
    u_tb = jnp.swapaxes(proj[n_p:, :S5_WIDTH].reshape(n_dec, dec_len, S5_WIDTH), 0, 1).reshape(n_s, S5_WIDTH)
    h0 = state_in.reshape(n_dec, S5_N_SLABS, S5_SLAB_ST, 2)
    h0_cat = jnp.concatenate([jnp.swapaxes(h0[..., 0], 0, 1), jnp.swapaxes(h0[..., 1], 0, 1)], axis=-1)
    y_tb, h_s = s5_sample(u_tb, a_cat, bcat, ccat, d_skip, h0_cat, n_b=n_dec, n_t=dec_len, cdt=cdt)
    y_s = jnp.swapaxes(y_tb.reshape(dec_len, n_dec, S5_WIDTH), 0, 1).reshape(n_s, S5_WIDTH)
    gated_p = s5_glu_gate(y_p, w_glu, proj, 0, cdt=cdt)
    gated_s = s5_glu_gate(y_s, w_glu, proj, n_p, cdt=cdt)
    hp = h_p.reshape(n_batch, S5_N_SLABS, 2, S5_SLAB_ST)
    st_p = jnp.stack([hp[:, :, 0], hp[:, :, 1]], axis=-1).reshape(n_batch, S5_GROUPS, S5_STATE, 2)
    hs = jnp.swapaxes(h_s, 0, 1).reshape(n_dec, S5_N_SLABS, 2, S5_SLAB_ST)
    st_s = jnp.stack([hs[:, :, 0], hs[:, :, 1]], axis=-1).reshape(n_dec, S5_GROUPS, S5_STATE, 2)
    return gated_p, gated_s, st_p, st_s


GDN_CONV_TT = 512
GDN_CONV_CW = 1024
GDN_HIST = 8
GDN_HB = 16
GDN_TT = 256
GDN_SAMPLE_ROWS = 8


def _gdn_conv_kernel(x_ref, hist_ref, w_ref, o_ref, ext_sc, *, rows, n_t, zero_first, cw):
    i, j = pl.program_id(0), pl.program_id(1)
    hist = hist_ref[...]
    if zero_first:
        hist = jnp.where(i % n_t == 0, 0.0, hist)
    ext_sc[0:GDN_HIST] = hist
    ext_sc[GDN_HIST:GDN_HIST + rows] = x_ref[...]
    acc = x_ref[...] * w_ref[GDN_CONV - 1:GDN_CONV, :]
    for s in range(1, GDN_CONV):
        acc = acc + ext_sc[GDN_HIST - s:GDN_HIST - s + rows] * w_ref[GDN_CONV - 1 - s:GDN_CONV - s, :]
    conv = acc * (1.0 / (1.0 + jnp.exp(-acc)))

    def normalised(h, scale):
        t = conv[:, h * GDN_DK:(h + 1) * GDN_DK]
        return t * lax.rsqrt(jnp.sum(t * t, axis=-1, keepdims=True) + 1e-6) * scale

    if cw == GDN_CONV_CH:
        for h in range(2 * GDN_QK_HEADS):
            o_ref[:, h * GDN_DK:(h + 1) * GDN_DK] = normalised(h, GDN_DK ** -0.5 if h < GDN_QK_HEADS else 1.0)
        o_ref[:, 2 * GDN_QK_WIDTH:] = conv[:, 2 * GDN_QK_WIDTH:]
    else:
        n_qk_blocks = 2 * GDN_QK_WIDTH // cw

        @pl.when(j >= n_qk_blocks)
        def _():
            o_ref[...] = conv

        @pl.when(j < n_qk_blocks)
        def _():
            scale = jnp.where(j < GDN_QK_WIDTH // cw, GDN_DK ** -0.5, 1.0)
            for h in range(cw // GDN_DK):
                o_ref[:, h * GDN_DK:(h + 1) * GDN_DK] = normalised(h, scale)


def gdn_conv(x, hist_src, conv_w, *, rows, n_blocks, n_t, data_map, hist_map, zero_first, cw=GDN_CONV_CW):
    n_out = n_blocks * rows
    return pl.pallas_call(
        functools.partial(_gdn_conv_kernel, rows=rows, n_t=n_t, zero_first=zero_first, cw=cw),
        grid=(n_blocks, GDN_CONV_CH // cw),
        in_specs=[pl.BlockSpec((rows, cw), lambda i, j: (data_map(i), j)),
                  pl.BlockSpec((GDN_HIST, cw), lambda i, j: (hist_map(i), j)),
                  pl.BlockSpec((GDN_CONV, cw), lambda i, j: (0, j))],
        out_specs=pl.BlockSpec((rows, cw), lambda i, j: (i, j)),
        out_shape=jax.ShapeDtypeStruct((n_out, GDN_CONV_CH), F32),
        scratch_shapes=[pltpu.VMEM((GDN_HIST + rows, cw), F32)],
        compiler_params=pltpu.CompilerParams(
            dimension_semantics=("parallel", "parallel"), vmem_limit_bytes=VMEM_LIMIT),
        name="gdn_conv",
    )(x, hist_src, conv_w)


def _gdn_chunk_lockstep_kernel(q_ref, k_ref, v_ref, z_ref, ab_ref, alog_ref, dtb_ref, nw_ref, s0_ref, o_ref,
                               sout_ref, s_sc, *, chunk, n_inner, n_tt, valid_len):
    C = chunk
    hb, tt = pl.program_id(1), pl.program_id(2)

    @pl.when(tt == 0)
    def _():
        s_sc[...] = s0_ref[0]

    rowi = lax.broadcasted_iota(jnp.int32, (C, C), 0)
    coli = lax.broadcasted_iota(jnp.int32, (C, C), 1)
    causal = rowi >= coli
    strict = rowi > coli
    ltri = jnp.where(causal, 1.0, 0.0)
    utri = jnp.where(rowi <= coli, 1.0, 0.0)
    eye = jnp.where(rowi == coli, 1.0, 0.0)
    hi = lax.Precision.HIGHEST
    shift = (LANE - hb * GDN_HB) % LANE
    alog = pltpu.roll(jnp.broadcast_to(alog_ref[...], (8, LANE)), shift, 1)[0:1]
    dtb = pltpu.roll(jnp.broadcast_to(dtb_ref[...], (8, LANE)), shift, 1)[0:1]
    nw = nw_ref[...]
    tok_valid = lax.broadcasted_iota(jnp.int32, (C, LANE), 0) < valid_len
    dot = functools.partial(jnp.dot, preferred_element_type=F32)
    dot_nt = lambda a, b: lax.dot_general(a, b, (((1,), (1,)), ((), ())), preferred_element_type=F32)
    dot_tn = lambda a, b: lax.dot_general(a, b, (((0,), (0,)), ((), ())), preferred_element_type=F32)
    units = [(c, i) for c in range(n_inner) for i in range(GDN_HB)]
    rows = lambda c: slice(c * C, (c + 1) * C)
    qk_cols = lambda i: slice((i // 2) * GDN_DK, (i // 2 + 1) * GDN_DK)
    v_cols = lambda i: slice(i * GDN_DV, (i + 1) * GDN_DV)

    g_all, beta_all = [], []
    for c in range(n_inner):
        ab = pltpu.roll(ab_ref[rows(c), :], shift, 1)
        xa = ab + dtb
        softplus = jnp.maximum(xa, 0.0) + jnp.log1p(jnp.exp(-jnp.abs(xa)))
        g_all.append(jnp.where(tok_valid, -jnp.exp(alog) * softplus, 0.0))
        beta_all.append(jnp.where(tok_valid, 1.0 / (1.0 + jnp.exp(-ab)), 0.0))
    gam_all = [jnp.dot(ltri, g, preferred_element_type=F32, precision=hi) for g in g_all]
    gamT_all = [lax.dot_general(g, utri, (((0,), (0,)), ((), ())), preferred_element_type=F32, precision=hi)
                for g in g_all]

    qkk = [dot_nt(jnp.concatenate([q_ref[rows(c), qk_cols(i)], k_ref[rows(c), qk_cols(i)]], axis=0).astype(BF16),
                  k_ref[rows(c), qk_cols(i)].astype(BF16)) for c, i in units]
    gam_c = [jnp.broadcast_to(gam_all[c][:, i:i + 1], (C, LANE)) for c, i in units]
    beta_c = [jnp.broadcast_to(beta_all[c][:, 32 + i:33 + i], (C, LANE)) for c, i in units]
    gam_last = [jnp.broadcast_to(gam_all[c][C - 1:C, i:i + 1], (1, LANE)) for c, i in units]
    decay = [jnp.where(causal, jnp.exp(jnp.where(causal, gc[:, :C] - jnp.broadcast_to(gamT_all[c][i:i + 1, :], (C, C)),
                                                 0.0)), 0.0) for gc, (c, i) in zip(gam_c, units)]
    qk = [(x[:C] * d).astype(BF16) for x, d in zip(qkk, decay)]
    neg_a = [jnp.where(strict, -(b[:, :C] * x[C:] * d), 0.0) for b, x, d in zip(beta_c, qkk, decay)]
    p_inv = [eye + n for n in neg_a]
    m_pow = neg_a
    for _ in range(int(math.log2(C)) - 1):
        m_pow = [dot(m.astype(BF16), m.astype(BF16)) for m in m_pow]
        p_inv = [p + dot(p.astype(BF16), m.astype(BF16)) for p, m in zip(p_inv, m_pow)]
    eg = [jnp.exp(gc) for gc in gam_c]
    sol = [dot(p.astype(BF16), jnp.concatenate([b * v_ref[rows(c), v_cols(i)], (b * e) * k_ref[rows(c), qk_cols(i)]],
                                               axis=1).astype(BF16))
           for p, b, e, (c, i) in zip(p_inv, beta_c, eg, units)]
    wq = [jnp.concatenate([s[:, GDN_DV:], q_ref[rows(c), qk_cols(i)] * e], axis=0).astype(BF16)
          for s, e, (c, i) in zip(sol, eg, units)]
    k_dec = [(k_ref[rows(c), qk_cols(i)] * jnp.exp(gl - gc)).astype(BF16)
             for gl, gc, (c, i) in zip(gam_last, gam_c, units)]

    state = [s_sc[i] for i in range(GDN_HB)]
    for c in range(n_inner):
        base = c * GDN_HB
        ws = [dot(wq[base + i], state[i].astype(BF16)) for i in range(GDN_HB)]
        v_new = [(sol[base + i][:, :GDN_DV] - ws[i][:C]).astype(BF16) for i in range(GDN_HB)]
        o = [ws[i][C:] + dot(qk[base + i], v_new[i]) for i in range(GDN_HB)]
        state = [state[i] * jnp.exp(gam_last[base + i]) + dot_tn(k_dec[base + i], v_new[i]) for i in range(GDN_HB)]
        for i in range(GDN_HB):
            rms = lax.rsqrt(jnp.mean(o[i] * o[i], axis=-1, keepdims=True) + 1e-6)
            zz = z_ref[rows(c), v_cols(i)]
            o_ref[rows(c), v_cols(i)] = (o[i] * rms * nw * (zz * (1.0 / (1.0 + jnp.exp(-zz))))).astype(o_ref.dtype)
    for i in range(GDN_HB):
        s_sc[i] = state[i]

    @pl.when(tt == n_tt - 1)
    def _():
        sout_ref[0] = s_sc[...]


def gdn_chunk(conv, z, ab, a_log, dt_bias, norm_w, s0, *, n_seq, rows_per_seq, rows_per_step, chunk, valid_len,
              z_col_off, out_dtype):
    n_tt = rows_per_seq // rows_per_step
    n_inner = rows_per_step // chunk
    n_hb = GDN_V_HEADS // GDN_HB
    qw, vw = GDN_HB // 2 * GDN_DK, GDN_HB * GDN_DV
    k_off, v_off, z_off = GDN_QK_WIDTH // qw, 2 * GDN_QK_WIDTH // vw, z_col_off // vw
    row = lambda b, hb, t: b * n_tt + t
    pad_row = lambda p: jnp.pad(p.astype(F32), (0, LANE - p.shape[0])).reshape(1, LANE)
    return pl.pallas_call(
        functools.partial(_gdn_chunk_lockstep_kernel, chunk=chunk, n_inner=n_inner, n_tt=n_tt, valid_len=valid_len),
        grid=(n_seq, n_hb, n_tt),
        in_specs=[pl.BlockSpec((rows_per_step, qw), lambda b, hb, t: (row(b, hb, t), hb)),
                  pl.BlockSpec((rows_per_step, qw), lambda b, hb, t: (row(b, hb, t), k_off + hb)),
                  pl.BlockSpec((rows_per_step, vw), lambda b, hb, t: (row(b, hb, t), v_off + hb)),
                  pl.BlockSpec((rows_per_step, vw), lambda b, hb, t: (row(b, hb, t), z_off + hb)),
                  pl.BlockSpec((rows_per_step, LANE), lambda b, hb, t: (row(b, hb, t), 0)),
                  pl.BlockSpec((1, LANE), lambda b, hb, t: (0, 0)),
                  pl.BlockSpec((1, LANE), lambda b, hb, t: (0, 0)),
                  pl.BlockSpec((1, GDN_DV), lambda b, hb, t: (0, 0)),
                  pl.BlockSpec((1, GDN_HB, GDN_DK, GDN_DV), lambda b, hb, t: (b, hb, 0, 0))],
        out_specs=[pl.BlockSpec((rows_per_step, vw), lambda b, hb, t: (row(b, hb, t), hb)),
                   pl.BlockSpec((1, GDN_HB, GDN_DK, GDN_DV), lambda b, hb, t: (b, hb, 0, 0))],
        out_shape=[jax.ShapeDtypeStruct((n_seq * rows_per_seq, GDN_V_WIDTH), out_dtype),
                   jax.ShapeDtypeStruct((n_seq, GDN_V_HEADS, GDN_DK, GDN_DV), F32)],
        scratch_shapes=[pltpu.VMEM((GDN_HB, GDN_DK, GDN_DV), F32)],
        compiler_params=pltpu.CompilerParams(
            dimension_semantics=("parallel", "parallel", "arbitrary"), vmem_limit_bytes=VMEM_LIMIT),
        name="gdn_chunk",
    )(conv, conv, conv, z, ab, pad_row(a_log), pad_row(dt_bias), norm_w.astype(F32).reshape(1, GDN_DV), s0)


def gdn_layer(qkvz, ab, state_in, conv_in, conv_w, a_log, dt_bias, norm_w, *, n_batch, seq_len, n_dec, dec_len):
    n_p = n_batch * seq_len
    n_tp = seq_len // GDN_CONV_TT
    hist_per_block = GDN_CONV_TT // GDN_HIST
    conv_p = gdn_conv(qkvz, qkvz, conv_w, rows=GDN_CONV_TT, n_blocks=n_batch * n_tp, n_t=n_tp,
                      data_map=lambda i: i, hist_map=lambda i: jnp.maximum(i * hist_per_block - 1, 0),
                      zero_first=True)
    zeros_s = jnp.zeros((n_batch, GDN_V_HEADS, GDN_DK, GDN_DV), F32)
    gated_p, st_p = gdn_chunk(conv_p, qkvz, ab, a_log, dt_bias, norm_w, zeros_s, n_seq=n_batch,
                              rows_per_seq=seq_len, rows_per_step=GDN_TT, chunk=GDN_CHUNK, valid_len=GDN_CHUNK,
                              z_col_off=GDN_CONV_CH, out_dtype=BF16)
    buf_p = jnp.stack([lax.slice(qkvz, ((b + 1) * seq_len - (GDN_CONV - 1), 0), ((b + 1) * seq_len, GDN_CONV_CH))
                       for b in range(n_batch)])
    R = GDN_SAMPLE_ROWS
    x_s = qkvz[n_p:].reshape(n_dec, dec_len, -1)
    pad_t = lambda t, front: jnp.pad(t, ((0, 0), (front, R - front - t.shape[1]), (0, 0)))
    ext = jnp.concatenate([pad_t(conv_in, R - (GDN_CONV - 1)), pad_t(x_s[..., :GDN_CONV_CH], 0)], axis=1)
    ext = ext.reshape(n_dec * 2 * R, GDN_CONV_CH)
    conv_s = gdn_conv(ext, ext, conv_w, rows=R, n_blocks=n_dec, n_t=1,
                      data_map=lambda i: 2 * i + 1, hist_map=lambda i: 2 * i, zero_first=False, cw=GDN_CONV_CH)
    z_s = pad_t(x_s[..., GDN_CONV_CH:], 0).reshape(n_dec * R, GDN_V_WIDTH)
    ab_s = pad_t(ab[n_p:].reshape(n_dec, dec_len, LANE), 0).reshape(n_dec * R, LANE)
    gated_s, st_s = gdn_chunk(conv_s, z_s, ab_s, a_log, dt_bias, norm_w, state_in, n_seq=n_dec, rows_per_seq=R,
                              rows_per_step=R, chunk=R, valid_len=dec_len, z_col_off=0, out_dtype=F32)
    gated_s = gated_s.reshape(n_dec, R, GDN_V_WIDTH)[:, :dec_len].reshape(n_dec * dec_len, GDN_V_WIDTH)
    buf_s = jnp.concatenate([conv_in, x_s[..., :GDN_CONV_CH]], axis=1)[:, dec_len:]
    return gated_p, gated_s, st_p, st_s, buf_p, buf_s


DSA_KC = 256
INT_MIN = -2 ** 31
NEG_BIG = -1e30
G_D = N_HEADS // KV_D
BIAS_WIN = DSA_KC + Q_BLOCK


def _sortable_key(s):
    b = pltpu.bitcast(s, jnp.int32)
    return jnp.where(b < 0, b ^ jnp.int32(0x7FFFFFFF), b)


def _dsa_prompt_kernel(qT_ref, qiT_ref, wiT_ref, zT_ref, ki_ref, k_ref, vT_ref, win_ref, o_ref,
                       key_sc, mask_sc, *, topk, idx_bits, cdt):
    qb = pl.program_id(1)
    t0 = qb * Q_BLOCK
    nch = (qb + 2) // 2
    t_idx = t0 + lax.broadcasted_iota(jnp.int32, (1, Q_BLOCK), 1)
    row_iota = lax.broadcasted_iota(jnp.int32, (DSA_KC, Q_BLOCK), 0)

    def score_chunk(c, carry):
        kic = ki_ref[0, c].astype(cdt)
        acc = jnp.zeros((DSA_KC, Q_BLOCK), F32)
        for hp in range(IDX_HEADS // 2):
            rhs = jnp.concatenate([qiT_ref[(2 * hp) * IDX_DIM:(2 * hp + 1) * IDX_DIM, :],
                                   qiT_ref[(2 * hp + 1) * IDX_DIM:(2 * hp + 2) * IDX_DIM, :]], axis=1)
            s = jnp.dot(kic, rhs, preferred_element_type=F32) * (IDX_DIM ** -0.5)
            s = jnp.maximum(s, 0.0)
            w0 = wiT_ref[2 * hp:2 * hp + 1, :] * (IDX_HEADS ** -0.5)
            w1 = wiT_ref[2 * hp + 1:2 * hp + 2, :] * (IDX_HEADS ** -0.5)
            acc = acc + s[:, :Q_BLOCK] * w0 + s[:, Q_BLOCK:] * w1
        s_idx = c * DSA_KC + row_iota
        key_sc[c] = jnp.where(s_idx <= t_idx, _sortable_key(acc), INT_MIN)
        return carry

    lax.fori_loop(0, nch, score_chunk, 0)

    def count(pred):
        def body(c, acc):
            hit = pred(key_sc[c], c * DSA_KC + row_iota)
            return acc + hit.reshape(DSA_KC // 8, 8, Q_BLOCK).sum(axis=0)
        acc = lax.fori_loop(0, nch, body, jnp.zeros((8, Q_BLOCK), jnp.int32))
        return jnp.sum(acc, axis=0, keepdims=True)

    c_nonneg = count(lambda k, s: jnp.where(k >= 0, 1, 0))
    thr = jnp.where(c_nonneg >= topk, 0, INT_MIN).astype(jnp.int32)

    def thr_bit(i, thr):
        cand = thr + jnp.left_shift(jnp.int32(1), 30 - i)
        return jnp.where(count(lambda k, s: jnp.where(k >= cand, 1, 0)) >= topk, cand, thr)

    thr = lax.fori_loop(0, 31, thr_bit, thr)
    need = topk - count(lambda k, s: jnp.where(k > thr, 1, 0))

    def lim_bit(i, lim):
        cand = lim + jnp.left_shift(jnp.int32(1), idx_bits - 1 - i)
        c = count(lambda k, s: jnp.where(k == thr, jnp.where(s < cand, 1, 0), 0))
        return jnp.where(c <= need, cand, lim)

    n_ties = count(lambda k, s: jnp.where(k == thr, 1, 0))
    settled = jnp.min(jnp.where(thr == INT_MIN, 1, jnp.where(n_ties == need, 1, 0))) == 1
    lim = lax.cond(settled,
                   lambda: jnp.full((1, Q_BLOCK), 1 << idx_bits, jnp.int32),
                   lambda: lax.fori_loop(0, idx_bits, lim_bit, jnp.zeros((1, Q_BLOCK), jnp.int32)))

    def mask_chunk(c, carry):
        k = key_sc[c]
        s_idx = c * DSA_KC + row_iota
        tie = jnp.where(k == thr, jnp.where(s_idx < lim, 0.0, NEG_BIG), NEG_BIG)
        m = jnp.where(k > thr, 0.0, tie)
        mask_sc[c] = jnp.where(k == INT_MIN, NEG_BIG, m)
        return carry

    lax.fori_loop(0, nch, mask_chunk, 0)

    n_cols = G_D * Q_BLOCK
    half = DSA_KC // 2

    def head_q(j):
        qj = jnp.concatenate([qT_ref[(G_D * j + g) * HEAD_DIM:(G_D * j + g + 1) * HEAD_DIM, :]
                              for g in range(G_D)], axis=1)
        return (qj.astype(F32) * (HEAD_DIM ** -0.5)).astype(cdt)

    for jp in range(KV_D // 2):
        q0, q1 = head_q(2 * jp), head_q(2 * jp + 1)
        zq = jnp.zeros_like(q0)
        rhs = jnp.concatenate([jnp.concatenate([q0, zq], axis=0), jnp.concatenate([zq, q1], axis=0)], axis=1)

        def chunk_body(c, carry, jp=jp, rhs=rhs):
            m, l, acc0, acc1 = carry
            kc = k_ref[0, c, :, jp * 2 * HEAD_DIM:(jp + 1) * 2 * HEAD_DIM].astype(cdt)
            s = jnp.dot(kc, rhs, preferred_element_type=F32)
            wt = win_ref[qb - 2 * c]
            madd = mask_sc[c]
            parts = []
            for hh in range(2 * G_D):
                h = 2 * G_D * jp + hh
                tiles = []
                for u in range(2):
                    lo = (1 - u) * half
                    r = jnp.broadcast_to(wt[h:h + 1, lo:lo + 2 * half], (half, 2 * half))
                    tiles.append(pltpu.roll(r, 0, 1, stride=1, stride_axis=0)[:, half:])
                parts.append(s[:, hh * Q_BLOCK:(hh + 1) * Q_BLOCK] + (jnp.concatenate(tiles, axis=0) + madd))
            s = jnp.concatenate(parts, axis=1)
            m_new = jnp.maximum(m, jnp.max(s, axis=0, keepdims=True))
            alpha = jnp.exp(m - m_new)
            p = jnp.exp(s - m_new)
            l = l * alpha + jnp.sum(p, axis=0, keepdims=True)
            p = p.astype(cdt)
            v0 = vT_ref[0, c, (2 * jp) * HEAD_DIM:(2 * jp + 1) * HEAD_DIM, :]
            v1 = vT_ref[0, c, (2 * jp + 1) * HEAD_DIM:(2 * jp + 2) * HEAD_DIM, :]
            acc0 = acc0 * alpha[:, :n_cols] + jnp.dot(v0, p[:, :n_cols], preferred_element_type=F32)
            acc1 = acc1 * alpha[:, n_cols:] + jnp.dot(v1, p[:, n_cols:], preferred_element_type=F32)
            return m_new, l, acc0, acc1

        init = (jnp.full((1, 2 * n_cols), NEG_BIG, F32), jnp.zeros((1, 2 * n_cols), F32),
                jnp.zeros((HEAD_DIM, n_cols), F32), jnp.zeros((HEAD_DIM, n_cols), F32))
        m, l, acc0, acc1 = lax.fori_loop(0, nch, chunk_body, init)
        inv = 1.0 / l
        for jj, acc in enumerate((acc0, acc1)):
            o = acc * inv[:, jj * n_cols:(jj + 1) * n_cols]
            for g in range(G_D):
                r0 = (G_D * (2 * jp + jj) + g) * HEAD_DIM
                z = zT_ref[r0:r0 + HEAD_DIM, :]
                gate = z * (1.0 / (1.0 + jnp.exp(-z)))
                o_ref[r0:r0 + HEAD_DIM, :] = (o[:, g * Q_BLOCK:(g + 1) * Q_BLOCK] * gate).astype(o_ref.dtype)


def dsa_bias_windows(rel_bias, seq_len):
    o = jnp.arange(seq_len // Q_BLOCK)[:, None]
    m = jnp.arange(BIAS_WIN)[None, :]
    d = jnp.maximum(o * Q_BLOCK + m - DSA_KC, 0)
    return jnp.moveaxis(rel_bias[rel_bucket(d)].astype(F32), -1, 1)


def dsa_prompt_attend(qT, qiT, wiT, zT, ki4, k4, v4T, win, *, n_batch, seq_len, cdt=BF16):
    nqb = seq_len // Q_BLOCK
    nc = seq_len // DSA_KC
    topk = min(TOPK_MAX, seq_len // 4)
    idx_bits = int(math.log2(seq_len)) + 1
    tok = lambda b, q: (0, b * nqb + q)
    per_batch = lambda b, q: (b, 0, 0, 0)
    return pl.pallas_call(
        functools.partial(_dsa_prompt_kernel, topk=topk, idx_bits=idx_bits, cdt=cdt),
        grid=(n_batch, nqb),
        in_specs=[pl.BlockSpec((ATT_WIDTH, Q_BLOCK), tok),
                  pl.BlockSpec((IDX_HEADS * IDX_DIM, Q_BLOCK), tok),
                  pl.BlockSpec((IDX_HEADS, Q_BLOCK), tok),
                  pl.BlockSpec((ATT_WIDTH, Q_BLOCK), tok),
                  pl.BlockSpec((1, nc, DSA_KC, IDX_DIM), per_batch),
                  pl.BlockSpec((1, nc, DSA_KC, D_KV), per_batch),
                  pl.BlockSpec((1, nc, D_KV, DSA_KC), per_batch),
                  pl.BlockSpec((nqb, N_HEADS, BIAS_WIN), lambda b, q: (0, 0, 0))],
        out_specs=pl.BlockSpec((ATT_WIDTH, Q_BLOCK), tok),
        out_shape=jax.ShapeDtypeStruct((ATT_WIDTH, n_batch * seq_len), BF16),
        scratch_shapes=[pltpu.VMEM((nc, DSA_KC, Q_BLOCK), jnp.int32),
                        pltpu.VMEM((nc, DSA_KC, Q_BLOCK), F32)],
        compiler_params=pltpu.CompilerParams(
            dimension_semantics=("parallel", "arbitrary"), vmem_limit_bytes=VMEM_LIMIT),
        name="dsa_prompt_attend",
    )(qT, qiT, wiT, zT, ki4, k4, v4T, win)


DSS_NP1 = 32
DSS_NP2 = 16
DSS_TP = 8
DSS_GROWS = G_D * DSS_TP
T5_LAST_BUCKET_DIST = 1600


def _dsa_sample_select_kernel(pt_ref, qi_ref, wb_ref, kinew_ref, *rest, n_pages, n_new, topk, idx_bits, cdt):
    del pt_ref
    page_refs, mask_ref, key_sc = rest[:DSS_NP1], rest[DSS_NP1], rest[DSS_NP1 + 1]
    s = pl.program_id(1)
    lane = lax.broadcasted_iota(jnp.int32, (DSS_TP, PAGE_SIZE), 1)
    trow = lax.broadcasted_iota(jnp.int32, (DSS_TP, PAGE_SIZE), 0)
    qi = qi_ref[0]
    wb = wb_ref[0]

    def page_keys(kp):
        sc = lax.dot_general(qi, kp.astype(cdt), (((1,), (1,)), ((), ())),
                             preferred_element_type=F32) * (IDX_DIM ** -0.5)
        sc = jnp.maximum(sc, 0.0) * wb
        return _sortable_key(sc.reshape(IDX_HEADS, DSS_TP, PAGE_SIZE).sum(axis=0))

    for i in range(DSS_NP1):
        key_sc[s * DSS_NP1 + i] = page_keys(page_refs[i][0, 0])

    @pl.when(s == 0)
    def _():
        kn = page_keys(kinew_ref[0])
        key_sc[n_pages] = jnp.where(lane < n_new, jnp.where(lane <= trow, kn, INT_MIN), INT_MIN)

    @pl.when(s == n_pages // DSS_NP1 - 1)
    def _():
        all_shape = (n_pages + 1, DSS_TP, PAGE_SIZE)

        def count(pred):
            key_idx = (lax.broadcasted_iota(jnp.int32, all_shape, 0) * PAGE_SIZE
                       + lax.broadcasted_iota(jnp.int32, all_shape, 2))
            acc = pred(key_sc[...], key_idx).sum(axis=0)
            return jnp.broadcast_to(jnp.sum(acc, axis=1, keepdims=True), (DSS_TP, PAGE_SIZE))

        c_nonneg = count(lambda k, i: jnp.where(k >= 0, 1, 0))
        thr = jnp.where(c_nonneg >= topk, 0, INT_MIN).astype(jnp.int32)

        def thr_bit(b, thr):
            cand = thr + jnp.left_shift(jnp.int32(1), 30 - b)
            return jnp.where(count(lambda k, i: jnp.where(k >= cand, 1, 0)) >= topk, cand, thr)

        thr = lax.fori_loop(0, 31, thr_bit, thr)
        need = topk - count(lambda k, i: jnp.where(k > thr, 1, 0))

        def lim_bit(b, lim):
            cand = lim + jnp.left_shift(jnp.int32(1), idx_bits - 1 - b)
            c = count(lambda k, i: jnp.where(k == thr, jnp.where(i < cand, 1, 0), 0))
            return jnp.where(c <= need, cand, lim)

        n_ties = count(lambda k, i: jnp.where(k == thr, 1, 0))
        settled = jnp.min(jnp.where(thr == INT_MIN, 1, jnp.where(n_ties == need, 1, 0))) == 1
        lim = lax.cond(settled,
                       lambda: jnp.full((DSS_TP, PAGE_SIZE), 1 << idx_bits, jnp.int32),
                       lambda: lax.fori_loop(0, idx_bits, lim_bit, jnp.zeros((DSS_TP, PAGE_SIZE), jnp.int32)))

        k = key_sc[...]
        key_idx = (lax.broadcasted_iota(jnp.int32, all_shape, 0) * PAGE_SIZE
                   + lax.broadcasted_iota(jnp.int32, all_shape, 2))
        tie = jnp.where(k == thr, jnp.where(key_idx < lim, 0.0, NEG_BIG), NEG_BIG)
        mask_ref[0] = jnp.where(k == INT_MIN, NEG_BIG, jnp.where(k > thr, 0.0, tie))


def _dsa_sample_attend_kernel(pt_ref, q_ref, z_ref, mask_ref, masknew_ref, bnear_ref, bfar_ref, kvnew_ref, *rest,
                              n_pages, n_far, cdt):
    del pt_ref
    page_refs, o_ref = rest[:DSS_NP2], rest[DSS_NP2]
    m_sc, l_sc, acc_sc = rest[DSS_NP2 + 1:]
    s = pl.program_id(1)
    heads = range(KV_D)
    rep = DSS_GROWS // DSS_TP

    def attend(pages, masks, page_ids):
        n = len(pages)
        madd = jnp.concatenate([jnp.concatenate([mk] * rep, axis=0) for mk in masks], axis=1)
        kT = [jnp.concatenate([pg(0, h) for pg in pages], axis=1).astype(cdt) for h in heads]
        vT = [jnp.concatenate([pg(1, h) for pg in pages], axis=1).astype(cdt) for h in heads]
        logits = [jnp.dot(q_ref[0, h], kT[h], preferred_element_type=F32) for h in heads]
        bias = [jnp.concatenate([jnp.where(pid >= n_far, bnear_ref[jnp.maximum(pid - n_far, 0), h], bfar_ref[h])
                                 for pid in page_ids], axis=1) for h in heads]
        logits = [lg + (b + madd) for lg, b in zip(logits, bias)]
        m_old = [m_sc[h] for h in heads]
        m_new = [jnp.maximum(mo, jnp.broadcast_to(jnp.max(lg, axis=1, keepdims=True), mo.shape))
                 for mo, lg in zip(m_old, logits)]
        alpha = [jnp.exp(mo - mn) for mo, mn in zip(m_old, m_new)]
        p = [jnp.exp(lg - jnp.concatenate([mn] * n, axis=1)) for lg, mn in zip(logits, m_new)]
        pv = [lax.dot_general(ph.astype(cdt), vT[h], (((1,), (1,)), ((), ())), preferred_element_type=F32)
              for h, ph in zip(heads, p)]
        for h in heads:
            l_sc[h] = l_sc[h] * alpha[h] + jnp.broadcast_to(jnp.sum(p[h], axis=1, keepdims=True), alpha[h].shape)
            acc_sc[h] = acc_sc[h] * alpha[h][:, :HEAD_DIM] + pv[h]
            m_sc[h] = m_new[h]

    @pl.when(s == 0)
    def _():
        m_sc[...] = jnp.full_like(m_sc, NEG_BIG)
        l_sc[...] = jnp.zeros_like(l_sc)
        acc_sc[...] = jnp.zeros_like(acc_sc)
        attend([lambda c, h: kvnew_ref[0, c, h]], [masknew_ref[0, 0]], [n_pages])

    attend([(lambda c, h, r=r: r[0, 0, c, h]) for r in page_refs], [mask_ref[0, i] for i in range(DSS_NP2)],
           [s * DSS_NP2 + i for i in range(DSS_NP2)])

    @pl.when(s == n_pages // DSS_NP2 - 1)
    def _():
        for h in heads:
            z = z_ref[0, h]
            o_ref[0, h] = acc_sc[h] * (1.0 / l_sc[h][:, :HEAD_DIM]) * (z * (1.0 / (1.0 + jnp.exp(-z))))


def dsa_sample(q_s, z_s, qi_s, wi_s, ki_s, kv_s, kv_pool, kidx_pool, layer, page_table, rel_bias, cdt=BF16):
    n_dec, n_pages = page_table.shape
    dec_len = q_s.shape[0] // n_dec
    past = n_pages * PAGE_SIZE
    total = past + dec_len
    topk = min(TOPK_MAX, total // 4)
    idx_bits = int(math.log2(total)) + 1
    pad_t = DSS_TP - dec_len
    n_pairs = KV_D // 2
    eye2 = jnp.eye(2, dtype=F32)

    qi = jnp.pad(jnp.swapaxes(qi_s.reshape(n_dec, dec_len, IDX_HEADS, IDX_DIM), 1, 2), ((0, 0), (0, 0), (0, pad_t), (0, 0)))
    qi = qi.reshape(n_dec, IDX_HEADS * DSS_TP, IDX_DIM).astype(cdt)
    wb = jnp.pad(jnp.swapaxes(wi_s.reshape(n_dec, dec_len, IDX_HEADS), 1, 2) * (IDX_HEADS ** -0.5), ((0, 0), (0, 0), (0, pad_t)))
    wb = jnp.broadcast_to(wb.reshape(n_dec, IDX_HEADS * DSS_TP, 1), (n_dec, IDX_HEADS * DSS_TP, PAGE_SIZE))
    ki_new = jnp.pad(ki_s.reshape(n_dec, dec_len, IDX_DIM), ((0, 0), (0, PAGE_SIZE - dec_len), (0, 0)))
    kidx4 = kidx_pool.reshape(kidx_pool.shape[0], kidx_pool.shape[1], PAGE_SIZE, IDX_DIM)
    page_spec = lambda np_, i, width: pl.BlockSpec(
        (1, 1, PAGE_SIZE, width), lambda b, s, pt: (layer, pt[b, s * np_ + i], 0, 0))
    per_b3 = lambda b, s, pt: (b, 0, 0)
    mask = pl.pallas_call(
        functools.partial(_dsa_sample_select_kernel, n_pages=n_pages, n_new=dec_len, topk=topk, idx_bits=idx_bits,
                          cdt=cdt),
        grid_spec=pltpu.PrefetchScalarGridSpec(
            num_scalar_prefetch=1, grid=(n_dec, n_pages // DSS_NP1),
            in_specs=[pl.BlockSpec((1, IDX_HEADS * DSS_TP, IDX_DIM), per_b3),
                      pl.BlockSpec((1, IDX_HEADS * DSS_TP, PAGE_SIZE), per_b3),
                      pl.BlockSpec((1, PAGE_SIZE, IDX_DIM), per_b3)]
                     + [page_spec(DSS_NP1, i, IDX_DIM) for i in range(DSS_NP1)],
            out_specs=pl.BlockSpec((1, n_pages + 1, DSS_TP, PAGE_SIZE), lambda b, s, pt: (b, 0, 0, 0)),
            scratch_shapes=[pltpu.VMEM((n_pages + 1, DSS_TP, PAGE_SIZE), jnp.int32)]),
        out_shape=jax.ShapeDtypeStruct((n_dec, n_pages + 1, DSS_TP, PAGE_SIZE), F32),
        compiler_params=pltpu.CompilerParams(
            dimension_semantics=("parallel", "arbitrary"), vmem_limit_bytes=VMEM_LIMIT),
        name="dsa_sample_select",
    )(page_table, qi, wb, ki_new, *([kidx4] * DSS_NP1))

    def head_rows(t, scale):
        t = t.reshape(n_dec, dec_len, KV_D, G_D, HEAD_DIM).transpose(0, 2, 3, 1, 4) * scale
        return jnp.pad(t, ((0, 0),) * 3 + ((0, pad_t), (0, 0))).reshape(n_dec, KV_D, DSS_GROWS, HEAD_DIM)

    q_hr = head_rows(q_s, HEAD_DIM ** -0.5).astype(cdt)
    z_hr = head_rows(z_s, 1.0)
    n_far = max(0, min(n_pages, (past - (PAGE_SIZE - 1) - T5_LAST_BUCKET_DIST) // PAGE_SIZE + 1))
    near_pages = jnp.arange(n_far, n_pages + 1)
    dist = (past + jnp.arange(DSS_TP)[None, :, None]
            - (near_pages[:, None, None] * PAGE_SIZE + jnp.arange(PAGE_SIZE)[None, None, :]))
    b_near = rel_bias[rel_bucket(dist)].astype(F32)
    b_near = b_near.transpose(0, 3, 1, 2).reshape(n_pages + 1 - n_far, KV_D, DSS_GROWS, PAGE_SIZE)
    b_far = jnp.broadcast_to(rel_bias[N_BUCKETS - 1].astype(F32)[:, None, None], (N_HEADS, DSS_TP, PAGE_SIZE))
    b_far = b_far.reshape(KV_D, DSS_GROWS, PAGE_SIZE)
    kv_new = jnp.pad(kv_s.reshape(n_dec, dec_len, 2, KV_D, HEAD_DIM).transpose(0, 2, 3, 4, 1),
                     ((0, 0),) * 4 + ((0, PAGE_SIZE - dec_len),))
    kv_t = kv_pool.reshape(kv_pool.shape[0], kv_pool.shape[1], PAGE_SIZE, 2, KV_D, HEAD_DIM).transpose(0, 1, 3, 4, 5, 2)
    kv_page = lambda i: pl.BlockSpec((1, 1, 2, KV_D, HEAD_DIM, PAGE_SIZE),
                                     lambda b, s, pt: (layer, pt[b, s * DSS_NP2 + i], 0, 0, 0, 0))
    per_b4 = lambda b, s, pt: (b, 0, 0, 0)
    o = pl.pallas_call(
        functools.partial(_dsa_sample_attend_kernel, n_pages=n_pages, n_far=n_far, cdt=cdt),
        grid_spec=pltpu.PrefetchScalarGridSpec(
            num_scalar_prefetch=1, grid=(n_dec, n_pages // DSS_NP2),
            in_specs=[pl.BlockSpec((1, KV_D, DSS_GROWS, HEAD_DIM), per_b4),
                      pl.BlockSpec((1, KV_D, DSS_GROWS, HEAD_DIM), per_b4),
                      pl.BlockSpec((1, DSS_NP2, DSS_TP, PAGE_SIZE), lambda b, s, pt: (b, s, 0, 0)),
                      pl.BlockSpec((1, 1, DSS_TP, PAGE_SIZE), lambda b, s, pt: (b, n_pages, 0, 0)),
                      pl.BlockSpec(b_near.shape, lambda b, s, pt: (0, 0, 0, 0)),
                      pl.BlockSpec(b_far.shape, lambda b, s, pt: (0, 0, 0)),
                      pl.BlockSpec((1, 2, KV_D, HEAD_DIM, PAGE_SIZE), lambda b, s, pt: (b, 0, 0, 0, 0))]
                     + [kv_page(i) for i in range(DSS_NP2)],
            out_specs=pl.BlockSpec((1, KV_D, DSS_GROWS, HEAD_DIM), per_b4),
            scratch_shapes=[pltpu.VMEM((KV_D, DSS_GROWS, LANE), F32), pltpu.VMEM((KV_D, DSS_GROWS, LANE), F32),
                            pltpu.VMEM((KV_D, DSS_GROWS, HEAD_DIM), F32)]),
        out_shape=jax.ShapeDtypeStruct((n_dec, KV_D, DSS_GROWS, HEAD_DIM), F32),
        compiler_params=pltpu.CompilerParams(
            dimension_semantics=("parallel", "arbitrary"), vmem_limit_bytes=VMEM_LIMIT),
        name="dsa_sample_attend",
    )(page_table, q_hr, z_hr, mask, mask, b_near, b_far, kv_new, *([kv_t] * DSS_NP2))
    o = o.reshape(n_dec, KV_D, G_D, DSS_TP, HEAD_DIM)[:, :, :, :dec_len]
    return o.transpose(0, 3, 1, 2, 4).reshape(n_dec * dec_len, ATT_WIDTH)


def _pad_cols(w, n):
    return jnp.pad(w, ((0, 0), (0, n - w.shape[1])))


def kernel(x_prompt, x_sample, cache_a_kv, state_s5, state_gdn, state_gdn_conv, cache_d_kv, cache_d_kidx,
           page_table, p_prompt, p_sample, rel_bias, ln_g, ln_b, ple_gate_w, ple_w,
           a_w_in, a_sinks, a_w_out,
           s5_w_in, s5_a_re, s5_a_im, s5_b_re, s5_b_im, s5_c_re, s5_c_im, s5_d, s5_log_dt, s5_w_glu, s5_w_out,
           gdn_w_in, gdn_conv_w, gdn_a_log, gdn_dt_bias, gdn_norm_w, gdn_w_out,
           dsa_w_in, dsa_w_out):
    x = join_tokens(x_prompt, x_sample)
    x_bf = x.astype(BF16)
    outs = {}
    (a_w_in, a_w_out, s5_w_in, s5_w_glu, s5_w_out, gdn_w_in, gdn_w_out, dsa_w_in, dsa_w_out, ple_gate_w, ple_w) = (
        w.astype(BF16) for w in (a_w_in, a_w_out, s5_w_in, s5_w_glu, s5_w_out, gdn_w_in, gdn_w_out, dsa_w_in,
                                 dsa_w_out, ple_gate_w, ple_w))

    def post(i, x, h):
        return post_norm_ple(x, h, join_tokens(p_prompt[i], p_sample[i]), ln_g[i], ln_b[i], ple_gate_w[i], ple_w[i])

    w_in = a_w_in[0]
    c_k, c_v, c_z = ATT_WIDTH, ATT_WIDTH + A_KV, ATT_WIDTH + 2 * A_KV
    kv_nat = matmul(x_bf, w_in, c_k, 2 * A_KV)
    xT_bf = x_bf[:N_PROMPT_TOK].T
    x_s = x_bf[N_PROMPT_TOK:]
    qT = matmul_wt(w_in, xT_bf, 0, ATT_WIDTH, out_dtype=BF16)
    vT = matmul_wt(w_in, xT_bf, c_v, A_KV, out_dtype=BF16)
    zT = matmul_wt(w_in, xT_bf, c_z, ATT_WIDTH)
    bias_p, sink_p, bias_s, sink_s = swa_tables(rel_bias, a_sinks[0], DEC_SEQ)
    h_p = matmul_ta(swa_prompt(qT, vT, zT, kv_nat, bias_p, sink_p, n_batch=BATCH, seq_len=SEQ), a_w_out[0])
    gs, outs['a_s'] = swa_sample(matmul(x_s, w_in, 0, ATT_WIDTH), matmul(x_s, w_in, c_z, ATT_WIDTH),
                                 kv_nat[N_PROMPT_TOK:], cache_a_kv[0], bias_s, sink_s)
    outs['a_p'] = kv_nat[:N_PROMPT_TOK].reshape(BATCH, SEQ, 2, KV_A, HEAD_DIM)[:, SEQ - WINDOW:]
    x, x_bf = post(0, x, jnp.concatenate([h_p, matmul(gs.astype(BF16), a_w_out[0])], axis=0))

    proj = matmul(x_bf, s5_w_in[0])
    tables = s5_tables(s5_a_re[0], s5_a_im[0], s5_b_re[0], s5_b_im[0], s5_c_re[0], s5_c_im[0], s5_log_dt[0])
    gp, gs, outs['s5_p'], outs['s5_s'] = s5_layer(proj, state_s5[0], tables, s5_d[0], s5_w_glu[0],
                                                  n_batch=BATCH, seq_len=SEQ, n_dec=DEC_BATCH, dec_len=DEC_SEQ)
    x, x_bf = post(1, x, jnp.concatenate([matmul(gp, s5_w_out[0]), matmul(gs, s5_w_out[0])], axis=0))

    w_in = gdn_w_in[0]
    c_gz = GDN_CONV_CH + GDN_V_WIDTH
    qkvz = matmul(x_bf, w_in, 0, c_gz)
    ab = matmul(x_bf, _pad_cols(w_in[:, c_gz:], LANE))
    gp, gs, outs['gd_p'], outs['gd_s'], outs['gc_p'], outs['gc_s'] = gdn_layer(
        qkvz, ab, state_gdn[0], state_gdn_conv[0], gdn_conv_w[0], gdn_a_log[0], gdn_dt_bias[0], gdn_norm_w[0],
        n_batch=BATCH, seq_len=SEQ, n_dec=DEC_BATCH, dec_len=DEC_SEQ)
    x, x_bf = post(2, x, jnp.concatenate([matmul(gp, gdn_w_out[0]), matmul(gs.astype(BF16), gdn_w_out[0])], axis=0))

    w_in = dsa_w_in[0]
    c_kv = ATT_WIDTH + 2 * D_KV
    c_z = 2 * ATT_WIDTH + 2 * D_KV
    c_qi = c_z + IDX_HEADS * IDX_DIM
    kv_nat = matmul(x_bf, w_in, ATT_WIDTH, 2 * D_KV)
    kiw = matmul(x_bf, _pad_cols(w_in[:, c_qi:], 2 * LANE))
    xT_bf = x_bf[:N_PROMPT_TOK].T
    x_s = x_bf[N_PROMPT_TOK:]
    qT = matmul_wt(w_in, xT_bf, 0, ATT_WIDTH, out_dtype=BF16)
    qiT = matmul_wt(w_in, xT_bf, c_z, IDX_HEADS * IDX_DIM, out_dtype=BF16)
    zT = matmul_wt(w_in, xT_bf, c_kv, ATT_WIDTH)
    wiT = matmul_wt(_pad_cols(w_in[:, c_qi + IDX_DIM:], LANE), xT_bf, 0, LANE)
    nc = SEQ // DSA_KC
    kv_p = kv_nat[:N_PROMPT_TOK]
    v4T = jnp.swapaxes(kv_p[:, D_KV:].astype(BF16).reshape(BATCH, nc, DSA_KC, D_KV), 2, 3)
    gT = dsa_prompt_attend(qT, qiT, wiT, zT, kiw[:N_PROMPT_TOK].reshape(BATCH, nc, DSA_KC, 2 * LANE),
                           kv_p.reshape(BATCH, nc, DSA_KC, 2 * D_KV), v4T, dsa_bias_windows(rel_bias, SEQ),
                           n_batch=BATCH, seq_len=SEQ)
    h_p = matmul_ta(gT, dsa_w_out[0])
    kiw_s = kiw[N_PROMPT_TOK:]
    gs = dsa_sample(matmul(x_s, w_in, 0, ATT_WIDTH), matmul(x_s, w_in, c_kv, ATT_WIDTH),
                    matmul(x_s, w_in, c_z, IDX_HEADS * IDX_DIM),
                    kiw_s[:, IDX_DIM:IDX_DIM + IDX_HEADS], kiw_s[:, :IDX_DIM], kv_nat[N_PROMPT_TOK:],
                    cache_d_kv, cache_d_kidx, 0, page_table, rel_bias)
    h_s = matmul(gs.astype(BF16), dsa_w_out[0])
    outs['dkv_p'] = kv_p.reshape(BATCH, SEQ, 2, KV_D, HEAD_DIM)
    outs['dkv_s'] = kv_nat[N_PROMPT_TOK:].reshape(DEC_BATCH, DEC_SEQ, 2, KV_D, HEAD_DIM)
    outs['dki_p'] = kiw[:N_PROMPT_TOK, :IDX_DIM].reshape(BATCH, SEQ, IDX_DIM)
    outs['dki_s'] = kiw_s[:, :IDX_DIM].reshape(DEC_BATCH, DEC_SEQ, IDX_DIM)
    x, x_bf = post(3, x, jnp.concatenate([h_p, h_s], axis=0))

    yp, ys = split_tokens(x)
    st = lambda name: outs[name][None]
    return (yp, ys, st('a_p'), st('a_s'), st('s5_p'), st('s5_s'), st('gd_p'), st('gd_s'),
            st('gc_p'), st('gc_s'), st('dkv_p'), st('dkv_s'), st('dki_p'), st('dki_s'))
```
